```python
import math
import jax, jax.numpy as jnp
from jax import lax
import numpy as np


D_MODEL = 1024
BATCH = 8
SEQ = 2048
DEPTH = 2

RMS_EPS = 1e-6
HEAD_DIM = 64
A_Q_HEADS = 8
A_KV_HEADS = 2
A_GROUP = A_Q_HEADS // A_KV_HEADS
A_WINDOW = 128
B_HEADS = 8
ATTN_BLOCK = 128
EVEN_IN = (A_Q_HEADS + 2 * A_KV_HEADS + 3 * B_HEADS) * HEAD_DIM + B_HEADS
EVEN_MIX = (A_Q_HEADS + B_HEADS) * HEAD_DIM
C_HEADS = 16
C_HEAD_DIM = 64
C_INNER = C_HEADS * C_HEAD_DIM
C_GROUPS = 2
C_HPG = C_HEADS // C_GROUPS
C_STATE = 128
C_CONV = 4
C_CHUNK = 128
C_CONV_DIM = C_INNER + 2 * C_GROUPS * C_STATE
D_HEADS = 4
D_KEY = D_MODEL // 2
D_VAL = D_MODEL
D_HK = D_KEY // D_HEADS
D_HV = D_VAL // D_HEADS
D_GATE_RANK = 16
D_GATE_NORM = 16.0
D_CHUNK = 64
ODD_IN = C_INNER + C_CONV_DIM + C_HEADS + 2 * D_KEY + D_VAL + D_GATE_RANK + D_VAL
ODD_MIX = C_INNER + D_VAL
MOE_GROUPS = 4
MOE_EPG = 8
MOE_EXPERTS = MOE_GROUPS * MOE_EPG
MOE_TOPK = 2
MOE_FF = 512
MOE_BLOCK = 128
N_EVEN = (DEPTH + 1) // 2
N_ODD = DEPTH // 2

kernel_name = 'hybrid_swa_fox_ssd_gla_hmoe'


def rmsnorm(x, g):
    xf = x.astype(jnp.float32)
    y = xf * lax.rsqrt(jnp.mean(xf * xf, axis=-1, keepdims=True) + RMS_EPS)
    return (y * g.astype(jnp.float32)).astype(x.dtype)


def split_cols(p, widths):
    offs = [int(o) for o in np.cumsum(widths)[:-1]]
    return jnp.split(p, offs, axis=-1)


def alibi_slopes(n_heads):
    return jnp.asarray(2.0 ** (-8.0 * np.arange(1, n_heads + 1) / n_heads), dtype=jnp.float32)


def sliding_window_sink_attention(q, k, v, sinks):
    b, s, kvh, g, hd = q.shape
    blk = A_WINDOW
    nb = s // blk

    def windows(t):
        tp = jnp.pad(t, ((0, 0), (blk, 0), (0, 0), (0, 0))).reshape(b, nb + 1, blk, kvh, hd)
        return jnp.concatenate([tp[:, :-1], tp[:, 1:]], axis=2)

    kw, vw = windows(k), windows(v)
    qb = q.reshape(b, nb, blk, kvh, g, hd)
    scores = jnp.einsum('bnqkgd,bnskd->bnkgqs', qb, kw).astype(jnp.float32) * (hd ** -0.5)
    dist = blk + jnp.arange(blk)[:, None] - jnp.arange(2 * blk)[None, :]
    key_pos = jnp.arange(nb)[:, None, None] * blk + jnp.arange(2 * blk)[None, None, :] - blk
    valid = (dist >= 0) & (dist < A_WINDOW) & (key_pos >= 0)
    slopes = alibi_slopes(kvh * g).reshape(kvh, g, 1, 1)
    scores = scores - slopes * dist.astype(jnp.float32)
    scores = jnp.where(valid[None, :, None, None], scores, -jnp.inf)
    sink = sinks.astype(jnp.float32).reshape(1, 1, kvh, g, 1, 1)
    m = jnp.maximum(jnp.max(scores, axis=-1, keepdims=True), sink)
    p = jnp.exp(scores - m)
    p = p / (jnp.sum(p, axis=-1, keepdims=True) + jnp.exp(sink - m))
    out = jnp.einsum('bnkgqs,bnskd->bnqkgd', p.astype(v.dtype), vw)
    return out.reshape(b, s, kvh * g, hd)


def forgetting_attention(q, k, v, log_f):
    b, s, h, hd = q.shape
    nb = s // ATTN_BLOCK
    c = jnp.transpose(jnp.cumsum(log_f, axis=1), (0, 2, 1))
    kpos = jnp.arange(s)

    def block(n):
        start = n * ATTN_BLOCK
        qb = lax.dynamic_slice_in_dim(q, start, ATTN_BLOCK, axis=1)
        cb = lax.dynamic_slice_in_dim(c, start, ATTN_BLOCK, axis=2)
        sc = jnp.einsum('bqhd,bshd->bhqs', qb, k).astype(jnp.float32) * (hd ** -0.5)
        sc = sc + cb[..., None] - c[:, :, None, :]
        qpos = start + jnp.arange(ATTN_BLOCK)
        sc = jnp.where(kpos[None, :] <= qpos[:, None], sc, -jnp.inf)
        p = jax.nn.softmax(sc, axis=-1).astype(v.dtype)
        return jnp.einsum('bhqs,bshd->bqhd', p, v)

    out = lax.map(block, jnp.arange(nb))
    return jnp.transpose(out, (1, 0, 2, 3, 4)).reshape(b, s, h, hd)


def causal_depthwise_conv(x, w, bias):
    kw, ch = w.shape
    y = lax.conv_general_dilated(x, w[:, None, :].astype(x.dtype), window_strides=(1,), padding=[(kw - 1, 0)],
                                 dimension_numbers=('NWC', 'WIO', 'NWC'), feature_group_count=ch)
    return y + bias.astype(y.dtype)


def ssd_chunked(x, dt, a, bm, cm):
    b, s, g, hg, p = x.shape
    n = bm.shape[-1]
    q = C_CHUNK
    nc = s // q
    xd = (x.astype(jnp.float32) * dt[..., None]).reshape(b, nc, q, g, hg, p)
    acs = jnp.cumsum(jnp.transpose((dt * a).reshape(b, nc, q, g, hg), (0, 3, 4, 1, 2)), axis=-1)
    bc = bm.astype(jnp.float32).reshape(b, nc, q, g, n)
    cc = cm.astype(jnp.float32).reshape(b, nc, q, g, n)
    tri = jnp.tril(jnp.ones((q, q), dtype=bool))
    seg = jnp.exp(jnp.where(tri, acs[..., :, None] - acs[..., None, :], -jnp.inf))
    cb = jnp.einsum('bclgn,bcsgn->bgcls', cc, bc)
    y_diag = jnp.einsum('bgcls,bghcls,bcsghp->bclghp', cb, seg, xd)
    decay_to_end = jnp.exp(acs[..., -1:] - acs)
    states = jnp.einsum('bcsgn,bghcs,bcsghp->bcghpn', bc, decay_to_end, xd)
    chunk_decay = jnp.exp(acs[..., -1])

    def step(hstate, inp):
        st, dec = inp
        return hstate * dec[..., None, None] + st, hstate

    h0 = jnp.zeros((b, g, hg, p, n), jnp.float32)
    _, prev = lax.scan(step, h0, (jnp.moveaxis(states, 1, 0), jnp.moveaxis(chunk_decay, 3, 0)))
    y_off = jnp.einsum('bclgn,cbghpn,bghcl->bclghp', cc, prev, jnp.exp(acs))
    return (y_diag + y_off).reshape(b, s, g, hg, p)


def gla_chunked(q, k, v, log_alpha):
    b, s, h, dk = q.shape
    dv = v.shape[-1]
    cq = D_CHUNK
    nc = s // cq
    qc = (q.astype(jnp.float32) * (dk ** -0.5)).reshape(b, nc, cq, h, dk)
    kc = k.astype(jnp.float32).reshape(b, nc, cq, h, dk)
    vc = v.astype(jnp.float32).reshape(b, nc, cq, h, dv)
    gcs = jnp.cumsum(log_alpha.reshape(b, nc, cq, h, dk), axis=2)
    q_dec = qc * jnp.exp(gcs)
    k_inv = kc * jnp.exp(-gcs)
    k_end = kc * jnp.exp(gcs[:, :, -1:] - gcs)
    tri = jnp.tril(jnp.ones((cq, cq), dtype=bool))
    attn = jnp.where(tri, jnp.einsum('bclhd,bcshd->bchls', q_dec, k_inv), 0.0)
    o_intra = jnp.einsum('bchls,bcshv->bclhv', attn, vc)
    states = jnp.einsum('bcshd,bcshv->bchdv', k_end, vc)
    chunk_decay = jnp.exp(gcs[:, :, -1])

    def step(hstate, inp):
        st, dec = inp
        return hstate * dec[..., None] + st, hstate

    h0 = jnp.zeros((b, h, dk, dv), jnp.float32)
    _, prev = lax.scan(step, h0, (jnp.moveaxis(states, 1, 0), jnp.moveaxis(chunk_decay, 1, 0)))
    o_inter = jnp.einsum('bclhd,cbhdv->bclhv', q_dec, prev)
    return (o_intra + o_inter).reshape(b, s, h, dv)


def even_mixer(h, w_in, sinks, forget_bias, w_out):
    b, s, _ = h.shape
    proj = h @ w_in
    q_a, k_a, v_a, q_b, k_b, v_b, f_b = split_cols(
        proj, (A_Q_HEADS * HEAD_DIM, A_KV_HEADS * HEAD_DIM, A_KV_HEADS * HEAD_DIM,
               B_HEADS * HEAD_DIM, B_HEADS * HEAD_DIM, B_HEADS * HEAD_DIM, B_HEADS))
    out_a = sliding_window_sink_attention(q_a.reshape(b, s, A_KV_HEADS, A_GROUP, HEAD_DIM),
                                          k_a.reshape(b, s, A_KV_HEADS, HEAD_DIM),
                                          v_a.reshape(b, s, A_KV_HEADS, HEAD_DIM), sinks)
    log_f = jax.nn.log_sigmoid(f_b.astype(jnp.float32) + forget_bias.astype(jnp.float32))
    out_b = forgetting_attention(q_b.reshape(b, s, B_HEADS, HEAD_DIM), k_b.reshape(b, s, B_HEADS, HEAD_DIM),
                                 v_b.reshape(b, s, B_HEADS, HEAD_DIM), log_f)
    heads = jnp.concatenate([out_a.reshape(b, s, -1), out_b.reshape(b, s, -1)], axis=-1).astype(h.dtype)
    return heads @ w_out


def odd_mixer(h, w_in, conv_w, conv_b, dt_bias, a_log, d_skip, ssd_norm, gk_w, gk_b, gla_norm, w_out):
    b, s, _ = h.shape
    proj = h @ w_in
    z, xbc, dt, q, k, v, g_lr, r = split_cols(
        proj, (C_INNER, C_CONV_DIM, C_HEADS, D_KEY, D_KEY, D_VAL, D_GATE_RANK, D_VAL))
    xbc = jax.nn.silu(causal_depthwise_conv(xbc, conv_w, conv_b))
    xs, bm, cm = split_cols(xbc, (C_INNER, C_GROUPS * C_STATE, C_GROUPS * C_STATE))
    dt = jax.nn.softplus(dt.astype(jnp.float32) + dt_bias.astype(jnp.float32)).reshape(b, s, C_GROUPS, C_HPG)
    a = -jnp.exp(a_log.astype(jnp.float32)).reshape(C_GROUPS, C_HPG)
    xs = xs.reshape(b, s, C_GROUPS, C_HPG, C_HEAD_DIM)
    y = ssd_chunked(xs, dt, a, bm.reshape(b, s, C_GROUPS, C_STATE), cm.reshape(b, s, C_GROUPS, C_STATE))
    y = y + d_skip.astype(jnp.float32).reshape(C_GROUPS, C_HPG, 1) * xs.astype(jnp.float32)
    y = rmsnorm(y.reshape(b, s, C_INNER) * jax.nn.silu(z.astype(jnp.float32)), ssd_norm)
    log_alpha = jax.nn.log_sigmoid((g_lr @ gk_w + gk_b).astype(jnp.float32)) / D_GATE_NORM
    o = gla_chunked(q.reshape(b, s, D_HEADS, D_HK), k.reshape(b, s, D_HEADS, D_HK),
                    v.reshape(b, s, D_HEADS, D_HV), log_alpha.reshape(b, s, D_HEADS, D_HK))
    o = rmsnorm(o, gla_norm).reshape(b, s, D_VAL) * jax.nn.silu(r.astype(jnp.float32))
    mixed = jnp.concatenate([y, o], axis=-1).astype(h.dtype)
    return mixed @ w_out


def hierarchical_moe(h, w_group, b_group, w_router, b_router, w_gate, w_up, w_down):
    b, s, d = h.shape
    n_tok = b * s
    xt = h.reshape(n_tok, d)
    g_prob = jax.nn.softmax((xt @ w_group).astype(jnp.float32) + b_group.astype(jnp.float32), axis=-1)
    g_w, g_idx = lax.top_k(g_prob, 1)
    e_logits = ((xt @ w_router).astype(jnp.float32) + b_router.astype(jnp.float32)).reshape(n_tok, MOE_GROUPS, MOE_EPG)
    e_logits = jnp.take_along_axis(e_logits, g_idx[:, :, None], axis=1)[:, 0]
    e_w, e_idx = lax.top_k(jax.nn.softmax(e_logits, axis=-1), MOE_TOPK)
    e_w = e_w / jnp.sum(e_w, axis=-1, keepdims=True)
    gate = (g_w * e_w).reshape(-1)
    expert = (g_idx * MOE_EPG + e_idx).reshape(-1).astype(jnp.int32)
    token = jnp.repeat(jnp.arange(n_tok, dtype=jnp.int32), MOE_TOPK)
    n_assign = n_tok * MOE_TOPK
    order = jnp.argsort(expert)
    expert_s, token_s, gate_s = expert[order], token[order], gate[order]
    counts = jnp.zeros((MOE_EXPERTS,), jnp.int32).at[expert].add(1)
    starts = jnp.cumsum(counts) - counts
    padded = (counts + MOE_BLOCK - 1) // MOE_BLOCK * MOE_BLOCK
    pad_ends = jnp.cumsum(padded)
    pad_starts = pad_ends - padded
    row = pad_starts[expert_s] + (jnp.arange(n_assign, dtype=jnp.int32) - starts[expert_s])
    n_blocks = (n_assign + MOE_EXPERTS * (MOE_BLOCK - 1) + MOE_BLOCK - 1) // MOE_BLOCK
    n_rows = n_blocks * MOE_BLOCK
    row_token = jnp.full((n_rows,), n_tok, jnp.int32).at[row].set(token_s)
    row_gate = jnp.zeros((n_rows,), jnp.float32).at[row].set(gate_s)
    block_start = jnp.arange(n_blocks, dtype=jnp.int32) * MOE_BLOCK
    block_expert = jnp.minimum(jnp.sum(block_start[:, None] >= pad_ends[None, :], axis=1),
                               MOE_EXPERTS - 1).astype(jnp.int32)
    x_rows = jnp.concatenate([xt, jnp.zeros((1, d), xt.dtype)], axis=0)[row_token].reshape(n_blocks, MOE_BLOCK, d)

    def expert_block(args):
        xb, e = args
        return (jax.nn.silu(xb @ w_gate[e]) * (xb @ w_up[e])) @ w_down[e]

    y_rows = lax.map(expert_block, (x_rows, block_expert)).reshape(n_rows, d)
    y = jax.ops.segment_sum(y_rows.astype(jnp.float32) * row_gate[:, None], row_token, num_segments=n_tok + 1)[:n_tok]
    return y.reshape(b, s, d).astype(h.dtype)


def setup_inputs(seed: int = 0) -> dict:
    key = jax.random.key(seed)
    ks = jax.random.split(key, 26)
    f32 = jnp.float32

    def nrm(k, shape, scale):
        return jax.random.normal(k, shape, f32) * scale

    dt0 = jnp.exp(jax.random.uniform(ks[11], (N_ODD, C_HEADS), f32) * (math.log(0.1) - math.log(0.001)) + math.log(0.001))
    return {
        'x': nrm(ks[0], (BATCH, SEQ, D_MODEL), 1.0),
        'norm_mix': 1.0 + nrm(ks[1], (DEPTH, D_MODEL), 0.02),
        'norm_moe': 1.0 + nrm(ks[2], (DEPTH, D_MODEL), 0.02),
        'norm_final': 1.0 + nrm(ks[3], (D_MODEL,), 0.02),
        'even_w_in': nrm(ks[4], (N_EVEN, D_MODEL, EVEN_IN), D_MODEL ** -0.5),
        'even_sinks': nrm(ks[5], (N_EVEN, A_Q_HEADS), 0.5),
        'even_forget_bias': 4.0 + nrm(ks[6], (N_EVEN, B_HEADS), 0.5),
        'even_w_out': nrm(ks[7], (N_EVEN, EVEN_MIX, D_MODEL), EVEN_MIX ** -0.5),
        'odd_w_in': nrm(ks[8], (N_ODD, D_MODEL, ODD_IN), D_MODEL ** -0.5),
        'odd_conv_w': nrm(ks[9], (N_ODD, C_CONV, C_CONV_DIM), C_CONV ** -0.5),
        'odd_conv_b': nrm(ks[10], (N_ODD, C_CONV_DIM), 0.02),
        'odd_dt_bias': dt0 + jnp.log(-jnp.expm1(-dt0)),
        'odd_a_log': jnp.log(jax.random.uniform(ks[12], (N_ODD, C_HEADS), f32, 1.0, 16.0)),
        'odd_d_skip': 1.0 + nrm(ks[13], (N_ODD, C_HEADS), 0.1),
        'odd_ssd_norm': 1.0 + nrm(ks[14], (N_ODD, C_INNER), 0.02),
        'odd_gk_w': nrm(ks[15], (N_ODD, D_GATE_RANK, D_KEY), D_GATE_RANK ** -0.5),
        'odd_gk_b': nrm(ks[16], (N_ODD, D_KEY), 0.02),
        'odd_gla_norm': 1.0 + nrm(ks[17], (N_ODD, D_HV), 0.02),
        'odd_w_out': nrm(ks[18], (N_ODD, ODD_MIX, D_MODEL), ODD_MIX ** -0.5),
        'moe_w_group': nrm(ks[19], (DEPTH, D_MODEL, MOE_GROUPS), D_MODEL ** -0.5),
        'moe_b_group': nrm(ks[20], (DEPTH, MOE_GROUPS), 0.01),
        'moe_w_router': nrm(ks[21], (DEPTH, D_MODEL, MOE_EXPERTS), D_MODEL ** -0.5),
        'moe_b_router': nrm(ks[22], (DEPTH, MOE_EXPERTS), 0.01),
        'moe_w_gate': nrm(ks[23], (DEPTH, MOE_EXPERTS, D_MODEL, MOE_FF), D_MODEL ** -0.5),
        'moe_w_up': nrm(ks[24], (DEPTH, MOE_EXPERTS, D_MODEL, MOE_FF), D_MODEL ** -0.5),
        'moe_w_down': nrm(ks[25], (DEPTH, MOE_EXPERTS, MOE_FF, D_MODEL), MOE_FF ** -0.5),
    }


def reference(x, norm_mix, norm_moe, norm_final, even_w_in, even_sinks, even_forget_bias, even_w_out,
              odd_w_in, odd_conv_w, odd_conv_b, odd_dt_bias, odd_a_log, odd_d_skip, odd_ssd_norm,
              odd_gk_w, odd_gk_b, odd_gla_norm, odd_w_out, moe_w_group, moe_b_group, moe_w_router,
              moe_b_router, moe_w_gate, moe_w_up, moe_w_down):
    h = x
    for layer in range(DEPTH):
        i = layer // 2
        hn = rmsnorm(h, norm_mix[layer])
        if layer % 2 == 0:
            mix = even_mixer(hn, even_w_in[i], even_sinks[i], even_forget_bias[i], even_w_out[i])
        else:
            mix = odd_mixer(hn, odd_w_in[i], odd_conv_w[i], odd_conv_b[i], odd_dt_bias[i], odd_a_log[i],
                            odd_d_skip[i], odd_ssd_norm[i], odd_gk_w[i], odd_gk_b[i], odd_gla_norm[i], odd_w_out[i])
        h = h + mix.astype(h.dtype)
        h = h + hierarchical_moe(rmsnorm(h, norm_moe[layer]), moe_w_group[layer], moe_b_group[layer],
                                 moe_w_router[layer], moe_b_router[layer], moe_w_gate[layer],
                                 moe_w_up[layer], moe_w_down[layer])
    return rmsnorm(h, norm_final)
```

```python
import functools
import math

import numpy as np
import jax
import jax.numpy as jnp
from jax import lax
from jax.experimental import pallas as pl
from jax.experimental.pallas import tpu as pltpu

F32 = jnp.float32
BF16 = jnp.bfloat16
HIGHEST = lax.Precision.HIGHEST

RMS_EPS = 1e-6
HEAD_DIM = 64
A_Q_HEADS = 8
A_KV_HEADS = 2
A_GROUP = A_Q_HEADS // A_KV_HEADS
A_WINDOW = 128
B_HEADS = 8
C_HEADS = 16
C_HEAD_DIM = 64
C_INNER = C_HEADS * C_HEAD_DIM
C_GROUPS = 2
C_HPG = C_HEADS // C_GROUPS
C_STATE = 128
C_CONV = 4
C_CHUNK = 128
C_CONV_DIM = C_INNER + 2 * C_GROUPS * C_STATE
D_HEADS = 4
D_HK = 128
D_HV = 256
D_KEY = D_HEADS * D_HK
D_VAL = D_HEADS * D_HV
D_GATE_RANK = 16
D_GATE_NORM = 16.0
D_CHUNK = 64
MOE_GROUPS = 4
MOE_EPG = 8
MOE_EXPERTS = MOE_GROUPS * MOE_EPG
MOE_TOPK = 2

LANES = 128
VMEM_LIMIT = 48 * 1024 * 1024
MOE_TILE = 512
COPY_WINDOW = 64


def _cparams(sem):
    return pltpu.CompilerParams(dimension_semantics=sem, vmem_limit_bytes=VMEM_LIMIT)


def _rms(x, g):
    ms = jnp.mean(x * x, axis=-1, keepdims=True)
    return x * lax.rsqrt(ms + RMS_EPS) * g


def _norm_proj_kernel(combine, has_aux, *refs):
    it = iter(refs)
    x_ref = next(it)
    if combine:
        m0_ref, m1_ref, gt_ref = next(it), next(it), next(it)
    g_ref, w_ref = next(it), next(it)
    wa_ref = next(it) if has_aux else None
    o_ref = next(it)
    oa_ref = next(it) if has_aux else None
    h_ref = next(it) if combine else None
    xn_ref = next(it)

    @pl.when(pl.program_id(1) == 0)
    def _():
        x = x_ref[...]
        if combine:
            gt = gt_ref[...]
            x = x + m0_ref[...] * gt[:, 0:1] + m1_ref[...] * gt[:, 1:2]
            h_ref[...] = x
        xn = _rms(x, g_ref[...]).astype(BF16)
        xn_ref[...] = xn
        if has_aux:
            oa_ref[...] = jnp.dot(xn, wa_ref[...], preferred_element_type=F32)

    o_ref[...] = jnp.dot(xn_ref[...], w_ref[...], preferred_element_type=F32).astype(o_ref.dtype)


def _norm_proj(x, g, w, *, out_dtype, tm, tn, w_aux=None, moe=None):
    t, d = x.shape
    n = w.shape[1]
    tm = min(tm, t)
    combine = moe is not None
    has_aux = w_aux is not None
    row = lambda i, j: (i, 0)
    in_specs = [pl.BlockSpec((tm, d), row)]
    args = [x]
    if combine:
        m, gates = moe
        in_specs += [pl.BlockSpec((None, tm, d), lambda i, j: (0, i, 0)),
                     pl.BlockSpec((None, tm, d), lambda i, j: (1, i, 0)),
                     pl.BlockSpec((tm, LANES), row)]
        args += [m, m, gates]
    in_specs += [pl.BlockSpec((1, d), lambda i, j: (0, 0)), pl.BlockSpec((d, tn), lambda i, j: (0, j))]
    args += [g.reshape(1, d), w]
    if has_aux:
        in_specs.append(pl.BlockSpec((d, LANES), lambda i, j: (0, 0)))
        args.append(w_aux)
    out_shape = [jax.ShapeDtypeStruct((t, n), out_dtype)]
    out_specs = [pl.BlockSpec((tm, tn), lambda i, j: (i, j))]
    if has_aux:
        out_shape.append(jax.ShapeDtypeStruct((t, LANES), F32))
        out_specs.append(pl.BlockSpec((tm, LANES), row))
    if combine:
        out_shape.append(jax.ShapeDtypeStruct((t, d), F32))
        out_specs.append(pl.BlockSpec((tm, d), row))
    outs = pl.pallas_call(
        functools.partial(_norm_proj_kernel, combine, has_aux),
        grid=(t // tm, n // tn),
        in_specs=in_specs, out_specs=out_specs, out_shape=out_shape,
        scratch_shapes=[pltpu.VMEM((tm, d), BF16)],
        compiler_params=_cparams(("parallel", "arbitrary")),
        name="norm_proj",
    )(*args)
    outs = list(outs)
    out = outs.pop(0)
    aux = outs.pop(0) if has_aux else None
    h = outs.pop(0) if combine else None
    return out, aux, h


def _swa_kernel(sink_ref, q_ref, kp_ref, kc_ref, vp_ref, vc_ref, o_ref):
    n = pl.program_id(1)
    blk = A_WINDOW
    row = lax.broadcasted_iota(jnp.int32, (blk, 2 * blk), 0)
    col = lax.broadcasted_iota(jnp.int32, (blk, 2 * blk), 1)
    dist = blk + row - col
    valid = (dist >= 0) & (dist < A_WINDOW) & ((col >= blk) | (n > 0))
    distf = dist.astype(F32)
    outs = []
    for kh in range(A_KV_HEADS):
        ks = slice(kh * HEAD_DIM, (kh + 1) * HEAD_DIM)
        k = jnp.concatenate([kp_ref[0, :, ks], kc_ref[0, :, ks]], axis=0)
        v = jnp.concatenate([vp_ref[0, :, ks], vc_ref[0, :, ks]], axis=0)
        for gi in range(A_GROUP):
            h = kh * A_GROUP + gi
            slope = float(2.0 ** (-8.0 * (h + 1) / A_Q_HEADS))
            q = q_ref[0, :, h * HEAD_DIM:(h + 1) * HEAD_DIM]
            s = lax.dot_general(q, k, (((1,), (1,)), ((), ())), preferred_element_type=F32)
            s = s * (HEAD_DIM ** -0.5) - slope * distf
            s = jnp.where(valid, s, -jnp.inf)
            sink = sink_ref[h]
            m = jnp.maximum(jnp.max(s, axis=-1, keepdims=True), sink)
            p = jnp.exp(s - m)
            denom = jnp.sum(p, axis=-1, keepdims=True) + jnp.exp(sink - m)
            o = jnp.dot(p.astype(BF16), v, preferred_element_type=F32)
            outs.append(o / denom)
    o_ref[0] = jnp.concatenate(outs, axis=-1).astype(o_ref.dtype)


def _swa(proj, sinks):
    b, s, _ = proj.shape
    blk = A_WINDOW
    qw = A_Q_HEADS * HEAD_DIM
    kw = A_KV_HEADS * HEAD_DIM
    q_blk = (3 * B_HEADS * HEAD_DIM) // qw
    k_blk = (3 * B_HEADS * HEAD_DIM + qw) // kw
    v_blk = k_blk + 1
    prev = lambda i, n: jnp.maximum(n - 1, 0)
    return pl.pallas_call(
        _swa_kernel,
        grid=(b, s // blk),
        in_specs=[
            pl.BlockSpec(memory_space=pltpu.SMEM),
            pl.BlockSpec((1, blk, qw), lambda i, n: (i, n, q_blk)),
            pl.BlockSpec((1, blk, kw), lambda i, n: (i, prev(i, n), k_blk)),
            pl.BlockSpec((1, blk, kw), lambda i, n: (i, n, k_blk)),
            pl.BlockSpec((1, blk, kw), lambda i, n: (i, prev(i, n), v_blk)),
            pl.BlockSpec((1, blk, kw), lambda i, n: (i, n, v_blk)),
        ],
        out_specs=pl.BlockSpec((1, blk, qw), lambda i, n: (i, n, 0)),
        out_shape=jax.ShapeDtypeStruct((b, s, qw), BF16),
        compiler_params=_cparams(("parallel", "parallel")),
        name="swa",
    )(sinks.astype(F32), proj, proj, proj, proj, proj)


def _tril(n, dtype=F32):
    r = lax.broadcasted_iota(jnp.int32, (n, n), 0)
    c = lax.broadcasted_iota(jnp.int32, (n, n), 1)
    return (c <= r).astype(dtype)


def _fox_gate_kernel(f_ref, b_ref, c_ref, ct_ref, carry_ref):
    @pl.when(pl.program_id(1) == 0)
    def _():
        carry_ref[...] = jnp.zeros_like(carry_ref)

    lf = jax.nn.log_sigmoid(f_ref[0] + b_ref[...])
    cs = jnp.dot(_tril(LANES), lf, precision=HIGHEST, preferred_element_type=F32) + carry_ref[...]
    carry_ref[...] = cs[LANES - 1:LANES, :]
    c_ref[0] = cs
    ct_ref[0, 0] = cs.T[:B_HEADS, :]


def _fox_gate(f_aux, bias):
    b, s, _ = f_aux.shape
    nb = s // LANES
    bias_p = jnp.zeros((1, LANES), F32).at[0, :B_HEADS].set(bias.astype(F32))
    return pl.pallas_call(
        _fox_gate_kernel,
        grid=(b, nb),
        in_specs=[pl.BlockSpec((1, LANES, LANES), lambda i, n: (i, n, 0)),
                  pl.BlockSpec((1, LANES), lambda i, n: (0, 0))],
        out_specs=[pl.BlockSpec((1, LANES, LANES), lambda i, n: (i, n, 0)),
                   pl.BlockSpec((1, 1, B_HEADS, LANES), lambda i, n: (i, n, 0, 0))],
        out_shape=[jax.ShapeDtypeStruct((b, s, LANES), F32),
                   jax.ShapeDtypeStruct((b, nb, B_HEADS, LANES), F32)],
        scratch_shapes=[pltpu.VMEM((1, LANES), F32)],
        compiler_params=_cparams(("parallel", "arbitrary")),
        name="fox_gate",
    )(f_aux, bias_p)


def _fox_kernel(q_ref, k_ref, v_ref, c_ref, ct_ref, o_ref, *, tq):
    qi = pl.program_id(1)
    sub = tq // LANES
    row = lax.broadcasted_iota(jnp.int32, (tq, tq), 0)
    col = lax.broadcasted_iota(jnp.int32, (tq, tq), 1)
    causal = col <= row
    outs = []
    for h in range(B_HEADS):
        hs = slice(h * HEAD_DIM, (h + 1) * HEAD_DIM)
        q = q_ref[0, :, hs] * (HEAD_DIM ** -0.5)
        ccol = c_ref[0, :, h:h + 1]

        def step(j, carry, masked, q=q, ccol=ccol, h=h, hs=hs):
            m, l, acc = carry
            start = pl.multiple_of(j * tq, tq)
            k = k_ref[0, pl.ds(start, tq), hs]
            v = v_ref[0, pl.ds(start, tq), hs]
            crow = jnp.concatenate([ct_ref[0, j * sub + u, h:h + 1, :] for u in range(sub)], axis=1)
            s = lax.dot_general(q, k, (((1,), (1,)), ((), ())), preferred_element_type=F32)
            s = s + (ccol - crow)
            if masked:
                s = jnp.where(causal, s, -jnp.inf)
            m_new = jnp.maximum(m, jnp.max(s, axis=-1, keepdims=True))
            alpha = jnp.exp(m - m_new)
            p = jnp.exp(s - m_new)
            l = alpha * l + jnp.sum(p, axis=-1, keepdims=True)
            acc = alpha * acc + jnp.dot(p.astype(BF16), v, preferred_element_type=F32)
            return m_new, l, acc

        init = (jnp.full((tq, 1), -jnp.inf, F32), jnp.zeros((tq, 1), F32), jnp.zeros((tq, HEAD_DIM), F32))
        carry = lax.fori_loop(0, qi, functools.partial(step, masked=False), init)
        _, l, acc = step(qi, carry, True)
        outs.append(acc / l)
    o_ref[0] = jnp.concatenate(outs, axis=-1).astype(o_ref.dtype)


def _fox(proj, c, ct, *, tq=256):
    b, s, _ = proj.shape
    w = B_HEADS * HEAD_DIM
    nb = s // LANES
    return pl.pallas_call(
        functools.partial(_fox_kernel, tq=tq),
        grid=(b, s // tq),
        in_specs=[
            pl.BlockSpec((1, tq, w), lambda i, n: (i, n, 0)),
            pl.BlockSpec((1, s, w), lambda i, n: (i, 0, 1)),
            pl.BlockSpec((1, s, w), lambda i, n: (i, 0, 2)),
            pl.BlockSpec((1, tq, LANES), lambda i, n: (i, n, 0)),
            pl.BlockSpec((1, nb, B_HEADS, LANES), lambda i, n: (i, 0, 0, 0)),
        ],
        out_specs=pl.BlockSpec((1, tq, w), lambda i, n: (i, n, 0)),
        out_shape=jax.ShapeDtypeStruct((b, s, w), BF16),
        compiler_params=_cparams(("parallel", "parallel")),
        name="fox",
    )(proj, proj, proj, c, ct)


def _ssd_gla_kernel(z_ref, q_ref, k_ref, v_ref, r_ref, xc_ref, xp_ref,
                    cw_ref, cb_ref, dtb_ref, alog_ref, dsk_ref, sn_ref, gkw_ref, gkb_ref, gn_ref,
                    o_ref, conv_ref, hs_ref, gs_ref):
    c = pl.program_id(1)
    q_len = C_CHUNK
    halo = 8

    @pl.when(c == 0)
    def _():
        hs_ref[...] = jnp.zeros_like(hs_ref)
        gs_ref[...] = jnp.zeros_like(gs_ref)

    prev = xp_ref[0, :, :C_CONV_DIM]
    conv_ref[0:halo, :] = jnp.where(c > 0, prev, jnp.zeros_like(prev))
    conv_ref[halo:halo + q_len, :] = xc_ref[0, :, :C_CONV_DIM]
    acc = jnp.zeros((q_len, C_CONV_DIM), F32) + cb_ref[...]
    for j in range(C_CONV):
        off = halo - (C_CONV - 1) + j
        acc = acc + cw_ref[j:j + 1, :] * conv_ref[off:off + q_len, :]
    xbc = jax.nn.silu(acc)
    xs = xbc[:, :C_INNER]
    gs_w = C_GROUPS * C_STATE
    bm = xbc[:, C_INNER:C_INNER + gs_w].astype(BF16)
    cm = xbc[:, C_INNER + gs_w:].astype(BF16)

    row = lax.broadcasted_iota(jnp.int32, (q_len, q_len), 0)
    col = lax.broadcasted_iota(jnp.int32, (q_len, q_len), 1)
    tri = col <= row
    tri_f = tri.astype(F32)

    side = xc_ref[0, :, C_CONV_DIM:]
    dt = jax.nn.softplus(side + dtb_ref[...])
    a = -jnp.exp(alog_ref[...])
    lane = lax.broadcasted_iota(jnp.int32, (1, LANES), 1)
    dta = jnp.where(lane < C_HEADS, dt * a, 0.0)
    acs = jnp.dot(tri_f, dta, precision=HIGHEST, preferred_element_type=F32)
    acs_t = acs.T
    acs_last = acs[q_len - 1:q_len, :]
    dec_end = jnp.exp(acs_last - acs)
    dec_in = jnp.exp(acs)
    chunk_dec = jnp.exp(acs_last)

    y_heads = []
    for g in range(C_GROUPS):
        b_g = bm[:, g * C_STATE:(g + 1) * C_STATE]
        c_g = cm[:, g * C_STATE:(g + 1) * C_STATE]
        cb = lax.dot_general(c_g, b_g, (((1,), (1,)), ((), ())), preferred_element_type=F32)
        h0 = g * C_HPG
        st_g = hs_ref[h0 * C_HEAD_DIM:(h0 + C_HPG) * C_HEAD_DIM, :]
        y_off = lax.dot_general(c_g, st_g.astype(BF16), (((1,), (1,)), ((), ())),
                                preferred_element_type=F32)
        xdd = []
        for hh in range(C_HPG):
            h = h0 + hh
            ps = slice(h * C_HEAD_DIM, (h + 1) * C_HEAD_DIM)
            x_h = xs[:, ps]
            xd = x_h * dt[:, h:h + 1]
            seg = jnp.exp(jnp.where(tri, acs[:, h:h + 1] - acs_t[h:h + 1, :], -jnp.inf))
            y = jnp.dot((cb * seg).astype(BF16), xd.astype(BF16), preferred_element_type=F32)
            y = y + y_off[:, hh * C_HEAD_DIM:(hh + 1) * C_HEAD_DIM] * dec_in[:, h:h + 1]
            y = y + dsk_ref[0:1, h:h + 1] * x_h
            y_heads.append(y)
            xdd.append(xd * dec_end[:, h:h + 1])
        xdd_t = jnp.concatenate(xdd, axis=1).T.astype(BF16)
        upd = jnp.dot(xdd_t, b_g, preferred_element_type=F32)
        for hh in range(C_HPG):
            h = h0 + hh
            ps = slice(h * C_HEAD_DIM, (h + 1) * C_HEAD_DIM)
            us = slice(hh * C_HEAD_DIM, (hh + 1) * C_HEAD_DIM)
            hs_ref[ps, :] = hs_ref[ps, :] * chunk_dec[0:1, h:h + 1] + upd[us, :]
    y = jnp.concatenate(y_heads, axis=1)
    y = y * jax.nn.silu(z_ref[0].astype(F32))
    o_ref[0, :, :C_INNER] = _rms(y, sn_ref[...]).astype(o_ref.dtype)

    same = (row // D_CHUNK) == (col // D_CHUNK)
    tri2 = tri & same
    la = jnp.dot(side.astype(BF16), gkw_ref[...], preferred_element_type=F32) + gkb_ref[...]
    la = jax.nn.log_sigmoid(la) / D_GATE_NORM
    gcs = jnp.dot(tri2.astype(F32), la, precision=HIGHEST, preferred_element_type=F32)
    first = lax.broadcasted_iota(jnp.int32, (q_len, 1), 0) < D_CHUNK
    r_all = r_ref[0]
    for h in range(D_HEADS):
        ks = slice(h * D_HK, (h + 1) * D_HK)
        vs = slice(h * D_HV, (h + 1) * D_HV)
        g_h = gcs[:, ks]
        g_end0 = g_h[D_CHUNK - 1:D_CHUNK, :]
        g_end1 = g_h[q_len - 1:q_len, :]
        q_h = q_ref[0, :, ks].astype(F32) * (D_HK ** -0.5)
        k_h = k_ref[0, :, ks].astype(F32)
        v_h = v_ref[0, :, vs]
        q_dec = (q_h * jnp.exp(g_h)).astype(BF16)
        k_inv = (k_h * jnp.exp(-g_h)).astype(BF16)
        k_end = k_h * jnp.exp(jnp.where(first, g_end0, g_end1) - g_h)
        ke0 = jnp.where(first, k_end, 0.0).astype(BF16)
        ke1 = jnp.where(first, 0.0, k_end).astype(BF16)
        attn = lax.dot_general(q_dec, k_inv, (((1,), (1,)), ((), ())), preferred_element_type=F32)
        attn = jnp.where(tri2, attn, 0.0).astype(BF16)
        o = jnp.dot(attn, v_h, preferred_element_type=F32)
        v_t = v_h.astype(F32).T.astype(BF16)
        st_rows = slice(h * D_HV, (h + 1) * D_HV)
        s0 = gs_ref[st_rows, :]
        s1 = s0 * jnp.exp(g_end0) + jnp.dot(v_t, ke0, preferred_element_type=F32)
        s2 = s1 * jnp.exp(g_end1) + jnp.dot(v_t, ke1, preferred_element_type=F32)
        gs_ref[st_rows, :] = s2
        tdims = (((1,), (1,)), ((), ()))
        o0 = lax.dot_general(q_dec, s0.astype(BF16), tdims, preferred_element_type=F32)
        o1 = lax.dot_general(q_dec, s1.astype(BF16), tdims, preferred_element_type=F32)
        o = o + jnp.where(first, o0, o1)
        o = _rms(o, gn_ref[...]) * jax.nn.silu(r_all[:, vs].astype(F32))
        o_ref[0, :, C_INNER + h * D_HV:C_INNER + (h + 1) * D_HV] = o.astype(o_ref.dtype)


def _ssd_gla(zqkvr, xside, p):
    b, s, _ = zqkvr.shape
    q_len = C_CHUNK
    side_w = xside.shape[-1]
    wide, narrow = C_INNER, D_KEY
    full = lambda shape: pl.BlockSpec(shape, lambda i, n: (0,) * len(shape))
    pad_lanes = lambda vec: jnp.zeros((1, LANES), F32).at[0, :vec.shape[0]].set(vec.astype(F32))
    gkw = jnp.zeros((LANES, D_KEY), F32).at[C_HEADS:C_HEADS + D_GATE_RANK].set(p["gk_w"]).astype(BF16)
    return pl.pallas_call(
        _ssd_gla_kernel,
        grid=(b, s // q_len),
        in_specs=[
            pl.BlockSpec((1, q_len, wide), lambda i, n: (i, n, 0)),
            pl.BlockSpec((1, q_len, narrow), lambda i, n: (i, n, wide // narrow)),
            pl.BlockSpec((1, q_len, narrow), lambda i, n: (i, n, wide // narrow + 1)),
            pl.BlockSpec((1, q_len, wide), lambda i, n: (i, n, 2)),
            pl.BlockSpec((1, q_len, wide), lambda i, n: (i, n, 3)),
            pl.BlockSpec((1, q_len, side_w), lambda i, n: (i, n, 0)),
            pl.BlockSpec((1, 8, side_w), lambda i, n: (i, jnp.maximum(n * (q_len // 8) - 1, 0), 0)),
            full((C_CONV, C_CONV_DIM)), full((1, C_CONV_DIM)),
            full((1, LANES)), full((1, LANES)), full((1, LANES)),
            full((1, C_INNER)), full((LANES, D_KEY)), full((1, D_KEY)), full((1, D_HV)),
        ],
        out_specs=pl.BlockSpec((1, q_len, C_INNER + D_VAL), lambda i, n: (i, n, 0)),
        out_shape=jax.ShapeDtypeStruct((b, s, C_INNER + D_VAL), BF16),
        scratch_shapes=[pltpu.VMEM((8 + q_len, C_CONV_DIM), F32),
                        pltpu.VMEM((C_INNER, C_STATE), F32),
                        pltpu.VMEM((D_VAL, D_HK), F32)],
        compiler_params=_cparams(("parallel", "arbitrary")),
        name="ssd_gla",
    )(zqkvr, zqkvr, zqkvr, zqkvr, zqkvr, xside, xside,
      p["conv_w"].astype(F32), p["conv_b"].reshape(1, -1).astype(F32),
      pad_lanes(p["dt_bias"]), pad_lanes(p["a_log"]), pad_lanes(p["d_skip"]),
      p["ssd_norm"].reshape(1, -1).astype(F32), gkw, p["gk_b"].reshape(1, -1).astype(F32),
      p["gla_norm"].reshape(1, -1).astype(F32))


def _pack_bf16_pairs(x):
    n = x.shape[1] // 2
    u = pltpu.bitcast(x.astype(BF16).astype(F32), jnp.uint32)
    return (u[:, :n] >> 16) | (u[:, n:] & jnp.uint32(0xFFFF0000))


def _unpack_bf16_pairs(u):
    lo = pltpu.bitcast(u << 16, F32).astype(BF16)
    hi = pltpu.bitcast(u & jnp.uint32(0xFFFF0000), F32).astype(BF16)
    return lo, hi


def _out_proj_kernel(n_parts, *refs):
    a_refs = refs[:n_parts]
    w_refs = refs[n_parts:2 * n_parts]
    h_ref, g_ref, wr_ref, br_ref, ho_ref, xp_ref, lg_ref = refs[2 * n_parts:]
    acc = h_ref[...]
    for a_ref, w_ref in zip(a_refs, w_refs):
        acc = acc + jnp.dot(a_ref[...], w_ref[...], preferred_element_type=F32)
    ho_ref[...] = acc
    xn = _rms(acc, g_ref[...])
    lg_ref[...] = jnp.dot(xn, wr_ref[...], precision=HIGHEST, preferred_element_type=F32) + br_ref[...]
    xp_ref[...] = _pack_bf16_pairs(xn)


def _out_proj(parts, w_parts, h, g, w_route, b_route, *, tm=512):
    t, d = h.shape
    tm = min(tm, t)
    row = lambda i: (i, 0)
    const = lambda i: (0, 0)
    in_specs = [pl.BlockSpec((tm, a.shape[1]), row) for a in parts]
    in_specs += [pl.BlockSpec(w.shape, const) for w in w_parts]
    in_specs += [pl.BlockSpec((tm, d), row), pl.BlockSpec((1, d), const),
                 pl.BlockSpec((d, LANES), const), pl.BlockSpec((1, LANES), const)]
    return pl.pallas_call(
        functools.partial(_out_proj_kernel, len(parts)),
        grid=(t // tm,),
        in_specs=in_specs,
        out_specs=[pl.BlockSpec((tm, d), row), pl.BlockSpec((tm, d // 2), row), pl.BlockSpec((tm, LANES), row)],
        out_shape=[jax.ShapeDtypeStruct((t, d), F32), jax.ShapeDtypeStruct((t, d // 2), jnp.uint32),
                   jax.ShapeDtypeStruct((t, LANES), F32)],
        compiler_params=_cparams(("parallel",)),
        name="out_proj",
    )(*parts, *w_parts, h, g.reshape(1, d), w_route, b_route)


def _row_copy_kernel(sidx_ref, didx_ref, src_ref, *rest, n, window):
    dst_ref, sem = rest[-2], rest[-1]

    def copy(si, di):
        return pltpu.make_async_copy(src_ref.at[pl.ds(si, 1)], dst_ref.at[pl.ds(di, 1)], sem)

    def body(i, carry):
        copy(sidx_ref[i], didx_ref[i]).start()

        @pl.when(i >= window)
        def _():
            copy(0, 0).wait()

        return carry

    lax.fori_loop(0, n, body, 0)
    lax.fori_loop(0, window, lambda i, carry: (copy(0, 0).wait(), carry)[1], 0)


def _row_copy(src, src_idx, dst_idx, n_dst, dst_init=None):
    n = src_idx.shape[0]
    any_spec = pl.BlockSpec(memory_space=pl.ANY)
    args = [src_idx, dst_idx, src]
    in_specs = [any_spec]
    aliases = {}
    if dst_init is not None:
        args.append(dst_init)
        in_specs.append(any_spec)
        aliases = {3: 0}
    return pl.pallas_call(
        functools.partial(_row_copy_kernel, n=n, window=COPY_WINDOW),
        grid_spec=pltpu.PrefetchScalarGridSpec(
            num_scalar_prefetch=2, grid=(1,), in_specs=in_specs, out_specs=any_spec,
            scratch_shapes=[pltpu.SemaphoreType.DMA(())]),
        out_shape=jax.ShapeDtypeStruct((n_dst, src.shape[1]), src.dtype),
        input_output_aliases=aliases,
        compiler_params=pltpu.CompilerParams(dimension_semantics=("arbitrary",), has_side_effects=True),
        name="row_copy",
    )(*args)


def _moe_kernel(te_ref, nu_ref, x_ref, wg_ref, wu_ref, wd_ref, y_ref):
    @pl.when(pl.program_id(0) < nu_ref[0])
    def _():
        lo, hi = _unpack_bf16_pairs(x_ref[...])
        half = lo.shape[1]
        wg = wg_ref[...].astype(BF16)
        wu = wu_ref[...].astype(BF16)
        gate = (jnp.dot(lo, wg[:half], preferred_element_type=F32)
                + jnp.dot(hi, wg[half:], preferred_element_type=F32))
        up = (jnp.dot(lo, wu[:half], preferred_element_type=F32)
              + jnp.dot(hi, wu[half:], preferred_element_type=F32))
        act = (jax.nn.silu(gate) * up).astype(BF16)
        y_ref[...] = jnp.dot(act, wd_ref[...].astype(BF16), preferred_element_type=F32)


def _moe_experts(x_rows, tile_expert, n_used, w_gate, w_up, w_down):
    n_rows, half = x_rows.shape
    d = 2 * half
    ff = w_gate.shape[-1]
    n_tiles = n_rows // MOE_TILE
    live = lambda i, te, nu: jnp.minimum(i, nu[0] - 1)
    return pl.pallas_call(
        _moe_kernel,
        grid_spec=pltpu.PrefetchScalarGridSpec(
            num_scalar_prefetch=2, grid=(n_tiles,),
            in_specs=[
                pl.BlockSpec((MOE_TILE, half), lambda i, te, nu: (live(i, te, nu), 0)),
                pl.BlockSpec((None, d, ff), lambda i, te, nu: (te[live(i, te, nu)], 0, 0)),
                pl.BlockSpec((None, d, ff), lambda i, te, nu: (te[live(i, te, nu)], 0, 0)),
                pl.BlockSpec((None, ff, d), lambda i, te, nu: (te[live(i, te, nu)], 0, 0)),
            ],
            out_specs=pl.BlockSpec((MOE_TILE, d), lambda i, te, nu: (live(i, te, nu), 0))),
        out_shape=jax.ShapeDtypeStruct((n_rows, d), F32),
        compiler_params=_cparams(("arbitrary",)),
        name="moe_experts",
    )(tile_expert, n_used, x_rows, w_gate, w_up, w_down)


def _final_norm_kernel(h_ref, m0_ref, m1_ref, gt_ref, g_ref, o_ref):
    gt = gt_ref[...]
    x = h_ref[...] + m0_ref[...] * gt[:, 0:1] + m1_ref[...] * gt[:, 1:2]
    o_ref[...] = _rms(x, g_ref[...])


def _final_norm(h, m, gates, g, *, tm=1024):
    t, d = h.shape
    tm = min(tm, t)
    row = lambda i: (i, 0)
    return pl.pallas_call(
        _final_norm_kernel,
        grid=(t // tm,),
        in_specs=[pl.BlockSpec((tm, d), row),
                  pl.BlockSpec((None, tm, d), lambda i: (0, i, 0)),
                  pl.BlockSpec((None, tm, d), lambda i: (1, i, 0)),
                  pl.BlockSpec((tm, LANES), row), pl.BlockSpec((1, d), lambda i: (0, 0))],
        out_specs=pl.BlockSpec((tm, d), row),
        out_shape=jax.ShapeDtypeStruct((t, d), F32),
        compiler_params=_cparams(("parallel",)),
        name="final_norm",
    )(h, m, m, gates, g.reshape(1, d))


def _route(logits):
    t = logits.shape[0]
    g_prob = jax.nn.softmax(logits[:, :MOE_GROUPS], axis=-1)
    g_w, g_idx = lax.top_k(g_prob, 1)
    e_logits = logits[:, MOE_GROUPS:MOE_GROUPS + MOE_EXPERTS].reshape(t, MOE_GROUPS, MOE_EPG)
    e_logits = jnp.take_along_axis(e_logits, g_idx[:, :, None], axis=1)[:, 0]
    e_w, e_idx = lax.top_k(jax.nn.softmax(e_logits, axis=-1), MOE_TOPK)
    e_w = e_w / jnp.sum(e_w, axis=-1, keepdims=True)
    gate = g_w * e_w
    expert = (g_idx * MOE_EPG + e_idx).astype(jnp.int32)
    onehot = (expert.reshape(-1)[:, None] == jnp.arange(MOE_EXPERTS, dtype=jnp.int32)[None, :]).astype(jnp.int32)
    csum = jnp.cumsum(onehot, axis=0)
    rank = jnp.sum(csum * onehot, axis=-1) - 1
    counts = csum[-1]
    padded = (counts + MOE_TILE - 1) // MOE_TILE * MOE_TILE
    pad_ends = jnp.cumsum(padded)
    pad_starts = pad_ends - padded
    row = (pad_starts[expert.reshape(-1)] + rank).reshape(t, MOE_TOPK).astype(jnp.int32)
    n_tiles = (t * MOE_TOPK + MOE_EXPERTS * (MOE_TILE - 1)) // MOE_TILE
    tile_start = jnp.arange(n_tiles, dtype=jnp.int32) * MOE_TILE
    tile_expert = jnp.minimum(jnp.sum(tile_start[:, None] >= pad_ends[None, :], axis=1),
                              MOE_EXPERTS - 1).astype(jnp.int32)
    n_used = (pad_ends[-1] // MOE_TILE).astype(jnp.int32).reshape(1)
    gates = jnp.zeros((t, LANES), F32).at[:, :MOE_TOPK].set(gate)
    return row, gates, tile_expert, n_used, n_tiles


def _moe(x_packed, logits, w_gate, w_up, w_down):
    t, half = x_packed.shape
    row, gates, tile_expert, n_used, n_tiles = _route(logits)
    n_rows = n_tiles * MOE_TILE
    tok = jnp.arange(t, dtype=jnp.int32)
    src_idx = jnp.concatenate([tok, tok])
    row_km = jnp.concatenate([row[:, 0], row[:, 1]])
    x_rows = _row_copy(x_packed, src_idx, row_km, n_rows, dst_init=jnp.zeros((n_rows, half), jnp.uint32))
    y_rows = _moe_experts(x_rows, tile_expert, n_used, w_gate, w_up, w_down)
    m = _row_copy(y_rows, row_km, jnp.arange(MOE_TOPK * t, dtype=jnp.int32), MOE_TOPK * t)
    return m.reshape(MOE_TOPK, t, 2 * half), gates


def _router_weights(w_group, b_group, w_router, b_router):
    d = w_group.shape[0]
    w = jnp.zeros((d, LANES), F32)
    w = w.at[:, :MOE_GROUPS].set(w_group).at[:, MOE_GROUPS:MOE_GROUPS + MOE_EXPERTS].set(w_router)
    b = jnp.zeros((1, LANES), F32)
    b = b.at[0, :MOE_GROUPS].set(b_group).at[0, MOE_GROUPS:MOE_GROUPS + MOE_EXPERTS].set(b_router)
    return w, b


def kernel(x, norm_mix, norm_moe, norm_final, even_w_in, even_sinks, even_forget_bias, even_w_out,
           odd_w_in, odd_conv_w, odd_conv_b, odd_dt_bias, odd_a_log, odd_d_skip, odd_ssd_norm,
           odd_gk_w, odd_gk_b, odd_gla_norm, odd_w_out, moe_w_group, moe_b_group, moe_w_router,
           moe_b_router, moe_w_gate, moe_w_up, moe_w_down):
    b, s, d = x.shape
    t = b * s
    depth = norm_mix.shape[0]
    h = x.reshape(t, d)
    moe = None
    for layer in range(depth):
        i = layer // 2
        if layer % 2 == 0:
            w = even_w_in[i]
            n_a = (A_Q_HEADS + 2 * A_KV_HEADS) * HEAD_DIM
            n_ab = n_a + 3 * B_HEADS * HEAD_DIM
            w_main = jnp.concatenate([w[:, n_a:n_ab], w[:, :n_a]], axis=1).astype(BF16)
            w_aux = jnp.zeros((d, LANES), F32).at[:, :B_HEADS].set(w[:, n_ab:]).astype(BF16)
            proj, f_aux, h_new = _norm_proj(h, norm_mix[layer], w_main, out_dtype=BF16, tm=1024, tn=768,
                                            w_aux=w_aux, moe=moe)
            h = h if h_new is None else h_new
            proj = proj.reshape(b, s, -1)
            out_a = _swa(proj, even_sinks[i])
            c, ct = _fox_gate(f_aux.reshape(b, s, LANES), even_forget_bias[i])
            out_b = _fox(proj, c, ct)
            n_ha = A_Q_HEADS * HEAD_DIM
            w_out = even_w_out[i].astype(BF16)
            parts = [out_a.reshape(t, -1), out_b.reshape(t, -1)]
            w_parts = [w_out[:n_ha], w_out[n_ha:]]
        else:
            w = odd_w_in[i]
            o_z, o_xbc = 0, C_INNER
            o_dt = o_xbc + C_CONV_DIM
            o_q = o_dt + C_HEADS
            o_k = o_q + D_KEY
            o_v = o_k + D_KEY
            o_g = o_v + D_VAL
            o_r = o_g + D_GATE_RANK
            w_main = jnp.concatenate([w[:, o_z:o_xbc], w[:, o_q:o_g], w[:, o_r:]], axis=1).astype(BF16)
            w_side = jnp.concatenate([w[:, o_xbc:o_q], w[:, o_g:o_r],
                                      jnp.zeros((d, LANES - C_HEADS - D_GATE_RANK), F32)], axis=1).astype(BF16)
            zqkvr, _, h_new = _norm_proj(h, norm_mix[layer], w_main, out_dtype=BF16, tm=1024, tn=1024, moe=moe)
            h = h if h_new is None else h_new
            xside, _, _ = _norm_proj(h, norm_mix[layer], w_side, out_dtype=F32, tm=512, tn=w_side.shape[1])
            params = dict(conv_w=odd_conv_w[i], conv_b=odd_conv_b[i], dt_bias=odd_dt_bias[i], a_log=odd_a_log[i],
                          d_skip=odd_d_skip[i], ssd_norm=odd_ssd_norm[i], gk_w=odd_gk_w[i], gk_b=odd_gk_b[i],
                          gla_norm=odd_gla_norm[i])
            mixed = _ssd_gla(zqkvr.reshape(b, s, -1), xside.reshape(b, s, -1), params)
            parts = [mixed.reshape(t, -1)]
            w_parts = [odd_w_out[i].astype(BF16)]
        w_route, b_route = _router_weights(moe_w_group[layer], moe_b_group[layer],
                                           moe_w_router[layer], moe_b_router[layer])
        h, x_packed, logits = _out_proj(parts, w_parts, h, norm_moe[layer], w_route, b_route)
        moe = _moe(x_packed, logits, moe_w_gate[layer], moe_w_up[layer], moe_w_down[layer])
    out = _final_norm(h, moe[0], moe[1], norm_final)
    return out.reshape(b, s, d)
```

```python
import functools
import math

import numpy as np
import jax
import jax.numpy as jnp
from jax import lax
from jax.experimental import pallas as pl
from jax.experimental.pallas import tpu as pltpu

F32 = jnp.float32
BF16 = jnp.bfloat16
HIGHEST = lax.Precision.HIGHEST

RMS_EPS = 1e-6
HEAD_DIM = 64
A_Q_HEADS = 8
A_KV_HEADS = 2
A_GROUP = A_Q_HEADS // A_KV_HEADS
A_WINDOW = 128
B_HEADS = 8
C_HEADS = 16
C_HEAD_DIM = 64
C_INNER = C_HEADS * C_HEAD_DIM
C_GROUPS = 2
C_HPG = C_HEADS // C_GROUPS
C_STATE = 128
C_CONV = 4
C_CHUNK = 128
C_CONV_DIM = C_INNER + 2 * C_GROUPS * C_STATE
D_HEADS = 4
D_HK = 128
D_HV = 256
D_KEY = D_HEADS * D_HK
D_VAL = D_HEADS * D_HV
D_GATE_RANK = 16
D_GATE_NORM = 16.0
D_CHUNK = 64
MOE_GROUPS = 4
MOE_EPG = 8
MOE_EXPERTS = MOE_GROUPS * MOE_EPG
MOE_TOPK = 2

LANES = 128
VMEM_LIMIT = 48 * 1024 * 1024
MOE_TILE = 512
COPY_WINDOW = 64


def _cparams(sem):
    return pltpu.CompilerParams(dimension_semantics=sem, vmem_limit_bytes=VMEM_LIMIT)


def _rms(x, g):
    ms = jnp.mean(x * x, axis=-1, keepdims=True)
    return x * lax.rsqrt(ms + RMS_EPS) * g


def _norm_proj_kernel(combine, has_aux, *refs):
    it = iter(refs)
    x_ref = next(it)
    if combine:
        m0_ref, m1_ref, gt_ref = next(it), next(it), next(it)
    g_ref, w_ref = next(it), next(it)
    wa_ref = next(it) if has_aux else None
    o_ref = next(it)
    oa_ref = next(it) if has_aux else None
    h_ref = next(it) if combine else None
    xn_ref = next(it)

    @pl.when(pl.program_id(1) == 0)
    def _():
        x = x_ref[...]
        if combine:
            x = _moe_combine(x, m0_ref, m1_ref, gt_ref)
            h_ref[...] = x
        xn = _rms(x, g_ref[...]).astype(BF16)
        xn_ref[...] = xn
        if has_aux:
            oa_ref[...] = jnp.dot(xn, wa_ref[...], preferred_element_type=F32)

    o_ref[...] = jnp.dot(xn_ref[...], w_ref[...], preferred_element_type=F32).astype(o_ref.dtype)


def _norm_proj(x, g, w, *, out_dtype, tm, tn, w_aux=None, moe=None):
    t, d = x.shape
    n = w.shape[1]
    tm = min(tm, t)
    combine = moe is not None
    has_aux = w_aux is not None
    row = lambda i, j: (i, 0)
    in_specs = [pl.BlockSpec((tm, d), row)]
    args = [x]
    if combine:
        m, gates = moe
        nc = d // LANES
        in_specs += [pl.BlockSpec((tm * nc, LANES), row),
                     pl.BlockSpec((tm * nc, LANES), lambda i, j: (t // tm + i, 0)),
                     pl.BlockSpec((tm, LANES), row)]
        args += [m, m, gates]
    in_specs += [pl.BlockSpec((1, d), lambda i, j: (0, 0)), pl.BlockSpec((d, tn), lambda i, j: (0, j))]
    args += [g.reshape(1, d), w]
    if has_aux:
        in_specs.append(pl.BlockSpec((d, LANES), lambda i, j: (0, 0)))
        args.append(w_aux)
    out_shape = [jax.ShapeDtypeStruct((t, n), out_dtype)]
    out_specs = [pl.BlockSpec((tm, tn), lambda i, j: (i, j))]
    if has_aux:
        out_shape.append(jax.ShapeDtypeStruct((t, LANES), F32))
        out_specs.append(pl.BlockSpec((tm, LANES), row))
    if combine:
        out_shape.append(jax.ShapeDtypeStruct((t, d), F32))
        out_specs.append(pl.BlockSpec((tm, d), row))
    outs = pl.pallas_call(
        functools.partial(_norm_proj_kernel, combine, has_aux),
        grid=(t // tm, n // tn),
        in_specs=in_specs, out_specs=out_specs, out_shape=out_shape,
        scratch_shapes=[pltpu.VMEM((tm, d), BF16)],
        compiler_params=_cparams(("parallel", "arbitrary")),
        name="norm_proj",
    )(*args)
    outs = list(outs)
    out = outs.pop(0)
    aux = outs.pop(0) if has_aux else None
    h = outs.pop(0) if combine else None
    return out, aux, h


def _swa_kernel(sink_ref, q_ref, kp_ref, kc_ref, vp_ref, vc_ref, o_ref):
    n = pl.program_id(1)
    blk = A_WINDOW
    row = lax.broadcasted_iota(jnp.int32, (blk, 2 * blk), 0)
    col = lax.broadcasted_iota(jnp.int32, (blk, 2 * blk), 1)
    dist = blk + row - col
    valid = (dist >= 0) & (dist < A_WINDOW) & ((col >= blk) | (n > 0))
    distf = dist.astype(F32)
    outs = []
    for kh in range(A_KV_HEADS):
        ks = slice(kh * HEAD_DIM, (kh + 1) * HEAD_DIM)
        k = jnp.concatenate([kp_ref[0, :, ks], kc_ref[0, :, ks]], axis=0)
        v = jnp.concatenate([vp_ref[0, :, ks], vc_ref[0, :, ks]], axis=0)
        for gi in range(A_GROUP):
            h = kh * A_GROUP + gi
            slope = float(2.0 ** (-8.0 * (h + 1) / A_Q_HEADS))
            q = q_ref[0, :, h * HEAD_DIM:(h + 1) * HEAD_DIM]
            s = lax.dot_general(q, k, (((1,), (1,)), ((), ())), preferred_element_type=F32)
            s = s * (HEAD_DIM ** -0.5) - slope * distf
            s = jnp.where(valid, s, -jnp.inf)
            sink = sink_ref[h]
            m = jnp.maximum(jnp.max(s, axis=-1, keepdims=True), sink)
            p = jnp.exp(s - m)
            denom = jnp.sum(p, axis=-1, keepdims=True) + jnp.exp(sink - m)
            o = jnp.dot(p.astype(BF16), v, preferred_element_type=F32)
            outs.append(o / denom)
    o_ref[0] = jnp.concatenate(outs, axis=-1).astype(o_ref.dtype)


def _swa(proj, sinks):
    b, s, _ = proj.shape
    blk = A_WINDOW
    qw = A_Q_HEADS * HEAD_DIM
    kw = A_KV_HEADS * HEAD_DIM
    q_blk = (3 * B_HEADS * HEAD_DIM) // qw
    k_blk = (3 * B_HEADS * HEAD_DIM + qw) // kw
    v_blk = k_blk + 1
    prev = lambda i, n: jnp.maximum(n - 1, 0)
    return pl.pallas_call(
        _swa_kernel,
        grid=(b, s // blk),
        in_specs=[
            pl.BlockSpec(memory_space=pltpu.SMEM),
            pl.BlockSpec((1, blk, qw), lambda i, n: (i, n, q_blk)),
            pl.BlockSpec((1, blk, kw), lambda i, n: (i, prev(i, n), k_blk)),
            pl.BlockSpec((1, blk, kw), lambda i, n: (i, n, k_blk)),
            pl.BlockSpec((1, blk, kw), lambda i, n: (i, prev(i, n), v_blk)),
            pl.BlockSpec((1, blk, kw), lambda i, n: (i, n, v_blk)),
        ],
        out_specs=pl.BlockSpec((1, blk, qw), lambda i, n: (i, n, 0)),
        out_shape=jax.ShapeDtypeStruct((b, s, qw), BF16),
        compiler_params=_cparams(("parallel", "parallel")),
        name="swa",
    )(sinks.astype(F32), proj, proj, proj, proj, proj)


def _tril(n, dtype=F32):
    r = lax.broadcasted_iota(jnp.int32, (n, n), 0)
    c = lax.broadcasted_iota(jnp.int32, (n, n), 1)
    return (c <= r).astype(dtype)


def _fox_gate_kernel(f_ref, b_ref, c_ref, ct_ref, carry_ref):
    @pl.when(pl.program_id(1) == 0)
    def _():
        carry_ref[...] = jnp.zeros_like(carry_ref)

    lf = jax.nn.log_sigmoid(f_ref[0] + b_ref[...])
    cs = jnp.dot(_tril(LANES), lf, precision=HIGHEST, preferred_element_type=F32) + carry_ref[...]
    carry_ref[...] = cs[LANES - 1:LANES, :]
    c_ref[0] = cs
    ct_ref[0, 0] = cs.T[:B_HEADS, :]


def _fox_gate(f_aux, bias):
    b, s, _ = f_aux.shape
    nb = s // LANES
    bias_p = jnp.zeros((1, LANES), F32).at[0, :B_HEADS].set(bias.astype(F32))
    return pl.pallas_call(
        _fox_gate_kernel,
        grid=(b, nb),
        in_specs=[pl.BlockSpec((1, LANES, LANES), lambda i, n: (i, n, 0)),
                  pl.BlockSpec((1, LANES), lambda i, n: (0, 0))],
        out_specs=[pl.BlockSpec((1, LANES, LANES), lambda i, n: (i, n, 0)),
                   pl.BlockSpec((1, 1, B_HEADS, LANES), lambda i, n: (i, n, 0, 0))],
        out_shape=[jax.ShapeDtypeStruct((b, s, LANES), F32),
                   jax.ShapeDtypeStruct((b, nb, B_HEADS, LANES), F32)],
        scratch_shapes=[pltpu.VMEM((1, LANES), F32)],
        compiler_params=_cparams(("parallel", "arbitrary")),
        name="fox_gate",
    )(f_aux, bias_p)


def _fox_kernel(q_ref, k_ref, v_ref, c_ref, ct_ref, o_ref, *, tq):
    qi = pl.program_id(1)
    sub = tq // LANES
    row = lax.broadcasted_iota(jnp.int32, (tq, tq), 0)
    col = lax.broadcasted_iota(jnp.int32, (tq, tq), 1)
    causal = col <= row
    outs = []
    for h in range(B_HEADS):
        hs = slice(h * HEAD_DIM, (h + 1) * HEAD_DIM)
        q = q_ref[0, :, hs] * (HEAD_DIM ** -0.5)
        ccol = c_ref[0, :, h:h + 1]

        def step(j, carry, masked, q=q, ccol=ccol, h=h, hs=hs):
            m, l, acc = carry
            start = pl.multiple_of(j * tq, tq)
            k = k_ref[0, pl.ds(start, tq), hs]
            v = v_ref[0, pl.ds(start, tq), hs]
            crow = jnp.concatenate([ct_ref[0, j * sub + u, h:h + 1, :] for u in range(sub)], axis=1)
            s = lax.dot_general(q, k, (((1,), (1,)), ((), ())), preferred_element_type=F32)
            s = s + (ccol - crow)
            if masked:
                s = jnp.where(causal, s, -jnp.inf)
            m_new = jnp.maximum(m, jnp.max(s, axis=-1, keepdims=True))
            alpha = jnp.exp(m - m_new)
            p = jnp.exp(s - m_new)
            l = alpha * l + jnp.sum(p, axis=-1, keepdims=True)
            acc = alpha * acc + jnp.dot(p.astype(BF16), v, preferred_element_type=F32)
            return m_new, l, acc

        init = (jnp.full((tq, 1), -jnp.inf, F32), jnp.zeros((tq, 1), F32), jnp.zeros((tq, HEAD_DIM), F32))
        carry = lax.fori_loop(0, qi, functools.partial(step, masked=False), init)
        _, l, acc = step(qi, carry, True)
        outs.append(acc / l)
    o_ref[0] = jnp.concatenate(outs, axis=-1).astype(o_ref.dtype)


def _fox(proj, c, ct, *, tq=256):
    b, s, _ = proj.shape
    w = B_HEADS * HEAD_DIM
    nb = s // LANES
    return pl.pallas_call(
        functools.partial(_fox_kernel, tq=tq),
        grid=(b, s // tq),
        in_specs=[
            pl.BlockSpec((1, tq, w), lambda i, n: (i, n, 0)),
            pl.BlockSpec((1, s, w), lambda i, n: (i, 0, 1)),
            pl.BlockSpec((1, s, w), lambda i, n: (i, 0, 2)),
            pl.BlockSpec((1, tq, LANES), lambda i, n: (i, n, 0)),
            pl.BlockSpec((1, nb, B_HEADS, LANES), lambda i, n: (i, 0, 0, 0)),
        ],
        out_specs=pl.BlockSpec((1, tq, w), lambda i, n: (i, n, 0)),
        out_shape=jax.ShapeDtypeStruct((b, s, w), BF16),
        compiler_params=_cparams(("parallel", "parallel")),
        name="fox",
    )(proj, proj, proj, c, ct)


def _ssd_gla_kernel(z_ref, q_ref, k_ref, v_ref, r_ref, xc_ref, xp_ref,
                    cw_ref, cb_ref, dtb_ref, alog_ref, dsk_ref, sn_ref, gkw_ref, gkb_ref, gn_ref,
                    o_ref, conv_ref, hs_ref, gs_ref):
    c = pl.program_id(1)
    q_len = C_CHUNK
    halo = 8

    @pl.when(c == 0)
    def _():
        hs_ref[...] = jnp.zeros_like(hs_ref)
        gs_ref[...] = jnp.zeros_like(gs_ref)

    prev = xp_ref[0, :, :C_CONV_DIM]
    conv_ref[0:halo, :] = jnp.where(c > 0, prev, jnp.zeros_like(prev))
    conv_ref[halo:halo + q_len, :] = xc_ref[0, :, :C_CONV_DIM]
    acc = jnp.zeros((q_len, C_CONV_DIM), F32) + cb_ref[...]
    for j in range(C_CONV):
        off = halo - (C_CONV - 1) + j
        acc = acc + cw_ref[j:j + 1, :] * conv_ref[off:off + q_len, :]
    xbc = jax.nn.silu(acc)
    xs = xbc[:, :C_INNER]
    gs_w = C_GROUPS * C_STATE
    bm = xbc[:, C_INNER:C_INNER + gs_w].astype(BF16)
    cm = xbc[:, C_INNER + gs_w:].astype(BF16)

    row = lax.broadcasted_iota(jnp.int32, (q_len, q_len), 0)
    col = lax.broadcasted_iota(jnp.int32, (q_len, q_len), 1)
    tri = col <= row
    tri_f = tri.astype(F32)

    side = xc_ref[0, :, C_CONV_DIM:]
    dt = jax.nn.softplus(side + dtb_ref[...])
    a = -jnp.exp(alog_ref[...])
    lane = lax.broadcasted_iota(jnp.int32, (1, LANES), 1)
    dta = jnp.where(lane < C_HEADS, dt * a, 0.0)
    acs = jnp.dot(tri_f, dta, precision=HIGHEST, preferred_element_type=F32)
    acs_t = acs.T
    acs_last = acs[q_len - 1:q_len, :]
    dec_end = jnp.exp(acs_last - acs)
    dec_in = jnp.exp(acs)
    chunk_dec = jnp.exp(acs_last)

    y_heads = []
    for g in range(C_GROUPS):
        b_g = bm[:, g * C_STATE:(g + 1) * C_STATE]
        c_g = cm[:, g * C_STATE:(g + 1) * C_STATE]
        cb = lax.dot_general(c_g, b_g, (((1,), (1,)), ((), ())), preferred_element_type=F32)
        h0 = g * C_HPG
        st_g = hs_ref[h0 * C_HEAD_DIM:(h0 + C_HPG) * C_HEAD_DIM, :]
        y_off = lax.dot_general(c_g, st_g.astype(BF16), (((1,), (1,)), ((), ())),
                                preferred_element_type=F32)
        xdd = []
        for hh in range(C_HPG):
            h = h0 + hh
            ps = slice(h * C_HEAD_DIM, (h + 1) * C_HEAD_DIM)
            x_h = xs[:, ps]
            xd = x_h * dt[:, h:h + 1]
            seg = jnp.exp(jnp.where(tri, acs[:, h:h + 1] - acs_t[h:h + 1, :], -jnp.inf))
            y = jnp.dot((cb * seg).astype(BF16), xd.astype(BF16), preferred_element_type=F32)
            y = y + y_off[:, hh * C_HEAD_DIM:(hh + 1) * C_HEAD_DIM] * dec_in[:, h:h + 1]
            y = y + dsk_ref[0:1, h:h + 1] * x_h
            y_heads.append(y)
            xdd.append(xd * dec_end[:, h:h + 1])
        xdd_t = jnp.concatenate(xdd, axis=1).T.astype(BF16)
        upd = jnp.dot(xdd_t, b_g, preferred_element_type=F32)
        for hh in range(C_HPG):
            h = h0 + hh
            ps = slice(h * C_HEAD_DIM, (h + 1) * C_HEAD_DIM)
            us = slice(hh * C_HEAD_DIM, (hh + 1) * C_HEAD_DIM)
            hs_ref[ps, :] = hs_ref[ps, :] * chunk_dec[0:1, h:h + 1] + upd[us, :]
    y = jnp.concatenate(y_heads, axis=1)
    y = y * jax.nn.silu(z_ref[0].astype(F32))
    o_ref[0, :, :C_INNER] = _rms(y, sn_ref[...]).astype(o_ref.dtype)

    same = (row // D_CHUNK) == (col // D_CHUNK)
    tri2 = tri & same
    la = jnp.dot(side.astype(BF16), gkw_ref[...], preferred_element_type=F32) + gkb_ref[...]
    la = jax.nn.log_sigmoid(la) / D_GATE_NORM
    gcs = jnp.dot(tri2.astype(F32), la, precision=HIGHEST, preferred_element_type=F32)
    first = lax.broadcasted_iota(jnp.int32, (q_len, 1), 0) < D_CHUNK
    r_all = r_ref[0]
    for h in range(D_HEADS):
        ks = slice(h * D_HK, (h + 1) * D_HK)
        vs = slice(h * D_HV, (h + 1) * D_HV)
        g_h = gcs[:, ks]
        g_end0 = g_h[D_CHUNK - 1:D_CHUNK, :]
        g_end1 = g_h[q_len - 1:q_len, :]
        q_h = q_ref[0, :, ks].astype(F32) * (D_HK ** -0.5)
        k_h = k_ref[0, :, ks].astype(F32)
        v_h = v_ref[0, :, vs]
        q_dec = (q_h * jnp.exp(g_h)).astype(BF16)
        k_inv = (k_h * jnp.exp(-g_h)).astype(BF16)
        k_end = k_h * jnp.exp(jnp.where(first, g_end0, g_end1) - g_h)
        ke0 = jnp.where(first, k_end, 0.0).astype(BF16)
        ke1 = jnp.where(first, 0.0, k_end).astype(BF16)
        attn = lax.dot_general(q_dec, k_inv, (((1,), (1,)), ((), ())), preferred_element_type=F32)
        attn = jnp.where(tri2, attn, 0.0).astype(BF16)
        o = jnp.dot(attn, v_h, preferred_element_type=F32)
        v_t = v_h.astype(F32).T.astype(BF16)
        st_rows = slice(h * D_HV, (h + 1) * D_HV)
        s0 = gs_ref[st_rows, :]
        s1 = s0 * jnp.exp(g_end0) + jnp.dot(v_t, ke0, preferred_element_type=F32)
        s2 = s1 * jnp.exp(g_end1) + jnp.dot(v_t, ke1, preferred_element_type=F32)
        gs_ref[st_rows, :] = s2
        tdims = (((1,), (1,)), ((), ()))
        o0 = lax.dot_general(q_dec, s0.astype(BF16), tdims, preferred_element_type=F32)
        o1 = lax.dot_general(q_dec, s1.astype(BF16), tdims, preferred_element_type=F32)
        o = o + jnp.where(first, o0, o1)
        o = _rms(o, gn_ref[...]) * jax.nn.silu(r_all[:, vs].astype(F32))
        o_ref[0, :, C_INNER + h * D_HV:C_INNER + (h + 1) * D_HV] = o.astype(o_ref.dtype)


def _ssd_gla(zqkvr, xside, p):
    b, s, _ = zqkvr.shape
    q_len = C_CHUNK
    side_w = xside.shape[-1]
    wide, narrow = C_INNER, D_KEY
    full = lambda shape: pl.BlockSpec(shape, lambda i, n: (0,) * len(shape))
    pad_lanes = lambda vec: jnp.zeros((1, LANES), F32).at[0, :vec.shape[0]].set(vec.astype(F32))
    gkw = jnp.zeros((LANES, D_KEY), F32).at[C_HEADS:C_HEADS + D_GATE_RANK].set(p["gk_w"]).astype(BF16)
    return pl.pallas_call(
        _ssd_gla_kernel,
        grid=(b, s // q_len),
        in_specs=[
            pl.BlockSpec((1, q_len, wide), lambda i, n: (i, n, 0)),
            pl.BlockSpec((1, q_len, narrow), lambda i, n: (i, n, wide // narrow)),
            pl.BlockSpec((1, q_len, narrow), lambda i, n: (i, n, wide // narrow + 1)),
            pl.BlockSpec((1, q_len, wide), lambda i, n: (i, n, 2)),
            pl.BlockSpec((1, q_len, wide), lambda i, n: (i, n, 3)),
            pl.BlockSpec((1, q_len, side_w), lambda i, n: (i, n, 0)),
            pl.BlockSpec((1, 8, side_w), lambda i, n: (i, jnp.maximum(n * (q_len // 8) - 1, 0), 0)),
            full((C_CONV, C_CONV_DIM)), full((1, C_CONV_DIM)),
            full((1, LANES)), full((1, LANES)), full((1, LANES)),
            full((1, C_INNER)), full((LANES, D_KEY)), full((1, D_KEY)), full((1, D_HV)),
        ],
        out_specs=pl.BlockSpec((1, q_len, C_INNER + D_VAL), lambda i, n: (i, n, 0)),
        out_shape=jax.ShapeDtypeStruct((b, s, C_INNER + D_VAL), BF16),
        scratch_shapes=[pltpu.VMEM((8 + q_len, C_CONV_DIM), F32),
                        pltpu.VMEM((C_INNER, C_STATE), F32),
                        pltpu.VMEM((D_VAL, D_HK), F32)],
        compiler_params=_cparams(("parallel", "arbitrary")),
        name="ssd_gla",
    )(zqkvr, zqkvr, zqkvr, zqkvr, zqkvr, xside, xside,
      p["conv_w"].astype(F32), p["conv_b"].reshape(1, -1).astype(F32),
      pad_lanes(p["dt_bias"]), pad_lanes(p["a_log"]), pad_lanes(p["d_skip"]),
      p["ssd_norm"].reshape(1, -1).astype(F32), gkw, p["gk_b"].reshape(1, -1).astype(F32),
      p["gla_norm"].reshape(1, -1).astype(F32))


def _store_rows_tiled(ref, val):
    m, d = val.shape
    nc = d // LANES
    for c in range(nc):
        ref[pl.ds(c, m, stride=nc), :] = val[:, c * LANES:(c + 1) * LANES]


def _load_rows_tiled(ref, m, dtype=None):
    nc = ref.shape[0] // m
    parts = [ref[pl.ds(c, m, stride=nc), :] for c in range(nc)]
    if dtype is not None:
        parts = [p.astype(dtype) for p in parts]
    return jnp.concatenate(parts, axis=1)


RT_GATE, RT_EXPERT, RT_RANK = 0, 2, 4


def _route_block(lg, carry):
    m = lg.shape[0]
    lane = lax.broadcasted_iota(jnp.int32, (m, LANES), 1)
    lane_f = lane.astype(F32)
    none = float(LANES)
    neg = -jnp.inf
    first_max = lambda v, vmax: jnp.min(jnp.where(v == vmax, lane_f, none), axis=-1, keepdims=True)
    gl = jnp.where(lane < MOE_GROUPS, lg, neg)
    gmax = jnp.max(gl, axis=-1, keepdims=True)
    g_w = 1.0 / jnp.sum(jnp.exp(gl - gmax), axis=-1, keepdims=True)
    lo = MOE_GROUPS + first_max(gl, gmax) * MOE_EPG
    el = jnp.where((lane_f >= lo) & (lane_f < lo + MOE_EPG), lg, neg)
    emax = jnp.max(el, axis=-1, keepdims=True)
    esum = jnp.sum(jnp.exp(el - emax), axis=-1, keepdims=True)
    l0 = first_max(el, emax)
    el2 = jnp.where(lane_f == l0, neg, el)
    emax2 = jnp.max(el2, axis=-1, keepdims=True)
    l1 = first_max(el2, emax2)
    p0 = 1.0 / esum
    p1 = jnp.exp(emax2 - emax) / esum
    w0 = g_w * (p0 / (p0 + p1))
    w1 = g_w * (p1 / (p0 + p1))
    oh0 = lane_f == l0
    oh1 = lane_f == l1
    oh = (oh0 | oh1).astype(BF16)
    r = lax.broadcasted_iota(jnp.int32, (m, m), 0)
    c = lax.broadcasted_iota(jnp.int32, (m, m), 1)
    cum = jnp.dot((c < r).astype(BF16), oh, preferred_element_type=F32) + carry
    rank0 = jnp.sum(jnp.where(oh0, cum, 0.0), axis=-1, keepdims=True)
    rank1 = jnp.sum(jnp.where(oh1, cum, 0.0), axis=-1, keepdims=True)
    carry = carry + jnp.sum(oh.astype(F32), axis=0, keepdims=True)
    rec = jnp.zeros((m, LANES), F32)
    for pos, val in ((RT_GATE, w0), (RT_GATE + 1, w1), (RT_EXPERT, l0 - MOE_GROUPS),
                     (RT_EXPERT + 1, l1 - MOE_GROUPS), (RT_RANK, rank0), (RT_RANK + 1, rank1)):
        rec = jnp.where(lane == pos, val, rec)
    return rec, carry


def _out_proj_kernel(n_parts, *refs):
    a_refs = refs[:n_parts]
    w_refs = refs[n_parts:2 * n_parts]
    h_ref, g_ref, wr_ref, br_ref, ho_ref, xt_ref, rt_ref, cnt_ref, carry_ref = refs[2 * n_parts:]

    @pl.when(pl.program_id(0) == 0)
    def _():
        carry_ref[...] = jnp.zeros_like(carry_ref)

    acc = h_ref[...]
    for a_ref, w_ref in zip(a_refs, w_refs):
        acc = acc + jnp.dot(a_ref[...], w_ref[...], preferred_element_type=F32)
    ho_ref[...] = acc
    xn = _rms(acc, g_ref[...])
    _store_rows_tiled(xt_ref, xn)
    lg = jnp.dot(xn, wr_ref[...], precision=HIGHEST, preferred_element_type=F32) + br_ref[...]
    rec, carry = _route_block(lg, carry_ref[...])
    rt_ref[...] = rec
    carry_ref[...] = carry
    cnt_ref[...] = carry


def _out_proj(parts, w_parts, h, g, w_route, b_route, *, tm=512):
    t, d = h.shape
    tm = min(tm, t)
    nc = d // LANES
    row = lambda i: (i, 0)
    const = lambda i: (0, 0)
    in_specs = [pl.BlockSpec((tm, a.shape[1]), row) for a in parts]
    in_specs += [pl.BlockSpec(w.shape, const) for w in w_parts]
    in_specs += [pl.BlockSpec((tm, d), row), pl.BlockSpec((1, d), const),
                 pl.BlockSpec((d, LANES), const), pl.BlockSpec((1, LANES), const)]
    return pl.pallas_call(
        functools.partial(_out_proj_kernel, len(parts)),
        grid=(t // tm,),
        in_specs=in_specs,
        out_specs=[pl.BlockSpec((tm, d), row), pl.BlockSpec((tm * nc, LANES), row),
                   pl.BlockSpec((tm, LANES), row), pl.BlockSpec((1, LANES), const)],
        out_shape=[jax.ShapeDtypeStruct((t, d), F32), jax.ShapeDtypeStruct((t * nc, LANES), F32),
                   jax.ShapeDtypeStruct((t, LANES), F32), jax.ShapeDtypeStruct((1, LANES), F32)],
        scratch_shapes=[pltpu.VMEM((1, LANES), F32)],
        compiler_params=_cparams(("arbitrary",)),
        name="out_proj",
    )(*parts, *w_parts, h, g.reshape(1, d), w_route, b_route)


def _moe_rows_kernel(e_ref, r_ref, st_ref, src_ref, *rest, n, n_tok, window, dispatch):
    dst_ref, sem = rest[-2], rest[-1]

    def copy(src_row, dst_row):
        return pltpu.make_async_copy(src_ref.at[src_row], dst_ref.at[dst_row], sem)

    def start(i):
        row = st_ref[e_ref[i]] + r_ref[i]
        if dispatch:
            copy(jnp.where(i >= n_tok, i - n_tok, i), row).start()
        else:
            copy(row, i).start()

    def wait_one():
        copy(0, 0).wait()

    def fill(i, carry):
        start(i)
        return carry

    def steady(i, carry):
        wait_one()
        start(i)
        return carry

    def drain(i, carry):
        wait_one()
        return carry

    lax.fori_loop(0, window, fill, 0, unroll=8)
    lax.fori_loop(window, n, steady, 0, unroll=8)
    lax.fori_loop(0, window, drain, 0, unroll=8)


def _moe_rows(src, expert, rank, starts, n_dst, n_tok, *, dispatch, dst_init=None):
    n = expert.shape[0]
    any_spec = pl.BlockSpec(memory_space=pl.ANY)
    args = [expert, rank, starts, src]
    in_specs = [any_spec]
    aliases = {}
    if dst_init is not None:
        args.append(dst_init)
        in_specs.append(any_spec)
        aliases = {4: 0}
    return pl.pallas_call(
        functools.partial(_moe_rows_kernel, n=n, n_tok=n_tok, window=min(COPY_WINDOW, n), dispatch=dispatch),
        grid_spec=pltpu.PrefetchScalarGridSpec(
            num_scalar_prefetch=3, grid=(1,), in_specs=in_specs, out_specs=any_spec,
            scratch_shapes=[pltpu.SemaphoreType.DMA(())]),
        out_shape=jax.ShapeDtypeStruct((n_dst,) + src.shape[1:], src.dtype),
        input_output_aliases=aliases,
        compiler_params=pltpu.CompilerParams(dimension_semantics=("arbitrary",), has_side_effects=True),
        name="moe_dispatch" if dispatch else "moe_gather",
    )(*args)


def _moe_kernel(te_ref, nu_ref, x_ref, wg_ref, wu_ref, wd_ref, y_ref):
    @pl.when(pl.program_id(0) < nu_ref[0])
    def _():
        x = _load_rows_tiled(x_ref, MOE_TILE, BF16)
        gate = jnp.dot(x, wg_ref[...].astype(BF16), preferred_element_type=F32)
        up = jnp.dot(x, wu_ref[...].astype(BF16), preferred_element_type=F32)
        act = (jax.nn.silu(gate) * up).astype(BF16)
        y = jnp.dot(act, wd_ref[...].astype(BF16), preferred_element_type=F32)
        _store_rows_tiled(y_ref, y)


def _moe_experts(x_rows, tile_expert, n_used, w_gate, w_up, w_down):
    d, ff = w_gate.shape[-2:]
    nc = d // LANES
    n_tiles = x_rows.shape[0] // (MOE_TILE * nc)
    live = lambda i, te, nu: jnp.minimum(i, nu[0] - 1)
    return pl.pallas_call(
        _moe_kernel,
        grid_spec=pltpu.PrefetchScalarGridSpec(
            num_scalar_prefetch=2, grid=(n_tiles,),
            in_specs=[
                pl.BlockSpec((MOE_TILE * nc, LANES), lambda i, te, nu: (live(i, te, nu), 0)),
                pl.BlockSpec((None, d, ff), lambda i, te, nu: (te[live(i, te, nu)], 0, 0)),
                pl.BlockSpec((None, d, ff), lambda i, te, nu: (te[live(i, te, nu)], 0, 0)),
                pl.BlockSpec((None, ff, d), lambda i, te, nu: (te[live(i, te, nu)], 0, 0)),
            ],
            out_specs=pl.BlockSpec((MOE_TILE * nc, LANES), lambda i, te, nu: (live(i, te, nu), 0))),
        out_shape=jax.ShapeDtypeStruct(x_rows.shape, F32),
        compiler_params=_cparams(("arbitrary",)),
        name="moe_experts",
    )(tile_expert, n_used, x_rows, w_gate, w_up, w_down)


def _moe_combine(h, m0_ref, m1_ref, rt_ref):
    rt = rt_ref[...]
    tm = h.shape[0]
    return (h + _load_rows_tiled(m0_ref, tm) * rt[:, RT_GATE:RT_GATE + 1]
            + _load_rows_tiled(m1_ref, tm) * rt[:, RT_GATE + 1:RT_GATE + 2])


def _final_norm_kernel(h_ref, m0_ref, m1_ref, rt_ref, g_ref, o_ref):
    o_ref[...] = _rms(_moe_combine(h_ref[...], m0_ref, m1_ref, rt_ref), g_ref[...])


def _final_norm(h, m, route, g, *, tm=512):
    t, d = h.shape
    tm = min(tm, t)
    nc = d // LANES
    row = lambda i: (i, 0)
    return pl.pallas_call(
        _final_norm_kernel,
        grid=(t // tm,),
        in_specs=[pl.BlockSpec((tm, d), row),
                  pl.BlockSpec((tm * nc, LANES), row),
                  pl.BlockSpec((tm * nc, LANES), lambda i: (t // tm + i, 0)),
                  pl.BlockSpec((tm, LANES), row), pl.BlockSpec((1, d), lambda i: (0, 0))],
        out_specs=pl.BlockSpec((tm, d), row),
        out_shape=jax.ShapeDtypeStruct((t, d), F32),
        compiler_params=_cparams(("parallel",)),
        name="final_norm",
    )(h, m, m, route, g.reshape(1, d))


def _moe(x_tiled, route, counts, w_gate, w_up, w_down):
    t = route.shape[0]
    nc = x_tiled.shape[0] // t
    cnt = counts[0, MOE_GROUPS:MOE_GROUPS + MOE_EXPERTS].astype(jnp.int32)
    padded = (cnt + MOE_TILE - 1) // MOE_TILE * MOE_TILE
    pad_ends = jnp.cumsum(padded)
    starts = (pad_ends - padded).astype(jnp.int32)
    n_tiles = (t * MOE_TOPK + MOE_EXPERTS * (MOE_TILE - 1)) // MOE_TILE
    tile_start = jnp.arange(n_tiles, dtype=jnp.int32) * MOE_TILE
    tile_expert = jnp.minimum(jnp.sum(tile_start[:, None] >= pad_ends[None, :], axis=1),
                              MOE_EXPERTS - 1).astype(jnp.int32)
    n_used = (pad_ends[-1] // MOE_TILE).astype(jnp.int32).reshape(1)
    n_rows = n_tiles * MOE_TILE
    k_major = lambda lane0: route[:, lane0:lane0 + MOE_TOPK].T.reshape(-1).astype(jnp.int32)
    expert, rank = k_major(RT_EXPERT), k_major(RT_RANK)
    x_rows = _moe_rows(x_tiled.reshape(t, nc, LANES), expert, rank, starts, n_rows, t, dispatch=True,
                       dst_init=jnp.zeros((n_rows, nc, LANES), F32))
    y_rows = _moe_experts(x_rows.reshape(n_rows * nc, LANES), tile_expert, n_used, w_gate, w_up, w_down)
    m = _moe_rows(y_rows.reshape(n_rows, nc, LANES), expert, rank, starts, MOE_TOPK * t, t, dispatch=False)
    return m.reshape(MOE_TOPK * t * nc, LANES)


def _router_weights(w_group, b_group, w_router, b_router):
    d = w_group.shape[0]
    w = jnp.zeros((d, LANES), F32)
    w = w.at[:, :MOE_GROUPS].set(w_group).at[:, MOE_GROUPS:MOE_GROUPS + MOE_EXPERTS].set(w_router)
    b = jnp.zeros((1, LANES), F32)
    b = b.at[0, :MOE_GROUPS].set(b_group).at[0, MOE_GROUPS:MOE_GROUPS + MOE_EXPERTS].set(b_router)
    return w, b


def kernel(x, norm_mix, norm_moe, norm_final, even_w_in, even_sinks, even_forget_bias, even_w_out,
           odd_w_in, odd_conv_w, odd_conv_b, odd_dt_bias, odd_a_log, odd_d_skip, odd_ssd_norm,
           odd_gk_w, odd_gk_b, odd_gla_norm, odd_w_out, moe_w_group, moe_b_group, moe_w_router,
           moe_b_router, moe_w_gate, moe_w_up, moe_w_down):
    b, s, d = x.shape
    t = b * s
    depth = norm_mix.shape[0]
    h = x.reshape(t, d)
    moe = None
    for layer in range(depth):
        i = layer // 2
        if layer % 2 == 0:
            w = even_w_in[i]
            n_a = (A_Q_HEADS + 2 * A_KV_HEADS) * HEAD_DIM
            n_ab = n_a + 3 * B_HEADS * HEAD_DIM
            w_main = jnp.concatenate([w[:, n_a:n_ab], w[:, :n_a]], axis=1).astype(BF16)
            w_aux = jnp.zeros((d, LANES), F32).at[:, :B_HEADS].set(w[:, n_ab:]).astype(BF16)
            proj, f_aux, h_new = _norm_proj(h, norm_mix[layer], w_main, out_dtype=BF16, tm=1024, tn=768,
                                            w_aux=w_aux, moe=moe)
            h = h if h_new is None else h_new
            proj = proj.reshape(b, s, -1)
            out_a = _swa(proj, even_sinks[i])
            c, ct = _fox_gate(f_aux.reshape(b, s, LANES), even_forget_bias[i])
            out_b = _fox(proj, c, ct)
            n_ha = A_Q_HEADS * HEAD_DIM
            w_out = even_w_out[i].astype(BF16)
            parts = [out_a.reshape(t, -1), out_b.reshape(t, -1)]
            w_parts = [w_out[:n_ha], w_out[n_ha:]]
        else:
            w = odd_w_in[i]
            o_z, o_xbc = 0, C_INNER
            o_dt = o_xbc + C_CONV_DIM
            o_q = o_dt + C_HEADS
            o_k = o_q + D_KEY
            o_v = o_k + D_KEY
            o_g = o_v + D_VAL
            o_r = o_g + D_GATE_RANK
            w_main = jnp.concatenate([w[:, o_z:o_xbc], w[:, o_q:o_g], w[:, o_r:]], axis=1).astype(BF16)
            w_side = jnp.concatenate([w[:, o_xbc:o_q], w[:, o_g:o_r],
                                      jnp.zeros((d, LANES - C_HEADS - D_GATE_RANK), F32)], axis=1).astype(BF16)
            zqkvr, _, h_new = _norm_proj(h, norm_mix[layer], w_main, out_dtype=BF16, tm=512, tn=1024, moe=moe)
            h = h if h_new is None else h_new
            xside, _, _ = _norm_proj(h, norm_mix[layer], w_side, out_dtype=F32, tm=512, tn=w_side.shape[1])
            params = dict(conv_w=odd_conv_w[i], conv_b=odd_conv_b[i], dt_bias=odd_dt_bias[i], a_log=odd_a_log[i],
                          d_skip=odd_d_skip[i], ssd_norm=odd_ssd_norm[i], gk_w=odd_gk_w[i], gk_b=odd_gk_b[i],
                          gla_norm=odd_gla_norm[i])
            mixed = _ssd_gla(zqkvr.reshape(b, s, -1), xside.reshape(b, s, -1), params)
            parts = [mixed.reshape(t, -1)]
            w_parts = [odd_w_out[i].astype(BF16)]
        w_route, b_route = _router_weights(moe_w_group[layer], moe_b_group[layer],
                                           moe_w_router[layer], moe_b_router[layer])
        h, x_tiled, route, counts = _out_proj(parts, w_parts, h, norm_moe[layer], w_route, b_route)
        moe = (_moe(x_tiled, route, counts, moe_w_gate[layer], moe_w_up[layer], moe_w_down[layer]), route)
    out = _final_norm(h, moe[0], moe[1], norm_final)
    return out.reshape(b, s, d)
```

```python
import functools
import math

import numpy as np
import jax
import jax.numpy as jnp
from jax import lax
from jax.experimental import pallas as pl
from jax.experimental.pallas import tpu as pltpu

F32 = jnp.float32
BF16 = jnp.bfloat16
HIGHEST = lax.Precision.HIGHEST

RMS_EPS = 1e-6
HEAD_DIM = 64
A_Q_HEADS = 8
A_KV_HEADS = 2
A_GROUP = A_Q_HEADS // A_KV_HEADS
A_WINDOW = 128
B_HEADS = 8
C_HEADS = 16
C_HEAD_DIM = 64
C_INNER = C_HEADS * C_HEAD_DIM
C_GROUPS = 2
C_HPG = C_HEADS // C_GROUPS
C_STATE = 128
C_CONV = 4
C_CHUNK = 128
C_CONV_DIM = C_INNER + 2 * C_GROUPS * C_STATE
D_HEADS = 4
D_HK = 128
D_HV = 256
D_KEY = D_HEADS * D_HK
D_VAL = D_HEADS * D_HV
D_GATE_RANK = 16
D_GATE_NORM = 16.0
D_CHUNK = 64
MOE_GROUPS = 4
MOE_EPG = 8
MOE_EXPERTS = MOE_GROUPS * MOE_EPG
MOE_TOPK = 2

LANES = 128
VMEM_LIMIT = 48 * 1024 * 1024
MOE_TILE = 512
COPY_WINDOW = 64


def _cparams(sem):
    return pltpu.CompilerParams(dimension_semantics=sem, vmem_limit_bytes=VMEM_LIMIT)


def _rms(x, g):
    ms = jnp.mean(x * x, axis=-1, keepdims=True)
    return x * lax.rsqrt(ms + RMS_EPS) * g


def _norm_proj_kernel(combine, has_aux, *refs):
    it = iter(refs)
    x_ref = next(it)
    if combine:
        m0_ref, m1_ref, gt_ref = next(it), next(it), next(it)
    g_ref, w_ref = next(it), next(it)
    wa_ref = next(it) if has_aux else None
    o_ref = next(it)
    oa_ref = next(it) if has_aux else None
    h_ref = next(it) if combine else None
    xn_ref = next(it)

    @pl.when(pl.program_id(1) == 0)
    def _():
        x = x_ref[...]
        if combine:
            x = _moe_combine(x, m0_ref, m1_ref, gt_ref)
            h_ref[...] = x
        xn = _rms(x, g_ref[...]).astype(BF16)
        xn_ref[...] = xn
        if has_aux:
            oa_ref[...] = jnp.dot(xn, wa_ref[...], preferred_element_type=F32)

    o_ref[...] = jnp.dot(xn_ref[...], w_ref[...], preferred_element_type=F32).astype(o_ref.dtype)


def _norm_proj(x, g, w, *, out_dtype, tm, tn, w_aux=None, moe=None):
    t, d = x.shape
    n = w.shape[1]
    tm = min(tm, t)
    combine = moe is not None
    has_aux = w_aux is not None
    row = lambda i, j: (i, 0)
    in_specs = [pl.BlockSpec((tm, d), row)]
    args = [x]
    if combine:
        m, gates = moe
        nc = d // LANES
        in_specs += [pl.BlockSpec((tm * nc, LANES), row),
                     pl.BlockSpec((tm * nc, LANES), lambda i, j: (t // tm + i, 0)),
                     pl.BlockSpec((tm, LANES), row)]
        args += [m, m, gates]
    in_specs += [pl.BlockSpec((1, d), lambda i, j: (0, 0)), pl.BlockSpec((d, tn), lambda i, j: (0, j))]
    args += [g.reshape(1, d), w]
    if has_aux:
        in_specs.append(pl.BlockSpec((d, LANES), lambda i, j: (0, 0)))
        args.append(w_aux)
    out_shape = [jax.ShapeDtypeStruct((t, n), out_dtype)]
    out_specs = [pl.BlockSpec((tm, tn), lambda i, j: (i, j))]
    if has_aux:
        out_shape.append(jax.ShapeDtypeStruct((t, LANES), F32))
        out_specs.append(pl.BlockSpec((tm, LANES), row))
    if combine:
        out_shape.append(jax.ShapeDtypeStruct((t, d), F32))
        out_specs.append(pl.BlockSpec((tm, d), row))
    outs = pl.pallas_call(
        functools.partial(_norm_proj_kernel, combine, has_aux),
        grid=(t // tm, n // tn),
        in_specs=in_specs, out_specs=out_specs, out_shape=out_shape,
        scratch_shapes=[pltpu.VMEM((tm, d), BF16)],
        compiler_params=_cparams(("parallel", "arbitrary")),
        name="norm_proj",
    )(*args)
    outs = list(outs)
    out = outs.pop(0)
    aux = outs.pop(0) if has_aux else None
    h = outs.pop(0) if combine else None
    return out, aux, h


def _swa_kernel(sink_ref, q_ref, kp_ref, kc_ref, vp_ref, vc_ref, o_ref):
    n = pl.program_id(1)
    blk = A_WINDOW
    row = lax.broadcasted_iota(jnp.int32, (blk, 2 * blk), 0)
    col = lax.broadcasted_iota(jnp.int32, (blk, 2 * blk), 1)
    dist = blk + row - col
    valid = (dist >= 0) & (dist < A_WINDOW) & ((col >= blk) | (n > 0))
    distf = dist.astype(F32)
    outs = []
    for kh in range(A_KV_HEADS):
        ks = slice(kh * HEAD_DIM, (kh + 1) * HEAD_DIM)
        k = jnp.concatenate([kp_ref[0, :, ks], kc_ref[0, :, ks]], axis=0)
        v = jnp.concatenate([vp_ref[0, :, ks], vc_ref[0, :, ks]], axis=0)
        for gi in range(A_GROUP):
            h = kh * A_GROUP + gi
            slope = float(2.0 ** (-8.0 * (h + 1) / A_Q_HEADS))
            q = q_ref[0, :, h * HEAD_DIM:(h + 1) * HEAD_DIM]
            s = lax.dot_general(q, k, (((1,), (1,)), ((), ())), preferred_element_type=F32)
            s = s * (HEAD_DIM ** -0.5) - slope * distf
            s = jnp.where(valid, s, -jnp.inf)
            sink = sink_ref[h]
            m = jnp.maximum(jnp.max(s, axis=-1, keepdims=True), sink)
            p = jnp.exp(s - m)
            denom = jnp.sum(p, axis=-1, keepdims=True) + jnp.exp(sink - m)
            o = jnp.dot(p.astype(BF16), v, preferred_element_type=F32)
            outs.append(o / denom)
    o_ref[0] = jnp.concatenate(outs, axis=-1).astype(o_ref.dtype)


def _swa(proj, sinks):
    b, s, _ = proj.shape
    blk = A_WINDOW
    qw = A_Q_HEADS * HEAD_DIM
    kw = A_KV_HEADS * HEAD_DIM
    q_blk = (3 * B_HEADS * HEAD_DIM) // qw
    k_blk = (3 * B_HEADS * HEAD_DIM + qw) // kw
    v_blk = k_blk + 1
    prev = lambda i, n: jnp.maximum(n - 1, 0)
    return pl.pallas_call(
        _swa_kernel,
        grid=(b, s // blk),
        in_specs=[
            pl.BlockSpec(memory_space=pltpu.SMEM),
            pl.BlockSpec((1, blk, qw), lambda i, n: (i, n, q_blk)),
            pl.BlockSpec((1, blk, kw), lambda i, n: (i, prev(i, n), k_blk)),
            pl.BlockSpec((1, blk, kw), lambda i, n: (i, n, k_blk)),
            pl.BlockSpec((1, blk, kw), lambda i, n: (i, prev(i, n), v_blk)),
            pl.BlockSpec((1, blk, kw), lambda i, n: (i, n, v_blk)),
        ],
        out_specs=pl.BlockSpec((1, blk, qw), lambda i, n: (i, n, 0)),
        out_shape=jax.ShapeDtypeStruct((b, s, qw), BF16),
        compiler_params=_cparams(("parallel", "parallel")),
        name="swa",
    )(sinks.astype(F32), proj, proj, proj, proj, proj)


def _tril(n, dtype=F32):
    r = lax.broadcasted_iota(jnp.int32, (n, n), 0)
    c = lax.broadcasted_iota(jnp.int32, (n, n), 1)
    return (c <= r).astype(dtype)


def _fox_gate_kernel(f_ref, b_ref, c_ref, ct_ref, carry_ref):
    @pl.when(pl.program_id(1) == 0)
    def _():
        carry_ref[...] = jnp.zeros_like(carry_ref)

    lf = jax.nn.log_sigmoid(f_ref[0] + b_ref[...])
    cs = jnp.dot(_tril(LANES), lf, precision=HIGHEST, preferred_element_type=F32) + carry_ref[...]
    carry_ref[...] = cs[LANES - 1:LANES, :]
    c_ref[0] = cs
    ct_ref[0, 0] = cs.T[:B_HEADS, :]


def _fox_gate(f_aux, bias):
    b, s, _ = f_aux.shape
    nb = s // LANES
    bias_p = jnp.zeros((1, LANES), F32).at[0, :B_HEADS].set(bias.astype(F32))
    return pl.pallas_call(
        _fox_gate_kernel,
        grid=(b, nb),
        in_specs=[pl.BlockSpec((1, LANES, LANES), lambda i, n: (i, n, 0)),
                  pl.BlockSpec((1, LANES), lambda i, n: (0, 0))],
        out_specs=[pl.BlockSpec((1, LANES, LANES), lambda i, n: (i, n, 0)),
                   pl.BlockSpec((1, 1, B_HEADS, LANES), lambda i, n: (i, n, 0, 0))],
        out_shape=[jax.ShapeDtypeStruct((b, s, LANES), F32),
                   jax.ShapeDtypeStruct((b, nb, B_HEADS, LANES), F32)],
        scratch_shapes=[pltpu.VMEM((1, LANES), F32)],
        compiler_params=_cparams(("parallel", "arbitrary")),
        name="fox_gate",
    )(f_aux, bias_p)


def _fox_kernel(q_ref, k_ref, v_ref, c_ref, ct_ref, o_ref, *, tq):
    qi = pl.program_id(1)
    sub = tq // LANES
    row = lax.broadcasted_iota(jnp.int32, (tq, tq), 0)
    col = lax.broadcasted_iota(jnp.int32, (tq, tq), 1)
    causal = col <= row
    outs = []
    for h in range(B_HEADS):
        hs = slice(h * HEAD_DIM, (h + 1) * HEAD_DIM)
        q = q_ref[0, :, hs] * (HEAD_DIM ** -0.5)
        ccol = c_ref[0, :, h:h + 1]

        def step(j, carry, masked, q=q, ccol=ccol, h=h, hs=hs):
            m, l, acc = carry
            start = pl.multiple_of(j * tq, tq)
            k = k_ref[0, pl.ds(start, tq), hs]
            v = v_ref[0, pl.ds(start, tq), hs]
            crow = jnp.concatenate([ct_ref[0, j * sub + u, h:h + 1, :] for u in range(sub)], axis=1)
            s = lax.dot_general(q, k, (((1,), (1,)), ((), ())), preferred_element_type=F32)
            s = s + (ccol - crow)
            if masked:
                s = jnp.where(causal, s, -jnp.inf)
            m_new = jnp.maximum(m, jnp.max(s, axis=-1, keepdims=True))
            alpha = jnp.exp(m - m_new)
            p = jnp.exp(s - m_new)
            l = alpha * l + jnp.sum(p, axis=-1, keepdims=True)
            acc = alpha * acc + jnp.dot(p.astype(BF16), v, preferred_element_type=F32)
            return m_new, l, acc

        init = (jnp.full((tq, 1), -jnp.inf, F32), jnp.zeros((tq, 1), F32), jnp.zeros((tq, HEAD_DIM), F32))
        carry = lax.fori_loop(0, qi, functools.partial(step, masked=False), init)
        _, l, acc = step(qi, carry, True)
        outs.append(acc / l)
    o_ref[0] = jnp.concatenate(outs, axis=-1).astype(o_ref.dtype)


def _fox(proj, c, ct, *, tq=256):
    b, s, _ = proj.shape
    w = B_HEADS * HEAD_DIM
    nb = s // LANES
    return pl.pallas_call(
        functools.partial(_fox_kernel, tq=tq),
        grid=(b, s // tq),
        in_specs=[
            pl.BlockSpec((1, tq, w), lambda i, n: (i, n, 0)),
            pl.BlockSpec((1, s, w), lambda i, n: (i, 0, 1)),
            pl.BlockSpec((1, s, w), lambda i, n: (i, 0, 2)),
            pl.BlockSpec((1, tq, LANES), lambda i, n: (i, n, 0)),
            pl.BlockSpec((1, nb, B_HEADS, LANES), lambda i, n: (i, 0, 0, 0)),
        ],
        out_specs=pl.BlockSpec((1, tq, w), lambda i, n: (i, n, 0)),
        out_shape=jax.ShapeDtypeStruct((b, s, w), BF16),
        compiler_params=_cparams(("parallel", "parallel")),
        name="fox",
    )(proj, proj, proj, c, ct)


def _ssd_gla_kernel(z_ref, q_ref, k_ref, v_ref, r_ref, xc_ref, xp_ref,
                    cw_ref, cb_ref, dtb_ref, alog_ref, dsk_ref, sn_ref, gkw_ref, gkb_ref, gn_ref,
                    o_ref, conv_ref, hs_ref, gs_ref):
    c = pl.program_id(1)
    q_len = C_CHUNK
    halo = 8

    @pl.when(c == 0)
    def _():
        hs_ref[...] = jnp.zeros_like(hs_ref)
        gs_ref[...] = jnp.zeros_like(gs_ref)

    prev = xp_ref[0, :, :C_CONV_DIM]
    conv_ref[0:halo, :] = jnp.where(c > 0, prev, jnp.zeros_like(prev))
    conv_ref[halo:halo + q_len, :] = xc_ref[0, :, :C_CONV_DIM]
    acc = jnp.zeros((q_len, C_CONV_DIM), F32) + cb_ref[...]
    for j in range(C_CONV):
        off = halo - (C_CONV - 1) + j
        acc = acc + cw_ref[j:j + 1, :] * conv_ref[off:off + q_len, :]
    xbc = jax.nn.silu(acc)
    xs = xbc[:, :C_INNER]
    gs_w = C_GROUPS * C_STATE
    bm = xbc[:, C_INNER:C_INNER + gs_w].astype(BF16)
    cm = xbc[:, C_INNER + gs_w:].astype(BF16)

    row = lax.broadcasted_iota(jnp.int32, (q_len, q_len), 0)
    col = lax.broadcasted_iota(jnp.int32, (q_len, q_len), 1)
    tri = col <= row
    tri_f = tri.astype(F32)

    side = xc_ref[0, :, C_CONV_DIM:]
    dt = jax.nn.softplus(side + dtb_ref[...])
    a = -jnp.exp(alog_ref[...])
    lane = lax.broadcasted_iota(jnp.int32, (1, LANES), 1)
    dta = jnp.where(lane < C_HEADS, dt * a, 0.0)
    acs = jnp.dot(tri_f, dta, precision=HIGHEST, preferred_element_type=F32)
    acs_t = acs.T
    acs_last = acs[q_len - 1:q_len, :]
    dec_end = jnp.exp(acs_last - acs)
    dec_in = jnp.exp(acs)
    chunk_dec = jnp.exp(acs_last)

    y_heads = []
    for g in range(C_GROUPS):
        b_g = bm[:, g * C_STATE:(g + 1) * C_STATE]
        c_g = cm[:, g * C_STATE:(g + 1) * C_STATE]
        cb = lax.dot_general(c_g, b_g, (((1,), (1,)), ((), ())), preferred_element_type=F32)
        h0 = g * C_HPG
        st_g = hs_ref[h0 * C_HEAD_DIM:(h0 + C_HPG) * C_HEAD_DIM, :]
        y_off = lax.dot_general(c_g, st_g.astype(BF16), (((1,), (1,)), ((), ())),
                                preferred_element_type=F32)
        xdd = []
        for hh in range(C_HPG):
            h = h0 + hh
            ps = slice(h * C_HEAD_DIM, (h + 1) * C_HEAD_DIM)
            x_h = xs[:, ps]
            xd = x_h * dt[:, h:h + 1]
            seg = jnp.exp(jnp.where(tri, acs[:, h:h + 1] - acs_t[h:h + 1, :], -jnp.inf))
            y = jnp.dot((cb * seg).astype(BF16), xd.astype(BF16), preferred_element_type=F32)
            y = y + y_off[:, hh * C_HEAD_DIM:(hh + 1) * C_HEAD_DIM] * dec_in[:, h:h + 1]
            y = y + dsk_ref[0:1, h:h + 1] * x_h
            y_heads.append(y)
            xdd.append(xd * dec_end[:, h:h + 1])
        xdd_t = jnp.concatenate(xdd, axis=1).T.astype(BF16)
        upd = jnp.dot(xdd_t, b_g, preferred_element_type=F32)
        for hh in range(C_HPG):
            h = h0 + hh
            ps = slice(h * C_HEAD_DIM, (h + 1) * C_HEAD_DIM)
            us = slice(hh * C_HEAD_DIM, (hh + 1) * C_HEAD_DIM)
            hs_ref[ps, :] = hs_ref[ps, :] * chunk_dec[0:1, h:h + 1] + upd[us, :]
    y = jnp.concatenate(y_heads, axis=1)
    y = y * jax.nn.silu(z_ref[0].astype(F32))
    o_ref[0, :, :C_INNER] = _rms(y, sn_ref[...]).astype(o_ref.dtype)

    same = (row // D_CHUNK) == (col // D_CHUNK)
    tri2 = tri & same
    la = jnp.dot(side.astype(BF16), gkw_ref[...], preferred_element_type=F32) + gkb_ref[...]
    la = jax.nn.log_sigmoid(la) / D_GATE_NORM
    gcs = jnp.dot(tri2.astype(F32), la, precision=HIGHEST, preferred_element_type=F32)
    first = lax.broadcasted_iota(jnp.int32, (q_len, 1), 0) < D_CHUNK
    r_all = r_ref[0]
    for h in range(D_HEADS):
        ks = slice(h * D_HK, (h + 1) * D_HK)
        vs = slice(h * D_HV, (h + 1) * D_HV)
        g_h = gcs[:, ks]
        g_end0 = g_h[D_CHUNK - 1:D_CHUNK, :]
        g_end1 = g_h[q_len - 1:q_len, :]
        q_h = q_ref[0, :, ks].astype(F32) * (D_HK ** -0.5)
        k_h = k_ref[0, :, ks].astype(F32)
        v_h = v_ref[0, :, vs]
        q_dec = (q_h * jnp.exp(g_h)).astype(BF16)
        k_inv = (k_h * jnp.exp(-g_h)).astype(BF16)
        k_end = k_h * jnp.exp(jnp.where(first, g_end0, g_end1) - g_h)
        ke0 = jnp.where(first, k_end, 0.0).astype(BF16)
        ke1 = jnp.where(first, 0.0, k_end).astype(BF16)
        attn = lax.dot_general(q_dec, k_inv, (((1,), (1,)), ((), ())), preferred_element_type=F32)
        attn = jnp.where(tri2, attn, 0.0).astype(BF16)
        o = jnp.dot(attn, v_h, preferred_element_type=F32)
        v_t = v_h.astype(F32).T.astype(BF16)
        st_rows = slice(h * D_HV, (h + 1) * D_HV)
        s0 = gs_ref[st_rows, :]
        s1 = s0 * jnp.exp(g_end0) + jnp.dot(v_t, ke0, preferred_element_type=F32)
        s2 = s1 * jnp.exp(g_end1) + jnp.dot(v_t, ke1, preferred_element_type=F32)
        gs_ref[st_rows, :] = s2
        tdims = (((1,), (1,)), ((), ()))
        o0 = lax.dot_general(q_dec, s0.astype(BF16), tdims, preferred_element_type=F32)
        o1 = lax.dot_general(q_dec, s1.astype(BF16), tdims, preferred_element_type=F32)
        o = o + jnp.where(first, o0, o1)
        o = _rms(o, gn_ref[...]) * jax.nn.silu(r_all[:, vs].astype(F32))
        o_ref[0, :, C_INNER + h * D_HV:C_INNER + (h + 1) * D_HV] = o.astype(o_ref.dtype)


def _ssd_gla(zqkvr, xside, p):
    b, s, _ = zqkvr.shape
    q_len = C_CHUNK
    side_w = xside.shape[-1]
    wide, narrow = C_INNER, D_KEY
    full = lambda shape: pl.BlockSpec(shape, lambda i, n: (0,) * len(shape))
    pad_lanes = lambda vec: jnp.zeros((1, LANES), F32).at[0, :vec.shape[0]].set(vec.astype(F32))
    gkw = jnp.zeros((LANES, D_KEY), F32).at[C_HEADS:C_HEADS + D_GATE_RANK].set(p["gk_w"]).astype(BF16)
    return pl.pallas_call(
        _ssd_gla_kernel,
        grid=(b, s // q_len),
        in_specs=[
            pl.BlockSpec((1, q_len, wide), lambda i, n: (i, n, 0)),
            pl.BlockSpec((1, q_len, narrow), lambda i, n: (i, n, wide // narrow)),
            pl.BlockSpec((1, q_len, narrow), lambda i, n: (i, n, wide // narrow + 1)),
            pl.BlockSpec((1, q_len, wide), lambda i, n: (i, n, 2)),
            pl.BlockSpec((1, q_len, wide), lambda i, n: (i, n, 3)),
            pl.BlockSpec((1, q_len, side_w), lambda i, n: (i, n, 0)),
            pl.BlockSpec((1, 8, side_w), lambda i, n: (i, jnp.maximum(n * (q_len // 8) - 1, 0), 0)),
            full((C_CONV, C_CONV_DIM)), full((1, C_CONV_DIM)),
            full((1, LANES)), full((1, LANES)), full((1, LANES)),
            full((1, C_INNER)), full((LANES, D_KEY)), full((1, D_KEY)), full((1, D_HV)),
        ],
        out_specs=pl.BlockSpec((1, q_len, C_INNER + D_VAL), lambda i, n: (i, n, 0)),
        out_shape=jax.ShapeDtypeStruct((b, s, C_INNER + D_VAL), BF16),
        scratch_shapes=[pltpu.VMEM((8 + q_len, C_CONV_DIM), F32),
                        pltpu.VMEM((C_INNER, C_STATE), F32),
                        pltpu.VMEM((D_VAL, D_HK), F32)],
        compiler_params=_cparams(("parallel", "arbitrary")),
        name="ssd_gla",
    )(zqkvr, zqkvr, zqkvr, zqkvr, zqkvr, xside, xside,
      p["conv_w"].astype(F32), p["conv_b"].reshape(1, -1).astype(F32),
      pad_lanes(p["dt_bias"]), pad_lanes(p["a_log"]), pad_lanes(p["d_skip"]),
      p["ssd_norm"].reshape(1, -1).astype(F32), gkw, p["gk_b"].reshape(1, -1).astype(F32),
      p["gla_norm"].reshape(1, -1).astype(F32))


def _store_rows_tiled(ref, val):
    m, d = val.shape
    nc = d // LANES
    for c in range(nc):
        ref[pl.ds(c, m, stride=nc), :] = val[:, c * LANES:(c + 1) * LANES]


def _load_rows_tiled(ref, m, dtype=None):
    nc = ref.shape[0] // m
    parts = [ref[pl.ds(c, m, stride=nc), :] for c in range(nc)]
    if dtype is not None:
        parts = [p.astype(dtype) for p in parts]
    return jnp.concatenate(parts, axis=1)


RT_GATE, RT_EXPERT, RT_RANK = 0, 2, 4


def _route_block(lg, carry):
    m = lg.shape[0]
    lane = lax.broadcasted_iota(jnp.int32, (m, LANES), 1)
    lane_f = lane.astype(F32)
    none = float(LANES)
    neg = -jnp.inf
    first_max = lambda v, vmax: jnp.min(jnp.where(v == vmax, lane_f, none), axis=-1, keepdims=True)
    gl = jnp.where(lane < MOE_GROUPS, lg, neg)
    gmax = jnp.max(gl, axis=-1, keepdims=True)
    g_w = 1.0 / jnp.sum(jnp.exp(gl - gmax), axis=-1, keepdims=True)
    lo = MOE_GROUPS + first_max(gl, gmax) * MOE_EPG
    el = jnp.where((lane_f >= lo) & (lane_f < lo + MOE_EPG), lg, neg)
    emax = jnp.max(el, axis=-1, keepdims=True)
    esum = jnp.sum(jnp.exp(el - emax), axis=-1, keepdims=True)
    l0 = first_max(el, emax)
    el2 = jnp.where(lane_f == l0, neg, el)
    emax2 = jnp.max(el2, axis=-1, keepdims=True)
    l1 = first_max(el2, emax2)
    p0 = 1.0 / esum
    p1 = jnp.exp(emax2 - emax) / esum
    w0 = g_w * (p0 / (p0 + p1))
    w1 = g_w * (p1 / (p0 + p1))
    oh0 = lane_f == l0
    oh1 = lane_f == l1
    oh = (oh0 | oh1).astype(BF16)
    r = lax.broadcasted_iota(jnp.int32, (m, m), 0)
    c = lax.broadcasted_iota(jnp.int32, (m, m), 1)
    cum = jnp.dot((c < r).astype(BF16), oh, preferred_element_type=F32) + carry
    rank0 = jnp.sum(jnp.where(oh0, cum, 0.0), axis=-1, keepdims=True)
    rank1 = jnp.sum(jnp.where(oh1, cum, 0.0), axis=-1, keepdims=True)
    carry = carry + jnp.sum(oh.astype(F32), axis=0, keepdims=True)
    rec = jnp.zeros((m, LANES), F32)
    for pos, val in ((RT_GATE, w0), (RT_GATE + 1, w1), (RT_EXPERT, l0 - MOE_GROUPS),
                     (RT_EXPERT + 1, l1 - MOE_GROUPS), (RT_RANK, rank0), (RT_RANK + 1, rank1)):
        rec = jnp.where(lane == pos, val, rec)
    return rec, carry


def _out_proj_kernel(n_parts, *refs):
    a_refs = refs[:n_parts]
    w_refs = refs[n_parts:2 * n_parts]
    h_ref, g_ref, wr_ref, br_ref, ho_ref, xt_ref, rt_ref, cnt_ref, carry_ref = refs[2 * n_parts:]

    @pl.when(pl.program_id(0) == 0)
    def _():
        carry_ref[...] = jnp.zeros_like(carry_ref)

    acc = h_ref[...]
    for a_ref, w_ref in zip(a_refs, w_refs):
        acc = acc + jnp.dot(a_ref[...], w_ref[...], preferred_element_type=F32)
    ho_ref[...] = acc
    xn = _rms(acc, g_ref[...])
    _store_rows_tiled(xt_ref, xn)
    lg = jnp.dot(xn, wr_ref[...], precision=HIGHEST, preferred_element_type=F32) + br_ref[...]
    rec, carry = _route_block(lg, carry_ref[...])
    rt_ref[...] = rec
    carry_ref[...] = carry
    cnt_ref[...] = carry


def _out_proj(parts, w_parts, h, g, w_route, b_route, *, tm=512):
    t, d = h.shape
    tm = min(tm, t)
    nc = d // LANES
    row = lambda i: (i, 0)
    const = lambda i: (0, 0)
    in_specs = [pl.BlockSpec((tm, a.shape[1]), row) for a in parts]
    in_specs += [pl.BlockSpec(w.shape, const) for w in w_parts]
    in_specs += [pl.BlockSpec((tm, d), row), pl.BlockSpec((1, d), const),
                 pl.BlockSpec((d, LANES), const), pl.BlockSpec((1, LANES), const)]
    return pl.pallas_call(
        functools.partial(_out_proj_kernel, len(parts)),
        grid=(t // tm,),
        in_specs=in_specs,
        out_specs=[pl.BlockSpec((tm, d), row), pl.BlockSpec((tm * nc, LANES), row),
                   pl.BlockSpec((tm, LANES), row), pl.BlockSpec((1, LANES), const)],
        out_shape=[jax.ShapeDtypeStruct((t, d), F32), jax.ShapeDtypeStruct((t * nc, LANES), F32),
                   jax.ShapeDtypeStruct((t, LANES), F32), jax.ShapeDtypeStruct((1, LANES), F32)],
        scratch_shapes=[pltpu.VMEM((1, LANES), F32)],
        compiler_params=_cparams(("arbitrary",)),
        name="out_proj",
    )(*parts, *w_parts, h, g.reshape(1, d), w_route, b_route)


def _moe_dispatch_kernel(e_ref, r_ref, st_ref, x_ref, init_ref, rows_ref, sem, *, tm, n_tok):
    del init_ref
    t0 = pl.program_id(0) * tm

    def copy(r, row):
        return pltpu.make_async_copy(x_ref.at[r], rows_ref.at[row], sem)

    def start(r, carry):
        for k in range(MOE_TOPK):
            a = k * n_tok + t0 + r
            copy(r, st_ref[e_ref[a]] + r_ref[a]).start()
        return carry

    def wait(r, carry):
        for _ in range(MOE_TOPK):
            copy(0, 0).wait()
        return carry

    lax.fori_loop(0, tm, start, 0, unroll=8)
    lax.fori_loop(0, tm, wait, 0, unroll=8)


def _moe_dispatch(x3, expert, rank, starts, n_rows, *, tm=512):
    t = x3.shape[0]
    tm = min(tm, t)
    any_spec = pl.BlockSpec(memory_space=pl.ANY)
    return pl.pallas_call(
        functools.partial(_moe_dispatch_kernel, tm=tm, n_tok=t),
        grid_spec=pltpu.PrefetchScalarGridSpec(
            num_scalar_prefetch=3, grid=(t // tm,),
            in_specs=[pl.BlockSpec((tm,) + x3.shape[1:], lambda i, e, r, s: (i, 0, 0)), any_spec],
            out_specs=any_spec,
            scratch_shapes=[pltpu.SemaphoreType.DMA(())]),
        out_shape=jax.ShapeDtypeStruct((n_rows,) + x3.shape[1:], x3.dtype),
        input_output_aliases={4: 0},
        compiler_params=pltpu.CompilerParams(dimension_semantics=("arbitrary",), has_side_effects=True),
        name="moe_dispatch",
    )(expert, rank, starts, x3, jnp.zeros((n_rows,) + x3.shape[1:], x3.dtype))


def _moe_gather_kernel(e_ref, r_ref, st_ref, rows_ref, o_ref, sem, *, tm):
    a0 = pl.program_id(0) * tm

    def copy(row, r):
        return pltpu.make_async_copy(rows_ref.at[row], o_ref.at[r], sem)

    def start(r, carry):
        a = a0 + r
        copy(st_ref[e_ref[a]] + r_ref[a], r).start()
        return carry

    def wait(r, carry):
        copy(0, 0).wait()
        return carry

    lax.fori_loop(0, tm, start, 0, unroll=8)
    lax.fori_loop(0, tm, wait, 0, unroll=8)


def _moe_gather(rows3, expert, rank, starts, *, tm=1024):
    n = expert.shape[0]
    tm = min(tm, n)
    return pl.pallas_call(
        functools.partial(_moe_gather_kernel, tm=tm),
        grid_spec=pltpu.PrefetchScalarGridSpec(
            num_scalar_prefetch=3, grid=(n // tm,),
            in_specs=[pl.BlockSpec(memory_space=pl.ANY)],
            out_specs=pl.BlockSpec((tm,) + rows3.shape[1:], lambda i, e, r, s: (i, 0, 0)),
            scratch_shapes=[pltpu.SemaphoreType.DMA(())]),
        out_shape=jax.ShapeDtypeStruct((n,) + rows3.shape[1:], rows3.dtype),
        compiler_params=pltpu.CompilerParams(dimension_semantics=("arbitrary",)),
        name="moe_gather",
    )(expert, rank, starts, rows3)


def _moe_kernel(te_ref, nu_ref, x_ref, wg_ref, wu_ref, wd_ref, y_ref):
    @pl.when(pl.program_id(0) < nu_ref[0])
    def _():
        x = _load_rows_tiled(x_ref, MOE_TILE, BF16)
        gate = jnp.dot(x, wg_ref[...].astype(BF16), preferred_element_type=F32)
        up = jnp.dot(x, wu_ref[...].astype(BF16), preferred_element_type=F32)
        act = (jax.nn.silu(gate) * up).astype(BF16)
        y = jnp.dot(act, wd_ref[...].astype(BF16), preferred_element_type=F32)
        _store_rows_tiled(y_ref, y)


def _moe_experts(x_rows, tile_expert, n_used, w_gate, w_up, w_down):
    d, ff = w_gate.shape[-2:]
    nc = d // LANES
    n_tiles = x_rows.shape[0] // (MOE_TILE * nc)
    live = lambda i, te, nu: jnp.minimum(i, nu[0] - 1)
    return pl.pallas_call(
        _moe_kernel,
        grid_spec=pltpu.PrefetchScalarGridSpec(
            num_scalar_prefetch=2, grid=(n_tiles,),
            in_specs=[
                pl.BlockSpec((MOE_TILE * nc, LANES), lambda i, te, nu: (live(i, te, nu), 0)),
                pl.BlockSpec((None, d, ff), lambda i, te, nu: (te[live(i, te, nu)], 0, 0)),
                pl.BlockSpec((None, d, ff), lambda i, te, nu: (te[live(i, te, nu)], 0, 0)),
                pl.BlockSpec((None, ff, d), lambda i, te, nu: (te[live(i, te, nu)], 0, 0)),
            ],
            out_specs=pl.BlockSpec((MOE_TILE * nc, LANES), lambda i, te, nu: (live(i, te, nu), 0))),
        out_shape=jax.ShapeDtypeStruct(x_rows.shape, F32),
        compiler_params=_cparams(("arbitrary",)),
        name="moe_experts",
    )(tile_expert, n_used, x_rows, w_gate, w_up, w_down)


def _moe_combine(h, m0_ref, m1_ref, rt_ref):
    rt = rt_ref[...]
    tm = h.shape[0]
    return (h + _load_rows_tiled(m0_ref, tm) * rt[:, RT_GATE:RT_GATE + 1]
            + _load_rows_tiled(m1_ref, tm) * rt[:, RT_GATE + 1:RT_GATE + 2])


def _final_norm_kernel(h_ref, m0_ref, m1_ref, rt_ref, g_ref, o_ref):
    o_ref[...] = _rms(_moe_combine(h_ref[...], m0_ref, m1_ref, rt_ref), g_ref[...])


def _final_norm(h, m, route, g, *, tm=512):
    t, d = h.shape
    tm = min(tm, t)
    nc = d // LANES
    row = lambda i: (i, 0)
    return pl.pallas_call(
        _final_norm_kernel,
        grid=(t // tm,),
        in_specs=[pl.BlockSpec((tm, d), row),
                  pl.BlockSpec((tm * nc, LANES), row),
                  pl.BlockSpec((tm * nc, LANES), lambda i: (t // tm + i, 0)),
                  pl.BlockSpec((tm, LANES), row), pl.BlockSpec((1, d), lambda i: (0, 0))],
        out_specs=pl.BlockSpec((tm, d), row),
        out_shape=jax.ShapeDtypeStruct((t, d), F32),
        compiler_params=_cparams(("parallel",)),
        name="final_norm",
    )(h, m, m, route, g.reshape(1, d))


def _moe(x_tiled, route, counts, w_gate, w_up, w_down):
    t = route.shape[0]
    nc = x_tiled.shape[0] // t
    cnt = counts[0, MOE_GROUPS:MOE_GROUPS + MOE_EXPERTS].astype(jnp.int32)
    padded = (cnt + MOE_TILE - 1) // MOE_TILE * MOE_TILE
    pad_ends = jnp.cumsum(padded)
    starts = (pad_ends - padded).astype(jnp.int32)
    n_tiles = (t * MOE_TOPK + MOE_EXPERTS * (MOE_TILE - 1)) // MOE_TILE
    tile_start = jnp.arange(n_tiles, dtype=jnp.int32) * MOE_TILE
    tile_expert = jnp.minimum(jnp.sum(tile_start[:, None] >= pad_ends[None, :], axis=1),
                              MOE_EXPERTS - 1).astype(jnp.int32)
    n_used = (pad_ends[-1] // MOE_TILE).astype(jnp.int32).reshape(1)
    n_rows = n_tiles * MOE_TILE
    k_major = lambda lane0: route[:, lane0:lane0 + MOE_TOPK].T.reshape(-1).astype(jnp.int32)
    expert, rank = k_major(RT_EXPERT), k_major(RT_RANK)
    x_rows = _moe_dispatch(x_tiled.reshape(t, nc, LANES), expert, rank, starts, n_rows)
    y_rows = _moe_experts(x_rows.reshape(n_rows * nc, LANES), tile_expert, n_used, w_gate, w_up, w_down)
    m = _moe_gather(y_rows.reshape(n_rows, nc, LANES), expert, rank, starts)
    return m.reshape(MOE_TOPK * t * nc, LANES)


def _router_weights(w_group, b_group, w_router, b_router):
    d = w_group.shape[0]
    w = jnp.zeros((d, LANES), F32)
    w = w.at[:, :MOE_GROUPS].set(w_group).at[:, MOE_GROUPS:MOE_GROUPS + MOE_EXPERTS].set(w_router)
    b = jnp.zeros((1, LANES), F32)
    b = b.at[0, :MOE_GROUPS].set(b_group).at[0, MOE_GROUPS:MOE_GROUPS + MOE_EXPERTS].set(b_router)
    return w, b


def kernel(x, norm_mix, norm_moe, norm_final, even_w_in, even_sinks, even_forget_bias, even_w_out,
           odd_w_in, odd_conv_w, odd_conv_b, odd_dt_bias, odd_a_log, odd_d_skip, odd_ssd_norm,
           odd_gk_w, odd_gk_b, odd_gla_norm, odd_w_out, moe_w_group, moe_b_group, moe_w_router,
           moe_b_router, moe_w_gate, moe_w_up, moe_w_down):
    b, s, d = x.shape
    t = b * s
    depth = norm_mix.shape[0]
    h = x.reshape(t, d)
    moe = None
    for layer in range(depth):
        i = layer // 2
        if layer % 2 == 0:
            w = even_w_in[i]
            n_a = (A_Q_HEADS + 2 * A_KV_HEADS) * HEAD_DIM
            n_ab = n_a + 3 * B_HEADS * HEAD_DIM
            w_main = jnp.concatenate([w[:, n_a:n_ab], w[:, :n_a]], axis=1).astype(BF16)
            w_aux = jnp.zeros((d, LANES), F32).at[:, :B_HEADS].set(w[:, n_ab:]).astype(BF16)
            proj, f_aux, h_new = _norm_proj(h, norm_mix[layer], w_main, out_dtype=BF16, tm=1024, tn=768,
                                            w_aux=w_aux, moe=moe)
            h = h if h_new is None else h_new
            proj = proj.reshape(b, s, -1)
            out_a = _swa(proj, even_sinks[i])
            c, ct = _fox_gate(f_aux.reshape(b, s, LANES), even_forget_bias[i])
            out_b = _fox(proj, c, ct)
            n_ha = A_Q_HEADS * HEAD_DIM
            w_out = even_w_out[i].astype(BF16)
            parts = [out_a.reshape(t, -1), out_b.reshape(t, -1)]
            w_parts = [w_out[:n_ha], w_out[n_ha:]]
        else:
            w = odd_w_in[i]
            o_z, o_xbc = 0, C_INNER
            o_dt = o_xbc + C_CONV_DIM
            o_q = o_dt + C_HEADS
            o_k = o_q + D_KEY
            o_v = o_k + D_KEY
            o_g = o_v + D_VAL
            o_r = o_g + D_GATE_RANK
            w_main = jnp.concatenate([w[:, o_z:o_xbc], w[:, o_q:o_g], w[:, o_r:]], axis=1).astype(BF16)
            w_side = jnp.concatenate([w[:, o_xbc:o_q], w[:, o_g:o_r],
                                      jnp.zeros((d, LANES - C_HEADS - D_GATE_RANK), F32)], axis=1).astype(BF16)
            zqkvr, _, h_new = _norm_proj(h, norm_mix[layer], w_main, out_dtype=BF16, tm=512, tn=1024, moe=moe)
            h = h if h_new is None else h_new
            xside, _, _ = _norm_proj(h, norm_mix[layer], w_side, out_dtype=F32, tm=512, tn=w_side.shape[1])
            params = dict(conv_w=odd_conv_w[i], conv_b=odd_conv_b[i], dt_bias=odd_dt_bias[i], a_log=odd_a_log[i],
                          d_skip=odd_d_skip[i], ssd_norm=odd_ssd_norm[i], gk_w=odd_gk_w[i], gk_b=odd_gk_b[i],
                          gla_norm=odd_gla_norm[i])
            mixed = _ssd_gla(zqkvr.reshape(b, s, -1), xside.reshape(b, s, -1), params)
            parts = [mixed.reshape(t, -1)]
            w_parts = [odd_w_out[i].astype(BF16)]
        w_route, b_route = _router_weights(moe_w_group[layer], moe_b_group[layer],
                                           moe_w_router[layer], moe_b_router[layer])
        h, x_tiled, route, counts = _out_proj(parts, w_parts, h, norm_moe[layer], w_route, b_route)
        moe = (_moe(x_tiled, route, counts, moe_w_gate[layer], moe_w_up[layer], moe_w_down[layer]), route)
    out = _final_norm(h, moe[0], moe[1], norm_final)
    return out.reshape(b, s, d)
```

```python
import functools
import math

import numpy as np
import jax
import jax.numpy as jnp
from jax import lax
from jax.experimental import pallas as pl
from jax.experimental.pallas import tpu as pltpu

F32 = jnp.float32
BF16 = jnp.bfloat16
HIGHEST = lax.Precision.HIGHEST

RMS_EPS = 1e-6
HEAD_DIM = 64
A_Q_HEADS = 8
A_KV_HEADS = 2
A_GROUP = A_Q_HEADS // A_KV_HEADS
A_WINDOW = 128
B_HEADS = 8
C_HEADS = 16
C_HEAD_DIM = 64
C_INNER = C_HEADS * C_HEAD_DIM
C_GROUPS = 2
C_HPG = C_HEADS // C_GROUPS
C_STATE = 128
C_CONV = 4
C_CHUNK = 128
C_CONV_DIM = C_INNER + 2 * C_GROUPS * C_STATE
D_HEADS = 4
D_HK = 128
D_HV = 256
D_KEY = D_HEADS * D_HK
D_VAL = D_HEADS * D_HV
D_GATE_RANK = 16
D_GATE_NORM = 16.0
D_CHUNK = 64
MOE_GROUPS = 4
MOE_EPG = 8
MOE_EXPERTS = MOE_GROUPS * MOE_EPG
MOE_TOPK = 2

LANES = 128
VMEM_LIMIT = 48 * 1024 * 1024
MOE_TILE = 512
COPY_WINDOW = 64


def _cparams(sem):
    return pltpu.CompilerParams(dimension_semantics=sem, vmem_limit_bytes=VMEM_LIMIT)


def _rms(x, g):
    ms = jnp.mean(x * x, axis=-1, keepdims=True)
    return x * lax.rsqrt(ms + RMS_EPS) * g


def _norm_proj_kernel(combine, has_aux, *refs):
    it = iter(refs)
    x_ref = next(it)
    if combine:
        m0_ref, m1_ref, gt_ref = next(it), next(it), next(it)
    g_ref, w_ref = next(it), next(it)
    wa_ref = next(it) if has_aux else None
    o_ref = next(it)
    oa_ref = next(it) if has_aux else None
    h_ref = next(it) if combine else None
    xn_ref = next(it)

    @pl.when(pl.program_id(1) == 0)
    def _():
        x = x_ref[...]
        if combine:
            x = _moe_combine(x, m0_ref, m1_ref, gt_ref)
            h_ref[...] = x
        xn = _rms(x, g_ref[...]).astype(BF16)
        xn_ref[...] = xn
        if has_aux:
            oa_ref[...] = jnp.dot(xn, wa_ref[...], preferred_element_type=F32)

    o_ref[...] = jnp.dot(xn_ref[...], w_ref[...], preferred_element_type=F32).astype(o_ref.dtype)


def _norm_proj(x, g, w, *, out_dtype, tm, tn, w_aux=None, moe=None):
    t, d = x.shape
    n = w.shape[1]
    tm = min(tm, t)
    combine = moe is not None
    has_aux = w_aux is not None
    row = lambda i, j: (i, 0)
    in_specs = [pl.BlockSpec((tm, d), row)]
    args = [x]
    if combine:
        m, gates = moe
        nc = d // LANES
        in_specs += [pl.BlockSpec((tm * nc, LANES), row),
                     pl.BlockSpec((tm * nc, LANES), lambda i, j: (t // tm + i, 0)),
                     pl.BlockSpec((tm, LANES), row)]
        args += [m, m, gates]
    in_specs += [pl.BlockSpec((1, d), lambda i, j: (0, 0)), pl.BlockSpec((d, tn), lambda i, j: (0, j))]
    args += [g.reshape(1, d), w]
    if has_aux:
        in_specs.append(pl.BlockSpec((d, LANES), lambda i, j: (0, 0)))
        args.append(w_aux)
    out_shape = [jax.ShapeDtypeStruct((t, n), out_dtype)]
    out_specs = [pl.BlockSpec((tm, tn), lambda i, j: (i, j))]
    if has_aux:
        out_shape.append(jax.ShapeDtypeStruct((t, LANES), F32))
        out_specs.append(pl.BlockSpec((tm, LANES), row))
    if combine:
        out_shape.append(jax.ShapeDtypeStruct((t, d), F32))
        out_specs.append(pl.BlockSpec((tm, d), row))
    outs = pl.pallas_call(
        functools.partial(_norm_proj_kernel, combine, has_aux),
        grid=(t // tm, n // tn),
        in_specs=in_specs, out_specs=out_specs, out_shape=out_shape,
        scratch_shapes=[pltpu.VMEM((tm, d), BF16)],
        compiler_params=_cparams(("parallel", "arbitrary")),
        name="norm_proj",
    )(*args)
    outs = list(outs)
    out = outs.pop(0)
    aux = outs.pop(0) if has_aux else None
    h = outs.pop(0) if combine else None
    return out, aux, h


def _swa_kernel(sink_ref, q_ref, kp_ref, kc_ref, vp_ref, vc_ref, o_ref):
    n = pl.program_id(1)
    blk = A_WINDOW
    row = lax.broadcasted_iota(jnp.int32, (blk, 2 * blk), 0)
    col = lax.broadcasted_iota(jnp.int32, (blk, 2 * blk), 1)
    dist = blk + row - col
    valid = (dist >= 0) & (dist < A_WINDOW) & ((col >= blk) | (n > 0))
    distf = dist.astype(F32)
    outs = []
    for kh in range(A_KV_HEADS):
        ks = slice(kh * HEAD_DIM, (kh + 1) * HEAD_DIM)
        k = jnp.concatenate([kp_ref[0, :, ks], kc_ref[0, :, ks]], axis=0)
        v = jnp.concatenate([vp_ref[0, :, ks], vc_ref[0, :, ks]], axis=0)
        for gi in range(A_GROUP):
            h = kh * A_GROUP + gi
            slope = float(2.0 ** (-8.0 * (h + 1) / A_Q_HEADS))
            q = q_ref[0, :, h * HEAD_DIM:(h + 1) * HEAD_DIM]
            s = lax.dot_general(q, k, (((1,), (1,)), ((), ())), preferred_element_type=F32)
            s = s * (HEAD_DIM ** -0.5) - slope * distf
            s = jnp.where(valid, s, -jnp.inf)
            sink = sink_ref[h]
            m = jnp.maximum(jnp.max(s, axis=-1, keepdims=True), sink)
            p = jnp.exp(s - m)
            denom = jnp.sum(p, axis=-1, keepdims=True) + jnp.exp(sink - m)
            o = jnp.dot(p.astype(BF16), v, preferred_element_type=F32)
            outs.append(o / denom)
    o_ref[0] = jnp.concatenate(outs, axis=-1).astype(o_ref.dtype)


def _swa(proj, sinks):
    b, s, _ = proj.shape
    blk = A_WINDOW
    qw = A_Q_HEADS * HEAD_DIM
    kw = A_KV_HEADS * HEAD_DIM
    q_blk = (3 * B_HEADS * HEAD_DIM) // qw
    k_blk = (3 * B_HEADS * HEAD_DIM + qw) // kw
    v_blk = k_blk + 1
    prev = lambda i, n: jnp.maximum(n - 1, 0)
    return pl.pallas_call(
        _swa_kernel,
        grid=(b, s // blk),
        in_specs=[
            pl.BlockSpec(memory_space=pltpu.SMEM),
            pl.BlockSpec((1, blk, qw), lambda i, n: (i, n, q_blk)),
            pl.BlockSpec((1, blk, kw), lambda i, n: (i, prev(i, n), k_blk)),
            pl.BlockSpec((1, blk, kw), lambda i, n: (i, n, k_blk)),
            pl.BlockSpec((1, blk, kw), lambda i, n: (i, prev(i, n), v_blk)),
            pl.BlockSpec((1, blk, kw), lambda i, n: (i, n, v_blk)),
        ],
        out_specs=pl.BlockSpec((1, blk, qw), lambda i, n: (i, n, 0)),
        out_shape=jax.ShapeDtypeStruct((b, s, qw), BF16),
        compiler_params=_cparams(("parallel", "parallel")),
        name="swa",
    )(sinks.astype(F32), proj, proj, proj, proj, proj)


def _tril(n, dtype=F32):
    r = lax.broadcasted_iota(jnp.int32, (n, n), 0)
    c = lax.broadcasted_iota(jnp.int32, (n, n), 1)
    return (c <= r).astype(dtype)


def _fox_gate_kernel(f_ref, b_ref, c_ref, ct_ref, carry_ref):
    @pl.when(pl.program_id(1) == 0)
    def _():
        carry_ref[...] = jnp.zeros_like(carry_ref)

    lf = jax.nn.log_sigmoid(f_ref[0] + b_ref[...])
    cs = jnp.dot(_tril(LANES), lf, precision=HIGHEST, preferred_element_type=F32) + carry_ref[...]
    carry_ref[...] = cs[LANES - 1:LANES, :]
    c_ref[0] = cs
    ct_ref[0, 0] = cs.T[:B_HEADS, :]


def _fox_gate(f_aux, bias):
    b, s, _ = f_aux.shape
    nb = s // LANES
    bias_p = jnp.zeros((1, LANES), F32).at[0, :B_HEADS].set(bias.astype(F32))
    return pl.pallas_call(
        _fox_gate_kernel,
        grid=(b, nb),
        in_specs=[pl.BlockSpec((1, LANES, LANES), lambda i, n: (i, n, 0)),
                  pl.BlockSpec((1, LANES), lambda i, n: (0, 0))],
        out_specs=[pl.BlockSpec((1, LANES, LANES), lambda i, n: (i, n, 0)),
                   pl.BlockSpec((1, 1, B_HEADS, LANES), lambda i, n: (i, n, 0, 0))],
        out_shape=[jax.ShapeDtypeStruct((b, s, LANES), F32),
                   jax.ShapeDtypeStruct((b, nb, B_HEADS, LANES), F32)],
        scratch_shapes=[pltpu.VMEM((1, LANES), F32)],
        compiler_params=_cparams(("parallel", "arbitrary")),
        name="fox_gate",
    )(f_aux, bias_p)


def _fox_kernel(q_ref, k_ref, vt_ref, c_ref, ctq_ref, o_ref, *, tq, heads_per_step):
    qi = pl.program_id(1)
    sub = tq // LANES
    key = lax.broadcasted_iota(jnp.int32, (tq, tq), 0)
    qry = lax.broadcasted_iota(jnp.int32, (tq, tq), 1)
    causal = key <= qry
    nt = (((1,), (1,)), ((), ()))
    outs = []
    for h0 in range(0, B_HEADS, heads_per_step):
        heads = list(range(h0, h0 + heads_per_step))
        hsl = [slice(h * HEAD_DIM, (h + 1) * HEAD_DIM) for h in heads]
        qs = [q_ref[0, :, hs] * (HEAD_DIM ** -0.5) for hs in hsl]
        cqs = [jnp.concatenate([ctq_ref[0, u, h:h + 1, :] for u in range(sub)], axis=1) for h in heads]

        def step(j, carry, masked, heads=heads, hsl=hsl, qs=qs, cqs=cqs):
            start = pl.multiple_of(j * tq, tq)
            sts = [lax.dot_general(k_ref[0, pl.ds(start, tq), hs], q, nt, preferred_element_type=F32)
                   for hs, q in zip(hsl, qs)]
            ps, stats = [], []
            for idx, h in enumerate(heads):
                m, l, _ = carry[3 * idx:3 * idx + 3]
                ck = c_ref[0, pl.ds(start, tq), h:h + 1]
                st = (sts[idx] - ck) + cqs[idx]
                if masked:
                    st = jnp.where(causal, st, -jnp.inf)
                m_new = jnp.maximum(m, jnp.max(st, axis=0, keepdims=True))
                alpha = jnp.exp(m - m_new)
                p = jnp.exp(st - m_new)
                stats.append((m_new, alpha, alpha * l + jnp.sum(p, axis=0, keepdims=True)))
                ps.append(p.astype(BF16))
            new = []
            for idx in range(len(heads)):
                m_new, alpha, l = stats[idx]
                pv = jnp.dot(vt_ref[0, j, hsl[idx], :], ps[idx], preferred_element_type=F32)
                new += [m_new, l, alpha * carry[3 * idx + 2] + pv]
            return tuple(new)

        init = (jnp.full((1, tq), -jnp.inf, F32), jnp.zeros((1, tq), F32),
                jnp.zeros((HEAD_DIM, tq), F32)) * heads_per_step
        carry = lax.fori_loop(0, qi, functools.partial(step, masked=False), init)
        carry = step(qi, carry, True)
        for idx in range(heads_per_step):
            outs.append(carry[3 * idx + 2] / carry[3 * idx + 1])
    o_ref[0] = jnp.concatenate(outs, axis=0).T.astype(o_ref.dtype)


def _fox(proj, c, ct, *, tq=256, heads_per_step=8):
    b, s, _ = proj.shape
    w = B_HEADS * HEAD_DIM
    nk = s // tq
    sub = tq // LANES
    v_t = proj[:, :, 2 * w:3 * w].reshape(b, nk, tq, w).transpose(0, 1, 3, 2)
    return pl.pallas_call(
        functools.partial(_fox_kernel, tq=tq, heads_per_step=heads_per_step),
        grid=(b, s // tq),
        in_specs=[
            pl.BlockSpec((1, tq, w), lambda i, n: (i, n, 0)),
            pl.BlockSpec((1, s, w), lambda i, n: (i, 0, 1)),
            pl.BlockSpec((1, nk, w, tq), lambda i, n: (i, 0, 0, 0)),
            pl.BlockSpec((1, s, LANES), lambda i, n: (i, 0, 0)),
            pl.BlockSpec((1, sub, B_HEADS, LANES), lambda i, n: (i, n, 0, 0)),
        ],
        out_specs=pl.BlockSpec((1, tq, w), lambda i, n: (i, n, 0)),
        out_shape=jax.ShapeDtypeStruct((b, s, w), BF16),
        compiler_params=_cparams(("parallel", "parallel")),
        name="fox",
    )(proj, proj, v_t, c, ct)


def _ssd_gla_kernel(z_ref, q_ref, k_ref, v_ref, r_ref, xc_ref, xp_ref,
                    cw_ref, cb_ref, dtb_ref, alog_ref, dsk_ref, sn_ref, gkw_ref, gkb_ref, gn_ref,
                    o_ref, conv_ref, hs_ref, gs_ref):
    c = pl.program_id(1)
    q_len = C_CHUNK
    halo = 8

    @pl.when(c == 0)
    def _():
        hs_ref[...] = jnp.zeros_like(hs_ref)
        gs_ref[...] = jnp.zeros_like(gs_ref)

    prev = xp_ref[0, :, :C_CONV_DIM]
    conv_ref[0:halo, :] = jnp.where(c > 0, prev, jnp.zeros_like(prev))
    conv_ref[halo:halo + q_len, :] = xc_ref[0, :, :C_CONV_DIM]
    acc = jnp.zeros((q_len, C_CONV_DIM), F32) + cb_ref[...]
    for j in range(C_CONV):
        off = halo - (C_CONV - 1) + j
        acc = acc + cw_ref[j:j + 1, :] * conv_ref[off:off + q_len, :]
    xbc = jax.nn.silu(acc)
    xs = xbc[:, :C_INNER]
    gs_w = C_GROUPS * C_STATE
    bm = xbc[:, C_INNER:C_INNER + gs_w].astype(BF16)
    cm = xbc[:, C_INNER + gs_w:].astype(BF16)

    row = lax.broadcasted_iota(jnp.int32, (q_len, q_len), 0)
    col = lax.broadcasted_iota(jnp.int32, (q_len, q_len), 1)
    tri = col <= row
    tri_f = tri.astype(F32)

    side = xc_ref[0, :, C_CONV_DIM:]
    dt = jax.nn.softplus(side + dtb_ref[...])
    a = -jnp.exp(alog_ref[...])
    lane = lax.broadcasted_iota(jnp.int32, (1, LANES), 1)
    dta = jnp.where(lane < C_HEADS, dt * a, 0.0)
    acs = jnp.dot(tri_f, dta, precision=HIGHEST, preferred_element_type=F32)
    acs_t = acs.T
    acs_last = acs[q_len - 1:q_len, :]
    dec_end = jnp.exp(acs_last - acs)
    dec_in = jnp.exp(acs)
    chunk_dec = jnp.exp(acs_last)

    y_heads = []
    for g in range(C_GROUPS):
        b_g = bm[:, g * C_STATE:(g + 1) * C_STATE]
        c_g = cm[:, g * C_STATE:(g + 1) * C_STATE]
        cb = lax.dot_general(c_g, b_g, (((1,), (1,)), ((), ())), preferred_element_type=F32)
        h0 = g * C_HPG
        st_g = hs_ref[h0 * C_HEAD_DIM:(h0 + C_HPG) * C_HEAD_DIM, :]
        y_off = lax.dot_general(c_g, st_g.astype(BF16), (((1,), (1,)), ((), ())),
                                preferred_element_type=F32)
        xdd = []
        for hh in range(C_HPG):
            h = h0 + hh
            ps = slice(h * C_HEAD_DIM, (h + 1) * C_HEAD_DIM)
            x_h = xs[:, ps]
            xd = x_h * dt[:, h:h + 1]
            seg = jnp.exp(jnp.where(tri, acs[:, h:h + 1] - acs_t[h:h + 1, :], -jnp.inf))
            y = jnp.dot((cb * seg).astype(BF16), xd.astype(BF16), preferred_element_type=F32)
            y = y + y_off[:, hh * C_HEAD_DIM:(hh + 1) * C_HEAD_DIM] * dec_in[:, h:h + 1]
            y = y + dsk_ref[0:1, h:h + 1] * x_h
            y_heads.append(y)
            xdd.append(xd * dec_end[:, h:h + 1])
        xdd_t = jnp.concatenate(xdd, axis=1).T.astype(BF16)
        upd = jnp.dot(xdd_t, b_g, preferred_element_type=F32)
        for hh in range(C_HPG):
            h = h0 + hh
            ps = slice(h * C_HEAD_DIM, (h + 1) * C_HEAD_DIM)
            us = slice(hh * C_HEAD_DIM, (hh + 1) * C_HEAD_DIM)
            hs_ref[ps, :] = hs_ref[ps, :] * chunk_dec[0:1, h:h + 1] + upd[us, :]
    y = jnp.concatenate(y_heads, axis=1)
    y = y * jax.nn.silu(z_ref[0].astype(F32))
    o_ref[0, :, :C_INNER] = _rms(y, sn_ref[...]).astype(o_ref.dtype)

    same = (row // D_CHUNK) == (col // D_CHUNK)
    tri2 = tri & same
    la = jnp.dot(side.astype(BF16), gkw_ref[...], preferred_element_type=F32) + gkb_ref[...]
    la = jax.nn.log_sigmoid(la) / D_GATE_NORM
    gcs = jnp.dot(tri2.astype(F32), la, precision=HIGHEST, preferred_element_type=F32)
    first = lax.broadcasted_iota(jnp.int32, (q_len, 1), 0) < D_CHUNK
    r_all = r_ref[0]
    for h in range(D_HEADS):
        ks = slice(h * D_HK, (h + 1) * D_HK)
        vs = slice(h * D_HV, (h + 1) * D_HV)
        g_h = gcs[:, ks]
        g_end0 = g_h[D_CHUNK - 1:D_CHUNK, :]
        g_end1 = g_h[q_len - 1:q_len, :]
        q_h = q_ref[0, :, ks].astype(F32) * (D_HK ** -0.5)
        k_h = k_ref[0, :, ks].astype(F32)
        v_h = v_ref[0, :, vs]
        q_dec = (q_h * jnp.exp(g_h)).astype(BF16)
        k_inv = (k_h * jnp.exp(-g_h)).astype(BF16)
        k_end = k_h * jnp.exp(jnp.where(first, g_end0, g_end1) - g_h)
        ke0 = jnp.where(first, k_end, 0.0).astype(BF16)
        ke1 = jnp.where(first, 0.0, k_end).astype(BF16)
        attn = lax.dot_general(q_dec, k_inv, (((1,), (1,)), ((), ())), preferred_element_type=F32)
        attn = jnp.where(tri2, attn, 0.0).astype(BF16)
        o = jnp.dot(attn, v_h, preferred_element_type=F32)
        v_t = v_h.astype(F32).T.astype(BF16)
        st_rows = slice(h * D_HV, (h + 1) * D_HV)
        s0 = gs_ref[st_rows, :]
        s1 = s0 * jnp.exp(g_end0) + jnp.dot(v_t, ke0, preferred_element_type=F32)
        s2 = s1 * jnp.exp(g_end1) + jnp.dot(v_t, ke1, preferred_element_type=F32)
        gs_ref[st_rows, :] = s2
        tdims = (((1,), (1,)), ((), ()))
        o0 = lax.dot_general(q_dec, s0.astype(BF16), tdims, preferred_element_type=F32)
        o1 = lax.dot_general(q_dec, s1.astype(BF16), tdims, preferred_element_type=F32)
        o = o + jnp.where(first, o0, o1)
        o = _rms(o, gn_ref[...]) * jax.nn.silu(r_all[:, vs].astype(F32))
        o_ref[0, :, C_INNER + h * D_HV:C_INNER + (h + 1) * D_HV] = o.astype(o_ref.dtype)


def _ssd_gla(zqkvr, xside, p):
    b, s, _ = zqkvr.shape
    q_len = C_CHUNK
    side_w = xside.shape[-1]
    wide, narrow = C_INNER, D_KEY
    full = lambda shape: pl.BlockSpec(shape, lambda i, n: (0,) * len(shape))
    pad_lanes = lambda vec: jnp.zeros((1, LANES), F32).at[0, :vec.shape[0]].set(vec.astype(F32))
    gkw = jnp.zeros((LANES, D_KEY), F32).at[C_HEADS:C_HEADS + D_GATE_RANK].set(p["gk_w"]).astype(BF16)
    return pl.pallas_call(
        _ssd_gla_kernel,
        grid=(b, s // q_len),
        in_specs=[
            pl.BlockSpec((1, q_len, wide), lambda i, n: (i, n, 0)),
            pl.BlockSpec((1, q_len, narrow), lambda i, n: (i, n, wide // narrow)),
            pl.BlockSpec((1, q_len, narrow), lambda i, n: (i, n, wide // narrow + 1)),
            pl.BlockSpec((1, q_len, wide), lambda i, n: (i, n, 2)),
            pl.BlockSpec((1, q_len, wide), lambda i, n: (i, n, 3)),
            pl.BlockSpec((1, q_len, side_w), lambda i, n: (i, n, 0)),
            pl.BlockSpec((1, 8, side_w), lambda i, n: (i, jnp.maximum(n * (q_len // 8) - 1, 0), 0)),
            full((C_CONV, C_CONV_DIM)), full((1, C_CONV_DIM)),
            full((1, LANES)), full((1, LANES)), full((1, LANES)),
            full((1, C_INNER)), full((LANES, D_KEY)), full((1, D_KEY)), full((1, D_HV)),
        ],
        out_specs=pl.BlockSpec((1, q_len, C_INNER + D_VAL), lambda i, n: (i, n, 0)),
        out_shape=jax.ShapeDtypeStruct((b, s, C_INNER + D_VAL), BF16),
        scratch_shapes=[pltpu.VMEM((8 + q_len, C_CONV_DIM), F32),
                        pltpu.VMEM((C_INNER, C_STATE), F32),
                        pltpu.VMEM((D_VAL, D_HK), F32)],
        compiler_params=_cparams(("parallel", "arbitrary")),
        name="ssd_gla",
    )(zqkvr, zqkvr, zqkvr, zqkvr, zqkvr, xside, xside,
      p["conv_w"].astype(F32), p["conv_b"].reshape(1, -1).astype(F32),
      pad_lanes(p["dt_bias"]), pad_lanes(p["a_log"]), pad_lanes(p["d_skip"]),
      p["ssd_norm"].reshape(1, -1).astype(F32), gkw, p["gk_b"].reshape(1, -1).astype(F32),
      p["gla_norm"].reshape(1, -1).astype(F32))


def _store_rows_tiled(ref, val):
    m, d = val.shape
    nc = d // LANES
    for c in range(nc):
        ref[pl.ds(c, m, stride=nc), :] = val[:, c * LANES:(c + 1) * LANES]


def _load_rows_tiled(ref, m, dtype=None):
    nc = ref.shape[0] // m
    parts = [ref[pl.ds(c, m, stride=nc), :] for c in range(nc)]
    if dtype is not None:
        parts = [p.astype(dtype) for p in parts]
    return jnp.concatenate(parts, axis=1)


RT_GATE, RT_EXPERT, RT_RANK = 0, 2, 4


def _route_block(lg, carry):
    m = lg.shape[0]
    lane = lax.broadcasted_iota(jnp.int32, (m, LANES), 1)
    lane_f = lane.astype(F32)
    none = float(LANES)
    neg = -jnp.inf
    first_max = lambda v, vmax: jnp.min(jnp.where(v == vmax, lane_f, none), axis=-1, keepdims=True)
    gl = jnp.where(lane < MOE_GROUPS, lg, neg)
    gmax = jnp.max(gl, axis=-1, keepdims=True)
    g_w = 1.0 / jnp.sum(jnp.exp(gl - gmax), axis=-1, keepdims=True)
    lo = MOE_GROUPS + first_max(gl, gmax) * MOE_EPG
    el = jnp.where((lane_f >= lo) & (lane_f < lo + MOE_EPG), lg, neg)
    emax = jnp.max(el, axis=-1, keepdims=True)
    esum = jnp.sum(jnp.exp(el - emax), axis=-1, keepdims=True)
    l0 = first_max(el, emax)
    el2 = jnp.where(lane_f == l0, neg, el)
    emax2 = jnp.max(el2, axis=-1, keepdims=True)
    l1 = first_max(el2, emax2)
    p0 = 1.0 / esum
    p1 = jnp.exp(emax2 - emax) / esum
    w0 = g_w * (p0 / (p0 + p1))
    w1 = g_w * (p1 / (p0 + p1))
    oh0 = lane_f == l0
    oh1 = lane_f == l1
    oh = (oh0 | oh1).astype(BF16)
    r = lax.broadcasted_iota(jnp.int32, (m, m), 0)
    c = lax.broadcasted_iota(jnp.int32, (m, m), 1)
    cum = jnp.dot((c < r).astype(BF16), oh, preferred_element_type=F32) + carry
    rank0 = jnp.sum(jnp.where(oh0, cum, 0.0), axis=-1, keepdims=True)
    rank1 = jnp.sum(jnp.where(oh1, cum, 0.0), axis=-1, keepdims=True)
    carry = carry + jnp.sum(oh.astype(F32), axis=0, keepdims=True)
    rec = jnp.zeros((m, LANES), F32)
    for pos, val in ((RT_GATE, w0), (RT_GATE + 1, w1), (RT_EXPERT, l0 - MOE_GROUPS),
                     (RT_EXPERT + 1, l1 - MOE_GROUPS), (RT_RANK, rank0), (RT_RANK + 1, rank1)):
        rec = jnp.where(lane == pos, val, rec)
    return rec, carry


def _out_proj_kernel(n_parts, *refs):
    a_refs = refs[:n_parts]
    w_refs = refs[n_parts:2 * n_parts]
    h_ref, g_ref, wr_ref, br_ref, ho_ref, xt_ref, rt_ref, cnt_ref, carry_ref = refs[2 * n_parts:]

    @pl.when(pl.program_id(0) == 0)
    def _():
        carry_ref[...] = jnp.zeros_like(carry_ref)

    acc = h_ref[...]
    for a_ref, w_ref in zip(a_refs, w_refs):
        acc = acc + jnp.dot(a_ref[...], w_ref[...], preferred_element_type=F32)
    ho_ref[...] = acc
    xn = _rms(acc, g_ref[...])
    _store_rows_tiled(xt_ref, xn)
    lg = jnp.dot(xn, wr_ref[...], precision=HIGHEST, preferred_element_type=F32) + br_ref[...]
    rec, carry = _route_block(lg, carry_ref[...])
    rt_ref[...] = rec
    carry_ref[...] = carry
    cnt_ref[...] = carry


def _out_proj(parts, w_parts, h, g, w_route, b_route, *, tm=512):
    t, d = h.shape
    tm = min(tm, t)
    nc = d // LANES
    row = lambda i: (i, 0)
    const = lambda i: (0, 0)
    in_specs = [pl.BlockSpec((tm, a.shape[1]), row) for a in parts]
    in_specs += [pl.BlockSpec(w.shape, const) for w in w_parts]
    in_specs += [pl.BlockSpec((tm, d), row), pl.BlockSpec((1, d), const),
                 pl.BlockSpec((d, LANES), const), pl.BlockSpec((1, LANES), const)]
    return pl.pallas_call(
        functools.partial(_out_proj_kernel, len(parts)),
        grid=(t // tm,),
        in_specs=in_specs,
        out_specs=[pl.BlockSpec((tm, d), row), pl.BlockSpec((tm * nc, LANES), row),
                   pl.BlockSpec((tm, LANES), row), pl.BlockSpec((1, LANES), const)],
        out_shape=[jax.ShapeDtypeStruct((t, d), F32), jax.ShapeDtypeStruct((t * nc, LANES), F32),
                   jax.ShapeDtypeStruct((t, LANES), F32), jax.ShapeDtypeStruct((1, LANES), F32)],
        scratch_shapes=[pltpu.VMEM((1, LANES), F32)],
        compiler_params=_cparams(("arbitrary",)),
        name="out_proj",
    )(*parts, *w_parts, h, g.reshape(1, d), w_route, b_route)


def _moe_dispatch_kernel(e_ref, r_ref, st_ref, x_ref, init_ref, rows_ref, sem, *, tm, n_tok):
    del init_ref
    t0 = pl.program_id(0) * tm

    def copy(r, row):
        return pltpu.make_async_copy(x_ref.at[r], rows_ref.at[row], sem)

    def start(r, carry):
        for k in range(MOE_TOPK):
            a = k * n_tok + t0 + r
            copy(r, st_ref[e_ref[a]] + r_ref[a]).start()
        return carry

    def wait(r, carry):
        for _ in range(MOE_TOPK):
            copy(0, 0).wait()
        return carry

    lax.fori_loop(0, tm, start, 0, unroll=8)
    lax.fori_loop(0, tm, wait, 0, unroll=8)


def _moe_dispatch(x3, expert, rank, starts, n_rows, *, tm=512):
    t = x3.shape[0]
    tm = min(tm, t)
    any_spec = pl.BlockSpec(memory_space=pl.ANY)
    return pl.pallas_call(
        functools.partial(_moe_dispatch_kernel, tm=tm, n_tok=t),
        grid_spec=pltpu.PrefetchScalarGridSpec(
            num_scalar_prefetch=3, grid=(t // tm,),
            in_specs=[pl.BlockSpec((tm,) + x3.shape[1:], lambda i, e, r, s: (i, 0, 0)), any_spec],
            out_specs=any_spec,
            scratch_shapes=[pltpu.SemaphoreType.DMA(())]),
        out_shape=jax.ShapeDtypeStruct((n_rows,) + x3.shape[1:], x3.dtype),
        input_output_aliases={4: 0},
        compiler_params=pltpu.CompilerParams(dimension_semantics=("arbitrary",), has_side_effects=True),
        name="moe_dispatch",
    )(expert, rank, starts, x3, jnp.zeros((n_rows,) + x3.shape[1:], x3.dtype))


def _moe_gather_kernel(e_ref, r_ref, st_ref, rows_ref, o_ref, sem, *, tm):
    a0 = pl.program_id(0) * tm

    def copy(row, r):
        return pltpu.make_async_copy(rows_ref.at[row], o_ref.at[r], sem)

    def start(r, carry):
        a = a0 + r
        copy(st_ref[e_ref[a]] + r_ref[a], r).start()
        return carry

    def wait(r, carry):
        copy(0, 0).wait()
        return carry

    lax.fori_loop(0, tm, start, 0, unroll=8)
    lax.fori_loop(0, tm, wait, 0, unroll=8)


def _moe_gather(rows3, expert, rank, starts, *, tm=1024):
    n = expert.shape[0]
    tm = min(tm, n)
    return pl.pallas_call(
        functools.partial(_moe_gather_kernel, tm=tm),
        grid_spec=pltpu.PrefetchScalarGridSpec(
            num_scalar_prefetch=3, grid=(n // tm,),
            in_specs=[pl.BlockSpec(memory_space=pl.ANY)],
            out_specs=pl.BlockSpec((tm,) + rows3.shape[1:], lambda i, e, r, s: (i, 0, 0)),
            scratch_shapes=[pltpu.SemaphoreType.DMA(())]),
        out_shape=jax.ShapeDtypeStruct((n,) + rows3.shape[1:], rows3.dtype),
        compiler_params=pltpu.CompilerParams(dimension_semantics=("arbitrary",)),
        name="moe_gather",
    )(expert, rank, starts, rows3)


def _moe_kernel(te_ref, nu_ref, x_ref, wg_ref, wu_ref, wd_ref, y_ref):
    @pl.when(pl.program_id(0) < nu_ref[0])
    def _():
        x = _load_rows_tiled(x_ref, MOE_TILE, BF16)
        gate = jnp.dot(x, wg_ref[...].astype(BF16), preferred_element_type=F32)
        up = jnp.dot(x, wu_ref[...].astype(BF16), preferred_element_type=F32)
        act = (jax.nn.silu(gate) * up).astype(BF16)
        y = jnp.dot(act, wd_ref[...].astype(BF16), preferred_element_type=F32)
        _store_rows_tiled(y_ref, y)


def _moe_experts(x_rows, tile_expert, n_used, w_gate, w_up, w_down):
    d, ff = w_gate.shape[-2:]
    nc = d // LANES
    n_tiles = x_rows.shape[0] // (MOE_TILE * nc)
    live = lambda i, te, nu: jnp.minimum(i, nu[0] - 1)
    return pl.pallas_call(
        _moe_kernel,
        grid_spec=pltpu.PrefetchScalarGridSpec(
            num_scalar_prefetch=2, grid=(n_tiles,),
            in_specs=[
                pl.BlockSpec((MOE_TILE * nc, LANES), lambda i, te, nu: (live(i, te, nu), 0)),
                pl.BlockSpec((None, d, ff), lambda i, te, nu: (te[live(i, te, nu)], 0, 0)),
                pl.BlockSpec((None, d, ff), lambda i, te, nu: (te[live(i, te, nu)], 0, 0)),
                pl.BlockSpec((None, ff, d), lambda i, te, nu: (te[live(i, te, nu)], 0, 0)),
            ],
            out_specs=pl.BlockSpec((MOE_TILE * nc, LANES), lambda i, te, nu: (live(i, te, nu), 0))),
        out_shape=jax.ShapeDtypeStruct(x_rows.shape, F32),
        compiler_params=_cparams(("arbitrary",)),
        name="moe_experts",
    )(tile_expert, n_used, x_rows, w_gate, w_up, w_down)


def _moe_combine(h, m0_ref, m1_ref, rt_ref):
    rt = rt_ref[...]
    tm = h.shape[0]
    return (h + _load_rows_tiled(m0_ref, tm) * rt[:, RT_GATE:RT_GATE + 1]
            + _load_rows_tiled(m1_ref, tm) * rt[:, RT_GATE + 1:RT_GATE + 2])


def _final_norm_kernel(h_ref, m0_ref, m1_ref, rt_ref, g_ref, o_ref):
    o_ref[...] = _rms(_moe_combine(h_ref[...], m0_ref, m1_ref, rt_ref), g_ref[...])


def _final_norm(h, m, route, g, *, tm=512):
    t, d = h.shape
    tm = min(tm, t)
    nc = d // LANES
    row = lambda i: (i, 0)
    return pl.pallas_call(
        _final_norm_kernel,
        grid=(t // tm,),
        in_specs=[pl.BlockSpec((tm, d), row),
                  pl.BlockSpec((tm * nc, LANES), row),
                  pl.BlockSpec((tm * nc, LANES), lambda i: (t // tm + i, 0)),
                  pl.BlockSpec((tm, LANES), row), pl.BlockSpec((1, d), lambda i: (0, 0))],
        out_specs=pl.BlockSpec((tm, d), row),
        out_shape=jax.ShapeDtypeStruct((t, d), F32),
        compiler_params=_cparams(("parallel",)),
        name="final_norm",
    )(h, m, m, route, g.reshape(1, d))


def _moe(x_tiled, route, counts, w_gate, w_up, w_down):
    t = route.shape[0]
    nc = x_tiled.shape[0] // t
    cnt = counts[0, MOE_GROUPS:MOE_GROUPS + MOE_EXPERTS].astype(jnp.int32)
    padded = (cnt + MOE_TILE - 1) // MOE_TILE * MOE_TILE
    pad_ends = jnp.cumsum(padded)
    starts = (pad_ends - padded).astype(jnp.int32)
    n_tiles = (t * MOE_TOPK + MOE_EXPERTS * (MOE_TILE - 1)) // MOE_TILE
    tile_start = jnp.arange(n_tiles, dtype=jnp.int32) * MOE_TILE
    tile_expert = jnp.minimum(jnp.sum(tile_start[:, None] >= pad_ends[None, :], axis=1),
                              MOE_EXPERTS - 1).astype(jnp.int32)
    n_used = (pad_ends[-1] // MOE_TILE).astype(jnp.int32).reshape(1)
    n_rows = n_tiles * MOE_TILE
    k_major = lambda lane0: route[:, lane0:lane0 + MOE_TOPK].T.reshape(-1).astype(jnp.int32)
    expert, rank = k_major(RT_EXPERT), k_major(RT_RANK)
    x_rows = _moe_dispatch(x_tiled.reshape(t, nc, LANES), expert, rank, starts, n_rows)
    y_rows = _moe_experts(x_rows.reshape(n_rows * nc, LANES), tile_expert, n_used, w_gate, w_up, w_down)
    m = _moe_gather(y_rows.reshape(n_rows, nc, LANES), expert, rank, starts)
    return m.reshape(MOE_TOPK * t * nc, LANES)


def _router_weights(w_group, b_group, w_router, b_router):
    d = w_group.shape[0]
    w = jnp.zeros((d, LANES), F32)
    w = w.at[:, :MOE_GROUPS].set(w_group).at[:, MOE_GROUPS:MOE_GROUPS + MOE_EXPERTS].set(w_router)
    b = jnp.zeros((1, LANES), F32)
    b = b.at[0, :MOE_GROUPS].set(b_group).at[0, MOE_GROUPS:MOE_GROUPS + MOE_EXPERTS].set(b_router)
    return w, b


def kernel(x, norm_mix, norm_moe, norm_final, even_w_in, even_sinks, even_forget_bias, even_w_out,
           odd_w_in, odd_conv_w, odd_conv_b, odd_dt_bias, odd_a_log, odd_d_skip, odd_ssd_norm,
           odd_gk_w, odd_gk_b, odd_gla_norm, odd_w_out, moe_w_group, moe_b_group, moe_w_router,
           moe_b_router, moe_w_gate, moe_w_up, moe_w_down):
    b, s, d = x.shape
    t = b * s
    depth = norm_mix.shape[0]
    h = x.reshape(t, d)
    moe = None
    for layer in range(depth):
        i = layer // 2
        if layer % 2 == 0:
            w = even_w_in[i]
            n_a = (A_Q_HEADS + 2 * A_KV_HEADS) * HEAD_DIM
            n_ab = n_a + 3 * B_HEADS * HEAD_DIM
            w_main = jnp.concatenate([w[:, n_a:n_ab], w[:, :n_a]], axis=1).astype(BF16)
            w_aux = jnp.zeros((d, LANES), F32).at[:, :B_HEADS].set(w[:, n_ab:]).astype(BF16)
            proj, f_aux, h_new = _norm_proj(h, norm_mix[layer], w_main, out_dtype=BF16, tm=1024, tn=768,
                                            w_aux=w_aux, moe=moe)
            h = h if h_new is None else h_new
            proj = proj.reshape(b, s, -1)
            out_a = _swa(proj, even_sinks[i])
            c, ct = _fox_gate(f_aux.reshape(b, s, LANES), even_forget_bias[i])
            out_b = _fox(proj, c, ct)
            n_ha = A_Q_HEADS * HEAD_DIM
            w_out = even_w_out[i].astype(BF16)
            parts = [out_a.reshape(t, -1), out_b.reshape(t, -1)]
            w_parts = [w_out[:n_ha], w_out[n_ha:]]
        else:
            w = odd_w_in[i]
            o_z, o_xbc = 0, C_INNER
            o_dt = o_xbc + C_CONV_DIM
            o_q = o_dt + C_HEADS
            o_k = o_q + D_KEY
            o_v = o_k + D_KEY
            o_g = o_v + D_VAL
            o_r = o_g + D_GATE_RANK
            w_main = jnp.concatenate([w[:, o_z:o_xbc], w[:, o_q:o_g], w[:, o_r:]], axis=1).astype(BF16)
            w_side = jnp.concatenate([w[:, o_xbc:o_q], w[:, o_g:o_r],
                                      jnp.zeros((d, LANES - C_HEADS - D_GATE_RANK), F32)], axis=1).astype(BF16)
            zqkvr, _, h_new = _norm_proj(h, norm_mix[layer], w_main, out_dtype=BF16, tm=512, tn=1024, moe=moe)
            h = h if h_new is None else h_new
            xside, _, _ = _norm_proj(h, norm_mix[layer], w_side, out_dtype=F32, tm=512, tn=w_side.shape[1])
            params = dict(conv_w=odd_conv_w[i], conv_b=odd_conv_b[i], dt_bias=odd_dt_bias[i], a_log=odd_a_log[i],
                          d_skip=odd_d_skip[i], ssd_norm=odd_ssd_norm[i], gk_w=odd_gk_w[i], gk_b=odd_gk_b[i],
                          gla_norm=odd_gla_norm[i])
            mixed = _ssd_gla(zqkvr.reshape(b, s, -1), xside.reshape(b, s, -1), params)
            parts = [mixed.reshape(t, -1)]
            w_parts = [odd_w_out[i].astype(BF16)]
        w_route, b_route = _router_weights(moe_w_group[layer], moe_b_group[layer],
                                           moe_w_router[layer], moe_b_router[layer])
        h, x_tiled, route, counts = _out_proj(parts, w_parts, h, norm_moe[layer], w_route, b_route)
        moe = (_moe(x_tiled, route, counts, moe_w_gate[layer], moe_w_up[layer], moe_w_down[layer]), route)
    out = _final_norm(h, moe[0], moe[1], norm_final)
    return out.reshape(b, s, d)
```

```python
import functools
import math

import numpy as np
import jax
import jax.numpy as jnp
from jax import lax
from jax.experimental import pallas as pl
from jax.experimental.pallas import tpu as pltpu

F32 = jnp.float32
BF16 = jnp.bfloat16
HIGHEST = lax.Precision.HIGHEST

RMS_EPS = 1e-6
HEAD_DIM = 64
A_Q_HEADS = 8
A_KV_HEADS = 2
A_GROUP = A_Q_HEADS // A_KV_HEADS
A_WINDOW = 128
B_HEADS = 8
C_HEADS = 16
C_HEAD_DIM = 64
C_INNER = C_HEADS * C_HEAD_DIM
C_GROUPS = 2
C_HPG = C_HEADS // C_GROUPS
C_STATE = 128
C_CONV = 4
C_CHUNK = 128
C_CONV_DIM = C_INNER + 2 * C_GROUPS * C_STATE
D_HEADS = 4
D_HK = 128
D_HV = 256
D_KEY = D_HEADS * D_HK
D_VAL = D_HEADS * D_HV
D_GATE_RANK = 16
D_GATE_NORM = 16.0
D_CHUNK = 64
MOE_GROUPS = 4
MOE_EPG = 8
MOE_EXPERTS = MOE_GROUPS * MOE_EPG
MOE_TOPK = 2

LANES = 128
VMEM_LIMIT = 48 * 1024 * 1024
MOE_TILE = 512
COPY_WINDOW = 64


def _cparams(sem):
    return pltpu.CompilerParams(dimension_semantics=sem, vmem_limit_bytes=VMEM_LIMIT)


def _rms(x, g):
    ms = jnp.mean(x * x, axis=-1, keepdims=True)
    return x * lax.rsqrt(ms + RMS_EPS) * g


def _norm_proj_kernel(combine, n_w, plan, *refs):
    it = iter(refs)
    x_ref = next(it)
    if combine:
        m0_ref, m1_ref, gt_ref = next(it), next(it), next(it)
    g_ref = next(it)
    w_refs = [next(it) for _ in range(n_w)]
    o_refs = [next(it) for _ in plan]
    h_ref = next(it) if combine else None
    res_refs = [next(it) for _ in range(n_w)]

    x = x_ref[...]
    if combine:
        x = _moe_combine(x, m0_ref, m1_ref, gt_ref)
        h_ref[...] = x
    xn = _rms(x, g_ref[...]).astype(BF16)
    for w_ref, res_ref in zip(w_refs, res_refs):
        res_ref[...] = jnp.dot(xn, w_ref[...], preferred_element_type=F32)
    for o_ref, (wi, start, width, _) in zip(o_refs, plan):
        o_ref[...] = res_refs[wi][:, start:start + width].astype(o_ref.dtype)


def _norm_proj(x, g, weights, plan, *, tm, moe=None):
    t, d = x.shape
    tm = min(tm, t)
    combine = moe is not None
    row = lambda i: (i, 0)
    const = lambda i: (0, 0)
    in_specs = [pl.BlockSpec((tm, d), row)]
    args = [x]
    if combine:
        m, gates = moe
        nc = d // LANES
        in_specs += [pl.BlockSpec((tm * nc, LANES), row),
                     pl.BlockSpec((tm * nc, LANES), lambda i: (t // tm + i, 0)),
                     pl.BlockSpec((tm, LANES), row)]
        args += [m, m, gates]
    in_specs.append(pl.BlockSpec((1, d), const))
    args.append(g.reshape(1, d))
    for w in weights:
        in_specs.append(pl.BlockSpec(w.shape, const, pipeline_mode=pl.Buffered(1)))
        args.append(w)
    out_shape = [jax.ShapeDtypeStruct((t, width), dtype) for _, _, width, dtype in plan]
    out_specs = [pl.BlockSpec((tm, width), row) for _, _, width, _ in plan]
    if combine:
        out_shape.append(jax.ShapeDtypeStruct((t, d), F32))
        out_specs.append(pl.BlockSpec((tm, d), row))
    outs = pl.pallas_call(
        functools.partial(_norm_proj_kernel, combine, len(weights), plan),
        grid=(t // tm,),
        in_specs=in_specs, out_specs=out_specs, out_shape=out_shape,
        scratch_shapes=[pltpu.VMEM((tm, w.shape[1]), F32) for w in weights],
        compiler_params=_cparams(("parallel",)),
        name="norm_proj",
    )(*args)
    outs = list(outs)
    h = outs.pop() if combine else None
    return outs, h


def _swa_kernel(sink_ref, q_ref, kp_ref, kc_ref, vp_ref, vc_ref, o_ref):
    n = pl.program_id(1)
    blk = A_WINDOW
    row = lax.broadcasted_iota(jnp.int32, (blk, 2 * blk), 0)
    col = lax.broadcasted_iota(jnp.int32, (blk, 2 * blk), 1)
    dist = blk + row - col
    valid = (dist >= 0) & (dist < A_WINDOW) & ((col >= blk) | (n > 0))
    distf = dist.astype(F32)
    outs = []
    for kh in range(A_KV_HEADS):
        ks = slice(kh * HEAD_DIM, (kh + 1) * HEAD_DIM)
        k = jnp.concatenate([kp_ref[0, :, ks], kc_ref[0, :, ks]], axis=0)
        v = jnp.concatenate([vp_ref[0, :, ks], vc_ref[0, :, ks]], axis=0)
        for gi in range(A_GROUP):
            h = kh * A_GROUP + gi
            slope = float(2.0 ** (-8.0 * (h + 1) / A_Q_HEADS))
            q = q_ref[0, :, h * HEAD_DIM:(h + 1) * HEAD_DIM]
            s = lax.dot_general(q, k, (((1,), (1,)), ((), ())), preferred_element_type=F32)
            s = s * (HEAD_DIM ** -0.5) - slope * distf
            s = jnp.where(valid, s, -jnp.inf)
            sink = sink_ref[h]
            m = jnp.maximum(jnp.max(s, axis=-1, keepdims=True), sink)
            p = jnp.exp(s - m)
            denom = jnp.sum(p, axis=-1, keepdims=True) + jnp.exp(sink - m)
            o = jnp.dot(p.astype(BF16), v, preferred_element_type=F32)
            outs.append(o / denom)
    o_ref[0] = jnp.concatenate(outs, axis=-1).astype(o_ref.dtype)


def _swa(proj, sinks):
    b, s, _ = proj.shape
    blk = A_WINDOW
    qw = A_Q_HEADS * HEAD_DIM
    kw = A_KV_HEADS * HEAD_DIM
    q_blk = 0
    k_blk = qw // kw
    v_blk = k_blk + 1
    prev = lambda i, n: jnp.maximum(n - 1, 0)
    return pl.pallas_call(
        _swa_kernel,
        grid=(b, s // blk),
        in_specs=[
            pl.BlockSpec(memory_space=pltpu.SMEM),
            pl.BlockSpec((1, blk, qw), lambda i, n: (i, n, q_blk)),
            pl.BlockSpec((1, blk, kw), lambda i, n: (i, prev(i, n), k_blk)),
            pl.BlockSpec((1, blk, kw), lambda i, n: (i, n, k_blk)),
            pl.BlockSpec((1, blk, kw), lambda i, n: (i, prev(i, n), v_blk)),
            pl.BlockSpec((1, blk, kw), lambda i, n: (i, n, v_blk)),
        ],
        out_specs=pl.BlockSpec((1, blk, qw), lambda i, n: (i, n, 0)),
        out_shape=jax.ShapeDtypeStruct((b, s, qw), BF16),
        compiler_params=_cparams(("parallel", "parallel")),
        name="swa",
    )(sinks.astype(F32), proj, proj, proj, proj, proj)


def _tril(n, dtype=F32):
    r = lax.broadcasted_iota(jnp.int32, (n, n), 0)
    c = lax.broadcasted_iota(jnp.int32, (n, n), 1)
    return (c <= r).astype(dtype)


def _fox_gate_kernel(f_ref, b_ref, c_ref, ct_ref, carry_ref):
    @pl.when(pl.program_id(1) == 0)
    def _():
        carry_ref[...] = jnp.zeros_like(carry_ref)

    lf = jax.nn.log_sigmoid(f_ref[0] + b_ref[...])
    cs = jnp.dot(_tril(LANES), lf, precision=HIGHEST, preferred_element_type=F32) + carry_ref[...]
    carry_ref[...] = cs[LANES - 1:LANES, :]
    c_ref[0] = cs
    ct_ref[0, 0] = cs.T[:B_HEADS, :]


def _fox_gate(f_aux, bias):
    b, s, _ = f_aux.shape
    nb = s // LANES
    bias_p = jnp.zeros((1, LANES), F32).at[0, :B_HEADS].set(bias.astype(F32))
    return pl.pallas_call(
        _fox_gate_kernel,
        grid=(b, nb),
        in_specs=[pl.BlockSpec((1, LANES, LANES), lambda i, n: (i, n, 0)),
                  pl.BlockSpec((1, LANES), lambda i, n: (0, 0))],
        out_specs=[pl.BlockSpec((1, LANES, LANES), lambda i, n: (i, n, 0)),
                   pl.BlockSpec((1, 1, B_HEADS, LANES), lambda i, n: (i, n, 0, 0))],
        out_shape=[jax.ShapeDtypeStruct((b, s, LANES), F32),
                   jax.ShapeDtypeStruct((b, nb, B_HEADS, LANES), F32)],
        scratch_shapes=[pltpu.VMEM((1, LANES), F32)],
        compiler_params=_cparams(("parallel", "arbitrary")),
        name="fox_gate",
    )(f_aux, bias_p)


def _fox_kernel(q0_ref, q1_ref, k0_ref, k1_ref, vt_ref, c_ref, ctq_ref, o_ref, *, tq, heads_per_step):
    qi = pl.program_id(1)
    sub = tq // LANES
    key = lax.broadcasted_iota(jnp.int32, (tq, tq), 0)
    qry = lax.broadcasted_iota(jnp.int32, (tq, tq), 1)
    causal = key <= qry
    nt = (((1,), (1,)), ((), ()))
    half = B_HEADS // 2
    q_refs, k_refs = (q0_ref, q1_ref), (k0_ref, k1_ref)
    outs = []
    for h0 in range(0, B_HEADS, heads_per_step):
        heads = list(range(h0, h0 + heads_per_step))
        hsl = [slice(h * HEAD_DIM, (h + 1) * HEAD_DIM) for h in heads]
        lsl = [slice((h % half) * HEAD_DIM, (h % half + 1) * HEAD_DIM) for h in heads]
        qs = [q_refs[h // half][0, :, ls] * (HEAD_DIM ** -0.5)
              for h, ls in zip(heads, lsl)]
        cqs = [jnp.concatenate([ctq_ref[0, u, h:h + 1, :] for u in range(sub)], axis=1) for h in heads]

        def step(j, carry, masked, heads=heads, hsl=hsl, lsl=lsl, qs=qs, cqs=cqs):
            start = pl.multiple_of(j * tq, tq)
            sts = [lax.dot_general(k_refs[h // half][0, pl.ds(start, tq), ls], q, nt,
                                   preferred_element_type=F32)
                   for h, ls, q in zip(heads, lsl, qs)]
            ps, stats = [], []
            for idx, h in enumerate(heads):
                m, l, _ = carry[3 * idx:3 * idx + 3]
                ck = c_ref[0, pl.ds(start, tq), h:h + 1]
                st = (sts[idx] - ck) + cqs[idx]
                if masked:
                    st = jnp.where(causal, st, -jnp.inf)
                m_new = jnp.maximum(m, jnp.max(st, axis=0, keepdims=True))
                alpha = jnp.exp(m - m_new)
                p = jnp.exp(st - m_new)
                stats.append((m_new, alpha, alpha * l + jnp.sum(p, axis=0, keepdims=True)))
                ps.append(p.astype(BF16))
            new = []
            for idx in range(len(heads)):
                m_new, alpha, l = stats[idx]
                pv = jnp.dot(vt_ref[0, j, hsl[idx], :], ps[idx], preferred_element_type=F32)
                new += [m_new, l, alpha * carry[3 * idx + 2] + pv]
            return tuple(new)

        init = (jnp.full((1, tq), -jnp.inf, F32), jnp.zeros((1, tq), F32),
                jnp.zeros((HEAD_DIM, tq), F32)) * heads_per_step
        carry = lax.fori_loop(0, qi, functools.partial(step, masked=False), init)
        carry = step(qi, carry, True)
        for idx in range(heads_per_step):
            outs.append(carry[3 * idx + 2] / carry[3 * idx + 1])
    o_ref[0] = jnp.concatenate(outs, axis=0).T.astype(o_ref.dtype)


def _fox(proj, c, ct, *, tq=256, heads_per_step=8):
    b, s, _ = proj.shape
    w = B_HEADS * HEAD_DIM
    nk = s // tq
    sub = tq // LANES
    hw = w // 2
    base = (A_Q_HEADS + 2 * A_KV_HEADS) * HEAD_DIM
    qb, kb = base // hw, (base + w) // hw
    v_t = proj[:, :, base + 2 * w:base + 3 * w].reshape(b, nk, tq, w).transpose(0, 1, 3, 2)
    return pl.pallas_call(
        functools.partial(_fox_kernel, tq=tq, heads_per_step=heads_per_step),
        grid=(b, s // tq),
        in_specs=[
            pl.BlockSpec((1, tq, hw), lambda i, n: (i, n, qb)),
            pl.BlockSpec((1, tq, hw), lambda i, n: (i, n, qb + 1)),
            pl.BlockSpec((1, s, hw), lambda i, n: (i, 0, kb)),
            pl.BlockSpec((1, s, hw), lambda i, n: (i, 0, kb + 1)),
            pl.BlockSpec((1, nk, w, tq), lambda i, n: (i, 0, 0, 0)),
            pl.BlockSpec((1, s, LANES), lambda i, n: (i, 0, 0)),
            pl.BlockSpec((1, sub, B_HEADS, LANES), lambda i, n: (i, n, 0, 0)),
        ],
        out_specs=pl.BlockSpec((1, tq, w), lambda i, n: (i, n, 0)),
        out_shape=jax.ShapeDtypeStruct((b, s, w), BF16),
        compiler_params=_cparams(("parallel", "parallel")),
        name="fox",
    )(proj, proj, proj, proj, v_t, c, ct)


def _ssd_gla_kernel(z_ref, q_ref, k_ref, v_ref, r_ref, xc_ref, xp_ref, sdt_ref, sg_ref,
                    cw_ref, cb_ref, dtb_ref, alog_ref, dsk_ref, sn_ref, gkw_ref, gkb_ref, gn_ref,
                    o_ref, conv_ref, hs_ref, gs_ref):
    c = pl.program_id(1)
    q_len = C_CHUNK
    halo = 8

    @pl.when(c == 0)
    def _():
        hs_ref[...] = jnp.zeros_like(hs_ref)
        gs_ref[...] = jnp.zeros_like(gs_ref)

    prev = xp_ref[0]
    conv_ref[0:halo, :] = jnp.where(c > 0, prev, jnp.zeros_like(prev))
    conv_ref[halo:halo + q_len, :] = xc_ref[0]
    acc = jnp.zeros((q_len, C_CONV_DIM), F32) + cb_ref[...]
    for j in range(C_CONV):
        off = halo - (C_CONV - 1) + j
        acc = acc + cw_ref[j:j + 1, :] * conv_ref[off:off + q_len, :]
    xbc = jax.nn.silu(acc)
    xs = xbc[:, :C_INNER]
    gs_w = C_GROUPS * C_STATE
    bm = xbc[:, C_INNER:C_INNER + gs_w].astype(BF16)
    cm = xbc[:, C_INNER + gs_w:].astype(BF16)

    row = lax.broadcasted_iota(jnp.int32, (q_len, q_len), 0)
    col = lax.broadcasted_iota(jnp.int32, (q_len, q_len), 1)
    tri = col <= row
    tri_f = tri.astype(F32)

    dt = jax.nn.softplus(sdt_ref[0] + dtb_ref[...])
    a = -jnp.exp(alog_ref[...])
    lane = lax.broadcasted_iota(jnp.int32, (1, LANES), 1)
    dta = jnp.where(lane < C_HEADS, dt * a, 0.0)
    acs = jnp.dot(tri_f, dta, precision=HIGHEST, preferred_element_type=F32)
    acs_t = acs.T
    acs_last = acs[q_len - 1:q_len, :]
    dec_end = jnp.exp(acs_last - acs)
    dec_in = jnp.exp(acs)
    chunk_dec = jnp.exp(acs_last)

    y_heads = []
    for g in range(C_GROUPS):
        b_g = bm[:, g * C_STATE:(g + 1) * C_STATE]
        c_g = cm[:, g * C_STATE:(g + 1) * C_STATE]
        cb = lax.dot_general(c_g, b_g, (((1,), (1,)), ((), ())), preferred_element_type=F32)
        h0 = g * C_HPG
        st_g = hs_ref[h0 * C_HEAD_DIM:(h0 + C_HPG) * C_HEAD_DIM, :]
        y_off = lax.dot_general(c_g, st_g.astype(BF16), (((1,), (1,)), ((), ())),
                                preferred_element_type=F32)
        xdd = []
        for hh in range(C_HPG):
            h = h0 + hh
            ps = slice(h * C_HEAD_DIM, (h + 1) * C_HEAD_DIM)
            x_h = xs[:, ps]
            xd = x_h * dt[:, h:h + 1]
            seg = jnp.exp(jnp.where(tri, acs[:, h:h + 1] - acs_t[h:h + 1, :], -jnp.inf))
            y = jnp.dot((cb * seg).astype(BF16), xd.astype(BF16), preferred_element_type=F32)
            y = y + y_off[:, hh * C_HEAD_DIM:(hh + 1) * C_HEAD_DIM] * dec_in[:, h:h + 1]
            y = y + dsk_ref[0:1, h:h + 1] * x_h
            y_heads.append(y)
            xdd.append(xd * dec_end[:, h:h + 1])
        xdd_t = jnp.concatenate(xdd, axis=1).T.astype(BF16)
        upd = jnp.dot(xdd_t, b_g, preferred_element_type=F32)
        for hh in range(C_HPG):
            h = h0 + hh
            ps = slice(h * C_HEAD_DIM, (h + 1) * C_HEAD_DIM)
            us = slice(hh * C_HEAD_DIM, (hh + 1) * C_HEAD_DIM)
            hs_ref[ps, :] = hs_ref[ps, :] * chunk_dec[0:1, h:h + 1] + upd[us, :]
    y = jnp.concatenate(y_heads, axis=1)
    y = y * jax.nn.silu(z_ref[0].astype(F32))
    o_ref[0, :, :C_INNER] = _rms(y, sn_ref[...]).astype(o_ref.dtype)

    same = (row // D_CHUNK) == (col // D_CHUNK)
    tri2 = tri & same
    la = jnp.dot(sg_ref[0].astype(BF16), gkw_ref[...], preferred_element_type=F32) + gkb_ref[...]
    la = jax.nn.log_sigmoid(la) / D_GATE_NORM
    gcs = jnp.dot(tri2.astype(F32), la, precision=HIGHEST, preferred_element_type=F32)
    first = lax.broadcasted_iota(jnp.int32, (q_len, 1), 0) < D_CHUNK
    r_all = r_ref[0]
    for h in range(D_HEADS):
        ks = slice(h * D_HK, (h + 1) * D_HK)
        vs = slice(h * D_HV, (h + 1) * D_HV)
        g_h = gcs[:, ks]
        g_end0 = g_h[D_CHUNK - 1:D_CHUNK, :]
        g_end1 = g_h[q_len - 1:q_len, :]
        q_h = q_ref[0, :, ks].astype(F32) * (D_HK ** -0.5)
        k_h = k_ref[0, :, ks].astype(F32)
        v_h = v_ref[0, :, vs]
        q_dec = (q_h * jnp.exp(g_h)).astype(BF16)
        k_inv = (k_h * jnp.exp(-g_h)).astype(BF16)
        k_end = k_h * jnp.exp(jnp.where(first, g_end0, g_end1) - g_h)
        ke0 = jnp.where(first, k_end, 0.0).astype(BF16)
        ke1 = jnp.where(first, 0.0, k_end).astype(BF16)
        attn = lax.dot_general(q_dec, k_inv, (((1,), (1,)), ((), ())), preferred_element_type=F32)
        attn = jnp.where(tri2, attn, 0.0).astype(BF16)
        o = jnp.dot(attn, v_h, preferred_element_type=F32)
        v_t = v_h.astype(F32).T.astype(BF16)
        st_rows = slice(h * D_HV, (h + 1) * D_HV)
        s0 = gs_ref[st_rows, :]
        s1 = s0 * jnp.exp(g_end0) + jnp.dot(v_t, ke0, preferred_element_type=F32)
        s2 = s1 * jnp.exp(g_end1) + jnp.dot(v_t, ke1, preferred_element_type=F32)
        gs_ref[st_rows, :] = s2
        tdims = (((1,), (1,)), ((), ()))
        o0 = lax.dot_general(q_dec, s0.astype(BF16), tdims, preferred_element_type=F32)
        o1 = lax.dot_general(q_dec, s1.astype(BF16), tdims, preferred_element_type=F32)
        o = o + jnp.where(first, o0, o1)
        o = _rms(o, gn_ref[...]) * jax.nn.silu(r_all[:, vs].astype(F32))
        o_ref[0, :, C_INNER + h * D_HV:C_INNER + (h + 1) * D_HV] = o.astype(o_ref.dtype)


def _ssd_gla(z, q, k, v, r, xbc, side_dt, side_g, g_lane, p):
    b, s, _ = z.shape
    q_len = C_CHUNK
    chunk = lambda width: pl.BlockSpec((1, q_len, width), lambda i, n: (i, n, 0))
    full = lambda shape: pl.BlockSpec(shape, lambda i, n: (0,) * len(shape))
    pad_lanes = lambda vec: jnp.zeros((1, LANES), F32).at[0, :vec.shape[0]].set(vec.astype(F32))
    gkw = jnp.zeros((LANES, D_KEY), F32).at[g_lane:g_lane + D_GATE_RANK].set(p["gk_w"]).astype(BF16)
    return pl.pallas_call(
        _ssd_gla_kernel,
        grid=(b, s // q_len),
        in_specs=[
            chunk(C_INNER), chunk(D_KEY), chunk(D_KEY), chunk(D_VAL), chunk(D_VAL), chunk(C_CONV_DIM),
            pl.BlockSpec((1, 8, C_CONV_DIM), lambda i, n: (i, jnp.maximum(n * (q_len // 8) - 1, 0), 0)),
            chunk(LANES), chunk(LANES),
            full((C_CONV, C_CONV_DIM)), full((1, C_CONV_DIM)),
            full((1, LANES)), full((1, LANES)), full((1, LANES)),
            full((1, C_INNER)), full((LANES, D_KEY)), full((1, D_KEY)), full((1, D_HV)),
        ],
        out_specs=pl.BlockSpec((1, q_len, C_INNER + D_VAL), lambda i, n: (i, n, 0)),
        out_shape=jax.ShapeDtypeStruct((b, s, C_INNER + D_VAL), BF16),
        scratch_shapes=[pltpu.VMEM((8 + q_len, C_CONV_DIM), F32),
                        pltpu.VMEM((C_INNER, C_STATE), F32),
                        pltpu.VMEM((D_VAL, D_HK), F32)],
        compiler_params=_cparams(("parallel", "arbitrary")),
        name="ssd_gla",
    )(z, q, k, v, r, xbc, xbc, side_dt, side_g,
      p["conv_w"].astype(F32), p["conv_b"].reshape(1, -1).astype(F32),
      pad_lanes(p["dt_bias"]), pad_lanes(p["a_log"]), pad_lanes(p["d_skip"]),
      p["ssd_norm"].reshape(1, -1).astype(F32), gkw, p["gk_b"].reshape(1, -1).astype(F32),
      p["gla_norm"].reshape(1, -1).astype(F32))


def _store_rows_tiled(ref, val):
    m, d = val.shape
    nc = d // LANES
    for c in range(nc):
        ref[pl.ds(c, m, stride=nc), :] = val[:, c * LANES:(c + 1) * LANES]


def _load_rows_tiled(ref, m, dtype=None):
    nc = ref.shape[0] // m
    parts = [ref[pl.ds(c, m, stride=nc), :] for c in range(nc)]
    if dtype is not None:
        parts = [p.astype(dtype) for p in parts]
    return jnp.concatenate(parts, axis=1)


RT_GATE, RT_EXPERT, RT_RANK = 0, 2, 4


def _route_block(lg, carry):
    m = lg.shape[0]
    lane = lax.broadcasted_iota(jnp.int32, (m, LANES), 1)
    lane_f = lane.astype(F32)
    none = float(LANES)
    neg = -jnp.inf
    first_max = lambda v, vmax: jnp.min(jnp.where(v == vmax, lane_f, none), axis=-1, keepdims=True)
    gl = jnp.where(lane < MOE_GROUPS, lg, neg)
    gmax = jnp.max(gl, axis=-1, keepdims=True)
    g_w = 1.0 / jnp.sum(jnp.exp(gl - gmax), axis=-1, keepdims=True)
    lo = MOE_GROUPS + first_max(gl, gmax) * MOE_EPG
    el = jnp.where((lane_f >= lo) & (lane_f < lo + MOE_EPG), lg, neg)
    emax = jnp.max(el, axis=-1, keepdims=True)
    esum = jnp.sum(jnp.exp(el - emax), axis=-1, keepdims=True)
    l0 = first_max(el, emax)
    el2 = jnp.where(lane_f == l0, neg, el)
    emax2 = jnp.max(el2, axis=-1, keepdims=True)
    l1 = first_max(el2, emax2)
    p0 = 1.0 / esum
    p1 = jnp.exp(emax2 - emax) / esum
    w0 = g_w * (p0 / (p0 + p1))
    w1 = g_w * (p1 / (p0 + p1))
    oh0 = lane_f == l0
    oh1 = lane_f == l1
    oh = (oh0 | oh1).astype(BF16)
    r = lax.broadcasted_iota(jnp.int32, (m, m), 0)
    c = lax.broadcasted_iota(jnp.int32, (m, m), 1)
    cum = jnp.dot((c < r).astype(BF16), oh, preferred_element_type=F32) + carry
    rank0 = jnp.sum(jnp.where(oh0, cum, 0.0), axis=-1, keepdims=True)
    rank1 = jnp.sum(jnp.where(oh1, cum, 0.0), axis=-1, keepdims=True)
    carry = carry + jnp.sum(oh.astype(F32), axis=0, keepdims=True)
    rec = jnp.zeros((m, LANES), F32)
    for pos, val in ((RT_GATE, w0), (RT_GATE + 1, w1), (RT_EXPERT, l0 - MOE_GROUPS),
                     (RT_EXPERT + 1, l1 - MOE_GROUPS), (RT_RANK, rank0), (RT_RANK + 1, rank1)):
        rec = jnp.where(lane == pos, val, rec)
    return rec, carry


def _out_proj_kernel(n_parts, *refs):
    a_refs = refs[:n_parts]
    w_refs = refs[n_parts:2 * n_parts]
    h_ref, g_ref, wr_ref, br_ref, ho_ref, xt_ref, rt_ref, cnt_ref, carry_ref = refs[2 * n_parts:]

    @pl.when(pl.program_id(0) == 0)
    def _():
        carry_ref[...] = jnp.zeros_like(carry_ref)

    acc = h_ref[...]
    for a_ref, w_ref in zip(a_refs, w_refs):
        acc = acc + jnp.dot(a_ref[...], w_ref[...], preferred_element_type=F32)
    ho_ref[...] = acc
    xn = _rms(acc, g_ref[...])
    _store_rows_tiled(xt_ref, xn)
    lg = jnp.dot(xn, wr_ref[...], precision=HIGHEST, preferred_element_type=F32) + br_ref[...]
    rec, carry = _route_block(lg, carry_ref[...])
    rt_ref[...] = rec
    carry_ref[...] = carry
    cnt_ref[...] = carry


def _out_proj(parts, w_parts, h, g, w_route, b_route, *, tm=512):
    t, d = h.shape
    tm = min(tm, t)
    nc = d // LANES
    row = lambda i: (i, 0)
    const = lambda i: (0, 0)
    in_specs = [pl.BlockSpec((tm, a.shape[1]), row) for a in parts]
    in_specs += [pl.BlockSpec(w.shape, const) for w in w_parts]
    in_specs += [pl.BlockSpec((tm, d), row), pl.BlockSpec((1, d), const),
                 pl.BlockSpec((d, LANES), const), pl.BlockSpec((1, LANES), const)]
    return pl.pallas_call(
        functools.partial(_out_proj_kernel, len(parts)),
        grid=(t // tm,),
        in_specs=in_specs,
        out_specs=[pl.BlockSpec((tm, d), row), pl.BlockSpec((tm * nc, LANES), row),
                   pl.BlockSpec((tm, LANES), row), pl.BlockSpec((1, LANES), const)],
        out_shape=[jax.ShapeDtypeStruct((t, d), F32), jax.ShapeDtypeStruct((t * nc, LANES), F32),
                   jax.ShapeDtypeStruct((t, LANES), F32), jax.ShapeDtypeStruct((1, LANES), F32)],
        scratch_shapes=[pltpu.VMEM((1, LANES), F32)],
        compiler_params=_cparams(("arbitrary",)),
        name="out_proj",
    )(*parts, *w_parts, h, g.reshape(1, d), w_route, b_route)


def _moe_dispatch_kernel(e_ref, r_ref, st_ref, x_ref, rows_ref, sem, *, tm, n_tok):
    t0 = pl.program_id(0) * tm

    def copy(r, row):
        return pltpu.make_async_copy(x_ref.at[r], rows_ref.at[row], sem)

    def start(r, carry):
        for k in range(MOE_TOPK):
            a = k * n_tok + t0 + r
            copy(r, st_ref[e_ref[a]] + r_ref[a]).start()
        return carry

    def wait(r, carry):
        for _ in range(MOE_TOPK):
            copy(0, 0).wait()
        return carry

    lax.fori_loop(0, tm, start, 0, unroll=8)
    lax.fori_loop(0, tm, wait, 0, unroll=8)


def _moe_dispatch(x3, expert, rank, starts, n_rows, *, tm=512):
    t = x3.shape[0]
    tm = min(tm, t)
    any_spec = pl.BlockSpec(memory_space=pl.ANY)
    return pl.pallas_call(
        functools.partial(_moe_dispatch_kernel, tm=tm, n_tok=t),
        grid_spec=pltpu.PrefetchScalarGridSpec(
            num_scalar_prefetch=3, grid=(t // tm,),
            in_specs=[pl.BlockSpec((tm,) + x3.shape[1:], lambda i, e, r, s: (i, 0, 0))],
            out_specs=any_spec,
            scratch_shapes=[pltpu.SemaphoreType.DMA(())]),
        out_shape=jax.ShapeDtypeStruct((n_rows,) + x3.shape[1:], x3.dtype),
        compiler_params=pltpu.CompilerParams(dimension_semantics=("arbitrary",), has_side_effects=True),
        name="moe_dispatch",
    )(expert, rank, starts, x3)


def _moe_gather_kernel(e_ref, r_ref, st_ref, rows_ref, o_ref, sem, *, tm):
    a0 = pl.program_id(0) * tm

    def copy(row, r):
        return pltpu.make_async_copy(rows_ref.at[row], o_ref.at[r], sem)

    def start(r, carry):
        a = a0 + r
        copy(st_ref[e_ref[a]] + r_ref[a], r).start()
        return carry

    def wait(r, carry):
        copy(0, 0).wait()
        return carry

    lax.fori_loop(0, tm, start, 0, unroll=8)
    lax.fori_loop(0, tm, wait, 0, unroll=8)


def _moe_gather(rows3, expert, rank, starts, *, tm=1024):
    n = expert.shape[0]
    tm = min(tm, n)
    return pl.pallas_call(
        functools.partial(_moe_gather_kernel, tm=tm),
        grid_spec=pltpu.PrefetchScalarGridSpec(
            num_scalar_prefetch=3, grid=(n // tm,),
            in_specs=[pl.BlockSpec(memory_space=pl.ANY)],
            out_specs=pl.BlockSpec((tm,) + rows3.shape[1:], lambda i, e, r, s: (i, 0, 0)),
            scratch_shapes=[pltpu.SemaphoreType.DMA(())]),
        out_shape=jax.ShapeDtypeStruct((n,) + rows3.shape[1:], rows3.dtype),
        compiler_params=pltpu.CompilerParams(dimension_semantics=("arbitrary",)),
        name="moe_gather",
    )(expert, rank, starts, rows3)


def _moe_kernel(te_ref, tv_ref, nu_ref, x_ref, wg_ref, wu_ref, wd_ref, y_ref):
    i = pl.program_id(0)

    @pl.when(i < nu_ref[0])
    def _():
        live_row = lax.broadcasted_iota(jnp.int32, (MOE_TILE, 1), 0) < tv_ref[i]
        x = jnp.where(live_row, _load_rows_tiled(x_ref, MOE_TILE), 0.0).astype(BF16)
        gate = jnp.dot(x, wg_ref[...].astype(BF16), preferred_element_type=F32)
        up = jnp.dot(x, wu_ref[...].astype(BF16), preferred_element_type=F32)
        act = (jax.nn.silu(gate) * up).astype(BF16)
        y = jnp.dot(act, wd_ref[...].astype(BF16), preferred_element_type=F32)
        _store_rows_tiled(y_ref, y)


def _moe_experts(x_rows, tile_expert, tile_valid, n_used, w_gate, w_up, w_down):
    d, ff = w_gate.shape[-2:]
    nc = d // LANES
    n_tiles = x_rows.shape[0] // (MOE_TILE * nc)
    live = lambda i, te, tv, nu: jnp.minimum(i, nu[0] - 1)
    return pl.pallas_call(
        _moe_kernel,
        grid_spec=pltpu.PrefetchScalarGridSpec(
            num_scalar_prefetch=3, grid=(n_tiles,),
            in_specs=[
                pl.BlockSpec((MOE_TILE * nc, LANES), lambda i, te, tv, nu: (live(i, te, tv, nu), 0)),
                pl.BlockSpec((None, d, ff), lambda i, te, tv, nu: (te[live(i, te, tv, nu)], 0, 0)),
                pl.BlockSpec((None, d, ff), lambda i, te, tv, nu: (te[live(i, te, tv, nu)], 0, 0)),
                pl.BlockSpec((None, ff, d), lambda i, te, tv, nu: (te[live(i, te, tv, nu)], 0, 0)),
            ],
            out_specs=pl.BlockSpec((MOE_TILE * nc, LANES), lambda i, te, tv, nu: (live(i, te, tv, nu), 0))),
        out_shape=jax.ShapeDtypeStruct(x_rows.shape, F32),
        compiler_params=_cparams(("arbitrary",)),
        name="moe_experts",
    )(tile_expert, tile_valid, n_used, x_rows, w_gate, w_up, w_down)


def _moe_combine(h, m0_ref, m1_ref, rt_ref):
    rt = rt_ref[...]
    tm = h.shape[0]
    return (h + _load_rows_tiled(m0_ref, tm) * rt[:, RT_GATE:RT_GATE + 1]
            + _load_rows_tiled(m1_ref, tm) * rt[:, RT_GATE + 1:RT_GATE + 2])


def _final_norm_kernel(h_ref, m0_ref, m1_ref, rt_ref, g_ref, o_ref):
    o_ref[...] = _rms(_moe_combine(h_ref[...], m0_ref, m1_ref, rt_ref), g_ref[...])


def _final_norm(h, m, route, g, *, tm=512):
    t, d = h.shape
    tm = min(tm, t)
    nc = d // LANES
    row = lambda i: (i, 0)
    return pl.pallas_call(
        _final_norm_kernel,
        grid=(t // tm,),
        in_specs=[pl.BlockSpec((tm, d), row),
                  pl.BlockSpec((tm * nc, LANES), row),
                  pl.BlockSpec((tm * nc, LANES), lambda i: (t // tm + i, 0)),
                  pl.BlockSpec((tm, LANES), row), pl.BlockSpec((1, d), lambda i: (0, 0))],
        out_specs=pl.BlockSpec((tm, d), row),
        out_shape=jax.ShapeDtypeStruct((t, d), F32),
        compiler_params=_cparams(("parallel",)),
        name="final_norm",
    )(h, m, m, route, g.reshape(1, d))


def _moe(x_tiled, route, counts, w_gate, w_up, w_down):
    t = route.shape[0]
    nc = x_tiled.shape[0] // t
    cnt = counts[0, MOE_GROUPS:MOE_GROUPS + MOE_EXPERTS].astype(jnp.int32)
    padded = (cnt + MOE_TILE - 1) // MOE_TILE * MOE_TILE
    pad_ends = jnp.cumsum(padded)
    starts = (pad_ends - padded).astype(jnp.int32)
    n_tiles = (t * MOE_TOPK + MOE_EXPERTS * (MOE_TILE - 1)) // MOE_TILE
    tile_start = jnp.arange(n_tiles, dtype=jnp.int32) * MOE_TILE
    tile_expert = jnp.minimum(jnp.sum(tile_start[:, None] >= pad_ends[None, :], axis=1),
                              MOE_EXPERTS - 1).astype(jnp.int32)
    n_used = (pad_ends[-1] // MOE_TILE).astype(jnp.int32).reshape(1)
    tile_valid = jnp.clip(cnt[tile_expert] - (tile_start - starts[tile_expert]), 0, MOE_TILE).astype(jnp.int32)
    n_rows = n_tiles * MOE_TILE
    k_major = lambda lane0: route[:, lane0:lane0 + MOE_TOPK].T.reshape(-1).astype(jnp.int32)
    expert, rank = k_major(RT_EXPERT), k_major(RT_RANK)
    x_rows = _moe_dispatch(x_tiled.reshape(t, nc, LANES), expert, rank, starts, n_rows)
    y_rows = _moe_experts(x_rows.reshape(n_rows * nc, LANES), tile_expert, tile_valid, n_used,
                          w_gate, w_up, w_down)
    m = _moe_gather(y_rows.reshape(n_rows, nc, LANES), expert, rank, starts)
    return m.reshape(MOE_TOPK * t * nc, LANES)


def _router_weights(w_group, b_group, w_router, b_router):
    d = w_group.shape[0]
    w = jnp.zeros((d, LANES), F32)
    w = w.at[:, :MOE_GROUPS].set(w_group).at[:, MOE_GROUPS:MOE_GROUPS + MOE_EXPERTS].set(w_router)
    b = jnp.zeros((1, LANES), F32)
    b = b.at[0, :MOE_GROUPS].set(b_group).at[0, MOE_GROUPS:MOE_GROUPS + MOE_EXPERTS].set(b_router)
    return w, b


def kernel(x, norm_mix, norm_moe, norm_final, even_w_in, even_sinks, even_forget_bias, even_w_out,
           odd_w_in, odd_conv_w, odd_conv_b, odd_dt_bias, odd_a_log, odd_d_skip, odd_ssd_norm,
           odd_gk_w, odd_gk_b, odd_gla_norm, odd_w_out, moe_w_group, moe_b_group, moe_w_router,
           moe_b_router, moe_w_gate, moe_w_up, moe_w_down):
    b, s, d = x.shape
    t = b * s
    depth = norm_mix.shape[0]
    h = x.reshape(t, d)
    moe = None
    for layer in range(depth):
        i = layer // 2
        if layer % 2 == 0:
            w = even_w_in[i]
            n_ab = (A_Q_HEADS + 2 * A_KV_HEADS + 3 * B_HEADS) * HEAD_DIM
            w_main = w[:, :n_ab].astype(BF16)
            w_aux = jnp.zeros((d, LANES), F32).at[:, :B_HEADS].set(w[:, n_ab:]).astype(BF16)
            (proj, f_aux), h_new = _norm_proj(h, norm_mix[layer], [w_main, w_aux],
                                              ((0, 0, n_ab, BF16), (1, 0, LANES, F32)), tm=512, moe=moe)
            h = h if h_new is None else h_new
            proj = proj.reshape(b, s, -1)
            out_a = _swa(proj, even_sinks[i])
            c, ct = _fox_gate(f_aux.reshape(b, s, LANES), even_forget_bias[i])
            out_b = _fox(proj, c, ct)
            n_ha = A_Q_HEADS * HEAD_DIM
            w_out = even_w_out[i].astype(BF16)
            parts = [out_a.reshape(t, -1), out_b.reshape(t, -1)]
            w_parts = [w_out[:n_ha], w_out[n_ha:]]
        else:
            w = odd_w_in[i]
            o_z, o_xbc = 0, C_INNER
            o_dt = o_xbc + C_CONV_DIM
            o_q = o_dt + C_HEADS
            o_k = o_q + D_KEY
            o_v = o_k + D_KEY
            o_g = o_v + D_VAL
            o_r = o_g + D_GATE_RANK
            n_b = w.shape[1] - o_dt
            w_a = w[:, :o_dt].astype(BF16)
            w_b = jnp.pad(w[:, o_dt:].astype(BF16), ((0, 0), (0, -n_b % LANES)))
            g_win = (o_g - o_dt) // LANES * LANES
            plan = ((0, o_z, C_INNER, BF16), (1, o_q - o_dt, D_KEY, BF16), (1, o_k - o_dt, D_KEY, BF16),
                    (1, o_v - o_dt, D_VAL, BF16), (1, o_r - o_dt, D_VAL, BF16), (0, o_xbc, C_CONV_DIM, F32),
                    (1, 0, LANES, F32), (1, g_win, LANES, F32))
            outs, h_new = _norm_proj(h, norm_mix[layer], [w_a, w_b], plan, tm=256, moe=moe)
            h = h if h_new is None else h_new
            params = dict(conv_w=odd_conv_w[i], conv_b=odd_conv_b[i], dt_bias=odd_dt_bias[i], a_log=odd_a_log[i],
                          d_skip=odd_d_skip[i], ssd_norm=odd_ssd_norm[i], gk_w=odd_gk_w[i], gk_b=odd_gk_b[i],
                          gla_norm=odd_gla_norm[i])
            mixed = _ssd_gla(*[o.reshape(b, s, -1) for o in outs], o_g - o_dt - g_win, params)
            parts = [mixed.reshape(t, -1)]
            w_parts = [odd_w_out[i].astype(BF16)]
        w_route, b_route = _router_weights(moe_w_group[layer], moe_b_group[layer],
                                           moe_w_router[layer], moe_b_router[layer])
        h, x_tiled, route, counts = _out_proj(parts, w_parts, h, norm_moe[layer], w_route, b_route)
        moe = (_moe(x_tiled, route, counts, moe_w_gate[layer], moe_w_up[layer], moe_w_down[layer]), route)
    out = _final_norm(h, moe[0], moe[1], norm_final)
    return out.reshape(b, s, d)
```

```python
import functools
import math

import numpy as np
import jax
import jax.numpy as jnp
from jax import lax
from jax.experimental import pallas as pl
from jax.experimental.pallas import tpu as pltpu

F32 = jnp.float32
BF16 = jnp.bfloat16
HIGHEST = lax.Precision.HIGHEST

RMS_EPS = 1e-6
HEAD_DIM = 64
A_Q_HEADS = 8
A_KV_HEADS = 2
A_GROUP = A_Q_HEADS // A_KV_HEADS
A_WINDOW = 128
B_HEADS = 8
C_HEADS = 16
C_HEAD_DIM = 64
C_INNER = C_HEADS * C_HEAD_DIM
C_GROUPS = 2
C_HPG = C_HEADS // C_GROUPS
C_STATE = 128
C_CONV = 4
C_CHUNK = 128
C_CONV_DIM = C_INNER + 2 * C_GROUPS * C_STATE
D_HEADS = 4
D_HK = 128
D_HV = 256
D_KEY = D_HEADS * D_HK
D_VAL = D_HEADS * D_HV
D_GATE_RANK = 16
D_GATE_NORM = 16.0
D_CHUNK = 64
MOE_GROUPS = 4
MOE_EPG = 8
MOE_EXPERTS = MOE_GROUPS * MOE_EPG
MOE_TOPK = 2

LANES = 128
VMEM_LIMIT = 48 * 1024 * 1024
MOE_TILE = 512
COPY_WINDOW = 64


def _cparams(sem):
    return pltpu.CompilerParams(dimension_semantics=sem, vmem_limit_bytes=VMEM_LIMIT)


def _rms(x, g):
    ms = jnp.mean(x * x, axis=-1, keepdims=True)
    return x * lax.rsqrt(ms + RMS_EPS) * g


def _norm_proj_kernel(combine, n_w, plan, *refs):
    it = iter(refs)
    x_ref = next(it)
    if combine:
        m0_ref, m1_ref, gt_ref = next(it), next(it), next(it)
    g_ref = next(it)
    w_refs = [next(it) for _ in range(n_w)]
    o_refs = [next(it) for _ in plan]
    h_ref = next(it) if combine else None
    res_refs = [next(it) for _ in range(n_w)]

    x = x_ref[...]
    if combine:
        x = _moe_combine(x, m0_ref, m1_ref, gt_ref)
        h_ref[...] = x
    xn = _rms(x, g_ref[...]).astype(BF16)
    for w_ref, res_ref in zip(w_refs, res_refs):
        res_ref[...] = jnp.dot(xn, w_ref[...], preferred_element_type=F32)
    for o_ref, (wi, start, width, _) in zip(o_refs, plan):
        o_ref[...] = res_refs[wi][:, start:start + width].astype(o_ref.dtype)


def _norm_proj(x, g, weights, plan, *, tm, moe=None):
    t, d = x.shape
    tm = min(tm, t)
    combine = moe is not None
    row = lambda i: (i, 0)
    const = lambda i: (0, 0)
    in_specs = [pl.BlockSpec((tm, d), row)]
    args = [x]
    if combine:
        m, gates = moe
        nc = d // LANES
        in_specs += [pl.BlockSpec((tm * nc, LANES), row),
                     pl.BlockSpec((tm * nc, LANES), lambda i: (t // tm + i, 0)),
                     pl.BlockSpec((tm, LANES), row)]
        args += [m, m, gates]
    in_specs.append(pl.BlockSpec((1, d), const))
    args.append(g.reshape(1, d))
    for w in weights:
        in_specs.append(pl.BlockSpec(w.shape, const, pipeline_mode=pl.Buffered(1)))
        args.append(w)
    out_shape = [jax.ShapeDtypeStruct((t, width), dtype) for _, _, width, dtype in plan]
    out_specs = [pl.BlockSpec((tm, width), row) for _, _, width, _ in plan]
    if combine:
        out_shape.append(jax.ShapeDtypeStruct((t, d), F32))
        out_specs.append(pl.BlockSpec((tm, d), row))
    outs = pl.pallas_call(
        functools.partial(_norm_proj_kernel, combine, len(weights), plan),
        grid=(t // tm,),
        in_specs=in_specs, out_specs=out_specs, out_shape=out_shape,
        scratch_shapes=[pltpu.VMEM((tm, w.shape[1]), F32) for w in weights],
        compiler_params=_cparams(("parallel",)),
        name="norm_proj",
    )(*args)
    outs = list(outs)
    h = outs.pop() if combine else None
    return outs, h


def _swa_kernel(sink_ref, q_ref, kp_ref, kc_ref, vp_ref, vc_ref, o_ref):
    n = pl.program_id(1)
    blk = A_WINDOW
    row = lax.broadcasted_iota(jnp.int32, (blk, 2 * blk), 0)
    col = lax.broadcasted_iota(jnp.int32, (blk, 2 * blk), 1)
    dist = blk + row - col
    valid = (dist >= 0) & (dist < A_WINDOW) & ((col >= blk) | (n > 0))
    distf = dist.astype(F32)
    outs = []
    for kh in range(A_KV_HEADS):
        ks = slice(kh * HEAD_DIM, (kh + 1) * HEAD_DIM)
        k = jnp.concatenate([kp_ref[0, :, ks], kc_ref[0, :, ks]], axis=0)
        v = jnp.concatenate([vp_ref[0, :, ks], vc_ref[0, :, ks]], axis=0)
        for gi in range(A_GROUP):
            h = kh * A_GROUP + gi
            slope = float(2.0 ** (-8.0 * (h + 1) / A_Q_HEADS))
            q = q_ref[0, :, h * HEAD_DIM:(h + 1) * HEAD_DIM]
            s = lax.dot_general(q, k, (((1,), (1,)), ((), ())), preferred_element_type=F32)
            s = s * (HEAD_DIM ** -0.5) - slope * distf
            s = jnp.where(valid, s, -jnp.inf)
            sink = sink_ref[h]
            m = jnp.maximum(jnp.max(s, axis=-1, keepdims=True), sink)
            p = jnp.exp(s - m)
            denom = jnp.sum(p, axis=-1, keepdims=True) + jnp.exp(sink - m)
            o = jnp.dot(p.astype(BF16), v, preferred_element_type=F32)
            outs.append(o / denom)
    o_ref[0] = jnp.concatenate(outs, axis=-1).astype(o_ref.dtype)


def _swa(proj, sinks):
    b, s, _ = proj.shape
    blk = A_WINDOW
    qw = A_Q_HEADS * HEAD_DIM
    kw = A_KV_HEADS * HEAD_DIM
    q_blk = 0
    k_blk = qw // kw
    v_blk = k_blk + 1
    prev = lambda i, n: jnp.maximum(n - 1, 0)
    return pl.pallas_call(
        _swa_kernel,
        grid=(b, s // blk),
        in_specs=[
            pl.BlockSpec(memory_space=pltpu.SMEM),
            pl.BlockSpec((1, blk, qw), lambda i, n: (i, n, q_blk)),
            pl.BlockSpec((1, blk, kw), lambda i, n: (i, prev(i, n), k_blk)),
            pl.BlockSpec((1, blk, kw), lambda i, n: (i, n, k_blk)),
            pl.BlockSpec((1, blk, kw), lambda i, n: (i, prev(i, n), v_blk)),
            pl.BlockSpec((1, blk, kw), lambda i, n: (i, n, v_blk)),
        ],
        out_specs=pl.BlockSpec((1, blk, qw), lambda i, n: (i, n, 0)),
        out_shape=jax.ShapeDtypeStruct((b, s, qw), BF16),
        compiler_params=_cparams(("parallel", "parallel")),
        name="swa",
    )(sinks.astype(F32), proj, proj, proj, proj, proj)


def _tril(n, dtype=F32):
    r = lax.broadcasted_iota(jnp.int32, (n, n), 0)
    c = lax.broadcasted_iota(jnp.int32, (n, n), 1)
    return (c <= r).astype(dtype)


def _fox_gate_kernel(f_ref, b_ref, c_ref, ct_ref, carry_ref):
    @pl.when(pl.program_id(1) == 0)
    def _():
        carry_ref[...] = jnp.zeros_like(carry_ref)

    lf = jax.nn.log_sigmoid(f_ref[0] + b_ref[...])
    cs = jnp.dot(_tril(LANES), lf, precision=HIGHEST, preferred_element_type=F32) + carry_ref[...]
    carry_ref[...] = cs[LANES - 1:LANES, :]
    c_ref[0] = cs
    ct_ref[0, 0] = cs.T[:B_HEADS, :]


def _fox_gate(f_aux, bias):
    b, s, _ = f_aux.shape
    nb = s // LANES
    bias_p = jnp.zeros((1, LANES), F32).at[0, :B_HEADS].set(bias.astype(F32))
    return pl.pallas_call(
        _fox_gate_kernel,
        grid=(b, nb),
        in_specs=[pl.BlockSpec((1, LANES, LANES), lambda i, n: (i, n, 0)),
                  pl.BlockSpec((1, LANES), lambda i, n: (0, 0))],
        out_specs=[pl.BlockSpec((1, LANES, LANES), lambda i, n: (i, n, 0)),
                   pl.BlockSpec((1, 1, B_HEADS, LANES), lambda i, n: (i, n, 0, 0))],
        out_shape=[jax.ShapeDtypeStruct((b, s, LANES), F32),
                   jax.ShapeDtypeStruct((b, nb, B_HEADS, LANES), F32)],
        scratch_shapes=[pltpu.VMEM((1, LANES), F32)],
        compiler_params=_cparams(("parallel", "arbitrary")),
        name="fox_gate",
    )(f_aux, bias_p)


def _fox_kernel(q0_ref, q1_ref, k0_ref, k1_ref, vt_ref, c_ref, ctq_ref, o_ref, *, tq, heads_per_step):
    qi = pl.program_id(1)
    sub = tq // LANES
    key = lax.broadcasted_iota(jnp.int32, (tq, tq), 0)
    qry = lax.broadcasted_iota(jnp.int32, (tq, tq), 1)
    causal = key <= qry
    nt = (((1,), (1,)), ((), ()))
    half = B_HEADS // 2
    q_refs, k_refs = (q0_ref, q1_ref), (k0_ref, k1_ref)
    outs = []
    for h0 in range(0, B_HEADS, heads_per_step):
        heads = list(range(h0, h0 + heads_per_step))
        hsl = [slice(h * HEAD_DIM, (h + 1) * HEAD_DIM) for h in heads]
        lsl = [slice((h % half) * HEAD_DIM, (h % half + 1) * HEAD_DIM) for h in heads]
        qs = [q_refs[h // half][0, :, ls] * (HEAD_DIM ** -0.5)
              for h, ls in zip(heads, lsl)]
        cqs = [jnp.concatenate([ctq_ref[0, u, h:h + 1, :] for u in range(sub)], axis=1) for h in heads]

        def step(j, carry, masked, heads=heads, hsl=hsl, lsl=lsl, qs=qs, cqs=cqs):
            start = pl.multiple_of(j * tq, tq)
            sts = [lax.dot_general(k_refs[h // half][0, pl.ds(start, tq), ls], q, nt,
                                   preferred_element_type=F32)
                   for h, ls, q in zip(heads, lsl, qs)]
            ps, stats = [], []
            for idx, h in enumerate(heads):
                m, l, _ = carry[3 * idx:3 * idx + 3]
                ck = c_ref[0, pl.ds(start, tq), h:h + 1]
                st = (sts[idx] - ck) + cqs[idx]
                if masked:
                    st = jnp.where(causal, st, -jnp.inf)
                m_new = jnp.maximum(m, jnp.max(st, axis=0, keepdims=True))
                alpha = jnp.exp(m - m_new)
                p = jnp.exp(st - m_new)
                stats.append((m_new, alpha, alpha * l + jnp.sum(p, axis=0, keepdims=True)))
                ps.append(p.astype(BF16))
            new = []
            for idx in range(len(heads)):
                m_new, alpha, l = stats[idx]
                pv = jnp.dot(vt_ref[0, j, hsl[idx], :], ps[idx], preferred_element_type=F32)
                new += [m_new, l, alpha * carry[3 * idx + 2] + pv]
            return tuple(new)

        init = (jnp.full((1, tq), -jnp.inf, F32), jnp.zeros((1, tq), F32),
                jnp.zeros((HEAD_DIM, tq), F32)) * heads_per_step
        carry = lax.fori_loop(0, qi, functools.partial(step, masked=False), init)
        carry = step(qi, carry, True)
        for idx in range(heads_per_step):
            outs.append(carry[3 * idx + 2] / carry[3 * idx + 1])
    o_ref[0] = jnp.concatenate(outs, axis=0).T.astype(o_ref.dtype)


def _fox(proj, c, ct, *, tq=256, heads_per_step=8):
    b, s, _ = proj.shape
    w = B_HEADS * HEAD_DIM
    nk = s // tq
    sub = tq // LANES
    hw = w // 2
    base = (A_Q_HEADS + 2 * A_KV_HEADS) * HEAD_DIM
    qb, kb = base // hw, (base + w) // hw
    v_t = proj[:, :, base + 2 * w:base + 3 * w].reshape(b, nk, tq, w).transpose(0, 1, 3, 2)
    return pl.pallas_call(
        functools.partial(_fox_kernel, tq=tq, heads_per_step=heads_per_step),
        grid=(b, s // tq),
        in_specs=[
            pl.BlockSpec((1, tq, hw), lambda i, n: (i, n, qb)),
            pl.BlockSpec((1, tq, hw), lambda i, n: (i, n, qb + 1)),
            pl.BlockSpec((1, s, hw), lambda i, n: (i, 0, kb)),
            pl.BlockSpec((1, s, hw), lambda i, n: (i, 0, kb + 1)),
            pl.BlockSpec((1, nk, w, tq), lambda i, n: (i, 0, 0, 0)),
            pl.BlockSpec((1, s, LANES), lambda i, n: (i, 0, 0)),
            pl.BlockSpec((1, sub, B_HEADS, LANES), lambda i, n: (i, n, 0, 0)),
        ],
        out_specs=pl.BlockSpec((1, tq, w), lambda i, n: (i, n, 0)),
        out_shape=jax.ShapeDtypeStruct((b, s, w), BF16),
        compiler_params=_cparams(("parallel", "parallel")),
        name="fox",
    )(proj, proj, proj, proj, v_t, c, ct)


def _ssd_gla_kernel(z_ref, q_ref, k_ref, v_ref, r_ref, xc_ref, xp_ref, sdt_ref, sg_ref,
                    cw_ref, cb_ref, dtb_ref, alog_ref, dsk_ref, sn_ref, gkw_ref, gkb_ref, gn_ref,
                    o_ref, conv_ref, hs_ref, gs_ref):
    c = pl.program_id(1)
    q_len = C_CHUNK
    halo = 8

    @pl.when(c == 0)
    def _():
        hs_ref[...] = jnp.zeros_like(hs_ref)
        gs_ref[...] = jnp.zeros_like(gs_ref)

    prev = xp_ref[0]
    conv_ref[0:halo, :] = jnp.where(c > 0, prev, jnp.zeros_like(prev))
    conv_ref[halo:halo + q_len, :] = xc_ref[0]
    acc = jnp.zeros((q_len, C_CONV_DIM), F32) + cb_ref[...]
    for j in range(C_CONV):
        off = halo - (C_CONV - 1) + j
        acc = acc + cw_ref[j:j + 1, :] * conv_ref[off:off + q_len, :]
    xbc = jax.nn.silu(acc)
    xs = xbc[:, :C_INNER]
    gs_w = C_GROUPS * C_STATE
    bm = xbc[:, C_INNER:C_INNER + gs_w].astype(BF16)
    cm = xbc[:, C_INNER + gs_w:].astype(BF16)

    row = lax.broadcasted_iota(jnp.int32, (q_len, q_len), 0)
    col = lax.broadcasted_iota(jnp.int32, (q_len, q_len), 1)
    tri = col <= row
    tri_f = tri.astype(F32)

    dt = jax.nn.softplus(sdt_ref[0] + dtb_ref[...])
    a = -jnp.exp(alog_ref[...])
    lane = lax.broadcasted_iota(jnp.int32, (1, LANES), 1)
    dta = jnp.where(lane < C_HEADS, dt * a, 0.0)
    acs = jnp.dot(tri_f, dta, precision=HIGHEST, preferred_element_type=F32)
    acs_t = acs.T
    acs_last = acs[q_len - 1:q_len, :]
    dec_end = jnp.exp(acs_last - acs)
    dec_in = jnp.exp(acs)
    chunk_dec = jnp.exp(acs_last)

    y_heads = []
    for g in range(C_GROUPS):
        b_g = bm[:, g * C_STATE:(g + 1) * C_STATE]
        c_g = cm[:, g * C_STATE:(g + 1) * C_STATE]
        cb = lax.dot_general(c_g, b_g, (((1,), (1,)), ((), ())), preferred_element_type=F32)
        h0 = g * C_HPG
        st_g = hs_ref[h0 * C_HEAD_DIM:(h0 + C_HPG) * C_HEAD_DIM, :]
        y_off = lax.dot_general(c_g, st_g.astype(BF16), (((1,), (1,)), ((), ())),
                                preferred_element_type=F32)
        xdd = []
        for hh in range(C_HPG):
            h = h0 + hh
            ps = slice(h * C_HEAD_DIM, (h + 1) * C_HEAD_DIM)
            x_h = xs[:, ps]
            xd = x_h * dt[:, h:h + 1]
            seg = jnp.exp(jnp.where(tri, acs[:, h:h + 1] - acs_t[h:h + 1, :], -jnp.inf))
            y = jnp.dot((cb * seg).astype(BF16), xd.astype(BF16), preferred_element_type=F32)
            y = y + y_off[:, hh * C_HEAD_DIM:(hh + 1) * C_HEAD_DIM] * dec_in[:, h:h + 1]
            y = y + dsk_ref[0:1, h:h + 1] * x_h
            y_heads.append(y)
            xdd.append(xd * dec_end[:, h:h + 1])
        xdd_t = jnp.concatenate(xdd, axis=1).T.astype(BF16)
        upd = jnp.dot(xdd_t, b_g, preferred_element_type=F32)
        for hh in range(C_HPG):
            h = h0 + hh
            ps = slice(h * C_HEAD_DIM, (h + 1) * C_HEAD_DIM)
            us = slice(hh * C_HEAD_DIM, (hh + 1) * C_HEAD_DIM)
            hs_ref[ps, :] = hs_ref[ps, :] * chunk_dec[0:1, h:h + 1] + upd[us, :]
    y = jnp.concatenate(y_heads, axis=1)
    y = y * jax.nn.silu(z_ref[0].astype(F32))
    o_ref[0, :, :C_INNER] = _rms(y, sn_ref[...]).astype(o_ref.dtype)

    same = (row // D_CHUNK) == (col // D_CHUNK)
    tri2 = tri & same
    la = jnp.dot(sg_ref[0].astype(BF16), gkw_ref[...], preferred_element_type=F32) + gkb_ref[...]
    la = jax.nn.log_sigmoid(la) / D_GATE_NORM
    gcs = jnp.dot(tri2.astype(F32), la, precision=HIGHEST, preferred_element_type=F32)
    first = lax.broadcasted_iota(jnp.int32, (q_len, 1), 0) < D_CHUNK
    r_all = r_ref[0]
    for h in range(D_HEADS):
        ks = slice(h * D_HK, (h + 1) * D_HK)
        vs = slice(h * D_HV, (h + 1) * D_HV)
        g_h = gcs[:, ks]
        g_end0 = g_h[D_CHUNK - 1:D_CHUNK, :]
        g_end1 = g_h[q_len - 1:q_len, :]
        q_h = q_ref[0, :, ks].astype(F32) * (D_HK ** -0.5)
        k_h = k_ref[0, :, ks].astype(F32)
        v_h = v_ref[0, :, vs]
        q_dec = (q_h * jnp.exp(g_h)).astype(BF16)
        k_inv = (k_h * jnp.exp(-g_h)).astype(BF16)
        k_end = k_h * jnp.exp(jnp.where(first, g_end0, g_end1) - g_h)
        ke0 = jnp.where(first, k_end, 0.0).astype(BF16)
        ke1 = jnp.where(first, 0.0, k_end).astype(BF16)
        attn = lax.dot_general(q_dec, k_inv, (((1,), (1,)), ((), ())), preferred_element_type=F32)
        attn = jnp.where(tri2, attn, 0.0).astype(BF16)
        o = jnp.dot(attn, v_h, preferred_element_type=F32)
        v_t = v_h.astype(F32).T.astype(BF16)
        st_rows = slice(h * D_HV, (h + 1) * D_HV)
        s0 = gs_ref[st_rows, :]
        s1 = s0 * jnp.exp(g_end0) + jnp.dot(v_t, ke0, preferred_element_type=F32)
        s2 = s1 * jnp.exp(g_end1) + jnp.dot(v_t, ke1, preferred_element_type=F32)
        gs_ref[st_rows, :] = s2
        tdims = (((1,), (1,)), ((), ()))
        o0 = lax.dot_general(q_dec, s0.astype(BF16), tdims, preferred_element_type=F32)
        o1 = lax.dot_general(q_dec, s1.astype(BF16), tdims, preferred_element_type=F32)
        o = o + jnp.where(first, o0, o1)
        o = _rms(o, gn_ref[...]) * jax.nn.silu(r_all[:, vs].astype(F32))
        o_ref[0, :, C_INNER + h * D_HV:C_INNER + (h + 1) * D_HV] = o.astype(o_ref.dtype)


def _ssd_gla(z, q, k, v, r, xbc, side_dt, side_g, g_lane, p):
    b, s, _ = z.shape
    q_len = C_CHUNK
    chunk = lambda width: pl.BlockSpec((1, q_len, width), lambda i, n: (i, n, 0))
    full = lambda shape: pl.BlockSpec(shape, lambda i, n: (0,) * len(shape))
    pad_lanes = lambda vec: jnp.zeros((1, LANES), F32).at[0, :vec.shape[0]].set(vec.astype(F32))
    gkw = jnp.zeros((LANES, D_KEY), F32).at[g_lane:g_lane + D_GATE_RANK].set(p["gk_w"]).astype(BF16)
    return pl.pallas_call(
        _ssd_gla_kernel,
        grid=(b, s // q_len),
        in_specs=[
            chunk(C_INNER), chunk(D_KEY), chunk(D_KEY), chunk(D_VAL), chunk(D_VAL), chunk(C_CONV_DIM),
            pl.BlockSpec((1, 8, C_CONV_DIM), lambda i, n: (i, jnp.maximum(n * (q_len // 8) - 1, 0), 0)),
            chunk(LANES), chunk(LANES),
            full((C_CONV, C_CONV_DIM)), full((1, C_CONV_DIM)),
            full((1, LANES)), full((1, LANES)), full((1, LANES)),
            full((1, C_INNER)), full((LANES, D_KEY)), full((1, D_KEY)), full((1, D_HV)),
        ],
        out_specs=pl.BlockSpec((1, q_len, C_INNER + D_VAL), lambda i, n: (i, n, 0)),
        out_shape=jax.ShapeDtypeStruct((b, s, C_INNER + D_VAL), BF16),
        scratch_shapes=[pltpu.VMEM((8 + q_len, C_CONV_DIM), F32),
                        pltpu.VMEM((C_INNER, C_STATE), F32),
                        pltpu.VMEM((D_VAL, D_HK), F32)],
        compiler_params=_cparams(("parallel", "arbitrary")),
        name="ssd_gla",
    )(z, q, k, v, r, xbc, xbc, side_dt, side_g,
      p["conv_w"].astype(F32), p["conv_b"].reshape(1, -1).astype(F32),
      pad_lanes(p["dt_bias"]), pad_lanes(p["a_log"]), pad_lanes(p["d_skip"]),
      p["ssd_norm"].reshape(1, -1).astype(F32), gkw, p["gk_b"].reshape(1, -1).astype(F32),
      p["gla_norm"].reshape(1, -1).astype(F32))


def _store_rows_tiled(ref, val):
    m, d = val.shape
    nc = d // LANES
    for c in range(nc):
        ref[pl.ds(c, m, stride=nc), :] = val[:, c * LANES:(c + 1) * LANES]


def _load_rows_tiled(ref, m, dtype=None):
    nc = ref.shape[0] // m
    parts = [ref[pl.ds(c, m, stride=nc), :] for c in range(nc)]
    if dtype is not None:
        parts = [p.astype(dtype) for p in parts]
    return jnp.concatenate(parts, axis=1)


RT_GATE, RT_EXPERT, RT_RANK = 0, 2, 4


def _route_block(lg, carry):
    m = lg.shape[0]
    lane = lax.broadcasted_iota(jnp.int32, (m, LANES), 1)
    lane_f = lane.astype(F32)
    none = float(LANES)
    neg = -jnp.inf
    first_max = lambda v, vmax: jnp.min(jnp.where(v == vmax, lane_f, none), axis=-1, keepdims=True)
    gl = jnp.where(lane < MOE_GROUPS, lg, neg)
    gmax = jnp.max(gl, axis=-1, keepdims=True)
    g_w = 1.0 / jnp.sum(jnp.exp(gl - gmax), axis=-1, keepdims=True)
    lo = MOE_GROUPS + first_max(gl, gmax) * MOE_EPG
    el = jnp.where((lane_f >= lo) & (lane_f < lo + MOE_EPG), lg, neg)
    emax = jnp.max(el, axis=-1, keepdims=True)
    esum = jnp.sum(jnp.exp(el - emax), axis=-1, keepdims=True)
    l0 = first_max(el, emax)
    el2 = jnp.where(lane_f == l0, neg, el)
    emax2 = jnp.max(el2, axis=-1, keepdims=True)
    l1 = first_max(el2, emax2)
    p0 = 1.0 / esum
    p1 = jnp.exp(emax2 - emax) / esum
    w0 = g_w * (p0 / (p0 + p1))
    w1 = g_w * (p1 / (p0 + p1))
    oh0 = lane_f == l0
    oh1 = lane_f == l1
    oh = (oh0 | oh1).astype(BF16)
    r = lax.broadcasted_iota(jnp.int32, (m, m), 0)
    c = lax.broadcasted_iota(jnp.int32, (m, m), 1)
    cum = jnp.dot((c < r).astype(BF16), oh, preferred_element_type=F32) + carry
    rank0 = jnp.sum(jnp.where(oh0, cum, 0.0), axis=-1, keepdims=True)
    rank1 = jnp.sum(jnp.where(oh1, cum, 0.0), axis=-1, keepdims=True)
    carry = carry + jnp.sum(oh.astype(F32), axis=0, keepdims=True)
    rec = jnp.zeros((m, LANES), F32)
    for pos, val in ((RT_GATE, w0), (RT_GATE + 1, w1), (RT_EXPERT, l0 - MOE_GROUPS),
                     (RT_EXPERT + 1, l1 - MOE_GROUPS), (RT_RANK, rank0), (RT_RANK + 1, rank1)):
        rec = jnp.where(lane == pos, val, rec)
    return rec, carry


def _out_proj_kernel(n_parts, *refs):
    a_refs = refs[:n_parts]
    w_refs = refs[n_parts:2 * n_parts]
    h_ref, g_ref, wr_ref, br_ref, ho_ref, xt_ref, rt_ref, cnt_ref, carry_ref = refs[2 * n_parts:]

    @pl.when(pl.program_id(0) == 0)
    def _():
        carry_ref[...] = jnp.zeros_like(carry_ref)

    acc = h_ref[...]
    for a_ref, w_ref in zip(a_refs, w_refs):
        acc = acc + jnp.dot(a_ref[...], w_ref[...], preferred_element_type=F32)
    ho_ref[...] = acc
    xn = _rms(acc, g_ref[...])
    _store_rows_tiled(xt_ref, xn)
    lg = jnp.dot(xn, wr_ref[...], precision=HIGHEST, preferred_element_type=F32) + br_ref[...]
    rec, carry = _route_block(lg, carry_ref[...])
    rt_ref[...] = rec
    carry_ref[...] = carry
    cnt_ref[...] = carry


def _out_proj(parts, w_parts, h, g, w_route, b_route, *, tm=512):
    t, d = h.shape
    tm = min(tm, t)
    nc = d // LANES
    row = lambda i: (i, 0)
    const = lambda i: (0, 0)
    in_specs = [pl.BlockSpec((tm, a.shape[1]), row) for a in parts]
    in_specs += [pl.BlockSpec(w.shape, const) for w in w_parts]
    in_specs += [pl.BlockSpec((tm, d), row), pl.BlockSpec((1, d), const),
                 pl.BlockSpec((d, LANES), const), pl.BlockSpec((1, LANES), const)]
    return pl.pallas_call(
        functools.partial(_out_proj_kernel, len(parts)),
        grid=(t // tm,),
        in_specs=in_specs,
        out_specs=[pl.BlockSpec((tm, d), row), pl.BlockSpec((tm * nc, LANES), row),
                   pl.BlockSpec((tm, LANES), row), pl.BlockSpec((1, LANES), const)],
        out_shape=[jax.ShapeDtypeStruct((t, d), F32), jax.ShapeDtypeStruct((t * nc, LANES), F32),
                   jax.ShapeDtypeStruct((t, LANES), F32), jax.ShapeDtypeStruct((1, LANES), F32)],
        scratch_shapes=[pltpu.VMEM((1, LANES), F32)],
        compiler_params=_cparams(("arbitrary",)),
        name="out_proj",
    )(*parts, *w_parts, h, g.reshape(1, d), w_route, b_route)


def _zero_fill_rows(rows_ref, z_ref, zsem, lo_ref, hi_ref, n_rows):
    zb = z_ref.shape[0]
    z_ref[...] = jnp.zeros_like(z_ref)
    bits = [1 << b for b in reversed(range(MOE_TILE.bit_length() - 1))]
    assert bits[0] <= zb and MOE_TILE % zb == 0

    def piece(row0, n):
        return pltpu.make_async_copy(z_ref.at[pl.ds(0, n)], rows_ref.at[pl.ds(row0, n)], zsem)

    def sweep(issue):
        def per_expert(e, carry):
            off = lo_ref[e]
            run = hi_ref[e] - off
            for n in bits:
                hit = (run & n) != 0

                @pl.when(hit)
                def _(off=off, n=n):
                    piece(off, n).start() if issue else piece(off, n).wait()

                off = off + jnp.where(hit, n, 0)
            return carry

        def per_block(i, carry):
            piece(i * zb, zb).start() if issue else piece(i * zb, zb).wait()
            return carry

        lax.fori_loop(0, MOE_EXPERTS, per_expert, 0)
        lax.fori_loop(hi_ref[MOE_EXPERTS - 1] // zb, n_rows // zb, per_block, 0)

    sweep(True)
    sweep(False)


def _moe_dispatch_kernel(e_ref, r_ref, st_ref, lo_ref, hi_ref, x_ref, rows_ref, z_ref, sem, zsem, *, tm, n_tok):
    t0 = pl.program_id(0) * tm

    @pl.when(pl.program_id(0) == 0)
    def _():
        _zero_fill_rows(rows_ref, z_ref, zsem, lo_ref, hi_ref, rows_ref.shape[0])

    def copy(r, row):
        return pltpu.make_async_copy(x_ref.at[r], rows_ref.at[row], sem)

    def start(r, carry):
        for k in range(MOE_TOPK):
            a = k * n_tok + t0 + r
            copy(r, st_ref[e_ref[a]] + r_ref[a]).start()
        return carry

    def wait(r, carry):
        for _ in range(MOE_TOPK):
            copy(0, 0).wait()
        return carry

    lax.fori_loop(0, tm, start, 0, unroll=8)
    lax.fori_loop(0, tm, wait, 0, unroll=8)


def _moe_dispatch(x3, expert, rank, starts, pad_lo, pad_hi, n_rows, *, tm=512):
    t = x3.shape[0]
    tm = min(tm, t)
    return pl.pallas_call(
        functools.partial(_moe_dispatch_kernel, tm=tm, n_tok=t),
        grid_spec=pltpu.PrefetchScalarGridSpec(
            num_scalar_prefetch=5, grid=(t // tm,),
            in_specs=[pl.BlockSpec((tm,) + x3.shape[1:], lambda i, *_: (i, 0, 0))],
            out_specs=pl.BlockSpec(memory_space=pl.ANY),
            scratch_shapes=[pltpu.VMEM((MOE_TILE // 2,) + x3.shape[1:], x3.dtype),
                            pltpu.SemaphoreType.DMA(()), pltpu.SemaphoreType.DMA(())]),
        out_shape=jax.ShapeDtypeStruct((n_rows,) + x3.shape[1:], x3.dtype),
        compiler_params=pltpu.CompilerParams(dimension_semantics=("arbitrary",), has_side_effects=True),
        name="moe_dispatch",
    )(expert, rank, starts, pad_lo, pad_hi, x3)


def _moe_gather_kernel(e_ref, r_ref, st_ref, rows_ref, o_ref, sem, *, tm):
    a0 = pl.program_id(0) * tm

    def copy(row, r):
        return pltpu.make_async_copy(rows_ref.at[row], o_ref.at[r], sem)

    def start(r, carry):
        a = a0 + r
        copy(st_ref[e_ref[a]] + r_ref[a], r).start()
        return carry

    def wait(r, carry):
        copy(0, 0).wait()
        return carry

    lax.fori_loop(0, tm, start, 0, unroll=8)
    lax.fori_loop(0, tm, wait, 0, unroll=8)


def _moe_gather(rows3, expert, rank, starts, *, tm=1024):
    n = expert.shape[0]
    tm = min(tm, n)
    return pl.pallas_call(
        functools.partial(_moe_gather_kernel, tm=tm),
        grid_spec=pltpu.PrefetchScalarGridSpec(
            num_scalar_prefetch=3, grid=(n // tm,),
            in_specs=[pl.BlockSpec(memory_space=pl.ANY)],
            out_specs=pl.BlockSpec((tm,) + rows3.shape[1:], lambda i, e, r, s: (i, 0, 0)),
            scratch_shapes=[pltpu.SemaphoreType.DMA(())]),
        out_shape=jax.ShapeDtypeStruct((n,) + rows3.shape[1:], rows3.dtype),
        compiler_params=pltpu.CompilerParams(dimension_semantics=("arbitrary",)),
        name="moe_gather",
    )(expert, rank, starts, rows3)


def _moe_kernel(te_ref, nu_ref, x_ref, wg_ref, wu_ref, wd_ref, y_ref):
    i = pl.program_id(0)

    @pl.when(i >= nu_ref[0])
    def _():
        y_ref[...] = jnp.zeros_like(y_ref)

    @pl.when(i < nu_ref[0])
    def _():
        x = _load_rows_tiled(x_ref, MOE_TILE, BF16)
        gate = jnp.dot(x, wg_ref[...].astype(BF16), preferred_element_type=F32)
        up = jnp.dot(x, wu_ref[...].astype(BF16), preferred_element_type=F32)
        act = (jax.nn.silu(gate) * up).astype(BF16)
        y = jnp.dot(act, wd_ref[...].astype(BF16), preferred_element_type=F32)
        _store_rows_tiled(y_ref, y)


def _moe_experts(x_rows, tile_expert, n_used, w_gate, w_up, w_down, layer):
    d, ff = w_gate.shape[-2:]
    nc = d // LANES
    n_tiles = x_rows.shape[0] // (MOE_TILE * nc)
    live = lambda i, nu: jnp.minimum(i, nu[0] - 1)
    w_spec = lambda a, b: pl.BlockSpec((None, None, a, b), lambda i, te, nu: (layer, te[live(i, nu)], 0, 0))
    return pl.pallas_call(
        _moe_kernel,
        grid_spec=pltpu.PrefetchScalarGridSpec(
            num_scalar_prefetch=2, grid=(n_tiles,),
            in_specs=[pl.BlockSpec((MOE_TILE * nc, LANES), lambda i, te, nu: (live(i, nu), 0)),
                      w_spec(d, ff), w_spec(d, ff), w_spec(ff, d)],
            out_specs=pl.BlockSpec((MOE_TILE * nc, LANES), lambda i, te, nu: (i, 0))),
        out_shape=jax.ShapeDtypeStruct(x_rows.shape, F32),
        compiler_params=_cparams(("arbitrary",)),
        name="moe_experts",
    )(tile_expert, n_used, x_rows, w_gate, w_up, w_down)


def _moe_combine(h, m0_ref, m1_ref, rt_ref):
    rt = rt_ref[...]
    tm = h.shape[0]
    return (h + _load_rows_tiled(m0_ref, tm) * rt[:, RT_GATE:RT_GATE + 1]
            + _load_rows_tiled(m1_ref, tm) * rt[:, RT_GATE + 1:RT_GATE + 2])


def _final_norm_kernel(h_ref, m0_ref, m1_ref, rt_ref, g_ref, o_ref):
    o_ref[...] = _rms(_moe_combine(h_ref[...], m0_ref, m1_ref, rt_ref), g_ref[...])


def _final_norm(h, m, route, g, *, tm=512):
    t, d = h.shape
    tm = min(tm, t)
    nc = d // LANES
    row = lambda i: (i, 0)
    return pl.pallas_call(
        _final_norm_kernel,
        grid=(t // tm,),
        in_specs=[pl.BlockSpec((tm, d), row),
                  pl.BlockSpec((tm * nc, LANES), row),
                  pl.BlockSpec((tm * nc, LANES), lambda i: (t // tm + i, 0)),
                  pl.BlockSpec((tm, LANES), row), pl.BlockSpec((1, d), lambda i: (0, 0))],
        out_specs=pl.BlockSpec((tm, d), row),
        out_shape=jax.ShapeDtypeStruct((t, d), F32),
        compiler_params=_cparams(("parallel",)),
        name="final_norm",
    )(h, m, m, route, g.reshape(1, d))


def _moe(x_tiled, route, counts, w_gate, w_up, w_down, layer):
    t = route.shape[0]
    nc = x_tiled.shape[0] // t
    cnt = counts[0, MOE_GROUPS:MOE_GROUPS + MOE_EXPERTS].astype(jnp.int32)
    padded = (cnt + MOE_TILE - 1) // MOE_TILE * MOE_TILE
    pad_ends = jnp.cumsum(padded)
    starts = (pad_ends - padded).astype(jnp.int32)
    n_tiles = (t * MOE_TOPK + MOE_EXPERTS * (MOE_TILE - 1)) // MOE_TILE
    tile_start = jnp.arange(n_tiles, dtype=jnp.int32) * MOE_TILE
    tile_expert = jnp.minimum(jnp.sum(tile_start[:, None] >= pad_ends[None, :], axis=1),
                              MOE_EXPERTS - 1).astype(jnp.int32)
    n_used = (pad_ends[-1] // MOE_TILE).astype(jnp.int32).reshape(1)
    n_rows = n_tiles * MOE_TILE
    k_major = lambda lane0: route[:, lane0:lane0 + MOE_TOPK].T.reshape(-1).astype(jnp.int32)
    expert, rank = k_major(RT_EXPERT), k_major(RT_RANK)
    x_rows = _moe_dispatch(x_tiled.reshape(t, nc, LANES), expert, rank, starts, starts + cnt,
                           pad_ends.astype(jnp.int32), n_rows)
    y_rows = _moe_experts(x_rows.reshape(n_rows * nc, LANES), tile_expert, n_used, w_gate, w_up, w_down, layer)
    m = _moe_gather(y_rows.reshape(n_rows, nc, LANES), expert, rank, starts)
    return m.reshape(MOE_TOPK * t * nc, LANES)


def _router_weights(w_group, b_group, w_router, b_router):
    d = w_group.shape[0]
    w = jnp.zeros((d, LANES), F32)
    w = w.at[:, :MOE_GROUPS].set(w_group).at[:, MOE_GROUPS:MOE_GROUPS + MOE_EXPERTS].set(w_router)
    b = jnp.zeros((1, LANES), F32)
    b = b.at[0, :MOE_GROUPS].set(b_group).at[0, MOE_GROUPS:MOE_GROUPS + MOE_EXPERTS].set(b_router)
    return w, b


def kernel(x, norm_mix, norm_moe, norm_final, even_w_in, even_sinks, even_forget_bias, even_w_out,
           odd_w_in, odd_conv_w, odd_conv_b, odd_dt_bias, odd_a_log, odd_d_skip, odd_ssd_norm,
           odd_gk_w, odd_gk_b, odd_gla_norm, odd_w_out, moe_w_group, moe_b_group, moe_w_router,
           moe_b_router, moe_w_gate, moe_w_up, moe_w_down):
    b, s, d = x.shape
    t = b * s
    depth = norm_mix.shape[0]
    h = x.reshape(t, d)
    moe = None
    for layer in range(depth):
        i = layer // 2
        if layer % 2 == 0:
            w = even_w_in[i]
            n_ab = (A_Q_HEADS + 2 * A_KV_HEADS + 3 * B_HEADS) * HEAD_DIM
            w_main = w[:, :n_ab].astype(BF16)
            w_aux = jnp.zeros((d, LANES), F32).at[:, :B_HEADS].set(w[:, n_ab:]).astype(BF16)
            (proj, f_aux), h_new = _norm_proj(h, norm_mix[layer], [w_main, w_aux],
                                              ((0, 0, n_ab, BF16), (1, 0, LANES, F32)), tm=512, moe=moe)
            h = h if h_new is None else h_new
            proj = proj.reshape(b, s, -1)
            out_a = _swa(proj, even_sinks[i])
            c, ct = _fox_gate(f_aux.reshape(b, s, LANES), even_forget_bias[i])
            out_b = _fox(proj, c, ct)
            n_ha = A_Q_HEADS * HEAD_DIM
            w_out = even_w_out[i].astype(BF16)
            parts = [out_a.reshape(t, -1), out_b.reshape(t, -1)]
            w_parts = [w_out[:n_ha], w_out[n_ha:]]
        else:
            w = odd_w_in[i]
            o_z, o_xbc = 0, C_INNER
            o_dt = o_xbc + C_CONV_DIM
            o_q = o_dt + C_HEADS
            o_k = o_q + D_KEY
            o_v = o_k + D_KEY
            o_g = o_v + D_VAL
            o_r = o_g + D_GATE_RANK
            n_b = w.shape[1] - o_dt
            w_a = w[:, :o_dt].astype(BF16)
            w_b = jnp.pad(w[:, o_dt:].astype(BF16), ((0, 0), (0, -n_b % LANES)))
            g_win = (o_g - o_dt) // LANES * LANES
            plan = ((0, o_z, C_INNER, BF16), (1, o_q - o_dt, D_KEY, BF16), (1, o_k - o_dt, D_KEY, BF16),
                    (1, o_v - o_dt, D_VAL, BF16), (1, o_r - o_dt, D_VAL, BF16), (0, o_xbc, C_CONV_DIM, F32),
                    (1, 0, LANES, F32), (1, g_win, LANES, F32))
            outs, h_new = _norm_proj(h, norm_mix[layer], [w_a, w_b], plan, tm=256, moe=moe)
            h = h if h_new is None else h_new
            params = dict(conv_w=odd_conv_w[i], conv_b=odd_conv_b[i], dt_bias=odd_dt_bias[i], a_log=odd_a_log[i],
                          d_skip=odd_d_skip[i], ssd_norm=odd_ssd_norm[i], gk_w=odd_gk_w[i], gk_b=odd_gk_b[i],
                          gla_norm=odd_gla_norm[i])
            mixed = _ssd_gla(*[o.reshape(b, s, -1) for o in outs], o_g - o_dt - g_win, params)
            parts = [mixed.reshape(t, -1)]
            w_parts = [odd_w_out[i].astype(BF16)]
        w_route, b_route = _router_weights(moe_w_group[layer], moe_b_group[layer],
                                           moe_w_router[layer], moe_b_router[layer])
        h, x_tiled, route, counts = _out_proj(parts, w_parts, h, norm_moe[layer], w_route, b_route)
        moe = (_moe(x_tiled, route, counts, moe_w_gate, moe_w_up, moe_w_down, layer), route)
    out = _final_norm(h, moe[0], moe[1], norm_final)
    return out.reshape(b, s, d)
```

```python
import functools
import math

import numpy as np
import jax
import jax.numpy as jnp
from jax import lax
from jax.experimental import pallas as pl
from jax.experimental.pallas import tpu as pltpu

F32 = jnp.float32
BF16 = jnp.bfloat16
HIGHEST = lax.Precision.HIGHEST

RMS_EPS = 1e-6
HEAD_DIM = 64
A_Q_HEADS = 8
A_KV_HEADS = 2
A_GROUP = A_Q_HEADS // A_KV_HEADS
A_WINDOW = 128
B_HEADS = 8
C_HEADS = 16
C_HEAD_DIM = 64
C_INNER = C_HEADS * C_HEAD_DIM
C_GROUPS = 2
C_HPG = C_HEADS // C_GROUPS
C_STATE = 128
C_CONV = 4
C_CHUNK = 128
C_CONV_DIM = C_INNER + 2 * C_GROUPS * C_STATE
D_HEADS = 4
D_HK = 128
D_HV = 256
D_KEY = D_HEADS * D_HK
D_VAL = D_HEADS * D_HV
D_GATE_RANK = 16
D_GATE_NORM = 16.0
D_CHUNK = 64
MOE_GROUPS = 4
MOE_EPG = 8
MOE_EXPERTS = MOE_GROUPS * MOE_EPG
MOE_TOPK = 2

LANES = 128
VMEM_LIMIT = 48 * 1024 * 1024
MOE_TILE = 512
COPY_WINDOW = 64


def _cparams(sem):
    return pltpu.CompilerParams(dimension_semantics=sem, vmem_limit_bytes=VMEM_LIMIT)


def _rms(x, g):
    ms = jnp.mean(x * x, axis=-1, keepdims=True)
    return x * lax.rsqrt(ms + RMS_EPS) * g


def _norm_proj_kernel(combine, n_w, plan, *refs):
    it = iter(refs)
    x_ref = next(it)
    if combine:
        m0_ref, m1_ref, gt_ref = next(it), next(it), next(it)
    g_ref = next(it)
    w_refs = [next(it) for _ in range(n_w)]
    o_refs = [next(it) for _ in plan]
    h_ref = next(it) if combine else None
    res_refs = [next(it) for _ in range(n_w)]

    x = x_ref[...]
    if combine:
        x = _moe_combine(x, m0_ref, m1_ref, gt_ref)
        h_ref[...] = x
    xn = _rms(x, g_ref[...]).astype(BF16)
    for w_ref, res_ref in zip(w_refs, res_refs):
        res_ref[...] = jnp.dot(xn, w_ref[...], preferred_element_type=F32)
    for o_ref, (wi, start, width, _) in zip(o_refs, plan):
        o_ref[...] = res_refs[wi][:, start:start + width].astype(o_ref.dtype)


def _norm_proj(x, g, weights, plan, *, tm, moe=None):
    t, d = x.shape
    tm = min(tm, t)
    combine = moe is not None
    row = lambda i: (i, 0)
    const = lambda i: (0, 0)
    in_specs = [pl.BlockSpec((tm, d), row)]
    args = [x]
    if combine:
        m, gates = moe
        nc = d // LANES
        in_specs += [pl.BlockSpec((tm * nc, LANES), row),
                     pl.BlockSpec((tm * nc, LANES), lambda i: (t // tm + i, 0)),
                     pl.BlockSpec((tm, LANES), row)]
        args += [m, m, gates]
    in_specs.append(pl.BlockSpec((1, d), const))
    args.append(g.reshape(1, d))
    for w in weights:
        in_specs.append(pl.BlockSpec(w.shape, const, pipeline_mode=pl.Buffered(1)))
        args.append(w)
    out_shape = [jax.ShapeDtypeStruct((t, width), dtype) for _, _, width, dtype in plan]
    out_specs = [pl.BlockSpec((tm, width), row) for _, _, width, _ in plan]
    if combine:
        out_shape.append(jax.ShapeDtypeStruct((t, d), F32))
        out_specs.append(pl.BlockSpec((tm, d), row))
    outs = pl.pallas_call(
        functools.partial(_norm_proj_kernel, combine, len(weights), plan),
        grid=(t // tm,),
        in_specs=in_specs, out_specs=out_specs, out_shape=out_shape,
        scratch_shapes=[pltpu.VMEM((tm, w.shape[1]), F32) for w in weights],
        compiler_params=_cparams(("parallel",)),
        name="norm_proj",
    )(*args)
    outs = list(outs)
    h = outs.pop() if combine else None
    return outs, h


def _swa_kernel(sink_ref, q_ref, kp_ref, kc_ref, vp_ref, vc_ref, o_ref):
    n = pl.program_id(1)
    blk = A_WINDOW
    row = lax.broadcasted_iota(jnp.int32, (blk, 2 * blk), 0)
    col = lax.broadcasted_iota(jnp.int32, (blk, 2 * blk), 1)
    dist = blk + row - col
    valid = (dist >= 0) & (dist < A_WINDOW) & ((col >= blk) | (n > 0))
    distf = dist.astype(F32)
    outs = []
    for kh in range(A_KV_HEADS):
        ks = slice(kh * HEAD_DIM, (kh + 1) * HEAD_DIM)
        k = jnp.concatenate([kp_ref[0, :, ks], kc_ref[0, :, ks]], axis=0)
        v = jnp.concatenate([vp_ref[0, :, ks], vc_ref[0, :, ks]], axis=0)
        for gi in range(A_GROUP):
            h = kh * A_GROUP + gi
            slope = float(2.0 ** (-8.0 * (h + 1) / A_Q_HEADS))
            q = q_ref[0, :, h * HEAD_DIM:(h + 1) * HEAD_DIM]
            s = lax.dot_general(q, k, (((1,), (1,)), ((), ())), preferred_element_type=F32)
            s = s * (HEAD_DIM ** -0.5) - slope * distf
            s = jnp.where(valid, s, -jnp.inf)
            sink = sink_ref[h]
            m = jnp.maximum(jnp.max(s, axis=-1, keepdims=True), sink)
            p = jnp.exp(s - m)
            denom = jnp.sum(p, axis=-1, keepdims=True) + jnp.exp(sink - m)
            o = jnp.dot(p.astype(BF16), v, preferred_element_type=F32)
            outs.append(o / denom)
    o_ref[0] = jnp.concatenate(outs, axis=-1).astype(o_ref.dtype)


def _swa(proj, sinks):
    b, s, _ = proj.shape
    blk = A_WINDOW
    qw = A_Q_HEADS * HEAD_DIM
    kw = A_KV_HEADS * HEAD_DIM
    q_blk = 0
    k_blk = qw // kw
    v_blk = k_blk + 1
    prev = lambda i, n: jnp.maximum(n - 1, 0)
    return pl.pallas_call(
        _swa_kernel,
        grid=(b, s // blk),
        in_specs=[
            pl.BlockSpec(memory_space=pltpu.SMEM),
            pl.BlockSpec((1, blk, qw), lambda i, n: (i, n, q_blk)),
            pl.BlockSpec((1, blk, kw), lambda i, n: (i, prev(i, n), k_blk)),
            pl.BlockSpec((1, blk, kw), lambda i, n: (i, n, k_blk)),
            pl.BlockSpec((1, blk, kw), lambda i, n: (i, prev(i, n), v_blk)),
            pl.BlockSpec((1, blk, kw), lambda i, n: (i, n, v_blk)),
        ],
        out_specs=pl.BlockSpec((1, blk, qw), lambda i, n: (i, n, 0)),
        out_shape=jax.ShapeDtypeStruct((b, s, qw), BF16),
        compiler_params=_cparams(("parallel", "parallel")),
        name="swa",
    )(sinks.astype(F32), proj, proj, proj, proj, proj)


def _tril(n, dtype=F32):
    r = lax.broadcasted_iota(jnp.int32, (n, n), 0)
    c = lax.broadcasted_iota(jnp.int32, (n, n), 1)
    return (c <= r).astype(dtype)


def _split3(x):
    hi = x.astype(BF16)
    r = x - hi.astype(F32)
    mid = r.astype(BF16)
    return hi, mid, (r - mid.astype(F32)).astype(BF16)


def _dot_mask_lhs(mask, x):
    return sum(jnp.dot(mask, part, preferred_element_type=F32) for part in _split3(x))


def _dot_mask_rhs(x, mask):
    return sum(jnp.dot(part, mask, preferred_element_type=F32) for part in _split3(x))


def _fox_gate_kernel(f_ref, b_ref, c_ref, ct_ref):
    tri = _tril(LANES, BF16)
    carry = jnp.zeros((1, LANES), F32)
    for n in range(f_ref.shape[1] // LANES):
        rows = slice(n * LANES, (n + 1) * LANES)
        lf = jax.nn.log_sigmoid(f_ref[0, rows, :] + b_ref[...])
        cs = _dot_mask_lhs(tri, lf) + carry
        carry = cs[LANES - 1:LANES, :]
        c_ref[0, rows, :] = cs
        ct_ref[0, n] = cs.T[:B_HEADS, :]


def _fox_gate(f_aux, bias):
    b, s, _ = f_aux.shape
    nb = s // LANES
    bias_p = jnp.zeros((1, LANES), F32).at[0, :B_HEADS].set(bias.astype(F32))
    return pl.pallas_call(
        _fox_gate_kernel,
        grid=(b,),
        in_specs=[pl.BlockSpec((1, s, LANES), lambda i: (i, 0, 0)),
                  pl.BlockSpec((1, LANES), lambda i: (0, 0))],
        out_specs=[pl.BlockSpec((1, s, LANES), lambda i: (i, 0, 0)),
                   pl.BlockSpec((1, nb, B_HEADS, LANES), lambda i: (i, 0, 0, 0))],
        out_shape=[jax.ShapeDtypeStruct((b, s, LANES), F32),
                   jax.ShapeDtypeStruct((b, nb, B_HEADS, LANES), F32)],
        compiler_params=_cparams(("parallel",)),
        name="fox_gate",
    )(f_aux, bias_p)


def _fox_kernel(q0_ref, q1_ref, k0_ref, k1_ref, vt_ref, c_ref, ctq_ref, o_ref, *, tq, heads_per_step):
    qi = pl.program_id(1)
    sub = tq // LANES
    key = lax.broadcasted_iota(jnp.int32, (tq, tq), 0)
    qry = lax.broadcasted_iota(jnp.int32, (tq, tq), 1)
    causal = key <= qry
    nt = (((1,), (1,)), ((), ()))
    half = B_HEADS // 2
    q_refs, k_refs = (q0_ref, q1_ref), (k0_ref, k1_ref)
    outs = []
    for h0 in range(0, B_HEADS, heads_per_step):
        heads = list(range(h0, h0 + heads_per_step))
        hsl = [slice(h * HEAD_DIM, (h + 1) * HEAD_DIM) for h in heads]
        lsl = [slice((h % half) * HEAD_DIM, (h % half + 1) * HEAD_DIM) for h in heads]
        qs = [q_refs[h // half][0, :, ls] * (HEAD_DIM ** -0.5)
              for h, ls in zip(heads, lsl)]
        cqs = [jnp.concatenate([ctq_ref[0, u, h:h + 1, :] for u in range(sub)], axis=1) for h in heads]

        def step(j, carry, masked, heads=heads, hsl=hsl, lsl=lsl, qs=qs, cqs=cqs):
            start = pl.multiple_of(j * tq, tq)
            sts = [lax.dot_general(k_refs[h // half][0, pl.ds(start, tq), ls], q, nt,
                                   preferred_element_type=F32)
                   for h, ls, q in zip(heads, lsl, qs)]
            ps, stats = [], []
            for idx, h in enumerate(heads):
                m, l, _ = carry[3 * idx:3 * idx + 3]
                ck = c_ref[0, pl.ds(start, tq), h:h + 1]
                st = (sts[idx] - ck) + cqs[idx]
                if masked:
                    st = jnp.where(causal, st, -jnp.inf)
                m_new = jnp.maximum(m, jnp.max(st, axis=0, keepdims=True))
                alpha = jnp.exp(m - m_new)
                p = jnp.exp(st - m_new)
                stats.append((m_new, alpha, alpha * l + jnp.sum(p, axis=0, keepdims=True)))
                ps.append(p.astype(BF16))
            new = []
            for idx in range(len(heads)):
                m_new, alpha, l = stats[idx]
                pv = jnp.dot(vt_ref[0, j, hsl[idx], :], ps[idx], preferred_element_type=F32)
                new += [m_new, l, alpha * carry[3 * idx + 2] + pv]
            return tuple(new)

        init = (jnp.full((1, tq), -jnp.inf, F32), jnp.zeros((1, tq), F32),
                jnp.zeros((HEAD_DIM, tq), F32)) * heads_per_step
        carry = lax.fori_loop(0, qi, functools.partial(step, masked=False), init)
        carry = step(qi, carry, True)
        for idx in range(heads_per_step):
            outs.append(carry[3 * idx + 2] / carry[3 * idx + 1])
    o_ref[0] = jnp.concatenate(outs, axis=0).T.astype(o_ref.dtype)


def _fox(proj, c, ct, *, tq=256, heads_per_step=8):
    b, s, _ = proj.shape
    w = B_HEADS * HEAD_DIM
    nk = s // tq
    sub = tq // LANES
    hw = w // 2
    base = (A_Q_HEADS + 2 * A_KV_HEADS) * HEAD_DIM
    qb, kb = base // hw, (base + w) // hw
    v_t = proj[:, :, base + 2 * w:base + 3 * w].reshape(b, nk, tq, w).transpose(0, 1, 3, 2)
    return pl.pallas_call(
        functools.partial(_fox_kernel, tq=tq, heads_per_step=heads_per_step),
        grid=(b, s // tq),
        in_specs=[
            pl.BlockSpec((1, tq, hw), lambda i, n: (i, n, qb)),
            pl.BlockSpec((1, tq, hw), lambda i, n: (i, n, qb + 1)),
            pl.BlockSpec((1, s, hw), lambda i, n: (i, 0, kb)),
            pl.BlockSpec((1, s, hw), lambda i, n: (i, 0, kb + 1)),
            pl.BlockSpec((1, nk, w, tq), lambda i, n: (i, 0, 0, 0)),
            pl.BlockSpec((1, s, LANES), lambda i, n: (i, 0, 0)),
            pl.BlockSpec((1, sub, B_HEADS, LANES), lambda i, n: (i, n, 0, 0)),
        ],
        out_specs=pl.BlockSpec((1, tq, w), lambda i, n: (i, n, 0)),
        out_shape=jax.ShapeDtypeStruct((b, s, w), BF16),
        compiler_params=_cparams(("parallel", "parallel")),
        name="fox",
    )(proj, proj, proj, proj, v_t, c, ct)


def _ssd_gla_kernel(z_ref, q_ref, k_ref, v_ref, r_ref, xc_ref, xp_ref, sdt_ref, sg_ref,
                    cw_ref, cb_ref, dtb_ref, alog_ref, dsk_ref, ex_ref, sn_ref, gkw_ref, gkb_ref, gn_ref,
                    o_ref, conv_ref, hs_ref, gs_ref):
    c = pl.program_id(1)
    q_len = C_CHUNK
    halo = 8

    @pl.when(c == 0)
    def _():
        hs_ref[...] = jnp.zeros_like(hs_ref)
        gs_ref[...] = jnp.zeros_like(gs_ref)

    prev = xp_ref[0]
    conv_ref[0:halo, :] = jnp.where(c > 0, prev, jnp.zeros_like(prev))
    conv_ref[halo:halo + q_len, :] = xc_ref[0]
    acc = jnp.zeros((q_len, C_CONV_DIM), F32) + cb_ref[...]
    for j in range(C_CONV):
        off = halo - (C_CONV - 1) + j
        acc = acc + cw_ref[j:j + 1, :] * conv_ref[off:off + q_len, :]
    xbc = jax.nn.silu(acc)
    xs = xbc[:, :C_INNER]
    gs_w = C_GROUPS * C_STATE
    bm = xbc[:, C_INNER:C_INNER + gs_w].astype(BF16)
    cm = xbc[:, C_INNER + gs_w:].astype(BF16)

    row = lax.broadcasted_iota(jnp.int32, (q_len, q_len), 0)
    col = lax.broadcasted_iota(jnp.int32, (q_len, q_len), 1)
    tri = col <= row

    lane = lax.broadcasted_iota(jnp.int32, (1, LANES), 1)
    dt = jnp.where(lane < C_HEADS, jax.nn.softplus(sdt_ref[0] + dtb_ref[...]), 0.0)
    dta = dt * -jnp.exp(alog_ref[...])
    acs = _dot_mask_lhs(tri.astype(BF16), dta)
    acs_t = acs.T
    chunk_dec = jnp.exp(acs[q_len - 1:q_len, :])
    expand = ex_ref[...]
    dt_x = _dot_mask_rhs(dt, expand)
    acs_x = _dot_mask_rhs(acs, expand)
    xd = xs * dt_x
    xd_b = xd.astype(BF16)
    xdd = xd * jnp.exp(acs_x[q_len - 1:q_len, :] - acs_x)
    low_half = lax.broadcasted_iota(jnp.int32, (q_len, LANES), 1) < C_HEAD_DIM

    y_pairs, y_offs = [], []
    tdims = (((1,), (1,)), ((), ()))
    for g in range(C_GROUPS):
        b_g = bm[:, g * C_STATE:(g + 1) * C_STATE]
        c_g = cm[:, g * C_STATE:(g + 1) * C_STATE]
        cb = lax.dot_general(c_g, b_g, tdims, preferred_element_type=F32)
        h0 = g * C_HPG
        grp = slice(h0 * C_HEAD_DIM, (h0 + C_HPG) * C_HEAD_DIM)
        y_offs.append(lax.dot_general(c_g, hs_ref[grp, :].astype(BF16), tdims, preferred_element_type=F32))
        for h in range(h0, h0 + C_HPG, 2):
            xp = xd_b[:, h * C_HEAD_DIM:(h + 2) * C_HEAD_DIM]
            halves = []
            for hh in (h, h + 1):
                seg = jnp.exp(jnp.where(tri, acs[:, hh:hh + 1] - acs_t[hh:hh + 1, :], -jnp.inf))
                halves.append(jnp.dot((cb * seg).astype(BF16), xp, preferred_element_type=F32))
            y_pairs.append(jnp.where(low_half, halves[0], halves[1]))
        upd = jnp.dot(xdd[:, grp].T.astype(BF16), b_g, preferred_element_type=F32)
        for hh in range(C_HPG):
            h = h0 + hh
            ps = slice(h * C_HEAD_DIM, (h + 1) * C_HEAD_DIM)
            us = slice(hh * C_HEAD_DIM, (hh + 1) * C_HEAD_DIM)
            hs_ref[ps, :] = hs_ref[ps, :] * chunk_dec[0:1, h:h + 1] + upd[us, :]
    y = (jnp.concatenate(y_pairs, axis=1) + jnp.concatenate(y_offs, axis=1) * jnp.exp(acs_x)
         + dsk_ref[...] * xs)
    y = y * jax.nn.silu(z_ref[0].astype(F32))
    o_ref[0, :, :C_INNER] = _rms(y, sn_ref[...]).astype(o_ref.dtype)

    same = (row // D_CHUNK) == (col // D_CHUNK)
    tri2 = tri & same
    la = jnp.dot(sg_ref[0].astype(BF16), gkw_ref[...], preferred_element_type=F32) + gkb_ref[...]
    la = jax.nn.log_sigmoid(la) / D_GATE_NORM
    gcs = _dot_mask_lhs(tri2.astype(BF16), la)
    first = lax.broadcasted_iota(jnp.int32, (q_len, 1), 0) < D_CHUNK
    r_all = r_ref[0]
    for h in range(D_HEADS):
        ks = slice(h * D_HK, (h + 1) * D_HK)
        vs = slice(h * D_HV, (h + 1) * D_HV)
        g_h = gcs[:, ks]
        g_end0 = g_h[D_CHUNK - 1:D_CHUNK, :]
        g_end1 = g_h[q_len - 1:q_len, :]
        q_h = q_ref[0, :, ks].astype(F32) * (D_HK ** -0.5)
        k_h = k_ref[0, :, ks].astype(F32)
        v_h = v_ref[0, :, vs]
        q_dec = (q_h * jnp.exp(g_h)).astype(BF16)
        k_inv = (k_h * jnp.exp(-g_h)).astype(BF16)
        k_end = k_h * jnp.exp(jnp.where(first, g_end0, g_end1) - g_h)
        ke0 = jnp.where(first, k_end, 0.0).astype(BF16)
        ke1 = jnp.where(first, 0.0, k_end).astype(BF16)
        attn = lax.dot_general(q_dec, k_inv, (((1,), (1,)), ((), ())), preferred_element_type=F32)
        attn = jnp.where(tri2, attn, 0.0).astype(BF16)
        o = jnp.dot(attn, v_h, preferred_element_type=F32)
        v_t = v_h.astype(F32).T.astype(BF16)
        st_rows = slice(h * D_HV, (h + 1) * D_HV)
        s0 = gs_ref[st_rows, :]
        s1 = s0 * jnp.exp(g_end0) + jnp.dot(v_t, ke0, preferred_element_type=F32)
        s2 = s1 * jnp.exp(g_end1) + jnp.dot(v_t, ke1, preferred_element_type=F32)
        gs_ref[st_rows, :] = s2
        tdims = (((1,), (1,)), ((), ()))
        o0 = lax.dot_general(q_dec, s0.astype(BF16), tdims, preferred_element_type=F32)
        o1 = lax.dot_general(q_dec, s1.astype(BF16), tdims, preferred_element_type=F32)
        o = o + jnp.where(first, o0, o1)
        o = _rms(o, gn_ref[...]) * jax.nn.silu(r_all[:, vs].astype(F32))
        o_ref[0, :, C_INNER + h * D_HV:C_INNER + (h + 1) * D_HV] = o.astype(o_ref.dtype)


def _ssd_gla(z, q, k, v, r, xbc, side_dt, side_g, g_lane, p):
    b, s, _ = z.shape
    q_len = C_CHUNK
    chunk = lambda width: pl.BlockSpec((1, q_len, width), lambda i, n: (i, n, 0))
    full = lambda shape: pl.BlockSpec(shape, lambda i, n: (0,) * len(shape))
    pad_lanes = lambda vec: jnp.zeros((1, LANES), F32).at[0, :vec.shape[0]].set(vec.astype(F32))
    gkw = jnp.zeros((LANES, D_KEY), F32).at[g_lane:g_lane + D_GATE_RANK].set(p["gk_w"]).astype(BF16)
    expand = jnp.asarray(np.arange(C_INNER)[None, :] // C_HEAD_DIM == np.arange(LANES)[:, None], BF16)
    return pl.pallas_call(
        _ssd_gla_kernel,
        grid=(b, s // q_len),
        in_specs=[
            chunk(C_INNER), chunk(D_KEY), chunk(D_KEY), chunk(D_VAL), chunk(D_VAL), chunk(C_CONV_DIM),
            pl.BlockSpec((1, 8, C_CONV_DIM), lambda i, n: (i, jnp.maximum(n * (q_len // 8) - 1, 0), 0)),
            chunk(LANES), chunk(LANES),
            full((C_CONV, C_CONV_DIM)), full((1, C_CONV_DIM)),
            full((1, LANES)), full((1, LANES)), full((1, C_INNER)), full((LANES, C_INNER)),
            full((1, C_INNER)), full((LANES, D_KEY)), full((1, D_KEY)), full((1, D_HV)),
        ],
        out_specs=pl.BlockSpec((1, q_len, C_INNER + D_VAL), lambda i, n: (i, n, 0)),
        out_shape=jax.ShapeDtypeStruct((b, s, C_INNER + D_VAL), BF16),
        scratch_shapes=[pltpu.VMEM((8 + q_len, C_CONV_DIM), F32),
                        pltpu.VMEM((C_INNER, C_STATE), F32),
                        pltpu.VMEM((D_VAL, D_HK), F32)],
        compiler_params=_cparams(("parallel", "arbitrary")),
        name="ssd_gla",
    )(z, q, k, v, r, xbc, xbc, side_dt, side_g,
      p["conv_w"].astype(F32), p["conv_b"].reshape(1, -1).astype(F32),
      pad_lanes(p["dt_bias"]), pad_lanes(p["a_log"]),
      jnp.repeat(p["d_skip"].astype(F32), C_HEAD_DIM).reshape(1, C_INNER), expand,
      p["ssd_norm"].reshape(1, -1).astype(F32), gkw, p["gk_b"].reshape(1, -1).astype(F32),
      p["gla_norm"].reshape(1, -1).astype(F32))


def _store_rows_tiled(ref, val):
    m, d = val.shape
    nc = d // LANES
    for c in range(nc):
        ref[pl.ds(c, m, stride=nc), :] = val[:, c * LANES:(c + 1) * LANES]


def _load_rows_tiled(ref, m, dtype=None):
    nc = ref.shape[0] // m
    parts = [ref[pl.ds(c, m, stride=nc), :] for c in range(nc)]
    if dtype is not None:
        parts = [p.astype(dtype) for p in parts]
    return jnp.concatenate(parts, axis=1)


RT_GATE, RT_EXPERT, RT_RANK = 0, 2, 4


def _route_block(lg, carry):
    m = lg.shape[0]
    lane = lax.broadcasted_iota(jnp.int32, (m, LANES), 1)
    lane_f = lane.astype(F32)
    none = float(LANES)
    neg = -jnp.inf
    first_max = lambda v, vmax: jnp.min(jnp.where(v == vmax, lane_f, none), axis=-1, keepdims=True)
    gl = jnp.where(lane < MOE_GROUPS, lg, neg)
    gmax = jnp.max(gl, axis=-1, keepdims=True)
    g_w = 1.0 / jnp.sum(jnp.exp(gl - gmax), axis=-1, keepdims=True)
    lo = MOE_GROUPS + first_max(gl, gmax) * MOE_EPG
    el = jnp.where((lane_f >= lo) & (lane_f < lo + MOE_EPG), lg, neg)
    emax = jnp.max(el, axis=-1, keepdims=True)
    esum = jnp.sum(jnp.exp(el - emax), axis=-1, keepdims=True)
    l0 = first_max(el, emax)
    el2 = jnp.where(lane_f == l0, neg, el)
    emax2 = jnp.max(el2, axis=-1, keepdims=True)
    l1 = first_max(el2, emax2)
    p0 = 1.0 / esum
    p1 = jnp.exp(emax2 - emax) / esum
    w0 = g_w * (p0 / (p0 + p1))
    w1 = g_w * (p1 / (p0 + p1))
    oh0 = lane_f == l0
    oh1 = lane_f == l1
    oh = (oh0 | oh1).astype(BF16)
    r = lax.broadcasted_iota(jnp.int32, (m, m), 0)
    c = lax.broadcasted_iota(jnp.int32, (m, m), 1)
    cum = jnp.dot((c < r).astype(BF16), oh, preferred_element_type=F32) + carry
    rank0 = jnp.sum(jnp.where(oh0, cum, 0.0), axis=-1, keepdims=True)
    rank1 = jnp.sum(jnp.where(oh1, cum, 0.0), axis=-1, keepdims=True)
    carry = carry + jnp.sum(oh.astype(F32), axis=0, keepdims=True)
    rec = jnp.zeros((m, LANES), F32)
    for pos, val in ((RT_GATE, w0), (RT_GATE + 1, w1), (RT_EXPERT, l0 - MOE_GROUPS),
                     (RT_EXPERT + 1, l1 - MOE_GROUPS), (RT_RANK, rank0), (RT_RANK + 1, rank1)):
        rec = jnp.where(lane == pos, val, rec)
    return rec, carry


def _out_proj_kernel(n_parts, *refs):
    a_refs = refs[:n_parts]
    w_refs = refs[n_parts:2 * n_parts]
    h_ref, g_ref, wr_ref, br_ref, ho_ref, xt_ref, rt_ref, cnt_ref, carry_ref = refs[2 * n_parts:]

    @pl.when(pl.program_id(0) == 0)
    def _():
        carry_ref[...] = jnp.zeros_like(carry_ref)

    acc = h_ref[...]
    for a_ref, w_ref in zip(a_refs, w_refs):
        acc = acc + jnp.dot(a_ref[...], w_ref[...], preferred_element_type=F32)
    ho_ref[...] = acc
    xn = _rms(acc, g_ref[...])
    _store_rows_tiled(xt_ref, xn)
    x_hi, x_mid, _ = _split3(xn)
    wr = wr_ref[...]
    lg2 = jnp.dot(x_hi, wr, preferred_element_type=F32)
    lg = (lg2[:, :LANES] + lg2[:, LANES:] + jnp.dot(x_mid, wr[:, :LANES], preferred_element_type=F32)
          + br_ref[...])
    rec, carry = _route_block(lg, carry_ref[...])
    rt_ref[...] = rec
    carry_ref[...] = carry
    cnt_ref[...] = carry


def _out_proj(parts, w_parts, h, g, w_route, b_route, *, tm=512):
    t, d = h.shape
    tm = min(tm, t)
    nc = d // LANES
    row = lambda i: (i, 0)
    const = lambda i: (0, 0)
    in_specs = [pl.BlockSpec((tm, a.shape[1]), row) for a in parts]
    in_specs += [pl.BlockSpec(w.shape, const) for w in w_parts]
    in_specs += [pl.BlockSpec((tm, d), row), pl.BlockSpec((1, d), const),
                 pl.BlockSpec((d, 2 * LANES), const), pl.BlockSpec((1, LANES), const)]
    return pl.pallas_call(
        functools.partial(_out_proj_kernel, len(parts)),
        grid=(t // tm,),
        in_specs=in_specs,
        out_specs=[pl.BlockSpec((tm, d), row), pl.BlockSpec((tm * nc, LANES), row),
                   pl.BlockSpec((tm, LANES), row), pl.BlockSpec((1, LANES), const)],
        out_shape=[jax.ShapeDtypeStruct((t, d), F32), jax.ShapeDtypeStruct((t * nc, LANES), F32),
                   jax.ShapeDtypeStruct((t, LANES), F32), jax.ShapeDtypeStruct((1, LANES), F32)],
        scratch_shapes=[pltpu.VMEM((1, LANES), F32)],
        compiler_params=_cparams(("arbitrary",)),
        name="out_proj",
    )(*parts, *w_parts, h, g.reshape(1, d), w_route, b_route)


def _zero_fill_rows(rows_ref, z_ref, zsem, lo_ref, hi_ref, n_rows):
    zb = z_ref.shape[0]
    z_ref[...] = jnp.zeros_like(z_ref)
    bits = [1 << b for b in reversed(range(MOE_TILE.bit_length() - 1))]
    assert bits[0] <= zb and MOE_TILE % zb == 0

    def piece(row0, n):
        return pltpu.make_async_copy(z_ref.at[pl.ds(0, n)], rows_ref.at[pl.ds(row0, n)], zsem)

    def sweep(issue):
        def per_expert(e, carry):
            off = lo_ref[e]
            run = hi_ref[e] - off
            for n in bits:
                hit = (run & n) != 0

                @pl.when(hit)
                def _(off=off, n=n):
                    piece(off, n).start() if issue else piece(off, n).wait()

                off = off + jnp.where(hit, n, 0)
            return carry

        def per_block(i, carry):
            piece(i * zb, zb).start() if issue else piece(i * zb, zb).wait()
            return carry

        lax.fori_loop(0, MOE_EXPERTS, per_expert, 0)
        lax.fori_loop(hi_ref[MOE_EXPERTS - 1] // zb, n_rows // zb, per_block, 0)

    sweep(True)
    sweep(False)


def _moe_dispatch_kernel(e_ref, r_ref, st_ref, lo_ref, hi_ref, x_ref, rows_ref, z_ref, sem, zsem, *, tm, n_tok):
    t0 = pl.program_id(0) * tm

    @pl.when(pl.program_id(0) == 0)
    def _():
        _zero_fill_rows(rows_ref, z_ref, zsem, lo_ref, hi_ref, rows_ref.shape[0])

    def copy(r, row):
        return pltpu.make_async_copy(x_ref.at[r], rows_ref.at[row], sem)

    def start(r, carry):
        for k in range(MOE_TOPK):
            a = k * n_tok + t0 + r
            copy(r, st_ref[e_ref[a]] + r_ref[a]).start()
        return carry

    def wait(r, carry):
        for _ in range(MOE_TOPK):
            copy(0, 0).wait()
        return carry

    lax.fori_loop(0, tm, start, 0, unroll=8)
    lax.fori_loop(0, tm, wait, 0, unroll=8)


def _moe_dispatch(x3, expert, rank, starts, pad_lo, pad_hi, n_rows, *, tm=512):
    t = x3.shape[0]
    tm = min(tm, t)
    return pl.pallas_call(
        functools.partial(_moe_dispatch_kernel, tm=tm, n_tok=t),
        grid_spec=pltpu.PrefetchScalarGridSpec(
            num_scalar_prefetch=5, grid=(t // tm,),
            in_specs=[pl.BlockSpec((tm,) + x3.shape[1:], lambda i, *_: (i, 0, 0))],
            out_specs=pl.BlockSpec(memory_space=pl.ANY),
            scratch_shapes=[pltpu.VMEM((MOE_TILE // 2,) + x3.shape[1:], x3.dtype),
                            pltpu.SemaphoreType.DMA(()), pltpu.SemaphoreType.DMA(())]),
        out_shape=jax.ShapeDtypeStruct((n_rows,) + x3.shape[1:], x3.dtype),
        compiler_params=pltpu.CompilerParams(dimension_semantics=("arbitrary",), has_side_effects=True),
        name="moe_dispatch",
    )(expert, rank, starts, pad_lo, pad_hi, x3)


def _moe_gather_kernel(e_ref, r_ref, st_ref, rows_ref, o_ref, sem, *, tm):
    a0 = pl.program_id(0) * tm

    def copy(row, r):
        return pltpu.make_async_copy(rows_ref.at[row], o_ref.at[r], sem)

    def start(r, carry):
        a = a0 + r
        copy(st_ref[e_ref[a]] + r_ref[a], r).start()
        return carry

    def wait(r, carry):
        copy(0, 0).wait()
        return carry

    lax.fori_loop(0, tm, start, 0, unroll=8)
    lax.fori_loop(0, tm, wait, 0, unroll=8)


def _moe_gather(rows3, expert, rank, starts, *, tm=1024):
    n = expert.shape[0]
    tm = min(tm, n)
    return pl.pallas_call(
        functools.partial(_moe_gather_kernel, tm=tm),
        grid_spec=pltpu.PrefetchScalarGridSpec(
            num_scalar_prefetch=3, grid=(n // tm,),
            in_specs=[pl.BlockSpec(memory_space=pl.ANY)],
            out_specs=pl.BlockSpec((tm,) + rows3.shape[1:], lambda i, e, r, s: (i, 0, 0)),
            scratch_shapes=[pltpu.SemaphoreType.DMA(())]),
        out_shape=jax.ShapeDtypeStruct((n,) + rows3.shape[1:], rows3.dtype),
        compiler_params=pltpu.CompilerParams(dimension_semantics=("arbitrary",)),
        name="moe_gather",
    )(expert, rank, starts, rows3)


def _moe_kernel(te_ref, nu_ref, x_ref, wg_ref, wu_ref, wd_ref, y_ref):
    i = pl.program_id(0)

    @pl.when(i >= nu_ref[0])
    def _():
        y_ref[...] = jnp.zeros_like(y_ref)

    @pl.when(i < nu_ref[0])
    def _():
        x = _load_rows_tiled(x_ref, MOE_TILE, BF16)
        gate = jnp.dot(x, wg_ref[...].astype(BF16), preferred_element_type=F32)
        up = jnp.dot(x, wu_ref[...].astype(BF16), preferred_element_type=F32)
        act = (jax.nn.silu(gate) * up).astype(BF16)
        y = jnp.dot(act, wd_ref[...].astype(BF16), preferred_element_type=F32)
        _store_rows_tiled(y_ref, y)


def _moe_experts(x_rows, tile_expert, n_used, w_gate, w_up, w_down, layer):
    d, ff = w_gate.shape[-2:]
    nc = d // LANES
    n_tiles = x_rows.shape[0] // (MOE_TILE * nc)
    live = lambda i, nu: jnp.minimum(i, nu[0] - 1)
    w_spec = lambda a, b: pl.BlockSpec((None, None, a, b), lambda i, te, nu: (layer, te[live(i, nu)], 0, 0))
    return pl.pallas_call(
        _moe_kernel,
        grid_spec=pltpu.PrefetchScalarGridSpec(
            num_scalar_prefetch=2, grid=(n_tiles,),
            in_specs=[pl.BlockSpec((MOE_TILE * nc, LANES), lambda i, te, nu: (live(i, nu), 0)),
                      w_spec(d, ff), w_spec(d, ff), w_spec(ff, d)],
            out_specs=pl.BlockSpec((MOE_TILE * nc, LANES), lambda i, te, nu: (i, 0))),
        out_shape=jax.ShapeDtypeStruct(x_rows.shape, F32),
        compiler_params=_cparams(("arbitrary",)),
        name="moe_experts",
    )(tile_expert, n_used, x_rows, w_gate, w_up, w_down)


def _moe_combine(h, m0_ref, m1_ref, rt_ref):
    rt = rt_ref[...]
    tm = h.shape[0]
    return (h + _load_rows_tiled(m0_ref, tm) * rt[:, RT_GATE:RT_GATE + 1]
            + _load_rows_tiled(m1_ref, tm) * rt[:, RT_GATE + 1:RT_GATE + 2])


def _final_norm_kernel(h_ref, m0_ref, m1_ref, rt_ref, g_ref, o_ref):
    o_ref[...] = _rms(_moe_combine(h_ref[...], m0_ref, m1_ref, rt_ref), g_ref[...])


def _final_norm(h, m, route, g, *, tm=512):
    t, d = h.shape
    tm = min(tm, t)
    nc = d // LANES
    row = lambda i: (i, 0)
    return pl.pallas_call(
        _final_norm_kernel,
        grid=(t // tm,),
        in_specs=[pl.BlockSpec((tm, d), row),
                  pl.BlockSpec((tm * nc, LANES), row),
                  pl.BlockSpec((tm * nc, LANES), lambda i: (t // tm + i, 0)),
                  pl.BlockSpec((tm, LANES), row), pl.BlockSpec((1, d), lambda i: (0, 0))],
        out_specs=pl.BlockSpec((tm, d), row),
        out_shape=jax.ShapeDtypeStruct((t, d), F32),
        compiler_params=_cparams(("parallel",)),
        name="final_norm",
    )(h, m, m, route, g.reshape(1, d))


def _moe(x_tiled, route, counts, w_gate, w_up, w_down, layer):
    t = route.shape[0]
    nc = x_tiled.shape[0] // t
    cnt = counts[0, MOE_GROUPS:MOE_GROUPS + MOE_EXPERTS].astype(jnp.int32)
    padded = (cnt + MOE_TILE - 1) // MOE_TILE * MOE_TILE
    pad_ends = jnp.cumsum(padded)
    starts = (pad_ends - padded).astype(jnp.int32)
    n_tiles = (t * MOE_TOPK + MOE_EXPERTS * (MOE_TILE - 1)) // MOE_TILE
    tile_start = jnp.arange(n_tiles, dtype=jnp.int32) * MOE_TILE
    tile_expert = jnp.minimum(jnp.sum(tile_start[:, None] >= pad_ends[None, :], axis=1),
                              MOE_EXPERTS - 1).astype(jnp.int32)
    n_used = (pad_ends[-1] // MOE_TILE).astype(jnp.int32).reshape(1)
    n_rows = n_tiles * MOE_TILE
    k_major = lambda lane0: route[:, lane0:lane0 + MOE_TOPK].T.reshape(-1).astype(jnp.int32)
    expert, rank = k_major(RT_EXPERT), k_major(RT_RANK)
    x_rows = _moe_dispatch(x_tiled.reshape(t, nc, LANES), expert, rank, starts, starts + cnt,
                           pad_ends.astype(jnp.int32), n_rows)
    y_rows = _moe_experts(x_rows.reshape(n_rows * nc, LANES), tile_expert, n_used, w_gate, w_up, w_down, layer)
    m = _moe_gather(y_rows.reshape(n_rows, nc, LANES), expert, rank, starts)
    return m.reshape(MOE_TOPK * t * nc, LANES)


def _router_weights(w_group, b_group, w_router, b_router):
    d = w_group.shape[0]
    w = jnp.zeros((d, LANES), F32)
    w = w.at[:, :MOE_GROUPS].set(w_group).at[:, MOE_GROUPS:MOE_GROUPS + MOE_EXPERTS].set(w_router)
    b = jnp.zeros((1, LANES), F32)
    b = b.at[0, :MOE_GROUPS].set(b_group).at[0, MOE_GROUPS:MOE_GROUPS + MOE_EXPERTS].set(b_router)
    w_hi = w.astype(BF16)
    w_mid = (w - w_hi.astype(F32)).astype(BF16)
    return jnp.concatenate([w_hi, w_mid], axis=1), b


def kernel(x, norm_mix, norm_moe, norm_final, even_w_in, even_sinks, even_forget_bias, even_w_out,
           odd_w_in, odd_conv_w, odd_conv_b, odd_dt_bias, odd_a_log, odd_d_skip, odd_ssd_norm,
           odd_gk_w, odd_gk_b, odd_gla_norm, odd_w_out, moe_w_group, moe_b_group, moe_w_router,
           moe_b_router, moe_w_gate, moe_w_up, moe_w_down):
    b, s, d = x.shape
    t = b * s
    depth = norm_mix.shape[0]
    h = x.reshape(t, d)
    moe = None
    for layer in range(depth):
        i = layer // 2
        if layer % 2 == 0:
            w = even_w_in[i]
            n_ab = (A_Q_HEADS + 2 * A_KV_HEADS + 3 * B_HEADS) * HEAD_DIM
            w_main = w[:, :n_ab].astype(BF16)
            w_aux = jnp.zeros((d, LANES), F32).at[:, :B_HEADS].set(w[:, n_ab:]).astype(BF16)
            (proj, f_aux), h_new = _norm_proj(h, norm_mix[layer], [w_main, w_aux],
                                              ((0, 0, n_ab, BF16), (1, 0, LANES, F32)), tm=512, moe=moe)
            h = h if h_new is None else h_new
            proj = proj.reshape(b, s, -1)
            out_a = _swa(proj, even_sinks[i])
            c, ct = _fox_gate(f_aux.reshape(b, s, LANES), even_forget_bias[i])
            out_b = _fox(proj, c, ct)
            n_ha = A_Q_HEADS * HEAD_DIM
            w_out = even_w_out[i].astype(BF16)
            parts = [out_a.reshape(t, -1), out_b.reshape(t, -1)]
            w_parts = [w_out[:n_ha], w_out[n_ha:]]
        else:
            w = odd_w_in[i]
            o_z, o_xbc = 0, C_INNER
            o_dt = o_xbc + C_CONV_DIM
            o_q = o_dt + C_HEADS
            o_k = o_q + D_KEY
            o_v = o_k + D_KEY
            o_g = o_v + D_VAL
            o_r = o_g + D_GATE_RANK
            n_b = w.shape[1] - o_dt
            w_a = w[:, :o_dt].astype(BF16)
            w_b = jnp.pad(w[:, o_dt:].astype(BF16), ((0, 0), (0, -n_b % LANES)))
            g_win = (o_g - o_dt) // LANES * LANES
            plan = ((0, o_z, C_INNER, BF16), (1, o_q - o_dt, D_KEY, BF16), (1, o_k - o_dt, D_KEY, BF16),
                    (1, o_v - o_dt, D_VAL, BF16), (1, o_r - o_dt, D_VAL, BF16), (0, o_xbc, C_CONV_DIM, F32),
                    (1, 0, LANES, F32), (1, g_win, LANES, F32))
            outs, h_new = _norm_proj(h, norm_mix[layer], [w_a, w_b], plan, tm=256, moe=moe)
            h = h if h_new is None else h_new
            params = dict(conv_w=odd_conv_w[i], conv_b=odd_conv_b[i], dt_bias=odd_dt_bias[i], a_log=odd_a_log[i],
                          d_skip=odd_d_skip[i], ssd_norm=odd_ssd_norm[i], gk_w=odd_gk_w[i], gk_b=odd_gk_b[i],
                          gla_norm=odd_gla_norm[i])
            mixed = _ssd_gla(*[o.reshape(b, s, -1) for o in outs], o_g - o_dt - g_win, params)
            parts = [mixed.reshape(t, -1)]
            w_parts = [odd_w_out[i].astype(BF16)]
        w_route, b_route = _router_weights(moe_w_group[layer], moe_b_group[layer],
                                           moe_w_router[layer], moe_b_router[layer])
        h, x_tiled, route, counts = _out_proj(parts, w_parts, h, norm_moe[layer], w_route, b_route)
        moe = (_moe(x_tiled, route, counts, moe_w_gate, moe_w_up, moe_w_down, layer), route)
    out = _final_norm(h, moe[0], moe[1], norm_final)
    return out.reshape(b, s, d)
```

```python
import functools
import math

import numpy as np
import jax
import jax.numpy as jnp
from jax import lax
from jax.experimental import pallas as pl
from jax.experimental.pallas import tpu as pltpu

F32 = jnp.float32
BF16 = jnp.bfloat16
HIGHEST = lax.Precision.HIGHEST

RMS_EPS = 1e-6
HEAD_DIM = 64
A_Q_HEADS = 8
A_KV_HEADS = 2
A_GROUP = A_Q_HEADS // A_KV_HEADS
A_WINDOW = 128
B_HEADS = 8
C_HEADS = 16
C_HEAD_DIM = 64
C_INNER = C_HEADS * C_HEAD_DIM
C_GROUPS = 2
C_HPG = C_HEADS // C_GROUPS
C_STATE = 128
C_CONV = 4
C_CHUNK = 128
C_CONV_DIM = C_INNER + 2 * C_GROUPS * C_STATE
D_HEADS = 4
D_HK = 128
D_HV = 256
D_KEY = D_HEADS * D_HK
D_VAL = D_HEADS * D_HV
D_GATE_RANK = 16
D_GATE_NORM = 16.0
D_CHUNK = 64
MOE_GROUPS = 4
MOE_EPG = 8
MOE_EXPERTS = MOE_GROUPS * MOE_EPG
MOE_TOPK = 2

LANES = 128
VMEM_LIMIT = 48 * 1024 * 1024
MOE_TILE = 512
COPY_WINDOW = 64


def _cparams(sem):
    return pltpu.CompilerParams(dimension_semantics=sem, vmem_limit_bytes=VMEM_LIMIT)


def _rms(x, g):
    ms = jnp.mean(x * x, axis=-1, keepdims=True)
    return x * lax.rsqrt(ms + RMS_EPS) * g


def _norm_proj_kernel(combine, n_w, plan, *refs):
    it = iter(refs)
    x_ref = next(it)
    if combine:
        m0_ref, m1_ref, gt_ref = next(it), next(it), next(it)
    g_ref = next(it)
    w_refs = [next(it) for _ in range(n_w)]
    o_refs = [next(it) for _ in plan]
    h_ref = next(it) if combine else None
    res_refs = [next(it) for _ in range(n_w)]

    x = x_ref[...]
    if combine:
        x = _moe_combine(x, m0_ref, m1_ref, gt_ref)
        h_ref[...] = x
    xn = _rms(x, g_ref[...]).astype(BF16)
    for w_ref, res_ref in zip(w_refs, res_refs):
        res_ref[...] = jnp.dot(xn, w_ref[...], preferred_element_type=F32)
    for o_ref, (wi, start, width, _) in zip(o_refs, plan):
        o_ref[...] = res_refs[wi][:, start:start + width].astype(o_ref.dtype)


def _norm_proj(x, g, weights, plan, *, tm, moe=None):
    t, d = x.shape
    tm = min(tm, t)
    combine = moe is not None
    row = lambda i: (i, 0)
    const = lambda i: (0, 0)
    in_specs = [pl.BlockSpec((tm, d), row)]
    args = [x]
    if combine:
        m, gates = moe
        nc = d // LANES
        in_specs += [pl.BlockSpec((tm * nc, LANES), row),
                     pl.BlockSpec((tm * nc, LANES), lambda i: (t // tm + i, 0)),
                     pl.BlockSpec((tm, LANES), row)]
        args += [m, m, gates]
    in_specs.append(pl.BlockSpec((1, d), const))
    args.append(g.reshape(1, d))
    for w in weights:
        in_specs.append(pl.BlockSpec(w.shape, const, pipeline_mode=pl.Buffered(1)))
        args.append(w)
    out_shape = [jax.ShapeDtypeStruct((t, width), dtype) for _, _, width, dtype in plan]
    out_specs = [pl.BlockSpec((tm, width), row) for _, _, width, _ in plan]
    if combine:
        out_shape.append(jax.ShapeDtypeStruct((t, d), F32))
        out_specs.append(pl.BlockSpec((tm, d), row))
    outs = pl.pallas_call(
        functools.partial(_norm_proj_kernel, combine, len(weights), plan),
        grid=(t // tm,),
        in_specs=in_specs, out_specs=out_specs, out_shape=out_shape,
        scratch_shapes=[pltpu.VMEM((tm, w.shape[1]), F32) for w in weights],
        compiler_params=_cparams(("parallel",)),
        name="norm_proj",
    )(*args)
    outs = list(outs)
    h = outs.pop() if combine else None
    return outs, h


def _swa_kernel(sink_ref, q_ref, kp_ref, kc_ref, vp_ref, vc_ref, o_ref):
    n = pl.program_id(1)
    blk = A_WINDOW
    row = lax.broadcasted_iota(jnp.int32, (blk, 2 * blk), 0)
    col = lax.broadcasted_iota(jnp.int32, (blk, 2 * blk), 1)
    dist = blk + row - col
    valid = (dist >= 0) & (dist < A_WINDOW) & ((col >= blk) | (n > 0))
    distf = dist.astype(F32)
    outs = []
    for kh in range(A_KV_HEADS):
        ks = slice(kh * HEAD_DIM, (kh + 1) * HEAD_DIM)
        k = jnp.concatenate([kp_ref[0, :, ks], kc_ref[0, :, ks]], axis=0)
        v = jnp.concatenate([vp_ref[0, :, ks], vc_ref[0, :, ks]], axis=0)
        for gi in range(A_GROUP):
            h = kh * A_GROUP + gi
            slope = float(2.0 ** (-8.0 * (h + 1) / A_Q_HEADS))
            q = q_ref[0, :, h * HEAD_DIM:(h + 1) * HEAD_DIM]
            s = lax.dot_general(q, k, (((1,), (1,)), ((), ())), preferred_element_type=F32)
            s = s * (HEAD_DIM ** -0.5) - slope * distf
            s = jnp.where(valid, s, -jnp.inf)
            sink = sink_ref[h]
            m = jnp.maximum(jnp.max(s, axis=-1, keepdims=True), sink)
            p = jnp.exp(s - m)
            denom = jnp.sum(p, axis=-1, keepdims=True) + jnp.exp(sink - m)
            o = jnp.dot(p.astype(BF16), v, preferred_element_type=F32)
            outs.append(o / denom)
    o_ref[0] = jnp.concatenate(outs, axis=-1).astype(o_ref.dtype)


def _swa(proj, sinks):
    b, s, _ = proj.shape
    blk = A_WINDOW
    qw = A_Q_HEADS * HEAD_DIM
    kw = A_KV_HEADS * HEAD_DIM
    q_blk = 0
    k_blk = qw // kw
    v_blk = k_blk + 1
    prev = lambda i, n: jnp.maximum(n - 1, 0)
    return pl.pallas_call(
        _swa_kernel,
        grid=(b, s // blk),
        in_specs=[
            pl.BlockSpec(memory_space=pltpu.SMEM),
            pl.BlockSpec((1, blk, qw), lambda i, n: (i, n, q_blk)),
            pl.BlockSpec((1, blk, kw), lambda i, n: (i, prev(i, n), k_blk)),
            pl.BlockSpec((1, blk, kw), lambda i, n: (i, n, k_blk)),
            pl.BlockSpec((1, blk, kw), lambda i, n: (i, prev(i, n), v_blk)),
            pl.BlockSpec((1, blk, kw), lambda i, n: (i, n, v_blk)),
        ],
        out_specs=pl.BlockSpec((1, blk, qw), lambda i, n: (i, n, 0)),
        out_shape=jax.ShapeDtypeStruct((b, s, qw), BF16),
        compiler_params=_cparams(("parallel", "parallel")),
        name="swa",
    )(sinks.astype(F32), proj, proj, proj, proj, proj)


def _tril(n, dtype=F32):
    r = lax.broadcasted_iota(jnp.int32, (n, n), 0)
    c = lax.broadcasted_iota(jnp.int32, (n, n), 1)
    return (c <= r).astype(dtype)


def _split3(x):
    hi = x.astype(BF16)
    r = x - hi.astype(F32)
    mid = r.astype(BF16)
    return hi, mid, (r - mid.astype(F32)).astype(BF16)


def _dot_mask_lhs(mask, x):
    return sum(jnp.dot(mask, part, preferred_element_type=F32) for part in _split3(x))


def _dot_mask_rhs(x, mask):
    return sum(jnp.dot(part, mask, preferred_element_type=F32) for part in _split3(x))


def _fox_gate_kernel(f_ref, b_ref, c_ref, ct_ref):
    tri = _tril(LANES, BF16)
    carry = jnp.zeros((1, LANES), F32)
    for n in range(f_ref.shape[1] // LANES):
        rows = slice(n * LANES, (n + 1) * LANES)
        lf = jax.nn.log_sigmoid(f_ref[0, rows, :] + b_ref[...])
        cs = _dot_mask_lhs(tri, lf) + carry
        carry = cs[LANES - 1:LANES, :]
        c_ref[0, rows, :] = cs
        ct_ref[0, n] = cs.T[:B_HEADS, :]


def _fox_gate(f_aux, bias):
    b, s, _ = f_aux.shape
    nb = s // LANES
    bias_p = jnp.zeros((1, LANES), F32).at[0, :B_HEADS].set(bias.astype(F32))
    return pl.pallas_call(
        _fox_gate_kernel,
        grid=(b,),
        in_specs=[pl.BlockSpec((1, s, LANES), lambda i: (i, 0, 0)),
                  pl.BlockSpec((1, LANES), lambda i: (0, 0))],
        out_specs=[pl.BlockSpec((1, s, LANES), lambda i: (i, 0, 0)),
                   pl.BlockSpec((1, nb, B_HEADS, LANES), lambda i: (i, 0, 0, 0))],
        out_shape=[jax.ShapeDtypeStruct((b, s, LANES), F32),
                   jax.ShapeDtypeStruct((b, nb, B_HEADS, LANES), F32)],
        compiler_params=_cparams(("parallel",)),
        name="fox_gate",
    )(f_aux, bias_p)


def _fox_kernel(q0_ref, q1_ref, k0_ref, k1_ref, vt_ref, c_ref, ctq_ref, o_ref, *, tq, heads_per_step):
    qi = pl.program_id(1)
    sub = tq // LANES
    key = lax.broadcasted_iota(jnp.int32, (tq, tq), 0)
    qry = lax.broadcasted_iota(jnp.int32, (tq, tq), 1)
    causal = key <= qry
    nt = (((1,), (1,)), ((), ()))
    half = B_HEADS // 2
    q_refs, k_refs = (q0_ref, q1_ref), (k0_ref, k1_ref)
    outs = []
    for h0 in range(0, B_HEADS, heads_per_step):
        heads = list(range(h0, h0 + heads_per_step))
        hsl = [slice(h * HEAD_DIM, (h + 1) * HEAD_DIM) for h in heads]
        lsl = [slice((h % half) * HEAD_DIM, (h % half + 1) * HEAD_DIM) for h in heads]
        qs = [q_refs[h // half][0, :, ls] * (HEAD_DIM ** -0.5)
              for h, ls in zip(heads, lsl)]
        cqs = [jnp.concatenate([ctq_ref[0, u, h:h + 1, :] for u in range(sub)], axis=1) for h in heads]

        def step(j, carry, masked, heads=heads, hsl=hsl, lsl=lsl, qs=qs, cqs=cqs):
            start = pl.multiple_of(j * tq, tq)
            sts = [lax.dot_general(k_refs[h // half][0, pl.ds(start, tq), ls], q, nt,
                                   preferred_element_type=F32)
                   for h, ls, q in zip(heads, lsl, qs)]
            ps, stats = [], []
            for idx, h in enumerate(heads):
                m, l, _ = carry[3 * idx:3 * idx + 3]
                ck = c_ref[0, pl.ds(start, tq), h:h + 1]
                st = (sts[idx] - ck) + cqs[idx]
                if masked:
                    st = jnp.where(causal, st, -jnp.inf)
                m_new = jnp.maximum(m, jnp.max(st, axis=0, keepdims=True))
                alpha = jnp.exp(m - m_new)
                p = jnp.exp(st - m_new)
                stats.append((m_new, alpha, alpha * l + jnp.sum(p, axis=0, keepdims=True)))
                ps.append(p.astype(BF16))
            new = []
            for idx in range(len(heads)):
                m_new, alpha, l = stats[idx]
                pv = jnp.dot(vt_ref[0, j, hsl[idx], :], ps[idx], preferred_element_type=F32)
                new += [m_new, l, alpha * carry[3 * idx + 2] + pv]
            return tuple(new)

        init = (jnp.full((1, tq), -jnp.inf, F32), jnp.zeros((1, tq), F32),
                jnp.zeros((HEAD_DIM, tq), F32)) * heads_per_step
        carry = lax.fori_loop(0, qi, functools.partial(step, masked=False), init)
        carry = step(qi, carry, True)
        for idx in range(heads_per_step):
            outs.append(carry[3 * idx + 2] / carry[3 * idx + 1])
    o_ref[0] = jnp.concatenate(outs, axis=0).T.astype(o_ref.dtype)


def _fox(proj, c, ct, *, tq=256, heads_per_step=8):
    b, s, _ = proj.shape
    w = B_HEADS * HEAD_DIM
    nk = s // tq
    sub = tq // LANES
    hw = w // 2
    base = (A_Q_HEADS + 2 * A_KV_HEADS) * HEAD_DIM
    qb, kb = base // hw, (base + w) // hw
    v_t = proj[:, :, base + 2 * w:base + 3 * w].reshape(b, nk, tq, w).transpose(0, 1, 3, 2)
    return pl.pallas_call(
        functools.partial(_fox_kernel, tq=tq, heads_per_step=heads_per_step),
        grid=(b, s // tq),
        in_specs=[
            pl.BlockSpec((1, tq, hw), lambda i, n: (i, n, qb)),
            pl.BlockSpec((1, tq, hw), lambda i, n: (i, n, qb + 1)),
            pl.BlockSpec((1, s, hw), lambda i, n: (i, 0, kb)),
            pl.BlockSpec((1, s, hw), lambda i, n: (i, 0, kb + 1)),
            pl.BlockSpec((1, nk, w, tq), lambda i, n: (i, 0, 0, 0)),
            pl.BlockSpec((1, s, LANES), lambda i, n: (i, 0, 0)),
            pl.BlockSpec((1, sub, B_HEADS, LANES), lambda i, n: (i, n, 0, 0)),
        ],
        out_specs=pl.BlockSpec((1, tq, w), lambda i, n: (i, n, 0)),
        out_shape=jax.ShapeDtypeStruct((b, s, w), BF16),
        compiler_params=_cparams(("parallel", "parallel")),
        name="fox",
    )(proj, proj, proj, proj, v_t, c, ct)


def _ssd_gla_kernel(z_ref, q_ref, k_ref, v_ref, r_ref, xc_ref, xp_ref, sdt_ref, sg_ref,
                    cw_ref, cb_ref, dtb_ref, alog_ref, dsk_ref, ex_ref, sn_ref, gkw_ref, gkb_ref, gn_ref,
                    o_ref, conv_ref, hs_ref, gs_ref):
    c = pl.program_id(1)
    q_len = C_CHUNK
    halo = 8

    @pl.when(c == 0)
    def _():
        hs_ref[...] = jnp.zeros_like(hs_ref)
        gs_ref[...] = jnp.zeros_like(gs_ref)

    prev = xp_ref[0]
    conv_ref[0:halo, :] = jnp.where(c > 0, prev, jnp.zeros_like(prev))
    conv_ref[halo:halo + q_len, :] = xc_ref[0]
    acc = jnp.zeros((q_len, C_CONV_DIM), F32) + cb_ref[...]
    for j in range(C_CONV):
        off = halo - (C_CONV - 1) + j
        acc = acc + cw_ref[j:j + 1, :] * conv_ref[off:off + q_len, :]
    xbc = jax.nn.silu(acc)
    xs = xbc[:, :C_INNER]
    gs_w = C_GROUPS * C_STATE
    bm = xbc[:, C_INNER:C_INNER + gs_w].astype(BF16)
    cm = xbc[:, C_INNER + gs_w:].astype(BF16)

    row = lax.broadcasted_iota(jnp.int32, (q_len, q_len), 0)
    col = lax.broadcasted_iota(jnp.int32, (q_len, q_len), 1)
    tri = col <= row

    lane = lax.broadcasted_iota(jnp.int32, (1, LANES), 1)
    dt = jnp.where(lane < C_HEADS, jax.nn.softplus(sdt_ref[0] + dtb_ref[...]), 0.0)
    dta = dt * -jnp.exp(alog_ref[...])
    acs = _dot_mask_lhs(tri.astype(BF16), dta)
    acs_t = acs.T
    chunk_dec = jnp.exp(acs[q_len - 1:q_len, :])
    expand = ex_ref[...]
    dt_x = _dot_mask_rhs(dt, expand)
    acs_x = _dot_mask_rhs(acs, expand)
    xd = xs * dt_x
    xd_b = xd.astype(BF16)
    xdd = xd * jnp.exp(acs_x[q_len - 1:q_len, :] - acs_x)
    low_half = lax.broadcasted_iota(jnp.int32, (q_len, LANES), 1) < C_HEAD_DIM

    y_pairs, y_offs = [], []
    tdims = (((1,), (1,)), ((), ()))
    for g in range(C_GROUPS):
        b_g = bm[:, g * C_STATE:(g + 1) * C_STATE]
        c_g = cm[:, g * C_STATE:(g + 1) * C_STATE]
        cb = lax.dot_general(c_g, b_g, tdims, preferred_element_type=F32)
        h0 = g * C_HPG
        grp = slice(h0 * C_HEAD_DIM, (h0 + C_HPG) * C_HEAD_DIM)
        y_offs.append(lax.dot_general(c_g, hs_ref[grp, :].astype(BF16), tdims, preferred_element_type=F32))
        for h in range(h0, h0 + C_HPG, 2):
            xp = xd_b[:, h * C_HEAD_DIM:(h + 2) * C_HEAD_DIM]
            halves = []
            for hh in (h, h + 1):
                seg = jnp.exp(jnp.where(tri, acs[:, hh:hh + 1] - acs_t[hh:hh + 1, :], -jnp.inf))
                halves.append(jnp.dot((cb * seg).astype(BF16), xp, preferred_element_type=F32))
            y_pairs.append(jnp.where(low_half, halves[0], halves[1]))
        upd = jnp.dot(xdd[:, grp].T.astype(BF16), b_g, preferred_element_type=F32)
        for hh in range(C_HPG):
            h = h0 + hh
            ps = slice(h * C_HEAD_DIM, (h + 1) * C_HEAD_DIM)
            us = slice(hh * C_HEAD_DIM, (hh + 1) * C_HEAD_DIM)
            hs_ref[ps, :] = hs_ref[ps, :] * chunk_dec[0:1, h:h + 1] + upd[us, :]
    y = (jnp.concatenate(y_pairs, axis=1) + jnp.concatenate(y_offs, axis=1) * jnp.exp(acs_x)
         + dsk_ref[...] * xs)
    y = y * jax.nn.silu(z_ref[0].astype(F32))
    o_ref[0, :, :C_INNER] = _rms(y, sn_ref[...]).astype(o_ref.dtype)

    same = (row // D_CHUNK) == (col // D_CHUNK)
    tri2 = tri & same
    la = jnp.dot(sg_ref[0].astype(BF16), gkw_ref[...], preferred_element_type=F32) + gkb_ref[...]
    la = jax.nn.log_sigmoid(la) / D_GATE_NORM
    gcs = _dot_mask_lhs(tri2.astype(BF16), la)
    first = lax.broadcasted_iota(jnp.int32, (q_len, 1), 0) < D_CHUNK
    r_all = r_ref[0]
    for h in range(D_HEADS):
        ks = slice(h * D_HK, (h + 1) * D_HK)
        vs = slice(h * D_HV, (h + 1) * D_HV)
        g_h = gcs[:, ks]
        g_end0 = g_h[D_CHUNK - 1:D_CHUNK, :]
        g_end1 = g_h[q_len - 1:q_len, :]
        q_h = q_ref[0, :, ks].astype(F32) * (D_HK ** -0.5)
        k_h = k_ref[0, :, ks].astype(F32)
        v_h = v_ref[0, :, vs]
        q_dec = (q_h * jnp.exp(g_h)).astype(BF16)
        k_inv = (k_h * jnp.exp(-g_h)).astype(BF16)
        k_end = k_h * jnp.exp(jnp.where(first, g_end0, g_end1) - g_h)
        ke0 = jnp.where(first, k_end, 0.0).astype(BF16)
        ke1 = jnp.where(first, 0.0, k_end).astype(BF16)
        attn = lax.dot_general(q_dec, k_inv, (((1,), (1,)), ((), ())), preferred_element_type=F32)
        attn = jnp.where(tri2, attn, 0.0).astype(BF16)
        o = jnp.dot(attn, v_h, preferred_element_type=F32)
        v_t = v_h.astype(F32).T.astype(BF16)
        st_rows = slice(h * D_HV, (h + 1) * D_HV)
        s0 = gs_ref[st_rows, :]
        s1 = s0 * jnp.exp(g_end0) + jnp.dot(v_t, ke0, preferred_element_type=F32)
        s2 = s1 * jnp.exp(g_end1) + jnp.dot(v_t, ke1, preferred_element_type=F32)
        gs_ref[st_rows, :] = s2
        tdims = (((1,), (1,)), ((), ()))
        o0 = lax.dot_general(q_dec, s0.astype(BF16), tdims, preferred_element_type=F32)
        o1 = lax.dot_general(q_dec, s1.astype(BF16), tdims, preferred_element_type=F32)
        o = o + jnp.where(first, o0, o1)
        o = _rms(o, gn_ref[...]) * jax.nn.silu(r_all[:, vs].astype(F32))
        o_ref[0, :, C_INNER + h * D_HV:C_INNER + (h + 1) * D_HV] = o.astype(o_ref.dtype)


def _ssd_gla(z, q, k, v, r, xbc, side_dt, side_g, g_lane, p):
    b, s, _ = z.shape
    q_len = C_CHUNK
    chunk = lambda width: pl.BlockSpec((1, q_len, width), lambda i, n: (i, n, 0))
    full = lambda shape: pl.BlockSpec(shape, lambda i, n: (0,) * len(shape))
    pad_lanes = lambda vec: jnp.zeros((1, LANES), F32).at[0, :vec.shape[0]].set(vec.astype(F32))
    gkw = jnp.zeros((LANES, D_KEY), F32).at[g_lane:g_lane + D_GATE_RANK].set(p["gk_w"]).astype(BF16)
    expand = jnp.asarray(np.arange(C_INNER)[None, :] // C_HEAD_DIM == np.arange(LANES)[:, None], BF16)
    return pl.pallas_call(
        _ssd_gla_kernel,
        grid=(b, s // q_len),
        in_specs=[
            chunk(C_INNER), chunk(D_KEY), chunk(D_KEY), chunk(D_VAL), chunk(D_VAL), chunk(C_CONV_DIM),
            pl.BlockSpec((1, 8, C_CONV_DIM), lambda i, n: (i, jnp.maximum(n * (q_len // 8) - 1, 0), 0)),
            chunk(LANES), chunk(LANES),
            full((C_CONV, C_CONV_DIM)), full((1, C_CONV_DIM)),
            full((1, LANES)), full((1, LANES)), full((1, C_INNER)), full((LANES, C_INNER)),
            full((1, C_INNER)), full((LANES, D_KEY)), full((1, D_KEY)), full((1, D_HV)),
        ],
        out_specs=pl.BlockSpec((1, q_len, C_INNER + D_VAL), lambda i, n: (i, n, 0)),
        out_shape=jax.ShapeDtypeStruct((b, s, C_INNER + D_VAL), BF16),
        scratch_shapes=[pltpu.VMEM((8 + q_len, C_CONV_DIM), F32),
                        pltpu.VMEM((C_INNER, C_STATE), F32),
                        pltpu.VMEM((D_VAL, D_HK), F32)],
        compiler_params=_cparams(("parallel", "arbitrary")),
        name="ssd_gla",
    )(z, q, k, v, r, xbc, xbc, side_dt, side_g,
      p["conv_w"].astype(F32), p["conv_b"].reshape(1, -1).astype(F32),
      pad_lanes(p["dt_bias"]), pad_lanes(p["a_log"]),
      jnp.repeat(p["d_skip"].astype(F32), C_HEAD_DIM).reshape(1, C_INNER), expand,
      p["ssd_norm"].reshape(1, -1).astype(F32), gkw, p["gk_b"].reshape(1, -1).astype(F32),
      p["gla_norm"].reshape(1, -1).astype(F32))


def _store_rows_tiled(ref, val):
    m, d = val.shape
    nc = d // LANES
    for c in range(nc):
        ref[pl.ds(c, m, stride=nc), :] = val[:, c * LANES:(c + 1) * LANES]


def _load_rows_tiled(ref, m, dtype=None):
    nc = ref.shape[0] // m
    parts = [ref[pl.ds(c, m, stride=nc), :] for c in range(nc)]
    if dtype is not None:
        parts = [p.astype(dtype) for p in parts]
    return jnp.concatenate(parts, axis=1)


RT_GATE, RT_EXPERT, RT_RANK = 0, 2, 4


def _route_block(lg, carry):
    m = lg.shape[0]
    lane = lax.broadcasted_iota(jnp.int32, (m, LANES), 1)
    lane_f = lane.astype(F32)
    none = float(LANES)
    neg = -jnp.inf
    first_max = lambda v, vmax: jnp.min(jnp.where(v == vmax, lane_f, none), axis=-1, keepdims=True)
    gl = jnp.where(lane < MOE_GROUPS, lg, neg)
    gmax = jnp.max(gl, axis=-1, keepdims=True)
    g_w = 1.0 / jnp.sum(jnp.exp(gl - gmax), axis=-1, keepdims=True)
    lo = MOE_GROUPS + first_max(gl, gmax) * MOE_EPG
    el = jnp.where((lane_f >= lo) & (lane_f < lo + MOE_EPG), lg, neg)
    emax = jnp.max(el, axis=-1, keepdims=True)
    esum = jnp.sum(jnp.exp(el - emax), axis=-1, keepdims=True)
    l0 = first_max(el, emax)
    el2 = jnp.where(lane_f == l0, neg, el)
    emax2 = jnp.max(el2, axis=-1, keepdims=True)
    l1 = first_max(el2, emax2)
    p0 = 1.0 / esum
    p1 = jnp.exp(emax2 - emax) / esum
    w0 = g_w * (p0 / (p0 + p1))
    w1 = g_w * (p1 / (p0 + p1))
    oh0 = lane_f == l0
    oh1 = lane_f == l1
    oh = (oh0 | oh1).astype(BF16)
    r = lax.broadcasted_iota(jnp.int32, (m, m), 0)
    c = lax.broadcasted_iota(jnp.int32, (m, m), 1)
    cum = jnp.dot((c < r).astype(BF16), oh, preferred_element_type=F32) + carry
    rank0 = jnp.sum(jnp.where(oh0, cum, 0.0), axis=-1, keepdims=True)
    rank1 = jnp.sum(jnp.where(oh1, cum, 0.0), axis=-1, keepdims=True)
    carry = carry + jnp.sum(oh.astype(F32), axis=0, keepdims=True)
    rec = jnp.zeros((m, LANES), F32)
    for pos, val in ((RT_GATE, w0), (RT_GATE + 1, w1), (RT_EXPERT, l0 - MOE_GROUPS),
                     (RT_EXPERT + 1, l1 - MOE_GROUPS), (RT_RANK, rank0), (RT_RANK + 1, rank1)):
        rec = jnp.where(lane == pos, val, rec)
    return rec, carry


def _out_proj_kernel(n_parts, *refs):
    a_refs = refs[:n_parts]
    w_refs = refs[n_parts:2 * n_parts]
    h_ref, g_ref, wr_ref, br_ref, ho_ref, xt_ref, rt_ref, cnt_ref, carry_ref = refs[2 * n_parts:]

    @pl.when(pl.program_id(0) == 0)
    def _():
        carry_ref[...] = jnp.zeros_like(carry_ref)

    acc = h_ref[...]
    for a_ref, w_ref in zip(a_refs, w_refs):
        acc = acc + jnp.dot(a_ref[...], w_ref[...], preferred_element_type=F32)
    ho_ref[...] = acc
    xn = _rms(acc, g_ref[...])
    _store_rows_tiled(xt_ref, xn)
    x_hi, x_mid, _ = _split3(xn)
    wr = wr_ref[...]
    lg2 = jnp.dot(x_hi, wr, preferred_element_type=F32)
    lg = (lg2[:, :LANES] + lg2[:, LANES:] + jnp.dot(x_mid, wr[:, :LANES], preferred_element_type=F32)
          + br_ref[...])
    rec, carry = _route_block(lg, carry_ref[...])
    rt_ref[...] = rec
    carry_ref[...] = carry
    cnt_ref[0] = carry


def _out_proj(parts, w_parts, h, g, w_route, b_route, *, tm=512):
    t, d = h.shape
    tm = min(tm, t)
    nc = d // LANES
    row = lambda i: (i, 0)
    const = lambda i: (0, 0)
    in_specs = [pl.BlockSpec((tm, a.shape[1]), row) for a in parts]
    in_specs += [pl.BlockSpec(w.shape, const) for w in w_parts]
    in_specs += [pl.BlockSpec((tm, d), row), pl.BlockSpec((1, d), const),
                 pl.BlockSpec((d, 2 * LANES), const), pl.BlockSpec((1, LANES), const)]
    return pl.pallas_call(
        functools.partial(_out_proj_kernel, len(parts)),
        grid=(t // tm,),
        in_specs=in_specs,
        out_specs=[pl.BlockSpec((tm, d), row), pl.BlockSpec((tm * nc, LANES), row),
                   pl.BlockSpec((tm, LANES), row), pl.BlockSpec((1, 1, LANES), lambda i: (i, 0, 0))],
        out_shape=[jax.ShapeDtypeStruct((t, d), F32), jax.ShapeDtypeStruct((t * nc, LANES), F32),
                   jax.ShapeDtypeStruct((t, LANES), F32), jax.ShapeDtypeStruct((t // tm, 1, LANES), F32)],
        scratch_shapes=[pltpu.VMEM((1, LANES), F32)],
        compiler_params=_cparams(("arbitrary",)),
        name="out_proj",
    )(*parts, *w_parts, h, g.reshape(1, d), w_route, b_route)


def _zero_fill_rows(rows_ref, z_ref, zsem, lo_ref, hi_ref, n_rows):
    zb = z_ref.shape[0]
    z_ref[...] = jnp.zeros_like(z_ref)
    bits = [1 << b for b in reversed(range(MOE_TILE.bit_length() - 1))]
    assert bits[0] <= zb and MOE_TILE % zb == 0

    def piece(row0, n):
        return pltpu.make_async_copy(z_ref.at[pl.ds(0, n)], rows_ref.at[pl.ds(row0, n)], zsem)

    def sweep(issue):
        def per_expert(e, carry):
            off = lo_ref[e]
            run = hi_ref[e] - off
            for n in bits:
                hit = (run & n) != 0

                @pl.when(hit)
                def _(off=off, n=n):
                    piece(off, n).start() if issue else piece(off, n).wait()

                off = off + jnp.where(hit, n, 0)
            return carry

        def per_block(i, carry):
            piece(i * zb, zb).start() if issue else piece(i * zb, zb).wait()
            return carry

        lax.fori_loop(0, MOE_EXPERTS, per_expert, 0)
        lax.fori_loop(hi_ref[MOE_EXPERTS - 1] // zb, n_rows // zb, per_block, 0)

    sweep(True)
    sweep(False)


def _moe_dispatch_kernel(e_ref, r_ref, st_ref, lo_ref, hi_ref, x_ref, rows_ref, z_ref, sem, zsem, *, tm, n_tok):
    t0 = pl.program_id(0) * tm

    @pl.when(pl.program_id(0) == 0)
    def _():
        _zero_fill_rows(rows_ref, z_ref, zsem, lo_ref, hi_ref, rows_ref.shape[0])

    def copy(r, row):
        return pltpu.make_async_copy(x_ref.at[r], rows_ref.at[row], sem)

    def start(r, carry):
        for k in range(MOE_TOPK):
            a = k * n_tok + t0 + r
            copy(r, st_ref[e_ref[a]] + r_ref[a]).start()
        return carry

    def wait(r, carry):
        for _ in range(MOE_TOPK):
            copy(0, 0).wait()
        return carry

    lax.fori_loop(0, tm, start, 0, unroll=8)
    lax.fori_loop(0, tm, wait, 0, unroll=8)


def _moe_dispatch(x3, expert, rank, starts, pad_lo, pad_hi, n_rows, *, tm=512):
    t = x3.shape[0]
    tm = min(tm, t)
    return pl.pallas_call(
        functools.partial(_moe_dispatch_kernel, tm=tm, n_tok=t),
        grid_spec=pltpu.PrefetchScalarGridSpec(
            num_scalar_prefetch=5, grid=(t // tm,),
            in_specs=[pl.BlockSpec((tm,) + x3.shape[1:], lambda i, *_: (i, 0, 0))],
            out_specs=pl.BlockSpec(memory_space=pl.ANY),
            scratch_shapes=[pltpu.VMEM((MOE_TILE // 2,) + x3.shape[1:], x3.dtype),
                            pltpu.SemaphoreType.DMA(()), pltpu.SemaphoreType.DMA(())]),
        out_shape=jax.ShapeDtypeStruct((n_rows,) + x3.shape[1:], x3.dtype),
        compiler_params=pltpu.CompilerParams(dimension_semantics=("arbitrary",), has_side_effects=True),
        name="moe_dispatch",
    )(expert, rank, starts, pad_lo, pad_hi, x3)


def _moe_gather_kernel(e_ref, r_ref, st_ref, rows_ref, o_ref, sem, *, tm):
    a0 = pl.program_id(0) * tm

    def copy(row, r):
        return pltpu.make_async_copy(rows_ref.at[row], o_ref.at[r], sem)

    def start(r, carry):
        a = a0 + r
        copy(st_ref[e_ref[a]] + r_ref[a], r).start()
        return carry

    def wait(r, carry):
        copy(0, 0).wait()
        return carry

    lax.fori_loop(0, tm, start, 0, unroll=8)
    lax.fori_loop(0, tm, wait, 0, unroll=8)


def _moe_gather(rows3, expert, rank, starts, *, tm=1024):
    n = expert.shape[0]
    tm = min(tm, n)
    return pl.pallas_call(
        functools.partial(_moe_gather_kernel, tm=tm),
        grid_spec=pltpu.PrefetchScalarGridSpec(
            num_scalar_prefetch=3, grid=(n // tm,),
            in_specs=[pl.BlockSpec(memory_space=pl.ANY)],
            out_specs=pl.BlockSpec((tm,) + rows3.shape[1:], lambda i, e, r, s: (i, 0, 0)),
            scratch_shapes=[pltpu.SemaphoreType.DMA(())]),
        out_shape=jax.ShapeDtypeStruct((n,) + rows3.shape[1:], rows3.dtype),
        compiler_params=pltpu.CompilerParams(dimension_semantics=("arbitrary",)),
        name="moe_gather",
    )(expert, rank, starts, rows3)


def _moe_kernel(te_ref, nu_ref, x_ref, wg_ref, wu_ref, wd_ref, y_ref):
    i = pl.program_id(0)

    @pl.when(i >= nu_ref[0])
    def _():
        y_ref[...] = jnp.zeros_like(y_ref)

    @pl.when(i < nu_ref[0])
    def _():
        x = _load_rows_tiled(x_ref, MOE_TILE, BF16)
        gate = jnp.dot(x, wg_ref[...].astype(BF16), preferred_element_type=F32)
        up = jnp.dot(x, wu_ref[...].astype(BF16), preferred_element_type=F32)
        act = (jax.nn.silu(gate) * up).astype(BF16)
        y = jnp.dot(act, wd_ref[...].astype(BF16), preferred_element_type=F32)
        _store_rows_tiled(y_ref, y)


def _moe_experts(x_rows, tile_expert, n_used, w_gate, w_up, w_down, layer):
    d, ff = w_gate.shape[-2:]
    nc = d // LANES
    n_tiles = x_rows.shape[0] // (MOE_TILE * nc)
    live = lambda i, nu: jnp.minimum(i, nu[0] - 1)
    w_spec = lambda a, b: pl.BlockSpec((None, None, a, b), lambda i, te, nu: (layer, te[live(i, nu)], 0, 0))
    return pl.pallas_call(
        _moe_kernel,
        grid_spec=pltpu.PrefetchScalarGridSpec(
            num_scalar_prefetch=2, grid=(n_tiles,),
            in_specs=[pl.BlockSpec((MOE_TILE * nc, LANES), lambda i, te, nu: (live(i, nu), 0)),
                      w_spec(d, ff), w_spec(d, ff), w_spec(ff, d)],
            out_specs=pl.BlockSpec((MOE_TILE * nc, LANES), lambda i, te, nu: (i, 0))),
        out_shape=jax.ShapeDtypeStruct(x_rows.shape, F32),
        compiler_params=_cparams(("arbitrary",)),
        name="moe_experts",
    )(tile_expert, n_used, x_rows, w_gate, w_up, w_down)


def _moe_combine(h, m0_ref, m1_ref, rt_ref):
    rt = rt_ref[...]
    tm = h.shape[0]
    return (h + _load_rows_tiled(m0_ref, tm) * rt[:, RT_GATE:RT_GATE + 1]
            + _load_rows_tiled(m1_ref, tm) * rt[:, RT_GATE + 1:RT_GATE + 2])


def _run_pieces(run, max_rows):
    return [(n, (run & n) != 0) for n in (1 << b for b in reversed(range(max_rows.bit_length())))]


def _moe_combine_kernel(src_ref, len_ref, h_ref, rt_ref, pb_ref, g_ref, y_hbm, o_ref, ybuf, sem, *, tm, nc, final):
    i = pl.program_id(0)
    n_steps = pl.num_programs(0)
    rows = MOE_TOPK * tm

    def runs(tile, slot, issue):
        def per_expert(e, carry):
            seg = tile * MOE_EXPERTS + e
            src = src_ref[seg]
            dst = jnp.int32(0) + carry
            for n, hit in _run_pieces(len_ref[seg], rows):
                @pl.when(hit)
                def _(src=src, dst=dst, n=n):
                    cp = pltpu.make_async_copy(y_hbm.at[pl.ds(src * nc, n * nc)],
                                               ybuf.at[slot, pl.ds(dst * nc, n * nc)], sem.at[slot])
                    cp.start() if issue else cp.wait()

                step = jnp.where(hit, n, 0)
                src, dst = src + step, dst + step
            return dst

        lax.fori_loop(0, MOE_EXPERTS, per_expert, jnp.int32(0))

    slot = lax.rem(i, 2)

    @pl.when(i == 0)
    def _():
        runs(0, 0, True)

    @pl.when(i + 1 < n_steps)
    def _():
        runs(i + 1, 1 - slot, True)

    runs(i, slot, False)
    y = _load_rows_tiled(ybuf.at[slot], rows, BF16)
    rt = rt_ref[...]
    lane_f = lax.broadcasted_iota(jnp.int32, (tm, LANES), 1).astype(F32)
    pos_f = lax.broadcasted_iota(jnp.int32, (tm, rows), 1).astype(F32)
    out = h_ref[...]
    for k in range(MOE_TOPK):
        e_lane = rt[:, RT_EXPERT + k:RT_EXPERT + k + 1] + MOE_GROUPS
        pos = rt[:, RT_RANK + k:RT_RANK + k + 1] + jnp.sum(jnp.where(lane_f == e_lane, pb_ref[0], 0.0),
                                                             axis=-1, keepdims=True)
        pick = (pos_f == pos).astype(BF16)
        out = out + rt[:, RT_GATE + k:RT_GATE + k + 1] * jnp.dot(pick, y, preferred_element_type=F32)
    o_ref[...] = _rms(out, g_ref[...]) if final else out


def _moe_combine_rows(h, y_rows, route, seg_src, seg_len, pos_base, g, *, tm, final):
    t, d = h.shape
    nc = d // LANES
    row = lambda i, *_: (i, 0)
    return pl.pallas_call(
        functools.partial(_moe_combine_kernel, tm=tm, nc=nc, final=final),
        grid_spec=pltpu.PrefetchScalarGridSpec(
            num_scalar_prefetch=2, grid=(t // tm,),
            in_specs=[pl.BlockSpec((tm, d), row), pl.BlockSpec((tm, LANES), row),
                      pl.BlockSpec((1, 1, LANES), lambda i, *_: (i, 0, 0)),
                      pl.BlockSpec((1, d), lambda i, *_: (0, 0)),
                      pl.BlockSpec(memory_space=pl.ANY)],
            out_specs=pl.BlockSpec((tm, d), row),
            scratch_shapes=[pltpu.VMEM((2, MOE_TOPK * tm * nc, LANES), F32), pltpu.SemaphoreType.DMA((2,))]),
        out_shape=jax.ShapeDtypeStruct((t, d), F32),
        compiler_params=_cparams(("arbitrary",)),
        name="moe_combine",
    )(seg_src, seg_len, h, route, pos_base, g.reshape(1, d), y_rows)


def _moe(h, x_tiled, route, tile_counts, w_gate, w_up, w_down, layer, g, *, final):
    t = route.shape[0]
    nc = x_tiled.shape[0] // t
    n_tt = tile_counts.shape[0]
    tm = t // n_tt
    after = tile_counts[:, 0, MOE_GROUPS:MOE_GROUPS + MOE_EXPERTS].astype(jnp.int32)
    before = jnp.concatenate([jnp.zeros((1, MOE_EXPERTS), jnp.int32), after[:-1]], axis=0)
    cnt = after[-1]
    padded = (cnt + MOE_TILE - 1) // MOE_TILE * MOE_TILE
    pad_ends = jnp.cumsum(padded)
    starts = (pad_ends - padded).astype(jnp.int32)
    n_tiles = (t * MOE_TOPK + MOE_EXPERTS * (MOE_TILE - 1)) // MOE_TILE
    tile_start = jnp.arange(n_tiles, dtype=jnp.int32) * MOE_TILE
    tile_expert = jnp.minimum(jnp.sum(tile_start[:, None] >= pad_ends[None, :], axis=1),
                              MOE_EXPERTS - 1).astype(jnp.int32)
    n_used = (pad_ends[-1] // MOE_TILE).astype(jnp.int32).reshape(1)
    n_rows = n_tiles * MOE_TILE
    k_major = lambda lane0: route[:, lane0:lane0 + MOE_TOPK].T.reshape(-1).astype(jnp.int32)
    expert, rank = k_major(RT_EXPERT), k_major(RT_RANK)
    x_rows = _moe_dispatch(x_tiled.reshape(t, nc, LANES), expert, rank, starts, starts + cnt,
                           pad_ends.astype(jnp.int32), n_rows)
    y_rows = _moe_experts(x_rows.reshape(n_rows * nc, LANES), tile_expert, n_used, w_gate, w_up, w_down, layer)
    seg_len = after - before
    seg_off = jnp.cumsum(seg_len, axis=1) - seg_len
    pos_base = jnp.zeros((n_tt, 1, LANES), F32).at[:, 0, MOE_GROUPS:MOE_GROUPS + MOE_EXPERTS].set(
        (seg_off - before).astype(F32))
    return _moe_combine_rows(h, y_rows, route, (starts[None, :] + before).reshape(-1), seg_len.reshape(-1),
                             pos_base, g, tm=tm, final=final)


def _router_weights(w_group, b_group, w_router, b_router):
    d = w_group.shape[0]
    w = jnp.zeros((d, LANES), F32)
    w = w.at[:, :MOE_GROUPS].set(w_group).at[:, MOE_GROUPS:MOE_GROUPS + MOE_EXPERTS].set(w_router)
    b = jnp.zeros((1, LANES), F32)
    b = b.at[0, :MOE_GROUPS].set(b_group).at[0, MOE_GROUPS:MOE_GROUPS + MOE_EXPERTS].set(b_router)
    w_hi = w.astype(BF16)
    w_mid = (w - w_hi.astype(F32)).astype(BF16)
    return jnp.concatenate([w_hi, w_mid], axis=1), b


def kernel(x, norm_mix, norm_moe, norm_final, even_w_in, even_sinks, even_forget_bias, even_w_out,
           odd_w_in, odd_conv_w, odd_conv_b, odd_dt_bias, odd_a_log, odd_d_skip, odd_ssd_norm,
           odd_gk_w, odd_gk_b, odd_gla_norm, odd_w_out, moe_w_group, moe_b_group, moe_w_router,
           moe_b_router, moe_w_gate, moe_w_up, moe_w_down):
    b, s, d = x.shape
    t = b * s
    depth = norm_mix.shape[0]
    h = x.reshape(t, d)
    moe = None
    for layer in range(depth):
        i = layer // 2
        if layer % 2 == 0:
            w = even_w_in[i]
            n_ab = (A_Q_HEADS + 2 * A_KV_HEADS + 3 * B_HEADS) * HEAD_DIM
            w_main = w[:, :n_ab].astype(BF16)
            w_aux = jnp.zeros((d, LANES), F32).at[:, :B_HEADS].set(w[:, n_ab:]).astype(BF16)
            (proj, f_aux), h_new = _norm_proj(h, norm_mix[layer], [w_main, w_aux],
                                              ((0, 0, n_ab, BF16), (1, 0, LANES, F32)), tm=512, moe=moe)
            h = h if h_new is None else h_new
            proj = proj.reshape(b, s, -1)
            out_a = _swa(proj, even_sinks[i])
            c, ct = _fox_gate(f_aux.reshape(b, s, LANES), even_forget_bias[i])
            out_b = _fox(proj, c, ct)
            n_ha = A_Q_HEADS * HEAD_DIM
            w_out = even_w_out[i].astype(BF16)
            parts = [out_a.reshape(t, -1), out_b.reshape(t, -1)]
            w_parts = [w_out[:n_ha], w_out[n_ha:]]
        else:
            w = odd_w_in[i]
            o_z, o_xbc = 0, C_INNER
            o_dt = o_xbc + C_CONV_DIM
            o_q = o_dt + C_HEADS
            o_k = o_q + D_KEY
            o_v = o_k + D_KEY
            o_g = o_v + D_VAL
            o_r = o_g + D_GATE_RANK
            n_b = w.shape[1] - o_dt
            w_a = w[:, :o_dt].astype(BF16)
            w_b = jnp.pad(w[:, o_dt:].astype(BF16), ((0, 0), (0, -n_b % LANES)))
            g_win = (o_g - o_dt) // LANES * LANES
            plan = ((0, o_z, C_INNER, BF16), (1, o_q - o_dt, D_KEY, BF16), (1, o_k - o_dt, D_KEY, BF16),
                    (1, o_v - o_dt, D_VAL, BF16), (1, o_r - o_dt, D_VAL, BF16), (0, o_xbc, C_CONV_DIM, F32),
                    (1, 0, LANES, F32), (1, g_win, LANES, F32))
            outs, h_new = _norm_proj(h, norm_mix[layer], [w_a, w_b], plan, tm=256, moe=moe)
            h = h if h_new is None else h_new
            params = dict(conv_w=odd_conv_w[i], conv_b=odd_conv_b[i], dt_bias=odd_dt_bias[i], a_log=odd_a_log[i],
                          d_skip=odd_d_skip[i], ssd_norm=odd_ssd_norm[i], gk_w=odd_gk_w[i], gk_b=odd_gk_b[i],
                          gla_norm=odd_gla_norm[i])
            mixed = _ssd_gla(*[o.reshape(b, s, -1) for o in outs], o_g - o_dt - g_win, params)
            parts = [mixed.reshape(t, -1)]
            w_parts = [odd_w_out[i].astype(BF16)]
        w_route, b_route = _router_weights(moe_w_group[layer], moe_b_group[layer],
                                           moe_w_router[layer], moe_b_router[layer])
        h, x_tiled, route, tile_counts = _out_proj(parts, w_parts, h, norm_moe[layer], w_route, b_route)
        h = _moe(h, x_tiled, route, tile_counts, moe_w_gate, moe_w_up, moe_w_down, layer, norm_final,
                 final=layer == depth - 1)
    out = h
    return out.reshape(b, s, d)
```

```python
import functools
import math

import numpy as np
import jax
import jax.numpy as jnp
from jax import lax
from jax.experimental import pallas as pl
from jax.experimental.pallas import tpu as pltpu

F32 = jnp.float32
BF16 = jnp.bfloat16
HIGHEST = lax.Precision.HIGHEST

RMS_EPS = 1e-6
HEAD_DIM = 64
A_Q_HEADS = 8
A_KV_HEADS = 2
A_GROUP = A_Q_HEADS // A_KV_HEADS
A_WINDOW = 128
B_HEADS = 8
C_HEADS = 16
C_HEAD_DIM = 64
C_INNER = C_HEADS * C_HEAD_DIM
C_GROUPS = 2
C_HPG = C_HEADS // C_GROUPS
C_STATE = 128
C_CONV = 4
C_CHUNK = 128
C_CONV_DIM = C_INNER + 2 * C_GROUPS * C_STATE
D_HEADS = 4
D_HK = 128
D_HV = 256
D_KEY = D_HEADS * D_HK
D_VAL = D_HEADS * D_HV
D_GATE_RANK = 16
D_GATE_NORM = 16.0
D_CHUNK = 64
MOE_GROUPS = 4
MOE_EPG = 8
MOE_EXPERTS = MOE_GROUPS * MOE_EPG
MOE_TOPK = 2

LANES = 128
VMEM_LIMIT = 48 * 1024 * 1024
MOE_TILE = 512
COPY_WINDOW = 64


def _cparams(sem):
    return pltpu.CompilerParams(dimension_semantics=sem, vmem_limit_bytes=VMEM_LIMIT)


def _rms(x, g):
    ms = jnp.mean(x * x, axis=-1, keepdims=True)
    return x * lax.rsqrt(ms + RMS_EPS) * g


def _norm_proj_kernel(combine, n_w, plan, *refs):
    it = iter(refs)
    x_ref = next(it)
    if combine:
        m0_ref, m1_ref, gt_ref = next(it), next(it), next(it)
    g_ref = next(it)
    w_refs = [next(it) for _ in range(n_w)]
    o_refs = [next(it) for _ in plan]
    h_ref = next(it) if combine else None
    res_refs = [next(it) for _ in range(n_w)]

    x = x_ref[...]
    if combine:
        x = _moe_combine(x, m0_ref, m1_ref, gt_ref)
        h_ref[...] = x
    xn = _rms(x, g_ref[...]).astype(BF16)
    for w_ref, res_ref in zip(w_refs, res_refs):
        res_ref[...] = jnp.dot(xn, w_ref[...], preferred_element_type=F32)
    for o_ref, (wi, start, width, _) in zip(o_refs, plan):
        o_ref[...] = res_refs[wi][:, start:start + width].astype(o_ref.dtype)


def _norm_proj(x, g, weights, plan, *, tm, moe=None):
    t, d = x.shape
    tm = min(tm, t)
    combine = moe is not None
    row = lambda i: (i, 0)
    const = lambda i: (0, 0)
    in_specs = [pl.BlockSpec((tm, d), row)]
    args = [x]
    if combine:
        m, gates = moe
        nc = d // LANES
        in_specs += [pl.BlockSpec((tm * nc, LANES), row),
                     pl.BlockSpec((tm * nc, LANES), lambda i: (t // tm + i, 0)),
                     pl.BlockSpec((tm, LANES), row)]
        args += [m, m, gates]
    in_specs.append(pl.BlockSpec((1, d), const))
    args.append(g.reshape(1, d))
    for w in weights:
        in_specs.append(pl.BlockSpec(w.shape, const, pipeline_mode=pl.Buffered(1)))
        args.append(w)
    out_shape = [jax.ShapeDtypeStruct((t, width), dtype) for _, _, width, dtype in plan]
    out_specs = [pl.BlockSpec((tm, width), row) for _, _, width, _ in plan]
    if combine:
        out_shape.append(jax.ShapeDtypeStruct((t, d), F32))
        out_specs.append(pl.BlockSpec((tm, d), row))
    outs = pl.pallas_call(
        functools.partial(_norm_proj_kernel, combine, len(weights), plan),
        grid=(t // tm,),
        in_specs=in_specs, out_specs=out_specs, out_shape=out_shape,
        scratch_shapes=[pltpu.VMEM((tm, w.shape[1]), F32) for w in weights],
        compiler_params=_cparams(("parallel",)),
        name="norm_proj",
    )(*args)
    outs = list(outs)
    h = outs.pop() if combine else None
    return outs, h


def _swa_kernel(sink_ref, q_ref, kp_ref, kc_ref, vp_ref, vc_ref, o_ref):
    n = pl.program_id(1)
    blk = A_WINDOW
    row = lax.broadcasted_iota(jnp.int32, (blk, 2 * blk), 0)
    col = lax.broadcasted_iota(jnp.int32, (blk, 2 * blk), 1)
    dist = blk + row - col
    valid = (dist >= 0) & (dist < A_WINDOW) & ((col >= blk) | (n > 0))
    distf = dist.astype(F32)
    outs = []
    for kh in range(A_KV_HEADS):
        ks = slice(kh * HEAD_DIM, (kh + 1) * HEAD_DIM)
        k = jnp.concatenate([kp_ref[0, :, ks], kc_ref[0, :, ks]], axis=0)
        v = jnp.concatenate([vp_ref[0, :, ks], vc_ref[0, :, ks]], axis=0)
        for gi in range(A_GROUP):
            h = kh * A_GROUP + gi
            slope = float(2.0 ** (-8.0 * (h + 1) / A_Q_HEADS))
            q = q_ref[0, :, h * HEAD_DIM:(h + 1) * HEAD_DIM]
            s = lax.dot_general(q, k, (((1,), (1,)), ((), ())), preferred_element_type=F32)
            s = s * (HEAD_DIM ** -0.5) - slope * distf
            s = jnp.where(valid, s, -jnp.inf)
            sink = sink_ref[h]
            m = jnp.maximum(jnp.max(s, axis=-1, keepdims=True), sink)
            p = jnp.exp(s - m)
            denom = jnp.sum(p, axis=-1, keepdims=True) + jnp.exp(sink - m)
            o = jnp.dot(p.astype(BF16), v, preferred_element_type=F32)
            outs.append(o / denom)
    o_ref[0] = jnp.concatenate(outs, axis=-1).astype(o_ref.dtype)


def _swa(proj, sinks):
    b, s, _ = proj.shape
    blk = A_WINDOW
    qw = A_Q_HEADS * HEAD_DIM
    kw = A_KV_HEADS * HEAD_DIM
    q_blk = 0
    k_blk = qw // kw
    v_blk = k_blk + 1
    prev = lambda i, n: jnp.maximum(n - 1, 0)
    return pl.pallas_call(
        _swa_kernel,
        grid=(b, s // blk),
        in_specs=[
            pl.BlockSpec(memory_space=pltpu.SMEM),
            pl.BlockSpec((1, blk, qw), lambda i, n: (i, n, q_blk)),
            pl.BlockSpec((1, blk, kw), lambda i, n: (i, prev(i, n), k_blk)),
            pl.BlockSpec((1, blk, kw), lambda i, n: (i, n, k_blk)),
            pl.BlockSpec((1, blk, kw), lambda i, n: (i, prev(i, n), v_blk)),
            pl.BlockSpec((1, blk, kw), lambda i, n: (i, n, v_blk)),
        ],
        out_specs=pl.BlockSpec((1, blk, qw), lambda i, n: (i, n, 0)),
        out_shape=jax.ShapeDtypeStruct((b, s, qw), BF16),
        compiler_params=_cparams(("parallel", "parallel")),
        name="swa",
    )(sinks.astype(F32), proj, proj, proj, proj, proj)


def _tril(n, dtype=F32):
    r = lax.broadcasted_iota(jnp.int32, (n, n), 0)
    c = lax.broadcasted_iota(jnp.int32, (n, n), 1)
    return (c <= r).astype(dtype)


def _split3(x):
    hi = x.astype(BF16)
    r = x - hi.astype(F32)
    mid = r.astype(BF16)
    return hi, mid, (r - mid.astype(F32)).astype(BF16)


def _dot_mask_lhs(mask, x):
    return sum(jnp.dot(mask, part, preferred_element_type=F32) for part in _split3(x))


def _dot_mask_rhs(x, mask):
    return sum(jnp.dot(part, mask, preferred_element_type=F32) for part in _split3(x))


def _fox_gate_kernel(f_ref, b_ref, c_ref, ct_ref):
    tri = _tril(LANES, BF16)
    carry = jnp.zeros((1, LANES), F32)
    for n in range(f_ref.shape[1] // LANES):
        rows = slice(n * LANES, (n + 1) * LANES)
        lf = jax.nn.log_sigmoid(f_ref[0, rows, :] + b_ref[...])
        cs = _dot_mask_lhs(tri, lf) + carry
        carry = cs[LANES - 1:LANES, :]
        c_ref[0, rows, :] = cs
        ct_ref[0, n] = cs.T[:B_HEADS, :]


def _fox_gate(f_aux, bias):
    b, s, _ = f_aux.shape
    nb = s // LANES
    bias_p = jnp.zeros((1, LANES), F32).at[0, :B_HEADS].set(bias.astype(F32))
    return pl.pallas_call(
        _fox_gate_kernel,
        grid=(b,),
        in_specs=[pl.BlockSpec((1, s, LANES), lambda i: (i, 0, 0)),
                  pl.BlockSpec((1, LANES), lambda i: (0, 0))],
        out_specs=[pl.BlockSpec((1, s, LANES), lambda i: (i, 0, 0)),
                   pl.BlockSpec((1, nb, B_HEADS, LANES), lambda i: (i, 0, 0, 0))],
        out_shape=[jax.ShapeDtypeStruct((b, s, LANES), F32),
                   jax.ShapeDtypeStruct((b, nb, B_HEADS, LANES), F32)],
        compiler_params=_cparams(("parallel",)),
        name="fox_gate",
    )(f_aux, bias_p)


def _fox_kernel(q0_ref, q1_ref, k0_ref, k1_ref, vt_ref, c_ref, ctq_ref, o_ref, *, tq, heads_per_step):
    qi = pl.program_id(1)
    sub = tq // LANES
    key = lax.broadcasted_iota(jnp.int32, (tq, tq), 0)
    qry = lax.broadcasted_iota(jnp.int32, (tq, tq), 1)
    causal = key <= qry
    nt = (((1,), (1,)), ((), ()))
    half = B_HEADS // 2
    q_refs, k_refs = (q0_ref, q1_ref), (k0_ref, k1_ref)
    outs = []
    for h0 in range(0, B_HEADS, heads_per_step):
        heads = list(range(h0, h0 + heads_per_step))
        hsl = [slice(h * HEAD_DIM, (h + 1) * HEAD_DIM) for h in heads]
        lsl = [slice((h % half) * HEAD_DIM, (h % half + 1) * HEAD_DIM) for h in heads]
        qs = [q_refs[h // half][0, :, ls] * (HEAD_DIM ** -0.5)
              for h, ls in zip(heads, lsl)]
        cqs = [jnp.concatenate([ctq_ref[0, u, h:h + 1, :] for u in range(sub)], axis=1) for h in heads]

        def step(j, carry, masked, heads=heads, hsl=hsl, lsl=lsl, qs=qs, cqs=cqs):
            start = pl.multiple_of(j * tq, tq)
            sts = [lax.dot_general(k_refs[h // half][0, pl.ds(start, tq), ls], q, nt,
                                   preferred_element_type=F32)
                   for h, ls, q in zip(heads, lsl, qs)]
            ps, stats = [], []
            for idx, h in enumerate(heads):
                m, l, _ = carry[3 * idx:3 * idx + 3]
                ck = c_ref[0, pl.ds(start, tq), h:h + 1]
                st = (sts[idx] - ck) + cqs[idx]
                if masked:
                    st = jnp.where(causal, st, -jnp.inf)
                m_new = jnp.maximum(m, jnp.max(st, axis=0, keepdims=True))
                alpha = jnp.exp(m - m_new)
                p = jnp.exp(st - m_new)
                stats.append((m_new, alpha, alpha * l + jnp.sum(p, axis=0, keepdims=True)))
                ps.append(p.astype(BF16))
            new = []
            for idx in range(len(heads)):
                m_new, alpha, l = stats[idx]
                pv = jnp.dot(vt_ref[0, j, hsl[idx], :], ps[idx], preferred_element_type=F32)
                new += [m_new, l, alpha * carry[3 * idx + 2] + pv]
            return tuple(new)

        init = (jnp.full((1, tq), -jnp.inf, F32), jnp.zeros((1, tq), F32),
                jnp.zeros((HEAD_DIM, tq), F32)) * heads_per_step
        carry = lax.fori_loop(0, qi, functools.partial(step, masked=False), init)
        carry = step(qi, carry, True)
        for idx in range(heads_per_step):
            outs.append(carry[3 * idx + 2] / carry[3 * idx + 1])
    o_ref[0] = jnp.concatenate(outs, axis=0).T.astype(o_ref.dtype)


def _fox(proj, c, ct, *, tq=256, heads_per_step=8):
    b, s, _ = proj.shape
    w = B_HEADS * HEAD_DIM
    nk = s // tq
    sub = tq // LANES
    hw = w // 2
    base = (A_Q_HEADS + 2 * A_KV_HEADS) * HEAD_DIM
    qb, kb = base // hw, (base + w) // hw
    v_t = proj[:, :, base + 2 * w:base + 3 * w].reshape(b, nk, tq, w).transpose(0, 1, 3, 2)
    return pl.pallas_call(
        functools.partial(_fox_kernel, tq=tq, heads_per_step=heads_per_step),
        grid=(b, s // tq),
        in_specs=[
            pl.BlockSpec((1, tq, hw), lambda i, n: (i, n, qb)),
            pl.BlockSpec((1, tq, hw), lambda i, n: (i, n, qb + 1)),
            pl.BlockSpec((1, s, hw), lambda i, n: (i, 0, kb)),
            pl.BlockSpec((1, s, hw), lambda i, n: (i, 0, kb + 1)),
            pl.BlockSpec((1, nk, w, tq), lambda i, n: (i, 0, 0, 0)),
            pl.BlockSpec((1, s, LANES), lambda i, n: (i, 0, 0)),
            pl.BlockSpec((1, sub, B_HEADS, LANES), lambda i, n: (i, n, 0, 0)),
        ],
        out_specs=pl.BlockSpec((1, tq, w), lambda i, n: (i, n, 0)),
        out_shape=jax.ShapeDtypeStruct((b, s, w), BF16),
        compiler_params=_cparams(("parallel", "parallel")),
        name="fox",
    )(proj, proj, proj, proj, v_t, c, ct)


def _ssd_gla_kernel(z_ref, q_ref, k_ref, v_ref, r_ref, xc_ref, xp_ref, sdt_ref, sg_ref,
                    cw_ref, cb_ref, dtb_ref, alog_ref, dsk_ref, ex_ref, sn_ref, gkw_ref, gkb_ref, gn_ref,
                    o_ref, conv_ref, hs_ref, gs_ref):
    c = pl.program_id(1)
    q_len = C_CHUNK
    halo = 8

    @pl.when(c == 0)
    def _():
        hs_ref[...] = jnp.zeros_like(hs_ref)
        gs_ref[...] = jnp.zeros_like(gs_ref)

    prev = xp_ref[0]
    conv_ref[0:halo, :] = jnp.where(c > 0, prev, jnp.zeros_like(prev))
    conv_ref[halo:halo + q_len, :] = xc_ref[0]
    acc = jnp.zeros((q_len, C_CONV_DIM), F32) + cb_ref[...]
    for j in range(C_CONV):
        off = halo - (C_CONV - 1) + j
        acc = acc + cw_ref[j:j + 1, :] * conv_ref[off:off + q_len, :]
    xbc = jax.nn.silu(acc)
    xs = xbc[:, :C_INNER]
    gs_w = C_GROUPS * C_STATE
    bm = xbc[:, C_INNER:C_INNER + gs_w].astype(BF16)
    cm = xbc[:, C_INNER + gs_w:].astype(BF16)

    row = lax.broadcasted_iota(jnp.int32, (q_len, q_len), 0)
    col = lax.broadcasted_iota(jnp.int32, (q_len, q_len), 1)
    tri = col <= row

    lane = lax.broadcasted_iota(jnp.int32, (1, LANES), 1)
    dt = jnp.where(lane < C_HEADS, jax.nn.softplus(sdt_ref[0] + dtb_ref[...]), 0.0)
    dta = dt * -jnp.exp(alog_ref[...])
    acs = _dot_mask_lhs(tri.astype(BF16), dta)
    acs_t = acs.T
    chunk_dec = jnp.exp(acs[q_len - 1:q_len, :])
    expand = ex_ref[...]
    dt_x = _dot_mask_rhs(dt, expand)
    acs_x = _dot_mask_rhs(acs, expand)
    xd = xs * dt_x
    xd_b = xd.astype(BF16)
    xdd = xd * jnp.exp(acs_x[q_len - 1:q_len, :] - acs_x)
    low_half = lax.broadcasted_iota(jnp.int32, (q_len, LANES), 1) < C_HEAD_DIM

    y_pairs, y_offs = [], []
    tdims = (((1,), (1,)), ((), ()))
    for g in range(C_GROUPS):
        b_g = bm[:, g * C_STATE:(g + 1) * C_STATE]
        c_g = cm[:, g * C_STATE:(g + 1) * C_STATE]
        cb = lax.dot_general(c_g, b_g, tdims, preferred_element_type=F32)
        h0 = g * C_HPG
        grp = slice(h0 * C_HEAD_DIM, (h0 + C_HPG) * C_HEAD_DIM)
        y_offs.append(lax.dot_general(c_g, hs_ref[grp, :].astype(BF16), tdims, preferred_element_type=F32))
        for h in range(h0, h0 + C_HPG, 2):
            xp = xd_b[:, h * C_HEAD_DIM:(h + 2) * C_HEAD_DIM]
            halves = []
            for hh in (h, h + 1):
                seg = jnp.exp(jnp.where(tri, acs[:, hh:hh + 1] - acs_t[hh:hh + 1, :], -jnp.inf))
                halves.append(jnp.dot((cb * seg).astype(BF16), xp, preferred_element_type=F32))
            y_pairs.append(jnp.where(low_half, halves[0], halves[1]))
        upd = jnp.dot(xdd[:, grp].T.astype(BF16), b_g, preferred_element_type=F32)
        for hh in range(C_HPG):
            h = h0 + hh
            ps = slice(h * C_HEAD_DIM, (h + 1) * C_HEAD_DIM)
            us = slice(hh * C_HEAD_DIM, (hh + 1) * C_HEAD_DIM)
            hs_ref[ps, :] = hs_ref[ps, :] * chunk_dec[0:1, h:h + 1] + upd[us, :]
    y = (jnp.concatenate(y_pairs, axis=1) + jnp.concatenate(y_offs, axis=1) * jnp.exp(acs_x)
         + dsk_ref[...] * xs)
    y = y * jax.nn.silu(z_ref[0].astype(F32))
    o_ref[0, :, :C_INNER] = _rms(y, sn_ref[...]).astype(o_ref.dtype)

    same = (row // D_CHUNK) == (col // D_CHUNK)
    tri2 = tri & same
    la = jnp.dot(sg_ref[0].astype(BF16), gkw_ref[...], preferred_element_type=F32) + gkb_ref[...]
    la = jax.nn.log_sigmoid(la) / D_GATE_NORM
    gcs = _dot_mask_lhs(tri2.astype(BF16), la)
    first = lax.broadcasted_iota(jnp.int32, (q_len, 1), 0) < D_CHUNK
    r_all = r_ref[0]
    for h in range(D_HEADS):
        ks = slice(h * D_HK, (h + 1) * D_HK)
        vs = slice(h * D_HV, (h + 1) * D_HV)
        g_h = gcs[:, ks]
        g_end0 = g_h[D_CHUNK - 1:D_CHUNK, :]
        g_end1 = g_h[q_len - 1:q_len, :]
        q_h = q_ref[0, :, ks].astype(F32) * (D_HK ** -0.5)
        k_h = k_ref[0, :, ks].astype(F32)
        v_h = v_ref[0, :, vs]
        q_dec = (q_h * jnp.exp(g_h)).astype(BF16)
        k_inv = (k_h * jnp.exp(-g_h)).astype(BF16)
        k_end = k_h * jnp.exp(jnp.where(first, g_end0, g_end1) - g_h)
        ke0 = jnp.where(first, k_end, 0.0).astype(BF16)
        ke1 = jnp.where(first, 0.0, k_end).astype(BF16)
        attn = lax.dot_general(q_dec, k_inv, (((1,), (1,)), ((), ())), preferred_element_type=F32)
        attn = jnp.where(tri2, attn, 0.0).astype(BF16)
        o = jnp.dot(attn, v_h, preferred_element_type=F32)
        v_t = v_h.astype(F32).T.astype(BF16)
        st_rows = slice(h * D_HV, (h + 1) * D_HV)
        s0 = gs_ref[st_rows, :]
        s1 = s0 * jnp.exp(g_end0) + jnp.dot(v_t, ke0, preferred_element_type=F32)
        s2 = s1 * jnp.exp(g_end1) + jnp.dot(v_t, ke1, preferred_element_type=F32)
        gs_ref[st_rows, :] = s2
        tdims = (((1,), (1,)), ((), ()))
        o0 = lax.dot_general(q_dec, s0.astype(BF16), tdims, preferred_element_type=F32)
        o1 = lax.dot_general(q_dec, s1.astype(BF16), tdims, preferred_element_type=F32)
        o = o + jnp.where(first, o0, o1)
        o = _rms(o, gn_ref[...]) * jax.nn.silu(r_all[:, vs].astype(F32))
        o_ref[0, :, C_INNER + h * D_HV:C_INNER + (h + 1) * D_HV] = o.astype(o_ref.dtype)


def _ssd_gla(z, q, k, v, r, xbc, side_dt, side_g, g_lane, p):
    b, s, _ = z.shape
    q_len = C_CHUNK
    chunk = lambda width: pl.BlockSpec((1, q_len, width), lambda i, n: (i, n, 0))
    full = lambda shape: pl.BlockSpec(shape, lambda i, n: (0,) * len(shape))
    pad_lanes = lambda vec: jnp.zeros((1, LANES), F32).at[0, :vec.shape[0]].set(vec.astype(F32))
    gkw = jnp.zeros((LANES, D_KEY), F32).at[g_lane:g_lane + D_GATE_RANK].set(p["gk_w"]).astype(BF16)
    expand = jnp.asarray(np.arange(C_INNER)[None, :] // C_HEAD_DIM == np.arange(LANES)[:, None], BF16)
    return pl.pallas_call(
        _ssd_gla_kernel,
        grid=(b, s // q_len),
        in_specs=[
            chunk(C_INNER), chunk(D_KEY), chunk(D_KEY), chunk(D_VAL), chunk(D_VAL), chunk(C_CONV_DIM),
            pl.BlockSpec((1, 8, C_CONV_DIM), lambda i, n: (i, jnp.maximum(n * (q_len // 8) - 1, 0), 0)),
            chunk(LANES), chunk(LANES),
            full((C_CONV, C_CONV_DIM)), full((1, C_CONV_DIM)),
            full((1, LANES)), full((1, LANES)), full((1, C_INNER)), full((LANES, C_INNER)),
            full((1, C_INNER)), full((LANES, D_KEY)), full((1, D_KEY)), full((1, D_HV)),
        ],
        out_specs=pl.BlockSpec((1, q_len, C_INNER + D_VAL), lambda i, n: (i, n, 0)),
        out_shape=jax.ShapeDtypeStruct((b, s, C_INNER + D_VAL), BF16),
        scratch_shapes=[pltpu.VMEM((8 + q_len, C_CONV_DIM), F32),
                        pltpu.VMEM((C_INNER, C_STATE), F32),
                        pltpu.VMEM((D_VAL, D_HK), F32)],
        compiler_params=_cparams(("parallel", "arbitrary")),
        name="ssd_gla",
    )(z, q, k, v, r, xbc, xbc, side_dt, side_g,
      p["conv_w"].astype(F32), p["conv_b"].reshape(1, -1).astype(F32),
      pad_lanes(p["dt_bias"]), pad_lanes(p["a_log"]),
      jnp.repeat(p["d_skip"].astype(F32), C_HEAD_DIM).reshape(1, C_INNER), expand,
      p["ssd_norm"].reshape(1, -1).astype(F32), gkw, p["gk_b"].reshape(1, -1).astype(F32),
      p["gla_norm"].reshape(1, -1).astype(F32))


def _store_rows_tiled(ref, val):
    m, d = val.shape
    nc = d // LANES
    for c in range(nc):
        ref[pl.ds(c, m, stride=nc), :] = val[:, c * LANES:(c + 1) * LANES]


def _load_rows_tiled(ref, m, dtype=None):
    nc = ref.shape[0] // m
    parts = [ref[pl.ds(c, m, stride=nc), :] for c in range(nc)]
    if dtype is not None:
        parts = [p.astype(dtype) for p in parts]
    return jnp.concatenate(parts, axis=1)


RT_GATE, RT_EXPERT, RT_RANK = 0, 2, 4


def _route_block(lg, carry):
    m = lg.shape[0]
    lane = lax.broadcasted_iota(jnp.int32, (m, LANES), 1)
    lane_f = lane.astype(F32)
    none = float(LANES)
    neg = -jnp.inf
    first_max = lambda v, vmax: jnp.min(jnp.where(v == vmax, lane_f, none), axis=-1, keepdims=True)
    gl = jnp.where(lane < MOE_GROUPS, lg, neg)
    gmax = jnp.max(gl, axis=-1, keepdims=True)
    g_w = 1.0 / jnp.sum(jnp.exp(gl - gmax), axis=-1, keepdims=True)
    lo = MOE_GROUPS + first_max(gl, gmax) * MOE_EPG
    el = jnp.where((lane_f >= lo) & (lane_f < lo + MOE_EPG), lg, neg)
    emax = jnp.max(el, axis=-1, keepdims=True)
    esum = jnp.sum(jnp.exp(el - emax), axis=-1, keepdims=True)
    l0 = first_max(el, emax)
    el2 = jnp.where(lane_f == l0, neg, el)
    emax2 = jnp.max(el2, axis=-1, keepdims=True)
    l1 = first_max(el2, emax2)
    p0 = 1.0 / esum
    p1 = jnp.exp(emax2 - emax) / esum
    w0 = g_w * (p0 / (p0 + p1))
    w1 = g_w * (p1 / (p0 + p1))
    oh0 = lane_f == l0
    oh1 = lane_f == l1
    oh = (oh0 | oh1).astype(BF16)
    r = lax.broadcasted_iota(jnp.int32, (m, m), 0)
    c = lax.broadcasted_iota(jnp.int32, (m, m), 1)
    cum = jnp.dot((c < r).astype(BF16), oh, preferred_element_type=F32) + carry
    rank0 = jnp.sum(jnp.where(oh0, cum, 0.0), axis=-1, keepdims=True)
    rank1 = jnp.sum(jnp.where(oh1, cum, 0.0), axis=-1, keepdims=True)
    carry = carry + jnp.sum(oh.astype(F32), axis=0, keepdims=True)
    rec = jnp.zeros((m, LANES), F32)
    for pos, val in ((RT_GATE, w0), (RT_GATE + 1, w1), (RT_EXPERT, l0 - MOE_GROUPS),
                     (RT_EXPERT + 1, l1 - MOE_GROUPS), (RT_RANK, rank0), (RT_RANK + 1, rank1)):
        rec = jnp.where(lane == pos, val, rec)
    return rec, carry


def _out_proj_kernel(n_parts, *refs):
    a_refs = refs[:n_parts]
    w_refs = refs[n_parts:2 * n_parts]
    h_ref, g_ref, wr_ref, br_ref, ho_ref, xt_ref, rt_ref, cnt_ref, carry_ref = refs[2 * n_parts:]

    @pl.when(pl.program_id(0) == 0)
    def _():
        carry_ref[...] = jnp.zeros_like(carry_ref)

    acc = h_ref[...]
    for a_ref, w_ref in zip(a_refs, w_refs):
        acc = acc + jnp.dot(a_ref[...], w_ref[...], preferred_element_type=F32)
    ho_ref[...] = acc
    xn = _rms(acc, g_ref[...])
    xt_ref[...] = xn.astype(xt_ref.dtype)
    x_hi, x_mid, _ = _split3(xn)
    wr = wr_ref[...]
    lg2 = jnp.dot(x_hi, wr, preferred_element_type=F32)
    lg = (lg2[:, :LANES] + lg2[:, LANES:] + jnp.dot(x_mid, wr[:, :LANES], preferred_element_type=F32)
          + br_ref[...])
    rec, carry = _route_block(lg, carry_ref[...])
    rt_ref[...] = rec
    carry_ref[...] = carry
    cnt_ref[0] = carry


def _out_proj(parts, w_parts, h, g, w_route, b_route, *, tm=512):
    t, d = h.shape
    tm = min(tm, t)
    nc = d // LANES
    row = lambda i: (i, 0)
    const = lambda i: (0, 0)
    in_specs = [pl.BlockSpec((tm, a.shape[1]), row) for a in parts]
    in_specs += [pl.BlockSpec(w.shape, const) for w in w_parts]
    in_specs += [pl.BlockSpec((tm, d), row), pl.BlockSpec((1, d), const),
                 pl.BlockSpec((d, 2 * LANES), const), pl.BlockSpec((1, LANES), const)]
    return pl.pallas_call(
        functools.partial(_out_proj_kernel, len(parts)),
        grid=(t // tm,),
        in_specs=in_specs,
        out_specs=[pl.BlockSpec((tm, d), row), pl.BlockSpec((tm, d), row),
                   pl.BlockSpec((tm, LANES), row), pl.BlockSpec((1, 1, LANES), lambda i: (i, 0, 0))],
        out_shape=[jax.ShapeDtypeStruct((t, d), F32), jax.ShapeDtypeStruct((t, d), BF16),
                   jax.ShapeDtypeStruct((t, LANES), F32), jax.ShapeDtypeStruct((t // tm, 1, LANES), F32)],
        scratch_shapes=[pltpu.VMEM((1, LANES), F32)],
        compiler_params=_cparams(("arbitrary",)),
        name="out_proj",
    )(*parts, *w_parts, h, g.reshape(1, d), w_route, b_route)


def _zero_fill_rows(rows_ref, z_ref, zsem, lo_ref, hi_ref, n_rows, nc):
    zb = z_ref.shape[0] // nc
    z_ref[...] = jnp.zeros_like(z_ref)
    assert MOE_TILE // 2 <= zb and MOE_TILE % zb == 0

    def piece(row0, n):
        return pltpu.make_async_copy(z_ref.at[pl.ds(0, n * nc)], rows_ref.at[pl.ds(row0 * nc, n * nc)], zsem)

    def sweep(issue):
        def per_expert(e, carry):
            off = lo_ref[e]
            for n, hit in _run_pieces(hi_ref[e] - off, MOE_TILE // 2):
                @pl.when(hit)
                def _(off=off, n=n):
                    piece(off, n).start() if issue else piece(off, n).wait()

                off = off + jnp.where(hit, n, 0)
            return carry

        def per_block(i, carry):
            piece(i * zb, zb).start() if issue else piece(i * zb, zb).wait()
            return carry

        lax.fori_loop(0, MOE_EXPERTS, per_expert, 0)
        lax.fori_loop(hi_ref[MOE_EXPERTS - 1] // zb, n_rows // zb, per_block, 0)

    sweep(True)
    sweep(False)


def _tile_positions(rt, pos_base):
    lane_f = lax.broadcasted_iota(jnp.int32, rt.shape, 1).astype(F32)
    out = []
    for k in range(MOE_TOPK):
        e_lane = rt[:, RT_EXPERT + k:RT_EXPERT + k + 1] + MOE_GROUPS
        out.append(rt[:, RT_RANK + k:RT_RANK + k + 1]
                   + jnp.sum(jnp.where(lane_f == e_lane, pos_base, 0.0), axis=-1, keepdims=True))
    return out


def _moe_dispatch_kernel(dst_ref, len_ref, lo_ref, hi_ref, x_ref, rt_ref, pb_ref, rows_ref,
                         sbuf, z_ref, sem, zsem, *, tm, nc, n_steps, n_rows):
    i = pl.program_id(0)
    rows = MOE_TOPK * tm
    slot = lax.rem(i, 2)

    def wait_slot(s):
        pltpu.make_async_copy(sbuf.at[s], rows_ref.at[pl.ds(0, rows * nc)], sem.at[s]).wait()

    @pl.when(i == 0)
    def _():
        _zero_fill_rows(rows_ref, z_ref, zsem, lo_ref, hi_ref, n_rows, nc)

    @pl.when(i >= 2)
    def _():
        wait_slot(slot)

    pos = _tile_positions(rt_ref[...], pb_ref[0])
    lane = lax.broadcasted_iota(jnp.int32, (tm, LANES), 1)
    pos_t = jnp.where(lane == 0, pos[0], jnp.where(lane == 1, pos[1], 0.0)).T
    p_iota = lax.broadcasted_iota(jnp.int32, (rows, tm), 0).astype(F32)
    place = ((p_iota == pos_t[0:1, :]) | (p_iota == pos_t[1:2, :])).astype(BF16)
    _store_rows_tiled(sbuf.at[slot], jnp.dot(place, x_ref[...], preferred_element_type=F32))

    def per_expert(e, src):
        seg = i * MOE_EXPERTS + e
        dst = dst_ref[seg]
        for n, hit in _run_pieces(len_ref[seg], rows):
            @pl.when(hit)
            def _(src=src, dst=dst, n=n):
                pltpu.make_async_copy(sbuf.at[slot, pl.ds(src * nc, n * nc)],
                                      rows_ref.at[pl.ds(dst * nc, n * nc)], sem.at[slot]).start()

            step = jnp.where(hit, n, 0)
            src, dst = src + step, dst + step
        return src

    lax.fori_loop(0, MOE_EXPERTS, per_expert, jnp.int32(0))

    @pl.when(i == n_steps - 1)
    def _():
        wait_slot(slot)
        if n_steps > 1:
            wait_slot(1 - slot)


def _moe_dispatch(x, route, pos_base, seg_dst, seg_len, pad_lo, pad_hi, n_rows, *, tm):
    t, d = x.shape
    nc = d // LANES
    n_steps = t // tm
    row = lambda i, *_: (i, 0)
    return pl.pallas_call(
        functools.partial(_moe_dispatch_kernel, tm=tm, nc=nc, n_steps=n_steps, n_rows=n_rows),
        grid_spec=pltpu.PrefetchScalarGridSpec(
            num_scalar_prefetch=4, grid=(n_steps,),
            in_specs=[pl.BlockSpec((tm, d), row), pl.BlockSpec((tm, LANES), row),
                      pl.BlockSpec((1, 1, LANES), lambda i, *_: (i, 0, 0))],
            out_specs=pl.BlockSpec(memory_space=pl.ANY),
            scratch_shapes=[pltpu.VMEM((2, MOE_TOPK * tm * nc, LANES), F32),
                            pltpu.VMEM((MOE_TILE // 2 * nc, LANES), F32),
                            pltpu.SemaphoreType.DMA((2,)), pltpu.SemaphoreType.DMA(())]),
        out_shape=jax.ShapeDtypeStruct((n_rows * nc, LANES), F32),
        compiler_params=pltpu.CompilerParams(dimension_semantics=("arbitrary",), has_side_effects=True,
                                             vmem_limit_bytes=VMEM_LIMIT),
        name="moe_dispatch",
    )(seg_dst, seg_len, pad_lo, pad_hi, x, route, pos_base)


def _moe_kernel(te_ref, nu_ref, x_ref, wg_ref, wu_ref, wd_ref, y_ref):
    i = pl.program_id(0)

    @pl.when(i >= nu_ref[0])
    def _():
        y_ref[...] = jnp.zeros_like(y_ref)

    @pl.when(i < nu_ref[0])
    def _():
        x = _load_rows_tiled(x_ref, MOE_TILE, BF16)
        gate = jnp.dot(x, wg_ref[...].astype(BF16), preferred_element_type=F32)
        up = jnp.dot(x, wu_ref[...].astype(BF16), preferred_element_type=F32)
        act = (jax.nn.silu(gate) * up).astype(BF16)
        y = jnp.dot(act, wd_ref[...].astype(BF16), preferred_element_type=F32)
        _store_rows_tiled(y_ref, y)


def _moe_experts(x_rows, tile_expert, n_used, w_gate, w_up, w_down, layer):
    d, ff = w_gate.shape[-2:]
    nc = d // LANES
    n_tiles = x_rows.shape[0] // (MOE_TILE * nc)
    live = lambda i, nu: jnp.minimum(i, nu[0] - 1)
    w_spec = lambda a, b: pl.BlockSpec((None, None, a, b), lambda i, te, nu: (layer, te[live(i, nu)], 0, 0))
    return pl.pallas_call(
        _moe_kernel,
        grid_spec=pltpu.PrefetchScalarGridSpec(
            num_scalar_prefetch=2, grid=(n_tiles,),
            in_specs=[pl.BlockSpec((MOE_TILE * nc, LANES), lambda i, te, nu: (live(i, nu), 0)),
                      w_spec(d, ff), w_spec(d, ff), w_spec(ff, d)],
            out_specs=pl.BlockSpec((MOE_TILE * nc, LANES), lambda i, te, nu: (i, 0))),
        out_shape=jax.ShapeDtypeStruct(x_rows.shape, F32),
        compiler_params=_cparams(("arbitrary",)),
        name="moe_experts",
    )(tile_expert, n_used, x_rows, w_gate, w_up, w_down)


def _moe_combine(h, m0_ref, m1_ref, rt_ref):
    rt = rt_ref[...]
    tm = h.shape[0]
    return (h + _load_rows_tiled(m0_ref, tm) * rt[:, RT_GATE:RT_GATE + 1]
            + _load_rows_tiled(m1_ref, tm) * rt[:, RT_GATE + 1:RT_GATE + 2])


def _run_pieces(run, max_rows):
    return [(n, (run & n) != 0) for n in (1 << b for b in reversed(range(max_rows.bit_length())))]


def _moe_combine_kernel(src_ref, len_ref, h_ref, rt_ref, pb_ref, g_ref, y_hbm, o_ref, ybuf, sem, *, tm, nc, final):
    i = pl.program_id(0)
    n_steps = pl.num_programs(0)
    rows = MOE_TOPK * tm

    def fetch(tile, slot):
        def per_expert(e, dst):
            seg = tile * MOE_EXPERTS + e
            src = src_ref[seg]
            for n, hit in _run_pieces(len_ref[seg], rows):
                @pl.when(hit)
                def _(src=src, dst=dst, n=n):
                    pltpu.make_async_copy(y_hbm.at[pl.ds(src * nc, n * nc)],
                                          ybuf.at[slot, pl.ds(dst * nc, n * nc)], sem.at[slot]).start()

                step = jnp.where(hit, n, 0)
                src, dst = src + step, dst + step
            return dst

        lax.fori_loop(0, MOE_EXPERTS, per_expert, jnp.int32(0))

    slot = lax.rem(i, 2)

    @pl.when(i == 0)
    def _():
        fetch(0, 0)

    @pl.when(i + 1 < n_steps)
    def _():
        fetch(i + 1, 1 - slot)

    pltpu.make_async_copy(y_hbm.at[pl.ds(0, rows * nc)], ybuf.at[slot], sem.at[slot]).wait()
    y = _load_rows_tiled(ybuf.at[slot], rows, BF16)
    rt = rt_ref[...]
    pos_f = lax.broadcasted_iota(jnp.int32, (tm, rows), 1).astype(F32)
    pick = jnp.zeros((tm, rows), F32)
    for k, pos in enumerate(_tile_positions(rt, pb_ref[0])):
        pick = jnp.where(pos_f == pos, rt[:, RT_GATE + k:RT_GATE + k + 1], pick)
    out = h_ref[...] + jnp.dot(pick.astype(BF16), y, preferred_element_type=F32)
    o_ref[...] = _rms(out, g_ref[...]) if final else out


def _moe_combine_rows(h, y_rows, route, seg_src, seg_len, pos_base, g, *, tm, final):
    t, d = h.shape
    nc = d // LANES
    row = lambda i, *_: (i, 0)
    return pl.pallas_call(
        functools.partial(_moe_combine_kernel, tm=tm, nc=nc, final=final),
        grid_spec=pltpu.PrefetchScalarGridSpec(
            num_scalar_prefetch=2, grid=(t // tm,),
            in_specs=[pl.BlockSpec((tm, d), row), pl.BlockSpec((tm, LANES), row),
                      pl.BlockSpec((1, 1, LANES), lambda i, *_: (i, 0, 0)),
                      pl.BlockSpec((1, d), lambda i, *_: (0, 0)),
                      pl.BlockSpec(memory_space=pl.ANY)],
            out_specs=pl.BlockSpec((tm, d), row),
            scratch_shapes=[pltpu.VMEM((2, MOE_TOPK * tm * nc, LANES), F32), pltpu.SemaphoreType.DMA((2,))]),
        out_shape=jax.ShapeDtypeStruct((t, d), F32),
        compiler_params=_cparams(("arbitrary",)),
        name="moe_combine",
    )(seg_src, seg_len, h, route, pos_base, g.reshape(1, d), y_rows)


def _moe(h, x, route, tile_counts, w_gate, w_up, w_down, layer, g, *, final):
    t = route.shape[0]
    n_tt = tile_counts.shape[0]
    tm = t // n_tt
    after = tile_counts[:, 0, MOE_GROUPS:MOE_GROUPS + MOE_EXPERTS].astype(jnp.int32)
    before = jnp.concatenate([jnp.zeros((1, MOE_EXPERTS), jnp.int32), after[:-1]], axis=0)
    cnt = after[-1]
    padded = (cnt + MOE_TILE - 1) // MOE_TILE * MOE_TILE
    pad_ends = jnp.cumsum(padded)
    starts = (pad_ends - padded).astype(jnp.int32)
    n_tiles = (t * MOE_TOPK + MOE_EXPERTS * (MOE_TILE - 1)) // MOE_TILE
    tile_start = jnp.arange(n_tiles, dtype=jnp.int32) * MOE_TILE
    tile_expert = jnp.minimum(jnp.sum(tile_start[:, None] >= pad_ends[None, :], axis=1),
                              MOE_EXPERTS - 1).astype(jnp.int32)
    n_used = (pad_ends[-1] // MOE_TILE).astype(jnp.int32).reshape(1)
    n_rows = n_tiles * MOE_TILE
    seg_len = (after - before).reshape(-1)
    seg_off = jnp.cumsum(after - before, axis=1) - (after - before)
    seg_row = (starts[None, :] + before).reshape(-1)
    pos_base = jnp.zeros((n_tt, 1, LANES), F32).at[:, 0, MOE_GROUPS:MOE_GROUPS + MOE_EXPERTS].set(
        (seg_off - before).astype(F32))
    x_rows = _moe_dispatch(x, route, pos_base, seg_row, seg_len, starts + cnt, pad_ends.astype(jnp.int32),
                           n_rows, tm=tm)
    y_rows = _moe_experts(x_rows, tile_expert, n_used, w_gate, w_up, w_down, layer)
    return _moe_combine_rows(h, y_rows, route, seg_row, seg_len, pos_base, g, tm=tm, final=final)


def _router_weights(w_group, b_group, w_router, b_router):
    d = w_group.shape[0]
    w = jnp.zeros((d, LANES), F32)
    w = w.at[:, :MOE_GROUPS].set(w_group).at[:, MOE_GROUPS:MOE_GROUPS + MOE_EXPERTS].set(w_router)
    b = jnp.zeros((1, LANES), F32)
    b = b.at[0, :MOE_GROUPS].set(b_group).at[0, MOE_GROUPS:MOE_GROUPS + MOE_EXPERTS].set(b_router)
    w_hi = w.astype(BF16)
    w_mid = (w - w_hi.astype(F32)).astype(BF16)
    return jnp.concatenate([w_hi, w_mid], axis=1), b


def kernel(x, norm_mix, norm_moe, norm_final, even_w_in, even_sinks, even_forget_bias, even_w_out,
           odd_w_in, odd_conv_w, odd_conv_b, odd_dt_bias, odd_a_log, odd_d_skip, odd_ssd_norm,
           odd_gk_w, odd_gk_b, odd_gla_norm, odd_w_out, moe_w_group, moe_b_group, moe_w_router,
           moe_b_router, moe_w_gate, moe_w_up, moe_w_down):
    b, s, d = x.shape
    t = b * s
    depth = norm_mix.shape[0]
    h = x.reshape(t, d)
    moe = None
    for layer in range(depth):
        i = layer // 2
        if layer % 2 == 0:
            w = even_w_in[i]
            n_ab = (A_Q_HEADS + 2 * A_KV_HEADS + 3 * B_HEADS) * HEAD_DIM
            w_main = w[:, :n_ab].astype(BF16)
            w_aux = jnp.zeros((d, LANES), F32).at[:, :B_HEADS].set(w[:, n_ab:]).astype(BF16)
            (proj, f_aux), h_new = _norm_proj(h, norm_mix[layer], [w_main, w_aux],
                                              ((0, 0, n_ab, BF16), (1, 0, LANES, F32)), tm=512, moe=moe)
            h = h if h_new is None else h_new
            proj = proj.reshape(b, s, -1)
            out_a = _swa(proj, even_sinks[i])
            c, ct = _fox_gate(f_aux.reshape(b, s, LANES), even_forget_bias[i])
            out_b = _fox(proj, c, ct)
            n_ha = A_Q_HEADS * HEAD_DIM
            w_out = even_w_out[i].astype(BF16)
            parts = [out_a.reshape(t, -1), out_b.reshape(t, -1)]
            w_parts = [w_out[:n_ha], w_out[n_ha:]]
        else:
            w = odd_w_in[i]
            o_z, o_xbc = 0, C_INNER
            o_dt = o_xbc + C_CONV_DIM
            o_q = o_dt + C_HEADS
            o_k = o_q + D_KEY
            o_v = o_k + D_KEY
            o_g = o_v + D_VAL
            o_r = o_g + D_GATE_RANK
            n_b = w.shape[1] - o_dt
            w_a = w[:, :o_dt].astype(BF16)
            w_b = jnp.pad(w[:, o_dt:].astype(BF16), ((0, 0), (0, -n_b % LANES)))
            g_win = (o_g - o_dt) // LANES * LANES
            plan = ((0, o_z, C_INNER, BF16), (1, o_q - o_dt, D_KEY, BF16), (1, o_k - o_dt, D_KEY, BF16),
                    (1, o_v - o_dt, D_VAL, BF16), (1, o_r - o_dt, D_VAL, BF16), (0, o_xbc, C_CONV_DIM, F32),
                    (1, 0, LANES, F32), (1, g_win, LANES, F32))
            outs, h_new = _norm_proj(h, norm_mix[layer], [w_a, w_b], plan, tm=256, moe=moe)
            h = h if h_new is None else h_new
            params = dict(conv_w=odd_conv_w[i], conv_b=odd_conv_b[i], dt_bias=odd_dt_bias[i], a_log=odd_a_log[i],
                          d_skip=odd_d_skip[i], ssd_norm=odd_ssd_norm[i], gk_w=odd_gk_w[i], gk_b=odd_gk_b[i],
                          gla_norm=odd_gla_norm[i])
            mixed = _ssd_gla(*[o.reshape(b, s, -1) for o in outs], o_g - o_dt - g_win, params)
            parts = [mixed.reshape(t, -1)]
            w_parts = [odd_w_out[i].astype(BF16)]
        w_route, b_route = _router_weights(moe_w_group[layer], moe_b_group[layer],
                                           moe_w_router[layer], moe_b_router[layer])
        h, x_tiled, route, tile_counts = _out_proj(parts, w_parts, h, norm_moe[layer], w_route, b_route)
        h = _moe(h, x_tiled, route, tile_counts, moe_w_gate, moe_w_up, moe_w_down, layer, norm_final,
                 final=layer == depth - 1)
    out = h
    return out.reshape(b, s, d)
```

```python
import functools
import math

import numpy as np
import jax
import jax.numpy as jnp
from jax import lax
from jax.experimental import pallas as pl
from jax.experimental.pallas import tpu as pltpu

F32 = jnp.float32
BF16 = jnp.bfloat16
HIGHEST = lax.Precision.HIGHEST

RMS_EPS = 1e-6
HEAD_DIM = 64
A_Q_HEADS = 8
A_KV_HEADS = 2
A_GROUP = A_Q_HEADS // A_KV_HEADS
A_WINDOW = 128
B_HEADS = 8
C_HEADS = 16
C_HEAD_DIM = 64
C_INNER = C_HEADS * C_HEAD_DIM
C_GROUPS = 2
C_HPG = C_HEADS // C_GROUPS
C_STATE = 128
C_CONV = 4
C_CHUNK = 128
C_CONV_DIM = C_INNER + 2 * C_GROUPS * C_STATE
D_HEADS = 4
D_HK = 128
D_HV = 256
D_KEY = D_HEADS * D_HK
D_VAL = D_HEADS * D_HV
D_GATE_RANK = 16
D_GATE_NORM = 16.0
D_CHUNK = 64
MOE_GROUPS = 4
MOE_EPG = 8
MOE_EXPERTS = MOE_GROUPS * MOE_EPG
MOE_TOPK = 2

LANES = 128
VMEM_LIMIT = 48 * 1024 * 1024
MOE_TILE = 512
COPY_WINDOW = 64


def _cparams(sem):
    return pltpu.CompilerParams(dimension_semantics=sem, vmem_limit_bytes=VMEM_LIMIT)


def _rms(x, g):
    ms = jnp.mean(x * x, axis=-1, keepdims=True)
    return x * lax.rsqrt(ms + RMS_EPS) * g


def _norm_proj_kernel(combine, n_w, plan, *refs):
    it = iter(refs)
    x_ref = next(it)
    if combine:
        m0_ref, m1_ref, gt_ref = next(it), next(it), next(it)
    g_ref = next(it)
    w_refs = [next(it) for _ in range(n_w)]
    o_refs = [next(it) for _ in plan]
    h_ref = next(it) if combine else None
    res_refs = [next(it) for _ in range(n_w)]

    x = x_ref[...]
    if combine:
        x = _moe_combine(x, m0_ref, m1_ref, gt_ref)
        h_ref[...] = x
    xn = _rms(x, g_ref[...]).astype(BF16)
    for w_ref, res_ref in zip(w_refs, res_refs):
        res_ref[...] = jnp.dot(xn, w_ref[...], preferred_element_type=F32)
    for o_ref, (wi, start, width, _) in zip(o_refs, plan):
        o_ref[...] = res_refs[wi][:, start:start + width].astype(o_ref.dtype)


def _norm_proj(x, g, weights, plan, *, tm, moe=None):
    t, d = x.shape
    tm = min(tm, t)
    combine = moe is not None
    row = lambda i: (i, 0)
    const = lambda i: (0, 0)
    in_specs = [pl.BlockSpec((tm, d), row)]
    args = [x]
    if combine:
        m, gates = moe
        nc = d // LANES
        in_specs += [pl.BlockSpec((tm * nc, LANES), row),
                     pl.BlockSpec((tm * nc, LANES), lambda i: (t // tm + i, 0)),
                     pl.BlockSpec((tm, LANES), row)]
        args += [m, m, gates]
    in_specs.append(pl.BlockSpec((1, d), const))
    args.append(g.reshape(1, d))
    for w in weights:
        in_specs.append(pl.BlockSpec(w.shape, const, pipeline_mode=pl.Buffered(1)))
        args.append(w)
    out_shape = [jax.ShapeDtypeStruct((t, width), dtype) for _, _, width, dtype in plan]
    out_specs = [pl.BlockSpec((tm, width), row) for _, _, width, _ in plan]
    if combine:
        out_shape.append(jax.ShapeDtypeStruct((t, d), F32))
        out_specs.append(pl.BlockSpec((tm, d), row))
    outs = pl.pallas_call(
        functools.partial(_norm_proj_kernel, combine, len(weights), plan),
        grid=(t // tm,),
        in_specs=in_specs, out_specs=out_specs, out_shape=out_shape,
        scratch_shapes=[pltpu.VMEM((tm, w.shape[1]), F32) for w in weights],
        compiler_params=_cparams(("parallel",)),
        name="norm_proj",
    )(*args)
    outs = list(outs)
    h = outs.pop() if combine else None
    return outs, h


def _swa_kernel(sink_ref, q_ref, kp_ref, kc_ref, vp_ref, vc_ref, o_ref):
    n = pl.program_id(1)
    blk = A_WINDOW
    row = lax.broadcasted_iota(jnp.int32, (blk, 2 * blk), 0)
    col = lax.broadcasted_iota(jnp.int32, (blk, 2 * blk), 1)
    dist = blk + row - col
    valid = (dist >= 0) & (dist < A_WINDOW) & ((col >= blk) | (n > 0))
    distf = dist.astype(F32)
    outs = []
    for kh in range(A_KV_HEADS):
        ks = slice(kh * HEAD_DIM, (kh + 1) * HEAD_DIM)
        k = jnp.concatenate([kp_ref[0, :, ks], kc_ref[0, :, ks]], axis=0)
        v = jnp.concatenate([vp_ref[0, :, ks], vc_ref[0, :, ks]], axis=0)
        for gi in range(A_GROUP):
            h = kh * A_GROUP + gi
            slope = float(2.0 ** (-8.0 * (h + 1) / A_Q_HEADS))
            q = q_ref[0, :, h * HEAD_DIM:(h + 1) * HEAD_DIM]
            s = lax.dot_general(q, k, (((1,), (1,)), ((), ())), preferred_element_type=F32)
            s = s * (HEAD_DIM ** -0.5) - slope * distf
            s = jnp.where(valid, s, -jnp.inf)
            sink = sink_ref[h]
            m = jnp.maximum(jnp.max(s, axis=-1, keepdims=True), sink)
            p = jnp.exp(s - m)
            denom = jnp.sum(p, axis=-1, keepdims=True) + jnp.exp(sink - m)
            o = jnp.dot(p.astype(BF16), v, preferred_element_type=F32)
            outs.append(o / denom)
    o_ref[0] = jnp.concatenate(outs, axis=-1).astype(o_ref.dtype)


def _swa(proj, sinks):
    b, s, _ = proj.shape
    blk = A_WINDOW
    qw = A_Q_HEADS * HEAD_DIM
    kw = A_KV_HEADS * HEAD_DIM
    q_blk = 0
    k_blk = qw // kw
    v_blk = k_blk + 1
    prev = lambda i, n: jnp.maximum(n - 1, 0)
    return pl.pallas_call(
        _swa_kernel,
        grid=(b, s // blk),
        in_specs=[
            pl.BlockSpec(memory_space=pltpu.SMEM),
            pl.BlockSpec((1, blk, qw), lambda i, n: (i, n, q_blk)),
            pl.BlockSpec((1, blk, kw), lambda i, n: (i, prev(i, n), k_blk)),
            pl.BlockSpec((1, blk, kw), lambda i, n: (i, n, k_blk)),
            pl.BlockSpec((1, blk, kw), lambda i, n: (i, prev(i, n), v_blk)),
            pl.BlockSpec((1, blk, kw), lambda i, n: (i, n, v_blk)),
        ],
        out_specs=pl.BlockSpec((1, blk, qw), lambda i, n: (i, n, 0)),
        out_shape=jax.ShapeDtypeStruct((b, s, qw), BF16),
        compiler_params=_cparams(("parallel", "parallel")),
        name="swa",
    )(sinks.astype(F32), proj, proj, proj, proj, proj)


def _tril(n, dtype=F32):
    r = lax.broadcasted_iota(jnp.int32, (n, n), 0)
    c = lax.broadcasted_iota(jnp.int32, (n, n), 1)
    return (c <= r).astype(dtype)


def _split3(x):
    hi = x.astype(BF16)
    r = x - hi.astype(F32)
    mid = r.astype(BF16)
    return hi, mid, (r - mid.astype(F32)).astype(BF16)


def _dot_mask_lhs(mask, x):
    return sum(jnp.dot(mask, part, preferred_element_type=F32) for part in _split3(x))


def _dot_mask_rhs(x, mask):
    return sum(jnp.dot(part, mask, preferred_element_type=F32) for part in _split3(x))


def _fox_gate_kernel(f_ref, b_ref, c_ref, ct_ref):
    tri = _tril(LANES, BF16)
    carry = jnp.zeros((1, LANES), F32)
    for n in range(f_ref.shape[1] // LANES):
        rows = slice(n * LANES, (n + 1) * LANES)
        lf = jax.nn.log_sigmoid(f_ref[0, rows, :] + b_ref[...])
        cs = _dot_mask_lhs(tri, lf) + carry
        carry = cs[LANES - 1:LANES, :]
        c_ref[0, rows, :] = cs
        ct_ref[0, n] = cs.T[:B_HEADS, :]


def _fox_gate(f_aux, bias):
    b, s, _ = f_aux.shape
    nb = s // LANES
    bias_p = jnp.zeros((1, LANES), F32).at[0, :B_HEADS].set(bias.astype(F32))
    return pl.pallas_call(
        _fox_gate_kernel,
        grid=(b,),
        in_specs=[pl.BlockSpec((1, s, LANES), lambda i: (i, 0, 0)),
                  pl.BlockSpec((1, LANES), lambda i: (0, 0))],
        out_specs=[pl.BlockSpec((1, s, LANES), lambda i: (i, 0, 0)),
                   pl.BlockSpec((1, nb, B_HEADS, LANES), lambda i: (i, 0, 0, 0))],
        out_shape=[jax.ShapeDtypeStruct((b, s, LANES), F32),
                   jax.ShapeDtypeStruct((b, nb, B_HEADS, LANES), F32)],
        compiler_params=_cparams(("parallel",)),
        name="fox_gate",
    )(f_aux, bias_p)


def _fox_kernel(q0_ref, q1_ref, k0_ref, k1_ref, vt_ref, c_ref, ctq_ref, o_ref, *, tq, heads_per_step):
    qi = pl.program_id(1)
    sub = tq // LANES
    key = lax.broadcasted_iota(jnp.int32, (tq, tq), 0)
    qry = lax.broadcasted_iota(jnp.int32, (tq, tq), 1)
    causal = key <= qry
    nt = (((1,), (1,)), ((), ()))
    half = B_HEADS // 2
    q_refs, k_refs = (q0_ref, q1_ref), (k0_ref, k1_ref)
    outs = []
    for h0 in range(0, B_HEADS, heads_per_step):
        heads = list(range(h0, h0 + heads_per_step))
        hsl = [slice(h * HEAD_DIM, (h + 1) * HEAD_DIM) for h in heads]
        lsl = [slice((h % half) * HEAD_DIM, (h % half + 1) * HEAD_DIM) for h in heads]
        qs = [q_refs[h // half][0, :, ls] * (HEAD_DIM ** -0.5)
              for h, ls in zip(heads, lsl)]
        cqs = [jnp.concatenate([ctq_ref[0, u, h:h + 1, :] for u in range(sub)], axis=1) for h in heads]

        def step(j, carry, masked, heads=heads, hsl=hsl, lsl=lsl, qs=qs, cqs=cqs):
            start = pl.multiple_of(j * tq, tq)
            sts = [lax.dot_general(k_refs[h // half][0, pl.ds(start, tq), ls], q, nt,
                                   preferred_element_type=F32)
                   for h, ls, q in zip(heads, lsl, qs)]
            ps, stats = [], []
            for idx, h in enumerate(heads):
                m, l, _ = carry[3 * idx:3 * idx + 3]
                ck = c_ref[0, pl.ds(start, tq), h:h + 1]
                st = (sts[idx] - ck) + cqs[idx]
                if masked:
                    st = jnp.where(causal, st, -jnp.inf)
                m_new = jnp.maximum(m, jnp.max(st, axis=0, keepdims=True))
                alpha = jnp.exp(m - m_new)
                p = jnp.exp(st - m_new)
                stats.append((m_new, alpha, alpha * l + jnp.sum(p, axis=0, keepdims=True)))
                ps.append(p.astype(BF16))
            new = []
            for idx in range(len(heads)):
                m_new, alpha, l = stats[idx]
                pv = jnp.dot(vt_ref[0, j, hsl[idx], :], ps[idx], preferred_element_type=F32)
                new += [m_new, l, alpha * carry[3 * idx + 2] + pv]
            return tuple(new)

        init = (jnp.full((1, tq), -jnp.inf, F32), jnp.zeros((1, tq), F32),
                jnp.zeros((HEAD_DIM, tq), F32)) * heads_per_step
        carry = lax.fori_loop(0, qi, functools.partial(step, masked=False), init)
        carry = step(qi, carry, True)
        for idx in range(heads_per_step):
            outs.append(carry[3 * idx + 2] / carry[3 * idx + 1])
    o_ref[0] = jnp.concatenate(outs, axis=0).T.astype(o_ref.dtype)


def _fox(proj, c, ct, *, tq=256, heads_per_step=8):
    b, s, _ = proj.shape
    w = B_HEADS * HEAD_DIM
    nk = s // tq
    sub = tq // LANES
    hw = w // 2
    base = (A_Q_HEADS + 2 * A_KV_HEADS) * HEAD_DIM
    qb, kb = base // hw, (base + w) // hw
    v_t = proj[:, :, base + 2 * w:base + 3 * w].reshape(b, nk, tq, w).transpose(0, 1, 3, 2)
    return pl.pallas_call(
        functools.partial(_fox_kernel, tq=tq, heads_per_step=heads_per_step),
        grid=(b, s // tq),
        in_specs=[
            pl.BlockSpec((1, tq, hw), lambda i, n: (i, n, qb)),
            pl.BlockSpec((1, tq, hw), lambda i, n: (i, n, qb + 1)),
            pl.BlockSpec((1, s, hw), lambda i, n: (i, 0, kb)),
            pl.BlockSpec((1, s, hw), lambda i, n: (i, 0, kb + 1)),
            pl.BlockSpec((1, nk, w, tq), lambda i, n: (i, 0, 0, 0)),
            pl.BlockSpec((1, s, LANES), lambda i, n: (i, 0, 0)),
            pl.BlockSpec((1, sub, B_HEADS, LANES), lambda i, n: (i, n, 0, 0)),
        ],
        out_specs=pl.BlockSpec((1, tq, w), lambda i, n: (i, n, 0)),
        out_shape=jax.ShapeDtypeStruct((b, s, w), BF16),
        compiler_params=_cparams(("parallel", "parallel")),
        name="fox",
    )(proj, proj, proj, proj, v_t, c, ct)


def _ssd_gla_kernel(z_ref, q_ref, k_ref, v_ref, r_ref, xc_ref, xp_ref, sdt_ref, sg_ref,
                    cw_ref, cb_ref, dtb_ref, alog_ref, dsk_ref, ex_ref, sn_ref, gkw_ref, gkb_ref, gn_ref,
                    o_ref, conv_ref, hs_ref, gs_ref):
    c = pl.program_id(1)
    q_len = C_CHUNK
    halo = 8

    @pl.when(c == 0)
    def _():
        hs_ref[...] = jnp.zeros_like(hs_ref)
        gs_ref[...] = jnp.zeros_like(gs_ref)

    prev = xp_ref[0]
    conv_ref[0:halo, :] = jnp.where(c > 0, prev, jnp.zeros_like(prev))
    conv_ref[halo:halo + q_len, :] = xc_ref[0]
    acc = jnp.zeros((q_len, C_CONV_DIM), F32) + cb_ref[...]
    for j in range(C_CONV):
        off = halo - (C_CONV - 1) + j
        acc = acc + cw_ref[j:j + 1, :] * conv_ref[off:off + q_len, :]
    xbc = jax.nn.silu(acc)
    xs = xbc[:, :C_INNER]
    gs_w = C_GROUPS * C_STATE
    bm = xbc[:, C_INNER:C_INNER + gs_w].astype(BF16)
    cm = xbc[:, C_INNER + gs_w:].astype(BF16)

    row = lax.broadcasted_iota(jnp.int32, (q_len, q_len), 0)
    col = lax.broadcasted_iota(jnp.int32, (q_len, q_len), 1)
    tri = col <= row

    lane = lax.broadcasted_iota(jnp.int32, (1, LANES), 1)
    dt = jnp.where(lane < C_HEADS, jax.nn.softplus(sdt_ref[0] + dtb_ref[...]), 0.0)
    dta = dt * -jnp.exp(alog_ref[...])
    acs = _dot_mask_lhs(tri.astype(BF16), dta)
    acs_t = acs.T
    chunk_dec = jnp.exp(acs[q_len - 1:q_len, :])
    expand = ex_ref[...]
    dt_x = _dot_mask_rhs(dt, expand)
    acs_x = _dot_mask_rhs(acs, expand)
    xd = xs * dt_x
    xd_b = xd.astype(BF16)
    xdd = xd * jnp.exp(acs_x[q_len - 1:q_len, :] - acs_x)
    low_half = lax.broadcasted_iota(jnp.int32, (q_len, LANES), 1) < C_HEAD_DIM

    y_pairs, y_offs = [], []
    tdims = (((1,), (1,)), ((), ()))
    for g in range(C_GROUPS):
        b_g = bm[:, g * C_STATE:(g + 1) * C_STATE]
        c_g = cm[:, g * C_STATE:(g + 1) * C_STATE]
        cb = lax.dot_general(c_g, b_g, tdims, preferred_element_type=F32)
        h0 = g * C_HPG
        grp = slice(h0 * C_HEAD_DIM, (h0 + C_HPG) * C_HEAD_DIM)
        y_offs.append(lax.dot_general(c_g, hs_ref[grp, :].astype(BF16), tdims, preferred_element_type=F32))
        for h in range(h0, h0 + C_HPG, 2):
            xp = xd_b[:, h * C_HEAD_DIM:(h + 2) * C_HEAD_DIM]
            halves = []
            for hh in (h, h + 1):
                seg = jnp.exp(jnp.where(tri, acs[:, hh:hh + 1] - acs_t[hh:hh + 1, :], -jnp.inf))
                halves.append(jnp.dot((cb * seg).astype(BF16), xp, preferred_element_type=F32))
            y_pairs.append(jnp.where(low_half, halves[0], halves[1]))
        upd = jnp.dot(xdd[:, grp].T.astype(BF16), b_g, preferred_element_type=F32)
        for hh in range(C_HPG):
            h = h0 + hh
            ps = slice(h * C_HEAD_DIM, (h + 1) * C_HEAD_DIM)
            us = slice(hh * C_HEAD_DIM, (hh + 1) * C_HEAD_DIM)
            hs_ref[ps, :] = hs_ref[ps, :] * chunk_dec[0:1, h:h + 1] + upd[us, :]
    y = (jnp.concatenate(y_pairs, axis=1) + jnp.concatenate(y_offs, axis=1) * jnp.exp(acs_x)
         + dsk_ref[...] * xs)
    y = y * jax.nn.silu(z_ref[0].astype(F32))
    o_ref[0, :, :C_INNER] = _rms(y, sn_ref[...]).astype(o_ref.dtype)

    same = (row // D_CHUNK) == (col // D_CHUNK)
    tri2 = tri & same
    la = jnp.dot(sg_ref[0].astype(BF16), gkw_ref[...], preferred_element_type=F32) + gkb_ref[...]
    la = jax.nn.log_sigmoid(la) / D_GATE_NORM
    gcs = _dot_mask_lhs(tri2.astype(BF16), la)
    first = lax.broadcasted_iota(jnp.int32, (q_len, 1), 0) < D_CHUNK
    r_all = r_ref[0]
    for h in range(D_HEADS):
        ks = slice(h * D_HK, (h + 1) * D_HK)
        vs = slice(h * D_HV, (h + 1) * D_HV)
        g_h = gcs[:, ks]
        g_end0 = g_h[D_CHUNK - 1:D_CHUNK, :]
        g_end1 = g_h[q_len - 1:q_len, :]
        q_h = q_ref[0, :, ks].astype(F32) * (D_HK ** -0.5)
        k_h = k_ref[0, :, ks].astype(F32)
        v_h = v_ref[0, :, vs]
        q_dec = (q_h * jnp.exp(g_h)).astype(BF16)
        k_inv = (k_h * jnp.exp(-g_h)).astype(BF16)
        k_end = k_h * jnp.exp(jnp.where(first, g_end0, g_end1) - g_h)
        ke0 = jnp.where(first, k_end, 0.0).astype(BF16)
        ke1 = jnp.where(first, 0.0, k_end).astype(BF16)
        attn = lax.dot_general(q_dec, k_inv, (((1,), (1,)), ((), ())), preferred_element_type=F32)
        attn = jnp.where(tri2, attn, 0.0).astype(BF16)
        o = jnp.dot(attn, v_h, preferred_element_type=F32)
        v_t = v_h.astype(F32).T.astype(BF16)
        st_rows = slice(h * D_HV, (h + 1) * D_HV)
        s0 = gs_ref[st_rows, :]
        s1 = s0 * jnp.exp(g_end0) + jnp.dot(v_t, ke0, preferred_element_type=F32)
        s2 = s1 * jnp.exp(g_end1) + jnp.dot(v_t, ke1, preferred_element_type=F32)
        gs_ref[st_rows, :] = s2
        tdims = (((1,), (1,)), ((), ()))
        o0 = lax.dot_general(q_dec, s0.astype(BF16), tdims, preferred_element_type=F32)
        o1 = lax.dot_general(q_dec, s1.astype(BF16), tdims, preferred_element_type=F32)
        o = o + jnp.where(first, o0, o1)
        o = _rms(o, gn_ref[...]) * jax.nn.silu(r_all[:, vs].astype(F32))
        o_ref[0, :, C_INNER + h * D_HV:C_INNER + (h + 1) * D_HV] = o.astype(o_ref.dtype)


def _ssd_gla(z, q, k, v, r, xbc, side_dt, side_g, g_lane, p):
    b, s, _ = z.shape
    q_len = C_CHUNK
    chunk = lambda width: pl.BlockSpec((1, q_len, width), lambda i, n: (i, n, 0))
    full = lambda shape: pl.BlockSpec(shape, lambda i, n: (0,) * len(shape))
    pad_lanes = lambda vec: jnp.zeros((1, LANES), F32).at[0, :vec.shape[0]].set(vec.astype(F32))
    gkw = jnp.zeros((LANES, D_KEY), F32).at[g_lane:g_lane + D_GATE_RANK].set(p["gk_w"]).astype(BF16)
    expand = jnp.asarray(np.arange(C_INNER)[None, :] // C_HEAD_DIM == np.arange(LANES)[:, None], BF16)
    return pl.pallas_call(
        _ssd_gla_kernel,
        grid=(b, s // q_len),
        in_specs=[
            chunk(C_INNER), chunk(D_KEY), chunk(D_KEY), chunk(D_VAL), chunk(D_VAL), chunk(C_CONV_DIM),
            pl.BlockSpec((1, 8, C_CONV_DIM), lambda i, n: (i, jnp.maximum(n * (q_len // 8) - 1, 0), 0)),
            chunk(LANES), chunk(LANES),
            full((C_CONV, C_CONV_DIM)), full((1, C_CONV_DIM)),
            full((1, LANES)), full((1, LANES)), full((1, C_INNER)), full((LANES, C_INNER)),
            full((1, C_INNER)), full((LANES, D_KEY)), full((1, D_KEY)), full((1, D_HV)),
        ],
        out_specs=pl.BlockSpec((1, q_len, C_INNER + D_VAL), lambda i, n: (i, n, 0)),
        out_shape=jax.ShapeDtypeStruct((b, s, C_INNER + D_VAL), BF16),
        scratch_shapes=[pltpu.VMEM((8 + q_len, C_CONV_DIM), F32),
                        pltpu.VMEM((C_INNER, C_STATE), F32),
                        pltpu.VMEM((D_VAL, D_HK), F32)],
        compiler_params=_cparams(("parallel", "arbitrary")),
        name="ssd_gla",
    )(z, q, k, v, r, xbc, xbc, side_dt, side_g,
      p["conv_w"].astype(F32), p["conv_b"].reshape(1, -1).astype(F32),
      pad_lanes(p["dt_bias"]), pad_lanes(p["a_log"]),
      jnp.repeat(p["d_skip"].astype(F32), C_HEAD_DIM).reshape(1, C_INNER), expand,
      p["ssd_norm"].reshape(1, -1).astype(F32), gkw, p["gk_b"].reshape(1, -1).astype(F32),
      p["gla_norm"].reshape(1, -1).astype(F32))


def _store_rows_tiled(ref, val):
    m, d = val.shape
    nc = d // LANES
    for c in range(nc):
        ref[pl.ds(c, m, stride=nc), :] = val[:, c * LANES:(c + 1) * LANES]


def _load_rows_tiled(ref, m, dtype=None):
    nc = ref.shape[0] // m
    parts = [ref[pl.ds(c, m, stride=nc), :] for c in range(nc)]
    if dtype is not None:
        parts = [p.astype(dtype) for p in parts]
    return jnp.concatenate(parts, axis=1)


def _pack_pairs(x):
    n = x.shape[1] // 2
    u = pltpu.bitcast(x.astype(BF16).astype(F32), jnp.uint32)
    return (u[:, :n] >> 16) | (u[:, n:] & jnp.uint32(0xFFFF0000))


def _unpack_pairs(u):
    lo = pltpu.bitcast(u << 16, F32).astype(BF16)
    hi = pltpu.bitcast(u & jnp.uint32(0xFFFF0000), F32).astype(BF16)
    return jnp.concatenate([lo, hi], axis=1)


RT_GATE, RT_EXPERT, RT_RANK = 0, 2, 4


def _route_block(lg, carry):
    m = lg.shape[0]
    lane = lax.broadcasted_iota(jnp.int32, (m, LANES), 1)
    lane_f = lane.astype(F32)
    none = float(LANES)
    neg = -jnp.inf
    first_max = lambda v, vmax: jnp.min(jnp.where(v == vmax, lane_f, none), axis=-1, keepdims=True)
    gl = jnp.where(lane < MOE_GROUPS, lg, neg)
    gmax = jnp.max(gl, axis=-1, keepdims=True)
    g_w = 1.0 / jnp.sum(jnp.exp(gl - gmax), axis=-1, keepdims=True)
    lo = MOE_GROUPS + first_max(gl, gmax) * MOE_EPG
    el = jnp.where((lane_f >= lo) & (lane_f < lo + MOE_EPG), lg, neg)
    emax = jnp.max(el, axis=-1, keepdims=True)
    esum = jnp.sum(jnp.exp(el - emax), axis=-1, keepdims=True)
    l0 = first_max(el, emax)
    el2 = jnp.where(lane_f == l0, neg, el)
    emax2 = jnp.max(el2, axis=-1, keepdims=True)
    l1 = first_max(el2, emax2)
    p0 = 1.0 / esum
    p1 = jnp.exp(emax2 - emax) / esum
    w0 = g_w * (p0 / (p0 + p1))
    w1 = g_w * (p1 / (p0 + p1))
    oh0 = lane_f == l0
    oh1 = lane_f == l1
    oh = (oh0 | oh1).astype(BF16)
    r = lax.broadcasted_iota(jnp.int32, (m, m), 0)
    c = lax.broadcasted_iota(jnp.int32, (m, m), 1)
    cum = jnp.dot((c < r).astype(BF16), oh, preferred_element_type=F32) + carry
    rank0 = jnp.sum(jnp.where(oh0, cum, 0.0), axis=-1, keepdims=True)
    rank1 = jnp.sum(jnp.where(oh1, cum, 0.0), axis=-1, keepdims=True)
    carry = carry + jnp.sum(oh.astype(F32), axis=0, keepdims=True)
    rec = jnp.zeros((m, LANES), F32)
    for pos, val in ((RT_GATE, w0), (RT_GATE + 1, w1), (RT_EXPERT, l0 - MOE_GROUPS),
                     (RT_EXPERT + 1, l1 - MOE_GROUPS), (RT_RANK, rank0), (RT_RANK + 1, rank1)):
        rec = jnp.where(lane == pos, val, rec)
    return rec, carry


def _out_proj_kernel(n_parts, *refs):
    a_refs = refs[:n_parts]
    w_refs = refs[n_parts:2 * n_parts]
    h_ref, g_ref, wr_ref, br_ref, ho_ref, xt_ref, rt_ref, cnt_ref, carry_ref = refs[2 * n_parts:]

    @pl.when(pl.program_id(0) == 0)
    def _():
        carry_ref[...] = jnp.zeros_like(carry_ref)

    acc = h_ref[...]
    for a_ref, w_ref in zip(a_refs, w_refs):
        acc = acc + jnp.dot(a_ref[...], w_ref[...], preferred_element_type=F32)
    ho_ref[...] = acc
    xn = _rms(acc, g_ref[...])
    xt_ref[...] = xn.astype(xt_ref.dtype)
    x_hi, x_mid, _ = _split3(xn)
    wr = wr_ref[...]
    lg2 = jnp.dot(x_hi, wr, preferred_element_type=F32)
    lg = (lg2[:, :LANES] + lg2[:, LANES:] + jnp.dot(x_mid, wr[:, :LANES], preferred_element_type=F32)
          + br_ref[...])
    rec, carry = _route_block(lg, carry_ref[...])
    rt_ref[...] = rec
    carry_ref[...] = carry
    cnt_ref[0] = carry


def _out_proj(parts, w_parts, h, g, w_route, b_route, *, tm=512):
    t, d = h.shape
    tm = min(tm, t)
    nc = d // LANES
    row = lambda i: (i, 0)
    const = lambda i: (0, 0)
    in_specs = [pl.BlockSpec((tm, a.shape[1]), row) for a in parts]
    in_specs += [pl.BlockSpec(w.shape, const) for w in w_parts]
    in_specs += [pl.BlockSpec((tm, d), row), pl.BlockSpec((1, d), const),
                 pl.BlockSpec((d, 2 * LANES), const), pl.BlockSpec((1, LANES), const)]
    return pl.pallas_call(
        functools.partial(_out_proj_kernel, len(parts)),
        grid=(t // tm,),
        in_specs=in_specs,
        out_specs=[pl.BlockSpec((tm, d), row), pl.BlockSpec((tm, d), row),
                   pl.BlockSpec((tm, LANES), row), pl.BlockSpec((1, 1, LANES), lambda i: (i, 0, 0))],
        out_shape=[jax.ShapeDtypeStruct((t, d), F32), jax.ShapeDtypeStruct((t, d), BF16),
                   jax.ShapeDtypeStruct((t, LANES), F32), jax.ShapeDtypeStruct((t // tm, 1, LANES), F32)],
        scratch_shapes=[pltpu.VMEM((1, LANES), F32)],
        compiler_params=_cparams(("arbitrary",)),
        name="out_proj",
    )(*parts, *w_parts, h, g.reshape(1, d), w_route, b_route)


def _zero_fill_rows(rows_ref, z_ref, zsem, lo_ref, hi_ref, n_rows, nc):
    zb = z_ref.shape[0] // nc
    z_ref[...] = jnp.zeros_like(z_ref)
    assert MOE_TILE // 2 <= zb and MOE_TILE % zb == 0

    def piece(row0, n):
        return pltpu.make_async_copy(z_ref.at[pl.ds(0, n * nc)], rows_ref.at[pl.ds(row0 * nc, n * nc)], zsem)

    def sweep(issue):
        def per_expert(e, carry):
            off = lo_ref[e]
            for n, hit in _run_pieces(hi_ref[e] - off, MOE_TILE // 2):
                @pl.when(hit)
                def _(off=off, n=n):
                    piece(off, n).start() if issue else piece(off, n).wait()

                off = off + jnp.where(hit, n, 0)
            return carry

        def per_block(i, carry):
            piece(i * zb, zb).start() if issue else piece(i * zb, zb).wait()
            return carry

        lax.fori_loop(0, MOE_EXPERTS, per_expert, 0)
        lax.fori_loop(hi_ref[MOE_EXPERTS - 1] // zb, n_rows // zb, per_block, 0)

    sweep(True)
    sweep(False)


def _tile_positions(rt, pos_base):
    lane_f = lax.broadcasted_iota(jnp.int32, rt.shape, 1).astype(F32)
    out = []
    for k in range(MOE_TOPK):
        e_lane = rt[:, RT_EXPERT + k:RT_EXPERT + k + 1] + MOE_GROUPS
        out.append(rt[:, RT_RANK + k:RT_RANK + k + 1]
                   + jnp.sum(jnp.where(lane_f == e_lane, pos_base, 0.0), axis=-1, keepdims=True))
    return out


def _moe_dispatch_kernel(dst_ref, len_ref, lo_ref, hi_ref, x_ref, rt_ref, pb_ref, rows_ref,
                         sbuf, z_ref, sem, zsem, *, tm, nc, n_steps, n_rows):
    i = pl.program_id(0)
    rows = MOE_TOPK * tm
    slot = lax.rem(i, 2)

    def wait_slot(s):
        pltpu.make_async_copy(sbuf.at[s], rows_ref.at[pl.ds(0, rows * nc)], sem.at[s]).wait()

    @pl.when(i == 0)
    def _():
        _zero_fill_rows(rows_ref, z_ref, zsem, lo_ref, hi_ref, n_rows, nc)

    @pl.when(i >= 2)
    def _():
        wait_slot(slot)

    pos = _tile_positions(rt_ref[...], pb_ref[0])
    lane = lax.broadcasted_iota(jnp.int32, (tm, LANES), 1)
    pos_t = jnp.where(lane == 0, pos[0], jnp.where(lane == 1, pos[1], 0.0)).T
    p_iota = lax.broadcasted_iota(jnp.int32, (rows, tm), 0).astype(F32)
    place = ((p_iota == pos_t[0:1, :]) | (p_iota == pos_t[1:2, :])).astype(BF16)
    _store_rows_tiled(sbuf.at[slot], _pack_pairs(jnp.dot(place, x_ref[...], preferred_element_type=F32)))

    def per_expert(e, src):
        seg = i * MOE_EXPERTS + e
        dst = dst_ref[seg]
        for n, hit in _run_pieces(len_ref[seg], rows):
            @pl.when(hit)
            def _(src=src, dst=dst, n=n):
                pltpu.make_async_copy(sbuf.at[slot, pl.ds(src * nc, n * nc)],
                                      rows_ref.at[pl.ds(dst * nc, n * nc)], sem.at[slot]).start()

            step = jnp.where(hit, n, 0)
            src, dst = src + step, dst + step
        return src

    lax.fori_loop(0, MOE_EXPERTS, per_expert, jnp.int32(0))

    @pl.when(i == n_steps - 1)
    def _():
        wait_slot(slot)
        if n_steps > 1:
            wait_slot(1 - slot)


def _moe_dispatch(x, route, pos_base, seg_dst, seg_len, pad_lo, pad_hi, n_rows, *, tm):
    t, d = x.shape
    nc = d // (2 * LANES)
    n_steps = t // tm
    row = lambda i, *_: (i, 0)
    return pl.pallas_call(
        functools.partial(_moe_dispatch_kernel, tm=tm, nc=nc, n_steps=n_steps, n_rows=n_rows),
        grid_spec=pltpu.PrefetchScalarGridSpec(
            num_scalar_prefetch=4, grid=(n_steps,),
            in_specs=[pl.BlockSpec((tm, d), row), pl.BlockSpec((tm, LANES), row),
                      pl.BlockSpec((1, 1, LANES), lambda i, *_: (i, 0, 0))],
            out_specs=pl.BlockSpec(memory_space=pl.ANY),
            scratch_shapes=[pltpu.VMEM((2, MOE_TOPK * tm * nc, LANES), jnp.uint32),
                            pltpu.VMEM((MOE_TILE // 2 * nc, LANES), jnp.uint32),
                            pltpu.SemaphoreType.DMA((2,)), pltpu.SemaphoreType.DMA(())]),
        out_shape=jax.ShapeDtypeStruct((n_rows * nc, LANES), jnp.uint32),
        compiler_params=pltpu.CompilerParams(dimension_semantics=("arbitrary",), has_side_effects=True,
                                             vmem_limit_bytes=VMEM_LIMIT),
        name="moe_dispatch",
    )(seg_dst, seg_len, pad_lo, pad_hi, x, route, pos_base)


def _moe_kernel(te_ref, nu_ref, x_ref, wg_ref, wu_ref, wd_ref, y_ref):
    i = pl.program_id(0)

    @pl.when(i >= nu_ref[0])
    def _():
        y_ref[...] = jnp.zeros_like(y_ref)

    @pl.when(i < nu_ref[0])
    def _():
        x = _unpack_pairs(_load_rows_tiled(x_ref, MOE_TILE))
        gate = jnp.dot(x, wg_ref[...].astype(BF16), preferred_element_type=F32)
        up = jnp.dot(x, wu_ref[...].astype(BF16), preferred_element_type=F32)
        act = (jax.nn.silu(gate) * up).astype(BF16)
        y = jnp.dot(act, wd_ref[...].astype(BF16), preferred_element_type=F32)
        _store_rows_tiled(y_ref, _pack_pairs(y))


def _moe_experts(x_rows, tile_expert, n_used, w_gate, w_up, w_down, layer):
    d, ff = w_gate.shape[-2:]
    nc = d // (2 * LANES)
    n_tiles = x_rows.shape[0] // (MOE_TILE * nc)
    live = lambda i, nu: jnp.minimum(i, nu[0] - 1)
    w_spec = lambda a, b: pl.BlockSpec((None, None, a, b), lambda i, te, nu: (layer, te[live(i, nu)], 0, 0))
    return pl.pallas_call(
        _moe_kernel,
        grid_spec=pltpu.PrefetchScalarGridSpec(
            num_scalar_prefetch=2, grid=(n_tiles,),
            in_specs=[pl.BlockSpec((MOE_TILE * nc, LANES), lambda i, te, nu: (live(i, nu), 0)),
                      w_spec(d, ff), w_spec(d, ff), w_spec(ff, d)],
            out_specs=pl.BlockSpec((MOE_TILE * nc, LANES), lambda i, te, nu: (i, 0))),
        out_shape=jax.ShapeDtypeStruct(x_rows.shape, x_rows.dtype),
        compiler_params=_cparams(("arbitrary",)),
        name="moe_experts",
    )(tile_expert, n_used, x_rows, w_gate, w_up, w_down)


def _moe_combine(h, m0_ref, m1_ref, rt_ref):
    rt = rt_ref[...]
    tm = h.shape[0]
    return (h + _load_rows_tiled(m0_ref, tm) * rt[:, RT_GATE:RT_GATE + 1]
            + _load_rows_tiled(m1_ref, tm) * rt[:, RT_GATE + 1:RT_GATE + 2])


def _run_pieces(run, max_rows):
    return [(n, (run & n) != 0) for n in (1 << b for b in reversed(range(max_rows.bit_length())))]


def _moe_combine_kernel(src_ref, len_ref, h_ref, rt_ref, pb_ref, g_ref, y_hbm, o_ref, ybuf, sem, *, tm, nc, final):
    i = pl.program_id(0)
    n_steps = pl.num_programs(0)
    rows = MOE_TOPK * tm

    def fetch(tile, slot):
        def per_expert(e, dst):
            seg = tile * MOE_EXPERTS + e
            src = src_ref[seg]
            for n, hit in _run_pieces(len_ref[seg], rows):
                @pl.when(hit)
                def _(src=src, dst=dst, n=n):
                    pltpu.make_async_copy(y_hbm.at[pl.ds(src * nc, n * nc)],
                                          ybuf.at[slot, pl.ds(dst * nc, n * nc)], sem.at[slot]).start()

                step = jnp.where(hit, n, 0)
                src, dst = src + step, dst + step
            return dst

        lax.fori_loop(0, MOE_EXPERTS, per_expert, jnp.int32(0))

    slot = lax.rem(i, 2)

    @pl.when(i == 0)
    def _():
        fetch(0, 0)

    @pl.when(i + 1 < n_steps)
    def _():
        fetch(i + 1, 1 - slot)

    pltpu.make_async_copy(y_hbm.at[pl.ds(0, rows * nc)], ybuf.at[slot], sem.at[slot]).wait()
    y = _unpack_pairs(_load_rows_tiled(ybuf.at[slot], rows))
    rt = rt_ref[...]
    pos_f = lax.broadcasted_iota(jnp.int32, (tm, rows), 1).astype(F32)
    pick = jnp.zeros((tm, rows), F32)
    for k, pos in enumerate(_tile_positions(rt, pb_ref[0])):
        pick = jnp.where(pos_f == pos, rt[:, RT_GATE + k:RT_GATE + k + 1], pick)
    out = h_ref[...] + jnp.dot(pick.astype(BF16), y, preferred_element_type=F32)
    o_ref[...] = _rms(out, g_ref[...]) if final else out


def _moe_combine_rows(h, y_rows, route, seg_src, seg_len, pos_base, g, *, tm, final):
    t, d = h.shape
    nc = d // (2 * LANES)
    row = lambda i, *_: (i, 0)
    return pl.pallas_call(
        functools.partial(_moe_combine_kernel, tm=tm, nc=nc, final=final),
        grid_spec=pltpu.PrefetchScalarGridSpec(
            num_scalar_prefetch=2, grid=(t // tm,),
            in_specs=[pl.BlockSpec((tm, d), row), pl.BlockSpec((tm, LANES), row),
                      pl.BlockSpec((1, 1, LANES), lambda i, *_: (i, 0, 0)),
                      pl.BlockSpec((1, d), lambda i, *_: (0, 0)),
                      pl.BlockSpec(memory_space=pl.ANY)],
            out_specs=pl.BlockSpec((tm, d), row),
            scratch_shapes=[pltpu.VMEM((2, MOE_TOPK * tm * nc, LANES), y_rows.dtype),
                            pltpu.SemaphoreType.DMA((2,))]),
        out_shape=jax.ShapeDtypeStruct((t, d), F32),
        compiler_params=_cparams(("arbitrary",)),
        name="moe_combine",
    )(seg_src, seg_len, h, route, pos_base, g.reshape(1, d), y_rows)


def _moe(h, x, route, tile_counts, w_gate, w_up, w_down, layer, g, *, final):
    t = route.shape[0]
    n_tt = tile_counts.shape[0]
    tm = t // n_tt
    after = tile_counts[:, 0, MOE_GROUPS:MOE_GROUPS + MOE_EXPERTS].astype(jnp.int32)
    before = jnp.concatenate([jnp.zeros((1, MOE_EXPERTS), jnp.int32), after[:-1]], axis=0)
    cnt = after[-1]
    padded = (cnt + MOE_TILE - 1) // MOE_TILE * MOE_TILE
    pad_ends = jnp.cumsum(padded)
    starts = (pad_ends - padded).astype(jnp.int32)
    n_tiles = (t * MOE_TOPK + MOE_EXPERTS * (MOE_TILE - 1)) // MOE_TILE
    tile_start = jnp.arange(n_tiles, dtype=jnp.int32) * MOE_TILE
    tile_expert = jnp.minimum(jnp.sum(tile_start[:, None] >= pad_ends[None, :], axis=1),
                              MOE_EXPERTS - 1).astype(jnp.int32)
    n_used = (pad_ends[-1] // MOE_TILE).astype(jnp.int32).reshape(1)
    n_rows = n_tiles * MOE_TILE
    seg_len = (after - before).reshape(-1)
    seg_off = jnp.cumsum(after - before, axis=1) - (after - before)
    seg_row = (starts[None, :] + before).reshape(-1)
    pos_base = jnp.zeros((n_tt, 1, LANES), F32).at[:, 0, MOE_GROUPS:MOE_GROUPS + MOE_EXPERTS].set(
        (seg_off - before).astype(F32))
    x_rows = _moe_dispatch(x, route, pos_base, seg_row, seg_len, starts + cnt, pad_ends.astype(jnp.int32),
                           n_rows, tm=tm)
    y_rows = _moe_experts(x_rows, tile_expert, n_used, w_gate, w_up, w_down, layer)
    return _moe_combine_rows(h, y_rows, route, seg_row, seg_len, pos_base, g, tm=tm, final=final)


def _router_weights(w_group, b_group, w_router, b_router):
    d = w_group.shape[0]
    w = jnp.zeros((d, LANES), F32)
    w = w.at[:, :MOE_GROUPS].set(w_group).at[:, MOE_GROUPS:MOE_GROUPS + MOE_EXPERTS].set(w_router)
    b = jnp.zeros((1, LANES), F32)
    b = b.at[0, :MOE_GROUPS].set(b_group).at[0, MOE_GROUPS:MOE_GROUPS + MOE_EXPERTS].set(b_router)
    w_hi = w.astype(BF16)
    w_mid = (w - w_hi.astype(F32)).astype(BF16)
    return jnp.concatenate([w_hi, w_mid], axis=1), b


def kernel(x, norm_mix, norm_moe, norm_final, even_w_in, even_sinks, even_forget_bias, even_w_out,
           odd_w_in, odd_conv_w, odd_conv_b, odd_dt_bias, odd_a_log, odd_d_skip, odd_ssd_norm,
           odd_gk_w, odd_gk_b, odd_gla_norm, odd_w_out, moe_w_group, moe_b_group, moe_w_router,
           moe_b_router, moe_w_gate, moe_w_up, moe_w_down):
    b, s, d = x.shape
    t = b * s
    depth = norm_mix.shape[0]
    h = x.reshape(t, d)
    moe = None
    for layer in range(depth):
        i = layer // 2
        if layer % 2 == 0:
            w = even_w_in[i]
            n_ab = (A_Q_HEADS + 2 * A_KV_HEADS + 3 * B_HEADS) * HEAD_DIM
            w_main = w[:, :n_ab].astype(BF16)
            w_aux = jnp.zeros((d, LANES), F32).at[:, :B_HEADS].set(w[:, n_ab:]).astype(BF16)
            (proj, f_aux), h_new = _norm_proj(h, norm_mix[layer], [w_main, w_aux],
                                              ((0, 0, n_ab, BF16), (1, 0, LANES, F32)), tm=512, moe=moe)
            h = h if h_new is None else h_new
            proj = proj.reshape(b, s, -1)
            out_a = _swa(proj, even_sinks[i])
            c, ct = _fox_gate(f_aux.reshape(b, s, LANES), even_forget_bias[i])
            out_b = _fox(proj, c, ct)
            n_ha = A_Q_HEADS * HEAD_DIM
            w_out = even_w_out[i].astype(BF16)
            parts = [out_a.reshape(t, -1), out_b.reshape(t, -1)]
            w_parts = [w_out[:n_ha], w_out[n_ha:]]
        else:
            w = odd_w_in[i]
            o_z, o_xbc = 0, C_INNER
            o_dt = o_xbc + C_CONV_DIM
            o_q = o_dt + C_HEADS
            o_k = o_q + D_KEY
            o_v = o_k + D_KEY
            o_g = o_v + D_VAL
            o_r = o_g + D_GATE_RANK
            n_b = w.shape[1] - o_dt
            w_a = w[:, :o_dt].astype(BF16)
            w_b = jnp.pad(w[:, o_dt:].astype(BF16), ((0, 0), (0, -n_b % LANES)))
            g_win = (o_g - o_dt) // LANES * LANES
            plan = ((0, o_z, C_INNER, BF16), (1, o_q - o_dt, D_KEY, BF16), (1, o_k - o_dt, D_KEY, BF16),
                    (1, o_v - o_dt, D_VAL, BF16), (1, o_r - o_dt, D_VAL, BF16), (0, o_xbc, C_CONV_DIM, F32),
                    (1, 0, LANES, F32), (1, g_win, LANES, F32))
            outs, h_new = _norm_proj(h, norm_mix[layer], [w_a, w_b], plan, tm=256, moe=moe)
            h = h if h_new is None else h_new
            params = dict(conv_w=odd_conv_w[i], conv_b=odd_conv_b[i], dt_bias=odd_dt_bias[i], a_log=odd_a_log[i],
                          d_skip=odd_d_skip[i], ssd_norm=odd_ssd_norm[i], gk_w=odd_gk_w[i], gk_b=odd_gk_b[i],
                          gla_norm=odd_gla_norm[i])
            mixed = _ssd_gla(*[o.reshape(b, s, -1) for o in outs], o_g - o_dt - g_win, params)
            parts = [mixed.reshape(t, -1)]
            w_parts = [odd_w_out[i].astype(BF16)]
        w_route, b_route = _router_weights(moe_w_group[layer], moe_b_group[layer],
                                           moe_w_router[layer], moe_b_router[layer])
        h, x_tiled, route, tile_counts = _out_proj(parts, w_parts, h, norm_moe[layer], w_route, b_route)
        h = _moe(h, x_tiled, route, tile_counts, moe_w_gate, moe_w_up, moe_w_down, layer, norm_final,
                 final=layer == depth - 1)
    out = h
    return out.reshape(b, s, d)
```

```python
import functools
import math

import numpy as np
import jax
import jax.numpy as jnp
from jax import lax
from jax.experimental import pallas as pl
from jax.experimental.pallas import tpu as pltpu

F32 = jnp.float32
BF16 = jnp.bfloat16
HIGHEST = lax.Precision.HIGHEST

RMS_EPS = 1e-6
HEAD_DIM = 64
A_Q_HEADS = 8
A_KV_HEADS = 2
A_GROUP = A_Q_HEADS // A_KV_HEADS
A_WINDOW = 128
B_HEADS = 8
C_HEADS = 16
C_HEAD_DIM = 64
C_INNER = C_HEADS * C_HEAD_DIM
C_GROUPS = 2
C_HPG = C_HEADS // C_GROUPS
C_STATE = 128
C_CONV = 4
C_CHUNK = 128
C_CONV_DIM = C_INNER + 2 * C_GROUPS * C_STATE
D_HEADS = 4
D_HK = 128
D_HV = 256
D_KEY = D_HEADS * D_HK
D_VAL = D_HEADS * D_HV
D_GATE_RANK = 16
D_GATE_NORM = 16.0
D_CHUNK = 64
MOE_GROUPS = 4
MOE_EPG = 8
MOE_EXPERTS = MOE_GROUPS * MOE_EPG
MOE_TOPK = 2

LANES = 128
VMEM_LIMIT = 48 * 1024 * 1024
MOE_TILE = 512
COPY_WINDOW = 64


def _cparams(sem):
    return pltpu.CompilerParams(dimension_semantics=sem, vmem_limit_bytes=VMEM_LIMIT)


def _rms(x, g):
    ms = jnp.mean(x * x, axis=-1, keepdims=True)
    return x * lax.rsqrt(ms + RMS_EPS) * g


def _norm_proj_kernel(combine, n_w, plan, *refs):
    it = iter(refs)
    x_ref = next(it)
    if combine:
        m0_ref, m1_ref, gt_ref = next(it), next(it), next(it)
    g_ref = next(it)
    w_refs = [next(it) for _ in range(n_w)]
    o_refs = [next(it) for _ in plan]
    h_ref = next(it) if combine else None
    res_refs = [next(it) for _ in range(n_w)]

    x = x_ref[...]
    if combine:
        x = _moe_combine(x, m0_ref, m1_ref, gt_ref)
        h_ref[...] = x
    xn = _rms(x, g_ref[...]).astype(BF16)
    for w_ref, res_ref in zip(w_refs, res_refs):
        res_ref[...] = jnp.dot(xn, w_ref[...], preferred_element_type=F32)
    for o_ref, (wi, start, width, _) in zip(o_refs, plan):
        o_ref[...] = res_refs[wi][:, start:start + width].astype(o_ref.dtype)


def _norm_proj(x, g, weights, plan, *, tm, moe=None):
    t, d = x.shape
    tm = min(tm, t)
    combine = moe is not None
    row = lambda i: (i, 0)
    const = lambda i: (0, 0)
    in_specs = [pl.BlockSpec((tm, d), row)]
    args = [x]
    if combine:
        m, gates = moe
        nc = d // LANES
        in_specs += [pl.BlockSpec((tm * nc, LANES), row),
                     pl.BlockSpec((tm * nc, LANES), lambda i: (t // tm + i, 0)),
                     pl.BlockSpec((tm, LANES), row)]
        args += [m, m, gates]
    in_specs.append(pl.BlockSpec((1, d), const))
    args.append(g.reshape(1, d))
    for w in weights:
        in_specs.append(pl.BlockSpec(w.shape, const, pipeline_mode=pl.Buffered(1)))
        args.append(w)
    out_shape = [jax.ShapeDtypeStruct((t, width), dtype) for _, _, width, dtype in plan]
    out_specs = [pl.BlockSpec((tm, width), row) for _, _, width, _ in plan]
    if combine:
        out_shape.append(jax.ShapeDtypeStruct((t, d), F32))
        out_specs.append(pl.BlockSpec((tm, d), row))
    outs = pl.pallas_call(
        functools.partial(_norm_proj_kernel, combine, len(weights), plan),
        grid=(t // tm,),
        in_specs=in_specs, out_specs=out_specs, out_shape=out_shape,
        scratch_shapes=[pltpu.VMEM((tm, w.shape[1]), F32) for w in weights],
        compiler_params=_cparams(("parallel",)),
        name="norm_proj",
    )(*args)
    outs = list(outs)
    h = outs.pop() if combine else None
    return outs, h


SWA_QBLOCKS = 4


def _swa_kernel(sink_ref, slope_ref, q_ref, kp_ref, kc_ref, vp_ref, vc_ref, o_ref):
    n = pl.program_id(1)
    blk = A_WINDOW
    wide = A_GROUP * blk
    key = lax.broadcasted_iota(jnp.int32, (2 * blk, wide), 0)
    qry = lax.broadcasted_iota(jnp.int32, (2 * blk, wide), 1) % blk
    dist = blk + qry - key
    in_window = (dist >= 0) & (dist < A_WINDOW)
    distf = dist.astype(F32)
    nt = (((1,), (1,)), ((), ()))
    k_all = jnp.concatenate([kp_ref[0], kc_ref[0]], axis=0)
    v_all = jnp.concatenate([vp_ref[0], vc_ref[0]], axis=1)
    units = [(j, kh) for j in range(SWA_QBLOCKS) for kh in range(A_KV_HEADS)]
    scores = []
    for j, kh in units:
        k = k_all[j * blk:(j + 2) * blk, kh * HEAD_DIM:(kh + 1) * HEAD_DIM]
        q = jnp.concatenate([q_ref[0, j * blk:(j + 1) * blk,
                                   (kh * A_GROUP + g) * HEAD_DIM:(kh * A_GROUP + g + 1) * HEAD_DIM]
                             for g in range(A_GROUP)], axis=0)
        scores.append(lax.dot_general(k, q, nt, preferred_element_type=F32))
    probs = []
    for (j, kh), s in zip(units, scores):
        valid = in_window & ((key >= blk) | (n * SWA_QBLOCKS + j > 0))
        s = s * (HEAD_DIM ** -0.5) - slope_ref[kh:kh + 1, :] * distf
        s = jnp.where(valid, s, -jnp.inf)
        sink = sink_ref[kh:kh + 1, :]
        m = jnp.maximum(jnp.max(s, axis=0, keepdims=True), sink)
        p = jnp.exp(s - m)
        probs.append((p.astype(BF16), jnp.sum(p, axis=0, keepdims=True) + jnp.exp(sink - m)))
    for j in range(SWA_QBLOCKS):
        outs = []
        for kh in range(A_KV_HEADS):
            p, denom = probs[j * A_KV_HEADS + kh]
            v_t = v_all[kh * HEAD_DIM:(kh + 1) * HEAD_DIM, j * blk:(j + 2) * blk]
            o_t = jnp.dot(v_t, p, preferred_element_type=F32) / denom
            outs += [o_t[:, g * blk:(g + 1) * blk] for g in range(A_GROUP)]
        o_ref[0, j * blk:(j + 1) * blk, :] = jnp.concatenate(outs, axis=0).T.astype(o_ref.dtype)


def _swa(proj, sinks):
    b, s, _ = proj.shape
    blk = A_WINDOW
    qw = A_Q_HEADS * HEAD_DIM
    kw = A_KV_HEADS * HEAD_DIM
    k_blk = qw // kw
    v_t = proj[:, :, qw + kw:qw + 2 * kw].transpose(0, 2, 1)
    per_lane = lambda vec: jnp.repeat(vec.astype(F32), blk).reshape(A_KV_HEADS, A_GROUP * blk)
    slopes = jnp.asarray(2.0 ** (-8.0 * np.arange(1, A_Q_HEADS + 1) / A_Q_HEADS), F32)
    tq = SWA_QBLOCKS * blk
    prev = lambda n: jnp.maximum(n * SWA_QBLOCKS - 1, 0)
    full = pl.BlockSpec((A_KV_HEADS, A_GROUP * blk), lambda i, n: (0, 0))
    return pl.pallas_call(
        _swa_kernel,
        grid=(b, s // tq),
        in_specs=[
            full, full,
            pl.BlockSpec((1, tq, qw), lambda i, n: (i, n, 0)),
            pl.BlockSpec((1, blk, kw), lambda i, n: (i, prev(n), k_blk)),
            pl.BlockSpec((1, tq, kw), lambda i, n: (i, n, k_blk)),
            pl.BlockSpec((1, kw, blk), lambda i, n: (i, 0, prev(n))),
            pl.BlockSpec((1, kw, tq), lambda i, n: (i, 0, n)),
        ],
        out_specs=pl.BlockSpec((1, tq, qw), lambda i, n: (i, n, 0)),
        out_shape=jax.ShapeDtypeStruct((b, s, qw), BF16),
        compiler_params=_cparams(("parallel", "parallel")),
        name="swa",
    )(per_lane(sinks), per_lane(slopes), proj, proj, proj, v_t, v_t)


def _tril(n, dtype=F32):
    r = lax.broadcasted_iota(jnp.int32, (n, n), 0)
    c = lax.broadcasted_iota(jnp.int32, (n, n), 1)
    return (c <= r).astype(dtype)


def _split3(x):
    hi = x.astype(BF16)
    r = x - hi.astype(F32)
    mid = r.astype(BF16)
    return hi, mid, (r - mid.astype(F32)).astype(BF16)


def _dot_mask_lhs(mask, x):
    return sum(jnp.dot(mask, part, preferred_element_type=F32) for part in _split3(x))


def _dot_mask_rhs(x, mask):
    return sum(jnp.dot(part, mask, preferred_element_type=F32) for part in _split3(x))


def _fox_gate_kernel(f_ref, b_ref, c_ref, ct_ref):
    tri = _tril(LANES, BF16)
    carry = jnp.zeros((1, LANES), F32)
    for n in range(f_ref.shape[1] // LANES):
        rows = slice(n * LANES, (n + 1) * LANES)
        lf = jax.nn.log_sigmoid(f_ref[0, rows, :] + b_ref[...])
        cs = _dot_mask_lhs(tri, lf) + carry
        carry = cs[LANES - 1:LANES, :]
        c_ref[0, rows, :] = cs
        ct_ref[0, n] = cs.T[:B_HEADS, :]


def _fox_gate(f_aux, bias):
    b, s, _ = f_aux.shape
    nb = s // LANES
    bias_p = jnp.zeros((1, LANES), F32).at[0, :B_HEADS].set(bias.astype(F32))
    return pl.pallas_call(
        _fox_gate_kernel,
        grid=(b,),
        in_specs=[pl.BlockSpec((1, s, LANES), lambda i: (i, 0, 0)),
                  pl.BlockSpec((1, LANES), lambda i: (0, 0))],
        out_specs=[pl.BlockSpec((1, s, LANES), lambda i: (i, 0, 0)),
                   pl.BlockSpec((1, nb, B_HEADS, LANES), lambda i: (i, 0, 0, 0))],
        out_shape=[jax.ShapeDtypeStruct((b, s, LANES), F32),
                   jax.ShapeDtypeStruct((b, nb, B_HEADS, LANES), F32)],
        compiler_params=_cparams(("parallel",)),
        name="fox_gate",
    )(f_aux, bias_p)


def _fox_kernel(q0_ref, q1_ref, k0_ref, k1_ref, vt_ref, c_ref, ctq_ref, o_ref, *, tq, heads_per_step):
    qi = pl.program_id(1)
    sub = tq // LANES
    key = lax.broadcasted_iota(jnp.int32, (tq, tq), 0)
    qry = lax.broadcasted_iota(jnp.int32, (tq, tq), 1)
    causal = key <= qry
    nt = (((1,), (1,)), ((), ()))
    half = B_HEADS // 2
    q_refs, k_refs = (q0_ref, q1_ref), (k0_ref, k1_ref)
    outs = []
    for h0 in range(0, B_HEADS, heads_per_step):
        heads = list(range(h0, h0 + heads_per_step))
        hsl = [slice(h * HEAD_DIM, (h + 1) * HEAD_DIM) for h in heads]
        lsl = [slice((h % half) * HEAD_DIM, (h % half + 1) * HEAD_DIM) for h in heads]
        qs = [q_refs[h // half][0, :, ls] * (HEAD_DIM ** -0.5)
              for h, ls in zip(heads, lsl)]
        cqs = [jnp.concatenate([ctq_ref[0, u, h:h + 1, :] for u in range(sub)], axis=1) for h in heads]

        def step(j, carry, masked, heads=heads, hsl=hsl, lsl=lsl, qs=qs, cqs=cqs):
            start = pl.multiple_of(j * tq, tq)
            sts = [lax.dot_general(k_refs[h // half][0, pl.ds(start, tq), ls], q, nt,
                                   preferred_element_type=F32)
                   for h, ls, q in zip(heads, lsl, qs)]
            ps, stats = [], []
            for idx, h in enumerate(heads):
                m, l, _ = carry[3 * idx:3 * idx + 3]
                ck = c_ref[0, pl.ds(start, tq), h:h + 1]
                st = (sts[idx] - ck) + cqs[idx]
                if masked:
                    st = jnp.where(causal, st, -jnp.inf)
                m_new = jnp.maximum(m, jnp.max(st, axis=0, keepdims=True))
                alpha = jnp.exp(m - m_new)
                p = jnp.exp(st - m_new)
                stats.append((m_new, alpha, alpha * l + jnp.sum(p, axis=0, keepdims=True)))
                ps.append(p.astype(BF16))
            new = []
            for idx in range(len(heads)):
                m_new, alpha, l = stats[idx]
                pv = jnp.dot(vt_ref[0, j, hsl[idx], :], ps[idx], preferred_element_type=F32)
                new += [m_new, l, alpha * carry[3 * idx + 2] + pv]
            return tuple(new)

        init = (jnp.full((1, tq), -jnp.inf, F32), jnp.zeros((1, tq), F32),
                jnp.zeros((HEAD_DIM, tq), F32)) * heads_per_step
        carry = lax.fori_loop(0, qi, functools.partial(step, masked=False), init)
        carry = step(qi, carry, True)
        for idx in range(heads_per_step):
            outs.append(carry[3 * idx + 2] / carry[3 * idx + 1])
    o_ref[0] = jnp.concatenate(outs, axis=0).T.astype(o_ref.dtype)


def _fox(proj, c, ct, *, tq=256, heads_per_step=8):
    b, s, _ = proj.shape
    w = B_HEADS * HEAD_DIM
    nk = s // tq
    sub = tq // LANES
    hw = w // 2
    base = (A_Q_HEADS + 2 * A_KV_HEADS) * HEAD_DIM
    qb, kb = base // hw, (base + w) // hw
    v_t = proj[:, :, base + 2 * w:base + 3 * w].reshape(b, nk, tq, w).transpose(0, 1, 3, 2)
    return pl.pallas_call(
        functools.partial(_fox_kernel, tq=tq, heads_per_step=heads_per_step),
        grid=(b, s // tq),
        in_specs=[
            pl.BlockSpec((1, tq, hw), lambda i, n: (i, n, qb)),
            pl.BlockSpec((1, tq, hw), lambda i, n: (i, n, qb + 1)),
            pl.BlockSpec((1, s, hw), lambda i, n: (i, 0, kb)),
            pl.BlockSpec((1, s, hw), lambda i, n: (i, 0, kb + 1)),
            pl.BlockSpec((1, nk, w, tq), lambda i, n: (i, 0, 0, 0)),
            pl.BlockSpec((1, s, LANES), lambda i, n: (i, 0, 0)),
            pl.BlockSpec((1, sub, B_HEADS, LANES), lambda i, n: (i, n, 0, 0)),
        ],
        out_specs=pl.BlockSpec((1, tq, w), lambda i, n: (i, n, 0)),
        out_shape=jax.ShapeDtypeStruct((b, s, w), BF16),
        compiler_params=_cparams(("parallel", "parallel")),
        name="fox",
    )(proj, proj, proj, proj, v_t, c, ct)


def _ssd_gla_kernel(z_ref, q_ref, k_ref, v_ref, r_ref, xc_ref, xp_ref, sdt_ref, sg_ref,
                    cw_ref, cb_ref, dtb_ref, alog_ref, dsk_ref, ex_ref, sn_ref, gkw_ref, gkb_ref, gn_ref,
                    o_ref, conv_ref, hs_ref, gs_ref):
    c = pl.program_id(1)
    q_len = C_CHUNK
    halo = 8

    @pl.when(c == 0)
    def _():
        hs_ref[...] = jnp.zeros_like(hs_ref)
        gs_ref[...] = jnp.zeros_like(gs_ref)

    prev = xp_ref[0]
    conv_ref[0:halo, :] = jnp.where(c > 0, prev, jnp.zeros_like(prev))
    conv_ref[halo:halo + q_len, :] = xc_ref[0]
    acc = jnp.zeros((q_len, C_CONV_DIM), F32) + cb_ref[...]
    for j in range(C_CONV):
        off = halo - (C_CONV - 1) + j
        acc = acc + cw_ref[j:j + 1, :] * conv_ref[off:off + q_len, :]
    xbc = jax.nn.silu(acc)
    xs = xbc[:, :C_INNER]
    gs_w = C_GROUPS * C_STATE
    bm = xbc[:, C_INNER:C_INNER + gs_w].astype(BF16)
    cm = xbc[:, C_INNER + gs_w:].astype(BF16)

    row = lax.broadcasted_iota(jnp.int32, (q_len, q_len), 0)
    col = lax.broadcasted_iota(jnp.int32, (q_len, q_len), 1)
    tri = col <= row

    lane = lax.broadcasted_iota(jnp.int32, (1, LANES), 1)
    dt = jnp.where(lane < C_HEADS, jax.nn.softplus(sdt_ref[0] + dtb_ref[...]), 0.0)
    dta = dt * -jnp.exp(alog_ref[...])
    acs = _dot_mask_lhs(tri.astype(BF16), dta)
    acs_t = acs.T
    chunk_dec = jnp.exp(acs[q_len - 1:q_len, :])
    expand = ex_ref[...]
    dt_x = _dot_mask_rhs(dt, expand)
    acs_x = _dot_mask_rhs(acs, expand)
    xd = xs * dt_x
    xd_b = xd.astype(BF16)
    xdd = xd * jnp.exp(acs_x[q_len - 1:q_len, :] - acs_x)
    low_half = lax.broadcasted_iota(jnp.int32, (q_len, LANES), 1) < C_HEAD_DIM

    y_pairs, y_offs = [], []
    tdims = (((1,), (1,)), ((), ()))
    for g in range(C_GROUPS):
        b_g = bm[:, g * C_STATE:(g + 1) * C_STATE]
        c_g = cm[:, g * C_STATE:(g + 1) * C_STATE]
        cb = lax.dot_general(c_g, b_g, tdims, preferred_element_type=F32)
        h0 = g * C_HPG
        grp = slice(h0 * C_HEAD_DIM, (h0 + C_HPG) * C_HEAD_DIM)
        y_offs.append(lax.dot_general(c_g, hs_ref[grp, :].astype(BF16), tdims, preferred_element_type=F32))
        for h in range(h0, h0 + C_HPG, 2):
            xp = xd_b[:, h * C_HEAD_DIM:(h + 2) * C_HEAD_DIM]
            halves = []
            for hh in (h, h + 1):
                seg = jnp.exp(jnp.where(tri, acs[:, hh:hh + 1] - acs_t[hh:hh + 1, :], -jnp.inf))
                halves.append(jnp.dot((cb * seg).astype(BF16), xp, preferred_element_type=F32))
            y_pairs.append(jnp.where(low_half, halves[0], halves[1]))
        upd = jnp.dot(xdd[:, grp].T.astype(BF16), b_g, preferred_element_type=F32)
        for hh in range(C_HPG):
            h = h0 + hh
            ps = slice(h * C_HEAD_DIM, (h + 1) * C_HEAD_DIM)
            us = slice(hh * C_HEAD_DIM, (hh + 1) * C_HEAD_DIM)
            hs_ref[ps, :] = hs_ref[ps, :] * chunk_dec[0:1, h:h + 1] + upd[us, :]
    y = (jnp.concatenate(y_pairs, axis=1) + jnp.concatenate(y_offs, axis=1) * jnp.exp(acs_x)
         + dsk_ref[...] * xs)
    y = y * jax.nn.silu(z_ref[0].astype(F32))
    o_ref[0, :, :C_INNER] = _rms(y, sn_ref[...]).astype(o_ref.dtype)

    same = (row // D_CHUNK) == (col // D_CHUNK)
    tri2 = tri & same
    la = jnp.dot(sg_ref[0].astype(BF16), gkw_ref[...], preferred_element_type=F32) + gkb_ref[...]
    la = jax.nn.log_sigmoid(la) / D_GATE_NORM
    gcs = _dot_mask_lhs(tri2.astype(BF16), la)
    first = lax.broadcasted_iota(jnp.int32, (q_len, 1), 0) < D_CHUNK
    r_all = r_ref[0]
    for h in range(D_HEADS):
        ks = slice(h * D_HK, (h + 1) * D_HK)
        vs = slice(h * D_HV, (h + 1) * D_HV)
        g_h = gcs[:, ks]
        g_end0 = g_h[D_CHUNK - 1:D_CHUNK, :]
        g_end1 = g_h[q_len - 1:q_len, :]
        q_h = q_ref[0, :, ks].astype(F32) * (D_HK ** -0.5)
        k_h = k_ref[0, :, ks].astype(F32)
        v_h = v_ref[0, :, vs]
        q_dec = (q_h * jnp.exp(g_h)).astype(BF16)
        k_inv = (k_h * jnp.exp(-g_h)).astype(BF16)
        k_end = k_h * jnp.exp(jnp.where(first, g_end0, g_end1) - g_h)
        ke0 = jnp.where(first, k_end, 0.0).astype(BF16)
        ke1 = jnp.where(first, 0.0, k_end).astype(BF16)
        attn = lax.dot_general(q_dec, k_inv, (((1,), (1,)), ((), ())), preferred_element_type=F32)
        attn = jnp.where(tri2, attn, 0.0).astype(BF16)
        o = jnp.dot(attn, v_h, preferred_element_type=F32)
        v_t = v_h.astype(F32).T.astype(BF16)
        st_rows = slice(h * D_HV, (h + 1) * D_HV)
        s0 = gs_ref[st_rows, :]
        s1 = s0 * jnp.exp(g_end0) + jnp.dot(v_t, ke0, preferred_element_type=F32)
        s2 = s1 * jnp.exp(g_end1) + jnp.dot(v_t, ke1, preferred_element_type=F32)
        gs_ref[st_rows, :] = s2
        tdims = (((1,), (1,)), ((), ()))
        o0 = lax.dot_general(q_dec, s0.astype(BF16), tdims, preferred_element_type=F32)
        o1 = lax.dot_general(q_dec, s1.astype(BF16), tdims, preferred_element_type=F32)
        o = o + jnp.where(first, o0, o1)
        o = _rms(o, gn_ref[...]) * jax.nn.silu(r_all[:, vs].astype(F32))
        o_ref[0, :, C_INNER + h * D_HV:C_INNER + (h + 1) * D_HV] = o.astype(o_ref.dtype)


def _ssd_gla(z, q, k, v, r, xbc, side_dt, side_g, g_lane, p):
    b, s, _ = z.shape
    q_len = C_CHUNK
    chunk = lambda width: pl.BlockSpec((1, q_len, width), lambda i, n: (i, n, 0))
    full = lambda shape: pl.BlockSpec(shape, lambda i, n: (0,) * len(shape))
    pad_lanes = lambda vec: jnp.zeros((1, LANES), F32).at[0, :vec.shape[0]].set(vec.astype(F32))
    gkw = jnp.zeros((LANES, D_KEY), F32).at[g_lane:g_lane + D_GATE_RANK].set(p["gk_w"]).astype(BF16)
    expand = jnp.asarray(np.arange(C_INNER)[None, :] // C_HEAD_DIM == np.arange(LANES)[:, None], BF16)
    return pl.pallas_call(
        _ssd_gla_kernel,
        grid=(b, s // q_len),
        in_specs=[
            chunk(C_INNER), chunk(D_KEY), chunk(D_KEY), chunk(D_VAL), chunk(D_VAL), chunk(C_CONV_DIM),
            pl.BlockSpec((1, 8, C_CONV_DIM), lambda i, n: (i, jnp.maximum(n * (q_len // 8) - 1, 0), 0)),
            chunk(LANES), chunk(LANES),
            full((C_CONV, C_CONV_DIM)), full((1, C_CONV_DIM)),
            full((1, LANES)), full((1, LANES)), full((1, C_INNER)), full((LANES, C_INNER)),
            full((1, C_INNER)), full((LANES, D_KEY)), full((1, D_KEY)), full((1, D_HV)),
        ],
        out_specs=pl.BlockSpec((1, q_len, C_INNER + D_VAL), lambda i, n: (i, n, 0)),
        out_shape=jax.ShapeDtypeStruct((b, s, C_INNER + D_VAL), BF16),
        scratch_shapes=[pltpu.VMEM((8 + q_len, C_CONV_DIM), F32),
                        pltpu.VMEM((C_INNER, C_STATE), F32),
                        pltpu.VMEM((D_VAL, D_HK), F32)],
        compiler_params=_cparams(("parallel", "arbitrary")),
        name="ssd_gla",
    )(z, q, k, v, r, xbc, xbc, side_dt, side_g,
      p["conv_w"].astype(F32), p["conv_b"].reshape(1, -1).astype(F32),
      pad_lanes(p["dt_bias"]), pad_lanes(p["a_log"]),
      jnp.repeat(p["d_skip"].astype(F32), C_HEAD_DIM).reshape(1, C_INNER), expand,
      p["ssd_norm"].reshape(1, -1).astype(F32), gkw, p["gk_b"].reshape(1, -1).astype(F32),
      p["gla_norm"].reshape(1, -1).astype(F32))


def _store_rows_tiled(ref, val):
    m, d = val.shape
    nc = d // LANES
    for c in range(nc):
        ref[pl.ds(c, m, stride=nc), :] = val[:, c * LANES:(c + 1) * LANES]


def _load_rows_tiled(ref, m, dtype=None):
    nc = ref.shape[0] // m
    parts = [ref[pl.ds(c, m, stride=nc), :] for c in range(nc)]
    if dtype is not None:
        parts = [p.astype(dtype) for p in parts]
    return jnp.concatenate(parts, axis=1)


def _pack_pairs(x):
    n = x.shape[1] // 2
    u = pltpu.bitcast(x.astype(BF16).astype(F32), jnp.uint32)
    return (u[:, :n] >> 16) | (u[:, n:] & jnp.uint32(0xFFFF0000))


def _unpack_pairs(u):
    lo = pltpu.bitcast(u << 16, F32).astype(BF16)
    hi = pltpu.bitcast(u & jnp.uint32(0xFFFF0000), F32).astype(BF16)
    return jnp.concatenate([lo, hi], axis=1)


RT_GATE, RT_EXPERT, RT_RANK = 0, 2, 4


def _route_block(lg, carry):
    m = lg.shape[0]
    lane = lax.broadcasted_iota(jnp.int32, (m, LANES), 1)
    lane_f = lane.astype(F32)
    none = float(LANES)
    neg = -jnp.inf
    first_max = lambda v, vmax: jnp.min(jnp.where(v == vmax, lane_f, none), axis=-1, keepdims=True)
    gl = jnp.where(lane < MOE_GROUPS, lg, neg)
    gmax = jnp.max(gl, axis=-1, keepdims=True)
    g_w = 1.0 / jnp.sum(jnp.exp(gl - gmax), axis=-1, keepdims=True)
    lo = MOE_GROUPS + first_max(gl, gmax) * MOE_EPG
    el = jnp.where((lane_f >= lo) & (lane_f < lo + MOE_EPG), lg, neg)
    emax = jnp.max(el, axis=-1, keepdims=True)
    esum = jnp.sum(jnp.exp(el - emax), axis=-1, keepdims=True)
    l0 = first_max(el, emax)
    el2 = jnp.where(lane_f == l0, neg, el)
    emax2 = jnp.max(el2, axis=-1, keepdims=True)
    l1 = first_max(el2, emax2)
    p0 = 1.0 / esum
    p1 = jnp.exp(emax2 - emax) / esum
    w0 = g_w * (p0 / (p0 + p1))
    w1 = g_w * (p1 / (p0 + p1))
    oh0 = lane_f == l0
    oh1 = lane_f == l1
    oh = (oh0 | oh1).astype(BF16)
    r = lax.broadcasted_iota(jnp.int32, (m, m), 0)
    c = lax.broadcasted_iota(jnp.int32, (m, m), 1)
    cum = jnp.dot((c < r).astype(BF16), oh, preferred_element_type=F32) + carry
    rank0 = jnp.sum(jnp.where(oh0, cum, 0.0), axis=-1, keepdims=True)
    rank1 = jnp.sum(jnp.where(oh1, cum, 0.0), axis=-1, keepdims=True)
    carry = carry + jnp.sum(oh.astype(F32), axis=0, keepdims=True)
    rec = jnp.zeros((m, LANES), F32)
    for pos, val in ((RT_GATE, w0), (RT_GATE + 1, w1), (RT_EXPERT, l0 - MOE_GROUPS),
                     (RT_EXPERT + 1, l1 - MOE_GROUPS), (RT_RANK, rank0), (RT_RANK + 1, rank1)):
        rec = jnp.where(lane == pos, val, rec)
    return rec, carry


def _out_proj_kernel(n_parts, *refs):
    a_refs = refs[:n_parts]
    w_refs = refs[n_parts:2 * n_parts]
    h_ref, g_ref, wr_ref, br_ref, ho_ref, xt_ref, rt_ref, cnt_ref, carry_ref = refs[2 * n_parts:]

    @pl.when(pl.program_id(0) == 0)
    def _():
        carry_ref[...] = jnp.zeros_like(carry_ref)

    acc = h_ref[...]
    for a_ref, w_ref in zip(a_refs, w_refs):
        acc = acc + jnp.dot(a_ref[...], w_ref[...], preferred_element_type=F32)
    ho_ref[...] = acc
    xn = _rms(acc, g_ref[...])
    xt_ref[...] = xn.astype(xt_ref.dtype)
    x_hi, x_mid, _ = _split3(xn)
    wr = wr_ref[...]
    lg2 = jnp.dot(x_hi, wr, preferred_element_type=F32)
    lg = (lg2[:, :LANES] + lg2[:, LANES:] + jnp.dot(x_mid, wr[:, :LANES], preferred_element_type=F32)
          + br_ref[...])
    rec, carry = _route_block(lg, carry_ref[...])
    rt_ref[...] = rec
    carry_ref[...] = carry
    cnt_ref[0] = carry


def _out_proj(parts, w_parts, h, g, w_route, b_route, *, tm=512):
    t, d = h.shape
    tm = min(tm, t)
    nc = d // LANES
    row = lambda i: (i, 0)
    const = lambda i: (0, 0)
    in_specs = [pl.BlockSpec((tm, a.shape[1]), row) for a in parts]
    in_specs += [pl.BlockSpec(w.shape, const) for w in w_parts]
    in_specs += [pl.BlockSpec((tm, d), row), pl.BlockSpec((1, d), const),
                 pl.BlockSpec((d, 2 * LANES), const), pl.BlockSpec((1, LANES), const)]
    return pl.pallas_call(
        functools.partial(_out_proj_kernel, len(parts)),
        grid=(t // tm,),
        in_specs=in_specs,
        out_specs=[pl.BlockSpec((tm, d), row), pl.BlockSpec((tm, d), row),
                   pl.BlockSpec((tm, LANES), row), pl.BlockSpec((1, 1, LANES), lambda i: (i, 0, 0))],
        out_shape=[jax.ShapeDtypeStruct((t, d), F32), jax.ShapeDtypeStruct((t, d), BF16),
                   jax.ShapeDtypeStruct((t, LANES), F32), jax.ShapeDtypeStruct((t // tm, 1, LANES), F32)],
        scratch_shapes=[pltpu.VMEM((1, LANES), F32)],
        compiler_params=_cparams(("arbitrary",)),
        name="out_proj",
    )(*parts, *w_parts, h, g.reshape(1, d), w_route, b_route)


def _zero_fill_rows(rows_ref, z_ref, zsem, lo_ref, hi_ref, n_rows, nc):
    zb = z_ref.shape[0] // nc
    z_ref[...] = jnp.zeros_like(z_ref)
    assert MOE_TILE // 2 <= zb and MOE_TILE % zb == 0

    def piece(row0, n):
        return pltpu.make_async_copy(z_ref.at[pl.ds(0, n * nc)], rows_ref.at[pl.ds(row0 * nc, n * nc)], zsem)

    def sweep(issue):
        def per_expert(e, carry):
            off = lo_ref[e]
            for n, hit in _run_pieces(hi_ref[e] - off, MOE_TILE // 2):
                @pl.when(hit)
                def _(off=off, n=n):
                    piece(off, n).start() if issue else piece(off, n).wait()

                off = off + jnp.where(hit, n, 0)
            return carry

        def per_block(i, carry):
            piece(i * zb, zb).start() if issue else piece(i * zb, zb).wait()
            return carry

        lax.fori_loop(0, MOE_EXPERTS, per_expert, 0)
        lax.fori_loop(hi_ref[MOE_EXPERTS - 1] // zb, n_rows // zb, per_block, 0)

    sweep(True)
    sweep(False)


def _tile_positions(rt, pos_base):
    lane_f = lax.broadcasted_iota(jnp.int32, rt.shape, 1).astype(F32)
    out = []
    for k in range(MOE_TOPK):
        e_lane = rt[:, RT_EXPERT + k:RT_EXPERT + k + 1] + MOE_GROUPS
        out.append(rt[:, RT_RANK + k:RT_RANK + k + 1]
                   + jnp.sum(jnp.where(lane_f == e_lane, pos_base, 0.0), axis=-1, keepdims=True))
    return out


def _moe_dispatch_kernel(dst_ref, len_ref, lo_ref, hi_ref, x_ref, rt_ref, pb_ref, rows_ref,
                         sbuf, z_ref, sem, zsem, *, tm, nc, n_steps, n_rows):
    i = pl.program_id(0)
    rows = MOE_TOPK * tm
    slot = lax.rem(i, 2)

    def wait_slot(s):
        pltpu.make_async_copy(sbuf.at[s], rows_ref.at[pl.ds(0, rows * nc)], sem.at[s]).wait()

    @pl.when(i == 0)
    def _():
        _zero_fill_rows(rows_ref, z_ref, zsem, lo_ref, hi_ref, n_rows, nc)

    @pl.when(i >= 2)
    def _():
        wait_slot(slot)

    pos = _tile_positions(rt_ref[...], pb_ref[0])
    lane = lax.broadcasted_iota(jnp.int32, (tm, LANES), 1)
    pos_t = jnp.where(lane == 0, pos[0], jnp.where(lane == 1, pos[1], 0.0)).T
    p_iota = lax.broadcasted_iota(jnp.int32, (rows, tm), 0).astype(F32)
    place = ((p_iota == pos_t[0:1, :]) | (p_iota == pos_t[1:2, :])).astype(BF16)
    _store_rows_tiled(sbuf.at[slot], _pack_pairs(jnp.dot(place, x_ref[...], preferred_element_type=F32)))

    def per_expert(e, src):
        seg = i * MOE_EXPERTS + e
        dst = dst_ref[seg]
        for n, hit in _run_pieces(len_ref[seg], rows):
            @pl.when(hit)
            def _(src=src, dst=dst, n=n):
                pltpu.make_async_copy(sbuf.at[slot, pl.ds(src * nc, n * nc)],
                                      rows_ref.at[pl.ds(dst * nc, n * nc)], sem.at[slot]).start()

            step = jnp.where(hit, n, 0)
            src, dst = src + step, dst + step
        return src

    lax.fori_loop(0, MOE_EXPERTS, per_expert, jnp.int32(0))

    @pl.when(i == n_steps - 1)
    def _():
        wait_slot(slot)
        if n_steps > 1:
            wait_slot(1 - slot)


def _moe_dispatch(x, route, pos_base, seg_dst, seg_len, pad_lo, pad_hi, n_rows, *, tm):
    t, d = x.shape
    nc = d // (2 * LANES)
    n_steps = t // tm
    row = lambda i, *_: (i, 0)
    return pl.pallas_call(
        functools.partial(_moe_dispatch_kernel, tm=tm, nc=nc, n_steps=n_steps, n_rows=n_rows),
        grid_spec=pltpu.PrefetchScalarGridSpec(
            num_scalar_prefetch=4, grid=(n_steps,),
            in_specs=[pl.BlockSpec((tm, d), row), pl.BlockSpec((tm, LANES), row),
                      pl.BlockSpec((1, 1, LANES), lambda i, *_: (i, 0, 0))],
            out_specs=pl.BlockSpec(memory_space=pl.ANY),
            scratch_shapes=[pltpu.VMEM((2, MOE_TOPK * tm * nc, LANES), jnp.uint32),
                            pltpu.VMEM((MOE_TILE // 2 * nc, LANES), jnp.uint32),
                            pltpu.SemaphoreType.DMA((2,)), pltpu.SemaphoreType.DMA(())]),
        out_shape=jax.ShapeDtypeStruct((n_rows * nc, LANES), jnp.uint32),
        compiler_params=pltpu.CompilerParams(dimension_semantics=("arbitrary",), has_side_effects=True,
                                             vmem_limit_bytes=VMEM_LIMIT),
        name="moe_dispatch",
    )(seg_dst, seg_len, pad_lo, pad_hi, x, route, pos_base)


def _moe_kernel(te_ref, nu_ref, x_ref, wg_ref, wu_ref, wd_ref, y_ref):
    i = pl.program_id(0)

    @pl.when(i >= nu_ref[0])
    def _():
        y_ref[...] = jnp.zeros_like(y_ref)

    @pl.when(i < nu_ref[0])
    def _():
        x = _unpack_pairs(_load_rows_tiled(x_ref, MOE_TILE))
        gate = jnp.dot(x, wg_ref[...].astype(BF16), preferred_element_type=F32)
        up = jnp.dot(x, wu_ref[...].astype(BF16), preferred_element_type=F32)
        act = (jax.nn.silu(gate) * up).astype(BF16)
        y = jnp.dot(act, wd_ref[...].astype(BF16), preferred_element_type=F32)
        _store_rows_tiled(y_ref, _pack_pairs(y))


def _moe_experts(x_rows, tile_expert, n_used, w_gate, w_up, w_down, layer):
    d, ff = w_gate.shape[-2:]
    nc = d // (2 * LANES)
    n_tiles = x_rows.shape[0] // (MOE_TILE * nc)
    live = lambda i, nu: jnp.minimum(i, nu[0] - 1)
    w_spec = lambda a, b: pl.BlockSpec((None, None, a, b), lambda i, te, nu: (layer, te[live(i, nu)], 0, 0))
    return pl.pallas_call(
        _moe_kernel,
        grid_spec=pltpu.PrefetchScalarGridSpec(
            num_scalar_prefetch=2, grid=(n_tiles,),
            in_specs=[pl.BlockSpec((MOE_TILE * nc, LANES), lambda i, te, nu: (live(i, nu), 0)),
                      w_spec(d, ff), w_spec(d, ff), w_spec(ff, d)],
            out_specs=pl.BlockSpec((MOE_TILE * nc, LANES), lambda i, te, nu: (i, 0))),
        out_shape=jax.ShapeDtypeStruct(x_rows.shape, x_rows.dtype),
        compiler_params=_cparams(("arbitrary",)),
        name="moe_experts",
    )(tile_expert, n_used, x_rows, w_gate, w_up, w_down)


def _moe_combine(h, m0_ref, m1_ref, rt_ref):
    rt = rt_ref[...]
    tm = h.shape[0]
    return (h + _load_rows_tiled(m0_ref, tm) * rt[:, RT_GATE:RT_GATE + 1]
            + _load_rows_tiled(m1_ref, tm) * rt[:, RT_GATE + 1:RT_GATE + 2])


def _run_pieces(run, max_rows):
    return [(n, (run & n) != 0) for n in (1 << b for b in reversed(range(max_rows.bit_length())))]


def _moe_combine_kernel(src_ref, len_ref, h_ref, rt_ref, pb_ref, g_ref, y_hbm, o_ref, ybuf, sem, *, tm, nc, final):
    i = pl.program_id(0)
    n_steps = pl.num_programs(0)
    rows = MOE_TOPK * tm

    def fetch(tile, slot):
        def per_expert(e, dst):
            seg = tile * MOE_EXPERTS + e
            src = src_ref[seg]
            for n, hit in _run_pieces(len_ref[seg], rows):
                @pl.when(hit)
                def _(src=src, dst=dst, n=n):
                    pltpu.make_async_copy(y_hbm.at[pl.ds(src * nc, n * nc)],
                                          ybuf.at[slot, pl.ds(dst * nc, n * nc)], sem.at[slot]).start()

                step = jnp.where(hit, n, 0)
                src, dst = src + step, dst + step
            return dst

        lax.fori_loop(0, MOE_EXPERTS, per_expert, jnp.int32(0))

    slot = lax.rem(i, 2)

    @pl.when(i == 0)
    def _():
        fetch(0, 0)

    @pl.when(i + 1 < n_steps)
    def _():
        fetch(i + 1, 1 - slot)

    pltpu.make_async_copy(y_hbm.at[pl.ds(0, rows * nc)], ybuf.at[slot], sem.at[slot]).wait()
    y = _unpack_pairs(_load_rows_tiled(ybuf.at[slot], rows))
    rt = rt_ref[...]
    pos_f = lax.broadcasted_iota(jnp.int32, (tm, rows), 1).astype(F32)
    pick = jnp.zeros((tm, rows), F32)
    for k, pos in enumerate(_tile_positions(rt, pb_ref[0])):
        pick = jnp.where(pos_f == pos, rt[:, RT_GATE + k:RT_GATE + k + 1], pick)
    out = h_ref[...] + jnp.dot(pick.astype(BF16), y, preferred_element_type=F32)
    o_ref[...] = _rms(out, g_ref[...]) if final else out


def _moe_combine_rows(h, y_rows, route, seg_src, seg_len, pos_base, g, *, tm, final):
    t, d = h.shape
    nc = d // (2 * LANES)
    row = lambda i, *_: (i, 0)
    return pl.pallas_call(
        functools.partial(_moe_combine_kernel, tm=tm, nc=nc, final=final),
        grid_spec=pltpu.PrefetchScalarGridSpec(
            num_scalar_prefetch=2, grid=(t // tm,),
            in_specs=[pl.BlockSpec((tm, d), row), pl.BlockSpec((tm, LANES), row),
                      pl.BlockSpec((1, 1, LANES), lambda i, *_: (i, 0, 0)),
                      pl.BlockSpec((1, d), lambda i, *_: (0, 0)),
                      pl.BlockSpec(memory_space=pl.ANY)],
            out_specs=pl.BlockSpec((tm, d), row),
            scratch_shapes=[pltpu.VMEM((2, MOE_TOPK * tm * nc, LANES), y_rows.dtype),
                            pltpu.SemaphoreType.DMA((2,))]),
        out_shape=jax.ShapeDtypeStruct((t, d), F32),
        compiler_params=_cparams(("arbitrary",)),
        name="moe_combine",
    )(seg_src, seg_len, h, route, pos_base, g.reshape(1, d), y_rows)


def _moe(h, x, route, tile_counts, w_gate, w_up, w_down, layer, g, *, final):
    t = route.shape[0]
    n_tt = tile_counts.shape[0]
    tm = t // n_tt
    after = tile_counts[:, 0, MOE_GROUPS:MOE_GROUPS + MOE_EXPERTS].astype(jnp.int32)
    before = jnp.concatenate([jnp.zeros((1, MOE_EXPERTS), jnp.int32), after[:-1]], axis=0)
    cnt = after[-1]
    padded = (cnt + MOE_TILE - 1) // MOE_TILE * MOE_TILE
    pad_ends = jnp.cumsum(padded)
    starts = (pad_ends - padded).astype(jnp.int32)
    n_tiles = (t * MOE_TOPK + MOE_EXPERTS * (MOE_TILE - 1)) // MOE_TILE
    tile_start = jnp.arange(n_tiles, dtype=jnp.int32) * MOE_TILE
    tile_expert = jnp.minimum(jnp.sum(tile_start[:, None] >= pad_ends[None, :], axis=1),
                              MOE_EXPERTS - 1).astype(jnp.int32)
    n_used = (pad_ends[-1] // MOE_TILE).astype(jnp.int32).reshape(1)
    n_rows = n_tiles * MOE_TILE
    seg_len = (after - before).reshape(-1)
    seg_off = jnp.cumsum(after - before, axis=1) - (after - before)
    seg_row = (starts[None, :] + before).reshape(-1)
    pos_base = jnp.zeros((n_tt, 1, LANES), F32).at[:, 0, MOE_GROUPS:MOE_GROUPS + MOE_EXPERTS].set(
        (seg_off - before).astype(F32))
    x_rows = _moe_dispatch(x, route, pos_base, seg_row, seg_len, starts + cnt, pad_ends.astype(jnp.int32),
                           n_rows, tm=tm)
    y_rows = _moe_experts(x_rows, tile_expert, n_used, w_gate, w_up, w_down, layer)
    return _moe_combine_rows(h, y_rows, route, seg_row, seg_len, pos_base, g, tm=tm, final=final)


def _router_weights(w_group, b_group, w_router, b_router):
    d = w_group.shape[0]
    w = jnp.zeros((d, LANES), F32)
    w = w.at[:, :MOE_GROUPS].set(w_group).at[:, MOE_GROUPS:MOE_GROUPS + MOE_EXPERTS].set(w_router)
    b = jnp.zeros((1, LANES), F32)
    b = b.at[0, :MOE_GROUPS].set(b_group).at[0, MOE_GROUPS:MOE_GROUPS + MOE_EXPERTS].set(b_router)
    w_hi = w.astype(BF16)
    w_mid = (w - w_hi.astype(F32)).astype(BF16)
    return jnp.concatenate([w_hi, w_mid], axis=1), b


def kernel(x, norm_mix, norm_moe, norm_final, even_w_in, even_sinks, even_forget_bias, even_w_out,
           odd_w_in, odd_conv_w, odd_conv_b, odd_dt_bias, odd_a_log, odd_d_skip, odd_ssd_norm,
           odd_gk_w, odd_gk_b, odd_gla_norm, odd_w_out, moe_w_group, moe_b_group, moe_w_router,
           moe_b_router, moe_w_gate, moe_w_up, moe_w_down):
    b, s, d = x.shape
    t = b * s
    depth = norm_mix.shape[0]
    h = x.reshape(t, d)
    moe = None
    for layer in range(depth):
        i = layer // 2
        if layer % 2 == 0:
            w = even_w_in[i]
            n_ab = (A_Q_HEADS + 2 * A_KV_HEADS + 3 * B_HEADS) * HEAD_DIM
            w_main = w[:, :n_ab].astype(BF16)
            w_aux = jnp.zeros((d, LANES), F32).at[:, :B_HEADS].set(w[:, n_ab:]).astype(BF16)
            (proj, f_aux), h_new = _norm_proj(h, norm_mix[layer], [w_main, w_aux],
                                              ((0, 0, n_ab, BF16), (1, 0, LANES, F32)), tm=512, moe=moe)
            h = h if h_new is None else h_new
            proj = proj.reshape(b, s, -1)
            out_a = _swa(proj, even_sinks[i])
            c, ct = _fox_gate(f_aux.reshape(b, s, LANES), even_forget_bias[i])
            out_b = _fox(proj, c, ct)
            n_ha = A_Q_HEADS * HEAD_DIM
            w_out = even_w_out[i].astype(BF16)
            parts = [out_a.reshape(t, -1), out_b.reshape(t, -1)]
            w_parts = [w_out[:n_ha], w_out[n_ha:]]
        else:
            w = odd_w_in[i]
            o_z, o_xbc = 0, C_INNER
            o_dt = o_xbc + C_CONV_DIM
            o_q = o_dt + C_HEADS
            o_k = o_q + D_KEY
            o_v = o_k + D_KEY
            o_g = o_v + D_VAL
            o_r = o_g + D_GATE_RANK
            n_b = w.shape[1] - o_dt
            w_a = w[:, :o_dt].astype(BF16)
            w_b = jnp.pad(w[:, o_dt:].astype(BF16), ((0, 0), (0, -n_b % LANES)))
            g_win = (o_g - o_dt) // LANES * LANES
            plan = ((0, o_z, C_INNER, BF16), (1, o_q - o_dt, D_KEY, BF16), (1, o_k - o_dt, D_KEY, BF16),
                    (1, o_v - o_dt, D_VAL, BF16), (1, o_r - o_dt, D_VAL, BF16), (0, o_xbc, C_CONV_DIM, F32),
                    (1, 0, LANES, F32), (1, g_win, LANES, F32))
            outs, h_new = _norm_proj(h, norm_mix[layer], [w_a, w_b], plan, tm=256, moe=moe)
            h = h if h_new is None else h_new
            params = dict(conv_w=odd_conv_w[i], conv_b=odd_conv_b[i], dt_bias=odd_dt_bias[i], a_log=odd_a_log[i],
                          d_skip=odd_d_skip[i], ssd_norm=odd_ssd_norm[i], gk_w=odd_gk_w[i], gk_b=odd_gk_b[i],
                          gla_norm=odd_gla_norm[i])
            mixed = _ssd_gla(*[o.reshape(b, s, -1) for o in outs], o_g - o_dt - g_win, params)
            parts = [mixed.reshape(t, -1)]
            w_parts = [odd_w_out[i].astype(BF16)]
        w_route, b_route = _router_weights(moe_w_group[layer], moe_b_group[layer],
                                           moe_w_router[layer], moe_b_router[layer])
        h, x_tiled, route, tile_counts = _out_proj(parts, w_parts, h, norm_moe[layer], w_route, b_route)
        h = _moe(h, x_tiled, route, tile_counts, moe_w_gate, moe_w_up, moe_w_down, layer, norm_final,
                 final=layer == depth - 1)
    out = h
    return out.reshape(b, s, d)
```

```python
import functools
import math

import numpy as np
import jax
import jax.numpy as jnp
from jax import lax
from jax.experimental import pallas as pl
from jax.experimental.pallas import tpu as pltpu

F32 = jnp.float32
BF16 = jnp.bfloat16
HIGHEST = lax.Precision.HIGHEST

RMS_EPS = 1e-6
HEAD_DIM = 64
A_Q_HEADS = 8
A_KV_HEADS = 2
A_GROUP = A_Q_HEADS // A_KV_HEADS
A_WINDOW = 128
B_HEADS = 8
C_HEADS = 16
C_HEAD_DIM = 64
C_INNER = C_HEADS * C_HEAD_DIM
C_GROUPS = 2
C_HPG = C_HEADS // C_GROUPS
C_STATE = 128
C_CONV = 4
C_CHUNK = 128
C_CONV_DIM = C_INNER + 2 * C_GROUPS * C_STATE
D_HEADS = 4
D_HK = 128
D_HV = 256
D_KEY = D_HEADS * D_HK
D_VAL = D_HEADS * D_HV
D_GATE_RANK = 16
D_GATE_NORM = 16.0
D_CHUNK = 64
MOE_GROUPS = 4
MOE_EPG = 8
MOE_EXPERTS = MOE_GROUPS * MOE_EPG
MOE_TOPK = 2

LANES = 128
VMEM_LIMIT = 48 * 1024 * 1024
MOE_TILE = 512
COPY_WINDOW = 64


def _cparams(sem):
    return pltpu.CompilerParams(dimension_semantics=sem, vmem_limit_bytes=VMEM_LIMIT)


def _rms(x, g):
    ms = jnp.mean(x * x, axis=-1, keepdims=True)
    return x * lax.rsqrt(ms + RMS_EPS) * g


def _norm_proj_kernel(combine, n_w, plan, *refs):
    it = iter(refs)
    x_ref = next(it)
    if combine:
        m0_ref, m1_ref, gt_ref = next(it), next(it), next(it)
    g_ref = next(it)
    w_refs = [next(it) for _ in range(n_w)]
    o_refs = [next(it) for _ in plan]
    h_ref = next(it) if combine else None
    res_refs = [next(it) for _ in range(n_w)]

    x = x_ref[...]
    if combine:
        x = _moe_combine(x, m0_ref, m1_ref, gt_ref)
        h_ref[...] = x
    xn = _rms(x, g_ref[...]).astype(BF16)
    for w_ref, res_ref in zip(w_refs, res_refs):
        res_ref[...] = jnp.dot(xn, w_ref[...], preferred_element_type=F32)
    for o_ref, (wi, start, width, _) in zip(o_refs, plan):
        o_ref[...] = res_refs[wi][:, start:start + width].astype(o_ref.dtype)


def _norm_proj(x, g, weights, plan, *, tm, moe=None):
    t, d = x.shape
    tm = min(tm, t)
    combine = moe is not None
    row = lambda i: (i, 0)
    const = lambda i: (0, 0)
    in_specs = [pl.BlockSpec((tm, d), row)]
    args = [x]
    if combine:
        m, gates = moe
        nc = d // LANES
        in_specs += [pl.BlockSpec((tm * nc, LANES), row),
                     pl.BlockSpec((tm * nc, LANES), lambda i: (t // tm + i, 0)),
                     pl.BlockSpec((tm, LANES), row)]
        args += [m, m, gates]
    in_specs.append(pl.BlockSpec((1, d), const))
    args.append(g.reshape(1, d))
    for w in weights:
        in_specs.append(pl.BlockSpec(w.shape, const, pipeline_mode=pl.Buffered(1)))
        args.append(w)
    out_shape = [jax.ShapeDtypeStruct((t, width), dtype) for _, _, width, dtype in plan]
    out_specs = [pl.BlockSpec((tm, width), row) for _, _, width, _ in plan]
    if combine:
        out_shape.append(jax.ShapeDtypeStruct((t, d), F32))
        out_specs.append(pl.BlockSpec((tm, d), row))
    outs = pl.pallas_call(
        functools.partial(_norm_proj_kernel, combine, len(weights), plan),
        grid=(t // tm,),
        in_specs=in_specs, out_specs=out_specs, out_shape=out_shape,
        scratch_shapes=[pltpu.VMEM((tm, w.shape[1]), F32) for w in weights],
        compiler_params=_cparams(("parallel",)),
        name="norm_proj",
    )(*args)
    outs = list(outs)
    h = outs.pop() if combine else None
    return outs, h


SWA_QBLOCKS = 4


def _swa_kernel(sink_ref, slope_ref, q_ref, kp_ref, kc_ref, vp_ref, vc_ref, o_ref):
    n = pl.program_id(1)
    blk = A_WINDOW
    wide = A_GROUP * blk
    key = lax.broadcasted_iota(jnp.int32, (2 * blk, wide), 0)
    qry = lax.broadcasted_iota(jnp.int32, (2 * blk, wide), 1) % blk
    dist = blk + qry - key
    in_window = (dist >= 0) & (dist < A_WINDOW)
    distf = dist.astype(F32)
    nt = (((1,), (1,)), ((), ()))
    k_all = jnp.concatenate([kp_ref[0], kc_ref[0]], axis=0)
    v_all = jnp.concatenate([vp_ref[0], vc_ref[0]], axis=1)
    units = [(j, kh) for j in range(SWA_QBLOCKS) for kh in range(A_KV_HEADS)]
    scores = []
    for j, kh in units:
        k = k_all[j * blk:(j + 2) * blk, kh * HEAD_DIM:(kh + 1) * HEAD_DIM]
        q = jnp.concatenate([q_ref[0, j * blk:(j + 1) * blk,
                                   (kh * A_GROUP + g) * HEAD_DIM:(kh * A_GROUP + g + 1) * HEAD_DIM]
                             for g in range(A_GROUP)], axis=0)
        scores.append(lax.dot_general(k, q, nt, preferred_element_type=F32))
    probs = []
    for (j, kh), s in zip(units, scores):
        valid = in_window & ((key >= blk) | (n * SWA_QBLOCKS + j > 0))
        s = s * (HEAD_DIM ** -0.5) - slope_ref[kh:kh + 1, :] * distf
        s = jnp.where(valid, s, -jnp.inf)
        sink = sink_ref[kh:kh + 1, :]
        m = jnp.maximum(jnp.max(s, axis=0, keepdims=True), sink)
        p = jnp.exp(s - m)
        probs.append((p.astype(BF16), jnp.sum(p, axis=0, keepdims=True) + jnp.exp(sink - m)))
    for j in range(SWA_QBLOCKS):
        outs = []
        for kh in range(A_KV_HEADS):
            p, denom = probs[j * A_KV_HEADS + kh]
            v_t = v_all[kh * HEAD_DIM:(kh + 1) * HEAD_DIM, j * blk:(j + 2) * blk]
            o_t = jnp.dot(v_t, p, preferred_element_type=F32) / denom
            outs += [o_t[:, g * blk:(g + 1) * blk] for g in range(A_GROUP)]
        o_ref[0, j * blk:(j + 1) * blk, :] = jnp.concatenate(outs, axis=0).T.astype(o_ref.dtype)


def _swa(proj, sinks):
    b, s, _ = proj.shape
    blk = A_WINDOW
    qw = A_Q_HEADS * HEAD_DIM
    kw = A_KV_HEADS * HEAD_DIM
    k_blk = qw // kw
    v_t = proj[:, :, qw + kw:qw + 2 * kw].transpose(0, 2, 1)
    per_lane = lambda vec: jnp.repeat(vec.astype(F32), blk).reshape(A_KV_HEADS, A_GROUP * blk)
    slopes = jnp.asarray(2.0 ** (-8.0 * np.arange(1, A_Q_HEADS + 1) / A_Q_HEADS), F32)
    tq = SWA_QBLOCKS * blk
    prev = lambda n: jnp.maximum(n * SWA_QBLOCKS - 1, 0)
    full = pl.BlockSpec((A_KV_HEADS, A_GROUP * blk), lambda i, n: (0, 0))
    return pl.pallas_call(
        _swa_kernel,
        grid=(b, s // tq),
        in_specs=[
            full, full,
            pl.BlockSpec((1, tq, qw), lambda i, n: (i, n, 0)),
            pl.BlockSpec((1, blk, kw), lambda i, n: (i, prev(n), k_blk)),
            pl.BlockSpec((1, tq, kw), lambda i, n: (i, n, k_blk)),
            pl.BlockSpec((1, kw, blk), lambda i, n: (i, 0, prev(n))),
            pl.BlockSpec((1, kw, tq), lambda i, n: (i, 0, n)),
        ],
        out_specs=pl.BlockSpec((1, tq, qw), lambda i, n: (i, n, 0)),
        out_shape=jax.ShapeDtypeStruct((b, s, qw), BF16),
        compiler_params=_cparams(("parallel", "parallel")),
        name="swa",
    )(per_lane(sinks), per_lane(slopes), proj, proj, proj, v_t, v_t)


def _tril(n, dtype=F32):
    r = lax.broadcasted_iota(jnp.int32, (n, n), 0)
    c = lax.broadcasted_iota(jnp.int32, (n, n), 1)
    return (c <= r).astype(dtype)


def _split3(x):
    hi = x.astype(BF16)
    r = x - hi.astype(F32)
    mid = r.astype(BF16)
    return hi, mid, (r - mid.astype(F32)).astype(BF16)


def _dot_mask_lhs(mask, x):
    return sum(jnp.dot(mask, part, preferred_element_type=F32) for part in _split3(x))


def _dot_mask_rhs(x, mask):
    return sum(jnp.dot(part, mask, preferred_element_type=F32) for part in _split3(x))


def _fox_gate_kernel(f_ref, b_ref, c_ref, ct_ref):
    tri = _tril(LANES, BF16)
    carry = jnp.zeros((1, LANES), F32)
    for n in range(f_ref.shape[1] // LANES):
        rows = slice(n * LANES, (n + 1) * LANES)
        lf = jax.nn.log_sigmoid(f_ref[0, rows, :] + b_ref[...])
        cs = _dot_mask_lhs(tri, lf) + carry
        carry = cs[LANES - 1:LANES, :]
        c_ref[0, rows, :] = cs
        ct_ref[0, n] = cs.T[:B_HEADS, :]


def _fox_gate(f_aux, bias):
    b, s, _ = f_aux.shape
    nb = s // LANES
    bias_p = jnp.zeros((1, LANES), F32).at[0, :B_HEADS].set(bias.astype(F32))
    return pl.pallas_call(
        _fox_gate_kernel,
        grid=(b,),
        in_specs=[pl.BlockSpec((1, s, LANES), lambda i: (i, 0, 0)),
                  pl.BlockSpec((1, LANES), lambda i: (0, 0))],
        out_specs=[pl.BlockSpec((1, s, LANES), lambda i: (i, 0, 0)),
                   pl.BlockSpec((1, nb, B_HEADS, LANES), lambda i: (i, 0, 0, 0))],
        out_shape=[jax.ShapeDtypeStruct((b, s, LANES), F32),
                   jax.ShapeDtypeStruct((b, nb, B_HEADS, LANES), F32)],
        compiler_params=_cparams(("parallel",)),
        name="fox_gate",
    )(f_aux, bias_p)


def _fox_kernel(q0_ref, q1_ref, k0_ref, k1_ref, vt_ref, c_ref, ctq_ref, o_ref, *, tq, heads_per_step):
    qi = pl.program_id(1)
    sub = tq // LANES
    key = lax.broadcasted_iota(jnp.int32, (tq, tq), 0)
    qry = lax.broadcasted_iota(jnp.int32, (tq, tq), 1)
    causal = key <= qry
    nt = (((1,), (1,)), ((), ()))
    half = B_HEADS // 2
    q_refs, k_refs = (q0_ref, q1_ref), (k0_ref, k1_ref)
    outs = []
    for h0 in range(0, B_HEADS, heads_per_step):
        heads = list(range(h0, h0 + heads_per_step))
        hsl = [slice(h * HEAD_DIM, (h + 1) * HEAD_DIM) for h in heads]
        lsl = [slice((h % half) * HEAD_DIM, (h % half + 1) * HEAD_DIM) for h in heads]
        qs = [q_refs[h // half][0, :, ls] * (HEAD_DIM ** -0.5)
              for h, ls in zip(heads, lsl)]
        cqs = [jnp.concatenate([ctq_ref[0, u, h:h + 1, :] for u in range(sub)], axis=1) for h in heads]

        def step(j, carry, masked, heads=heads, hsl=hsl, lsl=lsl, qs=qs, cqs=cqs):
            start = pl.multiple_of(j * tq, tq)
            sts = [lax.dot_general(k_refs[h // half][0, pl.ds(start, tq), ls], q, nt,
                                   preferred_element_type=F32)
                   for h, ls, q in zip(heads, lsl, qs)]
            ps, stats = [], []
            for idx, h in enumerate(heads):
                m, l, _ = carry[3 * idx:3 * idx + 3]
                ck = c_ref[0, pl.ds(start, tq), h:h + 1]
                st = (sts[idx] - ck) + cqs[idx]
                if masked:
                    st = jnp.where(causal, st, -jnp.inf)
                m_new = jnp.maximum(m, jnp.max(st, axis=0, keepdims=True))
                alpha = jnp.exp(m - m_new)
                p = jnp.exp(st - m_new)
                stats.append((m_new, alpha, alpha * l + jnp.sum(p, axis=0, keepdims=True)))
                ps.append(p.astype(BF16))
            new = []
            for idx in range(len(heads)):
                m_new, alpha, l = stats[idx]
                pv = jnp.dot(vt_ref[0, j, hsl[idx], :], ps[idx], preferred_element_type=F32)
                new += [m_new, l, alpha * carry[3 * idx + 2] + pv]
            return tuple(new)

        init = (jnp.full((1, tq), -jnp.inf, F32), jnp.zeros((1, tq), F32),
                jnp.zeros((HEAD_DIM, tq), F32)) * heads_per_step
        carry = lax.fori_loop(0, qi, functools.partial(step, masked=False), init)
        carry = step(qi, carry, True)
        for idx in range(heads_per_step):
            outs.append(carry[3 * idx + 2] / carry[3 * idx + 1])
    o_ref[0] = jnp.concatenate(outs, axis=0).T.astype(o_ref.dtype)


def _fox(proj, c, ct, *, tq=256, heads_per_step=8):
    b, s, _ = proj.shape
    w = B_HEADS * HEAD_DIM
    nk = s // tq
    sub = tq // LANES
    hw = w // 2
    base = (A_Q_HEADS + 2 * A_KV_HEADS) * HEAD_DIM
    qb, kb = base // hw, (base + w) // hw
    v_t = proj[:, :, base + 2 * w:base + 3 * w].reshape(b, nk, tq, w).transpose(0, 1, 3, 2)
    return pl.pallas_call(
        functools.partial(_fox_kernel, tq=tq, heads_per_step=heads_per_step),
        grid=(b, s // tq),
        in_specs=[
            pl.BlockSpec((1, tq, hw), lambda i, n: (i, n, qb)),
            pl.BlockSpec((1, tq, hw), lambda i, n: (i, n, qb + 1)),
            pl.BlockSpec((1, s, hw), lambda i, n: (i, 0, kb)),
            pl.BlockSpec((1, s, hw), lambda i, n: (i, 0, kb + 1)),
            pl.BlockSpec((1, nk, w, tq), lambda i, n: (i, 0, 0, 0)),
            pl.BlockSpec((1, s, LANES), lambda i, n: (i, 0, 0)),
            pl.BlockSpec((1, sub, B_HEADS, LANES), lambda i, n: (i, n, 0, 0)),
        ],
        out_specs=pl.BlockSpec((1, tq, w), lambda i, n: (i, n, 0)),
        out_shape=jax.ShapeDtypeStruct((b, s, w), BF16),
        compiler_params=_cparams(("parallel", "parallel")),
        name="fox",
    )(proj, proj, proj, proj, v_t, c, ct)


def _ssd_gla_kernel(z_ref, q_ref, k_ref, v_ref, r_ref, xc_ref, xp_ref, sdt_ref, sg_ref,
                    cw_ref, cb_ref, dtb_ref, alog_ref, dsk_ref, ex_ref, sn_ref, gkw_ref, gkb_ref, gn_ref,
                    o_ref, hs_ref, gs_ref):
    c = pl.program_id(1)
    q_len = C_CHUNK
    halo = xp_ref.shape[1]

    @pl.when(c == 0)
    def _():
        hs_ref[...] = jnp.zeros_like(hs_ref)
        gs_ref[...] = jnp.zeros_like(gs_ref)

    prev = xp_ref[0]
    cur = xc_ref[0]
    ext = jnp.concatenate([jnp.where(c > 0, prev, jnp.zeros_like(prev)), cur], axis=0)
    t_out = lax.broadcasted_iota(jnp.int32, (q_len, halo + q_len), 0)
    t_in = lax.broadcasted_iota(jnp.int32, (q_len, halo + q_len), 1) - halo
    acc = cb_ref[...] + cw_ref[C_CONV - 1:C_CONV, :] * cur.astype(F32)
    for j in range(C_CONV - 1):
        shift = (t_in == t_out - (C_CONV - 1 - j)).astype(BF16)
        acc = acc + cw_ref[j:j + 1, :] * jnp.dot(shift, ext, preferred_element_type=F32)
    xbc = jax.nn.silu(acc)
    xs = xbc[:, :C_INNER]
    gs_w = C_GROUPS * C_STATE
    bm = xbc[:, C_INNER:C_INNER + gs_w].astype(BF16)
    cm = xbc[:, C_INNER + gs_w:].astype(BF16)

    row = lax.broadcasted_iota(jnp.int32, (q_len, q_len), 0)
    col = lax.broadcasted_iota(jnp.int32, (q_len, q_len), 1)
    tri = col <= row

    lane = lax.broadcasted_iota(jnp.int32, (1, LANES), 1)
    dt = jnp.where(lane < C_HEADS, jax.nn.softplus(sdt_ref[0] + dtb_ref[...]), 0.0)
    dta = dt * -jnp.exp(alog_ref[...])
    acs = _dot_mask_lhs(tri.astype(BF16), dta)
    acs_t = acs.T
    chunk_dec = jnp.exp(acs[q_len - 1:q_len, :])
    expand = ex_ref[...]
    dt_x = _dot_mask_rhs(dt, expand)
    acs_x = _dot_mask_rhs(acs, expand)
    xd = xs * dt_x
    xd_b = xd.astype(BF16)
    xdd = xd * jnp.exp(acs_x[q_len - 1:q_len, :] - acs_x)
    low_half = lax.broadcasted_iota(jnp.int32, (q_len, LANES), 1) < C_HEAD_DIM

    y_pairs, y_offs = [], []
    tdims = (((1,), (1,)), ((), ()))
    for g in range(C_GROUPS):
        b_g = bm[:, g * C_STATE:(g + 1) * C_STATE]
        c_g = cm[:, g * C_STATE:(g + 1) * C_STATE]
        cb = lax.dot_general(c_g, b_g, tdims, preferred_element_type=F32)
        h0 = g * C_HPG
        grp = slice(h0 * C_HEAD_DIM, (h0 + C_HPG) * C_HEAD_DIM)
        y_offs.append(lax.dot_general(c_g, hs_ref[grp, :].astype(BF16), tdims, preferred_element_type=F32))
        for h in range(h0, h0 + C_HPG, 2):
            xp = xd_b[:, h * C_HEAD_DIM:(h + 2) * C_HEAD_DIM]
            halves = []
            for hh in (h, h + 1):
                seg = jnp.exp(jnp.where(tri, acs[:, hh:hh + 1] - acs_t[hh:hh + 1, :], -jnp.inf))
                halves.append(jnp.dot((cb * seg).astype(BF16), xp, preferred_element_type=F32))
            y_pairs.append(jnp.where(low_half, halves[0], halves[1]))
        upd = jnp.dot(xdd[:, grp].T.astype(BF16), b_g, preferred_element_type=F32)
        for hh in range(C_HPG):
            h = h0 + hh
            ps = slice(h * C_HEAD_DIM, (h + 1) * C_HEAD_DIM)
            us = slice(hh * C_HEAD_DIM, (hh + 1) * C_HEAD_DIM)
            hs_ref[ps, :] = hs_ref[ps, :] * chunk_dec[0:1, h:h + 1] + upd[us, :]
    y = (jnp.concatenate(y_pairs, axis=1) + jnp.concatenate(y_offs, axis=1) * jnp.exp(acs_x)
         + dsk_ref[...] * xs)
    y = y * jax.nn.silu(z_ref[0].astype(F32))
    o_ref[0, :, :C_INNER] = _rms(y, sn_ref[...]).astype(o_ref.dtype)

    same = (row // D_CHUNK) == (col // D_CHUNK)
    tri2 = tri & same
    la = jnp.dot(sg_ref[0].astype(BF16), gkw_ref[...], preferred_element_type=F32) + gkb_ref[...]
    la = jax.nn.log_sigmoid(la) / D_GATE_NORM
    gcs = _dot_mask_lhs(tri2.astype(BF16), la)
    first = lax.broadcasted_iota(jnp.int32, (q_len, 1), 0) < D_CHUNK
    r_all = r_ref[0]
    for h in range(D_HEADS):
        ks = slice(h * D_HK, (h + 1) * D_HK)
        vs = slice(h * D_HV, (h + 1) * D_HV)
        g_h = gcs[:, ks]
        g_end0 = g_h[D_CHUNK - 1:D_CHUNK, :]
        g_end1 = g_h[q_len - 1:q_len, :]
        q_h = q_ref[0, :, ks].astype(F32) * (D_HK ** -0.5)
        k_h = k_ref[0, :, ks].astype(F32)
        v_h = v_ref[0, :, vs]
        q_dec = (q_h * jnp.exp(g_h)).astype(BF16)
        k_inv = (k_h * jnp.exp(-g_h)).astype(BF16)
        k_end = k_h * jnp.exp(jnp.where(first, g_end0, g_end1) - g_h)
        ke0 = jnp.where(first, k_end, 0.0).astype(BF16)
        ke1 = jnp.where(first, 0.0, k_end).astype(BF16)
        attn = lax.dot_general(q_dec, k_inv, (((1,), (1,)), ((), ())), preferred_element_type=F32)
        attn = jnp.where(tri2, attn, 0.0).astype(BF16)
        o = jnp.dot(attn, v_h, preferred_element_type=F32)
        v_t = v_h.astype(F32).T.astype(BF16)
        st_rows = slice(h * D_HV, (h + 1) * D_HV)
        s0 = gs_ref[st_rows, :]
        s1 = s0 * jnp.exp(g_end0) + jnp.dot(v_t, ke0, preferred_element_type=F32)
        s2 = s1 * jnp.exp(g_end1) + jnp.dot(v_t, ke1, preferred_element_type=F32)
        gs_ref[st_rows, :] = s2
        tdims = (((1,), (1,)), ((), ()))
        o0 = lax.dot_general(q_dec, s0.astype(BF16), tdims, preferred_element_type=F32)
        o1 = lax.dot_general(q_dec, s1.astype(BF16), tdims, preferred_element_type=F32)
        o = o + jnp.where(first, o0, o1)
        o = _rms(o, gn_ref[...]) * jax.nn.silu(r_all[:, vs].astype(F32))
        o_ref[0, :, C_INNER + h * D_HV:C_INNER + (h + 1) * D_HV] = o.astype(o_ref.dtype)


def _ssd_gla(z, q, k, v, r, xbc, side_dt, side_g, g_lane, p):
    b, s, _ = z.shape
    q_len = C_CHUNK
    halo = 16
    chunk = lambda width: pl.BlockSpec((1, q_len, width), lambda i, n: (i, n, 0))
    full = lambda shape: pl.BlockSpec(shape, lambda i, n: (0,) * len(shape))
    pad_lanes = lambda vec: jnp.zeros((1, LANES), F32).at[0, :vec.shape[0]].set(vec.astype(F32))
    gkw = jnp.zeros((LANES, D_KEY), F32).at[g_lane:g_lane + D_GATE_RANK].set(p["gk_w"]).astype(BF16)
    expand = jnp.asarray(np.arange(C_INNER)[None, :] // C_HEAD_DIM == np.arange(LANES)[:, None], BF16)
    return pl.pallas_call(
        _ssd_gla_kernel,
        grid=(b, s // q_len),
        in_specs=[
            chunk(C_INNER), chunk(D_KEY), chunk(D_KEY), chunk(D_VAL), chunk(D_VAL), chunk(C_CONV_DIM),
            pl.BlockSpec((1, halo, C_CONV_DIM), lambda i, n: (i, jnp.maximum(n * (q_len // halo) - 1, 0), 0)),
            chunk(LANES), chunk(LANES),
            full((C_CONV, C_CONV_DIM)), full((1, C_CONV_DIM)),
            full((1, LANES)), full((1, LANES)), full((1, C_INNER)), full((LANES, C_INNER)),
            full((1, C_INNER)), full((LANES, D_KEY)), full((1, D_KEY)), full((1, D_HV)),
        ],
        out_specs=pl.BlockSpec((1, q_len, C_INNER + D_VAL), lambda i, n: (i, n, 0)),
        out_shape=jax.ShapeDtypeStruct((b, s, C_INNER + D_VAL), BF16),
        scratch_shapes=[pltpu.VMEM((C_INNER, C_STATE), F32),
                        pltpu.VMEM((D_VAL, D_HK), F32)],
        compiler_params=_cparams(("parallel", "arbitrary")),
        name="ssd_gla",
    )(z, q, k, v, r, xbc, xbc, side_dt, side_g,
      p["conv_w"].astype(F32), p["conv_b"].reshape(1, -1).astype(F32),
      pad_lanes(p["dt_bias"]), pad_lanes(p["a_log"]),
      jnp.repeat(p["d_skip"].astype(F32), C_HEAD_DIM).reshape(1, C_INNER), expand,
      p["ssd_norm"].reshape(1, -1).astype(F32), gkw, p["gk_b"].reshape(1, -1).astype(F32),
      p["gla_norm"].reshape(1, -1).astype(F32))


def _store_rows_tiled(ref, val):
    m, d = val.shape
    nc = d // LANES
    for c in range(nc):
        ref[pl.ds(c, m, stride=nc), :] = val[:, c * LANES:(c + 1) * LANES]


def _load_rows_tiled(ref, m, dtype=None):
    nc = ref.shape[0] // m
    parts = [ref[pl.ds(c, m, stride=nc), :] for c in range(nc)]
    if dtype is not None:
        parts = [p.astype(dtype) for p in parts]
    return jnp.concatenate(parts, axis=1)


def _pack_pairs(x):
    n = x.shape[1] // 2
    u = pltpu.bitcast(x.astype(BF16).astype(F32), jnp.uint32)
    return (u[:, :n] >> 16) | (u[:, n:] & jnp.uint32(0xFFFF0000))


def _unpack_pairs(u):
    lo = pltpu.bitcast(u << 16, F32).astype(BF16)
    hi = pltpu.bitcast(u & jnp.uint32(0xFFFF0000), F32).astype(BF16)
    return jnp.concatenate([lo, hi], axis=1)


RT_GATE, RT_EXPERT, RT_RANK = 0, 2, 4


def _route_block(lg, carry):
    m = lg.shape[0]
    lane = lax.broadcasted_iota(jnp.int32, (m, LANES), 1)
    lane_f = lane.astype(F32)
    none = float(LANES)
    neg = -jnp.inf
    first_max = lambda v, vmax: jnp.min(jnp.where(v == vmax, lane_f, none), axis=-1, keepdims=True)
    gl = jnp.where(lane < MOE_GROUPS, lg, neg)
    gmax = jnp.max(gl, axis=-1, keepdims=True)
    g_w = 1.0 / jnp.sum(jnp.exp(gl - gmax), axis=-1, keepdims=True)
    lo = MOE_GROUPS + first_max(gl, gmax) * MOE_EPG
    el = jnp.where((lane_f >= lo) & (lane_f < lo + MOE_EPG), lg, neg)
    emax = jnp.max(el, axis=-1, keepdims=True)
    esum = jnp.sum(jnp.exp(el - emax), axis=-1, keepdims=True)
    l0 = first_max(el, emax)
    el2 = jnp.where(lane_f == l0, neg, el)
    emax2 = jnp.max(el2, axis=-1, keepdims=True)
    l1 = first_max(el2, emax2)
    p0 = 1.0 / esum
    p1 = jnp.exp(emax2 - emax) / esum
    w0 = g_w * (p0 / (p0 + p1))
    w1 = g_w * (p1 / (p0 + p1))
    oh0 = lane_f == l0
    oh1 = lane_f == l1
    oh = (oh0 | oh1).astype(BF16)
    r = lax.broadcasted_iota(jnp.int32, (m, m), 0)
    c = lax.broadcasted_iota(jnp.int32, (m, m), 1)
    cum = jnp.dot((c < r).astype(BF16), oh, preferred_element_type=F32) + carry
    rank0 = jnp.sum(jnp.where(oh0, cum, 0.0), axis=-1, keepdims=True)
    rank1 = jnp.sum(jnp.where(oh1, cum, 0.0), axis=-1, keepdims=True)
    carry = carry + jnp.sum(oh.astype(F32), axis=0, keepdims=True)
    rec = jnp.zeros((m, LANES), F32)
    for pos, val in ((RT_GATE, w0), (RT_GATE + 1, w1), (RT_EXPERT, l0 - MOE_GROUPS),
                     (RT_EXPERT + 1, l1 - MOE_GROUPS), (RT_RANK, rank0), (RT_RANK + 1, rank1)):
        rec = jnp.where(lane == pos, val, rec)
    return rec, carry


def _out_proj_kernel(n_parts, *refs):
    a_refs = refs[:n_parts]
    w_refs = refs[n_parts:2 * n_parts]
    h_ref, g_ref, wr_ref, br_ref, ho_ref, xt_ref, rt_ref, cnt_ref, carry_ref = refs[2 * n_parts:]

    @pl.when(pl.program_id(0) == 0)
    def _():
        carry_ref[...] = jnp.zeros_like(carry_ref)

    acc = h_ref[...]
    for a_ref, w_ref in zip(a_refs, w_refs):
        acc = acc + jnp.dot(a_ref[...], w_ref[...], preferred_element_type=F32)
    ho_ref[...] = acc
    xn = _rms(acc, g_ref[...])
    xt_ref[...] = xn.astype(xt_ref.dtype)
    x_hi, x_mid, _ = _split3(xn)
    wr = wr_ref[...]
    lg2 = jnp.dot(x_hi, wr, preferred_element_type=F32)
    lg = (lg2[:, :LANES] + lg2[:, LANES:] + jnp.dot(x_mid, wr[:, :LANES], preferred_element_type=F32)
          + br_ref[...])
    rec, carry = _route_block(lg, carry_ref[...])
    rt_ref[...] = rec
    carry_ref[...] = carry
    cnt_ref[0] = carry


def _out_proj(parts, w_parts, h, g, w_route, b_route, *, tm=512):
    t, d = h.shape
    tm = min(tm, t)
    nc = d // LANES
    row = lambda i: (i, 0)
    const = lambda i: (0, 0)
    in_specs = [pl.BlockSpec((tm, a.shape[1]), row) for a in parts]
    in_specs += [pl.BlockSpec(w.shape, const) for w in w_parts]
    in_specs += [pl.BlockSpec((tm, d), row), pl.BlockSpec((1, d), const),
                 pl.BlockSpec((d, 2 * LANES), const), pl.BlockSpec((1, LANES), const)]
    return pl.pallas_call(
        functools.partial(_out_proj_kernel, len(parts)),
        grid=(t // tm,),
        in_specs=in_specs,
        out_specs=[pl.BlockSpec((tm, d), row), pl.BlockSpec((tm, d), row),
                   pl.BlockSpec((tm, LANES), row), pl.BlockSpec((1, 1, LANES), lambda i: (i, 0, 0))],
        out_shape=[jax.ShapeDtypeStruct((t, d), F32), jax.ShapeDtypeStruct((t, d), BF16),
                   jax.ShapeDtypeStruct((t, LANES), F32), jax.ShapeDtypeStruct((t // tm, 1, LANES), F32)],
        scratch_shapes=[pltpu.VMEM((1, LANES), F32)],
        compiler_params=_cparams(("arbitrary",)),
        name="out_proj",
    )(*parts, *w_parts, h, g.reshape(1, d), w_route, b_route)


RARE_RUN = 128


def _for_each_piece(run, max_rows, body):
    sizes = [1 << b for b in reversed(range(max_rows.bit_length()))]

    def sweep(group, off):
        for n in group:
            hit = (run & n) != 0
            pl.when(hit)(functools.partial(body, off, n))
            off = off + jnp.where(hit, n, 0)

    rare = [n for n in sizes if n >= RARE_RUN]
    if rare:
        pl.when(run >= RARE_RUN)(functools.partial(sweep, rare, jnp.int32(0)))
    sweep([n for n in sizes if n < RARE_RUN], run // RARE_RUN * RARE_RUN)


def _zero_fill_rows(rows_ref, z_ref, zsem, lo_ref, hi_ref, n_rows, nc):
    zb = z_ref.shape[0] // nc
    z_ref[...] = jnp.zeros_like(z_ref)
    assert MOE_TILE // 2 <= zb and MOE_TILE % zb == 0

    def piece(row0, n):
        return pltpu.make_async_copy(z_ref.at[pl.ds(0, n * nc)], rows_ref.at[pl.ds(row0 * nc, n * nc)], zsem)

    def sweep(issue):
        def per_expert(e, carry):
            lo = lo_ref[e]

            def one(off, n):
                piece(lo + off, n).start() if issue else piece(lo + off, n).wait()

            _for_each_piece(hi_ref[e] - lo, MOE_TILE // 2, one)
            return carry

        def per_block(i, carry):
            piece(i * zb, zb).start() if issue else piece(i * zb, zb).wait()
            return carry

        lax.fori_loop(0, MOE_EXPERTS, per_expert, 0)
        lax.fori_loop(hi_ref[MOE_EXPERTS - 1] // zb, n_rows // zb, per_block, 0)

    sweep(True)
    sweep(False)


def _tile_positions(rt, pos_base):
    lane_f = lax.broadcasted_iota(jnp.int32, rt.shape, 1).astype(F32)
    out = []
    for k in range(MOE_TOPK):
        e_lane = rt[:, RT_EXPERT + k:RT_EXPERT + k + 1] + MOE_GROUPS
        out.append(rt[:, RT_RANK + k:RT_RANK + k + 1]
                   + jnp.sum(jnp.where(lane_f == e_lane, pos_base, 0.0), axis=-1, keepdims=True))
    return out


def _moe_dispatch_kernel(dst_ref, len_ref, lo_ref, hi_ref, x_ref, rt_ref, pb_ref, rows_ref,
                         sbuf, z_ref, sem, zsem, *, tm, nc, n_steps, n_rows):
    i = pl.program_id(0)
    rows = MOE_TOPK * tm
    slot = lax.rem(i, 2)

    def wait_slot(s):
        pltpu.make_async_copy(sbuf.at[s], rows_ref.at[pl.ds(0, rows * nc)], sem.at[s]).wait()

    @pl.when(i == 0)
    def _():
        _zero_fill_rows(rows_ref, z_ref, zsem, lo_ref, hi_ref, n_rows, nc)

    @pl.when(i >= 2)
    def _():
        wait_slot(slot)

    pos = _tile_positions(rt_ref[...], pb_ref[0])
    lane = lax.broadcasted_iota(jnp.int32, (tm, LANES), 1)
    pos_t = jnp.where(lane == 0, pos[0], jnp.where(lane == 1, pos[1], 0.0)).T
    p_iota = lax.broadcasted_iota(jnp.int32, (rows, tm), 0).astype(F32)
    place = ((p_iota == pos_t[0:1, :]) | (p_iota == pos_t[1:2, :])).astype(BF16)
    _store_rows_tiled(sbuf.at[slot], _pack_pairs(jnp.dot(place, x_ref[...], preferred_element_type=F32)))

    def per_expert(e, src):
        seg = i * MOE_EXPERTS + e
        dst = dst_ref[seg]

        def send(off, n):
            pltpu.make_async_copy(sbuf.at[slot, pl.ds((src + off) * nc, n * nc)],
                                  rows_ref.at[pl.ds((dst + off) * nc, n * nc)], sem.at[slot]).start()

        _for_each_piece(len_ref[seg], rows, send)
        return src + len_ref[seg]

    lax.fori_loop(0, MOE_EXPERTS, per_expert, jnp.int32(0))

    @pl.when(i == n_steps - 1)
    def _():
        wait_slot(slot)
        if n_steps > 1:
            wait_slot(1 - slot)


def _moe_dispatch(x, route, pos_base, seg_dst, seg_len, pad_lo, pad_hi, n_rows, *, tm):
    t, d = x.shape
    nc = d // (2 * LANES)
    n_steps = t // tm
    row = lambda i, *_: (i, 0)
    return pl.pallas_call(
        functools.partial(_moe_dispatch_kernel, tm=tm, nc=nc, n_steps=n_steps, n_rows=n_rows),
        grid_spec=pltpu.PrefetchScalarGridSpec(
            num_scalar_prefetch=4, grid=(n_steps,),
            in_specs=[pl.BlockSpec((tm, d), row), pl.BlockSpec((tm, LANES), row),
                      pl.BlockSpec((1, 1, LANES), lambda i, *_: (i, 0, 0))],
            out_specs=pl.BlockSpec(memory_space=pl.ANY),
            scratch_shapes=[pltpu.VMEM((2, MOE_TOPK * tm * nc, LANES), jnp.uint32),
                            pltpu.VMEM((MOE_TILE // 2 * nc, LANES), jnp.uint32),
                            pltpu.SemaphoreType.DMA((2,)), pltpu.SemaphoreType.DMA(())]),
        out_shape=jax.ShapeDtypeStruct((n_rows * nc, LANES), jnp.uint32),
        compiler_params=pltpu.CompilerParams(dimension_semantics=("arbitrary",), has_side_effects=True,
                                             vmem_limit_bytes=VMEM_LIMIT),
        name="moe_dispatch",
    )(seg_dst, seg_len, pad_lo, pad_hi, x, route, pos_base)


def _moe_kernel(te_ref, nu_ref, x_ref, wg_ref, wu_ref, wd_ref, y_ref):
    i = pl.program_id(0)

    @pl.when(i >= nu_ref[0])
    def _():
        y_ref[...] = jnp.zeros_like(y_ref)

    @pl.when(i < nu_ref[0])
    def _():
        x = _unpack_pairs(_load_rows_tiled(x_ref, MOE_TILE))
        gate = jnp.dot(x, wg_ref[...].astype(BF16), preferred_element_type=F32)
        up = jnp.dot(x, wu_ref[...].astype(BF16), preferred_element_type=F32)
        act = (jax.nn.silu(gate) * up).astype(BF16)
        y = jnp.dot(act, wd_ref[...].astype(BF16), preferred_element_type=F32)
        _store_rows_tiled(y_ref, _pack_pairs(y))


def _moe_experts(x_rows, tile_expert, n_used, w_gate, w_up, w_down, layer):
    d, ff = w_gate.shape[-2:]
    nc = d // (2 * LANES)
    n_tiles = x_rows.shape[0] // (MOE_TILE * nc)
    live = lambda i, nu: jnp.minimum(i, nu[0] - 1)
    w_spec = lambda a, b: pl.BlockSpec((None, None, a, b), lambda i, te, nu: (layer, te[live(i, nu)], 0, 0))
    return pl.pallas_call(
        _moe_kernel,
        grid_spec=pltpu.PrefetchScalarGridSpec(
            num_scalar_prefetch=2, grid=(n_tiles,),
            in_specs=[pl.BlockSpec((MOE_TILE * nc, LANES), lambda i, te, nu: (live(i, nu), 0)),
                      w_spec(d, ff), w_spec(d, ff), w_spec(ff, d)],
            out_specs=pl.BlockSpec((MOE_TILE * nc, LANES), lambda i, te, nu: (i, 0))),
        out_shape=jax.ShapeDtypeStruct(x_rows.shape, x_rows.dtype),
        compiler_params=_cparams(("arbitrary",)),
        name="moe_experts",
    )(tile_expert, n_used, x_rows, w_gate, w_up, w_down)


def _moe_combine(h, m0_ref, m1_ref, rt_ref):
    rt = rt_ref[...]
    tm = h.shape[0]
    return (h + _load_rows_tiled(m0_ref, tm) * rt[:, RT_GATE:RT_GATE + 1]
            + _load_rows_tiled(m1_ref, tm) * rt[:, RT_GATE + 1:RT_GATE + 2])


def _moe_combine_kernel(src_ref, len_ref, h_ref, rt_ref, pb_ref, g_ref, y_hbm, o_ref, ybuf, sem, *, tm, nc, final):
    i = pl.program_id(0)
    n_steps = pl.num_programs(0)
    rows = MOE_TOPK * tm

    def fetch(tile, slot):
        def per_expert(e, dst):
            seg = tile * MOE_EXPERTS + e
            src = src_ref[seg]

            def recv(off, n):
                pltpu.make_async_copy(y_hbm.at[pl.ds((src + off) * nc, n * nc)],
                                      ybuf.at[slot, pl.ds((dst + off) * nc, n * nc)], sem.at[slot]).start()

            _for_each_piece(len_ref[seg], rows, recv)
            return dst + len_ref[seg]

        lax.fori_loop(0, MOE_EXPERTS, per_expert, jnp.int32(0))

    slot = lax.rem(i, 2)

    @pl.when(i == 0)
    def _():
        fetch(0, 0)

    @pl.when(i + 1 < n_steps)
    def _():
        fetch(i + 1, 1 - slot)

    pltpu.make_async_copy(y_hbm.at[pl.ds(0, rows * nc)], ybuf.at[slot], sem.at[slot]).wait()
    y = _unpack_pairs(_load_rows_tiled(ybuf.at[slot], rows))
    rt = rt_ref[...]
    pos_f = lax.broadcasted_iota(jnp.int32, (tm, rows), 1).astype(F32)
    pick = jnp.zeros((tm, rows), F32)
    for k, pos in enumerate(_tile_positions(rt, pb_ref[0])):
        pick = jnp.where(pos_f == pos, rt[:, RT_GATE + k:RT_GATE + k + 1], pick)
    out = h_ref[...] + jnp.dot(pick.astype(BF16), y, preferred_element_type=F32)
    o_ref[...] = _rms(out, g_ref[...]) if final else out


def _moe_combine_rows(h, y_rows, route, seg_src, seg_len, pos_base, g, *, tm, final):
    t, d = h.shape
    nc = d // (2 * LANES)
    row = lambda i, *_: (i, 0)
    return pl.pallas_call(
        functools.partial(_moe_combine_kernel, tm=tm, nc=nc, final=final),
        grid_spec=pltpu.PrefetchScalarGridSpec(
            num_scalar_prefetch=2, grid=(t // tm,),
            in_specs=[pl.BlockSpec((tm, d), row), pl.BlockSpec((tm, LANES), row),
                      pl.BlockSpec((1, 1, LANES), lambda i, *_: (i, 0, 0)),
                      pl.BlockSpec((1, d), lambda i, *_: (0, 0)),
                      pl.BlockSpec(memory_space=pl.ANY)],
            out_specs=pl.BlockSpec((tm, d), row),
            scratch_shapes=[pltpu.VMEM((2, MOE_TOPK * tm * nc, LANES), y_rows.dtype),
                            pltpu.SemaphoreType.DMA((2,))]),
        out_shape=jax.ShapeDtypeStruct((t, d), F32),
        compiler_params=_cparams(("arbitrary",)),
        name="moe_combine",
    )(seg_src, seg_len, h, route, pos_base, g.reshape(1, d), y_rows)


def _moe(h, x, route, tile_counts, w_gate, w_up, w_down, layer, g, *, final):
    t = route.shape[0]
    n_tt = tile_counts.shape[0]
    tm = t // n_tt
    after = tile_counts[:, 0, MOE_GROUPS:MOE_GROUPS + MOE_EXPERTS].astype(jnp.int32)
    before = jnp.concatenate([jnp.zeros((1, MOE_EXPERTS), jnp.int32), after[:-1]], axis=0)
    cnt = after[-1]
    padded = (cnt + MOE_TILE - 1) // MOE_TILE * MOE_TILE
    pad_ends = jnp.cumsum(padded)
    starts = (pad_ends - padded).astype(jnp.int32)
    n_tiles = (t * MOE_TOPK + MOE_EXPERTS * (MOE_TILE - 1)) // MOE_TILE
    tile_start = jnp.arange(n_tiles, dtype=jnp.int32) * MOE_TILE
    tile_expert = jnp.minimum(jnp.sum(tile_start[:, None] >= pad_ends[None, :], axis=1),
                              MOE_EXPERTS - 1).astype(jnp.int32)
    n_used = (pad_ends[-1] // MOE_TILE).astype(jnp.int32).reshape(1)
    n_rows = n_tiles * MOE_TILE
    seg_len = (after - before).reshape(-1)
    seg_off = jnp.cumsum(after - before, axis=1) - (after - before)
    seg_row = (starts[None, :] + before).reshape(-1)
    pos_base = jnp.zeros((n_tt, 1, LANES), F32).at[:, 0, MOE_GROUPS:MOE_GROUPS + MOE_EXPERTS].set(
        (seg_off - before).astype(F32))
    x_rows = _moe_dispatch(x, route, pos_base, seg_row, seg_len, starts + cnt, pad_ends.astype(jnp.int32),
                           n_rows, tm=tm)
    y_rows = _moe_experts(x_rows, tile_expert, n_used, w_gate, w_up, w_down, layer)
    return _moe_combine_rows(h, y_rows, route, seg_row, seg_len, pos_base, g, tm=tm, final=final)


def _router_weights(w_group, b_group, w_router, b_router):
    d = w_group.shape[0]
    w = jnp.zeros((d, LANES), F32)
    w = w.at[:, :MOE_GROUPS].set(w_group).at[:, MOE_GROUPS:MOE_GROUPS + MOE_EXPERTS].set(w_router)
    b = jnp.zeros((1, LANES), F32)
    b = b.at[0, :MOE_GROUPS].set(b_group).at[0, MOE_GROUPS:MOE_GROUPS + MOE_EXPERTS].set(b_router)
    w_hi = w.astype(BF16)
    w_mid = (w - w_hi.astype(F32)).astype(BF16)
    return jnp.concatenate([w_hi, w_mid], axis=1), b


def kernel(x, norm_mix, norm_moe, norm_final, even_w_in, even_sinks, even_forget_bias, even_w_out,
           odd_w_in, odd_conv_w, odd_conv_b, odd_dt_bias, odd_a_log, odd_d_skip, odd_ssd_norm,
           odd_gk_w, odd_gk_b, odd_gla_norm, odd_w_out, moe_w_group, moe_b_group, moe_w_router,
           moe_b_router, moe_w_gate, moe_w_up, moe_w_down):
    b, s, d = x.shape
    t = b * s
    depth = norm_mix.shape[0]
    h = x.reshape(t, d)
    moe = None
    for layer in range(depth):
        i = layer // 2
        if layer % 2 == 0:
            w = even_w_in[i]
            n_ab = (A_Q_HEADS + 2 * A_KV_HEADS + 3 * B_HEADS) * HEAD_DIM
            w_main = w[:, :n_ab].astype(BF16)
            w_aux = jnp.zeros((d, LANES), F32).at[:, :B_HEADS].set(w[:, n_ab:]).astype(BF16)
            (proj, f_aux), h_new = _norm_proj(h, norm_mix[layer], [w_main, w_aux],
                                              ((0, 0, n_ab, BF16), (1, 0, LANES, F32)), tm=512, moe=moe)
            h = h if h_new is None else h_new
            proj = proj.reshape(b, s, -1)
            out_a = _swa(proj, even_sinks[i])
            c, ct = _fox_gate(f_aux.reshape(b, s, LANES), even_forget_bias[i])
            out_b = _fox(proj, c, ct)
            n_ha = A_Q_HEADS * HEAD_DIM
            w_out = even_w_out[i].astype(BF16)
            parts = [out_a.reshape(t, -1), out_b.reshape(t, -1)]
            w_parts = [w_out[:n_ha], w_out[n_ha:]]
        else:
            w = odd_w_in[i]
            o_z, o_xbc = 0, C_INNER
            o_dt = o_xbc + C_CONV_DIM
            o_q = o_dt + C_HEADS
            o_k = o_q + D_KEY
            o_v = o_k + D_KEY
            o_g = o_v + D_VAL
            o_r = o_g + D_GATE_RANK
            n_b = w.shape[1] - o_dt
            w_a = w[:, :o_dt].astype(BF16)
            w_b = jnp.pad(w[:, o_dt:].astype(BF16), ((0, 0), (0, -n_b % LANES)))
            g_win = (o_g - o_dt) // LANES * LANES
            plan = ((0, o_z, C_INNER, BF16), (1, o_q - o_dt, D_KEY, BF16), (1, o_k - o_dt, D_KEY, BF16),
                    (1, o_v - o_dt, D_VAL, BF16), (1, o_r - o_dt, D_VAL, BF16), (0, o_xbc, C_CONV_DIM, BF16),
                    (1, 0, LANES, F32), (1, g_win, LANES, F32))
            outs, h_new = _norm_proj(h, norm_mix[layer], [w_a, w_b], plan, tm=256, moe=moe)
            h = h if h_new is None else h_new
            params = dict(conv_w=odd_conv_w[i], conv_b=odd_conv_b[i], dt_bias=odd_dt_bias[i], a_log=odd_a_log[i],
                          d_skip=odd_d_skip[i], ssd_norm=odd_ssd_norm[i], gk_w=odd_gk_w[i], gk_b=odd_gk_b[i],
                          gla_norm=odd_gla_norm[i])
            mixed = _ssd_gla(*[o.reshape(b, s, -1) for o in outs], o_g - o_dt - g_win, params)
            parts = [mixed.reshape(t, -1)]
            w_parts = [odd_w_out[i].astype(BF16)]
        w_route, b_route = _router_weights(moe_w_group[layer], moe_b_group[layer],
                                           moe_w_router[layer], moe_b_router[layer])
        h, x_tiled, route, tile_counts = _out_proj(parts, w_parts, h, norm_moe[layer], w_route, b_route)
        h = _moe(h, x_tiled, route, tile_counts, moe_w_gate, moe_w_up, moe_w_down, layer, norm_final,
                 final=layer == depth - 1)
    out = h
    return out.reshape(b, s, d)
```

```python
import functools

import numpy as np
import jax
import jax.numpy as jnp
from jax import lax
from jax.experimental import pallas as pl
from jax.experimental.pallas import tpu as pltpu

F32 = jnp.float32
BF16 = jnp.bfloat16

RMS_EPS = 1e-6
HEAD_DIM = 64
A_Q_HEADS = 8
A_KV_HEADS = 2
A_GROUP = A_Q_HEADS // A_KV_HEADS
A_WINDOW = 128
B_HEADS = 8
C_HEADS = 16
C_HEAD_DIM = 64
C_INNER = C_HEADS * C_HEAD_DIM
C_GROUPS = 2
C_HPG = C_HEADS // C_GROUPS
C_STATE = 128
C_CONV = 4
C_CHUNK = 128
C_CONV_DIM = C_INNER + 2 * C_GROUPS * C_STATE
D_HEADS = 4
D_HK = 128
D_HV = 256
D_KEY = D_HEADS * D_HK
D_VAL = D_HEADS * D_HV
D_GATE_RANK = 16
D_GATE_NORM = 16.0
D_CHUNK = 64
MOE_GROUPS = 4
MOE_EPG = 8
MOE_EXPERTS = MOE_GROUPS * MOE_EPG
MOE_TOPK = 2

LANES = 128
VMEM_LIMIT = 48 * 1024 * 1024
MOE_TILE = 512


def _cparams(sem):
    return pltpu.CompilerParams(dimension_semantics=sem, vmem_limit_bytes=VMEM_LIMIT)


def _rms(x, g):
    ms = jnp.mean(x * x, axis=-1, keepdims=True)
    return x * lax.rsqrt(ms + RMS_EPS) * g


def _norm_proj_kernel(n_w, plan, *refs):
    x_ref, g_ref = refs[:2]
    w_refs = refs[2:2 + n_w]
    o_refs = refs[2 + n_w:2 + n_w + len(plan)]
    res_refs = refs[2 + n_w + len(plan):]
    xn = _rms(x_ref[...], g_ref[...]).astype(BF16)
    for w_ref, res_ref in zip(w_refs, res_refs):
        res_ref[...] = jnp.dot(xn, w_ref[...], preferred_element_type=F32)
    for o_ref, (wi, start, width, _) in zip(o_refs, plan):
        o_ref[...] = res_refs[wi][:, start:start + width].astype(o_ref.dtype)


def _norm_proj(x, g, weights, plan, *, tm):
    t, d = x.shape
    tm = min(tm, t)
    row = lambda i: (i, 0)
    const = lambda i: (0, 0)
    in_specs = [pl.BlockSpec((tm, d), row), pl.BlockSpec((1, d), const)]
    in_specs += [pl.BlockSpec(w.shape, const, pipeline_mode=pl.Buffered(1)) for w in weights]
    return pl.pallas_call(
        functools.partial(_norm_proj_kernel, len(weights), plan),
        grid=(t // tm,),
        in_specs=in_specs,
        out_specs=[pl.BlockSpec((tm, width), row) for _, _, width, _ in plan],
        out_shape=[jax.ShapeDtypeStruct((t, width), dtype) for _, _, width, dtype in plan],
        scratch_shapes=[pltpu.VMEM((tm, w.shape[1]), F32) for w in weights],
        compiler_params=_cparams(("parallel",)),
        name="norm_proj",
    )(x, g.reshape(1, d), *weights)


SWA_QBLOCKS = 4


def _swa_kernel(sink_ref, slope_ref, q_ref, kp_ref, kc_ref, vp_ref, vc_ref, o_ref):
    n = pl.program_id(1)
    blk = A_WINDOW
    wide = A_GROUP * blk
    key = lax.broadcasted_iota(jnp.int32, (2 * blk, wide), 0)
    qry = lax.broadcasted_iota(jnp.int32, (2 * blk, wide), 1) % blk
    dist = blk + qry - key
    in_window = (dist >= 0) & (dist < A_WINDOW)
    distf = dist.astype(F32)
    nt = (((1,), (1,)), ((), ()))
    k_all = jnp.concatenate([kp_ref[0], kc_ref[0]], axis=0)
    v_all = jnp.concatenate([vp_ref[0], vc_ref[0]], axis=1)
    units = [(j, kh) for j in range(SWA_QBLOCKS) for kh in range(A_KV_HEADS)]
    scores = []
    for j, kh in units:
        k = k_all[j * blk:(j + 2) * blk, kh * HEAD_DIM:(kh + 1) * HEAD_DIM]
        q = jnp.concatenate([q_ref[0, j * blk:(j + 1) * blk,
                                   (kh * A_GROUP + g) * HEAD_DIM:(kh * A_GROUP + g + 1) * HEAD_DIM]
                             for g in range(A_GROUP)], axis=0)
        scores.append(lax.dot_general(k, q, nt, preferred_element_type=F32))
    probs = []
    for (j, kh), s in zip(units, scores):
        valid = in_window & ((key >= blk) | (n * SWA_QBLOCKS + j > 0))
        s = s * (HEAD_DIM ** -0.5) - slope_ref[kh:kh + 1, :] * distf
        s = jnp.where(valid, s, -jnp.inf)
        sink = sink_ref[kh:kh + 1, :]
        m = jnp.maximum(jnp.max(s, axis=0, keepdims=True), sink)
        p = jnp.exp(s - m)
        probs.append((p.astype(BF16), jnp.sum(p, axis=0, keepdims=True) + jnp.exp(sink - m)))
    for j in range(SWA_QBLOCKS):
        outs = []
        for kh in range(A_KV_HEADS):
            p, denom = probs[j * A_KV_HEADS + kh]
            v_t = v_all[kh * HEAD_DIM:(kh + 1) * HEAD_DIM, j * blk:(j + 2) * blk]
            o_t = jnp.dot(v_t, p, preferred_element_type=F32) / denom
            outs += [o_t[:, g * blk:(g + 1) * blk] for g in range(A_GROUP)]
        o_ref[0, j * blk:(j + 1) * blk, :] = jnp.concatenate(outs, axis=0).T.astype(o_ref.dtype)


def _swa(proj, sinks):
    b, s, _ = proj.shape
    blk = A_WINDOW
    qw = A_Q_HEADS * HEAD_DIM
    kw = A_KV_HEADS * HEAD_DIM
    k_blk = qw // kw
    v_t = proj[:, :, qw + kw:qw + 2 * kw].transpose(0, 2, 1)
    per_lane = lambda vec: jnp.repeat(vec.astype(F32), blk).reshape(A_KV_HEADS, A_GROUP * blk)
    slopes = jnp.asarray(2.0 ** (-8.0 * np.arange(1, A_Q_HEADS + 1) / A_Q_HEADS), F32)
    tq = SWA_QBLOCKS * blk
    prev = lambda n: jnp.maximum(n * SWA_QBLOCKS - 1, 0)
    full = pl.BlockSpec((A_KV_HEADS, A_GROUP * blk), lambda i, n: (0, 0))
    return pl.pallas_call(
        _swa_kernel,
        grid=(b, s // tq),
        in_specs=[
            full, full,
            pl.BlockSpec((1, tq, qw), lambda i, n: (i, n, 0)),
            pl.BlockSpec((1, blk, kw), lambda i, n: (i, prev(n), k_blk)),
            pl.BlockSpec((1, tq, kw), lambda i, n: (i, n, k_blk)),
            pl.BlockSpec((1, kw, blk), lambda i, n: (i, 0, prev(n))),
            pl.BlockSpec((1, kw, tq), lambda i, n: (i, 0, n)),
        ],
        out_specs=pl.BlockSpec((1, tq, qw), lambda i, n: (i, n, 0)),
        out_shape=jax.ShapeDtypeStruct((b, s, qw), BF16),
        compiler_params=_cparams(("parallel", "parallel")),
        name="swa",
    )(per_lane(sinks), per_lane(slopes), proj, proj, proj, v_t, v_t)


def _tril(n, dtype=F32):
    r = lax.broadcasted_iota(jnp.int32, (n, n), 0)
    c = lax.broadcasted_iota(jnp.int32, (n, n), 1)
    return (c <= r).astype(dtype)


def _split3(x):
    hi = x.astype(BF16)
    r = x - hi.astype(F32)
    mid = r.astype(BF16)
    return hi, mid, (r - mid.astype(F32)).astype(BF16)


def _dot_mask_lhs(mask, x):
    return sum(jnp.dot(mask, part, preferred_element_type=F32) for part in _split3(x))


def _dot_mask_rhs(x, mask):
    return sum(jnp.dot(part, mask, preferred_element_type=F32) for part in _split3(x))


def _fox_gate_kernel(f_ref, b_ref, c_ref, ct_ref):
    tri = _tril(LANES, BF16)
    carry = jnp.zeros((1, LANES), F32)
    for n in range(f_ref.shape[1] // LANES):
        rows = slice(n * LANES, (n + 1) * LANES)
        lf = jax.nn.log_sigmoid(f_ref[0, rows, :] + b_ref[...])
        cs = _dot_mask_lhs(tri, lf) + carry
        carry = cs[LANES - 1:LANES, :]
        c_ref[0, rows, :] = cs
        ct_ref[0, n] = cs.T[:B_HEADS, :]


def _fox_gate(f_aux, bias):
    b, s, _ = f_aux.shape
    nb = s // LANES
    bias_p = jnp.zeros((1, LANES), F32).at[0, :B_HEADS].set(bias.astype(F32))
    return pl.pallas_call(
        _fox_gate_kernel,
        grid=(b,),
        in_specs=[pl.BlockSpec((1, s, LANES), lambda i: (i, 0, 0)),
                  pl.BlockSpec((1, LANES), lambda i: (0, 0))],
        out_specs=[pl.BlockSpec((1, s, LANES), lambda i: (i, 0, 0)),
                   pl.BlockSpec((1, nb, B_HEADS, LANES), lambda i: (i, 0, 0, 0))],
        out_shape=[jax.ShapeDtypeStruct((b, s, LANES), F32),
                   jax.ShapeDtypeStruct((b, nb, B_HEADS, LANES), F32)],
        compiler_params=_cparams(("parallel",)),
        name="fox_gate",
    )(f_aux, bias_p)


def _fox_kernel(q0_ref, q1_ref, k0_ref, k1_ref, vt_ref, c_ref, ctq_ref, o_ref, *, tq, heads_per_step):
    qi = pl.program_id(1)
    sub = tq // LANES
    key = lax.broadcasted_iota(jnp.int32, (tq, tq), 0)
    qry = lax.broadcasted_iota(jnp.int32, (tq, tq), 1)
    causal = key <= qry
    nt = (((1,), (1,)), ((), ()))
    half = B_HEADS // 2
    q_refs, k_refs = (q0_ref, q1_ref), (k0_ref, k1_ref)
    outs = []
    for h0 in range(0, B_HEADS, heads_per_step):
        heads = list(range(h0, h0 + heads_per_step))
        hsl = [slice(h * HEAD_DIM, (h + 1) * HEAD_DIM) for h in heads]
        lsl = [slice((h % half) * HEAD_DIM, (h % half + 1) * HEAD_DIM) for h in heads]
        qs = [q_refs[h // half][0, :, ls] * (HEAD_DIM ** -0.5)
              for h, ls in zip(heads, lsl)]
        cqs = [jnp.concatenate([ctq_ref[0, u, h:h + 1, :] for u in range(sub)], axis=1) for h in heads]

        def step(j, carry, masked, heads=heads, hsl=hsl, lsl=lsl, qs=qs, cqs=cqs):
            start = pl.multiple_of(j * tq, tq)
            sts = [lax.dot_general(k_refs[h // half][0, pl.ds(start, tq), ls], q, nt,
                                   preferred_element_type=F32)
                   for h, ls, q in zip(heads, lsl, qs)]
            ps, stats = [], []
            for idx, h in enumerate(heads):
                m, l, _ = carry[3 * idx:3 * idx + 3]
                ck = c_ref[0, pl.ds(start, tq), h:h + 1]
                st = (sts[idx] - ck) + cqs[idx]
                if masked:
                    st = jnp.where(causal, st, -jnp.inf)
                m_new = jnp.maximum(m, jnp.max(st, axis=0, keepdims=True))
                alpha = jnp.exp(m - m_new)
                p = jnp.exp(st - m_new)
                stats.append((m_new, alpha, alpha * l + jnp.sum(p, axis=0, keepdims=True)))
                ps.append(p.astype(BF16))
            new = []
            for idx in range(len(heads)):
                m_new, alpha, l = stats[idx]
                pv = jnp.dot(vt_ref[0, j, hsl[idx], :], ps[idx], preferred_element_type=F32)
                new += [m_new, l, alpha * carry[3 * idx + 2] + pv]
            return tuple(new)

        init = (jnp.full((1, tq), -jnp.inf, F32), jnp.zeros((1, tq), F32),
                jnp.zeros((HEAD_DIM, tq), F32)) * heads_per_step
        carry = lax.fori_loop(0, qi, functools.partial(step, masked=False), init)
        carry = step(qi, carry, True)
        for idx in range(heads_per_step):
            outs.append(carry[3 * idx + 2] / carry[3 * idx + 1])
    o_ref[0] = jnp.concatenate(outs, axis=0).T.astype(o_ref.dtype)


def _fox(proj, c, ct, *, tq=256, heads_per_step=8):
    b, s, _ = proj.shape
    w = B_HEADS * HEAD_DIM
    nk = s // tq
    sub = tq // LANES
    hw = w // 2
    base = (A_Q_HEADS + 2 * A_KV_HEADS) * HEAD_DIM
    qb, kb = base // hw, (base + w) // hw
    v_t = proj[:, :, base + 2 * w:base + 3 * w].reshape(b, nk, tq, w).transpose(0, 1, 3, 2)
    return pl.pallas_call(
        functools.partial(_fox_kernel, tq=tq, heads_per_step=heads_per_step),
        grid=(b, s // tq),
        in_specs=[
            pl.BlockSpec((1, tq, hw), lambda i, n: (i, n, qb)),
            pl.BlockSpec((1, tq, hw), lambda i, n: (i, n, qb + 1)),
            pl.BlockSpec((1, s, hw), lambda i, n: (i, 0, kb)),
            pl.BlockSpec((1, s, hw), lambda i, n: (i, 0, kb + 1)),
            pl.BlockSpec((1, nk, w, tq), lambda i, n: (i, 0, 0, 0)),
            pl.BlockSpec((1, s, LANES), lambda i, n: (i, 0, 0)),
            pl.BlockSpec((1, sub, B_HEADS, LANES), lambda i, n: (i, n, 0, 0)),
        ],
        out_specs=pl.BlockSpec((1, tq, w), lambda i, n: (i, n, 0)),
        out_shape=jax.ShapeDtypeStruct((b, s, w), BF16),
        compiler_params=_cparams(("parallel", "parallel")),
        name="fox",
    )(proj, proj, proj, proj, v_t, c, ct)


def _ssd_gla_kernel(z_ref, q_ref, k_ref, v_ref, r_ref, xc_ref, xp_ref, sdt_ref, sg_ref,
                    cw_ref, cb_ref, dtb_ref, alog_ref, dsk_ref, ex_ref, sn_ref, gkw_ref, gkb_ref, gn_ref,
                    o_ref, hs_ref, gs_ref):
    c = pl.program_id(1)
    q_len = C_CHUNK
    halo = xp_ref.shape[1]

    @pl.when(c == 0)
    def _():
        hs_ref[...] = jnp.zeros_like(hs_ref)
        gs_ref[...] = jnp.zeros_like(gs_ref)

    prev = xp_ref[0]
    cur = xc_ref[0]
    ext = jnp.concatenate([jnp.where(c > 0, prev, jnp.zeros_like(prev)), cur], axis=0)
    t_out = lax.broadcasted_iota(jnp.int32, (q_len, halo + q_len), 0)
    t_in = lax.broadcasted_iota(jnp.int32, (q_len, halo + q_len), 1) - halo
    acc = cb_ref[...] + cw_ref[C_CONV - 1:C_CONV, :] * cur.astype(F32)
    for j in range(C_CONV - 1):
        shift = (t_in == t_out - (C_CONV - 1 - j)).astype(BF16)
        acc = acc + cw_ref[j:j + 1, :] * jnp.dot(shift, ext, preferred_element_type=F32)
    xbc = jax.nn.silu(acc)
    xs = xbc[:, :C_INNER]
    gs_w = C_GROUPS * C_STATE
    bm = xbc[:, C_INNER:C_INNER + gs_w].astype(BF16)
    cm = xbc[:, C_INNER + gs_w:].astype(BF16)

    row = lax.broadcasted_iota(jnp.int32, (q_len, q_len), 0)
    col = lax.broadcasted_iota(jnp.int32, (q_len, q_len), 1)
    tri = col <= row

    lane = lax.broadcasted_iota(jnp.int32, (1, LANES), 1)
    dt = jnp.where(lane < C_HEADS, jax.nn.softplus(sdt_ref[0] + dtb_ref[...]), 0.0)
    dta = dt * -jnp.exp(alog_ref[...])
    acs = _dot_mask_lhs(tri.astype(BF16), dta)
    acs_t = acs.T
    chunk_dec = jnp.exp(acs[q_len - 1:q_len, :])
    expand = ex_ref[...]
    dt_x = _dot_mask_rhs(dt, expand)
    acs_x = _dot_mask_rhs(acs, expand)
    xd = xs * dt_x
    xd_b = xd.astype(BF16)
    xdd = xd * jnp.exp(acs_x[q_len - 1:q_len, :] - acs_x)
    low_half = lax.broadcasted_iota(jnp.int32, (q_len, LANES), 1) < C_HEAD_DIM

    y_pairs, y_offs = [], []
    tdims = (((1,), (1,)), ((), ()))
    for g in range(C_GROUPS):
        b_g = bm[:, g * C_STATE:(g + 1) * C_STATE]
        c_g = cm[:, g * C_STATE:(g + 1) * C_STATE]
        cb = lax.dot_general(c_g, b_g, tdims, preferred_element_type=F32)
        h0 = g * C_HPG
        grp = slice(h0 * C_HEAD_DIM, (h0 + C_HPG) * C_HEAD_DIM)
        y_offs.append(lax.dot_general(c_g, hs_ref[grp, :].astype(BF16), tdims, preferred_element_type=F32))
        for h in range(h0, h0 + C_HPG, 2):
            xp = xd_b[:, h * C_HEAD_DIM:(h + 2) * C_HEAD_DIM]
            halves = []
            for hh in (h, h + 1):
                seg = jnp.exp(jnp.where(tri, acs[:, hh:hh + 1] - acs_t[hh:hh + 1, :], -jnp.inf))
                halves.append(jnp.dot((cb * seg).astype(BF16), xp, preferred_element_type=F32))
            y_pairs.append(jnp.where(low_half, halves[0], halves[1]))
        upd = jnp.dot(xdd[:, grp].T.astype(BF16), b_g, preferred_element_type=F32)
        for hh in range(C_HPG):
            h = h0 + hh
            ps = slice(h * C_HEAD_DIM, (h + 1) * C_HEAD_DIM)
            us = slice(hh * C_HEAD_DIM, (hh + 1) * C_HEAD_DIM)
            hs_ref[ps, :] = hs_ref[ps, :] * chunk_dec[0:1, h:h + 1] + upd[us, :]
    y = (jnp.concatenate(y_pairs, axis=1) + jnp.concatenate(y_offs, axis=1) * jnp.exp(acs_x)
         + dsk_ref[...] * xs)
    y = y * jax.nn.silu(z_ref[0].astype(F32))
    o_ref[0, :, :C_INNER] = _rms(y, sn_ref[...]).astype(o_ref.dtype)

    same = (row // D_CHUNK) == (col // D_CHUNK)
    tri2 = tri & same
    la = jnp.dot(sg_ref[0].astype(BF16), gkw_ref[...], preferred_element_type=F32) + gkb_ref[...]
    la = jax.nn.log_sigmoid(la) / D_GATE_NORM
    gcs = _dot_mask_lhs(tri2.astype(BF16), la)
    first = lax.broadcasted_iota(jnp.int32, (q_len, 1), 0) < D_CHUNK
    r_all = r_ref[0]
    for h in range(D_HEADS):
        ks = slice(h * D_HK, (h + 1) * D_HK)
        vs = slice(h * D_HV, (h + 1) * D_HV)
        g_h = gcs[:, ks]
        g_end0 = g_h[D_CHUNK - 1:D_CHUNK, :]
        g_end1 = g_h[q_len - 1:q_len, :]
        q_h = q_ref[0, :, ks].astype(F32) * (D_HK ** -0.5)
        k_h = k_ref[0, :, ks].astype(F32)
        v_h = v_ref[0, :, vs]
        q_dec = (q_h * jnp.exp(g_h)).astype(BF16)
        k_inv = (k_h * jnp.exp(-g_h)).astype(BF16)
        k_end = k_h * jnp.exp(jnp.where(first, g_end0, g_end1) - g_h)
        ke0 = jnp.where(first, k_end, 0.0).astype(BF16)
        ke1 = jnp.where(first, 0.0, k_end).astype(BF16)
        attn = lax.dot_general(q_dec, k_inv, (((1,), (1,)), ((), ())), preferred_element_type=F32)
        attn = jnp.where(tri2, attn, 0.0).astype(BF16)
        o = jnp.dot(attn, v_h, preferred_element_type=F32)
        v_t = v_h.astype(F32).T.astype(BF16)
        st_rows = slice(h * D_HV, (h + 1) * D_HV)
        s0 = gs_ref[st_rows, :]
        s1 = s0 * jnp.exp(g_end0) + jnp.dot(v_t, ke0, preferred_element_type=F32)
        s2 = s1 * jnp.exp(g_end1) + jnp.dot(v_t, ke1, preferred_element_type=F32)
        gs_ref[st_rows, :] = s2
        tdims = (((1,), (1,)), ((), ()))
        o0 = lax.dot_general(q_dec, s0.astype(BF16), tdims, preferred_element_type=F32)
        o1 = lax.dot_general(q_dec, s1.astype(BF16), tdims, preferred_element_type=F32)
        o = o + jnp.where(first, o0, o1)
        o = _rms(o, gn_ref[...]) * jax.nn.silu(r_all[:, vs].astype(F32))
        o_ref[0, :, C_INNER + h * D_HV:C_INNER + (h + 1) * D_HV] = o.astype(o_ref.dtype)


def _ssd_gla(z, q, k, v, r, xbc, side_dt, side_g, g_lane, p):
    b, s, _ = z.shape
    q_len = C_CHUNK
    halo = 16
    chunk = lambda width: pl.BlockSpec((1, q_len, width), lambda i, n: (i, n, 0))
    full = lambda shape: pl.BlockSpec(shape, lambda i, n: (0,) * len(shape))
    pad_lanes = lambda vec: jnp.zeros((1, LANES), F32).at[0, :vec.shape[0]].set(vec.astype(F32))
    gkw = jnp.zeros((LANES, D_KEY), F32).at[g_lane:g_lane + D_GATE_RANK].set(p["gk_w"]).astype(BF16)
    expand = jnp.asarray(np.arange(C_INNER)[None, :] // C_HEAD_DIM == np.arange(LANES)[:, None], BF16)
    return pl.pallas_call(
        _ssd_gla_kernel,
        grid=(b, s // q_len),
        in_specs=[
            chunk(C_INNER), chunk(D_KEY), chunk(D_KEY), chunk(D_VAL), chunk(D_VAL), chunk(C_CONV_DIM),
            pl.BlockSpec((1, halo, C_CONV_DIM), lambda i, n: (i, jnp.maximum(n * (q_len // halo) - 1, 0), 0)),
            chunk(LANES), chunk(LANES),
            full((C_CONV, C_CONV_DIM)), full((1, C_CONV_DIM)),
            full((1, LANES)), full((1, LANES)), full((1, C_INNER)), full((LANES, C_INNER)),
            full((1, C_INNER)), full((LANES, D_KEY)), full((1, D_KEY)), full((1, D_HV)),
        ],
        out_specs=pl.BlockSpec((1, q_len, C_INNER + D_VAL), lambda i, n: (i, n, 0)),
        out_shape=jax.ShapeDtypeStruct((b, s, C_INNER + D_VAL), BF16),
        scratch_shapes=[pltpu.VMEM((C_INNER, C_STATE), F32),
                        pltpu.VMEM((D_VAL, D_HK), F32)],
        compiler_params=_cparams(("parallel", "arbitrary")),
        name="ssd_gla",
    )(z, q, k, v, r, xbc, xbc, side_dt, side_g,
      p["conv_w"].astype(F32), p["conv_b"].reshape(1, -1).astype(F32),
      pad_lanes(p["dt_bias"]), pad_lanes(p["a_log"]),
      jnp.repeat(p["d_skip"].astype(F32), C_HEAD_DIM).reshape(1, C_INNER), expand,
      p["ssd_norm"].reshape(1, -1).astype(F32), gkw, p["gk_b"].reshape(1, -1).astype(F32),
      p["gla_norm"].reshape(1, -1).astype(F32))


def _store_rows_tiled(ref, val):
    m, d = val.shape
    nc = d // LANES
    for c in range(nc):
        ref[pl.ds(c, m, stride=nc), :] = val[:, c * LANES:(c + 1) * LANES]


def _load_rows_tiled(ref, m, dtype=None):
    nc = ref.shape[0] // m
    parts = [ref[pl.ds(c, m, stride=nc), :] for c in range(nc)]
    if dtype is not None:
        parts = [p.astype(dtype) for p in parts]
    return jnp.concatenate(parts, axis=1)


def _pack_pairs(x):
    n = x.shape[1] // 2
    u = pltpu.bitcast(x.astype(BF16).astype(F32), jnp.uint32)
    return (u[:, :n] >> 16) | (u[:, n:] & jnp.uint32(0xFFFF0000))


def _unpack_pairs(u):
    lo = pltpu.bitcast(u << 16, F32).astype(BF16)
    hi = pltpu.bitcast(u & jnp.uint32(0xFFFF0000), F32).astype(BF16)
    return jnp.concatenate([lo, hi], axis=1)


RT_GATE, RT_EXPERT, RT_RANK = 0, 2, 4


def _route_block(lg, carry, earlier):
    m = lg.shape[0]
    lane = lax.broadcasted_iota(jnp.int32, (m, LANES), 1)
    lane_f = lane.astype(F32)
    none = float(LANES)
    neg = -jnp.inf
    first_max = lambda v, vmax: jnp.min(jnp.where(v == vmax, lane_f, none), axis=-1, keepdims=True)
    gl = jnp.where(lane < MOE_GROUPS, lg, neg)
    gmax = jnp.max(gl, axis=-1, keepdims=True)
    g_w = 1.0 / jnp.sum(jnp.exp(gl - gmax), axis=-1, keepdims=True)
    lo = MOE_GROUPS + first_max(gl, gmax) * MOE_EPG
    el = jnp.where((lane_f >= lo) & (lane_f < lo + MOE_EPG), lg, neg)
    emax = jnp.max(el, axis=-1, keepdims=True)
    esum = jnp.sum(jnp.exp(el - emax), axis=-1, keepdims=True)
    l0 = first_max(el, emax)
    el2 = jnp.where(lane_f == l0, neg, el)
    emax2 = jnp.max(el2, axis=-1, keepdims=True)
    l1 = first_max(el2, emax2)
    p0 = 1.0 / esum
    p1 = jnp.exp(emax2 - emax) / esum
    w0 = g_w * (p0 / (p0 + p1))
    w1 = g_w * (p1 / (p0 + p1))
    oh0 = lane_f == l0
    oh1 = lane_f == l1
    oh = (oh0 | oh1).astype(BF16)
    cum = jnp.dot(earlier, oh, preferred_element_type=F32) + carry
    rank0 = jnp.sum(jnp.where(oh0, cum, 0.0), axis=-1, keepdims=True)
    rank1 = jnp.sum(jnp.where(oh1, cum, 0.0), axis=-1, keepdims=True)
    carry = carry + jnp.sum(oh.astype(F32), axis=0, keepdims=True)
    rec = jnp.zeros((m, LANES), F32)
    for pos, val in ((RT_GATE, w0), (RT_GATE + 1, w1), (RT_EXPERT, l0 - MOE_GROUPS),
                     (RT_EXPERT + 1, l1 - MOE_GROUPS), (RT_RANK, rank0), (RT_RANK + 1, rank1)):
        rec = jnp.where(lane == pos, val, rec)
    return rec, carry


def _out_proj_kernel(n_parts, *refs):
    a_refs = refs[:n_parts]
    w_refs = refs[n_parts:2 * n_parts]
    h_ref, g_ref, wr_ref, br_ref, tri_ref, ho_ref, xt_ref, rt_ref, cnt_ref, carry_ref = refs[2 * n_parts:]

    @pl.when(pl.program_id(0) == 0)
    def _():
        carry_ref[...] = jnp.zeros_like(carry_ref)

    acc = h_ref[...]
    for a_ref, w_ref in zip(a_refs, w_refs):
        acc = acc + jnp.dot(a_ref[...], w_ref[...], preferred_element_type=F32)
    ho_ref[...] = acc
    xn = _rms(acc, g_ref[...])
    xt_ref[...] = xn.astype(xt_ref.dtype)
    x_hi, x_mid, _ = _split3(xn)
    wr = wr_ref[...]
    lg2 = jnp.dot(x_hi, wr, preferred_element_type=F32)
    lg = (lg2[:, :LANES] + lg2[:, LANES:] + jnp.dot(x_mid, wr[:, :LANES], preferred_element_type=F32)
          + br_ref[...])
    rec, carry = _route_block(lg, carry_ref[...], tri_ref[...])
    rt_ref[...] = rec
    carry_ref[...] = carry
    cnt_ref[0] = carry


def _out_proj(parts, w_parts, h, g, w_route, b_route, *, tm=512):
    t, d = h.shape
    tm = min(tm, t)
    nc = d // LANES
    row = lambda i: (i, 0)
    const = lambda i: (0, 0)
    in_specs = [pl.BlockSpec((tm, a.shape[1]), row) for a in parts]
    in_specs += [pl.BlockSpec(w.shape, const) for w in w_parts]
    in_specs += [pl.BlockSpec((tm, d), row), pl.BlockSpec((1, d), const),
                 pl.BlockSpec((d, 2 * LANES), const), pl.BlockSpec((1, LANES), const),
                 pl.BlockSpec((tm, tm), const)]
    earlier = jnp.asarray(np.tril(np.ones((tm, tm), np.float32), -1), BF16)
    return pl.pallas_call(
        functools.partial(_out_proj_kernel, len(parts)),
        grid=(t // tm,),
        in_specs=in_specs,
        out_specs=[pl.BlockSpec((tm, d), row), pl.BlockSpec((tm, d), row),
                   pl.BlockSpec((tm, LANES), row), pl.BlockSpec((1, 1, LANES), lambda i: (i, 0, 0))],
        out_shape=[jax.ShapeDtypeStruct((t, d), F32), jax.ShapeDtypeStruct((t, d), BF16),
                   jax.ShapeDtypeStruct((t, LANES), F32), jax.ShapeDtypeStruct((t // tm, 1, LANES), F32)],
        scratch_shapes=[pltpu.VMEM((1, LANES), F32)],
        compiler_params=_cparams(("arbitrary",)),
        name="out_proj",
    )(*parts, *w_parts, h, g.reshape(1, d), w_route, b_route, earlier)


def _for_each_piece(run, max_rows, body):
    off = jnp.int32(0)
    for n in (1 << b for b in reversed(range(max_rows.bit_length()))):
        hit = (run & n) != 0
        pl.when(hit)(functools.partial(body, off, n))
        off = off + jnp.where(hit, n, 0)


def _zero_fill_rows(rows_ref, z_ref, zsem, lo_ref, hi_ref, n_rows, nc):
    zb = z_ref.shape[0] // nc
    z_ref[...] = jnp.zeros_like(z_ref)
    assert MOE_TILE // 2 <= zb and MOE_TILE % zb == 0

    def piece(row0, n):
        return pltpu.make_async_copy(z_ref.at[pl.ds(0, n * nc)], rows_ref.at[pl.ds(row0 * nc, n * nc)], zsem)

    def sweep(issue):
        def per_expert(e, carry):
            lo = lo_ref[e]

            def one(off, n):
                piece(lo + off, n).start() if issue else piece(lo + off, n).wait()

            _for_each_piece(hi_ref[e] - lo, MOE_TILE // 2, one)
            return carry

        def per_block(i, carry):
            piece(i * zb, zb).start() if issue else piece(i * zb, zb).wait()
            return carry

        lax.fori_loop(0, MOE_EXPERTS, per_expert, 0)
        lax.fori_loop(hi_ref[MOE_EXPERTS - 1] // zb, n_rows // zb, per_block, 0)

    sweep(True)
    sweep(False)


def _tile_positions(rt, pos_base):
    lane_f = lax.broadcasted_iota(jnp.int32, rt.shape, 1).astype(F32)
    out = []
    for k in range(MOE_TOPK):
        e_lane = rt[:, RT_EXPERT + k:RT_EXPERT + k + 1] + MOE_GROUPS
        out.append(rt[:, RT_RANK + k:RT_RANK + k + 1]
                   + jnp.sum(jnp.where(lane_f == e_lane, pos_base, 0.0), axis=-1, keepdims=True))
    return out


def _moe_dispatch_kernel(dst_ref, len_ref, lo_ref, hi_ref, x_ref, rt_ref, pb_ref, rows_ref,
                         sbuf, z_ref, sem, zsem, *, tm, nc, n_steps, n_rows):
    i = pl.program_id(0)
    rows = MOE_TOPK * tm
    slot = lax.rem(i, 2)

    def wait_slot(s):
        pltpu.make_async_copy(sbuf.at[s], rows_ref.at[pl.ds(0, rows * nc)], sem.at[s]).wait()

    @pl.when(i == 0)
    def _():
        _zero_fill_rows(rows_ref, z_ref, zsem, lo_ref, hi_ref, n_rows, nc)

    @pl.when(i >= 2)
    def _():
        wait_slot(slot)

    pos = _tile_positions(rt_ref[...], pb_ref[0])
    lane = lax.broadcasted_iota(jnp.int32, (tm, LANES), 1)
    pos_t = jnp.where(lane == 0, pos[0], jnp.where(lane == 1, pos[1], 0.0)).T
    p_iota = lax.broadcasted_iota(jnp.int32, (rows, tm), 0).astype(F32)
    place = ((p_iota == pos_t[0:1, :]) | (p_iota == pos_t[1:2, :])).astype(BF16)
    _store_rows_tiled(sbuf.at[slot], _pack_pairs(jnp.dot(place, x_ref[...], preferred_element_type=F32)))

    def per_expert(e, src):
        seg = i * MOE_EXPERTS + e
        dst = dst_ref[seg]

        def send(off, n):
            pltpu.make_async_copy(sbuf.at[slot, pl.ds((src + off) * nc, n * nc)],
                                  rows_ref.at[pl.ds((dst + off) * nc, n * nc)], sem.at[slot]).start()

        _for_each_piece(len_ref[seg], rows, send)
        return src + len_ref[seg]

    lax.fori_loop(0, MOE_EXPERTS, per_expert, jnp.int32(0))

    @pl.when(i == n_steps - 1)
    def _():
        wait_slot(slot)
        if n_steps > 1:
            wait_slot(1 - slot)


def _moe_dispatch(x, route, pos_base, seg_dst, seg_len, pad_lo, pad_hi, n_rows, *, tm):
    t, d = x.shape
    nc = d // (2 * LANES)
    n_steps = t // tm
    row = lambda i, *_: (i, 0)
    return pl.pallas_call(
        functools.partial(_moe_dispatch_kernel, tm=tm, nc=nc, n_steps=n_steps, n_rows=n_rows),
        grid_spec=pltpu.PrefetchScalarGridSpec(
            num_scalar_prefetch=4, grid=(n_steps,),
            in_specs=[pl.BlockSpec((tm, d), row), pl.BlockSpec((tm, LANES), row),
                      pl.BlockSpec((1, 1, LANES), lambda i, *_: (i, 0, 0))],
            out_specs=pl.BlockSpec(memory_space=pl.ANY),
            scratch_shapes=[pltpu.VMEM((2, MOE_TOPK * tm * nc, LANES), jnp.uint32),
                            pltpu.VMEM((MOE_TILE // 2 * nc, LANES), jnp.uint32),
                            pltpu.SemaphoreType.DMA((2,)), pltpu.SemaphoreType.DMA(())]),
        out_shape=jax.ShapeDtypeStruct((n_rows * nc, LANES), jnp.uint32),
        compiler_params=pltpu.CompilerParams(dimension_semantics=("arbitrary",), has_side_effects=True,
                                             vmem_limit_bytes=VMEM_LIMIT),
        name="moe_dispatch",
    )(seg_dst, seg_len, pad_lo, pad_hi, x, route, pos_base)


def _moe_kernel(te_ref, nu_ref, x_ref, wg_ref, wu_ref, wd_ref, y_ref):
    i = pl.program_id(0)

    @pl.when(i >= nu_ref[0])
    def _():
        y_ref[...] = jnp.zeros_like(y_ref)

    @pl.when(i < nu_ref[0])
    def _():
        x = _unpack_pairs(_load_rows_tiled(x_ref, MOE_TILE))
        gate = jnp.dot(x, wg_ref[...].astype(BF16), preferred_element_type=F32)
        up = jnp.dot(x, wu_ref[...].astype(BF16), preferred_element_type=F32)
        act = (jax.nn.silu(gate) * up).astype(BF16)
        y = jnp.dot(act, wd_ref[...].astype(BF16), preferred_element_type=F32)
        _store_rows_tiled(y_ref, _pack_pairs(y))


def _moe_experts(x_rows, tile_expert, n_used, w_gate, w_up, w_down, layer):
    d, ff = w_gate.shape[-2:]
    nc = d // (2 * LANES)
    n_tiles = x_rows.shape[0] // (MOE_TILE * nc)
    live = lambda i, nu: jnp.minimum(i, nu[0] - 1)
    w_spec = lambda a, b: pl.BlockSpec((None, None, a, b), lambda i, te, nu: (layer, te[live(i, nu)], 0, 0))
    return pl.pallas_call(
        _moe_kernel,
        grid_spec=pltpu.PrefetchScalarGridSpec(
            num_scalar_prefetch=2, grid=(n_tiles,),
            in_specs=[pl.BlockSpec((MOE_TILE * nc, LANES), lambda i, te, nu: (live(i, nu), 0)),
                      w_spec(d, ff), w_spec(d, ff), w_spec(ff, d)],
            out_specs=pl.BlockSpec((MOE_TILE * nc, LANES), lambda i, te, nu: (i, 0))),
        out_shape=jax.ShapeDtypeStruct(x_rows.shape, x_rows.dtype),
        compiler_params=_cparams(("arbitrary",)),
        name="moe_experts",
    )(tile_expert, n_used, x_rows, w_gate, w_up, w_down)


def _moe_combine_kernel(src_ref, len_ref, h_ref, rt_ref, pb_ref, g_ref, y_hbm, o_ref, ybuf, sem, *, tm, nc, final):
    i = pl.program_id(0)
    n_steps = pl.num_programs(0)
    rows = MOE_TOPK * tm

    def fetch(tile, slot):
        def per_expert(e, dst):
            seg = tile * MOE_EXPERTS + e
            src = src_ref[seg]

            def recv(off, n):
                pltpu.make_async_copy(y_hbm.at[pl.ds((src + off) * nc, n * nc)],
                                      ybuf.at[slot, pl.ds((dst + off) * nc, n * nc)], sem.at[slot]).start()

            _for_each_piece(len_ref[seg], rows, recv)
            return dst + len_ref[seg]

        lax.fori_loop(0, MOE_EXPERTS, per_expert, jnp.int32(0))

    slot = lax.rem(i, 2)

    @pl.when(i == 0)
    def _():
        fetch(0, 0)

    @pl.when(i + 1 < n_steps)
    def _():
        fetch(i + 1, 1 - slot)

    pltpu.make_async_copy(y_hbm.at[pl.ds(0, rows * nc)], ybuf.at[slot], sem.at[slot]).wait()
    y = _unpack_pairs(_load_rows_tiled(ybuf.at[slot], rows))
    rt = rt_ref[...]
    pos_f = lax.broadcasted_iota(jnp.int32, (tm, rows), 1).astype(F32)
    pick = jnp.zeros((tm, rows), F32)
    for k, pos in enumerate(_tile_positions(rt, pb_ref[0])):
        pick = jnp.where(pos_f == pos, rt[:, RT_GATE + k:RT_GATE + k + 1], pick)
    out = h_ref[...] + jnp.dot(pick.astype(BF16), y, preferred_element_type=F32)
    o_ref[...] = _rms(out, g_ref[...]) if final else out


def _moe_combine_rows(h, y_rows, route, seg_src, seg_len, pos_base, g, *, tm, final):
    t, d = h.shape
    nc = d // (2 * LANES)
    row = lambda i, *_: (i, 0)
    return pl.pallas_call(
        functools.partial(_moe_combine_kernel, tm=tm, nc=nc, final=final),
        grid_spec=pltpu.PrefetchScalarGridSpec(
            num_scalar_prefetch=2, grid=(t // tm,),
            in_specs=[pl.BlockSpec((tm, d), row), pl.BlockSpec((tm, LANES), row),
                      pl.BlockSpec((1, 1, LANES), lambda i, *_: (i, 0, 0)),
                      pl.BlockSpec((1, d), lambda i, *_: (0, 0)),
                      pl.BlockSpec(memory_space=pl.ANY)],
            out_specs=pl.BlockSpec((tm, d), row),
            scratch_shapes=[pltpu.VMEM((2, MOE_TOPK * tm * nc, LANES), y_rows.dtype),
                            pltpu.SemaphoreType.DMA((2,))]),
        out_shape=jax.ShapeDtypeStruct((t, d), F32),
        compiler_params=_cparams(("arbitrary",)),
        name="moe_combine",
    )(seg_src, seg_len, h, route, pos_base, g.reshape(1, d), y_rows)


def _moe(h, x, route, tile_counts, w_gate, w_up, w_down, layer, g, *, final):
    t = route.shape[0]
    n_tt = tile_counts.shape[0]
    tm = t // n_tt
    after = tile_counts[:, 0, MOE_GROUPS:MOE_GROUPS + MOE_EXPERTS].astype(jnp.int32)
    before = jnp.concatenate([jnp.zeros((1, MOE_EXPERTS), jnp.int32), after[:-1]], axis=0)
    cnt = after[-1]
    padded = (cnt + MOE_TILE - 1) // MOE_TILE * MOE_TILE
    pad_ends = jnp.cumsum(padded)
    starts = (pad_ends - padded).astype(jnp.int32)
    n_tiles = (t * MOE_TOPK + MOE_EXPERTS * (MOE_TILE - 1)) // MOE_TILE
    tile_start = jnp.arange(n_tiles, dtype=jnp.int32) * MOE_TILE
    tile_expert = jnp.minimum(jnp.sum(tile_start[:, None] >= pad_ends[None, :], axis=1),
                              MOE_EXPERTS - 1).astype(jnp.int32)
    n_used = (pad_ends[-1] // MOE_TILE).astype(jnp.int32).reshape(1)
    n_rows = n_tiles * MOE_TILE
    seg_len = (after - before).reshape(-1)
    seg_off = jnp.cumsum(after - before, axis=1) - (after - before)
    seg_row = (starts[None, :] + before).reshape(-1)
    pos_base = jnp.zeros((n_tt, 1, LANES), F32).at[:, 0, MOE_GROUPS:MOE_GROUPS + MOE_EXPERTS].set(
        (seg_off - before).astype(F32))
    x_rows = _moe_dispatch(x, route, pos_base, seg_row, seg_len, starts + cnt, pad_ends.astype(jnp.int32),
                           n_rows, tm=tm)
    y_rows = _moe_experts(x_rows, tile_expert, n_used, w_gate, w_up, w_down, layer)
    return _moe_combine_rows(h, y_rows, route, seg_row, seg_len, pos_base, g, tm=tm, final=final)


def _router_weights(w_group, b_group, w_router, b_router):
    d = w_group.shape[0]
    w = jnp.zeros((d, LANES), F32)
    w = w.at[:, :MOE_GROUPS].set(w_group).at[:, MOE_GROUPS:MOE_GROUPS + MOE_EXPERTS].set(w_router)
    b = jnp.zeros((1, LANES), F32)
    b = b.at[0, :MOE_GROUPS].set(b_group).at[0, MOE_GROUPS:MOE_GROUPS + MOE_EXPERTS].set(b_router)
    w_hi = w.astype(BF16)
    w_mid = (w - w_hi.astype(F32)).astype(BF16)
    return jnp.concatenate([w_hi, w_mid], axis=1), b


def kernel(x, norm_mix, norm_moe, norm_final, even_w_in, even_sinks, even_forget_bias, even_w_out,
           odd_w_in, odd_conv_w, odd_conv_b, odd_dt_bias, odd_a_log, odd_d_skip, odd_ssd_norm,
           odd_gk_w, odd_gk_b, odd_gla_norm, odd_w_out, moe_w_group, moe_b_group, moe_w_router,
           moe_b_router, moe_w_gate, moe_w_up, moe_w_down):
    b, s, d = x.shape
    t = b * s
    depth = norm_mix.shape[0]
    h = x.reshape(t, d)
    for layer in range(depth):
        i = layer // 2
        if layer % 2 == 0:
            w = even_w_in[i]
            n_ab = (A_Q_HEADS + 2 * A_KV_HEADS + 3 * B_HEADS) * HEAD_DIM
            w_main = w[:, :n_ab].astype(BF16)
            w_aux = jnp.zeros((d, LANES), F32).at[:, :B_HEADS].set(w[:, n_ab:]).astype(BF16)
            proj, f_aux = _norm_proj(h, norm_mix[layer], [w_main, w_aux],
                                     ((0, 0, n_ab, BF16), (1, 0, LANES, F32)), tm=512)
            proj = proj.reshape(b, s, -1)
            out_a = _swa(proj, even_sinks[i])
            c, ct = _fox_gate(f_aux.reshape(b, s, LANES), even_forget_bias[i])
            out_b = _fox(proj, c, ct)
            n_ha = A_Q_HEADS * HEAD_DIM
            w_out = even_w_out[i].astype(BF16)
            parts = [out_a.reshape(t, -1), out_b.reshape(t, -1)]
            w_parts = [w_out[:n_ha], w_out[n_ha:]]
        else:
            w = odd_w_in[i]
            o_z, o_xbc = 0, C_INNER
            o_dt = o_xbc + C_CONV_DIM
            o_q = o_dt + C_HEADS
            o_k = o_q + D_KEY
            o_v = o_k + D_KEY
            o_g = o_v + D_VAL
            o_r = o_g + D_GATE_RANK
            n_b = w.shape[1] - o_dt
            w_a = w[:, :o_dt].astype(BF16)
            w_b = jnp.pad(w[:, o_dt:].astype(BF16), ((0, 0), (0, -n_b % LANES)))
            g_win = (o_g - o_dt) // LANES * LANES
            plan = ((0, o_z, C_INNER, BF16), (1, o_q - o_dt, D_KEY, BF16), (1, o_k - o_dt, D_KEY, BF16),
                    (1, o_v - o_dt, D_VAL, BF16), (1, o_r - o_dt, D_VAL, BF16), (0, o_xbc, C_CONV_DIM, BF16),
                    (1, 0, LANES, F32), (1, g_win, LANES, F32))
            outs = _norm_proj(h, norm_mix[layer], [w_a, w_b], plan, tm=256)
            params = dict(conv_w=odd_conv_w[i], conv_b=odd_conv_b[i], dt_bias=odd_dt_bias[i], a_log=odd_a_log[i],
                          d_skip=odd_d_skip[i], ssd_norm=odd_ssd_norm[i], gk_w=odd_gk_w[i], gk_b=odd_gk_b[i],
                          gla_norm=odd_gla_norm[i])
            mixed = _ssd_gla(*[o.reshape(b, s, -1) for o in outs], o_g - o_dt - g_win, params)
            parts = [mixed.reshape(t, -1)]
            w_parts = [odd_w_out[i].astype(BF16)]
        w_route, b_route = _router_weights(moe_w_group[layer], moe_b_group[layer],
                                           moe_w_router[layer], moe_b_router[layer])
        h, x_tiled, route, tile_counts = _out_proj(parts, w_parts, h, norm_moe[layer], w_route, b_route)
        h = _moe(h, x_tiled, route, tile_counts, moe_w_gate, moe_w_up, moe_w_down, layer, norm_final,
                 final=layer == depth - 1)
    out = h
    return out.reshape(b, s, d)
```

```python
import functools

import numpy as np
import jax
import jax.numpy as jnp
from jax import lax
from jax.experimental import pallas as pl
from jax.experimental.pallas import tpu as pltpu

F32 = jnp.float32
BF16 = jnp.bfloat16

RMS_EPS = 1e-6
HEAD_DIM = 64
A_Q_HEADS = 8
A_KV_HEADS = 2
A_GROUP = A_Q_HEADS // A_KV_HEADS
A_WINDOW = 128
B_HEADS = 8
C_HEADS = 16
C_HEAD_DIM = 64
C_INNER = C_HEADS * C_HEAD_DIM
C_GROUPS = 2
C_HPG = C_HEADS // C_GROUPS
C_STATE = 128
C_CONV = 4
C_CHUNK = 128
C_CONV_DIM = C_INNER + 2 * C_GROUPS * C_STATE
D_HEADS = 4
D_HK = 128
D_HV = 256
D_KEY = D_HEADS * D_HK
D_VAL = D_HEADS * D_HV
D_GATE_RANK = 16
D_GATE_NORM = 16.0
D_CHUNK = 64
MOE_GROUPS = 4
MOE_EPG = 8
MOE_EXPERTS = MOE_GROUPS * MOE_EPG
MOE_TOPK = 2

LANES = 128
VMEM_LIMIT = 48 * 1024 * 1024
MOE_TILE = 512


def _cparams(sem):
    return pltpu.CompilerParams(dimension_semantics=sem, vmem_limit_bytes=VMEM_LIMIT)


def _rms(x, g):
    ms = jnp.mean(x * x, axis=-1, keepdims=True)
    return x * lax.rsqrt(ms + RMS_EPS) * g


def _norm_proj_kernel(n_w, plan, *refs):
    x_ref, g_ref = refs[:2]
    w_refs = refs[2:2 + n_w]
    o_refs = refs[2 + n_w:2 + n_w + len(plan)]
    res_refs = refs[2 + n_w + len(plan):]
    xn = _rms(x_ref[...], g_ref[...]).astype(BF16)
    for w_ref, res_ref in zip(w_refs, res_refs):
        res_ref[...] = jnp.dot(xn, w_ref[...], preferred_element_type=F32)
    for o_ref, (wi, start, width, _) in zip(o_refs, plan):
        o_ref[...] = res_refs[wi][:, start:start + width].astype(o_ref.dtype)


def _norm_proj(x, g, weights, plan, *, tm):
    t, d = x.shape
    tm = min(tm, t)
    row = lambda i: (i, 0)
    const = lambda i: (0, 0)
    in_specs = [pl.BlockSpec((tm, d), row), pl.BlockSpec((1, d), const)]
    in_specs += [pl.BlockSpec(w.shape, const, pipeline_mode=pl.Buffered(1)) for w in weights]
    return pl.pallas_call(
        functools.partial(_norm_proj_kernel, len(weights), plan),
        grid=(t // tm,),
        in_specs=in_specs,
        out_specs=[pl.BlockSpec((tm, width), row) for _, _, width, _ in plan],
        out_shape=[jax.ShapeDtypeStruct((t, width), dtype) for _, _, width, dtype in plan],
        scratch_shapes=[pltpu.VMEM((tm, w.shape[1]), F32) for w in weights],
        compiler_params=_cparams(("parallel",)),
        name="norm_proj",
    )(x, g.reshape(1, d), *weights)


SWA_QBLOCKS = 4


def _swa_kernel(sink_ref, slope_ref, q_ref, kp_ref, kc_ref, vp_ref, vc_ref, o_ref):
    n = pl.program_id(1)
    blk = A_WINDOW
    wide = A_GROUP * blk
    key = lax.broadcasted_iota(jnp.int32, (2 * blk, wide), 0)
    qry = lax.broadcasted_iota(jnp.int32, (2 * blk, wide), 1) % blk
    dist = blk + qry - key
    in_window = (dist >= 0) & (dist < A_WINDOW)
    distf = dist.astype(F32)
    nt = (((1,), (1,)), ((), ()))
    k_all = jnp.concatenate([kp_ref[0], kc_ref[0]], axis=0)
    v_all = jnp.concatenate([vp_ref[0], vc_ref[0]], axis=1)
    units = [(j, kh) for j in range(SWA_QBLOCKS) for kh in range(A_KV_HEADS)]
    scores = []
    for j, kh in units:
        k = k_all[j * blk:(j + 2) * blk, kh * HEAD_DIM:(kh + 1) * HEAD_DIM]
        q = jnp.concatenate([q_ref[0, j * blk:(j + 1) * blk,
                                   (kh * A_GROUP + g) * HEAD_DIM:(kh * A_GROUP + g + 1) * HEAD_DIM]
                             for g in range(A_GROUP)], axis=0)
        scores.append(lax.dot_general(k, q, nt, preferred_element_type=F32))
    probs = []
    for (j, kh), s in zip(units, scores):
        valid = in_window & ((key >= blk) | (n * SWA_QBLOCKS + j > 0))
        s = s * (HEAD_DIM ** -0.5) - slope_ref[kh:kh + 1, :] * distf
        s = jnp.where(valid, s, -jnp.inf)
        sink = sink_ref[kh:kh + 1, :]
        m = jnp.maximum(jnp.max(s, axis=0, keepdims=True), sink)
        p = jnp.exp(s - m)
        probs.append((p.astype(BF16), jnp.sum(p, axis=0, keepdims=True) + jnp.exp(sink - m)))
    for j in range(SWA_QBLOCKS):
        outs = []
        for kh in range(A_KV_HEADS):
            p, denom = probs[j * A_KV_HEADS + kh]
            v_t = v_all[kh * HEAD_DIM:(kh + 1) * HEAD_DIM, j * blk:(j + 2) * blk]
            o_t = jnp.dot(v_t, p, preferred_element_type=F32) / denom
            outs += [o_t[:, g * blk:(g + 1) * blk] for g in range(A_GROUP)]
        o_ref[0, j * blk:(j + 1) * blk, :] = jnp.concatenate(outs, axis=0).T.astype(o_ref.dtype)


def _swa(proj, sinks):
    b, s, _ = proj.shape
    blk = A_WINDOW
    qw = A_Q_HEADS * HEAD_DIM
    kw = A_KV_HEADS * HEAD_DIM
    k_blk = qw // kw
    v_t = proj[:, :, qw + kw:qw + 2 * kw].transpose(0, 2, 1)
    per_lane = lambda vec: jnp.repeat(vec.astype(F32), blk).reshape(A_KV_HEADS, A_GROUP * blk)
    slopes = jnp.asarray(2.0 ** (-8.0 * np.arange(1, A_Q_HEADS + 1) / A_Q_HEADS), F32)
    tq = SWA_QBLOCKS * blk
    prev = lambda n: jnp.maximum(n * SWA_QBLOCKS - 1, 0)
    full = pl.BlockSpec((A_KV_HEADS, A_GROUP * blk), lambda i, n: (0, 0))
    return pl.pallas_call(
        _swa_kernel,
        grid=(b, s // tq),
        in_specs=[
            full, full,
            pl.BlockSpec((1, tq, qw), lambda i, n: (i, n, 0)),
            pl.BlockSpec((1, blk, kw), lambda i, n: (i, prev(n), k_blk)),
            pl.BlockSpec((1, tq, kw), lambda i, n: (i, n, k_blk)),
            pl.BlockSpec((1, kw, blk), lambda i, n: (i, 0, prev(n))),
            pl.BlockSpec((1, kw, tq), lambda i, n: (i, 0, n)),
        ],
        out_specs=pl.BlockSpec((1, tq, qw), lambda i, n: (i, n, 0)),
        out_shape=jax.ShapeDtypeStruct((b, s, qw), BF16),
        compiler_params=_cparams(("parallel", "parallel")),
        name="swa",
    )(per_lane(sinks), per_lane(slopes), proj, proj, proj, v_t, v_t)


def _tril(n, dtype=F32):
    r = lax.broadcasted_iota(jnp.int32, (n, n), 0)
    c = lax.broadcasted_iota(jnp.int32, (n, n), 1)
    return (c <= r).astype(dtype)


def _split3(x):
    hi = x.astype(BF16)
    r = x - hi.astype(F32)
    mid = r.astype(BF16)
    return hi, mid, (r - mid.astype(F32)).astype(BF16)


def _dot_mask_lhs(mask, x):
    return sum(jnp.dot(mask, part, preferred_element_type=F32) for part in _split3(x))


def _dot_mask_rhs(x, mask):
    return sum(jnp.dot(part, mask, preferred_element_type=F32) for part in _split3(x))


def _fox_gate_kernel(f_ref, b_ref, c_ref, ct_ref):
    tri = _tril(LANES, BF16)
    carry = jnp.zeros((1, LANES), F32)
    for n in range(f_ref.shape[1] // LANES):
        rows = slice(n * LANES, (n + 1) * LANES)
        lf = jax.nn.log_sigmoid(f_ref[0, rows, :] + b_ref[...])
        cs = _dot_mask_lhs(tri, lf) + carry
        carry = cs[LANES - 1:LANES, :]
        c_ref[0, rows, :] = cs
        ct_ref[0, n] = cs.T[:B_HEADS, :]


def _fox_gate(f_aux, bias):
    b, s, _ = f_aux.shape
    nb = s // LANES
    bias_p = jnp.zeros((1, LANES), F32).at[0, :B_HEADS].set(bias.astype(F32))
    return pl.pallas_call(
        _fox_gate_kernel,
        grid=(b,),
        in_specs=[pl.BlockSpec((1, s, LANES), lambda i: (i, 0, 0)),
                  pl.BlockSpec((1, LANES), lambda i: (0, 0))],
        out_specs=[pl.BlockSpec((1, s, LANES), lambda i: (i, 0, 0)),
                   pl.BlockSpec((1, nb, B_HEADS, LANES), lambda i: (i, 0, 0, 0))],
        out_shape=[jax.ShapeDtypeStruct((b, s, LANES), F32),
                   jax.ShapeDtypeStruct((b, nb, B_HEADS, LANES), F32)],
        compiler_params=_cparams(("parallel",)),
        name="fox_gate",
    )(f_aux, bias_p)


def _fox_kernel(q0_ref, q1_ref, k0_ref, k1_ref, vt_ref, c_ref, ctq_ref, o_ref, *, tq, heads_per_step):
    qi = pl.program_id(1)
    sub = tq // LANES
    key = lax.broadcasted_iota(jnp.int32, (tq, tq), 0)
    qry = lax.broadcasted_iota(jnp.int32, (tq, tq), 1)
    causal = key <= qry
    nt = (((1,), (1,)), ((), ()))
    half = B_HEADS // 2
    q_refs, k_refs = (q0_ref, q1_ref), (k0_ref, k1_ref)
    outs = []
    for h0 in range(0, B_HEADS, heads_per_step):
        heads = list(range(h0, h0 + heads_per_step))
        hsl = [slice(h * HEAD_DIM, (h + 1) * HEAD_DIM) for h in heads]
        lsl = [slice((h % half) * HEAD_DIM, (h % half + 1) * HEAD_DIM) for h in heads]
        qs = [q_refs[h // half][0, :, ls] * (HEAD_DIM ** -0.5)
              for h, ls in zip(heads, lsl)]
        cqs = [jnp.concatenate([ctq_ref[0, u, h:h + 1, :] for u in range(sub)], axis=1) for h in heads]

        def step(j, carry, masked, heads=heads, hsl=hsl, lsl=lsl, qs=qs, cqs=cqs):
            start = pl.multiple_of(j * tq, tq)
            sts = [lax.dot_general(k_refs[h // half][0, pl.ds(start, tq), ls], q, nt,
                                   preferred_element_type=F32)
                   for h, ls, q in zip(heads, lsl, qs)]
            ps, stats = [], []
            for idx, h in enumerate(heads):
                m, l, _ = carry[3 * idx:3 * idx + 3]
                ck = c_ref[0, pl.ds(start, tq), h:h + 1]
                st = (sts[idx] - ck) + cqs[idx]
                if masked:
                    st = jnp.where(causal, st, -jnp.inf)
                m_new = jnp.maximum(m, jnp.max(st, axis=0, keepdims=True))
                alpha = jnp.exp(m - m_new)
                p = jnp.exp(st - m_new)
                stats.append((m_new, alpha, alpha * l + jnp.sum(p, axis=0, keepdims=True)))
                ps.append(p.astype(BF16))
            new = []
            for idx in range(len(heads)):
                m_new, alpha, l = stats[idx]
                pv = jnp.dot(vt_ref[0, j, hsl[idx], :], ps[idx], preferred_element_type=F32)
                new += [m_new, l, alpha * carry[3 * idx + 2] + pv]
            return tuple(new)

        init = (jnp.full((1, tq), -jnp.inf, F32), jnp.zeros((1, tq), F32),
                jnp.zeros((HEAD_DIM, tq), F32)) * heads_per_step
        carry = lax.fori_loop(0, qi, functools.partial(step, masked=False), init)
        carry = step(qi, carry, True)
        for idx in range(heads_per_step):
            outs.append(carry[3 * idx + 2] / carry[3 * idx + 1])
    o_ref[0] = jnp.concatenate(outs, axis=0).T.astype(o_ref.dtype)


def _fox(proj, c, ct, *, tq=256, heads_per_step=8):
    b, s, _ = proj.shape
    w = B_HEADS * HEAD_DIM
    nk = s // tq
    sub = tq // LANES
    hw = w // 2
    base = (A_Q_HEADS + 2 * A_KV_HEADS) * HEAD_DIM
    qb, kb = base // hw, (base + w) // hw
    v_t = proj[:, :, base + 2 * w:base + 3 * w].reshape(b, nk, tq, w).transpose(0, 1, 3, 2)
    return pl.pallas_call(
        functools.partial(_fox_kernel, tq=tq, heads_per_step=heads_per_step),
        grid=(b, s // tq),
        in_specs=[
            pl.BlockSpec((1, tq, hw), lambda i, n: (i, n, qb)),
            pl.BlockSpec((1, tq, hw), lambda i, n: (i, n, qb + 1)),
            pl.BlockSpec((1, s, hw), lambda i, n: (i, 0, kb)),
            pl.BlockSpec((1, s, hw), lambda i, n: (i, 0, kb + 1)),
            pl.BlockSpec((1, nk, w, tq), lambda i, n: (i, 0, 0, 0)),
            pl.BlockSpec((1, s, LANES), lambda i, n: (i, 0, 0)),
            pl.BlockSpec((1, sub, B_HEADS, LANES), lambda i, n: (i, n, 0, 0)),
        ],
        out_specs=pl.BlockSpec((1, tq, w), lambda i, n: (i, n, 0)),
        out_shape=jax.ShapeDtypeStruct((b, s, w), BF16),
        compiler_params=_cparams(("parallel", "parallel")),
        name="fox",
    )(proj, proj, proj, proj, v_t, c, ct)


def _ssd_gla_kernel(z_ref, q_ref, k_ref, v_ref, r_ref, xc_ref, xp_ref, sdt_ref, sg_ref,
                    cw_ref, cb_ref, dtb_ref, alog_ref, dsk_ref, ex_ref, sn_ref, gkw_ref, gkb_ref, gn_ref,
                    o_ref, hs_ref, gs_ref):
    c = pl.program_id(1)
    q_len = C_CHUNK
    halo = xp_ref.shape[1]

    @pl.when(c == 0)
    def _():
        hs_ref[...] = jnp.zeros_like(hs_ref)
        gs_ref[...] = jnp.zeros_like(gs_ref)

    prev = xp_ref[0]
    cur = xc_ref[0]
    ext = jnp.concatenate([jnp.where(c > 0, prev, jnp.zeros_like(prev)), cur], axis=0)
    t_out = lax.broadcasted_iota(jnp.int32, (q_len, halo + q_len), 0)
    t_in = lax.broadcasted_iota(jnp.int32, (q_len, halo + q_len), 1) - halo
    acc = cb_ref[...] + cw_ref[C_CONV - 1:C_CONV, :] * cur.astype(F32)
    for j in range(C_CONV - 1):
        shift = (t_in == t_out - (C_CONV - 1 - j)).astype(BF16)
        acc = acc + cw_ref[j:j + 1, :] * jnp.dot(shift, ext, preferred_element_type=F32)
    xbc = jax.nn.silu(acc)
    xs = xbc[:, :C_INNER]
    gs_w = C_GROUPS * C_STATE
    bm = xbc[:, C_INNER:C_INNER + gs_w].astype(BF16)
    cm = xbc[:, C_INNER + gs_w:].astype(BF16)

    row = lax.broadcasted_iota(jnp.int32, (q_len, q_len), 0)
    col = lax.broadcasted_iota(jnp.int32, (q_len, q_len), 1)
    tri = col <= row

    lane = lax.broadcasted_iota(jnp.int32, (1, LANES), 1)
    dt = jnp.where(lane < C_HEADS, jax.nn.softplus(sdt_ref[0] + dtb_ref[...]), 0.0)
    dta = dt * -jnp.exp(alog_ref[...])
    acs = _dot_mask_lhs(tri.astype(BF16), dta)
    acs_t = acs.T
    chunk_dec = jnp.exp(acs[q_len - 1:q_len, :])
    expand = ex_ref[...]
    dt_x = _dot_mask_rhs(dt, expand)
    acs_x = _dot_mask_rhs(acs, expand)
    xd = xs * dt_x
    xd_b = xd.astype(BF16)
    xdd = xd * jnp.exp(acs_x[q_len - 1:q_len, :] - acs_x)
    low_half = lax.broadcasted_iota(jnp.int32, (q_len, LANES), 1) < C_HEAD_DIM

    y_pairs, y_offs = [], []
    tdims = (((1,), (1,)), ((), ()))
    for g in range(C_GROUPS):
        b_g = bm[:, g * C_STATE:(g + 1) * C_STATE]
        c_g = cm[:, g * C_STATE:(g + 1) * C_STATE]
        cb = lax.dot_general(c_g, b_g, tdims, preferred_element_type=F32)
        h0 = g * C_HPG
        grp = slice(h0 * C_HEAD_DIM, (h0 + C_HPG) * C_HEAD_DIM)
        y_offs.append(lax.dot_general(c_g, hs_ref[grp, :].astype(BF16), tdims, preferred_element_type=F32))
        for h in range(h0, h0 + C_HPG, 2):
            xp = xd_b[:, h * C_HEAD_DIM:(h + 2) * C_HEAD_DIM]
            halves = []
            for hh in (h, h + 1):
                seg = jnp.exp(jnp.where(tri, acs[:, hh:hh + 1] - acs_t[hh:hh + 1, :], -jnp.inf))
                halves.append(jnp.dot((cb * seg).astype(BF16), xp, preferred_element_type=F32))
            y_pairs.append(jnp.where(low_half, halves[0], halves[1]))
        upd = jnp.dot(xdd[:, grp].T.astype(BF16), b_g, preferred_element_type=F32)
        for hh in range(C_HPG):
            h = h0 + hh
            ps = slice(h * C_HEAD_DIM, (h + 1) * C_HEAD_DIM)
            us = slice(hh * C_HEAD_DIM, (hh + 1) * C_HEAD_DIM)
            hs_ref[ps, :] = hs_ref[ps, :] * chunk_dec[0:1, h:h + 1] + upd[us, :]
    y = (jnp.concatenate(y_pairs, axis=1) + jnp.concatenate(y_offs, axis=1) * jnp.exp(acs_x)
         + dsk_ref[...] * xs)
    y = y * jax.nn.silu(z_ref[0].astype(F32))
    o_ref[0, :, :C_INNER] = _rms(y, sn_ref[...]).astype(o_ref.dtype)

    same = (row // D_CHUNK) == (col // D_CHUNK)
    tri2 = tri & same
    la = jnp.dot(sg_ref[0].astype(BF16), gkw_ref[...], preferred_element_type=F32) + gkb_ref[...]
    la = jax.nn.log_sigmoid(la) / D_GATE_NORM
    gcs = _dot_mask_lhs(tri2.astype(BF16), la)
    first = lax.broadcasted_iota(jnp.int32, (q_len, 1), 0) < D_CHUNK
    r_all = r_ref[0]
    for h in range(D_HEADS):
        ks = slice(h * D_HK, (h + 1) * D_HK)
        vs = slice(h * D_HV, (h + 1) * D_HV)
        g_h = gcs[:, ks]
        g_end0 = g_h[D_CHUNK - 1:D_CHUNK, :]
        g_end1 = g_h[q_len - 1:q_len, :]
        q_h = q_ref[0, :, ks].astype(F32) * (D_HK ** -0.5)
        k_h = k_ref[0, :, ks].astype(F32)
        v_h = v_ref[0, :, vs]
        q_dec = (q_h * jnp.exp(g_h)).astype(BF16)
        k_inv = (k_h * jnp.exp(-g_h)).astype(BF16)
        k_end = k_h * jnp.exp(jnp.where(first, g_end0, g_end1) - g_h)
        ke0 = jnp.where(first, k_end, 0.0).astype(BF16)
        ke1 = jnp.where(first, 0.0, k_end).astype(BF16)
        attn = lax.dot_general(q_dec, k_inv, (((1,), (1,)), ((), ())), preferred_element_type=F32)
        attn = jnp.where(tri2, attn, 0.0).astype(BF16)
        o = jnp.dot(attn, v_h, preferred_element_type=F32)
        v_t = v_h.astype(F32).T.astype(BF16)
        st_rows = slice(h * D_HV, (h + 1) * D_HV)
        s0 = gs_ref[st_rows, :]
        s1 = s0 * jnp.exp(g_end0) + jnp.dot(v_t, ke0, preferred_element_type=F32)
        s2 = s1 * jnp.exp(g_end1) + jnp.dot(v_t, ke1, preferred_element_type=F32)
        gs_ref[st_rows, :] = s2
        tdims = (((1,), (1,)), ((), ()))
        o0 = lax.dot_general(q_dec, s0.astype(BF16), tdims, preferred_element_type=F32)
        o1 = lax.dot_general(q_dec, s1.astype(BF16), tdims, preferred_element_type=F32)
        o = o + jnp.where(first, o0, o1)
        o = _rms(o, gn_ref[...]) * jax.nn.silu(r_all[:, vs].astype(F32))
        o_ref[0, :, C_INNER + h * D_HV:C_INNER + (h + 1) * D_HV] = o.astype(o_ref.dtype)


def _ssd_gla(z, q, k, v, r, xbc, side_dt, side_g, g_lane, p):
    b, s, _ = z.shape
    q_len = C_CHUNK
    halo = 16
    chunk = lambda width: pl.BlockSpec((1, q_len, width), lambda i, n: (i, n, 0))
    full = lambda shape: pl.BlockSpec(shape, lambda i, n: (0,) * len(shape))
    pad_lanes = lambda vec: jnp.zeros((1, LANES), F32).at[0, :vec.shape[0]].set(vec.astype(F32))
    gkw = jnp.zeros((LANES, D_KEY), F32).at[g_lane:g_lane + D_GATE_RANK].set(p["gk_w"]).astype(BF16)
    expand = jnp.asarray(np.arange(C_INNER)[None, :] // C_HEAD_DIM == np.arange(LANES)[:, None], BF16)
    return pl.pallas_call(
        _ssd_gla_kernel,
        grid=(b, s // q_len),
        in_specs=[
            chunk(C_INNER), chunk(D_KEY), chunk(D_KEY), chunk(D_VAL), chunk(D_VAL), chunk(C_CONV_DIM),
            pl.BlockSpec((1, halo, C_CONV_DIM), lambda i, n: (i, jnp.maximum(n * (q_len // halo) - 1, 0), 0)),
            chunk(LANES), chunk(LANES),
            full((C_CONV, C_CONV_DIM)), full((1, C_CONV_DIM)),
            full((1, LANES)), full((1, LANES)), full((1, C_INNER)), full((LANES, C_INNER)),
            full((1, C_INNER)), full((LANES, D_KEY)), full((1, D_KEY)), full((1, D_HV)),
        ],
        out_specs=pl.BlockSpec((1, q_len, C_INNER + D_VAL), lambda i, n: (i, n, 0)),
        out_shape=jax.ShapeDtypeStruct((b, s, C_INNER + D_VAL), BF16),
        scratch_shapes=[pltpu.VMEM((C_INNER, C_STATE), F32),
                        pltpu.VMEM((D_VAL, D_HK), F32)],
        compiler_params=_cparams(("parallel", "arbitrary")),
        name="ssd_gla",
    )(z, q, k, v, r, xbc, xbc, side_dt, side_g,
      p["conv_w"].astype(F32), p["conv_b"].reshape(1, -1).astype(F32),
      pad_lanes(p["dt_bias"]), pad_lanes(p["a_log"]),
      jnp.repeat(p["d_skip"].astype(F32), C_HEAD_DIM).reshape(1, C_INNER), expand,
      p["ssd_norm"].reshape(1, -1).astype(F32), gkw, p["gk_b"].reshape(1, -1).astype(F32),
      p["gla_norm"].reshape(1, -1).astype(F32))


def _store_rows_tiled(ref, val):
    m, d = val.shape
    nc = d // LANES
    for c in range(nc):
        ref[pl.ds(c, m, stride=nc), :] = val[:, c * LANES:(c + 1) * LANES]


def _load_rows_tiled(ref, m, dtype=None):
    nc = ref.shape[0] // m
    parts = [ref[pl.ds(c, m, stride=nc), :] for c in range(nc)]
    if dtype is not None:
        parts = [p.astype(dtype) for p in parts]
    return jnp.concatenate(parts, axis=1)


def _pack_pairs(x):
    n = x.shape[1] // 2
    u = pltpu.bitcast(x.astype(BF16).astype(F32), jnp.uint32)
    return (u[:, :n] >> 16) | (u[:, n:] & jnp.uint32(0xFFFF0000))


def _unpack_pairs(u):
    lo = pltpu.bitcast(u << 16, F32).astype(BF16)
    hi = pltpu.bitcast(u & jnp.uint32(0xFFFF0000), F32).astype(BF16)
    return jnp.concatenate([lo, hi], axis=1)


RT_GATE, RT_EXPERT, RT_RANK = 0, 2, 4


def _route_block(lg, carry, earlier):
    m = lg.shape[0]
    lane = lax.broadcasted_iota(jnp.int32, (m, LANES), 1)
    lane_f = lane.astype(F32)
    none = float(LANES)
    neg = -jnp.inf
    first_max = lambda v, vmax: jnp.min(jnp.where(v == vmax, lane_f, none), axis=-1, keepdims=True)
    gl = jnp.where(lane < MOE_GROUPS, lg, neg)
    gmax = jnp.max(gl, axis=-1, keepdims=True)
    g_w = 1.0 / jnp.sum(jnp.exp(gl - gmax), axis=-1, keepdims=True)
    lo = MOE_GROUPS + first_max(gl, gmax) * MOE_EPG
    el = jnp.where((lane_f >= lo) & (lane_f < lo + MOE_EPG), lg, neg)
    emax = jnp.max(el, axis=-1, keepdims=True)
    esum = jnp.sum(jnp.exp(el - emax), axis=-1, keepdims=True)
    l0 = first_max(el, emax)
    el2 = jnp.where(lane_f == l0, neg, el)
    emax2 = jnp.max(el2, axis=-1, keepdims=True)
    l1 = first_max(el2, emax2)
    p0 = 1.0 / esum
    p1 = jnp.exp(emax2 - emax) / esum
    w0 = g_w * (p0 / (p0 + p1))
    w1 = g_w * (p1 / (p0 + p1))
    oh0 = lane_f == l0
    oh1 = lane_f == l1
    oh = (oh0 | oh1).astype(BF16)
    cum = jnp.dot(earlier, oh, preferred_element_type=F32) + carry
    rank0 = jnp.sum(jnp.where(oh0, cum, 0.0), axis=-1, keepdims=True)
    rank1 = jnp.sum(jnp.where(oh1, cum, 0.0), axis=-1, keepdims=True)
    carry = carry + jnp.sum(oh.astype(F32), axis=0, keepdims=True)
    rec = jnp.zeros((m, LANES), F32)
    for pos, val in ((RT_GATE, w0), (RT_GATE + 1, w1), (RT_EXPERT, l0 - MOE_GROUPS),
                     (RT_EXPERT + 1, l1 - MOE_GROUPS), (RT_RANK, rank0), (RT_RANK + 1, rank1)):
        rec = jnp.where(lane == pos, val, rec)
    return rec, carry


def _out_proj_kernel(n_parts, *refs):
    a_refs = refs[:n_parts]
    w_refs = refs[n_parts:2 * n_parts]
    h_ref, g_ref, wr_ref, br_ref, tri_ref, ho_ref, xt_ref, rt_ref, cnt_ref, carry_ref = refs[2 * n_parts:]

    @pl.when(pl.program_id(0) == 0)
    def _():
        carry_ref[...] = jnp.zeros_like(carry_ref)

    acc = h_ref[...]
    for a_ref, w_ref in zip(a_refs, w_refs):
        acc = acc + jnp.dot(a_ref[...], w_ref[...], preferred_element_type=F32)
    ho_ref[...] = acc
    xn = _rms(acc, g_ref[...])
    xt_ref[...] = xn.astype(xt_ref.dtype)
    x_hi, x_mid, _ = _split3(xn)
    wr = wr_ref[...]
    lg2 = jnp.dot(x_hi, wr, preferred_element_type=F32)
    lg = (lg2[:, :LANES] + lg2[:, LANES:] + jnp.dot(x_mid, wr[:, :LANES], preferred_element_type=F32)
          + br_ref[...])
    rec, carry = _route_block(lg, carry_ref[...], tri_ref[...])
    rt_ref[...] = rec
    carry_ref[...] = carry
    cnt_ref[0] = carry


def _out_proj(parts, w_parts, h, g, w_route, b_route, *, tm=512):
    t, d = h.shape
    tm = min(tm, t)
    nc = d // LANES
    row = lambda i: (i, 0)
    const = lambda i: (0, 0)
    in_specs = [pl.BlockSpec((tm, a.shape[1]), row) for a in parts]
    in_specs += [pl.BlockSpec(w.shape, const) for w in w_parts]
    in_specs += [pl.BlockSpec((tm, d), row), pl.BlockSpec((1, d), const),
                 pl.BlockSpec((d, 2 * LANES), const), pl.BlockSpec((1, LANES), const),
                 pl.BlockSpec((tm, tm), const)]
    earlier = jnp.asarray(np.tril(np.ones((tm, tm), np.float32), -1), BF16)
    return pl.pallas_call(
        functools.partial(_out_proj_kernel, len(parts)),
        grid=(t // tm,),
        in_specs=in_specs,
        out_specs=[pl.BlockSpec((tm, d), row), pl.BlockSpec((tm, d), row),
                   pl.BlockSpec((tm, LANES), row), pl.BlockSpec((1, 1, LANES), lambda i: (i, 0, 0))],
        out_shape=[jax.ShapeDtypeStruct((t, d), F32), jax.ShapeDtypeStruct((t, d), BF16),
                   jax.ShapeDtypeStruct((t, LANES), F32), jax.ShapeDtypeStruct((t // tm, 1, LANES), F32)],
        scratch_shapes=[pltpu.VMEM((1, LANES), F32)],
        compiler_params=_cparams(("arbitrary",)),
        name="out_proj",
    )(*parts, *w_parts, h, g.reshape(1, d), w_route, b_route, earlier)


LONG_RUN = 128


def _piece_sizes(max_rows, lo=1, hi=None):
    sizes = [1 << b for b in reversed(range(max_rows.bit_length()))]
    return [n for n in sizes if n >= lo and (hi is None or n < hi)]


def _for_each_piece(run, sizes, body):
    off = run // (2 * sizes[0]) * (2 * sizes[0])
    for n in sizes:
        hit = (run & n) != 0
        pl.when(hit)(functools.partial(body, off, n))
        off = off + jnp.where(hit, n, 0)


def _for_each_run(len_ref, tile, long_ref, max_rows, make_body):
    def sweep(sizes):
        def per_expert(e, first):
            run = len_ref[tile * MOE_EXPERTS + e]
            _for_each_piece(run, sizes, make_body(e, first))
            return first + run

        lax.fori_loop(0, MOE_EXPERTS, per_expert, jnp.int32(0))

    sweep(_piece_sizes(max_rows, hi=LONG_RUN))
    pl.when(long_ref[tile] != 0)(functools.partial(sweep, _piece_sizes(max_rows, lo=LONG_RUN)))


def _zero_fill_rows(rows_ref, z_ref, zsem, lo_ref, hi_ref, n_rows, nc):
    zb = z_ref.shape[0] // nc
    z_ref[...] = jnp.zeros_like(z_ref)
    assert MOE_TILE // 2 <= zb and MOE_TILE % zb == 0

    def piece(row0, n):
        return pltpu.make_async_copy(z_ref.at[pl.ds(0, n * nc)], rows_ref.at[pl.ds(row0 * nc, n * nc)], zsem)

    def sweep(issue):
        def per_expert(e, carry):
            lo = lo_ref[e]

            def one(off, n):
                piece(lo + off, n).start() if issue else piece(lo + off, n).wait()

            _for_each_piece(hi_ref[e] - lo, _piece_sizes(MOE_TILE // 2), one)
            return carry

        def per_block(i, carry):
            piece(i * zb, zb).start() if issue else piece(i * zb, zb).wait()
            return carry

        lax.fori_loop(0, MOE_EXPERTS, per_expert, 0)
        lax.fori_loop(hi_ref[MOE_EXPERTS - 1] // zb, n_rows // zb, per_block, 0)

    sweep(True)
    sweep(False)


def _tile_positions(rt, pos_base):
    lane_f = lax.broadcasted_iota(jnp.int32, rt.shape, 1).astype(F32)
    out = []
    for k in range(MOE_TOPK):
        e_lane = rt[:, RT_EXPERT + k:RT_EXPERT + k + 1] + MOE_GROUPS
        out.append(rt[:, RT_RANK + k:RT_RANK + k + 1]
                   + jnp.sum(jnp.where(lane_f == e_lane, pos_base, 0.0), axis=-1, keepdims=True))
    return out


def _moe_dispatch_kernel(dst_ref, len_ref, long_ref, lo_ref, hi_ref, x_ref, rt_ref, pb_ref, rows_ref,
                         sbuf, z_ref, sem, zsem, *, tm, nc, n_steps, n_rows):
    i = pl.program_id(0)
    rows = MOE_TOPK * tm
    slot = lax.rem(i, 2)

    def wait_slot(s):
        pltpu.make_async_copy(sbuf.at[s], rows_ref.at[pl.ds(0, rows * nc)], sem.at[s]).wait()

    @pl.when(i == 0)
    def _():
        _zero_fill_rows(rows_ref, z_ref, zsem, lo_ref, hi_ref, n_rows, nc)

    @pl.when(i >= 2)
    def _():
        wait_slot(slot)

    pos = _tile_positions(rt_ref[...], pb_ref[0])
    lane = lax.broadcasted_iota(jnp.int32, (tm, LANES), 1)
    pos_t = jnp.where(lane == 0, pos[0], jnp.where(lane == 1, pos[1], 0.0)).T
    p_iota = lax.broadcasted_iota(jnp.int32, (rows, tm), 0).astype(F32)
    place = ((p_iota == pos_t[0:1, :]) | (p_iota == pos_t[1:2, :])).astype(BF16)
    _store_rows_tiled(sbuf.at[slot], _pack_pairs(jnp.dot(place, x_ref[...], preferred_element_type=F32)))

    def sender(e, src):
        dst = dst_ref[i * MOE_EXPERTS + e]

        def send(off, n):
            pltpu.make_async_copy(sbuf.at[slot, pl.ds((src + off) * nc, n * nc)],
                                  rows_ref.at[pl.ds((dst + off) * nc, n * nc)], sem.at[slot]).start()

        return send

    _for_each_run(len_ref, i, long_ref, rows, sender)

    @pl.when(i == n_steps - 1)
    def _():
        wait_slot(slot)
        if n_steps > 1:
            wait_slot(1 - slot)


def _moe_dispatch(x, route, pos_base, seg_dst, seg_len, seg_long, pad_lo, pad_hi, n_rows, *, tm):
    t, d = x.shape
    nc = d // (2 * LANES)
    n_steps = t // tm
    row = lambda i, *_: (i, 0)
    return pl.pallas_call(
        functools.partial(_moe_dispatch_kernel, tm=tm, nc=nc, n_steps=n_steps, n_rows=n_rows),
        grid_spec=pltpu.PrefetchScalarGridSpec(
            num_scalar_prefetch=5, grid=(n_steps,),
            in_specs=[pl.BlockSpec((tm, d), row), pl.BlockSpec((tm, LANES), row),
                      pl.BlockSpec((1, 1, LANES), lambda i, *_: (i, 0, 0))],
            out_specs=pl.BlockSpec(memory_space=pl.ANY),
            scratch_shapes=[pltpu.VMEM((2, MOE_TOPK * tm * nc, LANES), jnp.uint32),
                            pltpu.VMEM((MOE_TILE // 2 * nc, LANES), jnp.uint32),
                            pltpu.SemaphoreType.DMA((2,)), pltpu.SemaphoreType.DMA(())]),
        out_shape=jax.ShapeDtypeStruct((n_rows * nc, LANES), jnp.uint32),
        compiler_params=pltpu.CompilerParams(dimension_semantics=("arbitrary",), has_side_effects=True,
                                             vmem_limit_bytes=VMEM_LIMIT),
        name="moe_dispatch",
    )(seg_dst, seg_len, seg_long, pad_lo, pad_hi, x, route, pos_base)


def _moe_kernel(te_ref, nu_ref, x_ref, wg_ref, wu_ref, wd_ref, y_ref, wg_b, wu_b, wd_b):
    i = pl.program_id(0)
    live = i < nu_ref[0]

    @pl.when(i >= nu_ref[0])
    def _():
        y_ref[...] = jnp.zeros_like(y_ref)

    @pl.when(live & ((i == 0) | (te_ref[i] != te_ref[jnp.maximum(i - 1, 0)])))
    def _():
        wg_b[...] = wg_ref[...].astype(BF16)
        wu_b[...] = wu_ref[...].astype(BF16)
        wd_b[...] = wd_ref[...].astype(BF16)

    @pl.when(live)
    def _():
        x = _unpack_pairs(_load_rows_tiled(x_ref, MOE_TILE))
        gate = jnp.dot(x, wg_b[...], preferred_element_type=F32)
        up = jnp.dot(x, wu_b[...], preferred_element_type=F32)
        act = (jax.nn.silu(gate) * up).astype(BF16)
        y = jnp.dot(act, wd_b[...], preferred_element_type=F32)
        _store_rows_tiled(y_ref, _pack_pairs(y))


def _moe_experts(x_rows, tile_expert, n_used, w_gate, w_up, w_down, layer):
    d, ff = w_gate.shape[-2:]
    nc = d // (2 * LANES)
    n_tiles = x_rows.shape[0] // (MOE_TILE * nc)
    live = lambda i, nu: jnp.minimum(i, nu[0] - 1)
    w_spec = lambda a, b: pl.BlockSpec((None, None, a, b), lambda i, te, nu: (layer, te[live(i, nu)], 0, 0))
    return pl.pallas_call(
        _moe_kernel,
        grid_spec=pltpu.PrefetchScalarGridSpec(
            num_scalar_prefetch=2, grid=(n_tiles,),
            in_specs=[pl.BlockSpec((MOE_TILE * nc, LANES), lambda i, te, nu: (live(i, nu), 0)),
                      w_spec(d, ff), w_spec(d, ff), w_spec(ff, d)],
            out_specs=pl.BlockSpec((MOE_TILE * nc, LANES), lambda i, te, nu: (i, 0)),
            scratch_shapes=[pltpu.VMEM((d, ff), BF16), pltpu.VMEM((d, ff), BF16), pltpu.VMEM((ff, d), BF16)]),
        out_shape=jax.ShapeDtypeStruct(x_rows.shape, x_rows.dtype),
        compiler_params=_cparams(("arbitrary",)),
        name="moe_experts",
    )(tile_expert, n_used, x_rows, w_gate, w_up, w_down)


def _moe_combine_kernel(src_ref, len_ref, long_ref, h_ref, rt_ref, pb_ref, g_ref, y_hbm, o_ref, ybuf, sem,
                        *, tm, nc, final):
    i = pl.program_id(0)
    n_steps = pl.num_programs(0)
    rows = MOE_TOPK * tm

    def fetch(tile, slot):
        def receiver(e, dst):
            src = src_ref[tile * MOE_EXPERTS + e]

            def recv(off, n):
                pltpu.make_async_copy(y_hbm.at[pl.ds((src + off) * nc, n * nc)],
                                      ybuf.at[slot, pl.ds((dst + off) * nc, n * nc)], sem.at[slot]).start()

            return recv

        _for_each_run(len_ref, tile, long_ref, rows, receiver)

    slot = lax.rem(i, 2)

    @pl.when(i == 0)
    def _():
        fetch(0, 0)

    @pl.when(i + 1 < n_steps)
    def _():
        fetch(i + 1, 1 - slot)

    pltpu.make_async_copy(y_hbm.at[pl.ds(0, rows * nc)], ybuf.at[slot], sem.at[slot]).wait()
    y = _unpack_pairs(_load_rows_tiled(ybuf.at[slot], rows))
    rt = rt_ref[...]
    pos_f = lax.broadcasted_iota(jnp.int32, (tm, rows), 1).astype(F32)
    pick = jnp.zeros((tm, rows), F32)
    for k, pos in enumerate(_tile_positions(rt, pb_ref[0])):
        pick = jnp.where(pos_f == pos, rt[:, RT_GATE + k:RT_GATE + k + 1], pick)
    out = h_ref[...] + jnp.dot(pick.astype(BF16), y, preferred_element_type=F32)
    o_ref[...] = _rms(out, g_ref[...]) if final else out


def _moe_combine_rows(h, y_rows, route, seg_src, seg_len, seg_long, pos_base, g, *, tm, final):
    t, d = h.shape
    nc = d // (2 * LANES)
    row = lambda i, *_: (i, 0)
    return pl.pallas_call(
        functools.partial(_moe_combine_kernel, tm=tm, nc=nc, final=final),
        grid_spec=pltpu.PrefetchScalarGridSpec(
            num_scalar_prefetch=3, grid=(t // tm,),
            in_specs=[pl.BlockSpec((tm, d), row), pl.BlockSpec((tm, LANES), row),
                      pl.BlockSpec((1, 1, LANES), lambda i, *_: (i, 0, 0)),
                      pl.BlockSpec((1, d), lambda i, *_: (0, 0)),
                      pl.BlockSpec(memory_space=pl.ANY)],
            out_specs=pl.BlockSpec((tm, d), row),
            scratch_shapes=[pltpu.VMEM((2, MOE_TOPK * tm * nc, LANES), y_rows.dtype),
                            pltpu.SemaphoreType.DMA((2,))]),
        out_shape=jax.ShapeDtypeStruct((t, d), F32),
        compiler_params=_cparams(("arbitrary",)),
        name="moe_combine",
    )(seg_src, seg_len, seg_long, h, route, pos_base, g.reshape(1, d), y_rows)


def _moe(h, x, route, tile_counts, w_gate, w_up, w_down, layer, g, *, final):
    t = route.shape[0]
    n_tt = tile_counts.shape[0]
    tm = t // n_tt
    after = tile_counts[:, 0, MOE_GROUPS:MOE_GROUPS + MOE_EXPERTS].astype(jnp.int32)
    before = jnp.concatenate([jnp.zeros((1, MOE_EXPERTS), jnp.int32), after[:-1]], axis=0)
    cnt = after[-1]
    padded = (cnt + MOE_TILE - 1) // MOE_TILE * MOE_TILE
    pad_ends = jnp.cumsum(padded)
    starts = (pad_ends - padded).astype(jnp.int32)
    n_tiles = (t * MOE_TOPK + MOE_EXPERTS * (MOE_TILE - 1)) // MOE_TILE
    tile_start = jnp.arange(n_tiles, dtype=jnp.int32) * MOE_TILE
    tile_expert = jnp.minimum(jnp.sum(tile_start[:, None] >= pad_ends[None, :], axis=1),
                              MOE_EXPERTS - 1).astype(jnp.int32)
    n_used = (pad_ends[-1] // MOE_TILE).astype(jnp.int32).reshape(1)
    n_rows = n_tiles * MOE_TILE
    seg_len = (after - before).reshape(-1)
    seg_off = jnp.cumsum(after - before, axis=1) - (after - before)
    seg_row = (starts[None, :] + before).reshape(-1)
    pos_base = jnp.zeros((n_tt, 1, LANES), F32).at[:, 0, MOE_GROUPS:MOE_GROUPS + MOE_EXPERTS].set(
        (seg_off - before).astype(F32))
    seg_long = jnp.any(after - before >= LONG_RUN, axis=1).astype(jnp.int32)
    x_rows = _moe_dispatch(x, route, pos_base, seg_row, seg_len, seg_long, starts + cnt,
                           pad_ends.astype(jnp.int32), n_rows, tm=tm)
    y_rows = _moe_experts(x_rows, tile_expert, n_used, w_gate, w_up, w_down, layer)
    return _moe_combine_rows(h, y_rows, route, seg_row, seg_len, seg_long, pos_base, g, tm=tm, final=final)


def _router_weights(w_group, b_group, w_router, b_router):
    d = w_group.shape[0]
    w = jnp.zeros((d, LANES), F32)
    w = w.at[:, :MOE_GROUPS].set(w_group).at[:, MOE_GROUPS:MOE_GROUPS + MOE_EXPERTS].set(w_router)
    b = jnp.zeros((1, LANES), F32)
    b = b.at[0, :MOE_GROUPS].set(b_group).at[0, MOE_GROUPS:MOE_GROUPS + MOE_EXPERTS].set(b_router)
    w_hi = w.astype(BF16)
    w_mid = (w - w_hi.astype(F32)).astype(BF16)
    return jnp.concatenate([w_hi, w_mid], axis=1), b


def kernel(x, norm_mix, norm_moe, norm_final, even_w_in, even_sinks, even_forget_bias, even_w_out,
           odd_w_in, odd_conv_w, odd_conv_b, odd_dt_bias, odd_a_log, odd_d_skip, odd_ssd_norm,
           odd_gk_w, odd_gk_b, odd_gla_norm, odd_w_out, moe_w_group, moe_b_group, moe_w_router,
           moe_b_router, moe_w_gate, moe_w_up, moe_w_down):
    b, s, d = x.shape
    t = b * s
    depth = norm_mix.shape[0]
    h = x.reshape(t, d)
    for layer in range(depth):
        i = layer // 2
        if layer % 2 == 0:
            w = even_w_in[i]
            n_ab = (A_Q_HEADS + 2 * A_KV_HEADS + 3 * B_HEADS) * HEAD_DIM
            w_main = w[:, :n_ab].astype(BF16)
            w_aux = jnp.zeros((d, LANES), F32).at[:, :B_HEADS].set(w[:, n_ab:]).astype(BF16)
            proj, f_aux = _norm_proj(h, norm_mix[layer], [w_main, w_aux],
                                     ((0, 0, n_ab, BF16), (1, 0, LANES, F32)), tm=512)
            proj = proj.reshape(b, s, -1)
            out_a = _swa(proj, even_sinks[i])
            c, ct = _fox_gate(f_aux.reshape(b, s, LANES), even_forget_bias[i])
            out_b = _fox(proj, c, ct)
            n_ha = A_Q_HEADS * HEAD_DIM
            w_out = even_w_out[i].astype(BF16)
            parts = [out_a.reshape(t, -1), out_b.reshape(t, -1)]
            w_parts = [w_out[:n_ha], w_out[n_ha:]]
        else:
            w = odd_w_in[i]
            o_z, o_xbc = 0, C_INNER
            o_dt = o_xbc + C_CONV_DIM
            o_q = o_dt + C_HEADS
            o_k = o_q + D_KEY
            o_v = o_k + D_KEY
            o_g = o_v + D_VAL
            o_r = o_g + D_GATE_RANK
            n_b = w.shape[1] - o_dt
            w_a = w[:, :o_dt].astype(BF16)
            w_b = jnp.pad(w[:, o_dt:].astype(BF16), ((0, 0), (0, -n_b % LANES)))
            g_win = (o_g - o_dt) // LANES * LANES
            plan = ((0, o_z, C_INNER, BF16), (1, o_q - o_dt, D_KEY, BF16), (1, o_k - o_dt, D_KEY, BF16),
                    (1, o_v - o_dt, D_VAL, BF16), (1, o_r - o_dt, D_VAL, BF16), (0, o_xbc, C_CONV_DIM, BF16),
                    (1, 0, LANES, F32), (1, g_win, LANES, F32))
            outs = _norm_proj(h, norm_mix[layer], [w_a, w_b], plan, tm=256)
            params = dict(conv_w=odd_conv_w[i], conv_b=odd_conv_b[i], dt_bias=odd_dt_bias[i], a_log=odd_a_log[i],
                          d_skip=odd_d_skip[i], ssd_norm=odd_ssd_norm[i], gk_w=odd_gk_w[i], gk_b=odd_gk_b[i],
                          gla_norm=odd_gla_norm[i])
            mixed = _ssd_gla(*[o.reshape(b, s, -1) for o in outs], o_g - o_dt - g_win, params)
            parts = [mixed.reshape(t, -1)]
            w_parts = [odd_w_out[i].astype(BF16)]
        w_route, b_route = _router_weights(moe_w_group[layer], moe_b_group[layer],
                                           moe_w_router[layer], moe_b_router[layer])
        h, x_tiled, route, tile_counts = _out_proj(parts, w_parts, h, norm_moe[layer], w_route, b_route)
        h = _moe(h, x_tiled, route, tile_counts, moe_w_gate, moe_w_up, moe_w_down, layer, norm_final,
                 final=layer == depth - 1)
    out = h
    return out.reshape(b, s, d)
```

```python
import functools

import numpy as np
import jax
import jax.numpy as jnp
from jax import lax
from jax.experimental import pallas as pl
from jax.experimental.pallas import tpu as pltpu

F32 = jnp.float32
BF16 = jnp.bfloat16

RMS_EPS = 1e-6
HEAD_DIM = 64
A_Q_HEADS = 8
A_KV_HEADS = 2
A_GROUP = A_Q_HEADS // A_KV_HEADS
A_WINDOW = 128
B_HEADS = 8
C_HEADS = 16
C_HEAD_DIM = 64
C_INNER = C_HEADS * C_HEAD_DIM
C_GROUPS = 2
C_HPG = C_HEADS // C_GROUPS
C_STATE = 128
C_CONV = 4
C_CHUNK = 128
C_CONV_DIM = C_INNER + 2 * C_GROUPS * C_STATE
D_HEADS = 4
D_HK = 128
D_HV = 256
D_KEY = D_HEADS * D_HK
D_VAL = D_HEADS * D_HV
D_GATE_RANK = 16
D_GATE_NORM = 16.0
D_CHUNK = 64
MOE_GROUPS = 4
MOE_EPG = 8
MOE_EXPERTS = MOE_GROUPS * MOE_EPG
MOE_TOPK = 2

LANES = 128
VMEM_LIMIT = 48 * 1024 * 1024
MOE_TILE = 512


def _cparams(sem):
    return pltpu.CompilerParams(dimension_semantics=sem, vmem_limit_bytes=VMEM_LIMIT)


def _rms(x, g):
    ms = jnp.mean(x * x, axis=-1, keepdims=True)
    return x * lax.rsqrt(ms + RMS_EPS) * g


def _norm_proj_kernel(n_w, plan, *refs):
    x_ref, g_ref = refs[:2]
    w_refs = refs[2:2 + n_w]
    o_refs = refs[2 + n_w:2 + n_w + len(plan)]
    res_refs = refs[2 + n_w + len(plan):]
    xn = _rms(x_ref[...], g_ref[...]).astype(BF16)
    for w_ref, res_ref in zip(w_refs, res_refs):
        res_ref[...] = jnp.dot(xn, w_ref[...], preferred_element_type=F32)
    for o_ref, (wi, start, width, _) in zip(o_refs, plan):
        o_ref[...] = res_refs[wi][:, start:start + width].astype(o_ref.dtype)


def _norm_proj(x, g, weights, plan, *, tm):
    t, d = x.shape
    tm = min(tm, t)
    row = lambda i: (i, 0)
    const = lambda i: (0, 0)
    in_specs = [pl.BlockSpec((tm, d), row), pl.BlockSpec((1, d), const)]
    in_specs += [pl.BlockSpec(w.shape, const, pipeline_mode=pl.Buffered(1)) for w in weights]
    return pl.pallas_call(
        functools.partial(_norm_proj_kernel, len(weights), plan),
        grid=(t // tm,),
        in_specs=in_specs,
        out_specs=[pl.BlockSpec((tm, width), row) for _, _, width, _ in plan],
        out_shape=[jax.ShapeDtypeStruct((t, width), dtype) for _, _, width, dtype in plan],
        scratch_shapes=[pltpu.VMEM((tm, w.shape[1]), F32) for w in weights],
        compiler_params=_cparams(("parallel",)),
        name="norm_proj",
    )(x, g.reshape(1, d), *weights)


SWA_QBLOCKS = 4


def _swa_kernel(sink_ref, slope_ref, q_ref, kp_ref, kc_ref, vp_ref, vc_ref, o_ref):
    n = pl.program_id(1)
    blk = A_WINDOW
    wide = A_GROUP * blk
    key = lax.broadcasted_iota(jnp.int32, (2 * blk, wide), 0)
    qry = lax.broadcasted_iota(jnp.int32, (2 * blk, wide), 1) % blk
    dist = blk + qry - key
    in_window = (dist >= 0) & (dist < A_WINDOW)
    distf = dist.astype(F32)
    nt = (((1,), (1,)), ((), ()))
    k_all = jnp.concatenate([kp_ref[0], kc_ref[0]], axis=0)
    v_all = jnp.concatenate([vp_ref[0], vc_ref[0]], axis=1)
    units = [(j, kh) for j in range(SWA_QBLOCKS) for kh in range(A_KV_HEADS)]
    scores = []
    for j, kh in units:
        k = k_all[j * blk:(j + 2) * blk, kh * HEAD_DIM:(kh + 1) * HEAD_DIM]
        q = jnp.concatenate([q_ref[0, j * blk:(j + 1) * blk,
                                   (kh * A_GROUP + g) * HEAD_DIM:(kh * A_GROUP + g + 1) * HEAD_DIM]
                             for g in range(A_GROUP)], axis=0)
        scores.append(lax.dot_general(k, q, nt, preferred_element_type=F32))
    probs = []
    for (j, kh), s in zip(units, scores):
        valid = in_window & ((key >= blk) | (n * SWA_QBLOCKS + j > 0))
        s = s * (HEAD_DIM ** -0.5) - slope_ref[kh:kh + 1, :] * distf
        s = jnp.where(valid, s, -jnp.inf)
        sink = sink_ref[kh:kh + 1, :]
        m = jnp.maximum(jnp.max(s, axis=0, keepdims=True), sink)
        p = jnp.exp(s - m)
        probs.append((p.astype(BF16), jnp.sum(p, axis=0, keepdims=True) + jnp.exp(sink - m)))
    for j in range(SWA_QBLOCKS):
        outs = []
        for kh in range(A_KV_HEADS):
            p, denom = probs[j * A_KV_HEADS + kh]
            v_t = v_all[kh * HEAD_DIM:(kh + 1) * HEAD_DIM, j * blk:(j + 2) * blk]
            o_t = jnp.dot(v_t, p, preferred_element_type=F32) / denom
            outs += [o_t[:, g * blk:(g + 1) * blk] for g in range(A_GROUP)]
        o_ref[0, j * blk:(j + 1) * blk, :] = jnp.concatenate(outs, axis=0).T.astype(o_ref.dtype)


def _swa(proj, sinks):
    b, s, _ = proj.shape
    blk = A_WINDOW
    qw = A_Q_HEADS * HEAD_DIM
    kw = A_KV_HEADS * HEAD_DIM
    k_blk = qw // kw
    v_t = proj[:, :, qw + kw:qw + 2 * kw].transpose(0, 2, 1)
    per_lane = lambda vec: jnp.repeat(vec.astype(F32), blk).reshape(A_KV_HEADS, A_GROUP * blk)
    slopes = jnp.asarray(2.0 ** (-8.0 * np.arange(1, A_Q_HEADS + 1) / A_Q_HEADS), F32)
    tq = SWA_QBLOCKS * blk
    prev = lambda n: jnp.maximum(n * SWA_QBLOCKS - 1, 0)
    full = pl.BlockSpec((A_KV_HEADS, A_GROUP * blk), lambda i, n: (0, 0))
    return pl.pallas_call(
        _swa_kernel,
        grid=(b, s // tq),
        in_specs=[
            full, full,
            pl.BlockSpec((1, tq, qw), lambda i, n: (i, n, 0)),
            pl.BlockSpec((1, blk, kw), lambda i, n: (i, prev(n), k_blk)),
            pl.BlockSpec((1, tq, kw), lambda i, n: (i, n, k_blk)),
            pl.BlockSpec((1, kw, blk), lambda i, n: (i, 0, prev(n))),
            pl.BlockSpec((1, kw, tq), lambda i, n: (i, 0, n)),
        ],
        out_specs=pl.BlockSpec((1, tq, qw), lambda i, n: (i, n, 0)),
        out_shape=jax.ShapeDtypeStruct((b, s, qw), BF16),
        compiler_params=_cparams(("parallel", "parallel")),
        name="swa",
    )(per_lane(sinks), per_lane(slopes), proj, proj, proj, v_t, v_t)


def _tril(n, dtype=F32):
    r = lax.broadcasted_iota(jnp.int32, (n, n), 0)
    c = lax.broadcasted_iota(jnp.int32, (n, n), 1)
    return (c <= r).astype(dtype)


def _split3(x):
    hi = x.astype(BF16)
    r = x - hi.astype(F32)
    mid = r.astype(BF16)
    return hi, mid, (r - mid.astype(F32)).astype(BF16)


def _dot_mask_lhs(mask, x):
    return sum(jnp.dot(mask, part, preferred_element_type=F32) for part in _split3(x))


def _dot_mask_rhs(x, mask):
    return sum(jnp.dot(part, mask, preferred_element_type=F32) for part in _split3(x))


def _fox_gate_kernel(f_ref, b_ref, c_ref, ct_ref):
    tri = _tril(LANES, BF16)
    carry = jnp.zeros((1, LANES), F32)
    for n in range(f_ref.shape[1] // LANES):
        rows = slice(n * LANES, (n + 1) * LANES)
        lf = jax.nn.log_sigmoid(f_ref[0, rows, :] + b_ref[...])
        cs = _dot_mask_lhs(tri, lf) + carry
        carry = cs[LANES - 1:LANES, :]
        c_ref[0, rows, :] = cs
        ct_ref[0, n] = cs.T[:B_HEADS, :]


def _fox_gate(f_aux, bias):
    b, s, _ = f_aux.shape
    nb = s // LANES
    bias_p = jnp.zeros((1, LANES), F32).at[0, :B_HEADS].set(bias.astype(F32))
    return pl.pallas_call(
        _fox_gate_kernel,
        grid=(b,),
        in_specs=[pl.BlockSpec((1, s, LANES), lambda i: (i, 0, 0)),
                  pl.BlockSpec((1, LANES), lambda i: (0, 0))],
        out_specs=[pl.BlockSpec((1, s, LANES), lambda i: (i, 0, 0)),
                   pl.BlockSpec((1, nb, B_HEADS, LANES), lambda i: (i, 0, 0, 0))],
        out_shape=[jax.ShapeDtypeStruct((b, s, LANES), F32),
                   jax.ShapeDtypeStruct((b, nb, B_HEADS, LANES), F32)],
        compiler_params=_cparams(("parallel",)),
        name="fox_gate",
    )(f_aux, bias_p)


def _fox_kernel(q0_ref, q1_ref, k0_ref, k1_ref, vt_ref, c_ref, ctq_ref, o_ref, *, tq, heads_per_step):
    qi = pl.program_id(1)
    sub = tq // LANES
    key = lax.broadcasted_iota(jnp.int32, (tq, tq), 0)
    qry = lax.broadcasted_iota(jnp.int32, (tq, tq), 1)
    causal = key <= qry
    nt = (((1,), (1,)), ((), ()))
    half = B_HEADS // 2
    q_refs, k_refs = (q0_ref, q1_ref), (k0_ref, k1_ref)
    outs = []
    for h0 in range(0, B_HEADS, heads_per_step):
        heads = list(range(h0, h0 + heads_per_step))
        hsl = [slice(h * HEAD_DIM, (h + 1) * HEAD_DIM) for h in heads]
        lsl = [slice((h % half) * HEAD_DIM, (h % half + 1) * HEAD_DIM) for h in heads]
        qs = [q_refs[h // half][0, :, ls] * (HEAD_DIM ** -0.5)
              for h, ls in zip(heads, lsl)]
        cqs = [jnp.concatenate([ctq_ref[0, u, h:h + 1, :] for u in range(sub)], axis=1) for h in heads]

        def step(j, carry, masked, heads=heads, hsl=hsl, lsl=lsl, qs=qs, cqs=cqs):
            start = pl.multiple_of(j * tq, tq)
            sts = [lax.dot_general(k_refs[h // half][0, pl.ds(start, tq), ls], q, nt,
                                   preferred_element_type=F32)
                   for h, ls, q in zip(heads, lsl, qs)]
            ps, stats = [], []
            for idx, h in enumerate(heads):
                m, l, _ = carry[3 * idx:3 * idx + 3]
                ck = c_ref[0, pl.ds(start, tq), h:h + 1]
                st = (sts[idx] - ck) + cqs[idx]
                if masked:
                    st = jnp.where(causal, st, -jnp.inf)
                m_new = jnp.maximum(m, jnp.max(st, axis=0, keepdims=True))
                alpha = jnp.exp(m - m_new)
                p = jnp.exp(st - m_new)
                stats.append((m_new, alpha, alpha * l + jnp.sum(p, axis=0, keepdims=True)))
                ps.append(p.astype(BF16))
            new = []
            for idx in range(len(heads)):
                m_new, alpha, l = stats[idx]
                pv = jnp.dot(vt_ref[0, j, hsl[idx], :], ps[idx], preferred_element_type=F32)
                new += [m_new, l, alpha * carry[3 * idx + 2] + pv]
            return tuple(new)

        init = (jnp.full((1, tq), -jnp.inf, F32), jnp.zeros((1, tq), F32),
                jnp.zeros((HEAD_DIM, tq), F32)) * heads_per_step
        carry = lax.fori_loop(0, qi, functools.partial(step, masked=False), init)
        carry = step(qi, carry, True)
        for idx in range(heads_per_step):
            outs.append(carry[3 * idx + 2] / carry[3 * idx + 1])
    o_ref[0] = jnp.concatenate(outs, axis=0).T.astype(o_ref.dtype)


def _fox(proj, c, ct, *, tq=256, heads_per_step=8):
    b, s, _ = proj.shape
    w = B_HEADS * HEAD_DIM
    nk = s // tq
    sub = tq // LANES
    hw = w // 2
    base = (A_Q_HEADS + 2 * A_KV_HEADS) * HEAD_DIM
    qb, kb = base // hw, (base + w) // hw
    v_t = proj[:, :, base + 2 * w:base + 3 * w].reshape(b, nk, tq, w).transpose(0, 1, 3, 2)
    return pl.pallas_call(
        functools.partial(_fox_kernel, tq=tq, heads_per_step=heads_per_step),
        grid=(b, s // tq),
        in_specs=[
            pl.BlockSpec((1, tq, hw), lambda i, n: (i, n, qb)),
            pl.BlockSpec((1, tq, hw), lambda i, n: (i, n, qb + 1)),
            pl.BlockSpec((1, s, hw), lambda i, n: (i, 0, kb)),
            pl.BlockSpec((1, s, hw), lambda i, n: (i, 0, kb + 1)),
            pl.BlockSpec((1, nk, w, tq), lambda i, n: (i, 0, 0, 0)),
            pl.BlockSpec((1, s, LANES), lambda i, n: (i, 0, 0)),
            pl.BlockSpec((1, sub, B_HEADS, LANES), lambda i, n: (i, n, 0, 0)),
        ],
        out_specs=pl.BlockSpec((1, tq, w), lambda i, n: (i, n, 0)),
        out_shape=jax.ShapeDtypeStruct((b, s, w), BF16),
        compiler_params=_cparams(("parallel", "parallel")),
        name="fox",
    )(proj, proj, proj, proj, v_t, c, ct)


def _ssd_gla_kernel(z_ref, q_ref, k_ref, v_ref, r_ref, xc_ref, xp_ref, sdt_ref, sg_ref,
                    cw_ref, cb_ref, dtb_ref, alog_ref, dsk_ref, ex_ref, sn_ref, gkw_ref, gkb_ref, gn_ref,
                    o_ref, hs_ref, gs_ref):
    c = pl.program_id(1)
    q_len = C_CHUNK
    halo = xp_ref.shape[1]

    @pl.when(c == 0)
    def _():
        hs_ref[...] = jnp.zeros_like(hs_ref)
        gs_ref[...] = jnp.zeros_like(gs_ref)

    prev = xp_ref[0]
    cur = xc_ref[0]
    ext = jnp.concatenate([jnp.where(c > 0, prev, jnp.zeros_like(prev)), cur], axis=0)
    t_out = lax.broadcasted_iota(jnp.int32, (q_len, halo + q_len), 0)
    t_in = lax.broadcasted_iota(jnp.int32, (q_len, halo + q_len), 1) - halo
    acc = cb_ref[...] + cw_ref[C_CONV - 1:C_CONV, :] * cur.astype(F32)
    for j in range(C_CONV - 1):
        shift = (t_in == t_out - (C_CONV - 1 - j)).astype(BF16)
        acc = acc + cw_ref[j:j + 1, :] * jnp.dot(shift, ext, preferred_element_type=F32)
    xbc = jax.nn.silu(acc)
    xs = xbc[:, :C_INNER]
    gs_w = C_GROUPS * C_STATE
    bm = xbc[:, C_INNER:C_INNER + gs_w].astype(BF16)
    cm = xbc[:, C_INNER + gs_w:].astype(BF16)

    row = lax.broadcasted_iota(jnp.int32, (q_len, q_len), 0)
    col = lax.broadcasted_iota(jnp.int32, (q_len, q_len), 1)
    tri = col <= row

    lane = lax.broadcasted_iota(jnp.int32, (1, LANES), 1)
    dt = jnp.where(lane < C_HEADS, jax.nn.softplus(sdt_ref[0] + dtb_ref[...]), 0.0)
    dta = dt * -jnp.exp(alog_ref[...])
    acs = _dot_mask_lhs(tri.astype(BF16), dta)
    acs_t = acs.T
    chunk_dec = jnp.exp(acs[q_len - 1:q_len, :])
    expand = ex_ref[...]
    dt_x = _dot_mask_rhs(dt, expand)
    acs_x = _dot_mask_rhs(acs, expand)
    xd = xs * dt_x
    xd_b = xd.astype(BF16)
    xdd = xd * jnp.exp(acs_x[q_len - 1:q_len, :] - acs_x)
    low_half = lax.broadcasted_iota(jnp.int32, (q_len, LANES), 1) < C_HEAD_DIM

    y_pairs, y_offs = [], []
    tdims = (((1,), (1,)), ((), ()))
    for g in range(C_GROUPS):
        b_g = bm[:, g * C_STATE:(g + 1) * C_STATE]
        c_g = cm[:, g * C_STATE:(g + 1) * C_STATE]
        cb = lax.dot_general(c_g, b_g, tdims, preferred_element_type=F32)
        h0 = g * C_HPG
        grp = slice(h0 * C_HEAD_DIM, (h0 + C_HPG) * C_HEAD_DIM)
        y_offs.append(lax.dot_general(c_g, hs_ref[grp, :].astype(BF16), tdims, preferred_element_type=F32))
        for h in range(h0, h0 + C_HPG, 2):
            xp = xd_b[:, h * C_HEAD_DIM:(h + 2) * C_HEAD_DIM]
            halves = []
            for hh in (h, h + 1):
                seg = jnp.exp(jnp.where(tri, acs[:, hh:hh + 1] - acs_t[hh:hh + 1, :], -jnp.inf))
                halves.append(jnp.dot((cb * seg).astype(BF16), xp, preferred_element_type=F32))
            y_pairs.append(jnp.where(low_half, halves[0], halves[1]))
        upd = jnp.dot(xdd[:, grp].T.astype(BF16), b_g, preferred_element_type=F32)
        for hh in range(C_HPG):
            h = h0 + hh
            ps = slice(h * C_HEAD_DIM, (h + 1) * C_HEAD_DIM)
            us = slice(hh * C_HEAD_DIM, (hh + 1) * C_HEAD_DIM)
            hs_ref[ps, :] = hs_ref[ps, :] * chunk_dec[0:1, h:h + 1] + upd[us, :]
    y = (jnp.concatenate(y_pairs, axis=1) + jnp.concatenate(y_offs, axis=1) * jnp.exp(acs_x)
         + dsk_ref[...] * xs)
    y = y * jax.nn.silu(z_ref[0].astype(F32))
    o_ref[0, :, :C_INNER] = _rms(y, sn_ref[...]).astype(o_ref.dtype)

    same = (row // D_CHUNK) == (col // D_CHUNK)
    tri2 = tri & same
    la = jnp.dot(sg_ref[0].astype(BF16), gkw_ref[...], preferred_element_type=F32) + gkb_ref[...]
    la = jax.nn.log_sigmoid(la) / D_GATE_NORM
    gcs = _dot_mask_lhs(tri2.astype(BF16), la)
    first = lax.broadcasted_iota(jnp.int32, (q_len, 1), 0) < D_CHUNK
    r_all = r_ref[0]
    for h in range(D_HEADS):
        ks = slice(h * D_HK, (h + 1) * D_HK)
        vs = slice(h * D_HV, (h + 1) * D_HV)
        g_h = gcs[:, ks]
        g_end0 = g_h[D_CHUNK - 1:D_CHUNK, :]
        g_end1 = g_h[q_len - 1:q_len, :]
        q_h = q_ref[0, :, ks].astype(F32) * (D_HK ** -0.5)
        k_h = k_ref[0, :, ks].astype(F32)
        v_h = v_ref[0, :, vs]
        q_dec = (q_h * jnp.exp(g_h)).astype(BF16)
        k_inv = (k_h * jnp.exp(-g_h)).astype(BF16)
        k_end = k_h * jnp.exp(jnp.where(first, g_end0, g_end1) - g_h)
        ke0 = jnp.where(first, k_end, 0.0).astype(BF16)
        ke1 = jnp.where(first, 0.0, k_end).astype(BF16)
        attn = lax.dot_general(q_dec, k_inv, (((1,), (1,)), ((), ())), preferred_element_type=F32)
        attn = jnp.where(tri2, attn, 0.0).astype(BF16)
        o = jnp.dot(attn, v_h, preferred_element_type=F32)
        v_t = v_h.astype(F32).T.astype(BF16)
        st_rows = slice(h * D_HV, (h + 1) * D_HV)
        s0 = gs_ref[st_rows, :]
        s1 = s0 * jnp.exp(g_end0) + jnp.dot(v_t, ke0, preferred_element_type=F32)
        s2 = s1 * jnp.exp(g_end1) + jnp.dot(v_t, ke1, preferred_element_type=F32)
        gs_ref[st_rows, :] = s2
        tdims = (((1,), (1,)), ((), ()))
        o0 = lax.dot_general(q_dec, s0.astype(BF16), tdims, preferred_element_type=F32)
        o1 = lax.dot_general(q_dec, s1.astype(BF16), tdims, preferred_element_type=F32)
        o = o + jnp.where(first, o0, o1)
        o = _rms(o, gn_ref[...]) * jax.nn.silu(r_all[:, vs].astype(F32))
        o_ref[0, :, C_INNER + h * D_HV:C_INNER + (h + 1) * D_HV] = o.astype(o_ref.dtype)


def _ssd_gla(z, q, k, v, r, xbc, side_dt, side_g, g_lane, p):
    b, s, _ = z.shape
    q_len = C_CHUNK
    halo = 16
    chunk = lambda width: pl.BlockSpec((1, q_len, width), lambda i, n: (i, n, 0))
    full = lambda shape: pl.BlockSpec(shape, lambda i, n: (0,) * len(shape))
    pad_lanes = lambda vec: jnp.zeros((1, LANES), F32).at[0, :vec.shape[0]].set(vec.astype(F32))
    gkw = jnp.zeros((LANES, D_KEY), F32).at[g_lane:g_lane + D_GATE_RANK].set(p["gk_w"]).astype(BF16)
    expand = jnp.asarray(np.arange(C_INNER)[None, :] // C_HEAD_DIM == np.arange(LANES)[:, None], BF16)
    return pl.pallas_call(
        _ssd_gla_kernel,
        grid=(b, s // q_len),
        in_specs=[
            chunk(C_INNER), chunk(D_KEY), chunk(D_KEY), chunk(D_VAL), chunk(D_VAL), chunk(C_CONV_DIM),
            pl.BlockSpec((1, halo, C_CONV_DIM), lambda i, n: (i, jnp.maximum(n * (q_len // halo) - 1, 0), 0)),
            chunk(LANES), chunk(LANES),
            full((C_CONV, C_CONV_DIM)), full((1, C_CONV_DIM)),
            full((1, LANES)), full((1, LANES)), full((1, C_INNER)), full((LANES, C_INNER)),
            full((1, C_INNER)), full((LANES, D_KEY)), full((1, D_KEY)), full((1, D_HV)),
        ],
        out_specs=pl.BlockSpec((1, q_len, C_INNER + D_VAL), lambda i, n: (i, n, 0)),
        out_shape=jax.ShapeDtypeStruct((b, s, C_INNER + D_VAL), BF16),
        scratch_shapes=[pltpu.VMEM((C_INNER, C_STATE), F32),
                        pltpu.VMEM((D_VAL, D_HK), F32)],
        compiler_params=_cparams(("parallel", "arbitrary")),
        name="ssd_gla",
    )(z, q, k, v, r, xbc, xbc, side_dt, side_g,
      p["conv_w"].astype(F32), p["conv_b"].reshape(1, -1).astype(F32),
      pad_lanes(p["dt_bias"]), pad_lanes(p["a_log"]),
      jnp.repeat(p["d_skip"].astype(F32), C_HEAD_DIM).reshape(1, C_INNER), expand,
      p["ssd_norm"].reshape(1, -1).astype(F32), gkw, p["gk_b"].reshape(1, -1).astype(F32),
      p["gla_norm"].reshape(1, -1).astype(F32))


def _store_rows_tiled(ref, val):
    m, d = val.shape
    nc = d // LANES
    for c in range(nc):
        ref[pl.ds(c, m, stride=nc), :] = val[:, c * LANES:(c + 1) * LANES]


def _load_rows_tiled(ref, m, dtype=None):
    nc = ref.shape[0] // m
    parts = [ref[pl.ds(c, m, stride=nc), :] for c in range(nc)]
    if dtype is not None:
        parts = [p.astype(dtype) for p in parts]
    return jnp.concatenate(parts, axis=1)


def _pack_pairs(x):
    n = x.shape[1] // 2
    u = pltpu.bitcast(x.astype(BF16).astype(F32), jnp.uint32)
    return (u[:, :n] >> 16) | (u[:, n:] & jnp.uint32(0xFFFF0000))


def _unpack_pairs(u):
    lo = pltpu.bitcast(u << 16, F32).astype(BF16)
    hi = pltpu.bitcast(u & jnp.uint32(0xFFFF0000), F32).astype(BF16)
    return jnp.concatenate([lo, hi], axis=1)


RT_GATE, RT_EXPERT, RT_RANK = 0, 2, 4


def _route_block(lg, carry, earlier):
    m = lg.shape[0]
    lane = lax.broadcasted_iota(jnp.int32, (m, LANES), 1)
    lane_f = lane.astype(F32)
    none = float(LANES)
    neg = -jnp.inf
    first_max = lambda v, vmax: jnp.min(jnp.where(v == vmax, lane_f, none), axis=-1, keepdims=True)
    gl = jnp.where(lane < MOE_GROUPS, lg, neg)
    gmax = jnp.max(gl, axis=-1, keepdims=True)
    g_w = 1.0 / jnp.sum(jnp.exp(gl - gmax), axis=-1, keepdims=True)
    lo = MOE_GROUPS + first_max(gl, gmax) * MOE_EPG
    el = jnp.where((lane_f >= lo) & (lane_f < lo + MOE_EPG), lg, neg)
    emax = jnp.max(el, axis=-1, keepdims=True)
    esum = jnp.sum(jnp.exp(el - emax), axis=-1, keepdims=True)
    l0 = first_max(el, emax)
    el2 = jnp.where(lane_f == l0, neg, el)
    emax2 = jnp.max(el2, axis=-1, keepdims=True)
    l1 = first_max(el2, emax2)
    p0 = 1.0 / esum
    p1 = jnp.exp(emax2 - emax) / esum
    w0 = g_w * (p0 / (p0 + p1))
    w1 = g_w * (p1 / (p0 + p1))
    oh0 = lane_f == l0
    oh1 = lane_f == l1
    oh = (oh0 | oh1).astype(BF16)
    cum = jnp.dot(earlier, oh, preferred_element_type=F32) + carry
    rank0 = jnp.sum(jnp.where(oh0, cum, 0.0), axis=-1, keepdims=True)
    rank1 = jnp.sum(jnp.where(oh1, cum, 0.0), axis=-1, keepdims=True)
    carry = carry + jnp.sum(oh.astype(F32), axis=0, keepdims=True)
    rec = jnp.zeros((m, LANES), F32)
    for pos, val in ((RT_GATE, w0), (RT_GATE + 1, w1), (RT_EXPERT, l0 - MOE_GROUPS),
                     (RT_EXPERT + 1, l1 - MOE_GROUPS), (RT_RANK, rank0), (RT_RANK + 1, rank1)):
        rec = jnp.where(lane == pos, val, rec)
    return rec, carry


def _out_proj_kernel(n_parts, *refs):
    a_refs = refs[:n_parts]
    w_refs = refs[n_parts:2 * n_parts]
    h_ref, g_ref, wr_ref, br_ref, tri_ref, ho_ref, xt_ref, rt_ref, cnt_ref, carry_ref = refs[2 * n_parts:]

    @pl.when(pl.program_id(0) == 0)
    def _():
        carry_ref[...] = jnp.zeros_like(carry_ref)

    acc = h_ref[...]
    for a_ref, w_ref in zip(a_refs, w_refs):
        acc = acc + jnp.dot(a_ref[...], w_ref[...], preferred_element_type=F32)
    ho_ref[...] = acc
    xn = _rms(acc, g_ref[...])
    xt_ref[...] = xn.astype(xt_ref.dtype)
    x_hi, x_mid, _ = _split3(xn)
    wr = wr_ref[...]
    lg2 = jnp.dot(x_hi, wr, preferred_element_type=F32)
    lg = (lg2[:, :LANES] + lg2[:, LANES:] + jnp.dot(x_mid, wr[:, :LANES], preferred_element_type=F32)
          + br_ref[...])
    rec, carry = _route_block(lg, carry_ref[...], tri_ref[...])
    rt_ref[...] = rec
    carry_ref[...] = carry
    cnt_ref[0] = carry


def _out_proj(parts, w_parts, h, g, w_route, b_route, *, tm=512):
    t, d = h.shape
    tm = min(tm, t)
    nc = d // LANES
    row = lambda i: (i, 0)
    const = lambda i: (0, 0)
    in_specs = [pl.BlockSpec((tm, a.shape[1]), row) for a in parts]
    in_specs += [pl.BlockSpec(w.shape, const) for w in w_parts]
    in_specs += [pl.BlockSpec((tm, d), row), pl.BlockSpec((1, d), const),
                 pl.BlockSpec((d, 2 * LANES), const), pl.BlockSpec((1, LANES), const),
                 pl.BlockSpec((tm, tm), const)]
    earlier = jnp.asarray(np.tril(np.ones((tm, tm), np.float32), -1), BF16)
    return pl.pallas_call(
        functools.partial(_out_proj_kernel, len(parts)),
        grid=(t // tm,),
        in_specs=in_specs,
        out_specs=[pl.BlockSpec((tm, d), row), pl.BlockSpec((tm, d), row),
                   pl.BlockSpec((tm, LANES), row), pl.BlockSpec((1, 1, LANES), lambda i: (i, 0, 0))],
        out_shape=[jax.ShapeDtypeStruct((t, d), F32), jax.ShapeDtypeStruct((t, d), BF16),
                   jax.ShapeDtypeStruct((t, LANES), F32), jax.ShapeDtypeStruct((t // tm, 1, LANES), F32)],
        scratch_shapes=[pltpu.VMEM((1, LANES), F32)],
        compiler_params=_cparams(("arbitrary",)),
        name="out_proj",
    )(*parts, *w_parts, h, g.reshape(1, d), w_route, b_route, earlier)


LONG_RUN = 128


def _piece_sizes(max_rows, lo=1, hi=None):
    sizes = [1 << b for b in reversed(range(max_rows.bit_length()))]
    return [n for n in sizes if n >= lo and (hi is None or n < hi)]


def _for_each_piece(run, sizes, body):
    off = run // (2 * sizes[0]) * (2 * sizes[0])
    for n in sizes:
        hit = (run & n) != 0
        pl.when(hit)(functools.partial(body, off, n))
        off = off + jnp.where(hit, n, 0)


def _for_each_run(len_ref, tile, long_ref, max_rows, make_body):
    def sweep(sizes):
        def per_expert(e, first):
            run = len_ref[tile * MOE_EXPERTS + e]
            _for_each_piece(run, sizes, make_body(e, first))
            return first + run

        lax.fori_loop(0, MOE_EXPERTS, per_expert, jnp.int32(0))

    sweep(_piece_sizes(max_rows, hi=LONG_RUN))
    pl.when(long_ref[tile] != 0)(functools.partial(sweep, _piece_sizes(max_rows, lo=LONG_RUN)))


def _zero_fill_rows(rows_ref, z_ref, zsem, lo_ref, hi_ref, n_rows, nc):
    zb = z_ref.shape[0] // nc
    z_ref[...] = jnp.zeros_like(z_ref)
    assert MOE_TILE // 2 <= zb and MOE_TILE % zb == 0

    def piece(row0, n):
        return pltpu.make_async_copy(z_ref.at[pl.ds(0, n * nc)], rows_ref.at[pl.ds(row0 * nc, n * nc)], zsem)

    def sweep(issue):
        def per_expert(e, carry):
            lo = lo_ref[e]

            def one(off, n):
                piece(lo + off, n).start() if issue else piece(lo + off, n).wait()

            _for_each_piece(hi_ref[e] - lo, _piece_sizes(MOE_TILE // 2), one)
            return carry

        def per_block(i, carry):
            piece(i * zb, zb).start() if issue else piece(i * zb, zb).wait()
            return carry

        lax.fori_loop(0, MOE_EXPERTS, per_expert, 0)
        lax.fori_loop(hi_ref[MOE_EXPERTS - 1] // zb, n_rows // zb, per_block, 0)

    sweep(True)
    sweep(False)


def _tile_positions(rt, pos_base):
    lane_f = lax.broadcasted_iota(jnp.int32, rt.shape, 1).astype(F32)
    out = []
    for k in range(MOE_TOPK):
        e_lane = rt[:, RT_EXPERT + k:RT_EXPERT + k + 1] + MOE_GROUPS
        out.append(rt[:, RT_RANK + k:RT_RANK + k + 1]
                   + jnp.sum(jnp.where(lane_f == e_lane, pos_base, 0.0), axis=-1, keepdims=True))
    return out


def _moe_dispatch_kernel(dst_ref, len_ref, long_ref, lo_ref, hi_ref, x_ref, rt_ref, pb_ref, rows_ref,
                         sbuf, z_ref, sem, zsem, *, tm, nc, n_steps, n_rows):
    i = pl.program_id(0)
    rows = MOE_TOPK * tm
    slot = lax.rem(i, 2)

    def wait_slot(s):
        pltpu.make_async_copy(sbuf.at[s], rows_ref.at[pl.ds(0, rows * nc)], sem.at[s]).wait()

    @pl.when(i == 0)
    def _():
        _zero_fill_rows(rows_ref, z_ref, zsem, lo_ref, hi_ref, n_rows, nc)

    @pl.when(i >= 2)
    def _():
        wait_slot(slot)

    pos = _tile_positions(rt_ref[...], pb_ref[0])
    lane = lax.broadcasted_iota(jnp.int32, (tm, LANES), 1)
    pos_t = jnp.where(lane == 0, pos[0], jnp.where(lane == 1, pos[1], 0.0)).T
    p_iota = lax.broadcasted_iota(jnp.int32, (rows, tm), 0).astype(F32)
    place = ((p_iota == pos_t[0:1, :]) | (p_iota == pos_t[1:2, :])).astype(BF16)
    _store_rows_tiled(sbuf.at[slot], _pack_pairs(jnp.dot(place, x_ref[...], preferred_element_type=F32)))

    def sender(e, src):
        dst = dst_ref[i * MOE_EXPERTS + e]

        def send(off, n):
            pltpu.make_async_copy(sbuf.at[slot, pl.ds((src + off) * nc, n * nc)],
                                  rows_ref.at[pl.ds((dst + off) * nc, n * nc)], sem.at[slot]).start()

        return send

    _for_each_run(len_ref, i, long_ref, rows, sender)

    @pl.when(i == n_steps - 1)
    def _():
        wait_slot(slot)
        if n_steps > 1:
            wait_slot(1 - slot)


def _moe_dispatch(x, route, pos_base, seg_dst, seg_len, seg_long, pad_lo, pad_hi, n_rows, *, tm):
    t, d = x.shape
    nc = d // (2 * LANES)
    n_steps = t // tm
    row = lambda i, *_: (i, 0)
    return pl.pallas_call(
        functools.partial(_moe_dispatch_kernel, tm=tm, nc=nc, n_steps=n_steps, n_rows=n_rows),
        grid_spec=pltpu.PrefetchScalarGridSpec(
            num_scalar_prefetch=5, grid=(n_steps,),
            in_specs=[pl.BlockSpec((tm, d), row), pl.BlockSpec((tm, LANES), row),
                      pl.BlockSpec((1, 1, LANES), lambda i, *_: (i, 0, 0))],
            out_specs=pl.BlockSpec(memory_space=pl.ANY),
            scratch_shapes=[pltpu.VMEM((2, MOE_TOPK * tm * nc, LANES), jnp.uint32),
                            pltpu.VMEM((MOE_TILE // 2 * nc, LANES), jnp.uint32),
                            pltpu.SemaphoreType.DMA((2,)), pltpu.SemaphoreType.DMA(())]),
        out_shape=jax.ShapeDtypeStruct((n_rows * nc, LANES), jnp.uint32),
        compiler_params=pltpu.CompilerParams(dimension_semantics=("arbitrary",), has_side_effects=True,
                                             vmem_limit_bytes=VMEM_LIMIT),
        name="moe_dispatch",
    )(seg_dst, seg_len, seg_long, pad_lo, pad_hi, x, route, pos_base)


def _moe_kernel(te_ref, nu_ref, x_ref, wg_ref, wu_ref, wd_ref, y_ref):
    i = pl.program_id(0)

    @pl.when(i >= nu_ref[0])
    def _():
        y_ref[...] = jnp.zeros_like(y_ref)

    @pl.when(i < nu_ref[0])
    def _():
        x = _unpack_pairs(_load_rows_tiled(x_ref, MOE_TILE))
        gate = jnp.dot(x, wg_ref[...].astype(BF16), preferred_element_type=F32)
        up = jnp.dot(x, wu_ref[...].astype(BF16), preferred_element_type=F32)
        act = (jax.nn.silu(gate) * up).astype(BF16)
        y = jnp.dot(act, wd_ref[...].astype(BF16), preferred_element_type=F32)
        _store_rows_tiled(y_ref, _pack_pairs(y))


def _moe_experts(x_rows, tile_expert, n_used, w_gate, w_up, w_down, layer):
    d, ff = w_gate.shape[-2:]
    nc = d // (2 * LANES)
    n_tiles = x_rows.shape[0] // (MOE_TILE * nc)
    live = lambda i, nu: jnp.minimum(i, nu[0] - 1)
    w_spec = lambda a, b: pl.BlockSpec((None, None, a, b), lambda i, te, nu: (layer, te[live(i, nu)], 0, 0))
    return pl.pallas_call(
        _moe_kernel,
        grid_spec=pltpu.PrefetchScalarGridSpec(
            num_scalar_prefetch=2, grid=(n_tiles,),
            in_specs=[pl.BlockSpec((MOE_TILE * nc, LANES), lambda i, te, nu: (live(i, nu), 0)),
                      w_spec(d, ff), w_spec(d, ff), w_spec(ff, d)],
            out_specs=pl.BlockSpec((MOE_TILE * nc, LANES), lambda i, te, nu: (i, 0))),
        out_shape=jax.ShapeDtypeStruct(x_rows.shape, x_rows.dtype),
        compiler_params=_cparams(("arbitrary",)),
        name="moe_experts",
    )(tile_expert, n_used, x_rows, w_gate, w_up, w_down)


def _moe_combine_kernel(src_ref, len_ref, long_ref, h_ref, rt_ref, pb_ref, g_ref, y_hbm, o_ref, ybuf, sem,
                        *, tm, nc, final):
    i = pl.program_id(0)
    n_steps = pl.num_programs(0)
    rows = MOE_TOPK * tm

    def fetch(tile, slot):
        def receiver(e, dst):
            src = src_ref[tile * MOE_EXPERTS + e]

            def recv(off, n):
                pltpu.make_async_copy(y_hbm.at[pl.ds((src + off) * nc, n * nc)],
                                      ybuf.at[slot, pl.ds((dst + off) * nc, n * nc)], sem.at[slot]).start()

            return recv

        _for_each_run(len_ref, tile, long_ref, rows, receiver)

    slot = lax.rem(i, 2)

    @pl.when(i == 0)
    def _():
        fetch(0, 0)

    @pl.when(i + 1 < n_steps)
    def _():
        fetch(i + 1, 1 - slot)

    pltpu.make_async_copy(y_hbm.at[pl.ds(0, rows * nc)], ybuf.at[slot], sem.at[slot]).wait()
    y = _unpack_pairs(_load_rows_tiled(ybuf.at[slot], rows))
    rt = rt_ref[...]
    pos_f = lax.broadcasted_iota(jnp.int32, (tm, rows), 1).astype(F32)
    pick = jnp.zeros((tm, rows), F32)
    for k, pos in enumerate(_tile_positions(rt, pb_ref[0])):
        pick = jnp.where(pos_f == pos, rt[:, RT_GATE + k:RT_GATE + k + 1], pick)
    out = h_ref[...] + jnp.dot(pick.astype(BF16), y, preferred_element_type=F32)
    o_ref[...] = _rms(out, g_ref[...]) if final else out


def _moe_combine_rows(h, y_rows, route, seg_src, seg_len, seg_long, pos_base, g, *, tm, final):
    t, d = h.shape
    nc = d // (2 * LANES)
    row = lambda i, *_: (i, 0)
    return pl.pallas_call(
        functools.partial(_moe_combine_kernel, tm=tm, nc=nc, final=final),
        grid_spec=pltpu.PrefetchScalarGridSpec(
            num_scalar_prefetch=3, grid=(t // tm,),
            in_specs=[pl.BlockSpec((tm, d), row), pl.BlockSpec((tm, LANES), row),
                      pl.BlockSpec((1, 1, LANES), lambda i, *_: (i, 0, 0)),
                      pl.BlockSpec((1, d), lambda i, *_: (0, 0)),
                      pl.BlockSpec(memory_space=pl.ANY)],
            out_specs=pl.BlockSpec((tm, d), row),
            scratch_shapes=[pltpu.VMEM((2, MOE_TOPK * tm * nc, LANES), y_rows.dtype),
                            pltpu.SemaphoreType.DMA((2,))]),
        out_shape=jax.ShapeDtypeStruct((t, d), F32),
        compiler_params=_cparams(("arbitrary",)),
        name="moe_combine",
    )(seg_src, seg_len, seg_long, h, route, pos_base, g.reshape(1, d), y_rows)


def _moe(h, x, route, tile_counts, w_gate, w_up, w_down, layer, g, *, final):
    t = route.shape[0]
    n_tt = tile_counts.shape[0]
    tm = t // n_tt
    after = tile_counts[:, 0, MOE_GROUPS:MOE_GROUPS + MOE_EXPERTS].astype(jnp.int32)
    before = jnp.concatenate([jnp.zeros((1, MOE_EXPERTS), jnp.int32), after[:-1]], axis=0)
    cnt = after[-1]
    padded = (cnt + MOE_TILE - 1) // MOE_TILE * MOE_TILE
    pad_ends = jnp.cumsum(padded)
    starts = (pad_ends - padded).astype(jnp.int32)
    n_tiles = (t * MOE_TOPK + MOE_EXPERTS * (MOE_TILE - 1)) // MOE_TILE
    tile_start = jnp.arange(n_tiles, dtype=jnp.int32) * MOE_TILE
    tile_expert = jnp.minimum(jnp.sum(tile_start[:, None] >= pad_ends[None, :], axis=1),
                              MOE_EXPERTS - 1).astype(jnp.int32)
    n_used = (pad_ends[-1] // MOE_TILE).astype(jnp.int32).reshape(1)
    n_rows = n_tiles * MOE_TILE
    seg_len = (after - before).reshape(-1)
    seg_off = jnp.cumsum(after - before, axis=1) - (after - before)
    seg_row = (starts[None, :] + before).reshape(-1)
    pos_base = jnp.zeros((n_tt, 1, LANES), F32).at[:, 0, MOE_GROUPS:MOE_GROUPS + MOE_EXPERTS].set(
        (seg_off - before).astype(F32))
    seg_long = jnp.any(after - before >= LONG_RUN, axis=1).astype(jnp.int32)
    x_rows = _moe_dispatch(x, route, pos_base, seg_row, seg_len, seg_long, starts + cnt,
                           pad_ends.astype(jnp.int32), n_rows, tm=tm)
    y_rows = _moe_experts(x_rows, tile_expert, n_used, w_gate, w_up, w_down, layer)
    return _moe_combine_rows(h, y_rows, route, seg_row, seg_len, seg_long, pos_base, g, tm=tm, final=final)


def _router_weights(w_group, b_group, w_router, b_router):
    d = w_group.shape[0]
    w = jnp.zeros((d, LANES), F32)
    w = w.at[:, :MOE_GROUPS].set(w_group).at[:, MOE_GROUPS:MOE_GROUPS + MOE_EXPERTS].set(w_router)
    b = jnp.zeros((1, LANES), F32)
    b = b.at[0, :MOE_GROUPS].set(b_group).at[0, MOE_GROUPS:MOE_GROUPS + MOE_EXPERTS].set(b_router)
    w_hi = w.astype(BF16)
    w_mid = (w - w_hi.astype(F32)).astype(BF16)
    return jnp.concatenate([w_hi, w_mid], axis=1), b


def kernel(x, norm_mix, norm_moe, norm_final, even_w_in, even_sinks, even_forget_bias, even_w_out,
           odd_w_in, odd_conv_w, odd_conv_b, odd_dt_bias, odd_a_log, odd_d_skip, odd_ssd_norm,
           odd_gk_w, odd_gk_b, odd_gla_norm, odd_w_out, moe_w_group, moe_b_group, moe_w_router,
           moe_b_router, moe_w_gate, moe_w_up, moe_w_down):
    b, s, d = x.shape
    t = b * s
    depth = norm_mix.shape[0]
    h = x.reshape(t, d)
    for layer in range(depth):
        i = layer // 2
        if layer % 2 == 0:
            w = even_w_in[i]
            n_ab = (A_Q_HEADS + 2 * A_KV_HEADS + 3 * B_HEADS) * HEAD_DIM
            w_main = w[:, :n_ab].astype(BF16)
            w_aux = jnp.zeros((d, LANES), F32).at[:, :B_HEADS].set(w[:, n_ab:]).astype(BF16)
            proj, f_aux = _norm_proj(h, norm_mix[layer], [w_main, w_aux],
                                     ((0, 0, n_ab, BF16), (1, 0, LANES, F32)), tm=512)
            proj = proj.reshape(b, s, -1)
            out_a = _swa(proj, even_sinks[i])
            c, ct = _fox_gate(f_aux.reshape(b, s, LANES), even_forget_bias[i])
            out_b = _fox(proj, c, ct)
            n_ha = A_Q_HEADS * HEAD_DIM
            w_out = even_w_out[i].astype(BF16)
            parts = [out_a.reshape(t, -1), out_b.reshape(t, -1)]
            w_parts = [w_out[:n_ha], w_out[n_ha:]]
        else:
            w = odd_w_in[i]
            o_z, o_xbc = 0, C_INNER
            o_dt = o_xbc + C_CONV_DIM
            o_q = o_dt + C_HEADS
            o_k = o_q + D_KEY
            o_v = o_k + D_KEY
            o_g = o_v + D_VAL
            o_r = o_g + D_GATE_RANK
            n_b = w.shape[1] - o_dt
            w_a = w[:, :o_dt].astype(BF16)
            w_b = jnp.pad(w[:, o_dt:].astype(BF16), ((0, 0), (0, -n_b % LANES)))
            g_win = (o_g - o_dt) // LANES * LANES
            plan = ((0, o_z, C_INNER, BF16), (1, o_q - o_dt, D_KEY, BF16), (1, o_k - o_dt, D_KEY, BF16),
                    (1, o_v - o_dt, D_VAL, BF16), (1, o_r - o_dt, D_VAL, BF16), (0, o_xbc, C_CONV_DIM, BF16),
                    (1, 0, LANES, F32), (1, g_win, LANES, F32))
            outs = _norm_proj(h, norm_mix[layer], [w_a, w_b], plan, tm=512)
            params = dict(conv_w=odd_conv_w[i], conv_b=odd_conv_b[i], dt_bias=odd_dt_bias[i], a_log=odd_a_log[i],
                          d_skip=odd_d_skip[i], ssd_norm=odd_ssd_norm[i], gk_w=odd_gk_w[i], gk_b=odd_gk_b[i],
                          gla_norm=odd_gla_norm[i])
            mixed = _ssd_gla(*[o.reshape(b, s, -1) for o in outs], o_g - o_dt - g_win, params)
            parts = [mixed.reshape(t, -1)]
            w_parts = [odd_w_out[i].astype(BF16)]
        w_route, b_route = _router_weights(moe_w_group[layer], moe_b_group[layer],
                                           moe_w_router[layer], moe_b_router[layer])
        h, x_tiled, route, tile_counts = _out_proj(parts, w_parts, h, norm_moe[layer], w_route, b_route)
        h = _moe(h, x_tiled, route, tile_counts, moe_w_gate, moe_w_up, moe_w_down, layer, norm_final,
                 final=layer == depth - 1)
    out = h
    return out.reshape(b, s, d)
```

```python
import functools

import numpy as np
import jax
import jax.numpy as jnp
from jax import lax
from jax.experimental import pallas as pl
from jax.experimental.pallas import tpu as pltpu

F32 = jnp.float32
BF16 = jnp.bfloat16

RMS_EPS = 1e-6
HEAD_DIM = 64
A_Q_HEADS = 8
A_KV_HEADS = 2
A_GROUP = A_Q_HEADS // A_KV_HEADS
A_WINDOW = 128
B_HEADS = 8
C_HEADS = 16
C_HEAD_DIM = 64
C_INNER = C_HEADS * C_HEAD_DIM
C_GROUPS = 2
C_HPG = C_HEADS // C_GROUPS
C_STATE = 128
C_CONV = 4
C_CHUNK = 128
C_CONV_DIM = C_INNER + 2 * C_GROUPS * C_STATE
D_HEADS = 4
D_HK = 128
D_HV = 256
D_KEY = D_HEADS * D_HK
D_VAL = D_HEADS * D_HV
D_GATE_RANK = 16
D_GATE_NORM = 16.0
D_CHUNK = 64
MOE_GROUPS = 4
MOE_EPG = 8
MOE_EXPERTS = MOE_GROUPS * MOE_EPG
MOE_TOPK = 2

LANES = 128
VMEM_LIMIT = 48 * 1024 * 1024
MOE_TILE = 512


def _cparams(sem):
    return pltpu.CompilerParams(dimension_semantics=sem, vmem_limit_bytes=VMEM_LIMIT)


def _rms(x, g):
    ms = jnp.mean(x * x, axis=-1, keepdims=True)
    return x * lax.rsqrt(ms + RMS_EPS) * g


def _norm_proj_kernel(n_w, plan, *refs):
    x_ref, g_ref = refs[:2]
    w_refs = refs[2:2 + n_w]
    o_refs = refs[2 + n_w:2 + n_w + len(plan)]
    res_refs = refs[2 + n_w + len(plan):]
    xn = _rms(x_ref[...], g_ref[...]).astype(BF16)
    for w_ref, res_ref in zip(w_refs, res_refs):
        res_ref[...] = jnp.dot(xn, w_ref[...], preferred_element_type=F32)
    for o_ref, (wi, start, width, _) in zip(o_refs, plan):
        o_ref[...] = res_refs[wi][:, start:start + width].astype(o_ref.dtype)


def _norm_proj(x, g, weights, plan, *, tm):
    t, d = x.shape
    tm = min(tm, t)
    row = lambda i: (i, 0)
    const = lambda i: (0, 0)
    in_specs = [pl.BlockSpec((tm, d), row), pl.BlockSpec((1, d), const)]
    in_specs += [pl.BlockSpec(w.shape, const, pipeline_mode=pl.Buffered(1)) for w in weights]
    return pl.pallas_call(
        functools.partial(_norm_proj_kernel, len(weights), plan),
        grid=(t // tm,),
        in_specs=in_specs,
        out_specs=[pl.BlockSpec((tm, width), row) for _, _, width, _ in plan],
        out_shape=[jax.ShapeDtypeStruct((t, width), dtype) for _, _, width, dtype in plan],
        scratch_shapes=[pltpu.VMEM((tm, w.shape[1]), F32) for w in weights],
        compiler_params=_cparams(("parallel",)),
        name="norm_proj",
    )(x, g.reshape(1, d), *weights)


SWA_QBLOCKS = 4


def _swa_kernel(sink_ref, slope_ref, q_ref, kp_ref, kc_ref, vp_ref, vc_ref, o_ref):
    n = pl.program_id(1)
    blk = A_WINDOW
    wide = A_GROUP * blk
    key = lax.broadcasted_iota(jnp.int32, (2 * blk, wide), 0)
    qry = lax.broadcasted_iota(jnp.int32, (2 * blk, wide), 1) % blk
    dist = blk + qry - key
    in_window = (dist >= 0) & (dist < A_WINDOW)
    distf = dist.astype(F32)
    nt = (((1,), (1,)), ((), ()))
    k_all = jnp.concatenate([kp_ref[0], kc_ref[0]], axis=0)
    v_all = jnp.concatenate([vp_ref[0], vc_ref[0]], axis=1)
    units = [(j, kh) for j in range(SWA_QBLOCKS) for kh in range(A_KV_HEADS)]
    scores = []
    for j, kh in units:
        k = k_all[j * blk:(j + 2) * blk, kh * HEAD_DIM:(kh + 1) * HEAD_DIM]
        q = jnp.concatenate([q_ref[0, j * blk:(j + 1) * blk,
                                   (kh * A_GROUP + g) * HEAD_DIM:(kh * A_GROUP + g + 1) * HEAD_DIM]
                             for g in range(A_GROUP)], axis=0)
        scores.append(lax.dot_general(k, q, nt, preferred_element_type=F32))
    probs = []
    for (j, kh), s in zip(units, scores):
        valid = in_window & ((key >= blk) | (n * SWA_QBLOCKS + j > 0))
        s = s * (HEAD_DIM ** -0.5) - slope_ref[kh:kh + 1, :] * distf
        s = jnp.where(valid, s, -jnp.inf)
        sink = sink_ref[kh:kh + 1, :]
        m = jnp.maximum(jnp.max(s, axis=0, keepdims=True), sink)
        p = jnp.exp(s - m)
        probs.append((p.astype(BF16), jnp.sum(p, axis=0, keepdims=True) + jnp.exp(sink - m)))
    for j in range(SWA_QBLOCKS):
        outs = []
        for kh in range(A_KV_HEADS):
            p, denom = probs[j * A_KV_HEADS + kh]
            v_t = v_all[kh * HEAD_DIM:(kh + 1) * HEAD_DIM, j * blk:(j + 2) * blk]
            o_t = jnp.dot(v_t, p, preferred_element_type=F32) / denom
            outs += [o_t[:, g * blk:(g + 1) * blk] for g in range(A_GROUP)]
        o_ref[0, j * blk:(j + 1) * blk, :] = jnp.concatenate(outs, axis=0).T.astype(o_ref.dtype)


def _swa(proj, sinks):
    b, s, _ = proj.shape
    blk = A_WINDOW
    qw = A_Q_HEADS * HEAD_DIM
    kw = A_KV_HEADS * HEAD_DIM
    k_blk = qw // kw
    v_t = proj[:, :, qw + kw:qw + 2 * kw].transpose(0, 2, 1)
    per_lane = lambda vec: jnp.repeat(vec.astype(F32), blk).reshape(A_KV_HEADS, A_GROUP * blk)
    slopes = jnp.asarray(2.0 ** (-8.0 * np.arange(1, A_Q_HEADS + 1) / A_Q_HEADS), F32)
    tq = SWA_QBLOCKS * blk
    prev = lambda n: jnp.maximum(n * SWA_QBLOCKS - 1, 0)
    full = pl.BlockSpec((A_KV_HEADS, A_GROUP * blk), lambda i, n: (0, 0))
    return pl.pallas_call(
        _swa_kernel,
        grid=(b, s // tq),
        in_specs=[
            full, full,
            pl.BlockSpec((1, tq, qw), lambda i, n: (i, n, 0)),
            pl.BlockSpec((1, blk, kw), lambda i, n: (i, prev(n), k_blk)),
            pl.BlockSpec((1, tq, kw), lambda i, n: (i, n, k_blk)),
            pl.BlockSpec((1, kw, blk), lambda i, n: (i, 0, prev(n))),
            pl.BlockSpec((1, kw, tq), lambda i, n: (i, 0, n)),
        ],
        out_specs=pl.BlockSpec((1, tq, qw), lambda i, n: (i, n, 0)),
        out_shape=jax.ShapeDtypeStruct((b, s, qw), BF16),
        compiler_params=_cparams(("parallel", "parallel")),
        name="swa",
    )(per_lane(sinks), per_lane(slopes), proj, proj, proj, v_t, v_t)


def _tril(n, dtype=F32):
    r = lax.broadcasted_iota(jnp.int32, (n, n), 0)
    c = lax.broadcasted_iota(jnp.int32, (n, n), 1)
    return (c <= r).astype(dtype)


def _split3(x):
    hi = x.astype(BF16)
    r = x - hi.astype(F32)
    mid = r.astype(BF16)
    return hi, mid, (r - mid.astype(F32)).astype(BF16)


def _dot_mask_lhs(mask, x):
    return sum(jnp.dot(mask, part, preferred_element_type=F32) for part in _split3(x))


def _dot_mask_rhs(x, mask):
    return sum(jnp.dot(part, mask, preferred_element_type=F32) for part in _split3(x))


def _fox_gate_kernel(f_ref, b_ref, c_ref, ct_ref):
    tri = _tril(LANES, BF16)
    carry = jnp.zeros((1, LANES), F32)
    for n in range(f_ref.shape[1] // LANES):
        rows = slice(n * LANES, (n + 1) * LANES)
        lf = jax.nn.log_sigmoid(f_ref[0, rows, :] + b_ref[...])
        cs = _dot_mask_lhs(tri, lf) + carry
        carry = cs[LANES - 1:LANES, :]
        c_ref[0, rows, :] = cs
        ct_ref[0, n] = cs.T[:B_HEADS, :]


def _fox_gate(f_aux, bias):
    b, s, _ = f_aux.shape
    nb = s // LANES
    bias_p = jnp.zeros((1, LANES), F32).at[0, :B_HEADS].set(bias.astype(F32))
    return pl.pallas_call(
        _fox_gate_kernel,
        grid=(b,),
        in_specs=[pl.BlockSpec((1, s, LANES), lambda i: (i, 0, 0)),
                  pl.BlockSpec((1, LANES), lambda i: (0, 0))],
        out_specs=[pl.BlockSpec((1, s, LANES), lambda i: (i, 0, 0)),
                   pl.BlockSpec((1, nb, B_HEADS, LANES), lambda i: (i, 0, 0, 0))],
        out_shape=[jax.ShapeDtypeStruct((b, s, LANES), F32),
                   jax.ShapeDtypeStruct((b, nb, B_HEADS, LANES), F32)],
        compiler_params=_cparams(("parallel",)),
        name="fox_gate",
    )(f_aux, bias_p)


def _fox_kernel(q0_ref, q1_ref, k0_ref, k1_ref, vt_ref, c_ref, ctq_ref, o_ref, *, tq, heads_per_step):
    qi = pl.program_id(1)
    sub = tq // LANES
    key = lax.broadcasted_iota(jnp.int32, (tq, tq), 0)
    qry = lax.broadcasted_iota(jnp.int32, (tq, tq), 1)
    causal = key <= qry
    nt = (((1,), (1,)), ((), ()))
    half = B_HEADS // 2
    q_refs, k_refs = (q0_ref, q1_ref), (k0_ref, k1_ref)
    outs = []
    for h0 in range(0, B_HEADS, heads_per_step):
        heads = list(range(h0, h0 + heads_per_step))
        hsl = [slice(h * HEAD_DIM, (h + 1) * HEAD_DIM) for h in heads]
        lsl = [slice((h % half) * HEAD_DIM, (h % half + 1) * HEAD_DIM) for h in heads]
        qs = [q_refs[h // half][0, :, ls] * (HEAD_DIM ** -0.5)
              for h, ls in zip(heads, lsl)]
        cqs = [jnp.concatenate([ctq_ref[0, u, h:h + 1, :] for u in range(sub)], axis=1) for h in heads]

        def step(j, carry, masked, heads=heads, hsl=hsl, lsl=lsl, qs=qs, cqs=cqs):
            start = pl.multiple_of(j * tq, tq)
            sts = [lax.dot_general(k_refs[h // half][0, pl.ds(start, tq), ls], q, nt,
                                   preferred_element_type=F32)
                   for h, ls, q in zip(heads, lsl, qs)]
            ps, stats = [], []
            for idx, h in enumerate(heads):
                m, l, _ = carry[3 * idx:3 * idx + 3]
                ck = c_ref[0, pl.ds(start, tq), h:h + 1]
                st = (sts[idx] - ck) + cqs[idx]
                if masked:
                    st = jnp.where(causal, st, -jnp.inf)
                m_new = jnp.maximum(m, jnp.max(st, axis=0, keepdims=True))
                alpha = jnp.exp(m - m_new)
                p = jnp.exp(st - m_new)
                stats.append((m_new, alpha, alpha * l + jnp.sum(p, axis=0, keepdims=True)))
                ps.append(p.astype(BF16))
            new = []
            for idx in range(len(heads)):
                m_new, alpha, l = stats[idx]
                pv = jnp.dot(vt_ref[0, j, hsl[idx], :], ps[idx], preferred_element_type=F32)
                new += [m_new, l, alpha * carry[3 * idx + 2] + pv]
            return tuple(new)

        init = (jnp.full((1, tq), -jnp.inf, F32), jnp.zeros((1, tq), F32),
                jnp.zeros((HEAD_DIM, tq), F32)) * heads_per_step
        carry = lax.fori_loop(0, qi, functools.partial(step, masked=False), init)
        carry = step(qi, carry, True)
        for idx in range(heads_per_step):
            outs.append(carry[3 * idx + 2] / carry[3 * idx + 1])
    o_ref[0] = jnp.concatenate(outs, axis=0).T.astype(o_ref.dtype)


def _fox(proj, c, ct, *, tq=256, heads_per_step=8):
    b, s, _ = proj.shape
    w = B_HEADS * HEAD_DIM
    nk = s // tq
    sub = tq // LANES
    hw = w // 2
    base = (A_Q_HEADS + 2 * A_KV_HEADS) * HEAD_DIM
    qb, kb = base // hw, (base + w) // hw
    v_t = proj[:, :, base + 2 * w:base + 3 * w].reshape(b, nk, tq, w).transpose(0, 1, 3, 2)
    return pl.pallas_call(
        functools.partial(_fox_kernel, tq=tq, heads_per_step=heads_per_step),
        grid=(b, s // tq),
        in_specs=[
            pl.BlockSpec((1, tq, hw), lambda i, n: (i, n, qb)),
            pl.BlockSpec((1, tq, hw), lambda i, n: (i, n, qb + 1)),
            pl.BlockSpec((1, s, hw), lambda i, n: (i, 0, kb)),
            pl.BlockSpec((1, s, hw), lambda i, n: (i, 0, kb + 1)),
            pl.BlockSpec((1, nk, w, tq), lambda i, n: (i, 0, 0, 0)),
            pl.BlockSpec((1, s, LANES), lambda i, n: (i, 0, 0)),
            pl.BlockSpec((1, sub, B_HEADS, LANES), lambda i, n: (i, n, 0, 0)),
        ],
        out_specs=pl.BlockSpec((1, tq, w), lambda i, n: (i, n, 0)),
        out_shape=jax.ShapeDtypeStruct((b, s, w), BF16),
        compiler_params=_cparams(("parallel", "parallel")),
        name="fox",
    )(proj, proj, proj, proj, v_t, c, ct)


SSD_BATCHES = 2


def _ssd_gla_kernel(*refs):
    n_data, n_par = 9, 10
    data, params = refs[:n_data], refs[n_data:n_data + n_par]
    o_ref, hs_ref, gs_ref = refs[n_data + n_par:]

    @pl.when(pl.program_id(1) == 0)
    def _():
        hs_ref[...] = jnp.zeros_like(hs_ref)
        gs_ref[...] = jnp.zeros_like(gs_ref)

    for bb in range(o_ref.shape[0]):
        one = lambda ref: ref.at[pl.ds(bb, 1)]
        _ssd_gla_chunk(*[one(ref) for ref in data], *params, one(o_ref), hs_ref.at[bb], gs_ref.at[bb])


def _ssd_gla_chunk(z_ref, q_ref, k_ref, v_ref, r_ref, xc_ref, xp_ref, sdt_ref, sg_ref,
                   cw_ref, cb_ref, dtb_ref, alog_ref, dsk_ref, ex_ref, sn_ref, gkw_ref, gkb_ref, gn_ref,
                   o_ref, hs_ref, gs_ref):
    c = pl.program_id(1)
    q_len = C_CHUNK
    halo = xp_ref.shape[1]

    prev = xp_ref[0]
    cur = xc_ref[0]
    ext = jnp.concatenate([jnp.where(c > 0, prev, jnp.zeros_like(prev)), cur], axis=0)
    t_out = lax.broadcasted_iota(jnp.int32, (q_len, halo + q_len), 0)
    t_in = lax.broadcasted_iota(jnp.int32, (q_len, halo + q_len), 1) - halo
    acc = cb_ref[...] + cw_ref[C_CONV - 1:C_CONV, :] * cur.astype(F32)
    for j in range(C_CONV - 1):
        shift = (t_in == t_out - (C_CONV - 1 - j)).astype(BF16)
        acc = acc + cw_ref[j:j + 1, :] * jnp.dot(shift, ext, preferred_element_type=F32)
    xbc = jax.nn.silu(acc)
    xs = xbc[:, :C_INNER]
    gs_w = C_GROUPS * C_STATE
    bm = xbc[:, C_INNER:C_INNER + gs_w].astype(BF16)
    cm = xbc[:, C_INNER + gs_w:].astype(BF16)

    row = lax.broadcasted_iota(jnp.int32, (q_len, q_len), 0)
    col = lax.broadcasted_iota(jnp.int32, (q_len, q_len), 1)
    tri = col <= row

    lane = lax.broadcasted_iota(jnp.int32, (1, LANES), 1)
    dt = jnp.where(lane < C_HEADS, jax.nn.softplus(sdt_ref[0] + dtb_ref[...]), 0.0)
    dta = dt * -jnp.exp(alog_ref[...])
    acs = _dot_mask_lhs(tri.astype(BF16), dta)
    acs_t = acs.T
    chunk_dec = jnp.exp(acs[q_len - 1:q_len, :])
    expand = ex_ref[...]
    dt_x = _dot_mask_rhs(dt, expand)
    acs_x = _dot_mask_rhs(acs, expand)
    xd = xs * dt_x
    xd_b = xd.astype(BF16)
    xdd = xd * jnp.exp(acs_x[q_len - 1:q_len, :] - acs_x)
    low_half = lax.broadcasted_iota(jnp.int32, (q_len, LANES), 1) < C_HEAD_DIM

    y_pairs, y_offs = [], []
    tdims = (((1,), (1,)), ((), ()))
    for g in range(C_GROUPS):
        b_g = bm[:, g * C_STATE:(g + 1) * C_STATE]
        c_g = cm[:, g * C_STATE:(g + 1) * C_STATE]
        cb = lax.dot_general(c_g, b_g, tdims, preferred_element_type=F32)
        h0 = g * C_HPG
        grp = slice(h0 * C_HEAD_DIM, (h0 + C_HPG) * C_HEAD_DIM)
        y_offs.append(lax.dot_general(c_g, hs_ref[grp, :].astype(BF16), tdims, preferred_element_type=F32))
        for h in range(h0, h0 + C_HPG, 2):
            xp = xd_b[:, h * C_HEAD_DIM:(h + 2) * C_HEAD_DIM]
            halves = []
            for hh in (h, h + 1):
                seg = jnp.exp(jnp.where(tri, acs[:, hh:hh + 1] - acs_t[hh:hh + 1, :], -jnp.inf))
                halves.append(jnp.dot((cb * seg).astype(BF16), xp, preferred_element_type=F32))
            y_pairs.append(jnp.where(low_half, halves[0], halves[1]))
        upd = jnp.dot(xdd[:, grp].T.astype(BF16), b_g, preferred_element_type=F32)
        for hh in range(C_HPG):
            h = h0 + hh
            ps = slice(h * C_HEAD_DIM, (h + 1) * C_HEAD_DIM)
            us = slice(hh * C_HEAD_DIM, (hh + 1) * C_HEAD_DIM)
            hs_ref[ps, :] = hs_ref[ps, :] * chunk_dec[0:1, h:h + 1] + upd[us, :]
    y = (jnp.concatenate(y_pairs, axis=1) + jnp.concatenate(y_offs, axis=1) * jnp.exp(acs_x)
         + dsk_ref[...] * xs)
    y = y * jax.nn.silu(z_ref[0].astype(F32))
    o_ref[0, :, :C_INNER] = _rms(y, sn_ref[...]).astype(o_ref.dtype)

    same = (row // D_CHUNK) == (col // D_CHUNK)
    tri2 = tri & same
    la = jnp.dot(sg_ref[0].astype(BF16), gkw_ref[...], preferred_element_type=F32) + gkb_ref[...]
    la = jax.nn.log_sigmoid(la) / D_GATE_NORM
    gcs = _dot_mask_lhs(tri2.astype(BF16), la)
    first = lax.broadcasted_iota(jnp.int32, (q_len, 1), 0) < D_CHUNK
    r_all = r_ref[0]
    for h in range(D_HEADS):
        ks = slice(h * D_HK, (h + 1) * D_HK)
        vs = slice(h * D_HV, (h + 1) * D_HV)
        g_h = gcs[:, ks]
        g_end0 = g_h[D_CHUNK - 1:D_CHUNK, :]
        g_end1 = g_h[q_len - 1:q_len, :]
        q_h = q_ref[0, :, ks].astype(F32) * (D_HK ** -0.5)
        k_h = k_ref[0, :, ks].astype(F32)
        v_h = v_ref[0, :, vs]
        q_dec = (q_h * jnp.exp(g_h)).astype(BF16)
        k_inv = (k_h * jnp.exp(-g_h)).astype(BF16)
        k_end = k_h * jnp.exp(jnp.where(first, g_end0, g_end1) - g_h)
        ke0 = jnp.where(first, k_end, 0.0).astype(BF16)
        ke1 = jnp.where(first, 0.0, k_end).astype(BF16)
        attn = lax.dot_general(q_dec, k_inv, (((1,), (1,)), ((), ())), preferred_element_type=F32)
        attn = jnp.where(tri2, attn, 0.0).astype(BF16)
        o = jnp.dot(attn, v_h, preferred_element_type=F32)
        v_t = v_h.astype(F32).T.astype(BF16)
        st_rows = slice(h * D_HV, (h + 1) * D_HV)
        s0 = gs_ref[st_rows, :]
        s1 = s0 * jnp.exp(g_end0) + jnp.dot(v_t, ke0, preferred_element_type=F32)
        s2 = s1 * jnp.exp(g_end1) + jnp.dot(v_t, ke1, preferred_element_type=F32)
        gs_ref[st_rows, :] = s2
        tdims = (((1,), (1,)), ((), ()))
        o0 = lax.dot_general(q_dec, s0.astype(BF16), tdims, preferred_element_type=F32)
        o1 = lax.dot_general(q_dec, s1.astype(BF16), tdims, preferred_element_type=F32)
        o = o + jnp.where(first, o0, o1)
        o = _rms(o, gn_ref[...]) * jax.nn.silu(r_all[:, vs].astype(F32))
        o_ref[0, :, C_INNER + h * D_HV:C_INNER + (h + 1) * D_HV] = o.astype(o_ref.dtype)


def _ssd_gla(z, q, k, v, r, xbc, side_dt, side_g, g_lane, p):
    b, s, _ = z.shape
    q_len = C_CHUNK
    halo = 16
    nb = SSD_BATCHES if b % SSD_BATCHES == 0 else 1
    chunk = lambda width: pl.BlockSpec((nb, q_len, width), lambda i, n: (i, n, 0))
    full = lambda shape: pl.BlockSpec(shape, lambda i, n: (0,) * len(shape))
    pad_lanes = lambda vec: jnp.zeros((1, LANES), F32).at[0, :vec.shape[0]].set(vec.astype(F32))
    gkw = jnp.zeros((LANES, D_KEY), F32).at[g_lane:g_lane + D_GATE_RANK].set(p["gk_w"]).astype(BF16)
    expand = jnp.asarray(np.arange(C_INNER)[None, :] // C_HEAD_DIM == np.arange(LANES)[:, None], BF16)
    return pl.pallas_call(
        _ssd_gla_kernel,
        grid=(b // nb, s // q_len),
        in_specs=[
            chunk(C_INNER), chunk(D_KEY), chunk(D_KEY), chunk(D_VAL), chunk(D_VAL), chunk(C_CONV_DIM),
            pl.BlockSpec((nb, halo, C_CONV_DIM), lambda i, n: (i, jnp.maximum(n * (q_len // halo) - 1, 0), 0)),
            chunk(LANES), chunk(LANES),
            full((C_CONV, C_CONV_DIM)), full((1, C_CONV_DIM)),
            full((1, LANES)), full((1, LANES)), full((1, C_INNER)), full((LANES, C_INNER)),
            full((1, C_INNER)), full((LANES, D_KEY)), full((1, D_KEY)), full((1, D_HV)),
        ],
        out_specs=pl.BlockSpec((nb, q_len, C_INNER + D_VAL), lambda i, n: (i, n, 0)),
        out_shape=jax.ShapeDtypeStruct((b, s, C_INNER + D_VAL), BF16),
        scratch_shapes=[pltpu.VMEM((nb, C_INNER, C_STATE), F32),
                        pltpu.VMEM((nb, D_VAL, D_HK), F32)],
        compiler_params=_cparams(("parallel", "arbitrary")),
        name="ssd_gla",
    )(z, q, k, v, r, xbc, xbc, side_dt, side_g,
      p["conv_w"].astype(F32), p["conv_b"].reshape(1, -1).astype(F32),
      pad_lanes(p["dt_bias"]), pad_lanes(p["a_log"]),
      jnp.repeat(p["d_skip"].astype(F32), C_HEAD_DIM).reshape(1, C_INNER), expand,
      p["ssd_norm"].reshape(1, -1).astype(F32), gkw, p["gk_b"].reshape(1, -1).astype(F32),
      p["gla_norm"].reshape(1, -1).astype(F32))


def _store_rows_tiled(ref, val):
    m, d = val.shape
    nc = d // LANES
    for c in range(nc):
        ref[pl.ds(c, m, stride=nc), :] = val[:, c * LANES:(c + 1) * LANES]


def _load_rows_tiled(ref, m, dtype=None):
    nc = ref.shape[0] // m
    parts = [ref[pl.ds(c, m, stride=nc), :] for c in range(nc)]
    if dtype is not None:
        parts = [p.astype(dtype) for p in parts]
    return jnp.concatenate(parts, axis=1)


def _pack_pairs(x):
    n = x.shape[1] // 2
    u = pltpu.bitcast(x.astype(BF16).astype(F32), jnp.uint32)
    return (u[:, :n] >> 16) | (u[:, n:] & jnp.uint32(0xFFFF0000))


def _unpack_pairs(u):
    lo = pltpu.bitcast(u << 16, F32).astype(BF16)
    hi = pltpu.bitcast(u & jnp.uint32(0xFFFF0000), F32).astype(BF16)
    return jnp.concatenate([lo, hi], axis=1)


RT_GATE, RT_EXPERT, RT_RANK = 0, 2, 4


def _route_block(lg, carry, earlier):
    m = lg.shape[0]
    lane = lax.broadcasted_iota(jnp.int32, (m, LANES), 1)
    lane_f = lane.astype(F32)
    none = float(LANES)
    neg = -jnp.inf
    first_max = lambda v, vmax: jnp.min(jnp.where(v == vmax, lane_f, none), axis=-1, keepdims=True)
    gl = jnp.where(lane < MOE_GROUPS, lg, neg)
    gmax = jnp.max(gl, axis=-1, keepdims=True)
    g_w = 1.0 / jnp.sum(jnp.exp(gl - gmax), axis=-1, keepdims=True)
    lo = MOE_GROUPS + first_max(gl, gmax) * MOE_EPG
    el = jnp.where((lane_f >= lo) & (lane_f < lo + MOE_EPG), lg, neg)
    emax = jnp.max(el, axis=-1, keepdims=True)
    esum = jnp.sum(jnp.exp(el - emax), axis=-1, keepdims=True)
    l0 = first_max(el, emax)
    el2 = jnp.where(lane_f == l0, neg, el)
    emax2 = jnp.max(el2, axis=-1, keepdims=True)
    l1 = first_max(el2, emax2)
    p0 = 1.0 / esum
    p1 = jnp.exp(emax2 - emax) / esum
    w0 = g_w * (p0 / (p0 + p1))
    w1 = g_w * (p1 / (p0 + p1))
    oh0 = lane_f == l0
    oh1 = lane_f == l1
    oh = (oh0 | oh1).astype(BF16)
    cum = jnp.dot(earlier, oh, preferred_element_type=F32) + carry
    rank0 = jnp.sum(jnp.where(oh0, cum, 0.0), axis=-1, keepdims=True)
    rank1 = jnp.sum(jnp.where(oh1, cum, 0.0), axis=-1, keepdims=True)
    carry = carry + jnp.sum(oh.astype(F32), axis=0, keepdims=True)
    rec = jnp.zeros((m, LANES), F32)
    for pos, val in ((RT_GATE, w0), (RT_GATE + 1, w1), (RT_EXPERT, l0 - MOE_GROUPS),
                     (RT_EXPERT + 1, l1 - MOE_GROUPS), (RT_RANK, rank0), (RT_RANK + 1, rank1)):
        rec = jnp.where(lane == pos, val, rec)
    return rec, carry


def _out_proj_kernel(n_parts, *refs):
    a_refs = refs[:n_parts]
    w_refs = refs[n_parts:2 * n_parts]
    h_ref, g_ref, wr_ref, br_ref, tri_ref, ho_ref, xt_ref, rt_ref, cnt_ref, carry_ref = refs[2 * n_parts:]

    @pl.when(pl.program_id(0) == 0)
    def _():
        carry_ref[...] = jnp.zeros_like(carry_ref)

    acc = h_ref[...]
    for a_ref, w_ref in zip(a_refs, w_refs):
        acc = acc + jnp.dot(a_ref[...], w_ref[...], preferred_element_type=F32)
    ho_ref[...] = acc
    xn = _rms(acc, g_ref[...])
    xt_ref[...] = xn.astype(xt_ref.dtype)
    x_hi, x_mid, _ = _split3(xn)
    wr = wr_ref[...]
    lg2 = jnp.dot(x_hi, wr, preferred_element_type=F32)
    lg = (lg2[:, :LANES] + lg2[:, LANES:] + jnp.dot(x_mid, wr[:, :LANES], preferred_element_type=F32)
          + br_ref[...])
    rec, carry = _route_block(lg, carry_ref[...], tri_ref[...])
    rt_ref[...] = rec
    carry_ref[...] = carry
    cnt_ref[0] = carry


def _out_proj(parts, w_parts, h, g, w_route, b_route, *, tm=512):
    t, d = h.shape
    tm = min(tm, t)
    nc = d // LANES
    row = lambda i: (i, 0)
    const = lambda i: (0, 0)
    in_specs = [pl.BlockSpec((tm, a.shape[1]), row) for a in parts]
    in_specs += [pl.BlockSpec(w.shape, const) for w in w_parts]
    in_specs += [pl.BlockSpec((tm, d), row), pl.BlockSpec((1, d), const),
                 pl.BlockSpec((d, 2 * LANES), const), pl.BlockSpec((1, LANES), const),
                 pl.BlockSpec((tm, tm), const)]
    earlier = jnp.asarray(np.tril(np.ones((tm, tm), np.float32), -1), BF16)
    return pl.pallas_call(
        functools.partial(_out_proj_kernel, len(parts)),
        grid=(t // tm,),
        in_specs=in_specs,
        out_specs=[pl.BlockSpec((tm, d), row), pl.BlockSpec((tm, d), row),
                   pl.BlockSpec((tm, LANES), row), pl.BlockSpec((1, 1, LANES), lambda i: (i, 0, 0))],
        out_shape=[jax.ShapeDtypeStruct((t, d), F32), jax.ShapeDtypeStruct((t, d), BF16),
                   jax.ShapeDtypeStruct((t, LANES), F32), jax.ShapeDtypeStruct((t // tm, 1, LANES), F32)],
        scratch_shapes=[pltpu.VMEM((1, LANES), F32)],
        compiler_params=_cparams(("arbitrary",)),
        name="out_proj",
    )(*parts, *w_parts, h, g.reshape(1, d), w_route, b_route, earlier)


LONG_RUN = 128


def _piece_sizes(max_rows, lo=1, hi=None):
    sizes = [1 << b for b in reversed(range(max_rows.bit_length()))]
    return [n for n in sizes if n >= lo and (hi is None or n < hi)]


def _for_each_piece(run, sizes, body):
    off = run // (2 * sizes[0]) * (2 * sizes[0])
    for n in sizes:
        hit = (run & n) != 0
        pl.when(hit)(functools.partial(body, off, n))
        off = off + jnp.where(hit, n, 0)


def _for_each_run(len_ref, tile, long_ref, max_rows, make_body):
    def sweep(sizes):
        def per_expert(e, first):
            run = len_ref[tile * MOE_EXPERTS + e]
            _for_each_piece(run, sizes, make_body(e, first))
            return first + run

        lax.fori_loop(0, MOE_EXPERTS, per_expert, jnp.int32(0))

    sweep(_piece_sizes(max_rows, hi=LONG_RUN))
    pl.when(long_ref[tile] != 0)(functools.partial(sweep, _piece_sizes(max_rows, lo=LONG_RUN)))


def _zero_fill_rows(rows_ref, z_ref, zsem, lo_ref, hi_ref, n_rows, nc):
    zb = z_ref.shape[0] // nc
    z_ref[...] = jnp.zeros_like(z_ref)
    assert MOE_TILE // 2 <= zb and MOE_TILE % zb == 0

    def piece(row0, n):
        return pltpu.make_async_copy(z_ref.at[pl.ds(0, n * nc)], rows_ref.at[pl.ds(row0 * nc, n * nc)], zsem)

    def sweep(issue):
        def per_expert(e, carry):
            lo = lo_ref[e]

            def one(off, n):
                piece(lo + off, n).start() if issue else piece(lo + off, n).wait()

            _for_each_piece(hi_ref[e] - lo, _piece_sizes(MOE_TILE // 2), one)
            return carry

        def per_block(i, carry):
            piece(i * zb, zb).start() if issue else piece(i * zb, zb).wait()
            return carry

        lax.fori_loop(0, MOE_EXPERTS, per_expert, 0)
        lax.fori_loop(hi_ref[MOE_EXPERTS - 1] // zb, n_rows // zb, per_block, 0)

    sweep(True)
    sweep(False)


def _tile_positions(rt, pos_base):
    lane_f = lax.broadcasted_iota(jnp.int32, rt.shape, 1).astype(F32)
    out = []
    for k in range(MOE_TOPK):
        e_lane = rt[:, RT_EXPERT + k:RT_EXPERT + k + 1] + MOE_GROUPS
        out.append(rt[:, RT_RANK + k:RT_RANK + k + 1]
                   + jnp.sum(jnp.where(lane_f == e_lane, pos_base, 0.0), axis=-1, keepdims=True))
    return out


def _moe_dispatch_kernel(dst_ref, len_ref, long_ref, lo_ref, hi_ref, x_ref, rt_ref, pb_ref, rows_ref,
                         sbuf, z_ref, sem, zsem, *, tm, nc, n_steps, n_rows):
    i = pl.program_id(0)
    rows = MOE_TOPK * tm
    slot = lax.rem(i, 2)

    def wait_slot(s):
        pltpu.make_async_copy(sbuf.at[s], rows_ref.at[pl.ds(0, rows * nc)], sem.at[s]).wait()

    @pl.when(i == 0)
    def _():
        _zero_fill_rows(rows_ref, z_ref, zsem, lo_ref, hi_ref, n_rows, nc)

    @pl.when(i >= 2)
    def _():
        wait_slot(slot)

    pos = _tile_positions(rt_ref[...], pb_ref[0])
    lane = lax.broadcasted_iota(jnp.int32, (tm, LANES), 1)
    pos_t = jnp.where(lane == 0, pos[0], jnp.where(lane == 1, pos[1], 0.0)).T
    p_iota = lax.broadcasted_iota(jnp.int32, (rows, tm), 0).astype(F32)
    place = ((p_iota == pos_t[0:1, :]) | (p_iota == pos_t[1:2, :])).astype(BF16)
    _store_rows_tiled(sbuf.at[slot], _pack_pairs(jnp.dot(place, x_ref[...], preferred_element_type=F32)))

    def sender(e, src):
        dst = dst_ref[i * MOE_EXPERTS + e]

        def send(off, n):
            pltpu.make_async_copy(sbuf.at[slot, pl.ds((src + off) * nc, n * nc)],
                                  rows_ref.at[pl.ds((dst + off) * nc, n * nc)], sem.at[slot]).start()

        return send

    _for_each_run(len_ref, i, long_ref, rows, sender)

    @pl.when(i == n_steps - 1)
    def _():
        wait_slot(slot)
        if n_steps > 1:
            wait_slot(1 - slot)


def _moe_dispatch(x, route, pos_base, seg_dst, seg_len, seg_long, pad_lo, pad_hi, n_rows, *, tm):
    t, d = x.shape
    nc = d // (2 * LANES)
    n_steps = t // tm
    row = lambda i, *_: (i, 0)
    return pl.pallas_call(
        functools.partial(_moe_dispatch_kernel, tm=tm, nc=nc, n_steps=n_steps, n_rows=n_rows),
        grid_spec=pltpu.PrefetchScalarGridSpec(
            num_scalar_prefetch=5, grid=(n_steps,),
            in_specs=[pl.BlockSpec((tm, d), row), pl.BlockSpec((tm, LANES), row),
                      pl.BlockSpec((1, 1, LANES), lambda i, *_: (i, 0, 0))],
            out_specs=pl.BlockSpec(memory_space=pl.ANY),
            scratch_shapes=[pltpu.VMEM((2, MOE_TOPK * tm * nc, LANES), jnp.uint32),
                            pltpu.VMEM((MOE_TILE // 2 * nc, LANES), jnp.uint32),
                            pltpu.SemaphoreType.DMA((2,)), pltpu.SemaphoreType.DMA(())]),
        out_shape=jax.ShapeDtypeStruct((n_rows * nc, LANES), jnp.uint32),
        compiler_params=pltpu.CompilerParams(dimension_semantics=("arbitrary",), has_side_effects=True,
                                             vmem_limit_bytes=VMEM_LIMIT),
        name="moe_dispatch",
    )(seg_dst, seg_len, seg_long, pad_lo, pad_hi, x, route, pos_base)


def _moe_kernel(te_ref, nu_ref, x_ref, wg_ref, wu_ref, wd_ref, y_ref):
    i = pl.program_id(0)

    @pl.when(i >= nu_ref[0])
    def _():
        y_ref[...] = jnp.zeros_like(y_ref)

    @pl.when(i < nu_ref[0])
    def _():
        x = _unpack_pairs(_load_rows_tiled(x_ref, MOE_TILE))
        gate = jnp.dot(x, wg_ref[...].astype(BF16), preferred_element_type=F32)
        up = jnp.dot(x, wu_ref[...].astype(BF16), preferred_element_type=F32)
        act = (jax.nn.silu(gate) * up).astype(BF16)
        y = jnp.dot(act, wd_ref[...].astype(BF16), preferred_element_type=F32)
        _store_rows_tiled(y_ref, _pack_pairs(y))


def _moe_experts(x_rows, tile_expert, n_used, w_gate, w_up, w_down, layer):
    d, ff = w_gate.shape[-2:]
    nc = d // (2 * LANES)
    n_tiles = x_rows.shape[0] // (MOE_TILE * nc)
    live = lambda i, nu: jnp.minimum(i, nu[0] - 1)
    w_spec = lambda a, b: pl.BlockSpec((None, None, a, b), lambda i, te, nu: (layer, te[live(i, nu)], 0, 0))
    return pl.pallas_call(
        _moe_kernel,
        grid_spec=pltpu.PrefetchScalarGridSpec(
            num_scalar_prefetch=2, grid=(n_tiles,),
            in_specs=[pl.BlockSpec((MOE_TILE * nc, LANES), lambda i, te, nu: (live(i, nu), 0)),
                      w_spec(d, ff), w_spec(d, ff), w_spec(ff, d)],
            out_specs=pl.BlockSpec((MOE_TILE * nc, LANES), lambda i, te, nu: (i, 0))),
        out_shape=jax.ShapeDtypeStruct(x_rows.shape, x_rows.dtype),
        compiler_params=_cparams(("arbitrary",)),
        name="moe_experts",
    )(tile_expert, n_used, x_rows, w_gate, w_up, w_down)


def _moe_combine_kernel(src_ref, len_ref, long_ref, h_ref, rt_ref, pb_ref, g_ref, y_hbm, o_ref, ybuf, sem,
                        *, tm, nc, final):
    i = pl.program_id(0)
    n_steps = pl.num_programs(0)
    rows = MOE_TOPK * tm

    def fetch(tile, slot):
        def receiver(e, dst):
            src = src_ref[tile * MOE_EXPERTS + e]

            def recv(off, n):
                pltpu.make_async_copy(y_hbm.at[pl.ds((src + off) * nc, n * nc)],
                                      ybuf.at[slot, pl.ds((dst + off) * nc, n * nc)], sem.at[slot]).start()

            return recv

        _for_each_run(len_ref, tile, long_ref, rows, receiver)

    slot = lax.rem(i, 2)

    @pl.when(i == 0)
    def _():
        fetch(0, 0)

    @pl.when(i + 1 < n_steps)
    def _():
        fetch(i + 1, 1 - slot)

    pltpu.make_async_copy(y_hbm.at[pl.ds(0, rows * nc)], ybuf.at[slot], sem.at[slot]).wait()
    y = _unpack_pairs(_load_rows_tiled(ybuf.at[slot], rows))
    rt = rt_ref[...]
    pos_f = lax.broadcasted_iota(jnp.int32, (tm, rows), 1).astype(F32)
    pick = jnp.zeros((tm, rows), F32)
    for k, pos in enumerate(_tile_positions(rt, pb_ref[0])):
        pick = jnp.where(pos_f == pos, rt[:, RT_GATE + k:RT_GATE + k + 1], pick)
    out = h_ref[...] + jnp.dot(pick.astype(BF16), y, preferred_element_type=F32)
    o_ref[...] = _rms(out, g_ref[...]) if final else out


def _moe_combine_rows(h, y_rows, route, seg_src, seg_len, seg_long, pos_base, g, *, tm, final):
    t, d = h.shape
    nc = d // (2 * LANES)
    row = lambda i, *_: (i, 0)
    return pl.pallas_call(
        functools.partial(_moe_combine_kernel, tm=tm, nc=nc, final=final),
        grid_spec=pltpu.PrefetchScalarGridSpec(
            num_scalar_prefetch=3, grid=(t // tm,),
            in_specs=[pl.BlockSpec((tm, d), row), pl.BlockSpec((tm, LANES), row),
                      pl.BlockSpec((1, 1, LANES), lambda i, *_: (i, 0, 0)),
                      pl.BlockSpec((1, d), lambda i, *_: (0, 0)),
                      pl.BlockSpec(memory_space=pl.ANY)],
            out_specs=pl.BlockSpec((tm, d), row),
            scratch_shapes=[pltpu.VMEM((2, MOE_TOPK * tm * nc, LANES), y_rows.dtype),
                            pltpu.SemaphoreType.DMA((2,))]),
        out_shape=jax.ShapeDtypeStruct((t, d), F32),
        compiler_params=_cparams(("arbitrary",)),
        name="moe_combine",
    )(seg_src, seg_len, seg_long, h, route, pos_base, g.reshape(1, d), y_rows)


def _moe(h, x, route, tile_counts, w_gate, w_up, w_down, layer, g, *, final):
    t = route.shape[0]
    n_tt = tile_counts.shape[0]
    tm = t // n_tt
    after = tile_counts[:, 0, MOE_GROUPS:MOE_GROUPS + MOE_EXPERTS].astype(jnp.int32)
    before = jnp.concatenate([jnp.zeros((1, MOE_EXPERTS), jnp.int32), after[:-1]], axis=0)
    cnt = after[-1]
    padded = (cnt + MOE_TILE - 1) // MOE_TILE * MOE_TILE
    pad_ends = jnp.cumsum(padded)
    starts = (pad_ends - padded).astype(jnp.int32)
    n_tiles = (t * MOE_TOPK + MOE_EXPERTS * (MOE_TILE - 1)) // MOE_TILE
    tile_start = jnp.arange(n_tiles, dtype=jnp.int32) * MOE_TILE
    tile_expert = jnp.minimum(jnp.sum(tile_start[:, None] >= pad_ends[None, :], axis=1),
                              MOE_EXPERTS - 1).astype(jnp.int32)
    n_used = (pad_ends[-1] // MOE_TILE).astype(jnp.int32).reshape(1)
    n_rows = n_tiles * MOE_TILE
    seg_len = (after - before).reshape(-1)
    seg_off = jnp.cumsum(after - before, axis=1) - (after - before)
    seg_row = (starts[None, :] + before).reshape(-1)
    pos_base = jnp.zeros((n_tt, 1, LANES), F32).at[:, 0, MOE_GROUPS:MOE_GROUPS + MOE_EXPERTS].set(
        (seg_off - before).astype(F32))
    seg_long = jnp.any(after - before >= LONG_RUN, axis=1).astype(jnp.int32)
    x_rows = _moe_dispatch(x, route, pos_base, seg_row, seg_len, seg_long, starts + cnt,
                           pad_ends.astype(jnp.int32), n_rows, tm=tm)
    y_rows = _moe_experts(x_rows, tile_expert, n_used, w_gate, w_up, w_down, layer)
    return _moe_combine_rows(h, y_rows, route, seg_row, seg_len, seg_long, pos_base, g, tm=tm, final=final)


def _router_weights(w_group, b_group, w_router, b_router):
    d = w_group.shape[0]
    w = jnp.zeros((d, LANES), F32)
    w = w.at[:, :MOE_GROUPS].set(w_group).at[:, MOE_GROUPS:MOE_GROUPS + MOE_EXPERTS].set(w_router)
    b = jnp.zeros((1, LANES), F32)
    b = b.at[0, :MOE_GROUPS].set(b_group).at[0, MOE_GROUPS:MOE_GROUPS + MOE_EXPERTS].set(b_router)
    w_hi = w.astype(BF16)
    w_mid = (w - w_hi.astype(F32)).astype(BF16)
    return jnp.concatenate([w_hi, w_mid], axis=1), b


def kernel(x, norm_mix, norm_moe, norm_final, even_w_in, even_sinks, even_forget_bias, even_w_out,
           odd_w_in, odd_conv_w, odd_conv_b, odd_dt_bias, odd_a_log, odd_d_skip, odd_ssd_norm,
           odd_gk_w, odd_gk_b, odd_gla_norm, odd_w_out, moe_w_group, moe_b_group, moe_w_router,
           moe_b_router, moe_w_gate, moe_w_up, moe_w_down):
    b, s, d = x.shape
    t = b * s
    depth = norm_mix.shape[0]
    h = x.reshape(t, d)
    for layer in range(depth):
        i = layer // 2
        if layer % 2 == 0:
            w = even_w_in[i]
            n_ab = (A_Q_HEADS + 2 * A_KV_HEADS + 3 * B_HEADS) * HEAD_DIM
            w_main = w[:, :n_ab].astype(BF16)
            w_aux = jnp.zeros((d, LANES), F32).at[:, :B_HEADS].set(w[:, n_ab:]).astype(BF16)
            proj, f_aux = _norm_proj(h, norm_mix[layer], [w_main, w_aux],
                                     ((0, 0, n_ab, BF16), (1, 0, LANES, F32)), tm=512)
            proj = proj.reshape(b, s, -1)
            out_a = _swa(proj, even_sinks[i])
            c, ct = _fox_gate(f_aux.reshape(b, s, LANES), even_forget_bias[i])
            out_b = _fox(proj, c, ct)
            n_ha = A_Q_HEADS * HEAD_DIM
            w_out = even_w_out[i].astype(BF16)
            parts = [out_a.reshape(t, -1), out_b.reshape(t, -1)]
            w_parts = [w_out[:n_ha], w_out[n_ha:]]
        else:
            w = odd_w_in[i]
            o_z, o_xbc = 0, C_INNER
            o_dt = o_xbc + C_CONV_DIM
            o_q = o_dt + C_HEADS
            o_k = o_q + D_KEY
            o_v = o_k + D_KEY
            o_g = o_v + D_VAL
            o_r = o_g + D_GATE_RANK
            n_b = w.shape[1] - o_dt
            w_a = w[:, :o_dt].astype(BF16)
            w_b = jnp.pad(w[:, o_dt:].astype(BF16), ((0, 0), (0, -n_b % LANES)))
            g_win = (o_g - o_dt) // LANES * LANES
            plan = ((0, o_z, C_INNER, BF16), (1, o_q - o_dt, D_KEY, BF16), (1, o_k - o_dt, D_KEY, BF16),
                    (1, o_v - o_dt, D_VAL, BF16), (1, o_r - o_dt, D_VAL, BF16), (0, o_xbc, C_CONV_DIM, BF16),
                    (1, 0, LANES, F32), (1, g_win, LANES, F32))
            outs = _norm_proj(h, norm_mix[layer], [w_a, w_b], plan, tm=512)
            params = dict(conv_w=odd_conv_w[i], conv_b=odd_conv_b[i], dt_bias=odd_dt_bias[i], a_log=odd_a_log[i],
                          d_skip=odd_d_skip[i], ssd_norm=odd_ssd_norm[i], gk_w=odd_gk_w[i], gk_b=odd_gk_b[i],
                          gla_norm=odd_gla_norm[i])
            mixed = _ssd_gla(*[o.reshape(b, s, -1) for o in outs], o_g - o_dt - g_win, params)
            parts = [mixed.reshape(t, -1)]
            w_parts = [odd_w_out[i].astype(BF16)]
        w_route, b_route = _router_weights(moe_w_group[layer], moe_b_group[layer],
                                           moe_w_router[layer], moe_b_router[layer])
        h, x_tiled, route, tile_counts = _out_proj(parts, w_parts, h, norm_moe[layer], w_route, b_route)
        h = _moe(h, x_tiled, route, tile_counts, moe_w_gate, moe_w_up, moe_w_down, layer, norm_final,
                 final=layer == depth - 1)
    out = h
    return out.reshape(b, s, d)
```

```python
import functools

import numpy as np
import jax
import jax.numpy as jnp
from jax import lax
from jax.experimental import pallas as pl
from jax.experimental.pallas import tpu as pltpu

F32 = jnp.float32
BF16 = jnp.bfloat16

RMS_EPS = 1e-6
HEAD_DIM = 64
A_Q_HEADS = 8
A_KV_HEADS = 2
A_GROUP = A_Q_HEADS // A_KV_HEADS
A_WINDOW = 128
B_HEADS = 8
C_HEADS = 16
C_HEAD_DIM = 64
C_INNER = C_HEADS * C_HEAD_DIM
C_GROUPS = 2
C_HPG = C_HEADS // C_GROUPS
C_STATE = 128
C_CONV = 4
C_CHUNK = 128
C_CONV_DIM = C_INNER + 2 * C_GROUPS * C_STATE
D_HEADS = 4
D_HK = 128
D_HV = 256
D_KEY = D_HEADS * D_HK
D_VAL = D_HEADS * D_HV
D_GATE_RANK = 16
D_GATE_NORM = 16.0
D_CHUNK = 64
MOE_GROUPS = 4
MOE_EPG = 8
MOE_EXPERTS = MOE_GROUPS * MOE_EPG
MOE_TOPK = 2

LANES = 128
VMEM_LIMIT = 48 * 1024 * 1024
MOE_TILE = 512


def _cparams(sem):
    return pltpu.CompilerParams(dimension_semantics=sem, vmem_limit_bytes=VMEM_LIMIT)


def _rms(x, g):
    ms = jnp.mean(x * x, axis=-1, keepdims=True)
    return x * lax.rsqrt(ms + RMS_EPS) * g


def _norm_proj_kernel(n_w, plan, *refs):
    x_ref, g_ref = refs[:2]
    w_refs = refs[2:2 + n_w]
    o_refs = refs[2 + n_w:2 + n_w + len(plan)]
    res_refs = refs[2 + n_w + len(plan):]
    xn = _rms(x_ref[...], g_ref[...]).astype(BF16)
    for w_ref, res_ref in zip(w_refs, res_refs):
        res_ref[...] = jnp.dot(xn, w_ref[...], preferred_element_type=F32)
    for o_ref, (wi, start, width, _) in zip(o_refs, plan):
        o_ref[...] = res_refs[wi][:, start:start + width].astype(o_ref.dtype)


def _norm_proj(x, g, weights, plan, *, tm):
    t, d = x.shape
    tm = min(tm, t)
    row = lambda i: (i, 0)
    const = lambda i: (0, 0)
    in_specs = [pl.BlockSpec((tm, d), row), pl.BlockSpec((1, d), const)]
    in_specs += [pl.BlockSpec(w.shape, const, pipeline_mode=pl.Buffered(1)) for w in weights]
    return pl.pallas_call(
        functools.partial(_norm_proj_kernel, len(weights), plan),
        grid=(t // tm,),
        in_specs=in_specs,
        out_specs=[pl.BlockSpec((tm, width), row) for _, _, width, _ in plan],
        out_shape=[jax.ShapeDtypeStruct((t, width), dtype) for _, _, width, dtype in plan],
        scratch_shapes=[pltpu.VMEM((tm, w.shape[1]), F32) for w in weights],
        compiler_params=_cparams(("parallel",)),
        name="norm_proj",
    )(x, g.reshape(1, d), *weights)


SWA_QBLOCKS = 8


def _swa_kernel(sink_ref, slope_ref, q_ref, kp_ref, kc_ref, vp_ref, vc_ref, o_ref):
    n = pl.program_id(1)
    blk = A_WINDOW
    wide = A_GROUP * blk
    key = lax.broadcasted_iota(jnp.int32, (2 * blk, wide), 0)
    qry = lax.broadcasted_iota(jnp.int32, (2 * blk, wide), 1) % blk
    dist = blk + qry - key
    in_window = (dist >= 0) & (dist < A_WINDOW)
    distf = dist.astype(F32)
    nt = (((1,), (1,)), ((), ()))
    k_all = jnp.concatenate([kp_ref[0], kc_ref[0]], axis=0)
    v_all = jnp.concatenate([vp_ref[0], vc_ref[0]], axis=1)
    units = [(j, kh) for j in range(SWA_QBLOCKS) for kh in range(A_KV_HEADS)]
    scores = []
    for j, kh in units:
        k = k_all[j * blk:(j + 2) * blk, kh * HEAD_DIM:(kh + 1) * HEAD_DIM]
        q = jnp.concatenate([q_ref[0, j * blk:(j + 1) * blk,
                                   (kh * A_GROUP + g) * HEAD_DIM:(kh * A_GROUP + g + 1) * HEAD_DIM]
                             for g in range(A_GROUP)], axis=0)
        scores.append(lax.dot_general(k, q, nt, preferred_element_type=F32))
    probs = []
    for (j, kh), s in zip(units, scores):
        valid = in_window & ((key >= blk) | (n * SWA_QBLOCKS + j > 0))
        s = s * (HEAD_DIM ** -0.5) - slope_ref[kh:kh + 1, :] * distf
        s = jnp.where(valid, s, -jnp.inf)
        sink = sink_ref[kh:kh + 1, :]
        m = jnp.maximum(jnp.max(s, axis=0, keepdims=True), sink)
        p = jnp.exp(s - m)
        probs.append((p.astype(BF16), jnp.sum(p, axis=0, keepdims=True) + jnp.exp(sink - m)))
    for j in range(SWA_QBLOCKS):
        outs = []
        for kh in range(A_KV_HEADS):
            p, denom = probs[j * A_KV_HEADS + kh]
            v_t = v_all[kh * HEAD_DIM:(kh + 1) * HEAD_DIM, j * blk:(j + 2) * blk]
            o_t = jnp.dot(v_t, p, preferred_element_type=F32) / denom
            outs += [o_t[:, g * blk:(g + 1) * blk] for g in range(A_GROUP)]
        o_ref[0, j * blk:(j + 1) * blk, :] = jnp.concatenate(outs, axis=0).T.astype(o_ref.dtype)


def _swa(proj, sinks):
    b, s, _ = proj.shape
    blk = A_WINDOW
    qw = A_Q_HEADS * HEAD_DIM
    kw = A_KV_HEADS * HEAD_DIM
    k_blk = qw // kw
    v_t = proj[:, :, qw + kw:qw + 2 * kw].transpose(0, 2, 1)
    per_lane = lambda vec: jnp.repeat(vec.astype(F32), blk).reshape(A_KV_HEADS, A_GROUP * blk)
    slopes = jnp.asarray(2.0 ** (-8.0 * np.arange(1, A_Q_HEADS + 1) / A_Q_HEADS), F32)
    tq = SWA_QBLOCKS * blk
    prev = lambda n: jnp.maximum(n * SWA_QBLOCKS - 1, 0)
    full = pl.BlockSpec((A_KV_HEADS, A_GROUP * blk), lambda i, n: (0, 0))
    return pl.pallas_call(
        _swa_kernel,
        grid=(b, s // tq),
        in_specs=[
            full, full,
            pl.BlockSpec((1, tq, qw), lambda i, n: (i, n, 0)),
            pl.BlockSpec((1, blk, kw), lambda i, n: (i, prev(n), k_blk)),
            pl.BlockSpec((1, tq, kw), lambda i, n: (i, n, k_blk)),
            pl.BlockSpec((1, kw, blk), lambda i, n: (i, 0, prev(n))),
            pl.BlockSpec((1, kw, tq), lambda i, n: (i, 0, n)),
        ],
        out_specs=pl.BlockSpec((1, tq, qw), lambda i, n: (i, n, 0)),
        out_shape=jax.ShapeDtypeStruct((b, s, qw), BF16),
        compiler_params=_cparams(("parallel", "parallel")),
        name="swa",
    )(per_lane(sinks), per_lane(slopes), proj, proj, proj, v_t, v_t)


def _tril(n, dtype=F32):
    r = lax.broadcasted_iota(jnp.int32, (n, n), 0)
    c = lax.broadcasted_iota(jnp.int32, (n, n), 1)
    return (c <= r).astype(dtype)


def _split3(x):
    hi = x.astype(BF16)
    r = x - hi.astype(F32)
    mid = r.astype(BF16)
    return hi, mid, (r - mid.astype(F32)).astype(BF16)


def _dot_mask_lhs(mask, x):
    return sum(jnp.dot(mask, part, preferred_element_type=F32) for part in _split3(x))


def _dot_mask_rhs(x, mask):
    return sum(jnp.dot(part, mask, preferred_element_type=F32) for part in _split3(x))


def _fox_gate_kernel(f_ref, b_ref, c_ref, ct_ref):
    tri = _tril(LANES, BF16)
    carry = jnp.zeros((1, LANES), F32)
    for n in range(f_ref.shape[1] // LANES):
        rows = slice(n * LANES, (n + 1) * LANES)
        lf = jax.nn.log_sigmoid(f_ref[0, rows, :] + b_ref[...])
        cs = _dot_mask_lhs(tri, lf) + carry
        carry = cs[LANES - 1:LANES, :]
        c_ref[0, rows, :] = cs
        ct_ref[0, n] = cs.T[:B_HEADS, :]


def _fox_gate(f_aux, bias):
    b, s, _ = f_aux.shape
    nb = s // LANES
    bias_p = jnp.zeros((1, LANES), F32).at[0, :B_HEADS].set(bias.astype(F32))
    return pl.pallas_call(
        _fox_gate_kernel,
        grid=(b,),
        in_specs=[pl.BlockSpec((1, s, LANES), lambda i: (i, 0, 0)),
                  pl.BlockSpec((1, LANES), lambda i: (0, 0))],
        out_specs=[pl.BlockSpec((1, s, LANES), lambda i: (i, 0, 0)),
                   pl.BlockSpec((1, nb, B_HEADS, LANES), lambda i: (i, 0, 0, 0))],
        out_shape=[jax.ShapeDtypeStruct((b, s, LANES), F32),
                   jax.ShapeDtypeStruct((b, nb, B_HEADS, LANES), F32)],
        compiler_params=_cparams(("parallel",)),
        name="fox_gate",
    )(f_aux, bias_p)


def _fox_kernel(q0_ref, q1_ref, k0_ref, k1_ref, vt_ref, c_ref, ctq_ref, o_ref, *, tq, heads_per_step):
    qi = pl.program_id(1)
    sub = tq // LANES
    key = lax.broadcasted_iota(jnp.int32, (tq, tq), 0)
    qry = lax.broadcasted_iota(jnp.int32, (tq, tq), 1)
    causal = key <= qry
    nt = (((1,), (1,)), ((), ()))
    half = B_HEADS // 2
    q_refs, k_refs = (q0_ref, q1_ref), (k0_ref, k1_ref)
    outs = []
    for h0 in range(0, B_HEADS, heads_per_step):
        heads = list(range(h0, h0 + heads_per_step))
        hsl = [slice(h * HEAD_DIM, (h + 1) * HEAD_DIM) for h in heads]
        lsl = [slice((h % half) * HEAD_DIM, (h % half + 1) * HEAD_DIM) for h in heads]
        qs = [q_refs[h // half][0, :, ls] * (HEAD_DIM ** -0.5)
              for h, ls in zip(heads, lsl)]
        cqs = [jnp.concatenate([ctq_ref[0, u, h:h + 1, :] for u in range(sub)], axis=1) for h in heads]

        def step(j, carry, masked, heads=heads, hsl=hsl, lsl=lsl, qs=qs, cqs=cqs):
            start = pl.multiple_of(j * tq, tq)
            sts = [lax.dot_general(k_refs[h // half][0, pl.ds(start, tq), ls], q, nt,
                                   preferred_element_type=F32)
                   for h, ls, q in zip(heads, lsl, qs)]
            ps, stats = [], []
            for idx, h in enumerate(heads):
                m, l, _ = carry[3 * idx:3 * idx + 3]
                ck = c_ref[0, pl.ds(start, tq), h:h + 1]
                st = (sts[idx] - ck) + cqs[idx]
                if masked:
                    st = jnp.where(causal, st, -jnp.inf)
                m_new = jnp.maximum(m, jnp.max(st, axis=0, keepdims=True))
                alpha = jnp.exp(m - m_new)
                p = jnp.exp(st - m_new)
                stats.append((m_new, alpha, alpha * l + jnp.sum(p, axis=0, keepdims=True)))
                ps.append(p.astype(BF16))
            new = []
            for idx in range(len(heads)):
                m_new, alpha, l = stats[idx]
                pv = jnp.dot(vt_ref[0, j, hsl[idx], :], ps[idx], preferred_element_type=F32)
                new += [m_new, l, alpha * carry[3 * idx + 2] + pv]
            return tuple(new)

        init = (jnp.full((1, tq), -jnp.inf, F32), jnp.zeros((1, tq), F32),
                jnp.zeros((HEAD_DIM, tq), F32)) * heads_per_step
        carry = lax.fori_loop(0, qi, functools.partial(step, masked=False), init)
        carry = step(qi, carry, True)
        for idx in range(heads_per_step):
            outs.append(carry[3 * idx + 2] / carry[3 * idx + 1])
    o_ref[0] = jnp.concatenate(outs, axis=0).T.astype(o_ref.dtype)


def _fox(proj, c, ct, *, tq=256, heads_per_step=8):
    b, s, _ = proj.shape
    w = B_HEADS * HEAD_DIM
    nk = s // tq
    sub = tq // LANES
    hw = w // 2
    base = (A_Q_HEADS + 2 * A_KV_HEADS) * HEAD_DIM
    qb, kb = base // hw, (base + w) // hw
    v_t = proj[:, :, base + 2 * w:base + 3 * w].reshape(b, nk, tq, w).transpose(0, 1, 3, 2)
    return pl.pallas_call(
        functools.partial(_fox_kernel, tq=tq, heads_per_step=heads_per_step),
        grid=(b, s // tq),
        in_specs=[
            pl.BlockSpec((1, tq, hw), lambda i, n: (i, n, qb)),
            pl.BlockSpec((1, tq, hw), lambda i, n: (i, n, qb + 1)),
            pl.BlockSpec((1, s, hw), lambda i, n: (i, 0, kb)),
            pl.BlockSpec((1, s, hw), lambda i, n: (i, 0, kb + 1)),
            pl.BlockSpec((1, nk, w, tq), lambda i, n: (i, 0, 0, 0)),
            pl.BlockSpec((1, s, LANES), lambda i, n: (i, 0, 0)),
            pl.BlockSpec((1, sub, B_HEADS, LANES), lambda i, n: (i, n, 0, 0)),
        ],
        out_specs=pl.BlockSpec((1, tq, w), lambda i, n: (i, n, 0)),
        out_shape=jax.ShapeDtypeStruct((b, s, w), BF16),
        compiler_params=_cparams(("parallel", "parallel")),
        name="fox",
    )(proj, proj, proj, proj, v_t, c, ct)


SSD_BATCHES = 4


def _ssd_gla_kernel(*refs):
    n_data, n_par = 9, 10
    data, params = refs[:n_data], refs[n_data:n_data + n_par]
    o_ref, hs_ref, gs_ref = refs[n_data + n_par:]

    @pl.when(pl.program_id(1) == 0)
    def _():
        hs_ref[...] = jnp.zeros_like(hs_ref)
        gs_ref[...] = jnp.zeros_like(gs_ref)

    for bb in range(o_ref.shape[0]):
        one = lambda ref: ref.at[pl.ds(bb, 1)]
        _ssd_gla_chunk(*[one(ref) for ref in data], *params, one(o_ref), hs_ref.at[bb], gs_ref.at[bb])


def _ssd_gla_chunk(z_ref, q_ref, k_ref, v_ref, r_ref, xc_ref, xp_ref, sdt_ref, sg_ref,
                   cw_ref, cb_ref, dtb_ref, alog_ref, dsk_ref, ex_ref, sn_ref, gkw_ref, gkb_ref, gn_ref,
                   o_ref, hs_ref, gs_ref):
    c = pl.program_id(1)
    q_len = C_CHUNK
    halo = xp_ref.shape[1]

    prev = xp_ref[0]
    cur = xc_ref[0]
    ext = jnp.concatenate([jnp.where(c > 0, prev, jnp.zeros_like(prev)), cur], axis=0)
    t_out = lax.broadcasted_iota(jnp.int32, (q_len, halo + q_len), 0)
    t_in = lax.broadcasted_iota(jnp.int32, (q_len, halo + q_len), 1) - halo
    acc = cb_ref[...] + cw_ref[C_CONV - 1:C_CONV, :] * cur.astype(F32)
    for j in range(C_CONV - 1):
        shift = (t_in == t_out - (C_CONV - 1 - j)).astype(BF16)
        acc = acc + cw_ref[j:j + 1, :] * jnp.dot(shift, ext, preferred_element_type=F32)
    xbc = jax.nn.silu(acc)
    xs = xbc[:, :C_INNER]
    gs_w = C_GROUPS * C_STATE
    bm = xbc[:, C_INNER:C_INNER + gs_w].astype(BF16)
    cm = xbc[:, C_INNER + gs_w:].astype(BF16)

    row = lax.broadcasted_iota(jnp.int32, (q_len, q_len), 0)
    col = lax.broadcasted_iota(jnp.int32, (q_len, q_len), 1)
    tri = col <= row

    lane = lax.broadcasted_iota(jnp.int32, (1, LANES), 1)
    dt = jnp.where(lane < C_HEADS, jax.nn.softplus(sdt_ref[0] + dtb_ref[...]), 0.0)
    dta = dt * -jnp.exp(alog_ref[...])
    acs = _dot_mask_lhs(tri.astype(BF16), dta)
    acs_t = acs.T
    chunk_dec = jnp.exp(acs[q_len - 1:q_len, :])
    expand = ex_ref[...]
    dt_x = _dot_mask_rhs(dt, expand)
    acs_x = _dot_mask_rhs(acs, expand)
    xd = xs * dt_x
    xd_b = xd.astype(BF16)
    xdd = xd * jnp.exp(acs_x[q_len - 1:q_len, :] - acs_x)
    low_half = lax.broadcasted_iota(jnp.int32, (q_len, LANES), 1) < C_HEAD_DIM

    y_pairs, y_offs = [], []
    tdims = (((1,), (1,)), ((), ()))
    for g in range(C_GROUPS):
        b_g = bm[:, g * C_STATE:(g + 1) * C_STATE]
        c_g = cm[:, g * C_STATE:(g + 1) * C_STATE]
        cb = lax.dot_general(c_g, b_g, tdims, preferred_element_type=F32)
        h0 = g * C_HPG
        grp = slice(h0 * C_HEAD_DIM, (h0 + C_HPG) * C_HEAD_DIM)
        y_offs.append(lax.dot_general(c_g, hs_ref[grp, :].astype(BF16), tdims, preferred_element_type=F32))
        for h in range(h0, h0 + C_HPG, 2):
            xp = xd_b[:, h * C_HEAD_DIM:(h + 2) * C_HEAD_DIM]
            halves = []
            for hh in (h, h + 1):
                seg = jnp.exp(jnp.where(tri, acs[:, hh:hh + 1] - acs_t[hh:hh + 1, :], -jnp.inf))
                halves.append(jnp.dot((cb * seg).astype(BF16), xp, preferred_element_type=F32))
            y_pairs.append(jnp.where(low_half, halves[0], halves[1]))
        upd = jnp.dot(xdd[:, grp].T.astype(BF16), b_g, preferred_element_type=F32)
        for hh in range(C_HPG):
            h = h0 + hh
            ps = slice(h * C_HEAD_DIM, (h + 1) * C_HEAD_DIM)
            us = slice(hh * C_HEAD_DIM, (hh + 1) * C_HEAD_DIM)
            hs_ref[ps, :] = hs_ref[ps, :] * chunk_dec[0:1, h:h + 1] + upd[us, :]
    y = (jnp.concatenate(y_pairs, axis=1) + jnp.concatenate(y_offs, axis=1) * jnp.exp(acs_x)
         + dsk_ref[...] * xs)
    y = y * jax.nn.silu(z_ref[0].astype(F32))
    o_ref[0, :, :C_INNER] = _rms(y, sn_ref[...]).astype(o_ref.dtype)

    same = (row // D_CHUNK) == (col // D_CHUNK)
    tri2 = tri & same
    la = jnp.dot(sg_ref[0].astype(BF16), gkw_ref[...], preferred_element_type=F32) + gkb_ref[...]
    la = jax.nn.log_sigmoid(la) / D_GATE_NORM
    gcs = _dot_mask_lhs(tri2.astype(BF16), la)
    first = lax.broadcasted_iota(jnp.int32, (q_len, 1), 0) < D_CHUNK
    r_all = r_ref[0]
    for h in range(D_HEADS):
        ks = slice(h * D_HK, (h + 1) * D_HK)
        vs = slice(h * D_HV, (h + 1) * D_HV)
        g_h = gcs[:, ks]
        g_end0 = g_h[D_CHUNK - 1:D_CHUNK, :]
        g_end1 = g_h[q_len - 1:q_len, :]
        q_h = q_ref[0, :, ks].astype(F32) * (D_HK ** -0.5)
        k_h = k_ref[0, :, ks].astype(F32)
        v_h = v_ref[0, :, vs]
        q_dec = (q_h * jnp.exp(g_h)).astype(BF16)
        k_inv = (k_h * jnp.exp(-g_h)).astype(BF16)
        k_end = k_h * jnp.exp(jnp.where(first, g_end0, g_end1) - g_h)
        ke0 = jnp.where(first, k_end, 0.0).astype(BF16)
        ke1 = jnp.where(first, 0.0, k_end).astype(BF16)
        attn = lax.dot_general(q_dec, k_inv, (((1,), (1,)), ((), ())), preferred_element_type=F32)
        attn = jnp.where(tri2, attn, 0.0).astype(BF16)
        o = jnp.dot(attn, v_h, preferred_element_type=F32)
        v_t = v_h.astype(F32).T.astype(BF16)
        st_rows = slice(h * D_HV, (h + 1) * D_HV)
        s0 = gs_ref[st_rows, :]
        s1 = s0 * jnp.exp(g_end0) + jnp.dot(v_t, ke0, preferred_element_type=F32)
        s2 = s1 * jnp.exp(g_end1) + jnp.dot(v_t, ke1, preferred_element_type=F32)
        gs_ref[st_rows, :] = s2
        tdims = (((1,), (1,)), ((), ()))
        o0 = lax.dot_general(q_dec, s0.astype(BF16), tdims, preferred_element_type=F32)
        o1 = lax.dot_general(q_dec, s1.astype(BF16), tdims, preferred_element_type=F32)
        o = o + jnp.where(first, o0, o1)
        o = _rms(o, gn_ref[...]) * jax.nn.silu(r_all[:, vs].astype(F32))
        o_ref[0, :, C_INNER + h * D_HV:C_INNER + (h + 1) * D_HV] = o.astype(o_ref.dtype)


def _ssd_gla(z, q, k, v, r, xbc, side_dt, side_g, g_lane, p):
    b, s, _ = z.shape
    q_len = C_CHUNK
    halo = 16
    nb = SSD_BATCHES if b % SSD_BATCHES == 0 else 1
    chunk = lambda width: pl.BlockSpec((nb, q_len, width), lambda i, n: (i, n, 0))
    full = lambda shape: pl.BlockSpec(shape, lambda i, n: (0,) * len(shape))
    pad_lanes = lambda vec: jnp.zeros((1, LANES), F32).at[0, :vec.shape[0]].set(vec.astype(F32))
    gkw = jnp.zeros((LANES, D_KEY), F32).at[g_lane:g_lane + D_GATE_RANK].set(p["gk_w"]).astype(BF16)
    expand = jnp.asarray(np.arange(C_INNER)[None, :] // C_HEAD_DIM == np.arange(LANES)[:, None], BF16)
    return pl.pallas_call(
        _ssd_gla_kernel,
        grid=(b // nb, s // q_len),
        in_specs=[
            chunk(C_INNER), chunk(D_KEY), chunk(D_KEY), chunk(D_VAL), chunk(D_VAL), chunk(C_CONV_DIM),
            pl.BlockSpec((nb, halo, C_CONV_DIM), lambda i, n: (i, jnp.maximum(n * (q_len // halo) - 1, 0), 0)),
            chunk(LANES), chunk(LANES),
            full((C_CONV, C_CONV_DIM)), full((1, C_CONV_DIM)),
            full((1, LANES)), full((1, LANES)), full((1, C_INNER)), full((LANES, C_INNER)),
            full((1, C_INNER)), full((LANES, D_KEY)), full((1, D_KEY)), full((1, D_HV)),
        ],
        out_specs=pl.BlockSpec((nb, q_len, C_INNER + D_VAL), lambda i, n: (i, n, 0)),
        out_shape=jax.ShapeDtypeStruct((b, s, C_INNER + D_VAL), BF16),
        scratch_shapes=[pltpu.VMEM((nb, C_INNER, C_STATE), F32),
                        pltpu.VMEM((nb, D_VAL, D_HK), F32)],
        compiler_params=_cparams(("parallel", "arbitrary")),
        name="ssd_gla",
    )(z, q, k, v, r, xbc, xbc, side_dt, side_g,
      p["conv_w"].astype(F32), p["conv_b"].reshape(1, -1).astype(F32),
      pad_lanes(p["dt_bias"]), pad_lanes(p["a_log"]),
      jnp.repeat(p["d_skip"].astype(F32), C_HEAD_DIM).reshape(1, C_INNER), expand,
      p["ssd_norm"].reshape(1, -1).astype(F32), gkw, p["gk_b"].reshape(1, -1).astype(F32),
      p["gla_norm"].reshape(1, -1).astype(F32))


def _store_rows_tiled(ref, val):
    m, d = val.shape
    nc = d // LANES
    for c in range(nc):
        ref[pl.ds(c, m, stride=nc), :] = val[:, c * LANES:(c + 1) * LANES]


def _load_rows_tiled(ref, m, dtype=None):
    nc = ref.shape[0] // m
    parts = [ref[pl.ds(c, m, stride=nc), :] for c in range(nc)]
    if dtype is not None:
        parts = [p.astype(dtype) for p in parts]
    return jnp.concatenate(parts, axis=1)


def _pack_pairs(x):
    n = x.shape[1] // 2
    u = pltpu.bitcast(x.astype(BF16).astype(F32), jnp.uint32)
    return (u[:, :n] >> 16) | (u[:, n:] & jnp.uint32(0xFFFF0000))


def _unpack_pairs(u):
    lo = pltpu.bitcast(u << 16, F32).astype(BF16)
    hi = pltpu.bitcast(u & jnp.uint32(0xFFFF0000), F32).astype(BF16)
    return jnp.concatenate([lo, hi], axis=1)


RT_GATE, RT_EXPERT, RT_RANK = 0, 2, 4


def _route_block(lg, carry, earlier):
    m = lg.shape[0]
    lane = lax.broadcasted_iota(jnp.int32, (m, LANES), 1)
    lane_f = lane.astype(F32)
    none = float(LANES)
    neg = -jnp.inf
    first_max = lambda v, vmax: jnp.min(jnp.where(v == vmax, lane_f, none), axis=-1, keepdims=True)
    gl = jnp.where(lane < MOE_GROUPS, lg, neg)
    gmax = jnp.max(gl, axis=-1, keepdims=True)
    g_w = 1.0 / jnp.sum(jnp.exp(gl - gmax), axis=-1, keepdims=True)
    lo = MOE_GROUPS + first_max(gl, gmax) * MOE_EPG
    el = jnp.where((lane_f >= lo) & (lane_f < lo + MOE_EPG), lg, neg)
    emax = jnp.max(el, axis=-1, keepdims=True)
    esum = jnp.sum(jnp.exp(el - emax), axis=-1, keepdims=True)
    l0 = first_max(el, emax)
    el2 = jnp.where(lane_f == l0, neg, el)
    emax2 = jnp.max(el2, axis=-1, keepdims=True)
    l1 = first_max(el2, emax2)
    p0 = 1.0 / esum
    p1 = jnp.exp(emax2 - emax) / esum
    w0 = g_w * (p0 / (p0 + p1))
    w1 = g_w * (p1 / (p0 + p1))
    oh0 = lane_f == l0
    oh1 = lane_f == l1
    oh = (oh0 | oh1).astype(BF16)
    cum = jnp.dot(earlier, oh, preferred_element_type=F32) + carry
    rank0 = jnp.sum(jnp.where(oh0, cum, 0.0), axis=-1, keepdims=True)
    rank1 = jnp.sum(jnp.where(oh1, cum, 0.0), axis=-1, keepdims=True)
    carry = carry + jnp.sum(oh.astype(F32), axis=0, keepdims=True)
    rec = jnp.zeros((m, LANES), F32)
    for pos, val in ((RT_GATE, w0), (RT_GATE + 1, w1), (RT_EXPERT, l0 - MOE_GROUPS),
                     (RT_EXPERT + 1, l1 - MOE_GROUPS), (RT_RANK, rank0), (RT_RANK + 1, rank1)):
        rec = jnp.where(lane == pos, val, rec)
    return rec, carry


def _out_proj_kernel(n_parts, *refs):
    a_refs = refs[:n_parts]
    w_refs = refs[n_parts:2 * n_parts]
    h_ref, g_ref, wr_ref, br_ref, tri_ref, ho_ref, xt_ref, rt_ref, cnt_ref, carry_ref = refs[2 * n_parts:]

    @pl.when(pl.program_id(0) == 0)
    def _():
        carry_ref[...] = jnp.zeros_like(carry_ref)

    acc = h_ref[...]
    for a_ref, w_ref in zip(a_refs, w_refs):
        acc = acc + jnp.dot(a_ref[...], w_ref[...], preferred_element_type=F32)
    ho_ref[...] = acc
    xn = _rms(acc, g_ref[...])
    xt_ref[...] = xn.astype(xt_ref.dtype)
    x_hi, x_mid, _ = _split3(xn)
    wr = wr_ref[...]
    lg2 = jnp.dot(x_hi, wr, preferred_element_type=F32)
    lg = (lg2[:, :LANES] + lg2[:, LANES:] + jnp.dot(x_mid, wr[:, :LANES], preferred_element_type=F32)
          + br_ref[...])
    rec, carry = _route_block(lg, carry_ref[...], tri_ref[...])
    rt_ref[...] = rec
    carry_ref[...] = carry
    cnt_ref[0] = carry


def _out_proj(parts, w_parts, h, g, w_route, b_route, *, tm=512):
    t, d = h.shape
    tm = min(tm, t)
    nc = d // LANES
    row = lambda i: (i, 0)
    const = lambda i: (0, 0)
    in_specs = [pl.BlockSpec((tm, a.shape[1]), row) for a in parts]
    in_specs += [pl.BlockSpec(w.shape, const) for w in w_parts]
    in_specs += [pl.BlockSpec((tm, d), row), pl.BlockSpec((1, d), const),
                 pl.BlockSpec((d, 2 * LANES), const), pl.BlockSpec((1, LANES), const),
                 pl.BlockSpec((tm, tm), const)]
    earlier = jnp.asarray(np.tril(np.ones((tm, tm), np.float32), -1), BF16)
    return pl.pallas_call(
        functools.partial(_out_proj_kernel, len(parts)),
        grid=(t // tm,),
        in_specs=in_specs,
        out_specs=[pl.BlockSpec((tm, d), row), pl.BlockSpec((tm, d), row),
                   pl.BlockSpec((tm, LANES), row), pl.BlockSpec((1, 1, LANES), lambda i: (i, 0, 0))],
        out_shape=[jax.ShapeDtypeStruct((t, d), F32), jax.ShapeDtypeStruct((t, d), BF16),
                   jax.ShapeDtypeStruct((t, LANES), F32), jax.ShapeDtypeStruct((t // tm, 1, LANES), F32)],
        scratch_shapes=[pltpu.VMEM((1, LANES), F32)],
        compiler_params=_cparams(("arbitrary",)),
        name="out_proj",
    )(*parts, *w_parts, h, g.reshape(1, d), w_route, b_route, earlier)


LONG_RUN = 128


def _piece_sizes(max_rows, lo=1, hi=None):
    sizes = [1 << b for b in reversed(range(max_rows.bit_length()))]
    return [n for n in sizes if n >= lo and (hi is None or n < hi)]


def _for_each_piece(run, sizes, body):
    off = run // (2 * sizes[0]) * (2 * sizes[0])
    for n in sizes:
        hit = (run & n) != 0
        pl.when(hit)(functools.partial(body, off, n))
        off = off + jnp.where(hit, n, 0)


def _for_each_run(len_ref, tile, long_ref, max_rows, make_body):
    def sweep(sizes):
        def per_expert(e, first):
            run = len_ref[tile * MOE_EXPERTS + e]
            _for_each_piece(run, sizes, make_body(e, first))
            return first + run

        lax.fori_loop(0, MOE_EXPERTS, per_expert, jnp.int32(0))

    sweep(_piece_sizes(max_rows, hi=LONG_RUN))
    pl.when(long_ref[tile] != 0)(functools.partial(sweep, _piece_sizes(max_rows, lo=LONG_RUN)))


def _zero_fill_rows(rows_ref, z_ref, zsem, lo_ref, hi_ref, n_rows, nc):
    zb = z_ref.shape[0] // nc
    z_ref[...] = jnp.zeros_like(z_ref)
    assert MOE_TILE // 2 <= zb and MOE_TILE % zb == 0

    def piece(row0, n):
        return pltpu.make_async_copy(z_ref.at[pl.ds(0, n * nc)], rows_ref.at[pl.ds(row0 * nc, n * nc)], zsem)

    def sweep(issue):
        def per_expert(e, carry):
            lo = lo_ref[e]

            def one(off, n):
                piece(lo + off, n).start() if issue else piece(lo + off, n).wait()

            _for_each_piece(hi_ref[e] - lo, _piece_sizes(MOE_TILE // 2), one)
            return carry

        def per_block(i, carry):
            piece(i * zb, zb).start() if issue else piece(i * zb, zb).wait()
            return carry

        lax.fori_loop(0, MOE_EXPERTS, per_expert, 0)
        lax.fori_loop(hi_ref[MOE_EXPERTS - 1] // zb, n_rows // zb, per_block, 0)

    sweep(True)
    sweep(False)


def _tile_positions(rt, pos_base):
    lane_f = lax.broadcasted_iota(jnp.int32, rt.shape, 1).astype(F32)
    out = []
    for k in range(MOE_TOPK):
        e_lane = rt[:, RT_EXPERT + k:RT_EXPERT + k + 1] + MOE_GROUPS
        out.append(rt[:, RT_RANK + k:RT_RANK + k + 1]
                   + jnp.sum(jnp.where(lane_f == e_lane, pos_base, 0.0), axis=-1, keepdims=True))
    return out


def _moe_dispatch_kernel(dst_ref, len_ref, long_ref, lo_ref, hi_ref, x_ref, rt_ref, pb_ref, rows_ref,
                         sbuf, z_ref, sem, zsem, *, tm, nc, n_steps, n_rows):
    i = pl.program_id(0)
    rows = MOE_TOPK * tm
    slot = lax.rem(i, 2)

    def wait_slot(s):
        pltpu.make_async_copy(sbuf.at[s], rows_ref.at[pl.ds(0, rows * nc)], sem.at[s]).wait()

    @pl.when(i == 0)
    def _():
        _zero_fill_rows(rows_ref, z_ref, zsem, lo_ref, hi_ref, n_rows, nc)

    @pl.when(i >= 2)
    def _():
        wait_slot(slot)

    pos = _tile_positions(rt_ref[...], pb_ref[0])
    lane = lax.broadcasted_iota(jnp.int32, (tm, LANES), 1)
    pos_t = jnp.where(lane == 0, pos[0], jnp.where(lane == 1, pos[1], 0.0)).T
    p_iota = lax.broadcasted_iota(jnp.int32, (rows, tm), 0).astype(F32)
    place = ((p_iota == pos_t[0:1, :]) | (p_iota == pos_t[1:2, :])).astype(BF16)
    _store_rows_tiled(sbuf.at[slot], _pack_pairs(jnp.dot(place, x_ref[...], preferred_element_type=F32)))

    def sender(e, src):
        dst = dst_ref[i * MOE_EXPERTS + e]

        def send(off, n):
            pltpu.make_async_copy(sbuf.at[slot, pl.ds((src + off) * nc, n * nc)],
                                  rows_ref.at[pl.ds((dst + off) * nc, n * nc)], sem.at[slot]).start()

        return send

    _for_each_run(len_ref, i, long_ref, rows, sender)

    @pl.when(i == n_steps - 1)
    def _():
        wait_slot(slot)
        if n_steps > 1:
            wait_slot(1 - slot)


def _moe_dispatch(x, route, pos_base, seg_dst, seg_len, seg_long, pad_lo, pad_hi, n_rows, *, tm):
    t, d = x.shape
    nc = d // (2 * LANES)
    n_steps = t // tm
    row = lambda i, *_: (i, 0)
    return pl.pallas_call(
        functools.partial(_moe_dispatch_kernel, tm=tm, nc=nc, n_steps=n_steps, n_rows=n_rows),
        grid_spec=pltpu.PrefetchScalarGridSpec(
            num_scalar_prefetch=5, grid=(n_steps,),
            in_specs=[pl.BlockSpec((tm, d), row), pl.BlockSpec((tm, LANES), row),
                      pl.BlockSpec((1, 1, LANES), lambda i, *_: (i, 0, 0))],
            out_specs=pl.BlockSpec(memory_space=pl.ANY),
            scratch_shapes=[pltpu.VMEM((2, MOE_TOPK * tm * nc, LANES), jnp.uint32),
                            pltpu.VMEM((MOE_TILE // 2 * nc, LANES), jnp.uint32),
                            pltpu.SemaphoreType.DMA((2,)), pltpu.SemaphoreType.DMA(())]),
        out_shape=jax.ShapeDtypeStruct((n_rows * nc, LANES), jnp.uint32),
        compiler_params=pltpu.CompilerParams(dimension_semantics=("arbitrary",), has_side_effects=True,
                                             vmem_limit_bytes=VMEM_LIMIT),
        name="moe_dispatch",
    )(seg_dst, seg_len, seg_long, pad_lo, pad_hi, x, route, pos_base)


def _moe_kernel(te_ref, nu_ref, x_ref, wg_ref, wu_ref, wd_ref, y_ref):
    i = pl.program_id(0)

    @pl.when(i >= nu_ref[0])
    def _():
        y_ref[...] = jnp.zeros_like(y_ref)

    @pl.when(i < nu_ref[0])
    def _():
        x = _unpack_pairs(_load_rows_tiled(x_ref, MOE_TILE))
        gate = jnp.dot(x, wg_ref[...].astype(BF16), preferred_element_type=F32)
        up = jnp.dot(x, wu_ref[...].astype(BF16), preferred_element_type=F32)
        act = (jax.nn.silu(gate) * up).astype(BF16)
        y = jnp.dot(act, wd_ref[...].astype(BF16), preferred_element_type=F32)
        _store_rows_tiled(y_ref, _pack_pairs(y))


def _moe_experts(x_rows, tile_expert, n_used, w_gate, w_up, w_down, layer):
    d, ff = w_gate.shape[-2:]
    nc = d // (2 * LANES)
    n_tiles = x_rows.shape[0] // (MOE_TILE * nc)
    live = lambda i, nu: jnp.minimum(i, nu[0] - 1)
    w_spec = lambda a, b: pl.BlockSpec((None, None, a, b), lambda i, te, nu: (layer, te[live(i, nu)], 0, 0))
    return pl.pallas_call(
        _moe_kernel,
        grid_spec=pltpu.PrefetchScalarGridSpec(
            num_scalar_prefetch=2, grid=(n_tiles,),
            in_specs=[pl.BlockSpec((MOE_TILE * nc, LANES), lambda i, te, nu: (live(i, nu), 0)),
                      w_spec(d, ff), w_spec(d, ff), w_spec(ff, d)],
            out_specs=pl.BlockSpec((MOE_TILE * nc, LANES), lambda i, te, nu: (i, 0))),
        out_shape=jax.ShapeDtypeStruct(x_rows.shape, x_rows.dtype),
        compiler_params=_cparams(("arbitrary",)),
        name="moe_experts",
    )(tile_expert, n_used, x_rows, w_gate, w_up, w_down)


def _moe_combine_kernel(src_ref, len_ref, long_ref, h_ref, rt_ref, pb_ref, g_ref, y_hbm, o_ref, ybuf, sem,
                        *, tm, nc, final):
    i = pl.program_id(0)
    n_steps = pl.num_programs(0)
    rows = MOE_TOPK * tm

    def fetch(tile, slot):
        def receiver(e, dst):
            src = src_ref[tile * MOE_EXPERTS + e]

            def recv(off, n):
                pltpu.make_async_copy(y_hbm.at[pl.ds((src + off) * nc, n * nc)],
                                      ybuf.at[slot, pl.ds((dst + off) * nc, n * nc)], sem.at[slot]).start()

            return recv

        _for_each_run(len_ref, tile, long_ref, rows, receiver)

    slot = lax.rem(i, 2)

    @pl.when(i == 0)
    def _():
        fetch(0, 0)

    @pl.when(i + 1 < n_steps)
    def _():
        fetch(i + 1, 1 - slot)

    pltpu.make_async_copy(y_hbm.at[pl.ds(0, rows * nc)], ybuf.at[slot], sem.at[slot]).wait()
    y = _unpack_pairs(_load_rows_tiled(ybuf.at[slot], rows))
    rt = rt_ref[...]
    pos_f = lax.broadcasted_iota(jnp.int32, (tm, rows), 1).astype(F32)
    pick = jnp.zeros((tm, rows), F32)
    for k, pos in enumerate(_tile_positions(rt, pb_ref[0])):
        pick = jnp.where(pos_f == pos, rt[:, RT_GATE + k:RT_GATE + k + 1], pick)
    out = h_ref[...] + jnp.dot(pick.astype(BF16), y, preferred_element_type=F32)
    o_ref[...] = _rms(out, g_ref[...]) if final else out


def _moe_combine_rows(h, y_rows, route, seg_src, seg_len, seg_long, pos_base, g, *, tm, final):
    t, d = h.shape
    nc = d // (2 * LANES)
    row = lambda i, *_: (i, 0)
    return pl.pallas_call(
        functools.partial(_moe_combine_kernel, tm=tm, nc=nc, final=final),
        grid_spec=pltpu.PrefetchScalarGridSpec(
            num_scalar_prefetch=3, grid=(t // tm,),
            in_specs=[pl.BlockSpec((tm, d), row), pl.BlockSpec((tm, LANES), row),
                      pl.BlockSpec((1, 1, LANES), lambda i, *_: (i, 0, 0)),
                      pl.BlockSpec((1, d), lambda i, *_: (0, 0)),
                      pl.BlockSpec(memory_space=pl.ANY)],
            out_specs=pl.BlockSpec((tm, d), row),
            scratch_shapes=[pltpu.VMEM((2, MOE_TOPK * tm * nc, LANES), y_rows.dtype),
                            pltpu.SemaphoreType.DMA((2,))]),
        out_shape=jax.ShapeDtypeStruct((t, d), F32),
        compiler_params=_cparams(("arbitrary",)),
        name="moe_combine",
    )(seg_src, seg_len, seg_long, h, route, pos_base, g.reshape(1, d), y_rows)


def _moe(h, x, route, tile_counts, w_gate, w_up, w_down, layer, g, *, final):
    t = route.shape[0]
    n_tt = tile_counts.shape[0]
    tm = t // n_tt
    after = tile_counts[:, 0, MOE_GROUPS:MOE_GROUPS + MOE_EXPERTS].astype(jnp.int32)
    before = jnp.concatenate([jnp.zeros((1, MOE_EXPERTS), jnp.int32), after[:-1]], axis=0)
    cnt = after[-1]
    padded = (cnt + MOE_TILE - 1) // MOE_TILE * MOE_TILE
    pad_ends = jnp.cumsum(padded)
    starts = (pad_ends - padded).astype(jnp.int32)
    n_tiles = (t * MOE_TOPK + MOE_EXPERTS * (MOE_TILE - 1)) // MOE_TILE
    tile_start = jnp.arange(n_tiles, dtype=jnp.int32) * MOE_TILE
    tile_expert = jnp.minimum(jnp.sum(tile_start[:, None] >= pad_ends[None, :], axis=1),
                              MOE_EXPERTS - 1).astype(jnp.int32)
    n_used = (pad_ends[-1] // MOE_TILE).astype(jnp.int32).reshape(1)
    n_rows = n_tiles * MOE_TILE
    seg_len = (after - before).reshape(-1)
    seg_off = jnp.cumsum(after - before, axis=1) - (after - before)
    seg_row = (starts[None, :] + before).reshape(-1)
    pos_base = jnp.zeros((n_tt, 1, LANES), F32).at[:, 0, MOE_GROUPS:MOE_GROUPS + MOE_EXPERTS].set(
        (seg_off - before).astype(F32))
    seg_long = jnp.any(after - before >= LONG_RUN, axis=1).astype(jnp.int32)
    x_rows = _moe_dispatch(x, route, pos_base, seg_row, seg_len, seg_long, starts + cnt,
                           pad_ends.astype(jnp.int32), n_rows, tm=tm)
    y_rows = _moe_experts(x_rows, tile_expert, n_used, w_gate, w_up, w_down, layer)
    return _moe_combine_rows(h, y_rows, route, seg_row, seg_len, seg_long, pos_base, g, tm=tm, final=final)


def _router_weights(w_group, b_group, w_router, b_router):
    d = w_group.shape[0]
    w = jnp.zeros((d, LANES), F32)
    w = w.at[:, :MOE_GROUPS].set(w_group).at[:, MOE_GROUPS:MOE_GROUPS + MOE_EXPERTS].set(w_router)
    b = jnp.zeros((1, LANES), F32)
    b = b.at[0, :MOE_GROUPS].set(b_group).at[0, MOE_GROUPS:MOE_GROUPS + MOE_EXPERTS].set(b_router)
    w_hi = w.astype(BF16)
    w_mid = (w - w_hi.astype(F32)).astype(BF16)
    return jnp.concatenate([w_hi, w_mid], axis=1), b


def kernel(x, norm_mix, norm_moe, norm_final, even_w_in, even_sinks, even_forget_bias, even_w_out,
           odd_w_in, odd_conv_w, odd_conv_b, odd_dt_bias, odd_a_log, odd_d_skip, odd_ssd_norm,
           odd_gk_w, odd_gk_b, odd_gla_norm, odd_w_out, moe_w_group, moe_b_group, moe_w_router,
           moe_b_router, moe_w_gate, moe_w_up, moe_w_down):
    b, s, d = x.shape
    t = b * s
    depth = norm_mix.shape[0]
    h = x.reshape(t, d)
    for layer in range(depth):
        i = layer // 2
        if layer % 2 == 0:
            w = even_w_in[i]
            n_ab = (A_Q_HEADS + 2 * A_KV_HEADS + 3 * B_HEADS) * HEAD_DIM
            w_main = w[:, :n_ab].astype(BF16)
            w_aux = jnp.zeros((d, LANES), F32).at[:, :B_HEADS].set(w[:, n_ab:]).astype(BF16)
            proj, f_aux = _norm_proj(h, norm_mix[layer], [w_main, w_aux],
                                     ((0, 0, n_ab, BF16), (1, 0, LANES, F32)), tm=1024)
            proj = proj.reshape(b, s, -1)
            out_a = _swa(proj, even_sinks[i])
            c, ct = _fox_gate(f_aux.reshape(b, s, LANES), even_forget_bias[i])
            out_b = _fox(proj, c, ct)
            n_ha = A_Q_HEADS * HEAD_DIM
            w_out = even_w_out[i].astype(BF16)
            parts = [out_a.reshape(t, -1), out_b.reshape(t, -1)]
            w_parts = [w_out[:n_ha], w_out[n_ha:]]
        else:
            w = odd_w_in[i]
            o_z, o_xbc = 0, C_INNER
            o_dt = o_xbc + C_CONV_DIM
            o_q = o_dt + C_HEADS
            o_k = o_q + D_KEY
            o_v = o_k + D_KEY
            o_g = o_v + D_VAL
            o_r = o_g + D_GATE_RANK
            n_b = w.shape[1] - o_dt
            w_a = w[:, :o_dt].astype(BF16)
            w_b = jnp.pad(w[:, o_dt:].astype(BF16), ((0, 0), (0, -n_b % LANES)))
            g_win = (o_g - o_dt) // LANES * LANES
            plan = ((0, o_z, C_INNER, BF16), (1, o_q - o_dt, D_KEY, BF16), (1, o_k - o_dt, D_KEY, BF16),
                    (1, o_v - o_dt, D_VAL, BF16), (1, o_r - o_dt, D_VAL, BF16), (0, o_xbc, C_CONV_DIM, BF16),
                    (1, 0, LANES, F32), (1, g_win, LANES, F32))
            outs = _norm_proj(h, norm_mix[layer], [w_a, w_b], plan, tm=512)
            params = dict(conv_w=odd_conv_w[i], conv_b=odd_conv_b[i], dt_bias=odd_dt_bias[i], a_log=odd_a_log[i],
                          d_skip=odd_d_skip[i], ssd_norm=odd_ssd_norm[i], gk_w=odd_gk_w[i], gk_b=odd_gk_b[i],
                          gla_norm=odd_gla_norm[i])
            mixed = _ssd_gla(*[o.reshape(b, s, -1) for o in outs], o_g - o_dt - g_win, params)
            parts = [mixed.reshape(t, -1)]
            w_parts = [odd_w_out[i].astype(BF16)]
        w_route, b_route = _router_weights(moe_w_group[layer], moe_b_group[layer],
                                           moe_w_router[layer], moe_b_router[layer])
        h, x_tiled, route, tile_counts = _out_proj(parts, w_parts, h, norm_moe[layer], w_route, b_route)
        h = _moe(h, x_tiled, route, tile_counts, moe_w_gate, moe_w_up, moe_w_down, layer, norm_final,
                 final=layer == depth - 1)
    out = h
    return out.reshape(b, s, d)
```

```python
import functools

import numpy as np
import jax
import jax.numpy as jnp
from jax import lax
from jax.experimental import pallas as pl
from jax.experimental.pallas import tpu as pltpu

F32 = jnp.float32
BF16 = jnp.bfloat16

RMS_EPS = 1e-6
HEAD_DIM = 64
A_Q_HEADS = 8
A_KV_HEADS = 2
A_GROUP = A_Q_HEADS // A_KV_HEADS
A_WINDOW = 128
B_HEADS = 8
C_HEADS = 16
C_HEAD_DIM = 64
C_INNER = C_HEADS * C_HEAD_DIM
C_GROUPS = 2
C_HPG = C_HEADS // C_GROUPS
C_STATE = 128
C_CONV = 4
C_CHUNK = 128
C_CONV_DIM = C_INNER + 2 * C_GROUPS * C_STATE
D_HEADS = 4
D_HK = 128
D_HV = 256
D_KEY = D_HEADS * D_HK
D_VAL = D_HEADS * D_HV
D_GATE_RANK = 16
D_GATE_NORM = 16.0
D_CHUNK = 64
MOE_GROUPS = 4
MOE_EPG = 8
MOE_EXPERTS = MOE_GROUPS * MOE_EPG
MOE_TOPK = 2

LANES = 128
VMEM_LIMIT = 48 * 1024 * 1024
MOE_TILE = 512


def _cparams(sem):
    return pltpu.CompilerParams(dimension_semantics=sem, vmem_limit_bytes=VMEM_LIMIT)


def _rms(x, g):
    ms = jnp.mean(x * x, axis=-1, keepdims=True)
    return x * lax.rsqrt(ms + RMS_EPS) * g


def _norm_proj_kernel(n_w, plan, t_plan, *refs):
    x_ref, g_ref = refs[:2]
    w_refs = refs[2:2 + n_w]
    o_refs = refs[2 + n_w:2 + n_w + len(plan)]
    t_refs = refs[2 + n_w + len(plan):2 + n_w + len(plan) + len(t_plan)]
    res_refs = refs[2 + n_w + len(plan) + len(t_plan):]
    xn = _rms(x_ref[...], g_ref[...]).astype(BF16)
    for w_ref, res_ref in zip(w_refs, res_refs):
        res_ref[...] = jnp.dot(xn, w_ref[...], preferred_element_type=F32)
    for o_ref, (wi, start, width, _) in zip(o_refs, plan):
        o_ref[...] = res_refs[wi][:, start:start + width].astype(o_ref.dtype)
    for o_ref, (wi, start, width, _, rows) in zip(t_refs, t_plan):
        for u in range(o_ref.shape[0]):
            o_ref[u] = res_refs[wi][u * rows:(u + 1) * rows, start:start + width].T.astype(o_ref.dtype)


def _norm_proj(x, g, weights, plan, *, tm, t_plan=()):
    t, d = x.shape
    tm = min(tm, t)
    row = lambda i: (i, 0)
    const = lambda i: (0, 0)
    in_specs = [pl.BlockSpec((tm, d), row), pl.BlockSpec((1, d), const)]
    in_specs += [pl.BlockSpec(w.shape, const, pipeline_mode=pl.Buffered(1)) for w in weights]
    return pl.pallas_call(
        functools.partial(_norm_proj_kernel, len(weights), plan, t_plan),
        grid=(t // tm,),
        in_specs=in_specs,
        out_specs=([pl.BlockSpec((tm, width), row) for _, _, width, _ in plan]
                   + [pl.BlockSpec((tm // rows, width, rows), lambda i: (i, 0, 0))
                      for _, _, width, _, rows in t_plan]),
        out_shape=([jax.ShapeDtypeStruct((t, width), dtype) for _, _, width, dtype in plan]
                   + [jax.ShapeDtypeStruct((t // rows, width, rows), dtype) for _, _, width, dtype, rows in t_plan]),
        scratch_shapes=[pltpu.VMEM((tm, w.shape[1]), F32) for w in weights],
        compiler_params=_cparams(("parallel",)),
        name="norm_proj",
    )(x, g.reshape(1, d), *weights)


SWA_QBLOCKS = 8


def _swa_kernel(sink_ref, slope_ref, q_ref, kp_ref, kc_ref, vp_ref, vc_ref, o_ref):
    n = pl.program_id(1)
    blk = A_WINDOW
    wide = A_GROUP * blk
    key = lax.broadcasted_iota(jnp.int32, (2 * blk, wide), 0)
    qry = lax.broadcasted_iota(jnp.int32, (2 * blk, wide), 1) % blk
    dist = blk + qry - key
    in_window = (dist >= 0) & (dist < A_WINDOW)
    distf = dist.astype(F32)
    nt = (((1,), (1,)), ((), ()))
    k_all = jnp.concatenate([kp_ref[0], kc_ref[0]], axis=0)
    v_all = jnp.concatenate([vp_ref[0], vc_ref[0]], axis=1)
    units = [(j, kh) for j in range(SWA_QBLOCKS) for kh in range(A_KV_HEADS)]
    scores = []
    for j, kh in units:
        k = k_all[j * blk:(j + 2) * blk, kh * HEAD_DIM:(kh + 1) * HEAD_DIM]
        q = jnp.concatenate([q_ref[0, j * blk:(j + 1) * blk,
                                   (kh * A_GROUP + g) * HEAD_DIM:(kh * A_GROUP + g + 1) * HEAD_DIM]
                             for g in range(A_GROUP)], axis=0)
        scores.append(lax.dot_general(k, q, nt, preferred_element_type=F32))
    probs = []
    for (j, kh), s in zip(units, scores):
        valid = in_window & ((key >= blk) | (n * SWA_QBLOCKS + j > 0))
        s = s * (HEAD_DIM ** -0.5) - slope_ref[kh:kh + 1, :] * distf
        s = jnp.where(valid, s, -jnp.inf)
        sink = sink_ref[kh:kh + 1, :]
        m = jnp.maximum(jnp.max(s, axis=0, keepdims=True), sink)
        p = jnp.exp(s - m)
        probs.append((p.astype(BF16), jnp.sum(p, axis=0, keepdims=True) + jnp.exp(sink - m)))
    for j in range(SWA_QBLOCKS):
        outs = []
        for kh in range(A_KV_HEADS):
            p, denom = probs[j * A_KV_HEADS + kh]
            v_t = v_all[kh * HEAD_DIM:(kh + 1) * HEAD_DIM, j * blk:(j + 2) * blk]
            o_t = jnp.dot(v_t, p, preferred_element_type=F32) / denom
            outs += [o_t[:, g * blk:(g + 1) * blk] for g in range(A_GROUP)]
        o_ref[0, j * blk:(j + 1) * blk, :] = jnp.concatenate(outs, axis=0).T.astype(o_ref.dtype)


def _swa(proj, v_t, sinks):
    b, s, _ = proj.shape
    blk = A_WINDOW
    qw = A_Q_HEADS * HEAD_DIM
    kw = A_KV_HEADS * HEAD_DIM
    k_blk = qw // kw
    per_lane = lambda vec: jnp.repeat(vec.astype(F32), blk).reshape(A_KV_HEADS, A_GROUP * blk)
    slopes = jnp.asarray(2.0 ** (-8.0 * np.arange(1, A_Q_HEADS + 1) / A_Q_HEADS), F32)
    tq = SWA_QBLOCKS * blk
    assert v_t.shape == (b, s // tq, kw, tq)
    prev = lambda n: jnp.maximum(n * SWA_QBLOCKS - 1, 0)
    full = pl.BlockSpec((A_KV_HEADS, A_GROUP * blk), lambda i, n: (0, 0))
    return pl.pallas_call(
        _swa_kernel,
        grid=(b, s // tq),
        in_specs=[
            full, full,
            pl.BlockSpec((1, tq, qw), lambda i, n: (i, n, 0)),
            pl.BlockSpec((1, blk, kw), lambda i, n: (i, prev(n), k_blk)),
            pl.BlockSpec((1, tq, kw), lambda i, n: (i, n, k_blk)),
            pl.BlockSpec((1, None, kw, blk), lambda i, n: (i, jnp.maximum(n - 1, 0), 0, SWA_QBLOCKS - 1)),
            pl.BlockSpec((1, None, kw, tq), lambda i, n: (i, n, 0, 0)),
        ],
        out_specs=pl.BlockSpec((1, tq, qw), lambda i, n: (i, n, 0)),
        out_shape=jax.ShapeDtypeStruct((b, s, qw), BF16),
        compiler_params=_cparams(("parallel", "parallel")),
        name="swa",
    )(per_lane(sinks), per_lane(slopes), proj, proj, proj, v_t, v_t)


def _tril(n, dtype=F32):
    r = lax.broadcasted_iota(jnp.int32, (n, n), 0)
    c = lax.broadcasted_iota(jnp.int32, (n, n), 1)
    return (c <= r).astype(dtype)


def _split3(x):
    hi = x.astype(BF16)
    r = x - hi.astype(F32)
    mid = r.astype(BF16)
    return hi, mid, (r - mid.astype(F32)).astype(BF16)


def _dot_mask_lhs(mask, x):
    return sum(jnp.dot(mask, part, preferred_element_type=F32) for part in _split3(x))


def _dot_mask_rhs(x, mask):
    return sum(jnp.dot(part, mask, preferred_element_type=F32) for part in _split3(x))


def _fox_gate_kernel(f_ref, b_ref, c_ref, ct_ref):
    tri = _tril(LANES, BF16)
    carry = jnp.zeros((1, LANES), F32)
    for n in range(f_ref.shape[1] // LANES):
        rows = slice(n * LANES, (n + 1) * LANES)
        lf = jax.nn.log_sigmoid(f_ref[0, rows, :] + b_ref[...])
        cs = _dot_mask_lhs(tri, lf) + carry
        carry = cs[LANES - 1:LANES, :]
        c_ref[0, rows, :] = cs
        ct_ref[0, n] = cs.T[:B_HEADS, :]


def _fox_gate(f_aux, bias):
    b, s, _ = f_aux.shape
    nb = s // LANES
    bias_p = jnp.zeros((1, LANES), F32).at[0, :B_HEADS].set(bias.astype(F32))
    return pl.pallas_call(
        _fox_gate_kernel,
        grid=(b,),
        in_specs=[pl.BlockSpec((1, s, LANES), lambda i: (i, 0, 0)),
                  pl.BlockSpec((1, LANES), lambda i: (0, 0))],
        out_specs=[pl.BlockSpec((1, s, LANES), lambda i: (i, 0, 0)),
                   pl.BlockSpec((1, nb, B_HEADS, LANES), lambda i: (i, 0, 0, 0))],
        out_shape=[jax.ShapeDtypeStruct((b, s, LANES), F32),
                   jax.ShapeDtypeStruct((b, nb, B_HEADS, LANES), F32)],
        compiler_params=_cparams(("parallel",)),
        name="fox_gate",
    )(f_aux, bias_p)


def _fox_kernel(q0_ref, q1_ref, k0_ref, k1_ref, vt_ref, c_ref, ctq_ref, o_ref, *, tq, tk, heads_per_step):
    qi = pl.program_id(1)
    sub = tq // LANES
    key = lax.broadcasted_iota(jnp.int32, (tk, tq), 0)
    qry = lax.broadcasted_iota(jnp.int32, (tk, tq), 1)
    per_q = tq // tk
    causal = [u * tk + key <= qry for u in range(per_q)]
    nt = (((1,), (1,)), ((), ()))
    half = B_HEADS // 2
    q_refs, k_refs = (q0_ref, q1_ref), (k0_ref, k1_ref)
    outs = []
    for h0 in range(0, B_HEADS, heads_per_step):
        heads = list(range(h0, h0 + heads_per_step))
        hsl = [slice(h * HEAD_DIM, (h + 1) * HEAD_DIM) for h in heads]
        lsl = [slice((h % half) * HEAD_DIM, (h % half + 1) * HEAD_DIM) for h in heads]
        qs = [q_refs[h // half][0, :, ls] * (HEAD_DIM ** -0.5)
              for h, ls in zip(heads, lsl)]
        cqs = [jnp.concatenate([ctq_ref[0, u, h:h + 1, :] for u in range(sub)], axis=1) for h in heads]

        def step(j, carry, mask, heads=heads, hsl=hsl, lsl=lsl, qs=qs, cqs=cqs):
            start = pl.multiple_of(j * tk, tk)
            sts = [lax.dot_general(k_refs[h // half][0, pl.ds(start, tk), ls], q, nt,
                                   preferred_element_type=F32)
                   for h, ls, q in zip(heads, lsl, qs)]
            ps, stats = [], []
            for idx, h in enumerate(heads):
                m, l, _ = carry[3 * idx:3 * idx + 3]
                ck = c_ref[0, pl.ds(start, tk), h:h + 1]
                st = (sts[idx] - ck) + cqs[idx]
                if mask is not None:
                    st = jnp.where(mask, st, -jnp.inf)
                m_new = jnp.maximum(m, jnp.max(st, axis=0, keepdims=True))
                alpha = jnp.exp(m - m_new)
                p = jnp.exp(st - m_new)
                stats.append((m_new, alpha, alpha * l + jnp.sum(p, axis=0, keepdims=True)))
                ps.append(p.astype(BF16))
            new = []
            for idx in range(len(heads)):
                m_new, alpha, l = stats[idx]
                pv = jnp.dot(vt_ref[0, j, hsl[idx], :], ps[idx], preferred_element_type=F32)
                new += [m_new, l, alpha * carry[3 * idx + 2] + pv]
            return tuple(new)

        init = (jnp.full((1, tq), -jnp.inf, F32), jnp.zeros((1, tq), F32),
                jnp.zeros((HEAD_DIM, tq), F32)) * heads_per_step
        carry = lax.fori_loop(0, qi * per_q, functools.partial(step, mask=None), init)
        for u in range(per_q):
            carry = step(qi * per_q + u, carry, causal[u])
        for idx in range(heads_per_step):
            outs.append(carry[3 * idx + 2] / carry[3 * idx + 1])
    o_ref[0] = jnp.concatenate(outs, axis=0).T.astype(o_ref.dtype)


def _fox(proj, v_t, c, ct, *, tq=256, tk=256, heads_per_step=8):
    b, s, _ = proj.shape
    w = B_HEADS * HEAD_DIM
    nk = s // tk
    sub = tq // LANES
    hw = w // 2
    base = (A_Q_HEADS + 2 * A_KV_HEADS) * HEAD_DIM
    qb, kb = base // hw, (base + w) // hw
    assert v_t.shape == (b, nk, w, tk)
    return pl.pallas_call(
        functools.partial(_fox_kernel, tq=tq, tk=tk, heads_per_step=heads_per_step),
        grid=(b, s // tq),
        in_specs=[
            pl.BlockSpec((1, tq, hw), lambda i, n: (i, n, qb)),
            pl.BlockSpec((1, tq, hw), lambda i, n: (i, n, qb + 1)),
            pl.BlockSpec((1, s, hw), lambda i, n: (i, 0, kb)),
            pl.BlockSpec((1, s, hw), lambda i, n: (i, 0, kb + 1)),
            pl.BlockSpec((1, nk, w, tk), lambda i, n: (i, 0, 0, 0)),
            pl.BlockSpec((1, s, LANES), lambda i, n: (i, 0, 0)),
            pl.BlockSpec((1, sub, B_HEADS, LANES), lambda i, n: (i, n, 0, 0)),
        ],
        out_specs=pl.BlockSpec((1, tq, w), lambda i, n: (i, n, 0)),
        out_shape=jax.ShapeDtypeStruct((b, s, w), BF16),
        compiler_params=_cparams(("parallel", "parallel")),
        name="fox",
    )(proj, proj, proj, proj, v_t, c, ct)


SSD_BATCHES = 4


def _ssd_gla_kernel(*refs):
    n_data, n_par = 9, 10
    data, params = refs[:n_data], refs[n_data:n_data + n_par]
    o_ref, hs_ref, gs_ref = refs[n_data + n_par:]

    @pl.when(pl.program_id(1) == 0)
    def _():
        hs_ref[...] = jnp.zeros_like(hs_ref)
        gs_ref[...] = jnp.zeros_like(gs_ref)

    for bb in range(o_ref.shape[0]):
        one = lambda ref: ref.at[pl.ds(bb, 1)]
        _ssd_gla_chunk(*[one(ref) for ref in data], *params, one(o_ref), hs_ref.at[bb], gs_ref.at[bb])


def _ssd_gla_chunk(z_ref, q_ref, k_ref, v_ref, r_ref, xc_ref, xp_ref, sdt_ref, sg_ref,
                   cw_ref, cb_ref, dtb_ref, alog_ref, dsk_ref, ex_ref, sn_ref, gkw_ref, gkb_ref, gn_ref,
                   o_ref, hs_ref, gs_ref):
    c = pl.program_id(1)
    q_len = C_CHUNK
    halo = xp_ref.shape[1]

    prev = xp_ref[0]
    cur = xc_ref[0]
    ext = jnp.concatenate([jnp.where(c > 0, prev, jnp.zeros_like(prev)), cur], axis=0)
    t_out = lax.broadcasted_iota(jnp.int32, (q_len, halo + q_len), 0)
    t_in = lax.broadcasted_iota(jnp.int32, (q_len, halo + q_len), 1) - halo
    acc = cb_ref[...] + cw_ref[C_CONV - 1:C_CONV, :] * cur.astype(F32)
    for j in range(C_CONV - 1):
        shift = (t_in == t_out - (C_CONV - 1 - j)).astype(BF16)
        acc = acc + cw_ref[j:j + 1, :] * jnp.dot(shift, ext, preferred_element_type=F32)
    xbc = jax.nn.silu(acc)
    xs = xbc[:, :C_INNER]
    gs_w = C_GROUPS * C_STATE
    bm = xbc[:, C_INNER:C_INNER + gs_w].astype(BF16)
    cm = xbc[:, C_INNER + gs_w:].astype(BF16)

    row = lax.broadcasted_iota(jnp.int32, (q_len, q_len), 0)
    col = lax.broadcasted_iota(jnp.int32, (q_len, q_len), 1)
    tri = col <= row

    lane = lax.broadcasted_iota(jnp.int32, (1, LANES), 1)
    dt = jnp.where(lane < C_HEADS, jax.nn.softplus(sdt_ref[0] + dtb_ref[...]), 0.0)
    dta = dt * -jnp.exp(alog_ref[...])
    acs = _dot_mask_lhs(tri.astype(BF16), dta)
    acs_t = acs.T
    chunk_dec = jnp.exp(acs[q_len - 1:q_len, :])
    expand = ex_ref[...]
    dt_x = _dot_mask_rhs(dt, expand)
    acs_x = _dot_mask_rhs(acs, expand)
    xd = xs * dt_x
    xd_b = xd.astype(BF16)
    xdd = xd * jnp.exp(acs_x[q_len - 1:q_len, :] - acs_x)
    low_half = lax.broadcasted_iota(jnp.int32, (q_len, LANES), 1) < C_HEAD_DIM

    y_pairs, y_offs = [], []
    tdims = (((1,), (1,)), ((), ()))
    for g in range(C_GROUPS):
        b_g = bm[:, g * C_STATE:(g + 1) * C_STATE]
        c_g = cm[:, g * C_STATE:(g + 1) * C_STATE]
        cb = lax.dot_general(c_g, b_g, tdims, preferred_element_type=F32)
        h0 = g * C_HPG
        grp = slice(h0 * C_HEAD_DIM, (h0 + C_HPG) * C_HEAD_DIM)
        y_offs.append(lax.dot_general(c_g, hs_ref[grp, :].astype(BF16), tdims, preferred_element_type=F32))
        for h in range(h0, h0 + C_HPG, 2):
            xp = xd_b[:, h * C_HEAD_DIM:(h + 2) * C_HEAD_DIM]
            halves = []
            for hh in (h, h + 1):
                seg = jnp.exp(jnp.where(tri, acs[:, hh:hh + 1] - acs_t[hh:hh + 1, :], -jnp.inf))
                halves.append(jnp.dot((cb * seg).astype(BF16), xp, preferred_element_type=F32))
            y_pairs.append(jnp.where(low_half, halves[0], halves[1]))
        upd = jnp.dot(xdd[:, grp].T.astype(BF16), b_g, preferred_element_type=F32)
        for hh in range(C_HPG):
            h = h0 + hh
            ps = slice(h * C_HEAD_DIM, (h + 1) * C_HEAD_DIM)
            us = slice(hh * C_HEAD_DIM, (hh + 1) * C_HEAD_DIM)
            hs_ref[ps, :] = hs_ref[ps, :] * chunk_dec[0:1, h:h + 1] + upd[us, :]
    y = (jnp.concatenate(y_pairs, axis=1) + jnp.concatenate(y_offs, axis=1) * jnp.exp(acs_x)
         + dsk_ref[...] * xs)
    y = y * jax.nn.silu(z_ref[0].astype(F32))
    o_ref[0, :, :C_INNER] = _rms(y, sn_ref[...]).astype(o_ref.dtype)

    same = (row // D_CHUNK) == (col // D_CHUNK)
    tri2 = tri & same
    la = jnp.dot(sg_ref[0].astype(BF16), gkw_ref[...], preferred_element_type=F32) + gkb_ref[...]
    la = jax.nn.log_sigmoid(la) / D_GATE_NORM
    gcs = _dot_mask_lhs(tri2.astype(BF16), la)
    first = lax.broadcasted_iota(jnp.int32, (q_len, 1), 0) < D_CHUNK
    r_all = r_ref[0]
    for h in range(D_HEADS):
        ks = slice(h * D_HK, (h + 1) * D_HK)
        vs = slice(h * D_HV, (h + 1) * D_HV)
        g_h = gcs[:, ks]
        g_end0 = g_h[D_CHUNK - 1:D_CHUNK, :]
        g_end1 = g_h[q_len - 1:q_len, :]
        q_h = q_ref[0, :, ks].astype(F32) * (D_HK ** -0.5)
        k_h = k_ref[0, :, ks].astype(F32)
        v_h = v_ref[0, :, vs]
        q_dec = (q_h * jnp.exp(g_h)).astype(BF16)
        k_inv = (k_h * jnp.exp(-g_h)).astype(BF16)
        k_end = k_h * jnp.exp(jnp.where(first, g_end0, g_end1) - g_h)
        ke0 = jnp.where(first, k_end, 0.0).astype(BF16)
        ke1 = jnp.where(first, 0.0, k_end).astype(BF16)
        attn = lax.dot_general(q_dec, k_inv, (((1,), (1,)), ((), ())), preferred_element_type=F32)
        attn = jnp.where(tri2, attn, 0.0).astype(BF16)
        o = jnp.dot(attn, v_h, preferred_element_type=F32)
        v_t = v_h.astype(F32).T.astype(BF16)
        st_rows = slice(h * D_HV, (h + 1) * D_HV)
        s0 = gs_ref[st_rows, :]
        s1 = s0 * jnp.exp(g_end0) + jnp.dot(v_t, ke0, preferred_element_type=F32)
        s2 = s1 * jnp.exp(g_end1) + jnp.dot(v_t, ke1, preferred_element_type=F32)
        gs_ref[st_rows, :] = s2
        tdims = (((1,), (1,)), ((), ()))
        o0 = lax.dot_general(q_dec, s0.astype(BF16), tdims, preferred_element_type=F32)
        o1 = lax.dot_general(q_dec, s1.astype(BF16), tdims, preferred_element_type=F32)
        o = o + jnp.where(first, o0, o1)
        o = _rms(o, gn_ref[...]) * jax.nn.silu(r_all[:, vs].astype(F32))
        o_ref[0, :, C_INNER + h * D_HV:C_INNER + (h + 1) * D_HV] = o.astype(o_ref.dtype)


def _ssd_gla(z, q, k, v, r, xbc, side_dt, side_g, g_lane, p):
    b, s, _ = z.shape
    q_len = C_CHUNK
    halo = 16
    nb = SSD_BATCHES if b % SSD_BATCHES == 0 else 1
    chunk = lambda width: pl.BlockSpec((nb, q_len, width), lambda i, n: (i, n, 0))
    full = lambda shape: pl.BlockSpec(shape, lambda i, n: (0,) * len(shape))
    pad_lanes = lambda vec: jnp.zeros((1, LANES), F32).at[0, :vec.shape[0]].set(vec.astype(F32))
    gkw = jnp.zeros((LANES, D_KEY), F32).at[g_lane:g_lane + D_GATE_RANK].set(p["gk_w"]).astype(BF16)
    expand = jnp.asarray(np.arange(C_INNER)[None, :] // C_HEAD_DIM == np.arange(LANES)[:, None], BF16)
    return pl.pallas_call(
        _ssd_gla_kernel,
        grid=(b // nb, s // q_len),
        in_specs=[
            chunk(C_INNER), chunk(D_KEY), chunk(D_KEY), chunk(D_VAL), chunk(D_VAL), chunk(C_CONV_DIM),
            pl.BlockSpec((nb, halo, C_CONV_DIM), lambda i, n: (i, jnp.maximum(n * (q_len // halo) - 1, 0), 0)),
            chunk(LANES), chunk(LANES),
            full((C_CONV, C_CONV_DIM)), full((1, C_CONV_DIM)),
            full((1, LANES)), full((1, LANES)), full((1, C_INNER)), full((LANES, C_INNER)),
            full((1, C_INNER)), full((LANES, D_KEY)), full((1, D_KEY)), full((1, D_HV)),
        ],
        out_specs=pl.BlockSpec((nb, q_len, C_INNER + D_VAL), lambda i, n: (i, n, 0)),
        out_shape=jax.ShapeDtypeStruct((b, s, C_INNER + D_VAL), BF16),
        scratch_shapes=[pltpu.VMEM((nb, C_INNER, C_STATE), F32),
                        pltpu.VMEM((nb, D_VAL, D_HK), F32)],
        compiler_params=_cparams(("parallel", "arbitrary")),
        name="ssd_gla",
    )(z, q, k, v, r, xbc, xbc, side_dt, side_g,
      p["conv_w"].astype(F32), p["conv_b"].reshape(1, -1).astype(F32),
      pad_lanes(p["dt_bias"]), pad_lanes(p["a_log"]),
      jnp.repeat(p["d_skip"].astype(F32), C_HEAD_DIM).reshape(1, C_INNER), expand,
      p["ssd_norm"].reshape(1, -1).astype(F32), gkw, p["gk_b"].reshape(1, -1).astype(F32),
      p["gla_norm"].reshape(1, -1).astype(F32))


def _store_rows_tiled(ref, val):
    m, d = val.shape
    nc = d // LANES
    for c in range(nc):
        ref[pl.ds(c, m, stride=nc), :] = val[:, c * LANES:(c + 1) * LANES]


def _load_rows_tiled(ref, m, dtype=None):
    nc = ref.shape[0] // m
    parts = [ref[pl.ds(c, m, stride=nc), :] for c in range(nc)]
    if dtype is not None:
        parts = [p.astype(dtype) for p in parts]
    return jnp.concatenate(parts, axis=1)


def _pack_pairs(x):
    n = x.shape[1] // 2
    u = pltpu.bitcast(x.astype(BF16).astype(F32), jnp.uint32)
    return (u[:, :n] >> 16) | (u[:, n:] & jnp.uint32(0xFFFF0000))


def _unpack_pairs(u):
    lo = pltpu.bitcast(u << 16, F32).astype(BF16)
    hi = pltpu.bitcast(u & jnp.uint32(0xFFFF0000), F32).astype(BF16)
    return jnp.concatenate([lo, hi], axis=1)


RT_GATE, RT_EXPERT, RT_RANK = 0, 2, 4


def _route_block(lg, carry, earlier):
    m = lg.shape[0]
    lane = lax.broadcasted_iota(jnp.int32, (m, LANES), 1)
    lane_f = lane.astype(F32)
    none = float(LANES)
    neg = -jnp.inf
    first_max = lambda v, vmax: jnp.min(jnp.where(v == vmax, lane_f, none), axis=-1, keepdims=True)
    gl = jnp.where(lane < MOE_GROUPS, lg, neg)
    gmax = jnp.max(gl, axis=-1, keepdims=True)
    g_w = 1.0 / jnp.sum(jnp.exp(gl - gmax), axis=-1, keepdims=True)
    lo = MOE_GROUPS + first_max(gl, gmax) * MOE_EPG
    el = jnp.where((lane_f >= lo) & (lane_f < lo + MOE_EPG), lg, neg)
    emax = jnp.max(el, axis=-1, keepdims=True)
    esum = jnp.sum(jnp.exp(el - emax), axis=-1, keepdims=True)
    l0 = first_max(el, emax)
    el2 = jnp.where(lane_f == l0, neg, el)
    emax2 = jnp.max(el2, axis=-1, keepdims=True)
    l1 = first_max(el2, emax2)
    p0 = 1.0 / esum
    p1 = jnp.exp(emax2 - emax) / esum
    w0 = g_w * (p0 / (p0 + p1))
    w1 = g_w * (p1 / (p0 + p1))
    oh0 = lane_f == l0
    oh1 = lane_f == l1
    oh = (oh0 | oh1).astype(BF16)
    cum = jnp.dot(earlier, oh, preferred_element_type=F32) + carry
    rank0 = jnp.sum(jnp.where(oh0, cum, 0.0), axis=-1, keepdims=True)
    rank1 = jnp.sum(jnp.where(oh1, cum, 0.0), axis=-1, keepdims=True)
    carry = carry + jnp.sum(oh.astype(F32), axis=0, keepdims=True)
    rec = jnp.zeros((m, LANES), F32)
    for pos, val in ((RT_GATE, w0), (RT_GATE + 1, w1), (RT_EXPERT, l0 - MOE_GROUPS),
                     (RT_EXPERT + 1, l1 - MOE_GROUPS), (RT_RANK, rank0), (RT_RANK + 1, rank1)):
        rec = jnp.where(lane == pos, val, rec)
    return rec, carry


def _out_proj_kernel(n_parts, *refs):
    a_refs = refs[:n_parts]
    w_refs = refs[n_parts:2 * n_parts]
    h_ref, g_ref, wr_ref, br_ref, tri_ref, ho_ref, xt_ref, rt_ref, cnt_ref, carry_ref = refs[2 * n_parts:]

    @pl.when(pl.program_id(0) == 0)
    def _():
        carry_ref[...] = jnp.zeros_like(carry_ref)

    acc = h_ref[...]
    for a_ref, w_ref in zip(a_refs, w_refs):
        acc = acc + jnp.dot(a_ref[...], w_ref[...], preferred_element_type=F32)
    ho_ref[...] = acc
    xn = _rms(acc, g_ref[...])
    xt_ref[...] = xn.astype(xt_ref.dtype)
    x_hi, x_mid, _ = _split3(xn)
    wr = wr_ref[...]
    lg2 = jnp.dot(x_hi, wr, preferred_element_type=F32)
    lg = (lg2[:, :LANES] + lg2[:, LANES:] + jnp.dot(x_mid, wr[:, :LANES], preferred_element_type=F32)
          + br_ref[...])
    rec, carry = _route_block(lg, carry_ref[...], tri_ref[...])
    rt_ref[...] = rec
    carry_ref[...] = carry
    cnt_ref[0] = carry


def _out_proj(parts, w_parts, h, g, w_route, b_route, *, tm=512):
    t, d = h.shape
    tm = min(tm, t)
    nc = d // LANES
    row = lambda i: (i, 0)
    const = lambda i: (0, 0)
    in_specs = [pl.BlockSpec((tm, a.shape[1]), row) for a in parts]
    in_specs += [pl.BlockSpec(w.shape, const) for w in w_parts]
    in_specs += [pl.BlockSpec((tm, d), row), pl.BlockSpec((1, d), const),
                 pl.BlockSpec((d, 2 * LANES), const), pl.BlockSpec((1, LANES), const),
                 pl.BlockSpec((tm, tm), const)]
    earlier = jnp.asarray(np.tril(np.ones((tm, tm), np.float32), -1), BF16)
    return pl.pallas_call(
        functools.partial(_out_proj_kernel, len(parts)),
        grid=(t // tm,),
        in_specs=in_specs,
        out_specs=[pl.BlockSpec((tm, d), row), pl.BlockSpec((tm, d), row),
                   pl.BlockSpec((tm, LANES), row), pl.BlockSpec((1, 1, LANES), lambda i: (i, 0, 0))],
        out_shape=[jax.ShapeDtypeStruct((t, d), F32), jax.ShapeDtypeStruct((t, d), BF16),
                   jax.ShapeDtypeStruct((t, LANES), F32), jax.ShapeDtypeStruct((t // tm, 1, LANES), F32)],
        scratch_shapes=[pltpu.VMEM((1, LANES), F32)],
        compiler_params=_cparams(("arbitrary",)),
        name="out_proj",
    )(*parts, *w_parts, h, g.reshape(1, d), w_route, b_route, earlier)


LONG_RUN = 128


def _piece_sizes(max_rows, lo=1, hi=None):
    sizes = [1 << b for b in reversed(range(max_rows.bit_length()))]
    return [n for n in sizes if n >= lo and (hi is None or n < hi)]


def _for_each_piece(run, sizes, body):
    off = run // (2 * sizes[0]) * (2 * sizes[0])
    for n in sizes:
        hit = (run & n) != 0
        pl.when(hit)(functools.partial(body, off, n))
        off = off + jnp.where(hit, n, 0)


def _for_each_run(len_ref, tile, long_ref, max_rows, make_body):
    def sweep(sizes):
        def per_expert(e, first):
            run = len_ref[tile * MOE_EXPERTS + e]
            _for_each_piece(run, sizes, make_body(e, first))
            return first + run

        lax.fori_loop(0, MOE_EXPERTS, per_expert, jnp.int32(0))

    sweep(_piece_sizes(max_rows, hi=LONG_RUN))
    pl.when(long_ref[tile] != 0)(functools.partial(sweep, _piece_sizes(max_rows, lo=LONG_RUN)))


def _zero_fill_rows(rows_ref, z_ref, zsem, lo_ref, hi_ref, n_rows, nc):
    zb = z_ref.shape[0] // nc
    z_ref[...] = jnp.zeros_like(z_ref)
    assert MOE_TILE // 2 <= zb and MOE_TILE % zb == 0

    def piece(row0, n):
        return pltpu.make_async_copy(z_ref.at[pl.ds(0, n * nc)], rows_ref.at[pl.ds(row0 * nc, n * nc)], zsem)

    def sweep(issue):
        def per_expert(e, carry):
            lo = lo_ref[e]

            def one(off, n):
                piece(lo + off, n).start() if issue else piece(lo + off, n).wait()

            _for_each_piece(hi_ref[e] - lo, _piece_sizes(MOE_TILE // 2), one)
            return carry

        def per_block(i, carry):
            piece(i * zb, zb).start() if issue else piece(i * zb, zb).wait()
            return carry

        lax.fori_loop(0, MOE_EXPERTS, per_expert, 0)
        lax.fori_loop(hi_ref[MOE_EXPERTS - 1] // zb, n_rows // zb, per_block, 0)

    sweep(True)
    sweep(False)


def _tile_positions(rt, pos_base):
    lane_f = lax.broadcasted_iota(jnp.int32, rt.shape, 1).astype(F32)
    out = []
    for k in range(MOE_TOPK):
        e_lane = rt[:, RT_EXPERT + k:RT_EXPERT + k + 1] + MOE_GROUPS
        out.append(rt[:, RT_RANK + k:RT_RANK + k + 1]
                   + jnp.sum(jnp.where(lane_f == e_lane, pos_base, 0.0), axis=-1, keepdims=True))
    return out


def _moe_dispatch_kernel(dst_ref, len_ref, long_ref, lo_ref, hi_ref, x_ref, rt_ref, pb_ref, rows_ref,
                         sbuf, z_ref, sem, zsem, *, tm, nc, n_steps, n_rows):
    i = pl.program_id(0)
    rows = MOE_TOPK * tm
    slot = lax.rem(i, 2)

    def wait_slot(s):
        pltpu.make_async_copy(sbuf.at[s], rows_ref.at[pl.ds(0, rows * nc)], sem.at[s]).wait()

    @pl.when(i == 0)
    def _():
        _zero_fill_rows(rows_ref, z_ref, zsem, lo_ref, hi_ref, n_rows, nc)

    @pl.when(i >= 2)
    def _():
        wait_slot(slot)

    pos = _tile_positions(rt_ref[...], pb_ref[0])
    lane = lax.broadcasted_iota(jnp.int32, (tm, LANES), 1)
    pos_t = jnp.where(lane == 0, pos[0], jnp.where(lane == 1, pos[1], 0.0)).T
    p_iota = lax.broadcasted_iota(jnp.int32, (rows, tm), 0).astype(F32)
    place = ((p_iota == pos_t[0:1, :]) | (p_iota == pos_t[1:2, :])).astype(BF16)
    _store_rows_tiled(sbuf.at[slot], _pack_pairs(jnp.dot(place, x_ref[...], preferred_element_type=F32)))

    def sender(e, src):
        dst = dst_ref[i * MOE_EXPERTS + e]

        def send(off, n):
            pltpu.make_async_copy(sbuf.at[slot, pl.ds((src + off) * nc, n * nc)],
                                  rows_ref.at[pl.ds((dst + off) * nc, n * nc)], sem.at[slot]).start()

        return send

    _for_each_run(len_ref, i, long_ref, rows, sender)

    @pl.when(i == n_steps - 1)
    def _():
        wait_slot(slot)
        if n_steps > 1:
            wait_slot(1 - slot)


def _moe_dispatch(x, route, pos_base, seg_dst, seg_len, seg_long, pad_lo, pad_hi, n_rows, *, tm):
    t, d = x.shape
    nc = d // (2 * LANES)
    n_steps = t // tm
    row = lambda i, *_: (i, 0)
    return pl.pallas_call(
        functools.partial(_moe_dispatch_kernel, tm=tm, nc=nc, n_steps=n_steps, n_rows=n_rows),
        grid_spec=pltpu.PrefetchScalarGridSpec(
            num_scalar_prefetch=5, grid=(n_steps,),
            in_specs=[pl.BlockSpec((tm, d), row), pl.BlockSpec((tm, LANES), row),
                      pl.BlockSpec((1, 1, LANES), lambda i, *_: (i, 0, 0))],
            out_specs=pl.BlockSpec(memory_space=pl.ANY),
            scratch_shapes=[pltpu.VMEM((2, MOE_TOPK * tm * nc, LANES), jnp.uint32),
                            pltpu.VMEM((MOE_TILE // 2 * nc, LANES), jnp.uint32),
                            pltpu.SemaphoreType.DMA((2,)), pltpu.SemaphoreType.DMA(())]),
        out_shape=jax.ShapeDtypeStruct((n_rows * nc, LANES), jnp.uint32),
        compiler_params=pltpu.CompilerParams(dimension_semantics=("arbitrary",), has_side_effects=True,
                                             vmem_limit_bytes=VMEM_LIMIT),
        name="moe_dispatch",
    )(seg_dst, seg_len, seg_long, pad_lo, pad_hi, x, route, pos_base)


def _moe_kernel(te_ref, nu_ref, x_ref, wg_ref, wu_ref, wd_ref, y_ref):
    i = pl.program_id(0)

    @pl.when(i >= nu_ref[0])
    def _():
        y_ref[...] = jnp.zeros_like(y_ref)

    @pl.when(i < nu_ref[0])
    def _():
        x = _unpack_pairs(_load_rows_tiled(x_ref, MOE_TILE))
        gate = jnp.dot(x, wg_ref[...].astype(BF16), preferred_element_type=F32)
        up = jnp.dot(x, wu_ref[...].astype(BF16), preferred_element_type=F32)
        act = (jax.nn.silu(gate) * up).astype(BF16)
        y = jnp.dot(act, wd_ref[...].astype(BF16), preferred_element_type=F32)
        _store_rows_tiled(y_ref, _pack_pairs(y))


def _moe_experts(x_rows, tile_expert, n_used, w_gate, w_up, w_down, layer):
    d, ff = w_gate.shape[-2:]
    nc = d // (2 * LANES)
    n_tiles = x_rows.shape[0] // (MOE_TILE * nc)
    live = lambda i, nu: jnp.minimum(i, nu[0] - 1)
    w_spec = lambda a, b: pl.BlockSpec((None, None, a, b), lambda i, te, nu: (layer, te[live(i, nu)], 0, 0))
    return pl.pallas_call(
        _moe_kernel,
        grid_spec=pltpu.PrefetchScalarGridSpec(
            num_scalar_prefetch=2, grid=(n_tiles,),
            in_specs=[pl.BlockSpec((MOE_TILE * nc, LANES), lambda i, te, nu: (live(i, nu), 0)),
                      w_spec(d, ff), w_spec(d, ff), w_spec(ff, d)],
            out_specs=pl.BlockSpec((MOE_TILE * nc, LANES), lambda i, te, nu: (i, 0))),
        out_shape=jax.ShapeDtypeStruct(x_rows.shape, x_rows.dtype),
        compiler_params=_cparams(("arbitrary",)),
        name="moe_experts",
    )(tile_expert, n_used, x_rows, w_gate, w_up, w_down)


def _moe_combine_kernel(src_ref, len_ref, long_ref, h_ref, rt_ref, pb_ref, g_ref, y_hbm, o_ref, ybuf, sem,
                        *, tm, nc, final):
    i = pl.program_id(0)
    n_steps = pl.num_programs(0)
    rows = MOE_TOPK * tm

    def fetch(tile, slot):
        def receiver(e, dst):
            src = src_ref[tile * MOE_EXPERTS + e]

            def recv(off, n):
                pltpu.make_async_copy(y_hbm.at[pl.ds((src + off) * nc, n * nc)],
                                      ybuf.at[slot, pl.ds((dst + off) * nc, n * nc)], sem.at[slot]).start()

            return recv

        _for_each_run(len_ref, tile, long_ref, rows, receiver)

    slot = lax.rem(i, 2)

    @pl.when(i == 0)
    def _():
        fetch(0, 0)

    @pl.when(i + 1 < n_steps)
    def _():
        fetch(i + 1, 1 - slot)

    pltpu.make_async_copy(y_hbm.at[pl.ds(0, rows * nc)], ybuf.at[slot], sem.at[slot]).wait()
    y = _unpack_pairs(_load_rows_tiled(ybuf.at[slot], rows))
    rt = rt_ref[...]
    pos_f = lax.broadcasted_iota(jnp.int32, (tm, rows), 1).astype(F32)
    pick = jnp.zeros((tm, rows), F32)
    for k, pos in enumerate(_tile_positions(rt, pb_ref[0])):
        pick = jnp.where(pos_f == pos, rt[:, RT_GATE + k:RT_GATE + k + 1], pick)
    out = h_ref[...] + jnp.dot(pick.astype(BF16), y, preferred_element_type=F32)
    o_ref[...] = _rms(out, g_ref[...]) if final else out


def _moe_combine_rows(h, y_rows, route, seg_src, seg_len, seg_long, pos_base, g, *, tm, final):
    t, d = h.shape
    nc = d // (2 * LANES)
    row = lambda i, *_: (i, 0)
    return pl.pallas_call(
        functools.partial(_moe_combine_kernel, tm=tm, nc=nc, final=final),
        grid_spec=pltpu.PrefetchScalarGridSpec(
            num_scalar_prefetch=3, grid=(t // tm,),
            in_specs=[pl.BlockSpec((tm, d), row), pl.BlockSpec((tm, LANES), row),
                      pl.BlockSpec((1, 1, LANES), lambda i, *_: (i, 0, 0)),
                      pl.BlockSpec((1, d), lambda i, *_: (0, 0)),
                      pl.BlockSpec(memory_space=pl.ANY)],
            out_specs=pl.BlockSpec((tm, d), row),
            scratch_shapes=[pltpu.VMEM((2, MOE_TOPK * tm * nc, LANES), y_rows.dtype),
                            pltpu.SemaphoreType.DMA((2,))]),
        out_shape=jax.ShapeDtypeStruct((t, d), F32),
        compiler_params=_cparams(("arbitrary",)),
        name="moe_combine",
    )(seg_src, seg_len, seg_long, h, route, pos_base, g.reshape(1, d), y_rows)


def _moe(h, x, route, tile_counts, w_gate, w_up, w_down, layer, g, *, final):
    t = route.shape[0]
    n_tt = tile_counts.shape[0]
    tm = t // n_tt
    after = tile_counts[:, 0, MOE_GROUPS:MOE_GROUPS + MOE_EXPERTS].astype(jnp.int32)
    before = jnp.concatenate([jnp.zeros((1, MOE_EXPERTS), jnp.int32), after[:-1]], axis=0)
    cnt = after[-1]
    padded = (cnt + MOE_TILE - 1) // MOE_TILE * MOE_TILE
    pad_ends = jnp.cumsum(padded)
    starts = (pad_ends - padded).astype(jnp.int32)
    n_tiles = (t * MOE_TOPK + MOE_EXPERTS * (MOE_TILE - 1)) // MOE_TILE
    tile_start = jnp.arange(n_tiles, dtype=jnp.int32) * MOE_TILE
    tile_expert = jnp.minimum(jnp.sum(tile_start[:, None] >= pad_ends[None, :], axis=1),
                              MOE_EXPERTS - 1).astype(jnp.int32)
    n_used = (pad_ends[-1] // MOE_TILE).astype(jnp.int32).reshape(1)
    n_rows = n_tiles * MOE_TILE
    seg_len = (after - before).reshape(-1)
    seg_off = jnp.cumsum(after - before, axis=1) - (after - before)
    seg_row = (starts[None, :] + before).reshape(-1)
    pos_base = jnp.zeros((n_tt, 1, LANES), F32).at[:, 0, MOE_GROUPS:MOE_GROUPS + MOE_EXPERTS].set(
        (seg_off - before).astype(F32))
    seg_long = jnp.any(after - before >= LONG_RUN, axis=1).astype(jnp.int32)
    x_rows = _moe_dispatch(x, route, pos_base, seg_row, seg_len, seg_long, starts + cnt,
                           pad_ends.astype(jnp.int32), n_rows, tm=tm)
    y_rows = _moe_experts(x_rows, tile_expert, n_used, w_gate, w_up, w_down, layer)
    return _moe_combine_rows(h, y_rows, route, seg_row, seg_len, seg_long, pos_base, g, tm=tm, final=final)


def _router_weights(w_group, b_group, w_router, b_router):
    d = w_group.shape[0]
    w = jnp.zeros((d, LANES), F32)
    w = w.at[:, :MOE_GROUPS].set(w_group).at[:, MOE_GROUPS:MOE_GROUPS + MOE_EXPERTS].set(w_router)
    b = jnp.zeros((1, LANES), F32)
    b = b.at[0, :MOE_GROUPS].set(b_group).at[0, MOE_GROUPS:MOE_GROUPS + MOE_EXPERTS].set(b_router)
    w_hi = w.astype(BF16)
    w_mid = (w - w_hi.astype(F32)).astype(BF16)
    return jnp.concatenate([w_hi, w_mid], axis=1), b


def kernel(x, norm_mix, norm_moe, norm_final, even_w_in, even_sinks, even_forget_bias, even_w_out,
           odd_w_in, odd_conv_w, odd_conv_b, odd_dt_bias, odd_a_log, odd_d_skip, odd_ssd_norm,
           odd_gk_w, odd_gk_b, odd_gla_norm, odd_w_out, moe_w_group, moe_b_group, moe_w_router,
           moe_b_router, moe_w_gate, moe_w_up, moe_w_down):
    b, s, d = x.shape
    t = b * s
    depth = norm_mix.shape[0]
    h = x.reshape(t, d)
    for layer in range(depth):
        i = layer // 2
        if layer % 2 == 0:
            w = even_w_in[i]
            n_ab = (A_Q_HEADS + 2 * A_KV_HEADS + 3 * B_HEADS) * HEAD_DIM
            w_main = w[:, :n_ab].astype(BF16)
            w_aux = jnp.zeros((d, LANES), F32).at[:, :B_HEADS].set(w[:, n_ab:]).astype(BF16)
            n_a, kv_a, n_b = (A_Q_HEADS + 2 * A_KV_HEADS) * HEAD_DIM, A_KV_HEADS * HEAD_DIM, B_HEADS * HEAD_DIM
            swa_rows, fox_rows = SWA_QBLOCKS * A_WINDOW, 256
            proj, f_aux, va_t, vb_t = _norm_proj(
                h, norm_mix[layer], [w_main, w_aux], ((0, 0, n_ab, BF16), (1, 0, LANES, F32)), tm=1024,
                t_plan=((0, n_a - kv_a, kv_a, BF16, swa_rows), (0, n_ab - n_b, n_b, BF16, fox_rows)))
            proj = proj.reshape(b, s, -1)
            out_a = _swa(proj, va_t.reshape(b, s // swa_rows, kv_a, swa_rows), even_sinks[i])
            c, ct = _fox_gate(f_aux.reshape(b, s, LANES), even_forget_bias[i])
            out_b = _fox(proj, vb_t.reshape(b, s // fox_rows, n_b, fox_rows), c, ct, tk=fox_rows)
            n_ha = A_Q_HEADS * HEAD_DIM
            w_out = even_w_out[i].astype(BF16)
            parts = [out_a.reshape(t, -1), out_b.reshape(t, -1)]
            w_parts = [w_out[:n_ha], w_out[n_ha:]]
        else:
            w = odd_w_in[i]
            o_z, o_xbc = 0, C_INNER
            o_dt = o_xbc + C_CONV_DIM
            o_q = o_dt + C_HEADS
            o_k = o_q + D_KEY
            o_v = o_k + D_KEY
            o_g = o_v + D_VAL
            o_r = o_g + D_GATE_RANK
            n_b = w.shape[1] - o_dt
            w_a = w[:, :o_dt].astype(BF16)
            w_b = jnp.pad(w[:, o_dt:].astype(BF16), ((0, 0), (0, -n_b % LANES)))
            g_win = (o_g - o_dt) // LANES * LANES
            plan = ((0, o_z, C_INNER, BF16), (1, o_q - o_dt, D_KEY, BF16), (1, o_k - o_dt, D_KEY, BF16),
                    (1, o_v - o_dt, D_VAL, BF16), (1, o_r - o_dt, D_VAL, BF16), (0, o_xbc, C_CONV_DIM, BF16),
                    (1, 0, LANES, F32), (1, g_win, LANES, F32))
            outs = _norm_proj(h, norm_mix[layer], [w_a, w_b], plan, tm=512)
            params = dict(conv_w=odd_conv_w[i], conv_b=odd_conv_b[i], dt_bias=odd_dt_bias[i], a_log=odd_a_log[i],
                          d_skip=odd_d_skip[i], ssd_norm=odd_ssd_norm[i], gk_w=odd_gk_w[i], gk_b=odd_gk_b[i],
                          gla_norm=odd_gla_norm[i])
            mixed = _ssd_gla(*[o.reshape(b, s, -1) for o in outs], o_g - o_dt - g_win, params)
            parts = [mixed.reshape(t, -1)]
            w_parts = [odd_w_out[i].astype(BF16)]
        w_route, b_route = _router_weights(moe_w_group[layer], moe_b_group[layer],
                                           moe_w_router[layer], moe_b_router[layer])
        h, x_tiled, route, tile_counts = _out_proj(parts, w_parts, h, norm_moe[layer], w_route, b_route)
        h = _moe(h, x_tiled, route, tile_counts, moe_w_gate, moe_w_up, moe_w_down, layer, norm_final,
                 final=layer == depth - 1)
    out = h
    return out.reshape(b, s, d)
```

```python
import functools

import numpy as np
import jax
import jax.numpy as jnp
from jax import lax
from jax.experimental import pallas as pl
from jax.experimental.pallas import tpu as pltpu

F32 = jnp.float32
BF16 = jnp.bfloat16

RMS_EPS = 1e-6
HEAD_DIM = 64
A_Q_HEADS = 8
A_KV_HEADS = 2
A_GROUP = A_Q_HEADS // A_KV_HEADS
A_WINDOW = 128
B_HEADS = 8
C_HEADS = 16
C_HEAD_DIM = 64
C_INNER = C_HEADS * C_HEAD_DIM
C_GROUPS = 2
C_HPG = C_HEADS // C_GROUPS
C_STATE = 128
C_CONV = 4
C_CHUNK = 128
C_CONV_DIM = C_INNER + 2 * C_GROUPS * C_STATE
D_HEADS = 4
D_HK = 128
D_HV = 256
D_KEY = D_HEADS * D_HK
D_VAL = D_HEADS * D_HV
D_GATE_RANK = 16
D_GATE_NORM = 16.0
D_CHUNK = 64
MOE_GROUPS = 4
MOE_EPG = 8
MOE_EXPERTS = MOE_GROUPS * MOE_EPG
MOE_TOPK = 2

LANES = 128
VMEM_LIMIT = 48 * 1024 * 1024
MOE_TILE = 512


def _cparams(sem):
    return pltpu.CompilerParams(dimension_semantics=sem, vmem_limit_bytes=VMEM_LIMIT)


def _rms(x, g):
    ms = jnp.mean(x * x, axis=-1, keepdims=True)
    return x * lax.rsqrt(ms + RMS_EPS) * g


def _norm_proj_kernel(n_w, plan, t_plan, *refs):
    x_ref, g_ref = refs[:2]
    w_refs = refs[2:2 + n_w]
    o_refs = refs[2 + n_w:2 + n_w + len(plan)]
    t_refs = refs[2 + n_w + len(plan):2 + n_w + len(plan) + len(t_plan)]
    res_refs = refs[2 + n_w + len(plan) + len(t_plan):]
    xn = _rms(x_ref[...], g_ref[...]).astype(BF16)
    for w_ref, res_ref in zip(w_refs, res_refs):
        res_ref[...] = jnp.dot(xn, w_ref[...], preferred_element_type=F32)
    for o_ref, (wi, start, width, _) in zip(o_refs, plan):
        o_ref[...] = res_refs[wi][:, start:start + width].astype(o_ref.dtype)
    for o_ref, (wi, start, width, _, rows) in zip(t_refs, t_plan):
        for u in range(o_ref.shape[0]):
            o_ref[u] = res_refs[wi][u * rows:(u + 1) * rows, start:start + width].T.astype(o_ref.dtype)


def _norm_proj(x, g, weights, plan, *, tm, t_plan=()):
    t, d = x.shape
    tm = min(tm, t)
    row = lambda i: (i, 0)
    const = lambda i: (0, 0)
    in_specs = [pl.BlockSpec((tm, d), row), pl.BlockSpec((1, d), const)]
    in_specs += [pl.BlockSpec(w.shape, const, pipeline_mode=pl.Buffered(1)) for w in weights]
    return pl.pallas_call(
        functools.partial(_norm_proj_kernel, len(weights), plan, t_plan),
        grid=(t // tm,),
        in_specs=in_specs,
        out_specs=([pl.BlockSpec((tm, width), row) for _, _, width, _ in plan]
                   + [pl.BlockSpec((tm // rows, width, rows), lambda i: (i, 0, 0))
                      for _, _, width, _, rows in t_plan]),
        out_shape=([jax.ShapeDtypeStruct((t, width), dtype) for _, _, width, dtype in plan]
                   + [jax.ShapeDtypeStruct((t // rows, width, rows), dtype) for _, _, width, dtype, rows in t_plan]),
        scratch_shapes=[pltpu.VMEM((tm, w.shape[1]), F32) for w in weights],
        compiler_params=_cparams(("parallel",)),
        name="norm_proj",
    )(x, g.reshape(1, d), *weights)


SWA_QBLOCKS = 8


def _swa_kernel(sink_ref, slope_ref, q_ref, kp_ref, kc_ref, vp_ref, vc_ref, o_ref):
    n = pl.program_id(1)
    blk = A_WINDOW
    wide = A_GROUP * blk
    key = lax.broadcasted_iota(jnp.int32, (2 * blk, wide), 0)
    qry = lax.broadcasted_iota(jnp.int32, (2 * blk, wide), 1) % blk
    dist = blk + qry - key
    in_window = (dist >= 0) & (dist < A_WINDOW)
    distf = dist.astype(F32)
    nt = (((1,), (1,)), ((), ()))
    k_all = jnp.concatenate([kp_ref[0], kc_ref[0]], axis=0)
    v_all = jnp.concatenate([vp_ref[0], vc_ref[0]], axis=1)
    units = [(j, kh) for j in range(SWA_QBLOCKS) for kh in range(A_KV_HEADS)]
    scores = []
    for j, kh in units:
        k = k_all[j * blk:(j + 2) * blk, kh * HEAD_DIM:(kh + 1) * HEAD_DIM]
        q = jnp.concatenate([q_ref[0, j * blk:(j + 1) * blk,
                                   (kh * A_GROUP + g) * HEAD_DIM:(kh * A_GROUP + g + 1) * HEAD_DIM]
                             for g in range(A_GROUP)], axis=0)
        scores.append(lax.dot_general(k, q, nt, preferred_element_type=F32))
    probs = []
    for (j, kh), s in zip(units, scores):
        valid = in_window & ((key >= blk) | (n * SWA_QBLOCKS + j > 0))
        s = s * (HEAD_DIM ** -0.5) - slope_ref[kh:kh + 1, :] * distf
        s = jnp.where(valid, s, -jnp.inf)
        sink = sink_ref[kh:kh + 1, :]
        m = jnp.maximum(jnp.max(s, axis=0, keepdims=True), sink)
        p = jnp.exp(s - m)
        probs.append((p.astype(BF16), jnp.sum(p, axis=0, keepdims=True) + jnp.exp(sink - m)))
    for j in range(SWA_QBLOCKS):
        outs = []
        for kh in range(A_KV_HEADS):
            p, denom = probs[j * A_KV_HEADS + kh]
            v_t = v_all[kh * HEAD_DIM:(kh + 1) * HEAD_DIM, j * blk:(j + 2) * blk]
            o_t = jnp.dot(v_t, p, preferred_element_type=F32) / denom
            outs += [o_t[:, g * blk:(g + 1) * blk] for g in range(A_GROUP)]
        o_ref[0, j * blk:(j + 1) * blk, :] = jnp.concatenate(outs, axis=0).T.astype(o_ref.dtype)


def _swa(proj, v_t, sinks):
    b, s, _ = proj.shape
    blk = A_WINDOW
    qw = A_Q_HEADS * HEAD_DIM
    kw = A_KV_HEADS * HEAD_DIM
    k_blk = qw // kw
    per_lane = lambda vec: jnp.repeat(vec.astype(F32), blk).reshape(A_KV_HEADS, A_GROUP * blk)
    slopes = jnp.asarray(2.0 ** (-8.0 * np.arange(1, A_Q_HEADS + 1) / A_Q_HEADS), F32)
    tq = SWA_QBLOCKS * blk
    assert v_t.shape == (b, s // tq, kw, tq)
    prev = lambda n: jnp.maximum(n * SWA_QBLOCKS - 1, 0)
    full = pl.BlockSpec((A_KV_HEADS, A_GROUP * blk), lambda i, n: (0, 0))
    return pl.pallas_call(
        _swa_kernel,
        grid=(b, s // tq),
        in_specs=[
            full, full,
            pl.BlockSpec((1, tq, qw), lambda i, n: (i, n, 0)),
            pl.BlockSpec((1, blk, kw), lambda i, n: (i, prev(n), k_blk)),
            pl.BlockSpec((1, tq, kw), lambda i, n: (i, n, k_blk)),
            pl.BlockSpec((1, None, kw, blk), lambda i, n: (i, jnp.maximum(n - 1, 0), 0, SWA_QBLOCKS - 1)),
            pl.BlockSpec((1, None, kw, tq), lambda i, n: (i, n, 0, 0)),
        ],
        out_specs=pl.BlockSpec((1, tq, qw), lambda i, n: (i, n, 0)),
        out_shape=jax.ShapeDtypeStruct((b, s, qw), BF16),
        compiler_params=_cparams(("parallel", "parallel")),
        name="swa",
    )(per_lane(sinks), per_lane(slopes), proj, proj, proj, v_t, v_t)


def _tril(n, dtype=F32):
    r = lax.broadcasted_iota(jnp.int32, (n, n), 0)
    c = lax.broadcasted_iota(jnp.int32, (n, n), 1)
    return (c <= r).astype(dtype)


def _split3(x):
    hi = x.astype(BF16)
    r = x - hi.astype(F32)
    mid = r.astype(BF16)
    return hi, mid, (r - mid.astype(F32)).astype(BF16)


def _dot_mask_lhs(mask, x):
    return sum(jnp.dot(mask, part, preferred_element_type=F32) for part in _split3(x))


def _dot_mask_rhs(x, mask):
    return sum(jnp.dot(part, mask, preferred_element_type=F32) for part in _split3(x))


def _fox_gate_kernel(f_ref, b_ref, c_ref, ct_ref):
    tri = _tril(LANES, BF16)
    carry = jnp.zeros((1, LANES), F32)
    for n in range(f_ref.shape[1] // LANES):
        rows = slice(n * LANES, (n + 1) * LANES)
        lf = jax.nn.log_sigmoid(f_ref[0, rows, :] + b_ref[...])
        cs = _dot_mask_lhs(tri, lf) + carry
        carry = cs[LANES - 1:LANES, :]
        c_ref[0, rows, :] = cs
        ct_ref[0, n] = cs.T[:B_HEADS, :]


def _fox_gate(f_aux, bias):
    b, s, _ = f_aux.shape
    nb = s // LANES
    bias_p = jnp.zeros((1, LANES), F32).at[0, :B_HEADS].set(bias.astype(F32))
    return pl.pallas_call(
        _fox_gate_kernel,
        grid=(b,),
        in_specs=[pl.BlockSpec((1, s, LANES), lambda i: (i, 0, 0)),
                  pl.BlockSpec((1, LANES), lambda i: (0, 0))],
        out_specs=[pl.BlockSpec((1, s, LANES), lambda i: (i, 0, 0)),
                   pl.BlockSpec((1, nb, B_HEADS, LANES), lambda i: (i, 0, 0, 0))],
        out_shape=[jax.ShapeDtypeStruct((b, s, LANES), F32),
                   jax.ShapeDtypeStruct((b, nb, B_HEADS, LANES), F32)],
        compiler_params=_cparams(("parallel",)),
        name="fox_gate",
    )(f_aux, bias_p)


def _fox_kernel(q0_ref, q1_ref, k0_ref, k1_ref, vt_ref, c_ref, ctq_ref, o_ref, *, tq, tk, heads_per_step):
    qi = pl.program_id(1)
    sub = tq // LANES
    key = lax.broadcasted_iota(jnp.int32, (tk, tq), 0)
    qry = lax.broadcasted_iota(jnp.int32, (tk, tq), 1)
    per_q = tq // tk
    causal = [u * tk + key <= qry for u in range(per_q)]
    nt = (((1,), (1,)), ((), ()))
    half = B_HEADS // 2
    q_refs, k_refs = (q0_ref, q1_ref), (k0_ref, k1_ref)
    outs = []
    for h0 in range(0, B_HEADS, heads_per_step):
        heads = list(range(h0, h0 + heads_per_step))
        hsl = [slice(h * HEAD_DIM, (h + 1) * HEAD_DIM) for h in heads]
        lsl = [slice((h % half) * HEAD_DIM, (h % half + 1) * HEAD_DIM) for h in heads]
        qs = [q_refs[h // half][0, :, ls] * (HEAD_DIM ** -0.5)
              for h, ls in zip(heads, lsl)]
        cqs = [jnp.concatenate([ctq_ref[0, u, h:h + 1, :] for u in range(sub)], axis=1) for h in heads]

        def step(j, carry, mask, heads=heads, hsl=hsl, lsl=lsl, qs=qs, cqs=cqs):
            start = pl.multiple_of(j * tk, tk)
            sts = [lax.dot_general(k_refs[h // half][0, pl.ds(start, tk), ls], q, nt,
                                   preferred_element_type=F32)
                   for h, ls, q in zip(heads, lsl, qs)]
            ps, stats = [], []
            for idx, h in enumerate(heads):
                m, l, _ = carry[3 * idx:3 * idx + 3]
                ck = c_ref[0, pl.ds(start, tk), h:h + 1]
                st = (sts[idx] - ck) + cqs[idx]
                if mask is not None:
                    st = jnp.where(mask, st, -jnp.inf)
                m_new = jnp.maximum(m, jnp.max(st, axis=0, keepdims=True))
                alpha = jnp.exp(m - m_new)
                p = jnp.exp(st - m_new)
                stats.append((m_new, alpha, alpha * l + jnp.sum(p, axis=0, keepdims=True)))
                ps.append(p.astype(BF16))
            new = []
            for idx in range(len(heads)):
                m_new, alpha, l = stats[idx]
                pv = jnp.dot(vt_ref[0, j, hsl[idx], :], ps[idx], preferred_element_type=F32)
                new += [m_new, l, alpha * carry[3 * idx + 2] + pv]
            return tuple(new)

        init = (jnp.full((1, tq), -jnp.inf, F32), jnp.zeros((1, tq), F32),
                jnp.zeros((HEAD_DIM, tq), F32)) * heads_per_step
        carry = lax.fori_loop(0, qi * per_q, functools.partial(step, mask=None), init)
        for u in range(per_q):
            carry = step(qi * per_q + u, carry, causal[u])
        for idx in range(heads_per_step):
            outs.append(carry[3 * idx + 2] / carry[3 * idx + 1])
    o_ref[0] = jnp.concatenate(outs, axis=0).T.astype(o_ref.dtype)


def _fox(proj, v_t, c, ct, *, tq=256, tk=256, heads_per_step=8):
    b, s, _ = proj.shape
    w = B_HEADS * HEAD_DIM
    nk = s // tk
    sub = tq // LANES
    hw = w // 2
    base = (A_Q_HEADS + 2 * A_KV_HEADS) * HEAD_DIM
    qb, kb = base // hw, (base + w) // hw
    assert v_t.shape == (b, nk, w, tk)
    return pl.pallas_call(
        functools.partial(_fox_kernel, tq=tq, tk=tk, heads_per_step=heads_per_step),
        grid=(b, s // tq),
        in_specs=[
            pl.BlockSpec((1, tq, hw), lambda i, n: (i, n, qb)),
            pl.BlockSpec((1, tq, hw), lambda i, n: (i, n, qb + 1)),
            pl.BlockSpec((1, s, hw), lambda i, n: (i, 0, kb)),
            pl.BlockSpec((1, s, hw), lambda i, n: (i, 0, kb + 1)),
            pl.BlockSpec((1, nk, w, tk), lambda i, n: (i, 0, 0, 0)),
            pl.BlockSpec((1, s, LANES), lambda i, n: (i, 0, 0)),
            pl.BlockSpec((1, sub, B_HEADS, LANES), lambda i, n: (i, n, 0, 0)),
        ],
        out_specs=pl.BlockSpec((1, tq, w), lambda i, n: (i, n, 0)),
        out_shape=jax.ShapeDtypeStruct((b, s, w), BF16),
        compiler_params=_cparams(("parallel", "parallel")),
        name="fox",
    )(proj, proj, proj, proj, v_t, c, ct)


SSD_BATCHES = 4


def _ssd_gla_kernel(*refs):
    n_data, n_par = 9, 10
    data, params = refs[:n_data], refs[n_data:n_data + n_par]
    o_ref, hs_ref, gs_ref = refs[n_data + n_par:]

    @pl.when(pl.program_id(1) == 0)
    def _():
        hs_ref[...] = jnp.zeros_like(hs_ref)
        gs_ref[...] = jnp.zeros_like(gs_ref)

    for bb in range(o_ref.shape[0]):
        one = lambda ref: ref.at[pl.ds(bb, 1)]
        _ssd_gla_chunk(*[one(ref) for ref in data], *params, one(o_ref), hs_ref.at[bb], gs_ref.at[bb])


def _ssd_gla_chunk(z_ref, q_ref, k_ref, v_ref, r_ref, xc_ref, xp_ref, sdt_ref, sg_ref,
                   cw_ref, cb_ref, dtb_ref, alog_ref, dsk_ref, ex_ref, sn_ref, gkw_ref, gkb_ref, gn_ref,
                   o_ref, hs_ref, gs_ref):
    c = pl.program_id(1)
    q_len = C_CHUNK
    halo = xp_ref.shape[1]

    prev = xp_ref[0]
    cur = xc_ref[0]
    ext = jnp.concatenate([jnp.where(c > 0, prev, jnp.zeros_like(prev)), cur], axis=0)
    t_out = lax.broadcasted_iota(jnp.int32, (q_len, halo + q_len), 0)
    t_in = lax.broadcasted_iota(jnp.int32, (q_len, halo + q_len), 1) - halo
    acc = cb_ref[...] + cw_ref[C_CONV - 1:C_CONV, :] * cur.astype(F32)
    for j in range(C_CONV - 1):
        shift = (t_in == t_out - (C_CONV - 1 - j)).astype(BF16)
        acc = acc + cw_ref[j:j + 1, :] * jnp.dot(shift, ext, preferred_element_type=F32)
    xbc = jax.nn.silu(acc)
    xs = xbc[:, :C_INNER]
    gs_w = C_GROUPS * C_STATE
    bm = xbc[:, C_INNER:C_INNER + gs_w].astype(BF16)
    cm = xbc[:, C_INNER + gs_w:].astype(BF16)

    row = lax.broadcasted_iota(jnp.int32, (q_len, q_len), 0)
    col = lax.broadcasted_iota(jnp.int32, (q_len, q_len), 1)
    tri = col <= row

    lane = lax.broadcasted_iota(jnp.int32, (1, LANES), 1)
    dt = jnp.where(lane < C_HEADS, jax.nn.softplus(sdt_ref[0] + dtb_ref[...]), 0.0)
    dta = dt * -jnp.exp(alog_ref[...])
    acs = _dot_mask_lhs(tri.astype(BF16), dta)
    acs_t = acs.T
    chunk_dec = jnp.exp(acs[q_len - 1:q_len, :])
    expand = ex_ref[...]
    dt_x = _dot_mask_rhs(dt, expand)
    acs_x = _dot_mask_rhs(acs, expand)
    xd = xs * dt_x
    xd_b = xd.astype(BF16)
    xdd = xd * jnp.exp(acs_x[q_len - 1:q_len, :] - acs_x)
    low_half = lax.broadcasted_iota(jnp.int32, (q_len, LANES), 1) < C_HEAD_DIM

    y_pairs, y_offs = [], []
    tdims = (((1,), (1,)), ((), ()))
    for g in range(C_GROUPS):
        b_g = bm[:, g * C_STATE:(g + 1) * C_STATE]
        c_g = cm[:, g * C_STATE:(g + 1) * C_STATE]
        cb = lax.dot_general(c_g, b_g, tdims, preferred_element_type=F32)
        h0 = g * C_HPG
        grp = slice(h0 * C_HEAD_DIM, (h0 + C_HPG) * C_HEAD_DIM)
        y_offs.append(lax.dot_general(c_g, hs_ref[grp, :].astype(BF16), tdims, preferred_element_type=F32))
        for h in range(h0, h0 + C_HPG, 2):
            xp = xd_b[:, h * C_HEAD_DIM:(h + 2) * C_HEAD_DIM]
            halves = []
            for hh in (h, h + 1):
                seg = jnp.exp(jnp.where(tri, acs[:, hh:hh + 1] - acs_t[hh:hh + 1, :], -jnp.inf))
                halves.append(jnp.dot((cb * seg).astype(BF16), xp, preferred_element_type=F32))
            y_pairs.append(jnp.where(low_half, halves[0], halves[1]))
        upd = jnp.dot(xdd[:, grp].T.astype(BF16), b_g, preferred_element_type=F32)
        for hh in range(C_HPG):
            h = h0 + hh
            ps = slice(h * C_HEAD_DIM, (h + 1) * C_HEAD_DIM)
            us = slice(hh * C_HEAD_DIM, (hh + 1) * C_HEAD_DIM)
            hs_ref[ps, :] = hs_ref[ps, :] * chunk_dec[0:1, h:h + 1] + upd[us, :]
    y = (jnp.concatenate(y_pairs, axis=1) + jnp.concatenate(y_offs, axis=1) * jnp.exp(acs_x)
         + dsk_ref[...] * xs)
    y = y * jax.nn.silu(z_ref[0].astype(F32))
    o_ref[0, :, :C_INNER] = _rms(y, sn_ref[...]).astype(o_ref.dtype)

    same = (row // D_CHUNK) == (col // D_CHUNK)
    tri2 = tri & same
    la = jnp.dot(sg_ref[0].astype(BF16), gkw_ref[...], preferred_element_type=F32) + gkb_ref[...]
    la = jax.nn.log_sigmoid(la) / D_GATE_NORM
    gcs = _dot_mask_lhs(tri2.astype(BF16), la)
    first = lax.broadcasted_iota(jnp.int32, (q_len, 1), 0) < D_CHUNK
    r_all = r_ref[0]
    for h in range(D_HEADS):
        ks = slice(h * D_HK, (h + 1) * D_HK)
        vs = slice(h * D_HV, (h + 1) * D_HV)
        g_h = gcs[:, ks]
        g_end0 = g_h[D_CHUNK - 1:D_CHUNK, :]
        g_end1 = g_h[q_len - 1:q_len, :]
        q_h = q_ref[0, :, ks].astype(F32) * (D_HK ** -0.5)
        k_h = k_ref[0, :, ks].astype(F32)
        v_h = v_ref[0, :, vs]
        q_dec = (q_h * jnp.exp(g_h)).astype(BF16)
        k_inv = (k_h * jnp.exp(-g_h)).astype(BF16)
        k_end = k_h * jnp.exp(jnp.where(first, g_end0, g_end1) - g_h)
        ke0 = jnp.where(first, k_end, 0.0).astype(BF16)
        ke1 = jnp.where(first, 0.0, k_end).astype(BF16)
        attn = lax.dot_general(q_dec, k_inv, (((1,), (1,)), ((), ())), preferred_element_type=F32)
        attn = jnp.where(tri2, attn, 0.0).astype(BF16)
        o = jnp.dot(attn, v_h, preferred_element_type=F32)
        v_t = v_h.astype(F32).T.astype(BF16)
        st_rows = slice(h * D_HV, (h + 1) * D_HV)
        s0 = gs_ref[st_rows, :]
        s1 = s0 * jnp.exp(g_end0) + jnp.dot(v_t, ke0, preferred_element_type=F32)
        s2 = s1 * jnp.exp(g_end1) + jnp.dot(v_t, ke1, preferred_element_type=F32)
        gs_ref[st_rows, :] = s2
        tdims = (((1,), (1,)), ((), ()))
        o0 = lax.dot_general(q_dec, s0.astype(BF16), tdims, preferred_element_type=F32)
        o1 = lax.dot_general(q_dec, s1.astype(BF16), tdims, preferred_element_type=F32)
        o = o + jnp.where(first, o0, o1)
        o = _rms(o, gn_ref[...]) * jax.nn.silu(r_all[:, vs].astype(F32))
        o_ref[0, :, C_INNER + h * D_HV:C_INNER + (h + 1) * D_HV] = o.astype(o_ref.dtype)


def _ssd_gla(z, q, k, v, r, xbc, side_dt, side_g, g_lane, p):
    b, s, _ = z.shape
    q_len = C_CHUNK
    halo = 16
    nb = SSD_BATCHES if b % SSD_BATCHES == 0 else 1
    chunk = lambda width: pl.BlockSpec((nb, q_len, width), lambda i, n: (i, n, 0))
    full = lambda shape: pl.BlockSpec(shape, lambda i, n: (0,) * len(shape))
    pad_lanes = lambda vec: jnp.zeros((1, LANES), F32).at[0, :vec.shape[0]].set(vec.astype(F32))
    gkw = jnp.zeros((LANES, D_KEY), F32).at[g_lane:g_lane + D_GATE_RANK].set(p["gk_w"]).astype(BF16)
    expand = jnp.asarray(np.arange(C_INNER)[None, :] // C_HEAD_DIM == np.arange(LANES)[:, None], BF16)
    return pl.pallas_call(
        _ssd_gla_kernel,
        grid=(b // nb, s // q_len),
        in_specs=[
            chunk(C_INNER), chunk(D_KEY), chunk(D_KEY), chunk(D_VAL), chunk(D_VAL), chunk(C_CONV_DIM),
            pl.BlockSpec((nb, halo, C_CONV_DIM), lambda i, n: (i, jnp.maximum(n * (q_len // halo) - 1, 0), 0)),
            chunk(LANES), chunk(LANES),
            full((C_CONV, C_CONV_DIM)), full((1, C_CONV_DIM)),
            full((1, LANES)), full((1, LANES)), full((1, C_INNER)), full((LANES, C_INNER)),
            full((1, C_INNER)), full((LANES, D_KEY)), full((1, D_KEY)), full((1, D_HV)),
        ],
        out_specs=pl.BlockSpec((nb, q_len, C_INNER + D_VAL), lambda i, n: (i, n, 0)),
        out_shape=jax.ShapeDtypeStruct((b, s, C_INNER + D_VAL), BF16),
        scratch_shapes=[pltpu.VMEM((nb, C_INNER, C_STATE), F32),
                        pltpu.VMEM((nb, D_VAL, D_HK), F32)],
        compiler_params=_cparams(("parallel", "arbitrary")),
        name="ssd_gla",
    )(z, q, k, v, r, xbc, xbc, side_dt, side_g,
      p["conv_w"].astype(F32), p["conv_b"].reshape(1, -1).astype(F32),
      pad_lanes(p["dt_bias"]), pad_lanes(p["a_log"]),
      jnp.repeat(p["d_skip"].astype(F32), C_HEAD_DIM).reshape(1, C_INNER), expand,
      p["ssd_norm"].reshape(1, -1).astype(F32), gkw, p["gk_b"].reshape(1, -1).astype(F32),
      p["gla_norm"].reshape(1, -1).astype(F32))


def _store_rows_tiled(ref, val):
    m, d = val.shape
    nc = d // LANES
    for c in range(nc):
        ref[pl.ds(c, m, stride=nc), :] = val[:, c * LANES:(c + 1) * LANES]


def _load_rows_tiled(ref, m, dtype=None):
    nc = ref.shape[0] // m
    parts = [ref[pl.ds(c, m, stride=nc), :] for c in range(nc)]
    if dtype is not None:
        parts = [p.astype(dtype) for p in parts]
    return jnp.concatenate(parts, axis=1)


def _pack_pairs(x):
    n = x.shape[1] // 2
    u = pltpu.bitcast(x.astype(BF16).astype(F32), jnp.uint32)
    return (u[:, :n] >> 16) | (u[:, n:] & jnp.uint32(0xFFFF0000))


def _unpack_pairs(u):
    lo = pltpu.bitcast(u << 16, F32).astype(BF16)
    hi = pltpu.bitcast(u & jnp.uint32(0xFFFF0000), F32).astype(BF16)
    return jnp.concatenate([lo, hi], axis=1)


RT_GATE, RT_EXPERT, RT_RANK = 0, 2, 4


def _route_block(lg, carry, earlier):
    m = lg.shape[0]
    lane = lax.broadcasted_iota(jnp.int32, (m, LANES), 1)
    lane_f = lane.astype(F32)
    none = float(LANES)
    neg = -jnp.inf
    first_max = lambda v, vmax: jnp.min(jnp.where(v == vmax, lane_f, none), axis=-1, keepdims=True)
    gl = jnp.where(lane < MOE_GROUPS, lg, neg)
    gmax = jnp.max(gl, axis=-1, keepdims=True)
    g_w = 1.0 / jnp.sum(jnp.exp(gl - gmax), axis=-1, keepdims=True)
    lo = MOE_GROUPS + first_max(gl, gmax) * MOE_EPG
    el = jnp.where((lane_f >= lo) & (lane_f < lo + MOE_EPG), lg, neg)
    emax = jnp.max(el, axis=-1, keepdims=True)
    esum = jnp.sum(jnp.exp(el - emax), axis=-1, keepdims=True)
    l0 = first_max(el, emax)
    el2 = jnp.where(lane_f == l0, neg, el)
    emax2 = jnp.max(el2, axis=-1, keepdims=True)
    l1 = first_max(el2, emax2)
    p0 = 1.0 / esum
    p1 = jnp.exp(emax2 - emax) / esum
    w0 = g_w * (p0 / (p0 + p1))
    w1 = g_w * (p1 / (p0 + p1))
    oh0 = lane_f == l0
    oh1 = lane_f == l1
    oh = (oh0 | oh1).astype(BF16)
    cum = jnp.dot(earlier, oh, preferred_element_type=F32) + carry
    rank0 = jnp.sum(jnp.where(oh0, cum, 0.0), axis=-1, keepdims=True)
    rank1 = jnp.sum(jnp.where(oh1, cum, 0.0), axis=-1, keepdims=True)
    carry = carry + jnp.sum(oh.astype(F32), axis=0, keepdims=True)
    rec = jnp.zeros((m, LANES), F32)
    for pos, val in ((RT_GATE, w0), (RT_GATE + 1, w1), (RT_EXPERT, l0 - MOE_GROUPS),
                     (RT_EXPERT + 1, l1 - MOE_GROUPS), (RT_RANK, rank0), (RT_RANK + 1, rank1)):
        rec = jnp.where(lane == pos, val, rec)
    return rec, carry


def _out_proj_kernel(n_parts, *refs):
    a_refs = refs[:n_parts]
    w_refs = refs[n_parts:2 * n_parts]
    h_ref, g_ref, wr_ref, br_ref, tri_ref, ho_ref, xt_ref, rt_ref, cnt_ref, carry_ref = refs[2 * n_parts:]

    @pl.when(pl.program_id(0) == 0)
    def _():
        carry_ref[...] = jnp.zeros_like(carry_ref)

    acc = h_ref[...]
    for a_ref, w_ref in zip(a_refs, w_refs):
        acc = acc + jnp.dot(a_ref[...], w_ref[...], preferred_element_type=F32)
    ho_ref[...] = acc
    xn = _rms(acc, g_ref[...])
    xt_ref[...] = xn.astype(xt_ref.dtype)
    x_hi, x_mid, _ = _split3(xn)
    wr = wr_ref[...]
    lg2 = jnp.dot(x_hi, wr, preferred_element_type=F32)
    lg = (lg2[:, :LANES] + lg2[:, LANES:] + jnp.dot(x_mid, wr[:, :LANES], preferred_element_type=F32)
          + br_ref[...])
    rec, carry = _route_block(lg, carry_ref[...], tri_ref[...])
    rt_ref[...] = rec
    carry_ref[...] = carry
    cnt_ref[0] = carry


def _out_proj(parts, w_parts, h, g, w_route, b_route, *, tm=512):
    t, d = h.shape
    tm = min(tm, t)
    nc = d // LANES
    row = lambda i: (i, 0)
    const = lambda i: (0, 0)
    in_specs = [pl.BlockSpec((tm, a.shape[1]), row) for a in parts]
    in_specs += [pl.BlockSpec(w.shape, const) for w in w_parts]
    in_specs += [pl.BlockSpec((tm, d), row), pl.BlockSpec((1, d), const),
                 pl.BlockSpec((d, 2 * LANES), const), pl.BlockSpec((1, LANES), const),
                 pl.BlockSpec((tm, tm), const)]
    earlier = jnp.asarray(np.tril(np.ones((tm, tm), np.float32), -1), BF16)
    return pl.pallas_call(
        functools.partial(_out_proj_kernel, len(parts)),
        grid=(t // tm,),
        in_specs=in_specs,
        out_specs=[pl.BlockSpec((tm, d), row), pl.BlockSpec((tm, d), row),
                   pl.BlockSpec((tm, LANES), row), pl.BlockSpec((1, 1, LANES), lambda i: (i, 0, 0))],
        out_shape=[jax.ShapeDtypeStruct((t, d), F32), jax.ShapeDtypeStruct((t, d), BF16),
                   jax.ShapeDtypeStruct((t, LANES), F32), jax.ShapeDtypeStruct((t // tm, 1, LANES), F32)],
        scratch_shapes=[pltpu.VMEM((1, LANES), F32)],
        compiler_params=_cparams(("arbitrary",)),
        name="out_proj",
    )(*parts, *w_parts, h, g.reshape(1, d), w_route, b_route, earlier)


LONG_RUN = 128


def _piece_sizes(max_rows, lo=1, hi=None):
    sizes = [1 << b for b in reversed(range(max_rows.bit_length()))]
    return [n for n in sizes if n >= lo and (hi is None or n < hi)]


def _for_each_piece(run, sizes, body):
    off = run // (2 * sizes[0]) * (2 * sizes[0])
    for n in sizes:
        hit = (run & n) != 0
        pl.when(hit)(functools.partial(body, off, n))
        off = off + jnp.where(hit, n, 0)


def _for_each_run(len_ref, tile, long_ref, max_rows, make_body):
    def sweep(sizes):
        def per_expert(e, first):
            run = len_ref[tile * MOE_EXPERTS + e]
            _for_each_piece(run, sizes, make_body(e, first))
            return first + run

        lax.fori_loop(0, MOE_EXPERTS, per_expert, jnp.int32(0))

    sweep(_piece_sizes(max_rows, hi=LONG_RUN))
    pl.when(long_ref[tile] != 0)(functools.partial(sweep, _piece_sizes(max_rows, lo=LONG_RUN)))


def _zero_fill_rows(rows_ref, z_ref, zsem, lo_ref, hi_ref, n_rows, nc):
    zb = z_ref.shape[0] // nc
    z_ref[...] = jnp.zeros_like(z_ref)
    assert MOE_TILE // 2 <= zb and MOE_TILE % zb == 0

    def piece(row0, n):
        return pltpu.make_async_copy(z_ref.at[pl.ds(0, n * nc)], rows_ref.at[pl.ds(row0 * nc, n * nc)], zsem)

    def sweep(issue):
        def per_expert(e, carry):
            lo = lo_ref[e]

            def one(off, n):
                piece(lo + off, n).start() if issue else piece(lo + off, n).wait()

            _for_each_piece(hi_ref[e] - lo, _piece_sizes(MOE_TILE // 2), one)
            return carry

        def per_block(i, carry):
            piece(i * zb, zb).start() if issue else piece(i * zb, zb).wait()
            return carry

        lax.fori_loop(0, MOE_EXPERTS, per_expert, 0)
        lax.fori_loop(hi_ref[MOE_EXPERTS - 1] // zb, n_rows // zb, per_block, 0)

    sweep(True)
    sweep(False)


def _tile_positions(rt, pos_base):
    lane_f = lax.broadcasted_iota(jnp.int32, rt.shape, 1).astype(F32)
    out = []
    for k in range(MOE_TOPK):
        e_lane = rt[:, RT_EXPERT + k:RT_EXPERT + k + 1] + MOE_GROUPS
        out.append(rt[:, RT_RANK + k:RT_RANK + k + 1]
                   + jnp.sum(jnp.where(lane_f == e_lane, pos_base, 0.0), axis=-1, keepdims=True))
    return out


def _moe_dispatch_kernel(dst_ref, len_ref, long_ref, lo_ref, hi_ref, x_ref, rt_ref, pb_ref, rows_ref,
                         sbuf, z_ref, sem, zsem, *, tm, nc, n_steps, n_rows):
    i = pl.program_id(0)
    rows = MOE_TOPK * tm
    slot = lax.rem(i, 2)

    def wait_slot(s):
        pltpu.make_async_copy(sbuf.at[s], rows_ref.at[pl.ds(0, rows * nc)], sem.at[s]).wait()

    @pl.when(i == 0)
    def _():
        _zero_fill_rows(rows_ref, z_ref, zsem, lo_ref, hi_ref, n_rows, nc)

    @pl.when(i >= 2)
    def _():
        wait_slot(slot)

    pos = _tile_positions(rt_ref[...], pb_ref[0])
    lane = lax.broadcasted_iota(jnp.int32, (tm, LANES), 1)
    pos_t = jnp.where(lane == 0, pos[0], jnp.where(lane == 1, pos[1], 0.0)).T
    p_iota = lax.broadcasted_iota(jnp.int32, (rows, tm), 0).astype(F32)
    place = ((p_iota == pos_t[0:1, :]) | (p_iota == pos_t[1:2, :])).astype(BF16)
    _store_rows_tiled(sbuf.at[slot], _pack_pairs(jnp.dot(place, x_ref[...], preferred_element_type=F32)))

    def sender(e, src):
        dst = dst_ref[i * MOE_EXPERTS + e]

        def send(off, n):
            pltpu.make_async_copy(sbuf.at[slot, pl.ds((src + off) * nc, n * nc)],
                                  rows_ref.at[pl.ds((dst + off) * nc, n * nc)], sem.at[slot]).start()

        return send

    _for_each_run(len_ref, i, long_ref, rows, sender)

    @pl.when(i == n_steps - 1)
    def _():
        wait_slot(slot)
        if n_steps > 1:
            wait_slot(1 - slot)


def _moe_dispatch(x, route, pos_base, seg_dst, seg_len, seg_long, pad_lo, pad_hi, n_rows, *, tm):
    t, d = x.shape
    nc = d // (2 * LANES)
    n_steps = t // tm
    row = lambda i, *_: (i, 0)
    return pl.pallas_call(
        functools.partial(_moe_dispatch_kernel, tm=tm, nc=nc, n_steps=n_steps, n_rows=n_rows),
        grid_spec=pltpu.PrefetchScalarGridSpec(
            num_scalar_prefetch=5, grid=(n_steps,),
            in_specs=[pl.BlockSpec((tm, d), row), pl.BlockSpec((tm, LANES), row),
                      pl.BlockSpec((1, 1, LANES), lambda i, *_: (i, 0, 0))],
            out_specs=pl.BlockSpec(memory_space=pl.ANY),
            scratch_shapes=[pltpu.VMEM((2, MOE_TOPK * tm * nc, LANES), jnp.uint32),
                            pltpu.VMEM((MOE_TILE // 2 * nc, LANES), jnp.uint32),
                            pltpu.SemaphoreType.DMA((2,)), pltpu.SemaphoreType.DMA(())]),
        out_shape=jax.ShapeDtypeStruct((n_rows * nc, LANES), jnp.uint32),
        compiler_params=pltpu.CompilerParams(dimension_semantics=("arbitrary",), has_side_effects=True,
                                             vmem_limit_bytes=VMEM_LIMIT),
        name="moe_dispatch",
    )(seg_dst, seg_len, seg_long, pad_lo, pad_hi, x, route, pos_base)


def _moe_kernel(te_ref, nu_ref, x_ref, wg_ref, wu_ref, wd_ref, y_ref):
    i = pl.program_id(0)

    @pl.when(i >= nu_ref[0])
    def _():
        y_ref[...] = jnp.zeros_like(y_ref)

    @pl.when(i < nu_ref[0])
    def _():
        x = _unpack_pairs(_load_rows_tiled(x_ref, MOE_TILE))
        gate = jnp.dot(x, wg_ref[...].astype(BF16), preferred_element_type=F32)
        up = jnp.dot(x, wu_ref[...].astype(BF16), preferred_element_type=F32)
        act = (jax.nn.silu(gate) * up).astype(BF16)
        y = jnp.dot(act, wd_ref[...].astype(BF16), preferred_element_type=F32)
        _store_rows_tiled(y_ref, _pack_pairs(y))


def _moe_experts(x_rows, tile_expert, n_used, w_gate, w_up, w_down, layer):
    d, ff = w_gate.shape[-2:]
    nc = d // (2 * LANES)
    n_tiles = x_rows.shape[0] // (MOE_TILE * nc)
    live = lambda i, nu: jnp.minimum(i, nu[0] - 1)
    w_spec = lambda a, b: pl.BlockSpec((None, None, a, b), lambda i, te, nu: (layer, te[live(i, nu)], 0, 0))
    return pl.pallas_call(
        _moe_kernel,
        grid_spec=pltpu.PrefetchScalarGridSpec(
            num_scalar_prefetch=2, grid=(n_tiles,),
            in_specs=[pl.BlockSpec((MOE_TILE * nc, LANES), lambda i, te, nu: (live(i, nu), 0)),
                      w_spec(d, ff), w_spec(d, ff), w_spec(ff, d)],
            out_specs=pl.BlockSpec((MOE_TILE * nc, LANES), lambda i, te, nu: (i, 0))),
        out_shape=jax.ShapeDtypeStruct(x_rows.shape, x_rows.dtype),
        compiler_params=_cparams(("arbitrary",)),
        name="moe_experts",
    )(tile_expert, n_used, x_rows, w_gate, w_up, w_down)


def _moe_combine_kernel(src_ref, len_ref, long_ref, h_ref, rt_ref, pb_ref, g_ref, y_hbm, o_ref, ybuf, sem,
                        *, tm, nc, final):
    i = pl.program_id(0)
    n_steps = pl.num_programs(0)
    rows = MOE_TOPK * tm

    def fetch(tile, slot):
        def receiver(e, dst):
            src = src_ref[tile * MOE_EXPERTS + e]

            def recv(off, n):
                pltpu.make_async_copy(y_hbm.at[pl.ds((src + off) * nc, n * nc)],
                                      ybuf.at[slot, pl.ds((dst + off) * nc, n * nc)], sem.at[slot]).start()

            return recv

        _for_each_run(len_ref, tile, long_ref, rows, receiver)

    slot = lax.rem(i, 2)

    @pl.when(i == 0)
    def _():
        fetch(0, 0)

    @pl.when(i + 1 < n_steps)
    def _():
        fetch(i + 1, 1 - slot)

    pltpu.make_async_copy(y_hbm.at[pl.ds(0, rows * nc)], ybuf.at[slot], sem.at[slot]).wait()
    y = _unpack_pairs(_load_rows_tiled(ybuf.at[slot], rows))
    rt = rt_ref[...]
    pos_f = lax.broadcasted_iota(jnp.int32, (tm, rows), 1).astype(F32)
    pick = jnp.zeros((tm, rows), F32)
    for k, pos in enumerate(_tile_positions(rt, pb_ref[0])):
        pick = jnp.where(pos_f == pos, rt[:, RT_GATE + k:RT_GATE + k + 1], pick)
    out = h_ref[...] + jnp.dot(pick.astype(BF16), y, preferred_element_type=F32)
    o_ref[...] = _rms(out, g_ref[...]) if final else out


def _moe_combine_rows(h, y_rows, route, seg_src, seg_len, seg_long, pos_base, g, *, tm, final):
    t, d = h.shape
    nc = d // (2 * LANES)
    row = lambda i, *_: (i, 0)
    return pl.pallas_call(
        functools.partial(_moe_combine_kernel, tm=tm, nc=nc, final=final),
        grid_spec=pltpu.PrefetchScalarGridSpec(
            num_scalar_prefetch=3, grid=(t // tm,),
            in_specs=[pl.BlockSpec((tm, d), row), pl.BlockSpec((tm, LANES), row),
                      pl.BlockSpec((1, 1, LANES), lambda i, *_: (i, 0, 0)),
                      pl.BlockSpec((1, d), lambda i, *_: (0, 0)),
                      pl.BlockSpec(memory_space=pl.ANY)],
            out_specs=pl.BlockSpec((tm, d), row),
            scratch_shapes=[pltpu.VMEM((2, MOE_TOPK * tm * nc, LANES), y_rows.dtype),
                            pltpu.SemaphoreType.DMA((2,))]),
        out_shape=jax.ShapeDtypeStruct((t, d), F32),
        compiler_params=_cparams(("arbitrary",)),
        name="moe_combine",
    )(seg_src, seg_len, seg_long, h, route, pos_base, g.reshape(1, d), y_rows)


def _moe(h, x, route, tile_counts, w_gate, w_up, w_down, layer, g, *, final):
    t = route.shape[0]
    n_tt = tile_counts.shape[0]
    tm = t // n_tt
    after = tile_counts[:, 0, MOE_GROUPS:MOE_GROUPS + MOE_EXPERTS].astype(jnp.int32)
    before = jnp.concatenate([jnp.zeros((1, MOE_EXPERTS), jnp.int32), after[:-1]], axis=0)
    cnt = after[-1]
    padded = (cnt + MOE_TILE - 1) // MOE_TILE * MOE_TILE
    pad_ends = jnp.cumsum(padded)
    starts = (pad_ends - padded).astype(jnp.int32)
    n_tiles = (t * MOE_TOPK + MOE_EXPERTS * (MOE_TILE - 1)) // MOE_TILE
    tile_start = jnp.arange(n_tiles, dtype=jnp.int32) * MOE_TILE
    tile_expert = jnp.minimum(jnp.sum(tile_start[:, None] >= pad_ends[None, :], axis=1),
                              MOE_EXPERTS - 1).astype(jnp.int32)
    n_used = (pad_ends[-1] // MOE_TILE).astype(jnp.int32).reshape(1)
    n_rows = n_tiles * MOE_TILE
    seg_len = (after - before).reshape(-1)
    seg_off = jnp.cumsum(after - before, axis=1) - (after - before)
    seg_row = (starts[None, :] + before).reshape(-1)
    pos_base = jnp.zeros((n_tt, 1, LANES), F32).at[:, 0, MOE_GROUPS:MOE_GROUPS + MOE_EXPERTS].set(
        (seg_off - before).astype(F32))
    seg_long = jnp.any(after - before >= LONG_RUN, axis=1).astype(jnp.int32)
    x_rows = _moe_dispatch(x, route, pos_base, seg_row, seg_len, seg_long, starts + cnt,
                           pad_ends.astype(jnp.int32), n_rows, tm=tm)
    y_rows = _moe_experts(x_rows, tile_expert, n_used, w_gate, w_up, w_down, layer)
    return _moe_combine_rows(h, y_rows, route, seg_row, seg_len, seg_long, pos_base, g, tm=tm, final=final)


def _router_weights(w_group, b_group, w_router, b_router):
    d = w_group.shape[0]
    w = jnp.zeros((d, LANES), F32)
    w = w.at[:, :MOE_GROUPS].set(w_group).at[:, MOE_GROUPS:MOE_GROUPS + MOE_EXPERTS].set(w_router)
    b = jnp.zeros((1, LANES), F32)
    b = b.at[0, :MOE_GROUPS].set(b_group).at[0, MOE_GROUPS:MOE_GROUPS + MOE_EXPERTS].set(b_router)
    w_hi = w.astype(BF16)
    w_mid = (w - w_hi.astype(F32)).astype(BF16)
    return jnp.concatenate([w_hi, w_mid], axis=1), b


def kernel(x, norm_mix, norm_moe, norm_final, even_w_in, even_sinks, even_forget_bias, even_w_out,
           odd_w_in, odd_conv_w, odd_conv_b, odd_dt_bias, odd_a_log, odd_d_skip, odd_ssd_norm,
           odd_gk_w, odd_gk_b, odd_gla_norm, odd_w_out, moe_w_group, moe_b_group, moe_w_router,
           moe_b_router, moe_w_gate, moe_w_up, moe_w_down):
    b, s, d = x.shape
    t = b * s
    depth = norm_mix.shape[0]
    h = x.reshape(t, d)
    for layer in range(depth):
        i = layer // 2
        if layer % 2 == 0:
            w = even_w_in.astype(BF16)[i]
            n_ab = (A_Q_HEADS + 2 * A_KV_HEADS + 3 * B_HEADS) * HEAD_DIM
            w_main = w[:, :n_ab]
            w_aux = jnp.pad(w[:, n_ab:], ((0, 0), (0, LANES - B_HEADS)))
            n_a, kv_a, n_b = (A_Q_HEADS + 2 * A_KV_HEADS) * HEAD_DIM, A_KV_HEADS * HEAD_DIM, B_HEADS * HEAD_DIM
            swa_rows, fox_rows = SWA_QBLOCKS * A_WINDOW, 256
            proj, f_aux, va_t, vb_t = _norm_proj(
                h, norm_mix[layer], [w_main, w_aux], ((0, 0, n_ab, BF16), (1, 0, LANES, F32)), tm=1024,
                t_plan=((0, n_a - kv_a, kv_a, BF16, swa_rows), (0, n_ab - n_b, n_b, BF16, fox_rows)))
            proj = proj.reshape(b, s, -1)
            out_a = _swa(proj, va_t.reshape(b, s // swa_rows, kv_a, swa_rows), even_sinks[i])
            c, ct = _fox_gate(f_aux.reshape(b, s, LANES), even_forget_bias[i])
            out_b = _fox(proj, vb_t.reshape(b, s // fox_rows, n_b, fox_rows), c, ct, tk=fox_rows)
            n_ha = A_Q_HEADS * HEAD_DIM
            w_out = even_w_out[i].astype(BF16)
            parts = [out_a.reshape(t, -1), out_b.reshape(t, -1)]
            w_parts = [w_out[:n_ha], w_out[n_ha:]]
        else:
            w = odd_w_in.astype(BF16)[i]
            o_z, o_xbc = 0, C_INNER
            o_dt = o_xbc + C_CONV_DIM
            o_q = o_dt + C_HEADS
            o_k = o_q + D_KEY
            o_v = o_k + D_KEY
            o_g = o_v + D_VAL
            o_r = o_g + D_GATE_RANK
            n_b = w.shape[1] - o_dt
            w_a = w[:, :o_dt]
            w_b = jnp.pad(w[:, o_dt:], ((0, 0), (0, -n_b % LANES)))
            g_win = (o_g - o_dt) // LANES * LANES
            plan = ((0, o_z, C_INNER, BF16), (1, o_q - o_dt, D_KEY, BF16), (1, o_k - o_dt, D_KEY, BF16),
                    (1, o_v - o_dt, D_VAL, BF16), (1, o_r - o_dt, D_VAL, BF16), (0, o_xbc, C_CONV_DIM, BF16),
                    (1, 0, LANES, F32), (1, g_win, LANES, F32))
            outs = _norm_proj(h, norm_mix[layer], [w_a, w_b], plan, tm=512)
            params = dict(conv_w=odd_conv_w[i], conv_b=odd_conv_b[i], dt_bias=odd_dt_bias[i], a_log=odd_a_log[i],
                          d_skip=odd_d_skip[i], ssd_norm=odd_ssd_norm[i], gk_w=odd_gk_w[i], gk_b=odd_gk_b[i],
                          gla_norm=odd_gla_norm[i])
            mixed = _ssd_gla(*[o.reshape(b, s, -1) for o in outs], o_g - o_dt - g_win, params)
            parts = [mixed.reshape(t, -1)]
            w_parts = [odd_w_out[i].astype(BF16)]
        w_route, b_route = _router_weights(moe_w_group[layer], moe_b_group[layer],
                                           moe_w_router[layer], moe_b_router[layer])
        h, x_tiled, route, tile_counts = _out_proj(parts, w_parts, h, norm_moe[layer], w_route, b_route)
        h = _moe(h, x_tiled, route, tile_counts, moe_w_gate, moe_w_up, moe_w_down, layer, norm_final,
                 final=layer == depth - 1)
    out = h
    return out.reshape(b, s, d)
```

```python
import functools

import numpy as np
import jax
import jax.numpy as jnp
from jax import lax
from jax.experimental import pallas as pl
from jax.experimental.pallas import tpu as pltpu

F32 = jnp.float32
BF16 = jnp.bfloat16

RMS_EPS = 1e-6
HEAD_DIM = 64
A_Q_HEADS = 8
A_KV_HEADS = 2
A_GROUP = A_Q_HEADS // A_KV_HEADS
A_WINDOW = 128
B_HEADS = 8
C_HEADS = 16
C_HEAD_DIM = 64
C_INNER = C_HEADS * C_HEAD_DIM
C_GROUPS = 2
C_HPG = C_HEADS // C_GROUPS
C_STATE = 128
C_CONV = 4
C_CHUNK = 128
C_CONV_DIM = C_INNER + 2 * C_GROUPS * C_STATE
D_HEADS = 4
D_HK = 128
D_HV = 256
D_KEY = D_HEADS * D_HK
D_VAL = D_HEADS * D_HV
D_GATE_RANK = 16
D_GATE_NORM = 16.0
D_CHUNK = 64
MOE_GROUPS = 4
MOE_EPG = 8
MOE_EXPERTS = MOE_GROUPS * MOE_EPG
MOE_TOPK = 2

LANES = 128
VMEM_LIMIT = 48 * 1024 * 1024
MOE_TILE = 512


def _cparams(sem):
    return pltpu.CompilerParams(dimension_semantics=sem, vmem_limit_bytes=VMEM_LIMIT)


def _rms(x, g):
    ms = jnp.mean(x * x, axis=-1, keepdims=True)
    return x * lax.rsqrt(ms + RMS_EPS) * g


def _norm_proj_kernel(n_w, plan, t_plan, *refs):
    x_ref, g_ref = refs[:2]
    w_refs = refs[2:2 + n_w]
    o_refs = refs[2 + n_w:2 + n_w + len(plan)]
    t_refs = refs[2 + n_w + len(plan):2 + n_w + len(plan) + len(t_plan)]
    res_refs = refs[2 + n_w + len(plan) + len(t_plan):]
    xn = _rms(x_ref[...], g_ref[...]).astype(BF16)
    for w_ref, res_ref in zip(w_refs, res_refs):
        res_ref[...] = jnp.dot(xn, w_ref[...], preferred_element_type=F32)
    for o_ref, (wi, start, width, _) in zip(o_refs, plan):
        o_ref[...] = res_refs[wi][:, start:start + width].astype(o_ref.dtype)
    for o_ref, (wi, start, width, _, rows) in zip(t_refs, t_plan):
        for u in range(o_ref.shape[0]):
            o_ref[u] = res_refs[wi][u * rows:(u + 1) * rows, start:start + width].T.astype(o_ref.dtype)


def _norm_proj(x, g, weights, plan, *, tm, t_plan=()):
    t, d = x.shape
    tm = min(tm, t)
    row = lambda i: (i, 0)
    const = lambda i: (0, 0)
    in_specs = [pl.BlockSpec((tm, d), row), pl.BlockSpec((1, d), const)]
    in_specs += [pl.BlockSpec(w.shape, const, pipeline_mode=pl.Buffered(1)) for w in weights]
    return pl.pallas_call(
        functools.partial(_norm_proj_kernel, len(weights), plan, t_plan),
        grid=(t // tm,),
        in_specs=in_specs,
        out_specs=([pl.BlockSpec((tm, width), row) for _, _, width, _ in plan]
                   + [pl.BlockSpec((tm // rows, width, rows), lambda i: (i, 0, 0))
                      for _, _, width, _, rows in t_plan]),
        out_shape=([jax.ShapeDtypeStruct((t, width), dtype) for _, _, width, dtype in plan]
                   + [jax.ShapeDtypeStruct((t // rows, width, rows), dtype) for _, _, width, dtype, rows in t_plan]),
        scratch_shapes=[pltpu.VMEM((tm, w.shape[1]), F32) for w in weights],
        compiler_params=_cparams(("parallel",)),
        name="norm_proj",
    )(x, g.reshape(1, d), *weights)


SWA_QBLOCKS = 8


def _swa_kernel(sink_ref, slope_ref, q_ref, kp_ref, kc_ref, vp_ref, vc_ref, o_ref):
    n = pl.program_id(1)
    blk = A_WINDOW
    wide = A_GROUP * blk
    key = lax.broadcasted_iota(jnp.int32, (2 * blk, wide), 0)
    qry = lax.broadcasted_iota(jnp.int32, (2 * blk, wide), 1) % blk
    dist = blk + qry - key
    in_window = (dist >= 0) & (dist < A_WINDOW)
    distf = dist.astype(F32)
    nt = (((1,), (1,)), ((), ()))
    k_all = jnp.concatenate([kp_ref[0], kc_ref[0]], axis=0)
    v_all = jnp.concatenate([vp_ref[0], vc_ref[0]], axis=1)
    units = [(j, kh) for j in range(SWA_QBLOCKS) for kh in range(A_KV_HEADS)]
    scores = []
    for j, kh in units:
        k = k_all[j * blk:(j + 2) * blk, kh * HEAD_DIM:(kh + 1) * HEAD_DIM]
        q = jnp.concatenate([q_ref[0, j * blk:(j + 1) * blk,
                                   (kh * A_GROUP + g) * HEAD_DIM:(kh * A_GROUP + g + 1) * HEAD_DIM]
                             for g in range(A_GROUP)], axis=0)
        scores.append(lax.dot_general(k, q, nt, preferred_element_type=F32))
    probs = []
    for (j, kh), s in zip(units, scores):
        valid = in_window & ((key >= blk) | (n * SWA_QBLOCKS + j > 0))
        s = s * (HEAD_DIM ** -0.5) - slope_ref[kh:kh + 1, :] * distf
        s = jnp.where(valid, s, -jnp.inf)
        sink = sink_ref[kh:kh + 1, :]
        m = jnp.maximum(jnp.max(s, axis=0, keepdims=True), sink)
        p = jnp.exp(s - m)
        probs.append((p.astype(BF16), jnp.sum(p, axis=0, keepdims=True) + jnp.exp(sink - m)))
    for j in range(SWA_QBLOCKS):
        outs = []
        for kh in range(A_KV_HEADS):
            p, denom = probs[j * A_KV_HEADS + kh]
            v_t = v_all[kh * HEAD_DIM:(kh + 1) * HEAD_DIM, j * blk:(j + 2) * blk]
            o_t = jnp.dot(v_t, p, preferred_element_type=F32) / denom
            outs += [o_t[:, g * blk:(g + 1) * blk] for g in range(A_GROUP)]
        o_ref[0, j * blk:(j + 1) * blk, :] = jnp.concatenate(outs, axis=0).T.astype(o_ref.dtype)


def _swa(proj, v_t, sinks):
    b, s, _ = proj.shape
    blk = A_WINDOW
    qw = A_Q_HEADS * HEAD_DIM
    kw = A_KV_HEADS * HEAD_DIM
    k_blk = qw // kw
    per_lane = lambda vec: jnp.repeat(vec.astype(F32), blk).reshape(A_KV_HEADS, A_GROUP * blk)
    slopes = jnp.asarray(2.0 ** (-8.0 * np.arange(1, A_Q_HEADS + 1) / A_Q_HEADS), F32)
    tq = SWA_QBLOCKS * blk
    assert v_t.shape == (b, s // tq, kw, tq)
    prev = lambda n: jnp.maximum(n * SWA_QBLOCKS - 1, 0)
    full = pl.BlockSpec((A_KV_HEADS, A_GROUP * blk), lambda i, n: (0, 0))
    return pl.pallas_call(
        _swa_kernel,
        grid=(b, s // tq),
        in_specs=[
            full, full,
            pl.BlockSpec((1, tq, qw), lambda i, n: (i, n, 0)),
            pl.BlockSpec((1, blk, kw), lambda i, n: (i, prev(n), k_blk)),
            pl.BlockSpec((1, tq, kw), lambda i, n: (i, n, k_blk)),
            pl.BlockSpec((1, None, kw, blk), lambda i, n: (i, jnp.maximum(n - 1, 0), 0, SWA_QBLOCKS - 1)),
            pl.BlockSpec((1, None, kw, tq), lambda i, n: (i, n, 0, 0)),
        ],
        out_specs=pl.BlockSpec((1, tq, qw), lambda i, n: (i, n, 0)),
        out_shape=jax.ShapeDtypeStruct((b, s, qw), BF16),
        compiler_params=_cparams(("parallel", "parallel")),
        name="swa",
    )(per_lane(sinks), per_lane(slopes), proj, proj, proj, v_t, v_t)


def _tril(n, dtype=F32):
    r = lax.broadcasted_iota(jnp.int32, (n, n), 0)
    c = lax.broadcasted_iota(jnp.int32, (n, n), 1)
    return (c <= r).astype(dtype)


def _split3(x):
    hi = x.astype(BF16)
    r = x - hi.astype(F32)
    mid = r.astype(BF16)
    return hi, mid, (r - mid.astype(F32)).astype(BF16)


def _dot_mask_lhs(mask, x):
    return sum(jnp.dot(mask, part, preferred_element_type=F32) for part in _split3(x))


def _dot_mask_rhs(x, mask):
    return sum(jnp.dot(part, mask, preferred_element_type=F32) for part in _split3(x))


def _fox_gate_kernel(f_ref, b_ref, c_ref, ct_ref):
    tri = _tril(LANES, BF16)
    carry = jnp.zeros((1, LANES), F32)
    for n in range(f_ref.shape[1] // LANES):
        rows = slice(n * LANES, (n + 1) * LANES)
        lf = jax.nn.log_sigmoid(f_ref[0, rows, :] + b_ref[...])
        cs = _dot_mask_lhs(tri, lf) + carry
        carry = cs[LANES - 1:LANES, :]
        c_ref[0, rows, :] = cs
        ct_ref[0, n] = cs.T[:B_HEADS, :]


def _fox_gate(f_aux, bias):
    b, s, _ = f_aux.shape
    nb = s // LANES
    bias_p = jnp.zeros((1, LANES), F32).at[0, :B_HEADS].set(bias.astype(F32))
    return pl.pallas_call(
        _fox_gate_kernel,
        grid=(b,),
        in_specs=[pl.BlockSpec((1, s, LANES), lambda i: (i, 0, 0)),
                  pl.BlockSpec((1, LANES), lambda i: (0, 0))],
        out_specs=[pl.BlockSpec((1, s, LANES), lambda i: (i, 0, 0)),
                   pl.BlockSpec((1, nb, B_HEADS, LANES), lambda i: (i, 0, 0, 0))],
        out_shape=[jax.ShapeDtypeStruct((b, s, LANES), F32),
                   jax.ShapeDtypeStruct((b, nb, B_HEADS, LANES), F32)],
        compiler_params=_cparams(("parallel",)),
        name="fox_gate",
    )(f_aux, bias_p)


def _fox_kernel(q0_ref, q1_ref, k0_ref, k1_ref, vt_ref, c_ref, ctq_ref, o_ref, *, tq, tk, heads_per_step):
    qi = pl.program_id(1)
    sub = tq // LANES
    key = lax.broadcasted_iota(jnp.int32, (tk, tq), 0)
    qry = lax.broadcasted_iota(jnp.int32, (tk, tq), 1)
    per_q = tq // tk
    causal = [u * tk + key <= qry for u in range(per_q)]
    nt = (((1,), (1,)), ((), ()))
    half = B_HEADS // 2
    q_refs, k_refs = (q0_ref, q1_ref), (k0_ref, k1_ref)
    outs = []
    for h0 in range(0, B_HEADS, heads_per_step):
        heads = list(range(h0, h0 + heads_per_step))
        hsl = [slice(h * HEAD_DIM, (h + 1) * HEAD_DIM) for h in heads]
        lsl = [slice((h % half) * HEAD_DIM, (h % half + 1) * HEAD_DIM) for h in heads]
        qs = [q_refs[h // half][0, :, ls] * (HEAD_DIM ** -0.5)
              for h, ls in zip(heads, lsl)]
        cqs = [jnp.concatenate([ctq_ref[0, u, h:h + 1, :] for u in range(sub)], axis=1) for h in heads]

        def step(j, carry, mask, heads=heads, hsl=hsl, lsl=lsl, qs=qs, cqs=cqs):
            start = pl.multiple_of(j * tk, tk)
            sts = [lax.dot_general(k_refs[h // half][0, pl.ds(start, tk), ls], q, nt,
                                   preferred_element_type=F32)
                   for h, ls, q in zip(heads, lsl, qs)]
            ps, stats = [], []
            for idx, h in enumerate(heads):
                m, l, _ = carry[3 * idx:3 * idx + 3]
                ck = c_ref[0, pl.ds(start, tk), h:h + 1]
                st = (sts[idx] - ck) + cqs[idx]
                if mask is not None:
                    st = jnp.where(mask, st, -jnp.inf)
                m_new = jnp.maximum(m, jnp.max(st, axis=0, keepdims=True))
                alpha = jnp.exp(m - m_new)
                p = jnp.exp(st - m_new)
                stats.append((m_new, alpha, alpha * l + jnp.sum(p, axis=0, keepdims=True)))
                ps.append(p.astype(BF16))
            new = []
            for idx in range(len(heads)):
                m_new, alpha, l = stats[idx]
                pv = jnp.dot(vt_ref[0, j, hsl[idx], :], ps[idx], preferred_element_type=F32)
                new += [m_new, l, alpha * carry[3 * idx + 2] + pv]
            return tuple(new)

        init = (jnp.full((1, tq), -jnp.inf, F32), jnp.zeros((1, tq), F32),
                jnp.zeros((HEAD_DIM, tq), F32)) * heads_per_step
        carry = lax.fori_loop(0, qi * per_q, functools.partial(step, mask=None), init)
        for u in range(per_q):
            carry = step(qi * per_q + u, carry, causal[u])
        for idx in range(heads_per_step):
            outs.append(carry[3 * idx + 2] / carry[3 * idx + 1])
    o_ref[0] = jnp.concatenate(outs, axis=0).T.astype(o_ref.dtype)


def _fox(proj, v_t, c, ct, *, tq=256, tk=256, heads_per_step=8):
    b, s, _ = proj.shape
    w = B_HEADS * HEAD_DIM
    nk = s // tk
    sub = tq // LANES
    hw = w // 2
    base = (A_Q_HEADS + 2 * A_KV_HEADS) * HEAD_DIM
    qb, kb = base // hw, (base + w) // hw
    assert v_t.shape == (b, nk, w, tk)
    return pl.pallas_call(
        functools.partial(_fox_kernel, tq=tq, tk=tk, heads_per_step=heads_per_step),
        grid=(b, s // tq),
        in_specs=[
            pl.BlockSpec((1, tq, hw), lambda i, n: (i, n, qb)),
            pl.BlockSpec((1, tq, hw), lambda i, n: (i, n, qb + 1)),
            pl.BlockSpec((1, s, hw), lambda i, n: (i, 0, kb)),
            pl.BlockSpec((1, s, hw), lambda i, n: (i, 0, kb + 1)),
            pl.BlockSpec((1, nk, w, tk), lambda i, n: (i, 0, 0, 0)),
            pl.BlockSpec((1, s, LANES), lambda i, n: (i, 0, 0)),
            pl.BlockSpec((1, sub, B_HEADS, LANES), lambda i, n: (i, n, 0, 0)),
        ],
        out_specs=pl.BlockSpec((1, tq, w), lambda i, n: (i, n, 0)),
        out_shape=jax.ShapeDtypeStruct((b, s, w), BF16),
        compiler_params=_cparams(("parallel", "parallel")),
        name="fox",
    )(proj, proj, proj, proj, v_t, c, ct)


SSD_BATCHES = 4


def _ssd_gla_kernel(*refs):
    n_data, n_par = 9, 10
    data, params = refs[:n_data], refs[n_data:n_data + n_par]
    o_ref, hs_ref, gs_ref = refs[n_data + n_par:]

    @pl.when(pl.program_id(1) == 0)
    def _():
        hs_ref[...] = jnp.zeros_like(hs_ref)
        gs_ref[...] = jnp.zeros_like(gs_ref)

    for bb in range(o_ref.shape[0]):
        one = lambda ref: ref.at[pl.ds(bb, 1)]
        _ssd_gla_chunk(*[one(ref) for ref in data], *params, one(o_ref), hs_ref.at[bb], gs_ref.at[bb])


def _ssd_gla_chunk(z_ref, q_ref, k_ref, v_ref, r_ref, xc_ref, xp_ref, sdt_ref, sg_ref,
                   cw_ref, cb_ref, dtb_ref, alog_ref, dsk_ref, ex_ref, sn_ref, gkw_ref, gkb_ref, gn_ref,
                   o_ref, hs_ref, gs_ref):
    c = pl.program_id(1)
    q_len = C_CHUNK
    halo = xp_ref.shape[1]

    prev = xp_ref[0]
    cur = xc_ref[0]
    ext = jnp.concatenate([jnp.where(c > 0, prev, jnp.zeros_like(prev)), cur], axis=0)
    t_out = lax.broadcasted_iota(jnp.int32, (q_len, halo + q_len), 0)
    t_in = lax.broadcasted_iota(jnp.int32, (q_len, halo + q_len), 1) - halo
    acc = cb_ref[...] + cw_ref[C_CONV - 1:C_CONV, :] * cur.astype(F32)
    for j in range(C_CONV - 1):
        shift = (t_in == t_out - (C_CONV - 1 - j)).astype(BF16)
        acc = acc + cw_ref[j:j + 1, :] * jnp.dot(shift, ext, preferred_element_type=F32)
    xbc = jax.nn.silu(acc)
    xs = xbc[:, :C_INNER]
    gs_w = C_GROUPS * C_STATE
    bm = xbc[:, C_INNER:C_INNER + gs_w].astype(BF16)
    cm = xbc[:, C_INNER + gs_w:].astype(BF16)

    row = lax.broadcasted_iota(jnp.int32, (q_len, q_len), 0)
    col = lax.broadcasted_iota(jnp.int32, (q_len, q_len), 1)
    tri = col <= row

    lane = lax.broadcasted_iota(jnp.int32, (1, LANES), 1)
    dt = jnp.where(lane < C_HEADS, jax.nn.softplus(sdt_ref[0] + dtb_ref[...]), 0.0)
    dta = dt * -jnp.exp(alog_ref[...])
    acs = _dot_mask_lhs(tri.astype(BF16), dta)
    acs_t = acs.T
    chunk_dec = jnp.exp(acs[q_len - 1:q_len, :])
    expand = ex_ref[...]
    dt_x = _dot_mask_rhs(dt, expand)
    acs_x = _dot_mask_rhs(acs, expand)
    xd = xs * dt_x
    xd_b = xd.astype(BF16)
    xdd = xd * jnp.exp(acs_x[q_len - 1:q_len, :] - acs_x)
    low_half = lax.broadcasted_iota(jnp.int32, (q_len, LANES), 1) < C_HEAD_DIM

    y_pairs, y_offs = [], []
    tdims = (((1,), (1,)), ((), ()))
    for g in range(C_GROUPS):
        b_g = bm[:, g * C_STATE:(g + 1) * C_STATE]
        c_g = cm[:, g * C_STATE:(g + 1) * C_STATE]
        cb = lax.dot_general(c_g, b_g, tdims, preferred_element_type=F32)
        h0 = g * C_HPG
        grp = slice(h0 * C_HEAD_DIM, (h0 + C_HPG) * C_HEAD_DIM)
        y_offs.append(lax.dot_general(c_g, hs_ref[grp, :].astype(BF16), tdims, preferred_element_type=F32))
        for h in range(h0, h0 + C_HPG, 2):
            xp = xd_b[:, h * C_HEAD_DIM:(h + 2) * C_HEAD_DIM]
            halves = []
            for hh in (h, h + 1):
                seg = jnp.exp(jnp.where(tri, acs[:, hh:hh + 1] - acs_t[hh:hh + 1, :], -jnp.inf))
                halves.append(jnp.dot((cb * seg).astype(BF16), xp, preferred_element_type=F32))
            y_pairs.append(jnp.where(low_half, halves[0], halves[1]))
        upd = jnp.dot(xdd[:, grp].T.astype(BF16), b_g, preferred_element_type=F32)
        for hh in range(C_HPG):
            h = h0 + hh
            ps = slice(h * C_HEAD_DIM, (h + 1) * C_HEAD_DIM)
            us = slice(hh * C_HEAD_DIM, (hh + 1) * C_HEAD_DIM)
            hs_ref[ps, :] = hs_ref[ps, :] * chunk_dec[0:1, h:h + 1] + upd[us, :]
    y = (jnp.concatenate(y_pairs, axis=1) + jnp.concatenate(y_offs, axis=1) * jnp.exp(acs_x)
         + dsk_ref[...] * xs)
    y = y * jax.nn.silu(z_ref[0].astype(F32))
    o_ref[0, :, :C_INNER] = _rms(y, sn_ref[...]).astype(o_ref.dtype)

    same = (row // D_CHUNK) == (col // D_CHUNK)
    tri2 = tri & same
    la = jnp.dot(sg_ref[0].astype(BF16), gkw_ref[...], preferred_element_type=F32) + gkb_ref[...]
    la = jax.nn.log_sigmoid(la) / D_GATE_NORM
    gcs = _dot_mask_lhs(tri2.astype(BF16), la)
    first = lax.broadcasted_iota(jnp.int32, (q_len, 1), 0) < D_CHUNK
    r_all = r_ref[0]
    for h in range(D_HEADS):
        ks = slice(h * D_HK, (h + 1) * D_HK)
        vs = slice(h * D_HV, (h + 1) * D_HV)
        g_h = gcs[:, ks]
        g_end0 = g_h[D_CHUNK - 1:D_CHUNK, :]
        g_end1 = g_h[q_len - 1:q_len, :]
        q_h = q_ref[0, :, ks].astype(F32) * (D_HK ** -0.5)
        k_h = k_ref[0, :, ks].astype(F32)
        v_h = v_ref[0, :, vs]
        q_dec = (q_h * jnp.exp(g_h)).astype(BF16)
        k_inv = (k_h * jnp.exp(-g_h)).astype(BF16)
        k_end = k_h * jnp.exp(jnp.where(first, g_end0, g_end1) - g_h)
        ke0 = jnp.where(first, k_end, 0.0).astype(BF16)
        ke1 = jnp.where(first, 0.0, k_end).astype(BF16)
        attn = lax.dot_general(q_dec, k_inv, (((1,), (1,)), ((), ())), preferred_element_type=F32)
        attn = jnp.where(tri2, attn, 0.0).astype(BF16)
        o = jnp.dot(attn, v_h, preferred_element_type=F32)
        v_t = v_h.astype(F32).T.astype(BF16)
        st_rows = slice(h * D_HV, (h + 1) * D_HV)
        s0 = gs_ref[st_rows, :]
        s1 = s0 * jnp.exp(g_end0) + jnp.dot(v_t, ke0, preferred_element_type=F32)
        s2 = s1 * jnp.exp(g_end1) + jnp.dot(v_t, ke1, preferred_element_type=F32)
        gs_ref[st_rows, :] = s2
        tdims = (((1,), (1,)), ((), ()))
        o0 = lax.dot_general(q_dec, s0.astype(BF16), tdims, preferred_element_type=F32)
        o1 = lax.dot_general(q_dec, s1.astype(BF16), tdims, preferred_element_type=F32)
        o = o + jnp.where(first, o0, o1)
        o = _rms(o, gn_ref[...]) * jax.nn.silu(r_all[:, vs].astype(F32))
        o_ref[0, :, C_INNER + h * D_HV:C_INNER + (h + 1) * D_HV] = o.astype(o_ref.dtype)


def _ssd_gla(z, q, k, v, r, xbc, side_dt, side_g, g_lane, p):
    b, s, _ = z.shape
    q_len = C_CHUNK
    halo = 16
    nb = SSD_BATCHES if b % SSD_BATCHES == 0 else 1
    chunk = lambda width: pl.BlockSpec((nb, q_len, width), lambda i, n: (i, n, 0))
    full = lambda shape: pl.BlockSpec(shape, lambda i, n: (0,) * len(shape))
    pad_lanes = lambda vec: jnp.zeros((1, LANES), F32).at[0, :vec.shape[0]].set(vec.astype(F32))
    gkw = jnp.zeros((LANES, D_KEY), F32).at[g_lane:g_lane + D_GATE_RANK].set(p["gk_w"]).astype(BF16)
    expand = jnp.asarray(np.arange(C_INNER)[None, :] // C_HEAD_DIM == np.arange(LANES)[:, None], BF16)
    return pl.pallas_call(
        _ssd_gla_kernel,
        grid=(b // nb, s // q_len),
        in_specs=[
            chunk(C_INNER), chunk(D_KEY), chunk(D_KEY), chunk(D_VAL), chunk(D_VAL), chunk(C_CONV_DIM),
            pl.BlockSpec((nb, halo, C_CONV_DIM), lambda i, n: (i, jnp.maximum(n * (q_len // halo) - 1, 0), 0)),
            chunk(LANES), chunk(LANES),
            full((C_CONV, C_CONV_DIM)), full((1, C_CONV_DIM)),
            full((1, LANES)), full((1, LANES)), full((1, C_INNER)), full((LANES, C_INNER)),
            full((1, C_INNER)), full((LANES, D_KEY)), full((1, D_KEY)), full((1, D_HV)),
        ],
        out_specs=pl.BlockSpec((nb, q_len, C_INNER + D_VAL), lambda i, n: (i, n, 0)),
        out_shape=jax.ShapeDtypeStruct((b, s, C_INNER + D_VAL), BF16),
        scratch_shapes=[pltpu.VMEM((nb, C_INNER, C_STATE), F32),
                        pltpu.VMEM((nb, D_VAL, D_HK), F32)],
        compiler_params=_cparams(("parallel", "arbitrary")),
        name="ssd_gla",
    )(z, q, k, v, r, xbc, xbc, side_dt, side_g,
      p["conv_w"].astype(F32), p["conv_b"].reshape(1, -1).astype(F32),
      pad_lanes(p["dt_bias"]), pad_lanes(p["a_log"]),
      jnp.repeat(p["d_skip"].astype(F32), C_HEAD_DIM).reshape(1, C_INNER), expand,
      p["ssd_norm"].reshape(1, -1).astype(F32), gkw, p["gk_b"].reshape(1, -1).astype(F32),
      p["gla_norm"].reshape(1, -1).astype(F32))


def _store_rows_tiled(ref, val):
    m, d = val.shape
    nc = d // LANES
    for c in range(nc):
        ref[pl.ds(c, m, stride=nc), :] = val[:, c * LANES:(c + 1) * LANES]


def _load_rows_tiled(ref, m, dtype=None):
    nc = ref.shape[0] // m
    parts = [ref[pl.ds(c, m, stride=nc), :] for c in range(nc)]
    if dtype is not None:
        parts = [p.astype(dtype) for p in parts]
    return jnp.concatenate(parts, axis=1)


def _pack_pairs(x):
    n = x.shape[1] // 2
    u = pltpu.bitcast(x.astype(BF16).astype(F32), jnp.uint32)
    return (u[:, :n] >> 16) | (u[:, n:] & jnp.uint32(0xFFFF0000))


def _unpack_pairs(u):
    lo = pltpu.bitcast(u << 16, F32).astype(BF16)
    hi = pltpu.bitcast(u & jnp.uint32(0xFFFF0000), F32).astype(BF16)
    return jnp.concatenate([lo, hi], axis=1)


RT_GATE, RT_EXPERT, RT_RANK = 0, 2, 4


def _route_block(lg, carry, earlier):
    m = lg.shape[0]
    lane = lax.broadcasted_iota(jnp.int32, (m, LANES), 1)
    lane_f = lane.astype(F32)
    none = float(LANES)
    neg = -jnp.inf
    first_max = lambda v, vmax: jnp.min(jnp.where(v == vmax, lane_f, none), axis=-1, keepdims=True)
    gl = jnp.where(lane < MOE_GROUPS, lg, neg)
    gmax = jnp.max(gl, axis=-1, keepdims=True)
    g_w = 1.0 / jnp.sum(jnp.exp(gl - gmax), axis=-1, keepdims=True)
    lo = MOE_GROUPS + first_max(gl, gmax) * MOE_EPG
    el = jnp.where((lane_f >= lo) & (lane_f < lo + MOE_EPG), lg, neg)
    emax = jnp.max(el, axis=-1, keepdims=True)
    esum = jnp.sum(jnp.exp(el - emax), axis=-1, keepdims=True)
    l0 = first_max(el, emax)
    el2 = jnp.where(lane_f == l0, neg, el)
    emax2 = jnp.max(el2, axis=-1, keepdims=True)
    l1 = first_max(el2, emax2)
    p0 = 1.0 / esum
    p1 = jnp.exp(emax2 - emax) / esum
    w0 = g_w * (p0 / (p0 + p1))
    w1 = g_w * (p1 / (p0 + p1))
    oh0 = lane_f == l0
    oh1 = lane_f == l1
    oh = (oh0 | oh1).astype(BF16)
    cum = jnp.dot(earlier, oh, preferred_element_type=F32) + carry
    rank0 = jnp.sum(jnp.where(oh0, cum, 0.0), axis=-1, keepdims=True)
    rank1 = jnp.sum(jnp.where(oh1, cum, 0.0), axis=-1, keepdims=True)
    carry = carry + jnp.sum(oh.astype(F32), axis=0, keepdims=True)
    rec = jnp.zeros((m, LANES), F32)
    for pos, val in ((RT_GATE, w0), (RT_GATE + 1, w1), (RT_EXPERT, l0 - MOE_GROUPS),
                     (RT_EXPERT + 1, l1 - MOE_GROUPS), (RT_RANK, rank0), (RT_RANK + 1, rank1)):
        rec = jnp.where(lane == pos, val, rec)
    return rec, carry


def _out_proj_kernel(n_parts, *refs):
    a_refs = refs[:n_parts]
    w_refs = refs[n_parts:2 * n_parts]
    h_ref, g_ref, wr_ref, br_ref, tri_ref, ho_ref, xt_ref, rt_ref, cnt_ref, carry_ref = refs[2 * n_parts:]

    @pl.when(pl.program_id(0) == 0)
    def _():
        carry_ref[...] = jnp.zeros_like(carry_ref)

    acc = h_ref[...]
    for a_ref, w_ref in zip(a_refs, w_refs):
        acc = acc + jnp.dot(a_ref[...], w_ref[...], preferred_element_type=F32)
    ho_ref[...] = acc
    xn = _rms(acc, g_ref[...])
    xt_ref[...] = xn.astype(xt_ref.dtype)
    x_hi, x_mid, _ = _split3(xn)
    wr = wr_ref[...]
    lg2 = jnp.dot(x_hi, wr, preferred_element_type=F32)
    lg = (lg2[:, :LANES] + lg2[:, LANES:] + jnp.dot(x_mid, wr[:, :LANES], preferred_element_type=F32)
          + br_ref[...])
    rec, carry = _route_block(lg, carry_ref[...], tri_ref[...])
    rt_ref[...] = rec
    carry_ref[...] = carry
    cnt_ref[0] = carry


def _out_proj(parts, w_parts, h, g, w_route, b_route, *, tm=512):
    t, d = h.shape
    tm = min(tm, t)
    nc = d // LANES
    row = lambda i: (i, 0)
    const = lambda i: (0, 0)
    in_specs = [pl.BlockSpec((tm, a.shape[1]), row) for a in parts]
    in_specs += [pl.BlockSpec(w.shape, const) for w in w_parts]
    in_specs += [pl.BlockSpec((tm, d), row), pl.BlockSpec((1, d), const),
                 pl.BlockSpec((d, 2 * LANES), const), pl.BlockSpec((1, LANES), const),
                 pl.BlockSpec((tm, tm), const)]
    earlier = jnp.asarray(np.tril(np.ones((tm, tm), np.float32), -1), BF16)
    return pl.pallas_call(
        functools.partial(_out_proj_kernel, len(parts)),
        grid=(t // tm,),
        in_specs=in_specs,
        out_specs=[pl.BlockSpec((tm, d), row), pl.BlockSpec((tm, d), row),
                   pl.BlockSpec((tm, LANES), row), pl.BlockSpec((1, 1, LANES), lambda i: (i, 0, 0))],
        out_shape=[jax.ShapeDtypeStruct((t, d), F32), jax.ShapeDtypeStruct((t, d), BF16),
                   jax.ShapeDtypeStruct((t, LANES), F32), jax.ShapeDtypeStruct((t // tm, 1, LANES), F32)],
        scratch_shapes=[pltpu.VMEM((1, LANES), F32)],
        compiler_params=_cparams(("arbitrary",)),
        name="out_proj",
    )(*parts, *w_parts, h, g.reshape(1, d), w_route, b_route, earlier)


LONG_RUN = 128


def _piece_sizes(max_rows, lo=1, hi=None):
    sizes = [1 << b for b in reversed(range(max_rows.bit_length()))]
    return [n for n in sizes if n >= lo and (hi is None or n < hi)]


def _for_each_piece(run, sizes, body):
    off = run // (2 * sizes[0]) * (2 * sizes[0])
    for n in sizes:
        hit = (run & n) != 0
        pl.when(hit)(functools.partial(body, off, n))
        off = off + jnp.where(hit, n, 0)


def _for_each_run(len_ref, tile, long_ref, max_rows, make_body):
    def sweep(sizes):
        def per_expert(e, first):
            run = len_ref[tile * MOE_EXPERTS + e]
            _for_each_piece(run, sizes, make_body(e, first))
            return first + run

        lax.fori_loop(0, MOE_EXPERTS, per_expert, jnp.int32(0))

    sweep(_piece_sizes(max_rows, hi=LONG_RUN))
    pl.when(long_ref[tile] != 0)(functools.partial(sweep, _piece_sizes(max_rows, lo=LONG_RUN)))


def _zero_fill_rows(rows_ref, z_ref, zsem, lo_ref, hi_ref, n_rows, nc):
    zb = z_ref.shape[0] // nc
    z_ref[...] = jnp.zeros_like(z_ref)
    assert MOE_TILE // 2 <= zb and MOE_TILE % zb == 0

    def piece(row0, n):
        return pltpu.make_async_copy(z_ref.at[pl.ds(0, n * nc)], rows_ref.at[pl.ds(row0 * nc, n * nc)], zsem)

    def sweep(issue):
        def per_expert(e, carry):
            lo = lo_ref[e]

            def one(off, n):
                piece(lo + off, n).start() if issue else piece(lo + off, n).wait()

            _for_each_piece(hi_ref[e] - lo, _piece_sizes(MOE_TILE // 2), one)
            return carry

        def per_block(i, carry):
            piece(i * zb, zb).start() if issue else piece(i * zb, zb).wait()
            return carry

        lax.fori_loop(0, MOE_EXPERTS, per_expert, 0)
        lax.fori_loop(hi_ref[MOE_EXPERTS - 1] // zb, n_rows // zb, per_block, 0)

    sweep(True)
    sweep(False)


def _tile_positions(rt, pos_base):
    lane_f = lax.broadcasted_iota(jnp.int32, rt.shape, 1).astype(F32)
    out = []
    for k in range(MOE_TOPK):
        e_lane = rt[:, RT_EXPERT + k:RT_EXPERT + k + 1] + MOE_GROUPS
        out.append(rt[:, RT_RANK + k:RT_RANK + k + 1]
                   + jnp.sum(jnp.where(lane_f == e_lane, pos_base, 0.0), axis=-1, keepdims=True))
    return out


def _moe_dispatch_kernel(dst_ref, len_ref, long_ref, lo_ref, hi_ref, x_ref, rt_ref, pb_ref, rows_ref,
                         sbuf, z_ref, sem, zsem, *, tm, nc, n_steps, n_rows):
    i = pl.program_id(0)
    rows = MOE_TOPK * tm
    slot = lax.rem(i, 2)

    def wait_slot(s):
        pltpu.make_async_copy(sbuf.at[s], rows_ref.at[pl.ds(0, rows * nc)], sem.at[s]).wait()

    @pl.when(i == 0)
    def _():
        _zero_fill_rows(rows_ref, z_ref, zsem, lo_ref, hi_ref, n_rows, nc)

    @pl.when(i >= 2)
    def _():
        wait_slot(slot)

    pos = _tile_positions(rt_ref[...], pb_ref[0])
    lane = lax.broadcasted_iota(jnp.int32, (tm, LANES), 1)
    pos_t = jnp.where(lane == 0, pos[0], jnp.where(lane == 1, pos[1], 0.0)).T
    p_iota = lax.broadcasted_iota(jnp.int32, (rows, tm), 0).astype(F32)
    place = ((p_iota == pos_t[0:1, :]) | (p_iota == pos_t[1:2, :])).astype(BF16)
    _store_rows_tiled(sbuf.at[slot], _pack_pairs(jnp.dot(place, x_ref[...], preferred_element_type=F32)))

    def sender(e, src):
        dst = dst_ref[i * MOE_EXPERTS + e]

        def send(off, n):
            pltpu.make_async_copy(sbuf.at[slot, pl.ds((src + off) * nc, n * nc)],
                                  rows_ref.at[pl.ds((dst + off) * nc, n * nc)], sem.at[slot]).start()

        return send

    _for_each_run(len_ref, i, long_ref, rows, sender)

    @pl.when(i == n_steps - 1)
    def _():
        wait_slot(slot)
        if n_steps > 1:
            wait_slot(1 - slot)


def _moe_dispatch(x, route, pos_base, seg_dst, seg_len, seg_long, pad_lo, pad_hi, n_rows, *, tm):
    t, d = x.shape
    nc = d // (2 * LANES)
    n_steps = t // tm
    row = lambda i, *_: (i, 0)
    return pl.pallas_call(
        functools.partial(_moe_dispatch_kernel, tm=tm, nc=nc, n_steps=n_steps, n_rows=n_rows),
        grid_spec=pltpu.PrefetchScalarGridSpec(
            num_scalar_prefetch=5, grid=(n_steps,),
            in_specs=[pl.BlockSpec((tm, d), row), pl.BlockSpec((tm, LANES), row),
                      pl.BlockSpec((1, 1, LANES), lambda i, *_: (i, 0, 0))],
            out_specs=pl.BlockSpec(memory_space=pl.ANY),
            scratch_shapes=[pltpu.VMEM((2, MOE_TOPK * tm * nc, LANES), jnp.uint32),
                            pltpu.VMEM((MOE_TILE // 2 * nc, LANES), jnp.uint32),
                            pltpu.SemaphoreType.DMA((2,)), pltpu.SemaphoreType.DMA(())]),
        out_shape=jax.ShapeDtypeStruct((n_rows * nc, LANES), jnp.uint32),
        compiler_params=pltpu.CompilerParams(dimension_semantics=("arbitrary",), has_side_effects=True,
                                             vmem_limit_bytes=VMEM_LIMIT),
        name="moe_dispatch",
    )(seg_dst, seg_len, seg_long, pad_lo, pad_hi, x, route, pos_base)


def _moe_kernel(te_ref, tv_ref, nu_ref, x_ref, wg_ref, wu_ref, wd_ref, y_ref):
    i = pl.program_id(0)
    live = i < nu_ref[0]
    half = MOE_TILE // 2
    nc = x_ref.shape[0] // MOE_TILE

    def mlp(rows):
        x = _unpack_pairs(_load_rows_tiled(x_ref.at[pl.ds(0, rows * nc)], rows))
        gate = jnp.dot(x, wg_ref[...].astype(BF16), preferred_element_type=F32)
        up = jnp.dot(x, wu_ref[...].astype(BF16), preferred_element_type=F32)
        act = (jax.nn.silu(gate) * up).astype(BF16)
        y = jnp.dot(act, wd_ref[...].astype(BF16), preferred_element_type=F32)
        _store_rows_tiled(y_ref.at[pl.ds(0, rows * nc)], _pack_pairs(y))
        if rows < MOE_TILE:
            y_ref[pl.ds(rows * nc, (MOE_TILE - rows) * nc), :] = jnp.zeros(((MOE_TILE - rows) * nc, LANES), y_ref.dtype)

    @pl.when(i >= nu_ref[0])
    def _():
        y_ref[...] = jnp.zeros_like(y_ref)

    pl.when(live & (tv_ref[i] > half))(functools.partial(mlp, MOE_TILE))
    pl.when(live & (tv_ref[i] <= half))(functools.partial(mlp, half))


def _moe_experts(x_rows, tile_expert, tile_valid, n_used, w_gate, w_up, w_down, layer):
    d, ff = w_gate.shape[-2:]
    nc = d // (2 * LANES)
    n_tiles = x_rows.shape[0] // (MOE_TILE * nc)
    live = lambda i, nu: jnp.minimum(i, nu[0] - 1)
    w_spec = lambda a, b: pl.BlockSpec((None, None, a, b), lambda i, te, tv, nu: (layer, te[live(i, nu)], 0, 0))
    return pl.pallas_call(
        _moe_kernel,
        grid_spec=pltpu.PrefetchScalarGridSpec(
            num_scalar_prefetch=3, grid=(n_tiles,),
            in_specs=[pl.BlockSpec((MOE_TILE * nc, LANES), lambda i, te, tv, nu: (live(i, nu), 0)),
                      w_spec(d, ff), w_spec(d, ff), w_spec(ff, d)],
            out_specs=pl.BlockSpec((MOE_TILE * nc, LANES), lambda i, te, tv, nu: (i, 0))),
        out_shape=jax.ShapeDtypeStruct(x_rows.shape, x_rows.dtype),
        compiler_params=_cparams(("arbitrary",)),
        name="moe_experts",
    )(tile_expert, tile_valid, n_used, x_rows, w_gate, w_up, w_down)


def _moe_combine_kernel(src_ref, len_ref, long_ref, h_ref, rt_ref, pb_ref, g_ref, y_hbm, o_ref, ybuf, sem,
                        *, tm, nc, final):
    i = pl.program_id(0)
    n_steps = pl.num_programs(0)
    rows = MOE_TOPK * tm

    def fetch(tile, slot):
        def receiver(e, dst):
            src = src_ref[tile * MOE_EXPERTS + e]

            def recv(off, n):
                pltpu.make_async_copy(y_hbm.at[pl.ds((src + off) * nc, n * nc)],
                                      ybuf.at[slot, pl.ds((dst + off) * nc, n * nc)], sem.at[slot]).start()

            return recv

        _for_each_run(len_ref, tile, long_ref, rows, receiver)

    slot = lax.rem(i, 2)

    @pl.when(i == 0)
    def _():
        fetch(0, 0)

    @pl.when(i + 1 < n_steps)
    def _():
        fetch(i + 1, 1 - slot)

    pltpu.make_async_copy(y_hbm.at[pl.ds(0, rows * nc)], ybuf.at[slot], sem.at[slot]).wait()
    y = _unpack_pairs(_load_rows_tiled(ybuf.at[slot], rows))
    rt = rt_ref[...]
    pos_f = lax.broadcasted_iota(jnp.int32, (tm, rows), 1).astype(F32)
    pick = jnp.zeros((tm, rows), F32)
    for k, pos in enumerate(_tile_positions(rt, pb_ref[0])):
        pick = jnp.where(pos_f == pos, rt[:, RT_GATE + k:RT_GATE + k + 1], pick)
    out = h_ref[...] + jnp.dot(pick.astype(BF16), y, preferred_element_type=F32)
    o_ref[...] = _rms(out, g_ref[...]) if final else out


def _moe_combine_rows(h, y_rows, route, seg_src, seg_len, seg_long, pos_base, g, *, tm, final):
    t, d = h.shape
    nc = d // (2 * LANES)
    row = lambda i, *_: (i, 0)
    return pl.pallas_call(
        functools.partial(_moe_combine_kernel, tm=tm, nc=nc, final=final),
        grid_spec=pltpu.PrefetchScalarGridSpec(
            num_scalar_prefetch=3, grid=(t // tm,),
            in_specs=[pl.BlockSpec((tm, d), row), pl.BlockSpec((tm, LANES), row),
                      pl.BlockSpec((1, 1, LANES), lambda i, *_: (i, 0, 0)),
                      pl.BlockSpec((1, d), lambda i, *_: (0, 0)),
                      pl.BlockSpec(memory_space=pl.ANY)],
            out_specs=pl.BlockSpec((tm, d), row),
            scratch_shapes=[pltpu.VMEM((2, MOE_TOPK * tm * nc, LANES), y_rows.dtype),
                            pltpu.SemaphoreType.DMA((2,))]),
        out_shape=jax.ShapeDtypeStruct((t, d), F32),
        compiler_params=_cparams(("arbitrary",)),
        name="moe_combine",
    )(seg_src, seg_len, seg_long, h, route, pos_base, g.reshape(1, d), y_rows)


def _moe(h, x, route, tile_counts, w_gate, w_up, w_down, layer, g, *, final):
    t = route.shape[0]
    n_tt = tile_counts.shape[0]
    tm = t // n_tt
    after = tile_counts[:, 0, MOE_GROUPS:MOE_GROUPS + MOE_EXPERTS].astype(jnp.int32)
    before = jnp.concatenate([jnp.zeros((1, MOE_EXPERTS), jnp.int32), after[:-1]], axis=0)
    cnt = after[-1]
    padded = (cnt + MOE_TILE - 1) // MOE_TILE * MOE_TILE
    pad_ends = jnp.cumsum(padded)
    starts = (pad_ends - padded).astype(jnp.int32)
    n_tiles = (t * MOE_TOPK + MOE_EXPERTS * (MOE_TILE - 1)) // MOE_TILE
    tile_start = jnp.arange(n_tiles, dtype=jnp.int32) * MOE_TILE
    tile_expert = jnp.minimum(jnp.sum(tile_start[:, None] >= pad_ends[None, :], axis=1),
                              MOE_EXPERTS - 1).astype(jnp.int32)
    n_used = (pad_ends[-1] // MOE_TILE).astype(jnp.int32).reshape(1)
    tile_valid = jnp.clip(cnt[tile_expert] - (tile_start - starts[tile_expert]), 0, MOE_TILE).astype(jnp.int32)
    n_rows = n_tiles * MOE_TILE
    seg_len = (after - before).reshape(-1)
    seg_off = jnp.cumsum(after - before, axis=1) - (after - before)
    seg_row = (starts[None, :] + before).reshape(-1)
    pos_base = jnp.zeros((n_tt, 1, LANES), F32).at[:, 0, MOE_GROUPS:MOE_GROUPS + MOE_EXPERTS].set(
        (seg_off - before).astype(F32))
    seg_long = jnp.any(after - before >= LONG_RUN, axis=1).astype(jnp.int32)
    x_rows = _moe_dispatch(x, route, pos_base, seg_row, seg_len, seg_long, starts + cnt,
                           pad_ends.astype(jnp.int32), n_rows, tm=tm)
    y_rows = _moe_experts(x_rows, tile_expert, tile_valid, n_used, w_gate, w_up, w_down, layer)
    return _moe_combine_rows(h, y_rows, route, seg_row, seg_len, seg_long, pos_base, g, tm=tm, final=final)


def _router_weights(w_group, b_group, w_router, b_router):
    d = w_group.shape[0]
    w = jnp.concatenate([w_group, w_router, jnp.zeros((d, LANES - MOE_GROUPS - MOE_EXPERTS), F32)], axis=1)
    b = jnp.zeros((1, LANES), F32)
    b = b.at[0, :MOE_GROUPS].set(b_group).at[0, MOE_GROUPS:MOE_GROUPS + MOE_EXPERTS].set(b_router)
    w_hi = w.astype(BF16)
    w_mid = (w - w_hi.astype(F32)).astype(BF16)
    return jnp.concatenate([w_hi, w_mid], axis=1), b


def kernel(x, norm_mix, norm_moe, norm_final, even_w_in, even_sinks, even_forget_bias, even_w_out,
           odd_w_in, odd_conv_w, odd_conv_b, odd_dt_bias, odd_a_log, odd_d_skip, odd_ssd_norm,
           odd_gk_w, odd_gk_b, odd_gla_norm, odd_w_out, moe_w_group, moe_b_group, moe_w_router,
           moe_b_router, moe_w_gate, moe_w_up, moe_w_down):
    b, s, d = x.shape
    t = b * s
    depth = norm_mix.shape[0]
    h = x.reshape(t, d)
    for layer in range(depth):
        i = layer // 2
        if layer % 2 == 0:
            w = even_w_in.astype(BF16)[i]
            n_ab = (A_Q_HEADS + 2 * A_KV_HEADS + 3 * B_HEADS) * HEAD_DIM
            w_main = w[:, :n_ab]
            w_aux = jnp.pad(w[:, n_ab:], ((0, 0), (0, LANES - B_HEADS)))
            n_a, kv_a, n_b = (A_Q_HEADS + 2 * A_KV_HEADS) * HEAD_DIM, A_KV_HEADS * HEAD_DIM, B_HEADS * HEAD_DIM
            swa_rows, fox_rows = SWA_QBLOCKS * A_WINDOW, 256
            proj, f_aux, va_t, vb_t = _norm_proj(
                h, norm_mix[layer], [w_main, w_aux], ((0, 0, n_ab, BF16), (1, 0, LANES, F32)), tm=1024,
                t_plan=((0, n_a - kv_a, kv_a, BF16, swa_rows), (0, n_ab - n_b, n_b, BF16, fox_rows)))
            proj = proj.reshape(b, s, -1)
            out_a = _swa(proj, va_t.reshape(b, s // swa_rows, kv_a, swa_rows), even_sinks[i])
            c, ct = _fox_gate(f_aux.reshape(b, s, LANES), even_forget_bias[i])
            out_b = _fox(proj, vb_t.reshape(b, s // fox_rows, n_b, fox_rows), c, ct, tk=fox_rows)
            n_ha = A_Q_HEADS * HEAD_DIM
            w_out = even_w_out[i].astype(BF16)
            parts = [out_a.reshape(t, -1), out_b.reshape(t, -1)]
            w_parts = [w_out[:n_ha], w_out[n_ha:]]
        else:
            w = odd_w_in.astype(BF16)[i]
            o_z, o_xbc = 0, C_INNER
            o_dt = o_xbc + C_CONV_DIM
            o_q = o_dt + C_HEADS
            o_k = o_q + D_KEY
            o_v = o_k + D_KEY
            o_g = o_v + D_VAL
            o_r = o_g + D_GATE_RANK
            n_b = w.shape[1] - o_dt
            w_a = w[:, :o_dt]
            w_b = jnp.pad(w[:, o_dt:], ((0, 0), (0, -n_b % LANES)))
            g_win = (o_g - o_dt) // LANES * LANES
            plan = ((0, o_z, C_INNER, BF16), (1, o_q - o_dt, D_KEY, BF16), (1, o_k - o_dt, D_KEY, BF16),
                    (1, o_v - o_dt, D_VAL, BF16), (1, o_r - o_dt, D_VAL, BF16), (0, o_xbc, C_CONV_DIM, BF16),
                    (1, 0, LANES, F32), (1, g_win, LANES, F32))
            outs = _norm_proj(h, norm_mix[layer], [w_a, w_b], plan, tm=512)
            params = dict(conv_w=odd_conv_w[i], conv_b=odd_conv_b[i], dt_bias=odd_dt_bias[i], a_log=odd_a_log[i],
                          d_skip=odd_d_skip[i], ssd_norm=odd_ssd_norm[i], gk_w=odd_gk_w[i], gk_b=odd_gk_b[i],
                          gla_norm=odd_gla_norm[i])
            mixed = _ssd_gla(*[o.reshape(b, s, -1) for o in outs], o_g - o_dt - g_win, params)
            parts = [mixed.reshape(t, -1)]
            w_parts = [odd_w_out[i].astype(BF16)]
        w_route, b_route = _router_weights(moe_w_group[layer], moe_b_group[layer],
                                           moe_w_router[layer], moe_b_router[layer])
        h, x_tiled, route, tile_counts = _out_proj(parts, w_parts, h, norm_moe[layer], w_route, b_route)
        h = _moe(h, x_tiled, route, tile_counts, moe_w_gate, moe_w_up, moe_w_down, layer, norm_final,
                 final=layer == depth - 1)
    out = h
    return out.reshape(b, s, d)
```

```python
import functools

import numpy as np
import jax
import jax.numpy as jnp
from jax import lax
from jax.experimental import pallas as pl
from jax.experimental.pallas import tpu as pltpu

F32 = jnp.float32
BF16 = jnp.bfloat16

RMS_EPS = 1e-6
HEAD_DIM = 64
A_Q_HEADS = 8
A_KV_HEADS = 2
A_GROUP = A_Q_HEADS // A_KV_HEADS
A_WINDOW = 128
B_HEADS = 8
C_HEADS = 16
C_HEAD_DIM = 64
C_INNER = C_HEADS * C_HEAD_DIM
C_GROUPS = 2
C_HPG = C_HEADS // C_GROUPS
C_STATE = 128
C_CONV = 4
C_CHUNK = 128
C_CONV_DIM = C_INNER + 2 * C_GROUPS * C_STATE
D_HEADS = 4
D_HK = 128
D_HV = 256
D_KEY = D_HEADS * D_HK
D_VAL = D_HEADS * D_HV
D_GATE_RANK = 16
D_GATE_NORM = 16.0
D_CHUNK = 64
MOE_GROUPS = 4
MOE_EPG = 8
MOE_EXPERTS = MOE_GROUPS * MOE_EPG
MOE_TOPK = 2

LANES = 128
VMEM_LIMIT = 48 * 1024 * 1024
MOE_TILE = 512


def _cparams(sem):
    return pltpu.CompilerParams(dimension_semantics=sem, vmem_limit_bytes=VMEM_LIMIT)


def _rms(x, g):
    ms = jnp.mean(x * x, axis=-1, keepdims=True)
    return x * lax.rsqrt(ms + RMS_EPS) * g


def _norm_proj_kernel(n_w, plan, t_plan, *refs):
    x_ref, g_ref = refs[:2]
    w_refs = refs[2:2 + n_w]
    o_refs = refs[2 + n_w:2 + n_w + len(plan)]
    t_refs = refs[2 + n_w + len(plan):2 + n_w + len(plan) + len(t_plan)]
    res_refs = refs[2 + n_w + len(plan) + len(t_plan):]
    xn = _rms(x_ref[...], g_ref[...]).astype(BF16)
    for w_ref, res_ref in zip(w_refs, res_refs):
        res_ref[...] = jnp.dot(xn, w_ref[...], preferred_element_type=F32)
    for o_ref, (wi, start, width, _) in zip(o_refs, plan):
        o_ref[...] = res_refs[wi][:, start:start + width].astype(o_ref.dtype)
    for o_ref, (wi, start, width, _, rows) in zip(t_refs, t_plan):
        for u in range(o_ref.shape[0]):
            o_ref[u] = res_refs[wi][u * rows:(u + 1) * rows, start:start + width].T.astype(o_ref.dtype)


def _norm_proj(x, g, weights, plan, *, tm, t_plan=()):
    t, d = x.shape
    tm = min(tm, t)
    row = lambda i: (i, 0)
    const = lambda i: (0, 0)
    in_specs = [pl.BlockSpec((tm, d), row), pl.BlockSpec((1, d), const)]
    in_specs += [pl.BlockSpec(w.shape, const, pipeline_mode=pl.Buffered(1)) for w in weights]
    return pl.pallas_call(
        functools.partial(_norm_proj_kernel, len(weights), plan, t_plan),
        grid=(t // tm,),
        in_specs=in_specs,
        out_specs=([pl.BlockSpec((tm, width), row) for _, _, width, _ in plan]
                   + [pl.BlockSpec((tm // rows, width, rows), lambda i: (i, 0, 0))
                      for _, _, width, _, rows in t_plan]),
        out_shape=([jax.ShapeDtypeStruct((t, width), dtype) for _, _, width, dtype in plan]
                   + [jax.ShapeDtypeStruct((t // rows, width, rows), dtype) for _, _, width, dtype, rows in t_plan]),
        scratch_shapes=[pltpu.VMEM((tm, w.shape[1]), F32) for w in weights],
        compiler_params=_cparams(("parallel",)),
        name="norm_proj",
    )(x, g.reshape(1, d), *weights)


SWA_QBLOCKS = 8


def _swa_kernel(sink_ref, slope_ref, q_ref, kp_ref, kc_ref, vp_ref, vc_ref, o_ref):
    n = pl.program_id(1)
    blk = A_WINDOW
    wide = A_GROUP * blk
    key = lax.broadcasted_iota(jnp.int32, (2 * blk, wide), 0)
    qry = lax.broadcasted_iota(jnp.int32, (2 * blk, wide), 1) % blk
    dist = blk + qry - key
    in_window = (dist >= 0) & (dist < A_WINDOW)
    distf = dist.astype(F32)
    nt = (((1,), (1,)), ((), ()))
    k_all = jnp.concatenate([kp_ref[0], kc_ref[0]], axis=0)
    v_all = jnp.concatenate([vp_ref[0], vc_ref[0]], axis=1)
    units = [(j, kh) for j in range(SWA_QBLOCKS) for kh in range(A_KV_HEADS)]
    scores = []
    for j, kh in units:
        k = k_all[j * blk:(j + 2) * blk, kh * HEAD_DIM:(kh + 1) * HEAD_DIM]
        q = jnp.concatenate([q_ref[0, j * blk:(j + 1) * blk,
                                   (kh * A_GROUP + g) * HEAD_DIM:(kh * A_GROUP + g + 1) * HEAD_DIM]
                             for g in range(A_GROUP)], axis=0)
        scores.append(lax.dot_general(k, q, nt, preferred_element_type=F32))
    probs = []
    for (j, kh), s in zip(units, scores):
        valid = in_window & ((key >= blk) | (n * SWA_QBLOCKS + j > 0))
        s = s * (HEAD_DIM ** -0.5) - slope_ref[kh:kh + 1, :] * distf
        s = jnp.where(valid, s, -jnp.inf)
        sink = sink_ref[kh:kh + 1, :]
        m = jnp.maximum(jnp.max(s, axis=0, keepdims=True), sink)
        p = jnp.exp(s - m)
        probs.append((p.astype(BF16), jnp.sum(p, axis=0, keepdims=True) + jnp.exp(sink - m)))
    for j in range(SWA_QBLOCKS):
        outs = []
        for kh in range(A_KV_HEADS):
            p, denom = probs[j * A_KV_HEADS + kh]
            v_t = v_all[kh * HEAD_DIM:(kh + 1) * HEAD_DIM, j * blk:(j + 2) * blk]
            o_t = jnp.dot(v_t, p, preferred_element_type=F32) / denom
            outs += [o_t[:, g * blk:(g + 1) * blk] for g in range(A_GROUP)]
        o_ref[0, j * blk:(j + 1) * blk, :] = jnp.concatenate(outs, axis=0).T.astype(o_ref.dtype)


def _swa(proj, v_t, sinks):
    b, s, _ = proj.shape
    blk = A_WINDOW
    qw = A_Q_HEADS * HEAD_DIM
    kw = A_KV_HEADS * HEAD_DIM
    k_blk = qw // kw
    per_lane = lambda vec: jnp.repeat(vec.astype(F32), blk).reshape(A_KV_HEADS, A_GROUP * blk)
    slopes = jnp.asarray(2.0 ** (-8.0 * np.arange(1, A_Q_HEADS + 1) / A_Q_HEADS), F32)
    tq = SWA_QBLOCKS * blk
    assert v_t.shape == (b, s // tq, kw, tq)
    prev = lambda n: jnp.maximum(n * SWA_QBLOCKS - 1, 0)
    full = pl.BlockSpec((A_KV_HEADS, A_GROUP * blk), lambda i, n: (0, 0))
    return pl.pallas_call(
        _swa_kernel,
        grid=(b, s // tq),
        in_specs=[
            full, full,
            pl.BlockSpec((1, tq, qw), lambda i, n: (i, n, 0)),
            pl.BlockSpec((1, blk, kw), lambda i, n: (i, prev(n), k_blk)),
            pl.BlockSpec((1, tq, kw), lambda i, n: (i, n, k_blk)),
            pl.BlockSpec((1, None, kw, blk), lambda i, n: (i, jnp.maximum(n - 1, 0), 0, SWA_QBLOCKS - 1)),
            pl.BlockSpec((1, None, kw, tq), lambda i, n: (i, n, 0, 0)),
        ],
        out_specs=pl.BlockSpec((1, tq, qw), lambda i, n: (i, n, 0)),
        out_shape=jax.ShapeDtypeStruct((b, s, qw), BF16),
        compiler_params=_cparams(("parallel", "parallel")),
        name="swa",
    )(per_lane(sinks), per_lane(slopes), proj, proj, proj, v_t, v_t)


def _tril(n, dtype=F32):
    r = lax.broadcasted_iota(jnp.int32, (n, n), 0)
    c = lax.broadcasted_iota(jnp.int32, (n, n), 1)
    return (c <= r).astype(dtype)


def _split3(x):
    hi = x.astype(BF16)
    r = x - hi.astype(F32)
    mid = r.astype(BF16)
    return hi, mid, (r - mid.astype(F32)).astype(BF16)


def _dot_mask_lhs(mask, x):
    return sum(jnp.dot(mask, part, preferred_element_type=F32) for part in _split3(x))


def _dot_mask_rhs(x, mask):
    return sum(jnp.dot(part, mask, preferred_element_type=F32) for part in _split3(x))


def _fox_gate_kernel(f_ref, b_ref, c_ref, ct_ref):
    tri = _tril(LANES, BF16)
    carry = jnp.zeros((1, LANES), F32)
    for n in range(f_ref.shape[1] // LANES):
        rows = slice(n * LANES, (n + 1) * LANES)
        lf = jax.nn.log_sigmoid(f_ref[0, rows, :] + b_ref[...])
        cs = _dot_mask_lhs(tri, lf) + carry
        carry = cs[LANES - 1:LANES, :]
        c_ref[0, rows, :] = cs
        ct_ref[0, n] = cs.T[:B_HEADS, :]


def _fox_gate(f_aux, bias):
    b, s, _ = f_aux.shape
    nb = s // LANES
    bias_p = jnp.zeros((1, LANES), F32).at[0, :B_HEADS].set(bias.astype(F32))
    return pl.pallas_call(
        _fox_gate_kernel,
        grid=(b,),
        in_specs=[pl.BlockSpec((1, s, LANES), lambda i: (i, 0, 0)),
                  pl.BlockSpec((1, LANES), lambda i: (0, 0))],
        out_specs=[pl.BlockSpec((1, s, LANES), lambda i: (i, 0, 0)),
                   pl.BlockSpec((1, nb, B_HEADS, LANES), lambda i: (i, 0, 0, 0))],
        out_shape=[jax.ShapeDtypeStruct((b, s, LANES), F32),
                   jax.ShapeDtypeStruct((b, nb, B_HEADS, LANES), F32)],
        compiler_params=_cparams(("parallel",)),
        name="fox_gate",
    )(f_aux, bias_p)


def _fox_kernel(q0_ref, q1_ref, k0_ref, k1_ref, vt_ref, c_ref, ctq_ref, o_ref, *, tq, tk, heads_per_step):
    qi = pl.program_id(1)
    sub = tq // LANES
    key = lax.broadcasted_iota(jnp.int32, (tk, tq), 0)
    qry = lax.broadcasted_iota(jnp.int32, (tk, tq), 1)
    per_q = tq // tk
    causal = [u * tk + key <= qry for u in range(per_q)]
    nt = (((1,), (1,)), ((), ()))
    half = B_HEADS // 2
    q_refs, k_refs = (q0_ref, q1_ref), (k0_ref, k1_ref)
    outs = []
    for h0 in range(0, B_HEADS, heads_per_step):
        heads = list(range(h0, h0 + heads_per_step))
        hsl = [slice(h * HEAD_DIM, (h + 1) * HEAD_DIM) for h in heads]
        lsl = [slice((h % half) * HEAD_DIM, (h % half + 1) * HEAD_DIM) for h in heads]
        qs = [q_refs[h // half][0, :, ls] * (HEAD_DIM ** -0.5)
              for h, ls in zip(heads, lsl)]
        cqs = [jnp.concatenate([ctq_ref[0, u, h:h + 1, :] for u in range(sub)], axis=1) for h in heads]

        def step(j, carry, mask, heads=heads, hsl=hsl, lsl=lsl, qs=qs, cqs=cqs):
            start = pl.multiple_of(j * tk, tk)
            sts = [lax.dot_general(k_refs[h // half][0, pl.ds(start, tk), ls], q, nt,
                                   preferred_element_type=F32)
                   for h, ls, q in zip(heads, lsl, qs)]
            ps, stats = [], []
            for idx, h in enumerate(heads):
                m, l, _ = carry[3 * idx:3 * idx + 3]
                ck = c_ref[0, pl.ds(start, tk), h:h + 1]
                st = (sts[idx] - ck) + cqs[idx]
                if mask is not None:
                    st = jnp.where(mask, st, -jnp.inf)
                m_new = jnp.maximum(m, jnp.max(st, axis=0, keepdims=True))
                alpha = jnp.exp(m - m_new)
                p = jnp.exp(st - m_new)
                stats.append((m_new, alpha, alpha * l + jnp.sum(p, axis=0, keepdims=True)))
                ps.append(p.astype(BF16))
            new = []
            for idx in range(len(heads)):
                m_new, alpha, l = stats[idx]
                pv = jnp.dot(vt_ref[0, j, hsl[idx], :], ps[idx], preferred_element_type=F32)
                new += [m_new, l, alpha * carry[3 * idx + 2] + pv]
            return tuple(new)

        init = (jnp.full((1, tq), -jnp.inf, F32), jnp.zeros((1, tq), F32),
                jnp.zeros((HEAD_DIM, tq), F32)) * heads_per_step
        carry = lax.fori_loop(0, qi * per_q, functools.partial(step, mask=None), init)
        for u in range(per_q):
            carry = step(qi * per_q + u, carry, causal[u])
        for idx in range(heads_per_step):
            outs.append(carry[3 * idx + 2] / carry[3 * idx + 1])
    o_ref[0] = jnp.concatenate(outs, axis=0).T.astype(o_ref.dtype)


def _fox(proj, v_t, c, ct, *, tq=256, tk=256, heads_per_step=8):
    b, s, _ = proj.shape
    w = B_HEADS * HEAD_DIM
    nk = s // tk
    sub = tq // LANES
    hw = w // 2
    base = (A_Q_HEADS + 2 * A_KV_HEADS) * HEAD_DIM
    qb, kb = base // hw, (base + w) // hw
    assert v_t.shape == (b, nk, w, tk)
    return pl.pallas_call(
        functools.partial(_fox_kernel, tq=tq, tk=tk, heads_per_step=heads_per_step),
        grid=(b, s // tq),
        in_specs=[
            pl.BlockSpec((1, tq, hw), lambda i, n: (i, n, qb)),
            pl.BlockSpec((1, tq, hw), lambda i, n: (i, n, qb + 1)),
            pl.BlockSpec((1, s, hw), lambda i, n: (i, 0, kb)),
            pl.BlockSpec((1, s, hw), lambda i, n: (i, 0, kb + 1)),
            pl.BlockSpec((1, nk, w, tk), lambda i, n: (i, 0, 0, 0)),
            pl.BlockSpec((1, s, LANES), lambda i, n: (i, 0, 0)),
            pl.BlockSpec((1, sub, B_HEADS, LANES), lambda i, n: (i, n, 0, 0)),
        ],
        out_specs=pl.BlockSpec((1, tq, w), lambda i, n: (i, n, 0)),
        out_shape=jax.ShapeDtypeStruct((b, s, w), BF16),
        compiler_params=_cparams(("parallel", "parallel")),
        name="fox",
    )(proj, proj, proj, proj, v_t, c, ct)


SSD_BATCHES = 4


def _ssd_gla_kernel(*refs):
    n_data, n_par = 9, 10
    data, params = refs[:n_data], refs[n_data:n_data + n_par]
    o_ref, hs_ref, gs_ref = refs[n_data + n_par:]

    @pl.when(pl.program_id(1) == 0)
    def _():
        hs_ref[...] = jnp.zeros_like(hs_ref)
        gs_ref[...] = jnp.zeros_like(gs_ref)

    for bb in range(o_ref.shape[0]):
        one = lambda ref: ref.at[pl.ds(bb, 1)]
        _ssd_gla_chunk(*[one(ref) for ref in data], *params, one(o_ref), hs_ref.at[bb], gs_ref.at[bb])


def _ssd_gla_chunk(z_ref, q_ref, k_ref, v_ref, r_ref, xc_ref, xp_ref, sdt_ref, sg_ref,
                   cw_ref, cb_ref, dtb_ref, alog_ref, dsk_ref, ex_ref, sn_ref, gkw_ref, gkb_ref, gn_ref,
                   o_ref, hs_ref, gs_ref):
    c = pl.program_id(1)
    q_len = C_CHUNK
    halo = xp_ref.shape[1]

    prev = xp_ref[0]
    cur = xc_ref[0]
    ext = jnp.concatenate([jnp.where(c > 0, prev, jnp.zeros_like(prev)), cur], axis=0)
    t_out = lax.broadcasted_iota(jnp.int32, (q_len, halo + q_len), 0)
    t_in = lax.broadcasted_iota(jnp.int32, (q_len, halo + q_len), 1) - halo
    acc = cb_ref[...] + cw_ref[C_CONV - 1:C_CONV, :] * cur.astype(F32)
    for j in range(C_CONV - 1):
        shift = (t_in == t_out - (C_CONV - 1 - j)).astype(BF16)
        acc = acc + cw_ref[j:j + 1, :] * jnp.dot(shift, ext, preferred_element_type=F32)
    xbc = jax.nn.silu(acc)
    xs = xbc[:, :C_INNER]
    gs_w = C_GROUPS * C_STATE
    bm = xbc[:, C_INNER:C_INNER + gs_w].astype(BF16)
    cm = xbc[:, C_INNER + gs_w:].astype(BF16)

    row = lax.broadcasted_iota(jnp.int32, (q_len, q_len), 0)
    col = lax.broadcasted_iota(jnp.int32, (q_len, q_len), 1)
    tri = col <= row

    lane = lax.broadcasted_iota(jnp.int32, (1, LANES), 1)
    dt = jnp.where(lane < C_HEADS, jax.nn.softplus(sdt_ref[0] + dtb_ref[...]), 0.0)
    dta = dt * -jnp.exp(alog_ref[...])
    acs = _dot_mask_lhs(tri.astype(BF16), dta)
    acs_t = acs.T
    chunk_dec = jnp.exp(acs[q_len - 1:q_len, :])
    expand = ex_ref[...]
    dt_x = _dot_mask_rhs(dt, expand)
    acs_x = _dot_mask_rhs(acs, expand)
    xd = xs * dt_x
    xd_b = xd.astype(BF16)
    xdd = xd * jnp.exp(acs_x[q_len - 1:q_len, :] - acs_x)
    low_half = lax.broadcasted_iota(jnp.int32, (q_len, LANES), 1) < C_HEAD_DIM

    y_pairs, y_offs = [], []
    tdims = (((1,), (1,)), ((), ()))
    for g in range(C_GROUPS):
        b_g = bm[:, g * C_STATE:(g + 1) * C_STATE]
        c_g = cm[:, g * C_STATE:(g + 1) * C_STATE]
        cb = lax.dot_general(c_g, b_g, tdims, preferred_element_type=F32)
        h0 = g * C_HPG
        grp = slice(h0 * C_HEAD_DIM, (h0 + C_HPG) * C_HEAD_DIM)
        y_offs.append(lax.dot_general(c_g, hs_ref[grp, :].astype(BF16), tdims, preferred_element_type=F32))
        for h in range(h0, h0 + C_HPG, 2):
            xp = xd_b[:, h * C_HEAD_DIM:(h + 2) * C_HEAD_DIM]
            halves = []
            for hh in (h, h + 1):
                seg = jnp.exp(jnp.where(tri, acs[:, hh:hh + 1] - acs_t[hh:hh + 1, :], -jnp.inf))
                halves.append(jnp.dot((cb * seg).astype(BF16), xp, preferred_element_type=F32))
            y_pairs.append(jnp.where(low_half, halves[0], halves[1]))
        upd = jnp.dot(xdd[:, grp].T.astype(BF16), b_g, preferred_element_type=F32)
        for hh in range(C_HPG):
            h = h0 + hh
            ps = slice(h * C_HEAD_DIM, (h + 1) * C_HEAD_DIM)
            us = slice(hh * C_HEAD_DIM, (hh + 1) * C_HEAD_DIM)
            hs_ref[ps, :] = hs_ref[ps, :] * chunk_dec[0:1, h:h + 1] + upd[us, :]
    y = (jnp.concatenate(y_pairs, axis=1) + jnp.concatenate(y_offs, axis=1) * jnp.exp(acs_x)
         + dsk_ref[...] * xs)
    y = y * jax.nn.silu(z_ref[0].astype(F32))
    o_ref[0, :, :C_INNER] = _rms(y, sn_ref[...]).astype(o_ref.dtype)

    same = (row // D_CHUNK) == (col // D_CHUNK)
    tri2 = tri & same
    la = jnp.dot(sg_ref[0].astype(BF16), gkw_ref[...], preferred_element_type=F32) + gkb_ref[...]
    la = jax.nn.log_sigmoid(la) / D_GATE_NORM
    gcs = _dot_mask_lhs(tri2.astype(BF16), la)
    first = lax.broadcasted_iota(jnp.int32, (q_len, 1), 0) < D_CHUNK
    r_all = r_ref[0]
    for h in range(D_HEADS):
        ks = slice(h * D_HK, (h + 1) * D_HK)
        vs = slice(h * D_HV, (h + 1) * D_HV)
        g_h = gcs[:, ks]
        g_end0 = g_h[D_CHUNK - 1:D_CHUNK, :]
        g_end1 = g_h[q_len - 1:q_len, :]
        q_h = q_ref[0, :, ks].astype(F32) * (D_HK ** -0.5)
        k_h = k_ref[0, :, ks].astype(F32)
        v_h = v_ref[0, :, vs]
        q_dec = (q_h * jnp.exp(g_h)).astype(BF16)
        k_inv = (k_h * jnp.exp(-g_h)).astype(BF16)
        k_end = k_h * jnp.exp(jnp.where(first, g_end0, g_end1) - g_h)
        ke0 = jnp.where(first, k_end, 0.0).astype(BF16)
        ke1 = jnp.where(first, 0.0, k_end).astype(BF16)
        attn = lax.dot_general(q_dec, k_inv, (((1,), (1,)), ((), ())), preferred_element_type=F32)
        attn = jnp.where(tri2, attn, 0.0).astype(BF16)
        o = jnp.dot(attn, v_h, preferred_element_type=F32)
        v_t = v_h.astype(F32).T.astype(BF16)
        st_rows = slice(h * D_HV, (h + 1) * D_HV)
        s0 = gs_ref[st_rows, :]
        s1 = s0 * jnp.exp(g_end0) + jnp.dot(v_t, ke0, preferred_element_type=F32)
        s2 = s1 * jnp.exp(g_end1) + jnp.dot(v_t, ke1, preferred_element_type=F32)
        gs_ref[st_rows, :] = s2
        tdims = (((1,), (1,)), ((), ()))
        o0 = lax.dot_general(q_dec, s0.astype(BF16), tdims, preferred_element_type=F32)
        o1 = lax.dot_general(q_dec, s1.astype(BF16), tdims, preferred_element_type=F32)
        o = o + jnp.where(first, o0, o1)
        o = _rms(o, gn_ref[...]) * jax.nn.silu(r_all[:, vs].astype(F32))
        o_ref[0, :, C_INNER + h * D_HV:C_INNER + (h + 1) * D_HV] = o.astype(o_ref.dtype)


def _ssd_gla(z, q, k, v, r, xbc, side_dt, side_g, g_lane, p):
    b, s, _ = z.shape
    q_len = C_CHUNK
    halo = 16
    nb = SSD_BATCHES if b % SSD_BATCHES == 0 else 1
    chunk = lambda width: pl.BlockSpec((nb, q_len, width), lambda i, n: (i, n, 0))
    full = lambda shape: pl.BlockSpec(shape, lambda i, n: (0,) * len(shape))
    pad_lanes = lambda vec: jnp.zeros((1, LANES), F32).at[0, :vec.shape[0]].set(vec.astype(F32))
    gkw = jnp.zeros((LANES, D_KEY), F32).at[g_lane:g_lane + D_GATE_RANK].set(p["gk_w"]).astype(BF16)
    expand = jnp.asarray(np.arange(C_INNER)[None, :] // C_HEAD_DIM == np.arange(LANES)[:, None], BF16)
    return pl.pallas_call(
        _ssd_gla_kernel,
        grid=(b // nb, s // q_len),
        in_specs=[
            chunk(C_INNER), chunk(D_KEY), chunk(D_KEY), chunk(D_VAL), chunk(D_VAL), chunk(C_CONV_DIM),
            pl.BlockSpec((nb, halo, C_CONV_DIM), lambda i, n: (i, jnp.maximum(n * (q_len // halo) - 1, 0), 0)),
            chunk(LANES), chunk(LANES),
            full((C_CONV, C_CONV_DIM)), full((1, C_CONV_DIM)),
            full((1, LANES)), full((1, LANES)), full((1, C_INNER)), full((LANES, C_INNER)),
            full((1, C_INNER)), full((LANES, D_KEY)), full((1, D_KEY)), full((1, D_HV)),
        ],
        out_specs=pl.BlockSpec((nb, q_len, C_INNER + D_VAL), lambda i, n: (i, n, 0)),
        out_shape=jax.ShapeDtypeStruct((b, s, C_INNER + D_VAL), BF16),
        scratch_shapes=[pltpu.VMEM((nb, C_INNER, C_STATE), F32),
                        pltpu.VMEM((nb, D_VAL, D_HK), F32)],
        compiler_params=_cparams(("parallel", "arbitrary")),
        name="ssd_gla",
    )(z, q, k, v, r, xbc, xbc, side_dt, side_g,
      p["conv_w"].astype(F32), p["conv_b"].reshape(1, -1).astype(F32),
      pad_lanes(p["dt_bias"]), pad_lanes(p["a_log"]),
      jnp.repeat(p["d_skip"].astype(F32), C_HEAD_DIM).reshape(1, C_INNER), expand,
      p["ssd_norm"].reshape(1, -1).astype(F32), gkw, p["gk_b"].reshape(1, -1).astype(F32),
      p["gla_norm"].reshape(1, -1).astype(F32))


def _store_rows_tiled(ref, val):
    m, d = val.shape
    nc = d // LANES
    for c in range(nc):
        ref[pl.ds(c, m, stride=nc), :] = val[:, c * LANES:(c + 1) * LANES]


def _load_rows_tiled(ref, m, dtype=None):
    nc = ref.shape[0] // m
    parts = [ref[pl.ds(c, m, stride=nc), :] for c in range(nc)]
    if dtype is not None:
        parts = [p.astype(dtype) for p in parts]
    return jnp.concatenate(parts, axis=1)


def _pack_pairs(x):
    n = x.shape[1] // 2
    u = pltpu.bitcast(x.astype(BF16).astype(F32), jnp.uint32)
    return (u[:, :n] >> 16) | (u[:, n:] & jnp.uint32(0xFFFF0000))


def _unpack_pairs(u):
    lo = pltpu.bitcast(u << 16, F32).astype(BF16)
    hi = pltpu.bitcast(u & jnp.uint32(0xFFFF0000), F32).astype(BF16)
    return jnp.concatenate([lo, hi], axis=1)


RT_GATE, RT_EXPERT, RT_RANK = 0, 2, 4


def _route_block(lg, carry, earlier):
    m = lg.shape[0]
    lane = lax.broadcasted_iota(jnp.int32, (m, LANES), 1)
    lane_f = lane.astype(F32)
    none = float(LANES)
    neg = -jnp.inf
    first_max = lambda v, vmax: jnp.min(jnp.where(v == vmax, lane_f, none), axis=-1, keepdims=True)
    gl = jnp.where(lane < MOE_GROUPS, lg, neg)
    gmax = jnp.max(gl, axis=-1, keepdims=True)
    g_w = 1.0 / jnp.sum(jnp.exp(gl - gmax), axis=-1, keepdims=True)
    lo = MOE_GROUPS + first_max(gl, gmax) * MOE_EPG
    el = jnp.where((lane_f >= lo) & (lane_f < lo + MOE_EPG), lg, neg)
    emax = jnp.max(el, axis=-1, keepdims=True)
    esum = jnp.sum(jnp.exp(el - emax), axis=-1, keepdims=True)
    l0 = first_max(el, emax)
    el2 = jnp.where(lane_f == l0, neg, el)
    emax2 = jnp.max(el2, axis=-1, keepdims=True)
    l1 = first_max(el2, emax2)
    p0 = 1.0 / esum
    p1 = jnp.exp(emax2 - emax) / esum
    w0 = g_w * (p0 / (p0 + p1))
    w1 = g_w * (p1 / (p0 + p1))
    oh0 = lane_f == l0
    oh1 = lane_f == l1
    oh = (oh0 | oh1).astype(BF16)
    cum = jnp.dot(earlier, oh, preferred_element_type=F32) + carry
    rank0 = jnp.sum(jnp.where(oh0, cum, 0.0), axis=-1, keepdims=True)
    rank1 = jnp.sum(jnp.where(oh1, cum, 0.0), axis=-1, keepdims=True)
    carry = carry + jnp.sum(oh.astype(F32), axis=0, keepdims=True)
    rec = jnp.zeros((m, LANES), F32)
    for pos, val in ((RT_GATE, w0), (RT_GATE + 1, w1), (RT_EXPERT, l0 - MOE_GROUPS),
                     (RT_EXPERT + 1, l1 - MOE_GROUPS), (RT_RANK, rank0), (RT_RANK + 1, rank1)):
        rec = jnp.where(lane == pos, val, rec)
    return rec, carry


def _out_proj_kernel(n_parts, *refs):
    a_refs = refs[:n_parts]
    w_refs = refs[n_parts:2 * n_parts]
    h_ref, g_ref, wr_ref, br_ref, tri_ref, ho_ref, xt_ref, rt_ref, cnt_ref, carry_ref = refs[2 * n_parts:]

    @pl.when(pl.program_id(0) == 0)
    def _():
        carry_ref[...] = jnp.zeros_like(carry_ref)

    acc = h_ref[...]
    for a_ref, w_ref in zip(a_refs, w_refs):
        acc = acc + jnp.dot(a_ref[...], w_ref[...], preferred_element_type=F32)
    ho_ref[...] = acc
    xn = _rms(acc, g_ref[...])
    xt_ref[...] = xn.astype(xt_ref.dtype)
    x_hi, x_mid, _ = _split3(xn)
    wr = wr_ref[...]
    lg2 = jnp.dot(x_hi, wr, preferred_element_type=F32)
    lg = (lg2[:, :LANES] + lg2[:, LANES:] + jnp.dot(x_mid, wr[:, :LANES], preferred_element_type=F32)
          + br_ref[...])
    rec, carry = _route_block(lg, carry_ref[...], tri_ref[...])
    rt_ref[...] = rec
    carry_ref[...] = carry
    cnt_ref[0] = carry


def _out_proj(parts, w_parts, h, g, w_route, b_route, *, tm=512):
    t, d = h.shape
    tm = min(tm, t)
    nc = d // LANES
    row = lambda i: (i, 0)
    const = lambda i: (0, 0)
    in_specs = [pl.BlockSpec((tm, a.shape[1]), row) for a in parts]
    in_specs += [pl.BlockSpec(w.shape, const) for w in w_parts]
    in_specs += [pl.BlockSpec((tm, d), row), pl.BlockSpec((1, d), const),
                 pl.BlockSpec((d, 2 * LANES), const), pl.BlockSpec((1, LANES), const),
                 pl.BlockSpec((tm, tm), const)]
    earlier = jnp.asarray(np.tril(np.ones((tm, tm), np.float32), -1), BF16)
    return pl.pallas_call(
        functools.partial(_out_proj_kernel, len(parts)),
        grid=(t // tm,),
        in_specs=in_specs,
        out_specs=[pl.BlockSpec((tm, d), row), pl.BlockSpec((tm, d), row),
                   pl.BlockSpec((tm, LANES), row), pl.BlockSpec((1, 1, LANES), lambda i: (i, 0, 0))],
        out_shape=[jax.ShapeDtypeStruct((t, d), F32), jax.ShapeDtypeStruct((t, d), BF16),
                   jax.ShapeDtypeStruct((t, LANES), F32), jax.ShapeDtypeStruct((t // tm, 1, LANES), F32)],
        scratch_shapes=[pltpu.VMEM((1, LANES), F32)],
        compiler_params=_cparams(("arbitrary",)),
        name="out_proj",
    )(*parts, *w_parts, h, g.reshape(1, d), w_route, b_route, earlier)


LONG_RUN = 128


def _piece_sizes(max_rows, lo=1, hi=None):
    sizes = [1 << b for b in reversed(range(max_rows.bit_length()))]
    return [n for n in sizes if n >= lo and (hi is None or n < hi)]


def _for_each_piece(run, sizes, body):
    off = run // (2 * sizes[0]) * (2 * sizes[0])
    for n in sizes:
        hit = (run & n) != 0
        pl.when(hit)(functools.partial(body, off, n))
        off = off + jnp.where(hit, n, 0)


def _for_each_run(len_ref, tile, long_ref, max_rows, make_body):
    def sweep(sizes):
        def per_expert(e, first):
            run = len_ref[tile * MOE_EXPERTS + e]
            _for_each_piece(run, sizes, make_body(e, first))
            return first + run

        lax.fori_loop(0, MOE_EXPERTS, per_expert, jnp.int32(0))

    sweep(_piece_sizes(max_rows, hi=LONG_RUN))
    pl.when(long_ref[tile] != 0)(functools.partial(sweep, _piece_sizes(max_rows, lo=LONG_RUN)))


def _zero_fill_rows(rows_ref, z_ref, zsem, lo_ref, hi_ref, n_rows, nc):
    zb = z_ref.shape[0] // nc
    z_ref[...] = jnp.zeros_like(z_ref)
    assert MOE_TILE // 2 <= zb and MOE_TILE % zb == 0

    def piece(row0, n):
        return pltpu.make_async_copy(z_ref.at[pl.ds(0, n * nc)], rows_ref.at[pl.ds(row0 * nc, n * nc)], zsem)

    def sweep(issue):
        def per_expert(e, carry):
            lo = lo_ref[e]

            def one(off, n):
                piece(lo + off, n).start() if issue else piece(lo + off, n).wait()

            _for_each_piece(hi_ref[e] - lo, _piece_sizes(MOE_TILE // 2), one)
            return carry

        def per_block(i, carry):
            piece(i * zb, zb).start() if issue else piece(i * zb, zb).wait()
            return carry

        lax.fori_loop(0, MOE_EXPERTS, per_expert, 0)
        lax.fori_loop(hi_ref[MOE_EXPERTS - 1] // zb, n_rows // zb, per_block, 0)

    sweep(True)
    sweep(False)


def _tile_positions(rt, pos_base):
    lane_f = lax.broadcasted_iota(jnp.int32, rt.shape, 1).astype(F32)
    out = []
    for k in range(MOE_TOPK):
        e_lane = rt[:, RT_EXPERT + k:RT_EXPERT + k + 1] + MOE_GROUPS
        out.append(rt[:, RT_RANK + k:RT_RANK + k + 1]
                   + jnp.sum(jnp.where(lane_f == e_lane, pos_base, 0.0), axis=-1, keepdims=True))
    return out


def _moe_dispatch_kernel(dst_ref, len_ref, long_ref, lo_ref, hi_ref, x_ref, rt_ref, pb_ref, rows_ref,
                         sbuf, z_ref, sem, zsem, *, tm, nc, n_steps, n_rows):
    i = pl.program_id(0)
    rows = MOE_TOPK * tm
    slot = lax.rem(i, 2)

    def wait_slot(s):
        pltpu.make_async_copy(sbuf.at[s], rows_ref.at[pl.ds(0, rows * nc)], sem.at[s]).wait()

    @pl.when(i == 0)
    def _():
        _zero_fill_rows(rows_ref, z_ref, zsem, lo_ref, hi_ref, n_rows, nc)

    @pl.when(i >= 2)
    def _():
        wait_slot(slot)

    pos = _tile_positions(rt_ref[...], pb_ref[0])
    lane = lax.broadcasted_iota(jnp.int32, (tm, LANES), 1)
    pos_t = jnp.where(lane == 0, pos[0], jnp.where(lane == 1, pos[1], 0.0)).T
    p_iota = lax.broadcasted_iota(jnp.int32, (rows, tm), 0).astype(F32)
    place = ((p_iota == pos_t[0:1, :]) | (p_iota == pos_t[1:2, :])).astype(BF16)
    _store_rows_tiled(sbuf.at[slot], _pack_pairs(jnp.dot(place, x_ref[...], preferred_element_type=F32)))

    def sender(e, src):
        dst = dst_ref[i * MOE_EXPERTS + e]

        def send(off, n):
            pltpu.make_async_copy(sbuf.at[slot, pl.ds((src + off) * nc, n * nc)],
                                  rows_ref.at[pl.ds((dst + off) * nc, n * nc)], sem.at[slot]).start()

        return send

    _for_each_run(len_ref, i, long_ref, rows, sender)

    @pl.when(i == n_steps - 1)
    def _():
        wait_slot(slot)
        if n_steps > 1:
            wait_slot(1 - slot)


def _moe_dispatch(x, route, pos_base, seg_dst, seg_len, seg_long, pad_lo, pad_hi, n_rows, *, tm):
    t, d = x.shape
    nc = d // (2 * LANES)
    n_steps = t // tm
    row = lambda i, *_: (i, 0)
    return pl.pallas_call(
        functools.partial(_moe_dispatch_kernel, tm=tm, nc=nc, n_steps=n_steps, n_rows=n_rows),
        grid_spec=pltpu.PrefetchScalarGridSpec(
            num_scalar_prefetch=5, grid=(n_steps,),
            in_specs=[pl.BlockSpec((tm, d), row), pl.BlockSpec((tm, LANES), row),
                      pl.BlockSpec((1, 1, LANES), lambda i, *_: (i, 0, 0))],
            out_specs=pl.BlockSpec(memory_space=pl.ANY),
            scratch_shapes=[pltpu.VMEM((2, MOE_TOPK * tm * nc, LANES), jnp.uint32),
                            pltpu.VMEM((MOE_TILE // 2 * nc, LANES), jnp.uint32),
                            pltpu.SemaphoreType.DMA((2,)), pltpu.SemaphoreType.DMA(())]),
        out_shape=jax.ShapeDtypeStruct((n_rows * nc, LANES), jnp.uint32),
        compiler_params=pltpu.CompilerParams(dimension_semantics=("arbitrary",), has_side_effects=True,
                                             vmem_limit_bytes=VMEM_LIMIT),
        name="moe_dispatch",
    )(seg_dst, seg_len, seg_long, pad_lo, pad_hi, x, route, pos_base)


def _moe_kernel(te_ref, nu_ref, first_ref, slot_ref, next_ref, x_ref, wg_hbm, wu_hbm, wd_hbm, y_ref,
                wg_buf, wu_buf, wd_buf, sem, *, layer):
    i = pl.program_id(0)
    live = i < nu_ref[0]
    pairs = ((wg_hbm, wg_buf), (wu_hbm, wu_buf), (wd_hbm, wd_buf))

    def weights(e, s):
        return [pltpu.make_async_copy(hbm.at[layer, e], buf.at[s], sem.at[s, k])
                for k, (hbm, buf) in enumerate(pairs)]

    @pl.when(i == 0)
    def _():
        for cp in weights(te_ref[0], 0):
            cp.start()

    @pl.when(live & (first_ref[i] != 0))
    def _():
        s = slot_ref[i]
        for cp in weights(te_ref[i], s):
            cp.wait()

        @pl.when(next_ref[i] >= 0)
        def _():
            for cp in weights(next_ref[i], 1 - s):
                cp.start()

    @pl.when(i >= nu_ref[0])
    def _():
        y_ref[...] = jnp.zeros_like(y_ref)

    @pl.when(live)
    def _():
        s = slot_ref[i]
        x = _unpack_pairs(_load_rows_tiled(x_ref, MOE_TILE))
        gate = jnp.dot(x, wg_buf[s].astype(BF16), preferred_element_type=F32)
        up = jnp.dot(x, wu_buf[s].astype(BF16), preferred_element_type=F32)
        act = (jax.nn.silu(gate) * up).astype(BF16)
        y = jnp.dot(act, wd_buf[s].astype(BF16), preferred_element_type=F32)
        _store_rows_tiled(y_ref, _pack_pairs(y))


def _moe_experts(x_rows, tile_expert, n_used, w_gate, w_up, w_down, layer):
    d, ff = w_gate.shape[-2:]
    nc = d // (2 * LANES)
    n_tiles = x_rows.shape[0] // (MOE_TILE * nc)
    tiles = jnp.arange(n_tiles, dtype=jnp.int32)
    used = tiles < n_used[0]
    te = jnp.where(used, tile_expert, MOE_EXPERTS)
    first = (used & (te != jnp.concatenate([jnp.full((1,), -1, jnp.int32), te[:-1]]))).astype(jnp.int32)
    slot = ((jnp.cumsum(first) - 1) % 2).astype(jnp.int32)
    after = jnp.searchsorted(te, te, side="right")
    nxt = jnp.where(after < n_used[0], te[jnp.minimum(after, n_tiles - 1)], -1).astype(jnp.int32)
    live = lambda i, nu: jnp.minimum(i, nu[0] - 1)
    any_spec = pl.BlockSpec(memory_space=pl.ANY)
    return pl.pallas_call(
        functools.partial(_moe_kernel, layer=layer),
        grid_spec=pltpu.PrefetchScalarGridSpec(
            num_scalar_prefetch=5, grid=(n_tiles,),
            in_specs=[pl.BlockSpec((MOE_TILE * nc, LANES), lambda i, te, nu, *_: (live(i, nu), 0)),
                      any_spec, any_spec, any_spec],
            out_specs=pl.BlockSpec((MOE_TILE * nc, LANES), lambda i, *_: (i, 0)),
            scratch_shapes=[pltpu.VMEM((2, d, ff), F32), pltpu.VMEM((2, d, ff), F32), pltpu.VMEM((2, ff, d), F32),
                            pltpu.SemaphoreType.DMA((2, 3))]),
        out_shape=jax.ShapeDtypeStruct(x_rows.shape, x_rows.dtype),
        compiler_params=_cparams(("arbitrary",)),
        name="moe_experts",
    )(tile_expert, n_used, first, slot, nxt, x_rows, w_gate, w_up, w_down)


def _moe_combine_kernel(src_ref, len_ref, long_ref, h_ref, rt_ref, pb_ref, g_ref, y_hbm, o_ref, ybuf, sem,
                        *, tm, nc, final):
    i = pl.program_id(0)
    n_steps = pl.num_programs(0)
    rows = MOE_TOPK * tm

    def fetch(tile, slot):
        def receiver(e, dst):
            src = src_ref[tile * MOE_EXPERTS + e]

            def recv(off, n):
                pltpu.make_async_copy(y_hbm.at[pl.ds((src + off) * nc, n * nc)],
                                      ybuf.at[slot, pl.ds((dst + off) * nc, n * nc)], sem.at[slot]).start()

            return recv

        _for_each_run(len_ref, tile, long_ref, rows, receiver)

    slot = lax.rem(i, 2)

    @pl.when(i == 0)
    def _():
        fetch(0, 0)

    @pl.when(i + 1 < n_steps)
    def _():
        fetch(i + 1, 1 - slot)

    pltpu.make_async_copy(y_hbm.at[pl.ds(0, rows * nc)], ybuf.at[slot], sem.at[slot]).wait()
    y = _unpack_pairs(_load_rows_tiled(ybuf.at[slot], rows))
    rt = rt_ref[...]
    pos_f = lax.broadcasted_iota(jnp.int32, (tm, rows), 1).astype(F32)
    pick = jnp.zeros((tm, rows), F32)
    for k, pos in enumerate(_tile_positions(rt, pb_ref[0])):
        pick = jnp.where(pos_f == pos, rt[:, RT_GATE + k:RT_GATE + k + 1], pick)
    out = h_ref[...] + jnp.dot(pick.astype(BF16), y, preferred_element_type=F32)
    o_ref[...] = _rms(out, g_ref[...]) if final else out


def _moe_combine_rows(h, y_rows, route, seg_src, seg_len, seg_long, pos_base, g, *, tm, final):
    t, d = h.shape
    nc = d // (2 * LANES)
    row = lambda i, *_: (i, 0)
    return pl.pallas_call(
        functools.partial(_moe_combine_kernel, tm=tm, nc=nc, final=final),
        grid_spec=pltpu.PrefetchScalarGridSpec(
            num_scalar_prefetch=3, grid=(t // tm,),
            in_specs=[pl.BlockSpec((tm, d), row), pl.BlockSpec((tm, LANES), row),
                      pl.BlockSpec((1, 1, LANES), lambda i, *_: (i, 0, 0)),
                      pl.BlockSpec((1, d), lambda i, *_: (0, 0)),
                      pl.BlockSpec(memory_space=pl.ANY)],
            out_specs=pl.BlockSpec((tm, d), row),
            scratch_shapes=[pltpu.VMEM((2, MOE_TOPK * tm * nc, LANES), y_rows.dtype),
                            pltpu.SemaphoreType.DMA((2,))]),
        out_shape=jax.ShapeDtypeStruct((t, d), F32),
        compiler_params=_cparams(("arbitrary",)),
        name="moe_combine",
    )(seg_src, seg_len, seg_long, h, route, pos_base, g.reshape(1, d), y_rows)


def _moe(h, x, route, tile_counts, w_gate, w_up, w_down, layer, g, *, final):
    t = route.shape[0]
    n_tt = tile_counts.shape[0]
    tm = t // n_tt
    after = tile_counts[:, 0, MOE_GROUPS:MOE_GROUPS + MOE_EXPERTS].astype(jnp.int32)
    before = jnp.concatenate([jnp.zeros((1, MOE_EXPERTS), jnp.int32), after[:-1]], axis=0)
    cnt = after[-1]
    padded = (cnt + MOE_TILE - 1) // MOE_TILE * MOE_TILE
    pad_ends = jnp.cumsum(padded)
    starts = (pad_ends - padded).astype(jnp.int32)
    n_tiles = (t * MOE_TOPK + MOE_EXPERTS * (MOE_TILE - 1)) // MOE_TILE
    tile_start = jnp.arange(n_tiles, dtype=jnp.int32) * MOE_TILE
    tile_expert = jnp.minimum(jnp.sum(tile_start[:, None] >= pad_ends[None, :], axis=1),
                              MOE_EXPERTS - 1).astype(jnp.int32)
    n_used = (pad_ends[-1] // MOE_TILE).astype(jnp.int32).reshape(1)
    n_rows = n_tiles * MOE_TILE
    seg_len = (after - before).reshape(-1)
    seg_off = jnp.cumsum(after - before, axis=1) - (after - before)
    seg_row = (starts[None, :] + before).reshape(-1)
    pos_base = jnp.zeros((n_tt, 1, LANES), F32).at[:, 0, MOE_GROUPS:MOE_GROUPS + MOE_EXPERTS].set(
        (seg_off - before).astype(F32))
    seg_long = jnp.any(after - before >= LONG_RUN, axis=1).astype(jnp.int32)
    x_rows = _moe_dispatch(x, route, pos_base, seg_row, seg_len, seg_long, starts + cnt,
                           pad_ends.astype(jnp.int32), n_rows, tm=tm)
    y_rows = _moe_experts(x_rows, tile_expert, n_used, w_gate, w_up, w_down, layer)
    return _moe_combine_rows(h, y_rows, route, seg_row, seg_len, seg_long, pos_base, g, tm=tm, final=final)


def _router_weights(w_group, b_group, w_router, b_router):
    d = w_group.shape[0]
    w = jnp.zeros((d, LANES), F32)
    w = w.at[:, :MOE_GROUPS].set(w_group).at[:, MOE_GROUPS:MOE_GROUPS + MOE_EXPERTS].set(w_router)
    b = jnp.zeros((1, LANES), F32)
    b = b.at[0, :MOE_GROUPS].set(b_group).at[0, MOE_GROUPS:MOE_GROUPS + MOE_EXPERTS].set(b_router)
    w_hi = w.astype(BF16)
    w_mid = (w - w_hi.astype(F32)).astype(BF16)
    return jnp.concatenate([w_hi, w_mid], axis=1), b


def kernel(x, norm_mix, norm_moe, norm_final, even_w_in, even_sinks, even_forget_bias, even_w_out,
           odd_w_in, odd_conv_w, odd_conv_b, odd_dt_bias, odd_a_log, odd_d_skip, odd_ssd_norm,
           odd_gk_w, odd_gk_b, odd_gla_norm, odd_w_out, moe_w_group, moe_b_group, moe_w_router,
           moe_b_router, moe_w_gate, moe_w_up, moe_w_down):
    b, s, d = x.shape
    t = b * s
    depth = norm_mix.shape[0]
    h = x.reshape(t, d)
    for layer in range(depth):
        i = layer // 2
        if layer % 2 == 0:
            w = even_w_in.astype(BF16)[i]
            n_ab = (A_Q_HEADS + 2 * A_KV_HEADS + 3 * B_HEADS) * HEAD_DIM
            w_main = w[:, :n_ab]
            w_aux = jnp.pad(w[:, n_ab:], ((0, 0), (0, LANES - B_HEADS)))
            n_a, kv_a, n_b = (A_Q_HEADS + 2 * A_KV_HEADS) * HEAD_DIM, A_KV_HEADS * HEAD_DIM, B_HEADS * HEAD_DIM
            swa_rows, fox_rows = SWA_QBLOCKS * A_WINDOW, 256
            proj, f_aux, va_t, vb_t = _norm_proj(
                h, norm_mix[layer], [w_main, w_aux], ((0, 0, n_ab, BF16), (1, 0, LANES, F32)), tm=1024,
                t_plan=((0, n_a - kv_a, kv_a, BF16, swa_rows), (0, n_ab - n_b, n_b, BF16, fox_rows)))
            proj = proj.reshape(b, s, -1)
            out_a = _swa(proj, va_t.reshape(b, s // swa_rows, kv_a, swa_rows), even_sinks[i])
            c, ct = _fox_gate(f_aux.reshape(b, s, LANES), even_forget_bias[i])
            out_b = _fox(proj, vb_t.reshape(b, s // fox_rows, n_b, fox_rows), c, ct, tk=fox_rows)
            n_ha = A_Q_HEADS * HEAD_DIM
            w_out = even_w_out[i].astype(BF16)
            parts = [out_a.reshape(t, -1), out_b.reshape(t, -1)]
            w_parts = [w_out[:n_ha], w_out[n_ha:]]
        else:
            w = odd_w_in.astype(BF16)[i]
            o_z, o_xbc = 0, C_INNER
            o_dt = o_xbc + C_CONV_DIM
            o_q = o_dt + C_HEADS
            o_k = o_q + D_KEY
            o_v = o_k + D_KEY
            o_g = o_v + D_VAL
            o_r = o_g + D_GATE_RANK
            n_b = w.shape[1] - o_dt
            w_a = w[:, :o_dt]
            w_b = jnp.pad(w[:, o_dt:], ((0, 0), (0, -n_b % LANES)))
            g_win = (o_g - o_dt) // LANES * LANES
            plan = ((0, o_z, C_INNER, BF16), (1, o_q - o_dt, D_KEY, BF16), (1, o_k - o_dt, D_KEY, BF16),
                    (1, o_v - o_dt, D_VAL, BF16), (1, o_r - o_dt, D_VAL, BF16), (0, o_xbc, C_CONV_DIM, BF16),
                    (1, 0, LANES, F32), (1, g_win, LANES, F32))
            outs = _norm_proj(h, norm_mix[layer], [w_a, w_b], plan, tm=512)
            params = dict(conv_w=odd_conv_w[i], conv_b=odd_conv_b[i], dt_bias=odd_dt_bias[i], a_log=odd_a_log[i],
                          d_skip=odd_d_skip[i], ssd_norm=odd_ssd_norm[i], gk_w=odd_gk_w[i], gk_b=odd_gk_b[i],
                          gla_norm=odd_gla_norm[i])
            mixed = _ssd_gla(*[o.reshape(b, s, -1) for o in outs], o_g - o_dt - g_win, params)
            parts = [mixed.reshape(t, -1)]
            w_parts = [odd_w_out[i].astype(BF16)]
        w_route, b_route = _router_weights(moe_w_group[layer], moe_b_group[layer],
                                           moe_w_router[layer], moe_b_router[layer])
        h, x_tiled, route, tile_counts = _out_proj(parts, w_parts, h, norm_moe[layer], w_route, b_route)
        h = _moe(h, x_tiled, route, tile_counts, moe_w_gate, moe_w_up, moe_w_down, layer, norm_final,
                 final=layer == depth - 1)
    out = h
    return out.reshape(b, s, d)
```

```python
import functools

import numpy as np
import jax
import jax.numpy as jnp
from jax import lax
from jax.experimental import pallas as pl
from jax.experimental.pallas import tpu as pltpu

F32 = jnp.float32
BF16 = jnp.bfloat16

RMS_EPS = 1e-6
HEAD_DIM = 64
A_Q_HEADS = 8
A_KV_HEADS = 2
A_GROUP = A_Q_HEADS // A_KV_HEADS
A_WINDOW = 128
B_HEADS = 8
C_HEADS = 16
C_HEAD_DIM = 64
C_INNER = C_HEADS * C_HEAD_DIM
C_GROUPS = 2
C_HPG = C_HEADS // C_GROUPS
C_STATE = 128
C_CONV = 4
C_CHUNK = 128
C_CONV_DIM = C_INNER + 2 * C_GROUPS * C_STATE
D_HEADS = 4
D_HK = 128
D_HV = 256
D_KEY = D_HEADS * D_HK
D_VAL = D_HEADS * D_HV
D_GATE_RANK = 16
D_GATE_NORM = 16.0
D_CHUNK = 64
MOE_GROUPS = 4
MOE_EPG = 8
MOE_EXPERTS = MOE_GROUPS * MOE_EPG
MOE_TOPK = 2

LANES = 128
VMEM_LIMIT = 48 * 1024 * 1024
MOE_TILE = 512


def _cparams(sem):
    return pltpu.CompilerParams(dimension_semantics=sem, vmem_limit_bytes=VMEM_LIMIT)


def _rms(x, g):
    ms = jnp.mean(x * x, axis=-1, keepdims=True)
    return x * lax.rsqrt(ms + RMS_EPS) * g


def _norm_proj_kernel(n_w, plan, t_plan, *refs):
    x_ref, g_ref = refs[:2]
    w_refs = refs[2:2 + n_w]
    o_refs = refs[2 + n_w:2 + n_w + len(plan)]
    t_refs = refs[2 + n_w + len(plan):2 + n_w + len(plan) + len(t_plan)]
    res_refs = refs[2 + n_w + len(plan) + len(t_plan):]
    xn = _rms(x_ref[...], g_ref[...]).astype(BF16)
    for w_ref, res_ref in zip(w_refs, res_refs):
        res_ref[...] = jnp.dot(xn, w_ref[...], preferred_element_type=F32)
    for o_ref, (wi, start, width, _) in zip(o_refs, plan):
        o_ref[...] = res_refs[wi][:, start:start + width].astype(o_ref.dtype)
    for o_ref, (wi, start, width, _, rows) in zip(t_refs, t_plan):
        for u in range(o_ref.shape[0]):
            o_ref[u] = res_refs[wi][u * rows:(u + 1) * rows, start:start + width].T.astype(o_ref.dtype)


def _norm_proj(x, g, weights, plan, *, tm, t_plan=()):
    t, d = x.shape
    tm = min(tm, t)
    row = lambda i: (i, 0)
    const = lambda i: (0, 0)
    in_specs = [pl.BlockSpec((tm, d), row), pl.BlockSpec((1, d), const)]
    in_specs += [pl.BlockSpec(w.shape, const, pipeline_mode=pl.Buffered(1)) for w in weights]
    return pl.pallas_call(
        functools.partial(_norm_proj_kernel, len(weights), plan, t_plan),
        grid=(t // tm,),
        in_specs=in_specs,
        out_specs=([pl.BlockSpec((tm, width), row) for _, _, width, _ in plan]
                   + [pl.BlockSpec((tm // rows, width, rows), lambda i: (i, 0, 0))
                      for _, _, width, _, rows in t_plan]),
        out_shape=([jax.ShapeDtypeStruct((t, width), dtype) for _, _, width, dtype in plan]
                   + [jax.ShapeDtypeStruct((t // rows, width, rows), dtype) for _, _, width, dtype, rows in t_plan]),
        scratch_shapes=[pltpu.VMEM((tm, w.shape[1]), F32) for w in weights],
        compiler_params=_cparams(("parallel",)),
        name="norm_proj",
    )(x, g.reshape(1, d), *weights)


SWA_QBLOCKS = 8


def _swa_kernel(sink_ref, slope_ref, q_ref, kp_ref, kc_ref, vp_ref, vc_ref, o_ref):
    n = pl.program_id(1)
    blk = A_WINDOW
    wide = A_GROUP * blk
    key = lax.broadcasted_iota(jnp.int32, (2 * blk, wide), 0)
    qry = lax.broadcasted_iota(jnp.int32, (2 * blk, wide), 1) % blk
    dist = blk + qry - key
    in_window = (dist >= 0) & (dist < A_WINDOW)
    distf = dist.astype(F32)
    nt = (((1,), (1,)), ((), ()))
    k_all = jnp.concatenate([kp_ref[0], kc_ref[0]], axis=0)
    v_all = jnp.concatenate([vp_ref[0], vc_ref[0]], axis=1)
    units = [(j, kh) for j in range(SWA_QBLOCKS) for kh in range(A_KV_HEADS)]
    scores = []
    for j, kh in units:
        k = k_all[j * blk:(j + 2) * blk, kh * HEAD_DIM:(kh + 1) * HEAD_DIM]
        q = jnp.concatenate([q_ref[0, j * blk:(j + 1) * blk,
                                   (kh * A_GROUP + g) * HEAD_DIM:(kh * A_GROUP + g + 1) * HEAD_DIM]
                             for g in range(A_GROUP)], axis=0)
        scores.append(lax.dot_general(k, q, nt, preferred_element_type=F32))
    probs = []
    for (j, kh), s in zip(units, scores):
        valid = in_window & ((key >= blk) | (n * SWA_QBLOCKS + j > 0))
        s = s * (HEAD_DIM ** -0.5) - slope_ref[kh:kh + 1, :] * distf
        s = jnp.where(valid, s, -jnp.inf)
        sink = sink_ref[kh:kh + 1, :]
        m = jnp.maximum(jnp.max(s, axis=0, keepdims=True), sink)
        p = jnp.exp(s - m)
        probs.append((p.astype(BF16), jnp.sum(p, axis=0, keepdims=True) + jnp.exp(sink - m)))
    for j in range(SWA_QBLOCKS):
        outs = []
        for kh in range(A_KV_HEADS):
            p, denom = probs[j * A_KV_HEADS + kh]
            v_t = v_all[kh * HEAD_DIM:(kh + 1) * HEAD_DIM, j * blk:(j + 2) * blk]
            o_t = jnp.dot(v_t, p, preferred_element_type=F32) / denom
            outs += [o_t[:, g * blk:(g + 1) * blk] for g in range(A_GROUP)]
        o_ref[0, j * blk:(j + 1) * blk, :] = jnp.concatenate(outs, axis=0).T.astype(o_ref.dtype)


def _swa(proj, v_t, sinks):
    b, s, _ = proj.shape
    blk = A_WINDOW
    qw = A_Q_HEADS * HEAD_DIM
    kw = A_KV_HEADS * HEAD_DIM
    k_blk = qw // kw
    per_lane = lambda vec: jnp.repeat(vec.astype(F32), blk).reshape(A_KV_HEADS, A_GROUP * blk)
    slopes = jnp.asarray(2.0 ** (-8.0 * np.arange(1, A_Q_HEADS + 1) / A_Q_HEADS), F32)
    tq = SWA_QBLOCKS * blk
    assert v_t.shape == (b, s // tq, kw, tq)
    prev = lambda n: jnp.maximum(n * SWA_QBLOCKS - 1, 0)
    full = pl.BlockSpec((A_KV_HEADS, A_GROUP * blk), lambda i, n: (0, 0))
    return pl.pallas_call(
        _swa_kernel,
        grid=(b, s // tq),
        in_specs=[
            full, full,
            pl.BlockSpec((1, tq, qw), lambda i, n: (i, n, 0)),
            pl.BlockSpec((1, blk, kw), lambda i, n: (i, prev(n), k_blk)),
            pl.BlockSpec((1, tq, kw), lambda i, n: (i, n, k_blk)),
            pl.BlockSpec((1, None, kw, blk), lambda i, n: (i, jnp.maximum(n - 1, 0), 0, SWA_QBLOCKS - 1)),
            pl.BlockSpec((1, None, kw, tq), lambda i, n: (i, n, 0, 0)),
        ],
        out_specs=pl.BlockSpec((1, tq, qw), lambda i, n: (i, n, 0)),
        out_shape=jax.ShapeDtypeStruct((b, s, qw), BF16),
        compiler_params=_cparams(("parallel", "parallel")),
        name="swa",
    )(per_lane(sinks), per_lane(slopes), proj, proj, proj, v_t, v_t)


def _tril(n, dtype=F32):
    r = lax.broadcasted_iota(jnp.int32, (n, n), 0)
    c = lax.broadcasted_iota(jnp.int32, (n, n), 1)
    return (c <= r).astype(dtype)


def _split3(x):
    hi = x.astype(BF16)
    r = x - hi.astype(F32)
    mid = r.astype(BF16)
    return hi, mid, (r - mid.astype(F32)).astype(BF16)


def _dot_mask_lhs(mask, x):
    return sum(jnp.dot(mask, part, preferred_element_type=F32) for part in _split3(x))


def _dot_mask_rhs(x, mask):
    return sum(jnp.dot(part, mask, preferred_element_type=F32) for part in _split3(x))


def _fox_gate_kernel(f_ref, b_ref, c_ref, ct_ref):
    tri = _tril(LANES, BF16)
    carry = jnp.zeros((1, LANES), F32)
    for n in range(f_ref.shape[1] // LANES):
        rows = slice(n * LANES, (n + 1) * LANES)
        lf = jax.nn.log_sigmoid(f_ref[0, rows, :] + b_ref[...])
        cs = _dot_mask_lhs(tri, lf) + carry
        carry = cs[LANES - 1:LANES, :]
        c_ref[0, rows, :] = cs
        ct_ref[0, n] = cs.T[:B_HEADS, :]


def _fox_gate(f_aux, bias):
    b, s, _ = f_aux.shape
    nb = s // LANES
    bias_p = jnp.zeros((1, LANES), F32).at[0, :B_HEADS].set(bias.astype(F32))
    return pl.pallas_call(
        _fox_gate_kernel,
        grid=(b,),
        in_specs=[pl.BlockSpec((1, s, LANES), lambda i: (i, 0, 0)),
                  pl.BlockSpec((1, LANES), lambda i: (0, 0))],
        out_specs=[pl.BlockSpec((1, s, LANES), lambda i: (i, 0, 0)),
                   pl.BlockSpec((1, nb, B_HEADS, LANES), lambda i: (i, 0, 0, 0))],
        out_shape=[jax.ShapeDtypeStruct((b, s, LANES), F32),
                   jax.ShapeDtypeStruct((b, nb, B_HEADS, LANES), F32)],
        compiler_params=_cparams(("parallel",)),
        name="fox_gate",
    )(f_aux, bias_p)


def _fox_kernel(q0_ref, q1_ref, k0_ref, k1_ref, vt_ref, c_ref, ctq_ref, o_ref, *, tq, tk, heads_per_step):
    qi = pl.program_id(1)
    sub = tq // LANES
    key = lax.broadcasted_iota(jnp.int32, (tk, tq), 0)
    qry = lax.broadcasted_iota(jnp.int32, (tk, tq), 1)
    per_q = tq // tk
    causal = [u * tk + key <= qry for u in range(per_q)]
    nt = (((1,), (1,)), ((), ()))
    half = B_HEADS // 2
    q_refs, k_refs = (q0_ref, q1_ref), (k0_ref, k1_ref)
    outs = []
    for h0 in range(0, B_HEADS, heads_per_step):
        heads = list(range(h0, h0 + heads_per_step))
        hsl = [slice(h * HEAD_DIM, (h + 1) * HEAD_DIM) for h in heads]
        lsl = [slice((h % half) * HEAD_DIM, (h % half + 1) * HEAD_DIM) for h in heads]
        qs = [q_refs[h // half][0, :, ls] * (HEAD_DIM ** -0.5)
              for h, ls in zip(heads, lsl)]
        cqs = [jnp.concatenate([ctq_ref[0, u, h:h + 1, :] for u in range(sub)], axis=1) for h in heads]

        def step(j, carry, mask, heads=heads, hsl=hsl, lsl=lsl, qs=qs, cqs=cqs):
            start = pl.multiple_of(j * tk, tk)
            sts = [lax.dot_general(k_refs[h // half][0, pl.ds(start, tk), ls], q, nt,
                                   preferred_element_type=F32)
                   for h, ls, q in zip(heads, lsl, qs)]
            ps, stats = [], []
            for idx, h in enumerate(heads):
                m, l, _ = carry[3 * idx:3 * idx + 3]
                ck = c_ref[0, pl.ds(start, tk), h:h + 1]
                st = (sts[idx] - ck) + cqs[idx]
                if mask is not None:
                    st = jnp.where(mask, st, -jnp.inf)
                m_new = jnp.maximum(m, jnp.max(st, axis=0, keepdims=True))
                alpha = jnp.exp(m - m_new)
                p = jnp.exp(st - m_new)
                stats.append((m_new, alpha, alpha * l + jnp.sum(p, axis=0, keepdims=True)))
                ps.append(p.astype(BF16))
            new = []
            for idx in range(len(heads)):
                m_new, alpha, l = stats[idx]
                pv = jnp.dot(vt_ref[0, j, hsl[idx], :], ps[idx], preferred_element_type=F32)
                new += [m_new, l, alpha * carry[3 * idx + 2] + pv]
            return tuple(new)

        init = (jnp.full((1, tq), -jnp.inf, F32), jnp.zeros((1, tq), F32),
                jnp.zeros((HEAD_DIM, tq), F32)) * heads_per_step
        carry = lax.fori_loop(0, qi * per_q, functools.partial(step, mask=None), init)
        for u in range(per_q):
            carry = step(qi * per_q + u, carry, causal[u])
        for idx in range(heads_per_step):
            outs.append(carry[3 * idx + 2] / carry[3 * idx + 1])
    o_ref[0] = jnp.concatenate(outs, axis=0).T.astype(o_ref.dtype)


def _fox(proj, v_t, c, ct, *, tq=256, tk=256, heads_per_step=8):
    b, s, _ = proj.shape
    w = B_HEADS * HEAD_DIM
    nk = s // tk
    sub = tq // LANES
    hw = w // 2
    base = (A_Q_HEADS + 2 * A_KV_HEADS) * HEAD_DIM
    qb, kb = base // hw, (base + w) // hw
    assert v_t.shape == (b, nk, w, tk)
    return pl.pallas_call(
        functools.partial(_fox_kernel, tq=tq, tk=tk, heads_per_step=heads_per_step),
        grid=(b, s // tq),
        in_specs=[
            pl.BlockSpec((1, tq, hw), lambda i, n: (i, n, qb)),
            pl.BlockSpec((1, tq, hw), lambda i, n: (i, n, qb + 1)),
            pl.BlockSpec((1, s, hw), lambda i, n: (i, 0, kb)),
            pl.BlockSpec((1, s, hw), lambda i, n: (i, 0, kb + 1)),
            pl.BlockSpec((1, nk, w, tk), lambda i, n: (i, 0, 0, 0)),
            pl.BlockSpec((1, s, LANES), lambda i, n: (i, 0, 0)),
            pl.BlockSpec((1, sub, B_HEADS, LANES), lambda i, n: (i, n, 0, 0)),
        ],
        out_specs=pl.BlockSpec((1, tq, w), lambda i, n: (i, n, 0)),
        out_shape=jax.ShapeDtypeStruct((b, s, w), BF16),
        compiler_params=_cparams(("parallel", "parallel")),
        name="fox",
    )(proj, proj, proj, proj, v_t, c, ct)


SSD_BATCHES = 4


def _ssd_gla_kernel(*refs):
    n_data, n_par = 9, 10
    data, params = refs[:n_data], refs[n_data:n_data + n_par]
    o_ref, hs_ref, gs_ref = refs[n_data + n_par:]

    @pl.when(pl.program_id(1) == 0)
    def _():
        hs_ref[...] = jnp.zeros_like(hs_ref)
        gs_ref[...] = jnp.zeros_like(gs_ref)

    for bb in range(o_ref.shape[0]):
        one = lambda ref: ref.at[pl.ds(bb, 1)]
        _ssd_gla_chunk(*[one(ref) for ref in data], *params, one(o_ref), hs_ref.at[bb], gs_ref.at[bb])


def _ssd_gla_chunk(z_ref, q_ref, k_ref, v_ref, r_ref, xc_ref, xp_ref, sdt_ref, sg_ref,
                   cw_ref, cb_ref, dtb_ref, alog_ref, dsk_ref, ex_ref, sn_ref, gkw_ref, gkb_ref, gn_ref,
                   o_ref, hs_ref, gs_ref):
    c = pl.program_id(1)
    q_len = C_CHUNK
    halo = xp_ref.shape[1]

    prev = xp_ref[0]
    cur = xc_ref[0]
    ext = jnp.concatenate([jnp.where(c > 0, prev, jnp.zeros_like(prev)), cur], axis=0)
    t_out = lax.broadcasted_iota(jnp.int32, (q_len, halo + q_len), 0)
    t_in = lax.broadcasted_iota(jnp.int32, (q_len, halo + q_len), 1) - halo
    acc = cb_ref[...] + cw_ref[C_CONV - 1:C_CONV, :] * cur.astype(F32)
    for j in range(C_CONV - 1):
        shift = (t_in == t_out - (C_CONV - 1 - j)).astype(BF16)
        acc = acc + cw_ref[j:j + 1, :] * jnp.dot(shift, ext, preferred_element_type=F32)
    xbc = jax.nn.silu(acc)
    xs = xbc[:, :C_INNER]
    gs_w = C_GROUPS * C_STATE
    bm = xbc[:, C_INNER:C_INNER + gs_w].astype(BF16)
    cm = xbc[:, C_INNER + gs_w:].astype(BF16)

    row = lax.broadcasted_iota(jnp.int32, (q_len, q_len), 0)
    col = lax.broadcasted_iota(jnp.int32, (q_len, q_len), 1)
    tri = col <= row

    lane = lax.broadcasted_iota(jnp.int32, (1, LANES), 1)
    dt = jnp.where(lane < C_HEADS, jax.nn.softplus(sdt_ref[0] + dtb_ref[...]), 0.0)
    dta = dt * -jnp.exp(alog_ref[...])
    acs = _dot_mask_lhs(tri.astype(BF16), dta)
    acs_t = acs.T
    chunk_dec = jnp.exp(acs[q_len - 1:q_len, :])
    expand = ex_ref[...]
    dt_x = _dot_mask_rhs(dt, expand)
    acs_x = _dot_mask_rhs(acs, expand)
    xd = xs * dt_x
    xd_b = xd.astype(BF16)
    xdd = xd * jnp.exp(acs_x[q_len - 1:q_len, :] - acs_x)
    low_half = lax.broadcasted_iota(jnp.int32, (q_len, LANES), 1) < C_HEAD_DIM

    y_pairs, y_offs = [], []
    tdims = (((1,), (1,)), ((), ()))
    for g in range(C_GROUPS):
        b_g = bm[:, g * C_STATE:(g + 1) * C_STATE]
        c_g = cm[:, g * C_STATE:(g + 1) * C_STATE]
        cb = lax.dot_general(c_g, b_g, tdims, preferred_element_type=F32)
        h0 = g * C_HPG
        grp = slice(h0 * C_HEAD_DIM, (h0 + C_HPG) * C_HEAD_DIM)
        y_offs.append(lax.dot_general(c_g, hs_ref[grp, :].astype(BF16), tdims, preferred_element_type=F32))
        for h in range(h0, h0 + C_HPG, 2):
            xp = xd_b[:, h * C_HEAD_DIM:(h + 2) * C_HEAD_DIM]
            halves = []
            for hh in (h, h + 1):
                seg = jnp.exp(jnp.where(tri, acs[:, hh:hh + 1] - acs_t[hh:hh + 1, :], -jnp.inf))
                halves.append(jnp.dot((cb * seg).astype(BF16), xp, preferred_element_type=F32))
            y_pairs.append(jnp.where(low_half, halves[0], halves[1]))
        upd = jnp.dot(xdd[:, grp].T.astype(BF16), b_g, preferred_element_type=F32)
        for hh in range(C_HPG):
            h = h0 + hh
            ps = slice(h * C_HEAD_DIM, (h + 1) * C_HEAD_DIM)
            us = slice(hh * C_HEAD_DIM, (hh + 1) * C_HEAD_DIM)
            hs_ref[ps, :] = hs_ref[ps, :] * chunk_dec[0:1, h:h + 1] + upd[us, :]
    y = (jnp.concatenate(y_pairs, axis=1) + jnp.concatenate(y_offs, axis=1) * jnp.exp(acs_x)
         + dsk_ref[...] * xs)
    y = y * jax.nn.silu(z_ref[0].astype(F32))
    o_ref[0, :, :C_INNER] = _rms(y, sn_ref[...]).astype(o_ref.dtype)

    same = (row // D_CHUNK) == (col // D_CHUNK)
    tri2 = tri & same
    la = jnp.dot(sg_ref[0].astype(BF16), gkw_ref[...], preferred_element_type=F32) + gkb_ref[...]
    la = jax.nn.log_sigmoid(la) / D_GATE_NORM
    gcs = _dot_mask_lhs(tri2.astype(BF16), la)
    first = lax.broadcasted_iota(jnp.int32, (q_len, 1), 0) < D_CHUNK
    r_all = r_ref[0]
    for h in range(D_HEADS):
        ks = slice(h * D_HK, (h + 1) * D_HK)
        vs = slice(h * D_HV, (h + 1) * D_HV)
        g_h = gcs[:, ks]
        g_end0 = g_h[D_CHUNK - 1:D_CHUNK, :]
        g_end1 = g_h[q_len - 1:q_len, :]
        q_h = q_ref[0, :, ks].astype(F32) * (D_HK ** -0.5)
        k_h = k_ref[0, :, ks].astype(F32)
        v_h = v_ref[0, :, vs]
        q_dec = (q_h * jnp.exp(g_h)).astype(BF16)
        k_inv = (k_h * jnp.exp(-g_h)).astype(BF16)
        k_end = k_h * jnp.exp(jnp.where(first, g_end0, g_end1) - g_h)
        ke0 = jnp.where(first, k_end, 0.0).astype(BF16)
        ke1 = jnp.where(first, 0.0, k_end).astype(BF16)
        attn = lax.dot_general(q_dec, k_inv, (((1,), (1,)), ((), ())), preferred_element_type=F32)
        attn = jnp.where(tri2, attn, 0.0).astype(BF16)
        o = jnp.dot(attn, v_h, preferred_element_type=F32)
        v_t = v_h.astype(F32).T.astype(BF16)
        st_rows = slice(h * D_HV, (h + 1) * D_HV)
        s0 = gs_ref[st_rows, :]
        s1 = s0 * jnp.exp(g_end0) + jnp.dot(v_t, ke0, preferred_element_type=F32)
        s2 = s1 * jnp.exp(g_end1) + jnp.dot(v_t, ke1, preferred_element_type=F32)
        gs_ref[st_rows, :] = s2
        tdims = (((1,), (1,)), ((), ()))
        o0 = lax.dot_general(q_dec, s0.astype(BF16), tdims, preferred_element_type=F32)
        o1 = lax.dot_general(q_dec, s1.astype(BF16), tdims, preferred_element_type=F32)
        o = o + jnp.where(first, o0, o1)
        o = _rms(o, gn_ref[...]) * jax.nn.silu(r_all[:, vs].astype(F32))
        o_ref[0, :, C_INNER + h * D_HV:C_INNER + (h + 1) * D_HV] = o.astype(o_ref.dtype)


def _ssd_gla(z, q, k, v, r, xbc, side_dt, side_g, g_lane, p):
    b, s, _ = z.shape
    q_len = C_CHUNK
    halo = 16
    nb = SSD_BATCHES if b % SSD_BATCHES == 0 else 1
    chunk = lambda width: pl.BlockSpec((nb, q_len, width), lambda i, n: (i, n, 0))
    full = lambda shape: pl.BlockSpec(shape, lambda i, n: (0,) * len(shape))
    pad_lanes = lambda vec: jnp.zeros((1, LANES), F32).at[0, :vec.shape[0]].set(vec.astype(F32))
    gkw = jnp.zeros((LANES, D_KEY), F32).at[g_lane:g_lane + D_GATE_RANK].set(p["gk_w"]).astype(BF16)
    expand = jnp.asarray(np.arange(C_INNER)[None, :] // C_HEAD_DIM == np.arange(LANES)[:, None], BF16)
    return pl.pallas_call(
        _ssd_gla_kernel,
        grid=(b // nb, s // q_len),
        in_specs=[
            chunk(C_INNER), chunk(D_KEY), chunk(D_KEY), chunk(D_VAL), chunk(D_VAL), chunk(C_CONV_DIM),
            pl.BlockSpec((nb, halo, C_CONV_DIM), lambda i, n: (i, jnp.maximum(n * (q_len // halo) - 1, 0), 0)),
            chunk(LANES), chunk(LANES),
            full((C_CONV, C_CONV_DIM)), full((1, C_CONV_DIM)),
            full((1, LANES)), full((1, LANES)), full((1, C_INNER)), full((LANES, C_INNER)),
            full((1, C_INNER)), full((LANES, D_KEY)), full((1, D_KEY)), full((1, D_HV)),
        ],
        out_specs=pl.BlockSpec((nb, q_len, C_INNER + D_VAL), lambda i, n: (i, n, 0)),
        out_shape=jax.ShapeDtypeStruct((b, s, C_INNER + D_VAL), BF16),
        scratch_shapes=[pltpu.VMEM((nb, C_INNER, C_STATE), F32),
                        pltpu.VMEM((nb, D_VAL, D_HK), F32)],
        compiler_params=_cparams(("parallel", "arbitrary")),
        name="ssd_gla",
    )(z, q, k, v, r, xbc, xbc, side_dt, side_g,
      p["conv_w"].astype(F32), p["conv_b"].reshape(1, -1).astype(F32),
      pad_lanes(p["dt_bias"]), pad_lanes(p["a_log"]),
      jnp.repeat(p["d_skip"].astype(F32), C_HEAD_DIM).reshape(1, C_INNER), expand,
      p["ssd_norm"].reshape(1, -1).astype(F32), gkw, p["gk_b"].reshape(1, -1).astype(F32),
      p["gla_norm"].reshape(1, -1).astype(F32))


def _store_rows_tiled(ref, val):
    m, d = val.shape
    nc = d // LANES
    for c in range(nc):
        ref[pl.ds(c, m, stride=nc), :] = val[:, c * LANES:(c + 1) * LANES]


def _load_rows_tiled(ref, m, dtype=None):
    nc = ref.shape[0] // m
    parts = [ref[pl.ds(c, m, stride=nc), :] for c in range(nc)]
    if dtype is not None:
        parts = [p.astype(dtype) for p in parts]
    return jnp.concatenate(parts, axis=1)


def _pack_pairs(x):
    n = x.shape[1] // 2
    u = pltpu.bitcast(x.astype(BF16).astype(F32), jnp.uint32)
    return (u[:, :n] >> 16) | (u[:, n:] & jnp.uint32(0xFFFF0000))


def _unpack_pairs(u):
    lo = pltpu.bitcast(u << 16, F32).astype(BF16)
    hi = pltpu.bitcast(u & jnp.uint32(0xFFFF0000), F32).astype(BF16)
    return jnp.concatenate([lo, hi], axis=1)


RT_GATE, RT_EXPERT, RT_RANK = 0, 2, 4


def _route_block(lg, carry, earlier):
    m = lg.shape[0]
    lane = lax.broadcasted_iota(jnp.int32, (m, LANES), 1)
    lane_f = lane.astype(F32)
    none = float(LANES)
    neg = -jnp.inf
    first_max = lambda v, vmax: jnp.min(jnp.where(v == vmax, lane_f, none), axis=-1, keepdims=True)
    gl = jnp.where(lane < MOE_GROUPS, lg, neg)
    gmax = jnp.max(gl, axis=-1, keepdims=True)
    g_w = 1.0 / jnp.sum(jnp.exp(gl - gmax), axis=-1, keepdims=True)
    lo = MOE_GROUPS + first_max(gl, gmax) * MOE_EPG
    el = jnp.where((lane_f >= lo) & (lane_f < lo + MOE_EPG), lg, neg)
    emax = jnp.max(el, axis=-1, keepdims=True)
    esum = jnp.sum(jnp.exp(el - emax), axis=-1, keepdims=True)
    l0 = first_max(el, emax)
    el2 = jnp.where(lane_f == l0, neg, el)
    emax2 = jnp.max(el2, axis=-1, keepdims=True)
    l1 = first_max(el2, emax2)
    p0 = 1.0 / esum
    p1 = jnp.exp(emax2 - emax) / esum
    w0 = g_w * (p0 / (p0 + p1))
    w1 = g_w * (p1 / (p0 + p1))
    oh0 = lane_f == l0
    oh1 = lane_f == l1
    oh = (oh0 | oh1).astype(BF16)
    cum = jnp.dot(earlier, oh, preferred_element_type=F32) + carry
    rank0 = jnp.sum(jnp.where(oh0, cum, 0.0), axis=-1, keepdims=True)
    rank1 = jnp.sum(jnp.where(oh1, cum, 0.0), axis=-1, keepdims=True)
    carry = carry + jnp.sum(oh.astype(F32), axis=0, keepdims=True)
    rec = jnp.zeros((m, LANES), F32)
    for pos, val in ((RT_GATE, w0), (RT_GATE + 1, w1), (RT_EXPERT, l0 - MOE_GROUPS),
                     (RT_EXPERT + 1, l1 - MOE_GROUPS), (RT_RANK, rank0), (RT_RANK + 1, rank1)):
        rec = jnp.where(lane == pos, val, rec)
    return rec, carry


def _out_proj_kernel(n_parts, *refs):
    a_refs = refs[:n_parts]
    w_refs = refs[n_parts:2 * n_parts]
    h_ref, g_ref, wr_ref, br_ref, tri_ref, ho_ref, xt_ref, rt_ref, cnt_ref, carry_ref = refs[2 * n_parts:]

    @pl.when(pl.program_id(0) == 0)
    def _():
        carry_ref[...] = jnp.zeros_like(carry_ref)

    acc = h_ref[...]
    for a_ref, w_ref in zip(a_refs, w_refs):
        acc = acc + jnp.dot(a_ref[...], w_ref[...], preferred_element_type=F32)
    ho_ref[...] = acc
    xn = _rms(acc, g_ref[...])
    xt_ref[...] = xn.astype(xt_ref.dtype)
    x_hi, x_mid, _ = _split3(xn)
    wr = wr_ref[...]
    lg2 = jnp.dot(x_hi, wr, preferred_element_type=F32)
    lg = (lg2[:, :LANES] + lg2[:, LANES:] + jnp.dot(x_mid, wr[:, :LANES], preferred_element_type=F32)
          + br_ref[...])
    rec, carry = _route_block(lg, carry_ref[...], tri_ref[...])
    rt_ref[...] = rec
    carry_ref[...] = carry
    cnt_ref[0] = carry


def _out_proj(parts, w_parts, h, g, w_route, b_route, *, tm=512):
    t, d = h.shape
    tm = min(tm, t)
    nc = d // LANES
    row = lambda i: (i, 0)
    const = lambda i: (0, 0)
    in_specs = [pl.BlockSpec((tm, a.shape[1]), row) for a in parts]
    in_specs += [pl.BlockSpec(w.shape, const) for w in w_parts]
    in_specs += [pl.BlockSpec((tm, d), row), pl.BlockSpec((1, d), const),
                 pl.BlockSpec((d, 2 * LANES), const), pl.BlockSpec((1, LANES), const),
                 pl.BlockSpec((tm, tm), const)]
    earlier = jnp.asarray(np.tril(np.ones((tm, tm), np.float32), -1), BF16)
    return pl.pallas_call(
        functools.partial(_out_proj_kernel, len(parts)),
        grid=(t // tm,),
        in_specs=in_specs,
        out_specs=[pl.BlockSpec((tm, d), row), pl.BlockSpec((tm, d), row),
                   pl.BlockSpec((tm, LANES), row), pl.BlockSpec((1, 1, LANES), lambda i: (i, 0, 0))],
        out_shape=[jax.ShapeDtypeStruct((t, d), F32), jax.ShapeDtypeStruct((t, d), BF16),
                   jax.ShapeDtypeStruct((t, LANES), F32), jax.ShapeDtypeStruct((t // tm, 1, LANES), F32)],
        scratch_shapes=[pltpu.VMEM((1, LANES), F32)],
        compiler_params=_cparams(("arbitrary",)),
        name="out_proj",
    )(*parts, *w_parts, h, g.reshape(1, d), w_route, b_route, earlier)


LONG_RUN = 128


def _piece_sizes(max_rows, lo=1, hi=None):
    sizes = [1 << b for b in reversed(range(max_rows.bit_length()))]
    return [n for n in sizes if n >= lo and (hi is None or n < hi)]


def _for_each_piece(run, sizes, body):
    off = run // (2 * sizes[0]) * (2 * sizes[0])
    for n in sizes:
        hit = (run & n) != 0
        pl.when(hit)(functools.partial(body, off, n))
        off = off + jnp.where(hit, n, 0)


def _for_each_run(len_ref, tile, long_ref, max_rows, make_body):
    def sweep(sizes):
        def per_expert(e, first):
            run = len_ref[tile * MOE_EXPERTS + e]
            _for_each_piece(run, sizes, make_body(e, first))
            return first + run

        lax.fori_loop(0, MOE_EXPERTS, per_expert, jnp.int32(0))

    sweep(_piece_sizes(max_rows, hi=LONG_RUN))
    pl.when(long_ref[tile] != 0)(functools.partial(sweep, _piece_sizes(max_rows, lo=LONG_RUN)))


def _zero_fill_rows(rows_ref, z_ref, zsem, lo_ref, hi_ref, n_rows, nc):
    zb = z_ref.shape[0] // nc
    z_ref[...] = jnp.zeros_like(z_ref)
    assert MOE_TILE // 2 <= zb and MOE_TILE % zb == 0

    def piece(row0, n):
        return pltpu.make_async_copy(z_ref.at[pl.ds(0, n * nc)], rows_ref.at[pl.ds(row0 * nc, n * nc)], zsem)

    def sweep(issue):
        def per_expert(e, carry):
            lo = lo_ref[e]

            def one(off, n):
                piece(lo + off, n).start() if issue else piece(lo + off, n).wait()

            _for_each_piece(hi_ref[e] - lo, _piece_sizes(MOE_TILE // 2), one)
            return carry

        def per_block(i, carry):
            piece(i * zb, zb).start() if issue else piece(i * zb, zb).wait()
            return carry

        lax.fori_loop(0, MOE_EXPERTS, per_expert, 0)
        lax.fori_loop(hi_ref[MOE_EXPERTS - 1] // zb, n_rows // zb, per_block, 0)

    sweep(True)
    sweep(False)


def _tile_positions(rt, pos_base):
    lane_f = lax.broadcasted_iota(jnp.int32, rt.shape, 1).astype(F32)
    out = []
    for k in range(MOE_TOPK):
        e_lane = rt[:, RT_EXPERT + k:RT_EXPERT + k + 1] + MOE_GROUPS
        out.append(rt[:, RT_RANK + k:RT_RANK + k + 1]
                   + jnp.sum(jnp.where(lane_f == e_lane, pos_base, 0.0), axis=-1, keepdims=True))
    return out


def _moe_dispatch_kernel(dst_ref, len_ref, long_ref, lo_ref, hi_ref, x_ref, rt_ref, pb_ref, rows_ref,
                         sbuf, z_ref, sem, zsem, *, tm, nc, n_steps, n_rows):
    i = pl.program_id(0)
    rows = MOE_TOPK * tm
    slot = lax.rem(i, 2)

    def wait_slot(s):
        pltpu.make_async_copy(sbuf.at[s], rows_ref.at[pl.ds(0, rows * nc)], sem.at[s]).wait()

    @pl.when(i == 0)
    def _():
        _zero_fill_rows(rows_ref, z_ref, zsem, lo_ref, hi_ref, n_rows, nc)

    @pl.when(i >= 2)
    def _():
        wait_slot(slot)

    pos = _tile_positions(rt_ref[...], pb_ref[0])
    lane = lax.broadcasted_iota(jnp.int32, (tm, LANES), 1)
    pos_t = jnp.where(lane == 0, pos[0], jnp.where(lane == 1, pos[1], 0.0)).T
    p_iota = lax.broadcasted_iota(jnp.int32, (rows, tm), 0).astype(F32)
    place = ((p_iota == pos_t[0:1, :]) | (p_iota == pos_t[1:2, :])).astype(BF16)
    _store_rows_tiled(sbuf.at[slot], _pack_pairs(jnp.dot(place, x_ref[...], preferred_element_type=F32)))

    def sender(e, src):
        dst = dst_ref[i * MOE_EXPERTS + e]

        def send(off, n):
            pltpu.make_async_copy(sbuf.at[slot, pl.ds((src + off) * nc, n * nc)],
                                  rows_ref.at[pl.ds((dst + off) * nc, n * nc)], sem.at[slot]).start()

        return send

    _for_each_run(len_ref, i, long_ref, rows, sender)

    @pl.when(i == n_steps - 1)
    def _():
        wait_slot(slot)
        if n_steps > 1:
            wait_slot(1 - slot)


def _moe_dispatch(x, route, pos_base, seg_dst, seg_len, seg_long, pad_lo, pad_hi, n_rows, *, tm):
    t, d = x.shape
    nc = d // (2 * LANES)
    n_steps = t // tm
    row = lambda i, *_: (i, 0)
    return pl.pallas_call(
        functools.partial(_moe_dispatch_kernel, tm=tm, nc=nc, n_steps=n_steps, n_rows=n_rows),
        grid_spec=pltpu.PrefetchScalarGridSpec(
            num_scalar_prefetch=5, grid=(n_steps,),
            in_specs=[pl.BlockSpec((tm, d), row), pl.BlockSpec((tm, LANES), row),
                      pl.BlockSpec((1, 1, LANES), lambda i, *_: (i, 0, 0))],
            out_specs=pl.BlockSpec(memory_space=pl.ANY),
            scratch_shapes=[pltpu.VMEM((2, MOE_TOPK * tm * nc, LANES), jnp.uint32),
                            pltpu.VMEM((MOE_TILE // 2 * nc, LANES), jnp.uint32),
                            pltpu.SemaphoreType.DMA((2,)), pltpu.SemaphoreType.DMA(())]),
        out_shape=jax.ShapeDtypeStruct((n_rows * nc, LANES), jnp.uint32),
        compiler_params=pltpu.CompilerParams(dimension_semantics=("arbitrary",), has_side_effects=True,
                                             vmem_limit_bytes=VMEM_LIMIT),
        name="moe_dispatch",
    )(seg_dst, seg_len, seg_long, pad_lo, pad_hi, x, route, pos_base)


def _moe_kernel(te_ref, nu_ref, first_ref, slot_ref, next_ref, tv_ref, x_ref, wg_hbm, wu_hbm, wd_hbm, y_ref,
                wg_buf, wu_buf, wd_buf, sem, *, layer):
    i = pl.program_id(0)
    live = i < nu_ref[0]
    pairs = ((wg_hbm, wg_buf), (wu_hbm, wu_buf), (wd_hbm, wd_buf))

    def weights(e, s):
        return [pltpu.make_async_copy(hbm.at[layer, e], buf.at[s], sem.at[s, k])
                for k, (hbm, buf) in enumerate(pairs)]

    @pl.when(i == 0)
    def _():
        for cp in weights(te_ref[0], 0):
            cp.start()

    @pl.when(live & (first_ref[i] != 0))
    def _():
        s = slot_ref[i]
        for cp in weights(te_ref[i], s):
            cp.wait()

        @pl.when(next_ref[i] >= 0)
        def _():
            for cp in weights(next_ref[i], 1 - s):
                cp.start()

    @pl.when(i >= nu_ref[0])
    def _():
        y_ref[...] = jnp.zeros_like(y_ref)

    half = MOE_TILE // 2
    nc = x_ref.shape[0] // MOE_TILE

    def mlp(rows):
        s = slot_ref[i]
        x = _unpack_pairs(_load_rows_tiled(x_ref.at[pl.ds(0, rows * nc)], rows))
        gate = jnp.dot(x, wg_buf[s].astype(BF16), preferred_element_type=F32)
        up = jnp.dot(x, wu_buf[s].astype(BF16), preferred_element_type=F32)
        act = (jax.nn.silu(gate) * up).astype(BF16)
        y = jnp.dot(act, wd_buf[s].astype(BF16), preferred_element_type=F32)
        _store_rows_tiled(y_ref.at[pl.ds(0, rows * nc)], _pack_pairs(y))
        if rows < MOE_TILE:
            y_ref[pl.ds(rows * nc, (MOE_TILE - rows) * nc), :] = jnp.zeros(((MOE_TILE - rows) * nc, LANES), y_ref.dtype)

    pl.when(live & (tv_ref[i] > half))(functools.partial(mlp, MOE_TILE))
    pl.when(live & (tv_ref[i] <= half))(functools.partial(mlp, half))


def _moe_experts(x_rows, tile_expert, tile_valid, n_used, w_gate, w_up, w_down, layer):
    d, ff = w_gate.shape[-2:]
    nc = d // (2 * LANES)
    n_tiles = x_rows.shape[0] // (MOE_TILE * nc)
    tiles = jnp.arange(n_tiles, dtype=jnp.int32)
    used = tiles < n_used[0]
    te = jnp.where(used, tile_expert, MOE_EXPERTS)
    first = (used & (te != jnp.concatenate([jnp.full((1,), -1, jnp.int32), te[:-1]]))).astype(jnp.int32)
    slot = ((jnp.cumsum(first) - 1) % 2).astype(jnp.int32)
    after = jnp.searchsorted(te, te, side="right")
    nxt = jnp.where(after < n_used[0], te[jnp.minimum(after, n_tiles - 1)], -1).astype(jnp.int32)
    live = lambda i, nu: jnp.minimum(i, nu[0] - 1)
    any_spec = pl.BlockSpec(memory_space=pl.ANY)
    return pl.pallas_call(
        functools.partial(_moe_kernel, layer=layer),
        grid_spec=pltpu.PrefetchScalarGridSpec(
            num_scalar_prefetch=6, grid=(n_tiles,),
            in_specs=[pl.BlockSpec((MOE_TILE * nc, LANES), lambda i, te, nu, *_: (live(i, nu), 0)),
                      any_spec, any_spec, any_spec],
            out_specs=pl.BlockSpec((MOE_TILE * nc, LANES), lambda i, *_: (i, 0)),
            scratch_shapes=[pltpu.VMEM((2, d, ff), F32), pltpu.VMEM((2, d, ff), F32), pltpu.VMEM((2, ff, d), F32),
                            pltpu.SemaphoreType.DMA((2, 3))]),
        out_shape=jax.ShapeDtypeStruct(x_rows.shape, x_rows.dtype),
        compiler_params=_cparams(("arbitrary",)),
        name="moe_experts",
    )(tile_expert, n_used, first, slot, nxt, tile_valid, x_rows, w_gate, w_up, w_down)


def _moe_combine_kernel(src_ref, len_ref, long_ref, h_ref, rt_ref, pb_ref, g_ref, y_hbm, o_ref, ybuf, sem,
                        *, tm, nc, final):
    i = pl.program_id(0)
    n_steps = pl.num_programs(0)
    rows = MOE_TOPK * tm

    def fetch(tile, slot):
        def receiver(e, dst):
            src = src_ref[tile * MOE_EXPERTS + e]

            def recv(off, n):
                pltpu.make_async_copy(y_hbm.at[pl.ds((src + off) * nc, n * nc)],
                                      ybuf.at[slot, pl.ds((dst + off) * nc, n * nc)], sem.at[slot]).start()

            return recv

        _for_each_run(len_ref, tile, long_ref, rows, receiver)

    slot = lax.rem(i, 2)

    @pl.when(i == 0)
    def _():
        fetch(0, 0)

    @pl.when(i + 1 < n_steps)
    def _():
        fetch(i + 1, 1 - slot)

    pltpu.make_async_copy(y_hbm.at[pl.ds(0, rows * nc)], ybuf.at[slot], sem.at[slot]).wait()
    y = _unpack_pairs(_load_rows_tiled(ybuf.at[slot], rows))
    rt = rt_ref[...]
    pos_f = lax.broadcasted_iota(jnp.int32, (tm, rows), 1).astype(F32)
    pick = jnp.zeros((tm, rows), F32)
    for k, pos in enumerate(_tile_positions(rt, pb_ref[0])):
        pick = jnp.where(pos_f == pos, rt[:, RT_GATE + k:RT_GATE + k + 1], pick)
    out = h_ref[...] + jnp.dot(pick.astype(BF16), y, preferred_element_type=F32)
    o_ref[...] = _rms(out, g_ref[...]) if final else out


def _moe_combine_rows(h, y_rows, route, seg_src, seg_len, seg_long, pos_base, g, *, tm, final):
    t, d = h.shape
    nc = d // (2 * LANES)
    row = lambda i, *_: (i, 0)
    return pl.pallas_call(
        functools.partial(_moe_combine_kernel, tm=tm, nc=nc, final=final),
        grid_spec=pltpu.PrefetchScalarGridSpec(
            num_scalar_prefetch=3, grid=(t // tm,),
            in_specs=[pl.BlockSpec((tm, d), row), pl.BlockSpec((tm, LANES), row),
                      pl.BlockSpec((1, 1, LANES), lambda i, *_: (i, 0, 0)),
                      pl.BlockSpec((1, d), lambda i, *_: (0, 0)),
                      pl.BlockSpec(memory_space=pl.ANY)],
            out_specs=pl.BlockSpec((tm, d), row),
            scratch_shapes=[pltpu.VMEM((2, MOE_TOPK * tm * nc, LANES), y_rows.dtype),
                            pltpu.SemaphoreType.DMA((2,))]),
        out_shape=jax.ShapeDtypeStruct((t, d), F32),
        compiler_params=_cparams(("arbitrary",)),
        name="moe_combine",
    )(seg_src, seg_len, seg_long, h, route, pos_base, g.reshape(1, d), y_rows)


def _moe(h, x, route, tile_counts, w_gate, w_up, w_down, layer, g, *, final):
    t = route.shape[0]
    n_tt = tile_counts.shape[0]
    tm = t // n_tt
    after = tile_counts[:, 0, MOE_GROUPS:MOE_GROUPS + MOE_EXPERTS].astype(jnp.int32)
    before = jnp.concatenate([jnp.zeros((1, MOE_EXPERTS), jnp.int32), after[:-1]], axis=0)
    cnt = after[-1]
    padded = (cnt + MOE_TILE - 1) // MOE_TILE * MOE_TILE
    pad_ends = jnp.cumsum(padded)
    starts = (pad_ends - padded).astype(jnp.int32)
    n_tiles = (t * MOE_TOPK + MOE_EXPERTS * (MOE_TILE - 1)) // MOE_TILE
    tile_start = jnp.arange(n_tiles, dtype=jnp.int32) * MOE_TILE
    tile_expert = jnp.minimum(jnp.sum(tile_start[:, None] >= pad_ends[None, :], axis=1),
                              MOE_EXPERTS - 1).astype(jnp.int32)
    n_used = (pad_ends[-1] // MOE_TILE).astype(jnp.int32).reshape(1)
    tile_valid = jnp.clip(cnt[tile_expert] - (tile_start - starts[tile_expert]), 0, MOE_TILE).astype(jnp.int32)
    n_rows = n_tiles * MOE_TILE
    seg_len = (after - before).reshape(-1)
    seg_off = jnp.cumsum(after - before, axis=1) - (after - before)
    seg_row = (starts[None, :] + before).reshape(-1)
    pos_base = jnp.zeros((n_tt, 1, LANES), F32).at[:, 0, MOE_GROUPS:MOE_GROUPS + MOE_EXPERTS].set(
        (seg_off - before).astype(F32))
    seg_long = jnp.any(after - before >= LONG_RUN, axis=1).astype(jnp.int32)
    x_rows = _moe_dispatch(x, route, pos_base, seg_row, seg_len, seg_long, starts + cnt,
                           pad_ends.astype(jnp.int32), n_rows, tm=tm)
    y_rows = _moe_experts(x_rows, tile_expert, tile_valid, n_used, w_gate, w_up, w_down, layer)
    return _moe_combine_rows(h, y_rows, route, seg_row, seg_len, seg_long, pos_base, g, tm=tm, final=final)


def _router_weights(w_group, b_group, w_router, b_router):
    d = w_group.shape[0]
    w = jnp.zeros((d, LANES), F32)
    w = w.at[:, :MOE_GROUPS].set(w_group).at[:, MOE_GROUPS:MOE_GROUPS + MOE_EXPERTS].set(w_router)
    b = jnp.zeros((1, LANES), F32)
    b = b.at[0, :MOE_GROUPS].set(b_group).at[0, MOE_GROUPS:MOE_GROUPS + MOE_EXPERTS].set(b_router)
    w_hi = w.astype(BF16)
    w_mid = (w - w_hi.astype(F32)).astype(BF16)
    return jnp.concatenate([w_hi, w_mid], axis=1), b


def kernel(x, norm_mix, norm_moe, norm_final, even_w_in, even_sinks, even_forget_bias, even_w_out,
           odd_w_in, odd_conv_w, odd_conv_b, odd_dt_bias, odd_a_log, odd_d_skip, odd_ssd_norm,
           odd_gk_w, odd_gk_b, odd_gla_norm, odd_w_out, moe_w_group, moe_b_group, moe_w_router,
           moe_b_router, moe_w_gate, moe_w_up, moe_w_down):
    b, s, d = x.shape
    t = b * s
    depth = norm_mix.shape[0]
    h = x.reshape(t, d)
    for layer in range(depth):
        i = layer // 2
        if layer % 2 == 0:
            w = even_w_in.astype(BF16)[i]
            n_ab = (A_Q_HEADS + 2 * A_KV_HEADS + 3 * B_HEADS) * HEAD_DIM
            w_main = w[:, :n_ab]
            w_aux = jnp.pad(w[:, n_ab:], ((0, 0), (0, LANES - B_HEADS)))
            n_a, kv_a, n_b = (A_Q_HEADS + 2 * A_KV_HEADS) * HEAD_DIM, A_KV_HEADS * HEAD_DIM, B_HEADS * HEAD_DIM
            swa_rows, fox_rows = SWA_QBLOCKS * A_WINDOW, 256
            proj, f_aux, va_t, vb_t = _norm_proj(
                h, norm_mix[layer], [w_main, w_aux], ((0, 0, n_ab, BF16), (1, 0, LANES, F32)), tm=1024,
                t_plan=((0, n_a - kv_a, kv_a, BF16, swa_rows), (0, n_ab - n_b, n_b, BF16, fox_rows)))
            proj = proj.reshape(b, s, -1)
            out_a = _swa(proj, va_t.reshape(b, s // swa_rows, kv_a, swa_rows), even_sinks[i])
            c, ct = _fox_gate(f_aux.reshape(b, s, LANES), even_forget_bias[i])
            out_b = _fox(proj, vb_t.reshape(b, s // fox_rows, n_b, fox_rows), c, ct, tk=fox_rows)
            n_ha = A_Q_HEADS * HEAD_DIM
            w_out = even_w_out[i].astype(BF16)
            parts = [out_a.reshape(t, -1), out_b.reshape(t, -1)]
            w_parts = [w_out[:n_ha], w_out[n_ha:]]
        else:
            w = odd_w_in.astype(BF16)[i]
            o_z, o_xbc = 0, C_INNER
            o_dt = o_xbc + C_CONV_DIM
            o_q = o_dt + C_HEADS
            o_k = o_q + D_KEY
            o_v = o_k + D_KEY
            o_g = o_v + D_VAL
            o_r = o_g + D_GATE_RANK
            n_b = w.shape[1] - o_dt
            w_a = w[:, :o_dt]
            w_b = jnp.pad(w[:, o_dt:], ((0, 0), (0, -n_b % LANES)))
            g_win = (o_g - o_dt) // LANES * LANES
            plan = ((0, o_z, C_INNER, BF16), (1, o_q - o_dt, D_KEY, BF16), (1, o_k - o_dt, D_KEY, BF16),
                    (1, o_v - o_dt, D_VAL, BF16), (1, o_r - o_dt, D_VAL, BF16), (0, o_xbc, C_CONV_DIM, BF16),
                    (1, 0, LANES, F32), (1, g_win, LANES, F32))
            outs = _norm_proj(h, norm_mix[layer], [w_a, w_b], plan, tm=512)
            params = dict(conv_w=odd_conv_w[i], conv_b=odd_conv_b[i], dt_bias=odd_dt_bias[i], a_log=odd_a_log[i],
                          d_skip=odd_d_skip[i], ssd_norm=odd_ssd_norm[i], gk_w=odd_gk_w[i], gk_b=odd_gk_b[i],
                          gla_norm=odd_gla_norm[i])
            mixed = _ssd_gla(*[o.reshape(b, s, -1) for o in outs], o_g - o_dt - g_win, params)
            parts = [mixed.reshape(t, -1)]
            w_parts = [odd_w_out[i].astype(BF16)]
        w_route, b_route = _router_weights(moe_w_group[layer], moe_b_group[layer],
                                           moe_w_router[layer], moe_b_router[layer])
        h, x_tiled, route, tile_counts = _out_proj(parts, w_parts, h, norm_moe[layer], w_route, b_route)
        h = _moe(h, x_tiled, route, tile_counts, moe_w_gate, moe_w_up, moe_w_down, layer, norm_final,
                 final=layer == depth - 1)
    out = h
    return out.reshape(b, s, d)
```

```python
import functools

import numpy as np
import jax
import jax.numpy as jnp
from jax import lax
from jax.experimental import pallas as pl
from jax.experimental.pallas import tpu as pltpu

F32 = jnp.float32
BF16 = jnp.bfloat16

RMS_EPS = 1e-6
HEAD_DIM = 64
A_Q_HEADS = 8
A_KV_HEADS = 2
A_GROUP = A_Q_HEADS // A_KV_HEADS
A_WINDOW = 128
B_HEADS = 8
C_HEADS = 16
C_HEAD_DIM = 64
C_INNER = C_HEADS * C_HEAD_DIM
C_GROUPS = 2
C_HPG = C_HEADS // C_GROUPS
C_STATE = 128
C_CONV = 4
C_CHUNK = 128
C_CONV_DIM = C_INNER + 2 * C_GROUPS * C_STATE
D_HEADS = 4
D_HK = 128
D_HV = 256
D_KEY = D_HEADS * D_HK
D_VAL = D_HEADS * D_HV
D_GATE_RANK = 16
D_GATE_NORM = 16.0
D_CHUNK = 64
MOE_GROUPS = 4
MOE_EPG = 8
MOE_EXPERTS = MOE_GROUPS * MOE_EPG
MOE_TOPK = 2

LANES = 128
VMEM_LIMIT = 48 * 1024 * 1024
MOE_TILE = 512


def _cparams(sem):
    return pltpu.CompilerParams(dimension_semantics=sem, vmem_limit_bytes=VMEM_LIMIT)


def _rms(x, g):
    ms = jnp.mean(x * x, axis=-1, keepdims=True)
    return x * lax.rsqrt(ms + RMS_EPS) * g


def _norm_proj_kernel(n_w, plan, t_plan, *refs):
    x_ref, g_ref = refs[:2]
    w_refs = refs[2:2 + n_w]
    o_refs = refs[2 + n_w:2 + n_w + len(plan)]
    t_refs = refs[2 + n_w + len(plan):2 + n_w + len(plan) + len(t_plan)]
    res_refs = refs[2 + n_w + len(plan) + len(t_plan):]
    xn = _rms(x_ref[...], g_ref[...]).astype(BF16)
    for w_ref, res_ref in zip(w_refs, res_refs):
        res_ref[...] = jnp.dot(xn, w_ref[...], preferred_element_type=F32)
    for o_ref, (wi, start, width, _) in zip(o_refs, plan):
        o_ref[...] = res_refs[wi][:, start:start + width].astype(o_ref.dtype)
    for o_ref, (wi, start, width, _, rows) in zip(t_refs, t_plan):
        for u in range(o_ref.shape[0]):
            o_ref[u] = res_refs[wi][u * rows:(u + 1) * rows, start:start + width].T.astype(o_ref.dtype)


def _norm_proj(x, g, weights, plan, *, tm, t_plan=()):
    t, d = x.shape
    tm = min(tm, t)
    row = lambda i: (i, 0)
    const = lambda i: (0, 0)
    in_specs = [pl.BlockSpec((tm, d), row), pl.BlockSpec((1, d), const)]
    in_specs += [pl.BlockSpec(w.shape, const, pipeline_mode=pl.Buffered(1)) for w in weights]
    return pl.pallas_call(
        functools.partial(_norm_proj_kernel, len(weights), plan, t_plan),
        grid=(t // tm,),
        in_specs=in_specs,
        out_specs=([pl.BlockSpec((tm, width), row) for _, _, width, _ in plan]
                   + [pl.BlockSpec((tm // rows, width, rows), lambda i: (i, 0, 0))
                      for _, _, width, _, rows in t_plan]),
        out_shape=([jax.ShapeDtypeStruct((t, width), dtype) for _, _, width, dtype in plan]
                   + [jax.ShapeDtypeStruct((t // rows, width, rows), dtype) for _, _, width, dtype, rows in t_plan]),
        scratch_shapes=[pltpu.VMEM((tm, w.shape[1]), F32) for w in weights],
        compiler_params=_cparams(("parallel",)),
        name="norm_proj",
    )(x, g.reshape(1, d), *weights)


SWA_QBLOCKS = 8


def _swa_kernel(sink_ref, slope_ref, q_ref, kp_ref, kc_ref, vp_ref, vc_ref, o_ref):
    n = pl.program_id(1)
    blk = A_WINDOW
    wide = A_GROUP * blk
    key = lax.broadcasted_iota(jnp.int32, (2 * blk, wide), 0)
    qry = lax.broadcasted_iota(jnp.int32, (2 * blk, wide), 1) % blk
    dist = blk + qry - key
    in_window = (dist >= 0) & (dist < A_WINDOW)
    distf = dist.astype(F32)
    nt = (((1,), (1,)), ((), ()))
    k_all = jnp.concatenate([kp_ref[0], kc_ref[0]], axis=0)
    v_all = jnp.concatenate([vp_ref[0], vc_ref[0]], axis=1)
    units = [(j, kh) for j in range(SWA_QBLOCKS) for kh in range(A_KV_HEADS)]
    scores = []
    for j, kh in units:
        k = k_all[j * blk:(j + 2) * blk, kh * HEAD_DIM:(kh + 1) * HEAD_DIM]
        q = jnp.concatenate([q_ref[0, j * blk:(j + 1) * blk,
                                   (kh * A_GROUP + g) * HEAD_DIM:(kh * A_GROUP + g + 1) * HEAD_DIM]
                             for g in range(A_GROUP)], axis=0)
        scores.append(lax.dot_general(k, q, nt, preferred_element_type=F32))
    probs = []
    for (j, kh), s in zip(units, scores):
        valid = in_window & ((key >= blk) | (n * SWA_QBLOCKS + j > 0))
        s = s * (HEAD_DIM ** -0.5) - slope_ref[kh:kh + 1, :] * distf
        s = jnp.where(valid, s, -jnp.inf)
        sink = sink_ref[kh:kh + 1, :]
        m = jnp.maximum(jnp.max(s, axis=0, keepdims=True), sink)
        p = jnp.exp(s - m)
        probs.append((p.astype(BF16), jnp.sum(p, axis=0, keepdims=True) + jnp.exp(sink - m)))
    for j in range(SWA_QBLOCKS):
        outs = []
        for kh in range(A_KV_HEADS):
            p, denom = probs[j * A_KV_HEADS + kh]
            v_t = v_all[kh * HEAD_DIM:(kh + 1) * HEAD_DIM, j * blk:(j + 2) * blk]
            o_t = jnp.dot(v_t, p, preferred_element_type=F32) / denom
            outs += [o_t[:, g * blk:(g + 1) * blk] for g in range(A_GROUP)]
        o_ref[0, j * blk:(j + 1) * blk, :] = jnp.concatenate(outs, axis=0).T.astype(o_ref.dtype)


def _swa(proj, v_t, sinks):
    b, s, _ = proj.shape
    blk = A_WINDOW
    qw = A_Q_HEADS * HEAD_DIM
    kw = A_KV_HEADS * HEAD_DIM
    k_blk = qw // kw
    per_lane = lambda vec: jnp.repeat(vec.astype(F32), blk).reshape(A_KV_HEADS, A_GROUP * blk)
    slopes = jnp.asarray(2.0 ** (-8.0 * np.arange(1, A_Q_HEADS + 1) / A_Q_HEADS), F32)
    tq = SWA_QBLOCKS * blk
    assert v_t.shape == (b, s // tq, kw, tq)
    prev = lambda n: jnp.maximum(n * SWA_QBLOCKS - 1, 0)
    full = pl.BlockSpec((A_KV_HEADS, A_GROUP * blk), lambda i, n: (0, 0))
    return pl.pallas_call(
        _swa_kernel,
        grid=(b, s // tq),
        in_specs=[
            full, full,
            pl.BlockSpec((1, tq, qw), lambda i, n: (i, n, 0)),
            pl.BlockSpec((1, blk, kw), lambda i, n: (i, prev(n), k_blk)),
            pl.BlockSpec((1, tq, kw), lambda i, n: (i, n, k_blk)),
            pl.BlockSpec((1, None, kw, blk), lambda i, n: (i, jnp.maximum(n - 1, 0), 0, SWA_QBLOCKS - 1)),
            pl.BlockSpec((1, None, kw, tq), lambda i, n: (i, n, 0, 0)),
        ],
        out_specs=pl.BlockSpec((1, tq, qw), lambda i, n: (i, n, 0)),
        out_shape=jax.ShapeDtypeStruct((b, s, qw), BF16),
        compiler_params=_cparams(("parallel", "parallel")),
        name="swa",
    )(per_lane(sinks), per_lane(slopes), proj, proj, proj, v_t, v_t)


def _tril(n, dtype=F32):
    r = lax.broadcasted_iota(jnp.int32, (n, n), 0)
    c = lax.broadcasted_iota(jnp.int32, (n, n), 1)
    return (c <= r).astype(dtype)


def _split3(x):
    hi = x.astype(BF16)
    r = x - hi.astype(F32)
    mid = r.astype(BF16)
    return hi, mid, (r - mid.astype(F32)).astype(BF16)


def _dot_mask_lhs(mask, x):
    return sum(jnp.dot(mask, part, preferred_element_type=F32) for part in _split3(x))


def _dot_mask_rhs(x, mask):
    return sum(jnp.dot(part, mask, preferred_element_type=F32) for part in _split3(x))


def _fox_gate_kernel(f_ref, b_ref, c_ref, ct_ref):
    tri = _tril(LANES, BF16)
    carry = jnp.zeros((1, LANES), F32)
    for n in range(f_ref.shape[1] // LANES):
        rows = slice(n * LANES, (n + 1) * LANES)
        lf = jax.nn.log_sigmoid(f_ref[0, rows, :] + b_ref[...])
        cs = _dot_mask_lhs(tri, lf) + carry
        carry = cs[LANES - 1:LANES, :]
        c_ref[0, rows, :] = cs
        ct_ref[0, n] = cs.T[:B_HEADS, :]


def _fox_gate(f_aux, bias):
    b, s, _ = f_aux.shape
    nb = s // LANES
    bias_p = jnp.zeros((1, LANES), F32).at[0, :B_HEADS].set(bias.astype(F32))
    return pl.pallas_call(
        _fox_gate_kernel,
        grid=(b,),
        in_specs=[pl.BlockSpec((1, s, LANES), lambda i: (i, 0, 0)),
                  pl.BlockSpec((1, LANES), lambda i: (0, 0))],
        out_specs=[pl.BlockSpec((1, s, LANES), lambda i: (i, 0, 0)),
                   pl.BlockSpec((1, nb, B_HEADS, LANES), lambda i: (i, 0, 0, 0))],
        out_shape=[jax.ShapeDtypeStruct((b, s, LANES), F32),
                   jax.ShapeDtypeStruct((b, nb, B_HEADS, LANES), F32)],
        compiler_params=_cparams(("parallel",)),
        name="fox_gate",
    )(f_aux, bias_p)


def _fox_kernel(q0_ref, q1_ref, k0_ref, k1_ref, vt_ref, c_ref, ctq_ref, o_ref, *, tq, tk, heads_per_step):
    qi = pl.program_id(1)
    sub = tq // LANES
    key = lax.broadcasted_iota(jnp.int32, (tk, tq), 0)
    qry = lax.broadcasted_iota(jnp.int32, (tk, tq), 1)
    per_q = tq // tk
    causal = [u * tk + key <= qry for u in range(per_q)]
    nt = (((1,), (1,)), ((), ()))
    half = B_HEADS // 2
    q_refs, k_refs = (q0_ref, q1_ref), (k0_ref, k1_ref)
    outs = []
    for h0 in range(0, B_HEADS, heads_per_step):
        heads = list(range(h0, h0 + heads_per_step))
        hsl = [slice(h * HEAD_DIM, (h + 1) * HEAD_DIM) for h in heads]
        lsl = [slice((h % half) * HEAD_DIM, (h % half + 1) * HEAD_DIM) for h in heads]
        qs = [q_refs[h // half][0, :, ls] * (HEAD_DIM ** -0.5)
              for h, ls in zip(heads, lsl)]
        cqs = [jnp.concatenate([ctq_ref[0, u, h:h + 1, :] for u in range(sub)], axis=1) for h in heads]

        def step(j, carry, mask, heads=heads, hsl=hsl, lsl=lsl, qs=qs, cqs=cqs):
            start = pl.multiple_of(j * tk, tk)
            sts = [lax.dot_general(k_refs[h // half][0, pl.ds(start, tk), ls], q, nt,
                                   preferred_element_type=F32)
                   for h, ls, q in zip(heads, lsl, qs)]
            ps, stats = [], []
            for idx, h in enumerate(heads):
                m, l, _ = carry[3 * idx:3 * idx + 3]
                ck = c_ref[0, pl.ds(start, tk), h:h + 1]
                st = (sts[idx] - ck) + cqs[idx]
                if mask is not None:
                    st = jnp.where(mask, st, -jnp.inf)
                m_new = jnp.maximum(m, jnp.max(st, axis=0, keepdims=True))
                alpha = jnp.exp(m - m_new)
                p = jnp.exp(st - m_new)
                stats.append((m_new, alpha, alpha * l + jnp.sum(p, axis=0, keepdims=True)))
                ps.append(p.astype(BF16))
            new = []
            for idx in range(len(heads)):
                m_new, alpha, l = stats[idx]
                pv = jnp.dot(vt_ref[0, j, hsl[idx], :], ps[idx], preferred_element_type=F32)
                new += [m_new, l, alpha * carry[3 * idx + 2] + pv]
            return tuple(new)

        init = (jnp.full((1, tq), -jnp.inf, F32), jnp.zeros((1, tq), F32),
                jnp.zeros((HEAD_DIM, tq), F32)) * heads_per_step
        carry = lax.fori_loop(0, qi * per_q, functools.partial(step, mask=None), init)
        for u in range(per_q):
            carry = step(qi * per_q + u, carry, causal[u])
        for idx in range(heads_per_step):
            outs.append(carry[3 * idx + 2] / carry[3 * idx + 1])
    o_ref[0] = jnp.concatenate(outs, axis=0).T.astype(o_ref.dtype)


def _fox(proj, v_t, c, ct, *, tq=256, tk=256, heads_per_step=8):
    b, s, _ = proj.shape
    w = B_HEADS * HEAD_DIM
    nk = s // tk
    sub = tq // LANES
    hw = w // 2
    base = (A_Q_HEADS + 2 * A_KV_HEADS) * HEAD_DIM
    qb, kb = base // hw, (base + w) // hw
    assert v_t.shape == (b, nk, w, tk)
    return pl.pallas_call(
        functools.partial(_fox_kernel, tq=tq, tk=tk, heads_per_step=heads_per_step),
        grid=(b, s // tq),
        in_specs=[
            pl.BlockSpec((1, tq, hw), lambda i, n: (i, n, qb)),
            pl.BlockSpec((1, tq, hw), lambda i, n: (i, n, qb + 1)),
            pl.BlockSpec((1, s, hw), lambda i, n: (i, 0, kb)),
            pl.BlockSpec((1, s, hw), lambda i, n: (i, 0, kb + 1)),
            pl.BlockSpec((1, nk, w, tk), lambda i, n: (i, 0, 0, 0)),
            pl.BlockSpec((1, s, LANES), lambda i, n: (i, 0, 0)),
            pl.BlockSpec((1, sub, B_HEADS, LANES), lambda i, n: (i, n, 0, 0)),
        ],
        out_specs=pl.BlockSpec((1, tq, w), lambda i, n: (i, n, 0)),
        out_shape=jax.ShapeDtypeStruct((b, s, w), BF16),
        compiler_params=_cparams(("parallel", "parallel")),
        name="fox",
    )(proj, proj, proj, proj, v_t, c, ct)


SSD_BATCHES = 4


def _ssd_gla_kernel(*refs):
    n_data, n_par = 9, 10
    data, params = refs[:n_data], refs[n_data:n_data + n_par]
    o_ref, hs_ref, gs_ref = refs[n_data + n_par:]

    @pl.when(pl.program_id(1) == 0)
    def _():
        hs_ref[...] = jnp.zeros_like(hs_ref)
        gs_ref[...] = jnp.zeros_like(gs_ref)

    for bb in range(o_ref.shape[0]):
        one = lambda ref: ref.at[pl.ds(bb, 1)]
        _ssd_gla_chunk(*[one(ref) for ref in data], *params, one(o_ref), hs_ref.at[bb], gs_ref.at[bb])


def _ssd_gla_chunk(z_ref, q_ref, k_ref, v_ref, r_ref, xc_ref, xp_ref, sdt_ref, sg_ref,
                   cw_ref, cb_ref, dtb_ref, alog_ref, dsk_ref, ex_ref, sn_ref, gkw_ref, gkb_ref, gn_ref,
                   o_ref, hs_ref, gs_ref):
    c = pl.program_id(1)
    q_len = C_CHUNK
    halo = xp_ref.shape[1]

    prev = xp_ref[0]
    cur = xc_ref[0]
    ext = jnp.concatenate([jnp.where(c > 0, prev, jnp.zeros_like(prev)), cur], axis=0)
    t_out = lax.broadcasted_iota(jnp.int32, (q_len, halo + q_len), 0)
    t_in = lax.broadcasted_iota(jnp.int32, (q_len, halo + q_len), 1) - halo
    acc = cb_ref[...] + cw_ref[C_CONV - 1:C_CONV, :] * cur.astype(F32)
    for j in range(C_CONV - 1):
        shift = (t_in == t_out - (C_CONV - 1 - j)).astype(BF16)
        acc = acc + cw_ref[j:j + 1, :] * jnp.dot(shift, ext, preferred_element_type=F32)
    xbc = jax.nn.silu(acc)
    xs = xbc[:, :C_INNER]
    gs_w = C_GROUPS * C_STATE
    bm = xbc[:, C_INNER:C_INNER + gs_w].astype(BF16)
    cm = xbc[:, C_INNER + gs_w:].astype(BF16)

    row = lax.broadcasted_iota(jnp.int32, (q_len, q_len), 0)
    col = lax.broadcasted_iota(jnp.int32, (q_len, q_len), 1)
    tri = col <= row

    lane = lax.broadcasted_iota(jnp.int32, (1, LANES), 1)
    dt = jnp.where(lane < C_HEADS, jax.nn.softplus(sdt_ref[0] + dtb_ref[...]), 0.0)
    dta = dt * -jnp.exp(alog_ref[...])
    acs = _dot_mask_lhs(tri.astype(BF16), dta)
    acs_t = acs.T
    chunk_dec = jnp.exp(acs[q_len - 1:q_len, :])
    expand = ex_ref[...]
    dt_x = _dot_mask_rhs(dt, expand)
    acs_x = _dot_mask_rhs(acs, expand)
    xd = xs * dt_x
    xd_b = xd.astype(BF16)
    xdd = xd * jnp.exp(acs_x[q_len - 1:q_len, :] - acs_x)
    low_half = lax.broadcasted_iota(jnp.int32, (q_len, LANES), 1) < C_HEAD_DIM

    y_pairs, y_offs = [], []
    tdims = (((1,), (1,)), ((), ()))
    for g in range(C_GROUPS):
        b_g = bm[:, g * C_STATE:(g + 1) * C_STATE]
        c_g = cm[:, g * C_STATE:(g + 1) * C_STATE]
        cb = lax.dot_general(c_g, b_g, tdims, preferred_element_type=F32)
        h0 = g * C_HPG
        grp = slice(h0 * C_HEAD_DIM, (h0 + C_HPG) * C_HEAD_DIM)
        y_offs.append(lax.dot_general(c_g, hs_ref[grp, :].astype(BF16), tdims, preferred_element_type=F32))
        for h in range(h0, h0 + C_HPG, 2):
            xp = xd_b[:, h * C_HEAD_DIM:(h + 2) * C_HEAD_DIM]
            halves = []
            for hh in (h, h + 1):
                seg = jnp.exp(jnp.where(tri, acs[:, hh:hh + 1] - acs_t[hh:hh + 1, :], -jnp.inf))
                halves.append(jnp.dot((cb * seg).astype(BF16), xp, preferred_element_type=F32))
            y_pairs.append(jnp.where(low_half, halves[0], halves[1]))
        upd = jnp.dot(xdd[:, grp].T.astype(BF16), b_g, preferred_element_type=F32)
        for hh in range(C_HPG):
            h = h0 + hh
            ps = slice(h * C_HEAD_DIM, (h + 1) * C_HEAD_DIM)
            us = slice(hh * C_HEAD_DIM, (hh + 1) * C_HEAD_DIM)
            hs_ref[ps, :] = hs_ref[ps, :] * chunk_dec[0:1, h:h + 1] + upd[us, :]
    y = (jnp.concatenate(y_pairs, axis=1) + jnp.concatenate(y_offs, axis=1) * jnp.exp(acs_x)
         + dsk_ref[...] * xs)
    y = y * jax.nn.silu(z_ref[0].astype(F32))
    o_ref[0, :, :C_INNER] = _rms(y, sn_ref[...]).astype(o_ref.dtype)

    same = (row // D_CHUNK) == (col // D_CHUNK)
    tri2 = tri & same
    la = jnp.dot(sg_ref[0].astype(BF16), gkw_ref[...], preferred_element_type=F32) + gkb_ref[...]
    la = jax.nn.log_sigmoid(la) / D_GATE_NORM
    gcs = _dot_mask_lhs(tri2.astype(BF16), la)
    first = lax.broadcasted_iota(jnp.int32, (q_len, 1), 0) < D_CHUNK
    r_all = r_ref[0]
    for h in range(D_HEADS):
        ks = slice(h * D_HK, (h + 1) * D_HK)
        vs = slice(h * D_HV, (h + 1) * D_HV)
        g_h = gcs[:, ks]
        g_end0 = g_h[D_CHUNK - 1:D_CHUNK, :]
        g_end1 = g_h[q_len - 1:q_len, :]
        q_h = q_ref[0, :, ks].astype(F32) * (D_HK ** -0.5)
        k_h = k_ref[0, :, ks].astype(F32)
        v_h = v_ref[0, :, vs]
        q_dec = (q_h * jnp.exp(g_h)).astype(BF16)
        k_inv = (k_h * jnp.exp(-g_h)).astype(BF16)
        k_end = k_h * jnp.exp(jnp.where(first, g_end0, g_end1) - g_h)
        ke0 = jnp.where(first, k_end, 0.0).astype(BF16)
        ke1 = jnp.where(first, 0.0, k_end).astype(BF16)
        attn = lax.dot_general(q_dec, k_inv, (((1,), (1,)), ((), ())), preferred_element_type=F32)
        attn = jnp.where(tri2, attn, 0.0).astype(BF16)
        o = jnp.dot(attn, v_h, preferred_element_type=F32)
        v_t = v_h.astype(F32).T.astype(BF16)
        st_rows = slice(h * D_HV, (h + 1) * D_HV)
        s0 = gs_ref[st_rows, :]
        s1 = s0 * jnp.exp(g_end0) + jnp.dot(v_t, ke0, preferred_element_type=F32)
        s2 = s1 * jnp.exp(g_end1) + jnp.dot(v_t, ke1, preferred_element_type=F32)
        gs_ref[st_rows, :] = s2
        tdims = (((1,), (1,)), ((), ()))
        o0 = lax.dot_general(q_dec, s0.astype(BF16), tdims, preferred_element_type=F32)
        o1 = lax.dot_general(q_dec, s1.astype(BF16), tdims, preferred_element_type=F32)
        o = o + jnp.where(first, o0, o1)
        o = _rms(o, gn_ref[...]) * jax.nn.silu(r_all[:, vs].astype(F32))
        o_ref[0, :, C_INNER + h * D_HV:C_INNER + (h + 1) * D_HV] = o.astype(o_ref.dtype)


def _ssd_gla(z, q, k, v, r, xbc, side_dt, side_g, g_lane, p):
    b, s, _ = z.shape
    q_len = C_CHUNK
    halo = 16
    nb = SSD_BATCHES if b % SSD_BATCHES == 0 else 1
    chunk = lambda width: pl.BlockSpec((nb, q_len, width), lambda i, n: (i, n, 0))
    full = lambda shape: pl.BlockSpec(shape, lambda i, n: (0,) * len(shape))
    pad_lanes = lambda vec: jnp.zeros((1, LANES), F32).at[0, :vec.shape[0]].set(vec.astype(F32))
    gkw = jnp.zeros((LANES, D_KEY), F32).at[g_lane:g_lane + D_GATE_RANK].set(p["gk_w"]).astype(BF16)
    expand = jnp.asarray(np.arange(C_INNER)[None, :] // C_HEAD_DIM == np.arange(LANES)[:, None], BF16)
    return pl.pallas_call(
        _ssd_gla_kernel,
        grid=(b // nb, s // q_len),
        in_specs=[
            chunk(C_INNER), chunk(D_KEY), chunk(D_KEY), chunk(D_VAL), chunk(D_VAL), chunk(C_CONV_DIM),
            pl.BlockSpec((nb, halo, C_CONV_DIM), lambda i, n: (i, jnp.maximum(n * (q_len // halo) - 1, 0), 0)),
            chunk(LANES), chunk(LANES),
            full((C_CONV, C_CONV_DIM)), full((1, C_CONV_DIM)),
            full((1, LANES)), full((1, LANES)), full((1, C_INNER)), full((LANES, C_INNER)),
            full((1, C_INNER)), full((LANES, D_KEY)), full((1, D_KEY)), full((1, D_HV)),
        ],
        out_specs=pl.BlockSpec((nb, q_len, C_INNER + D_VAL), lambda i, n: (i, n, 0)),
        out_shape=jax.ShapeDtypeStruct((b, s, C_INNER + D_VAL), BF16),
        scratch_shapes=[pltpu.VMEM((nb, C_INNER, C_STATE), F32),
                        pltpu.VMEM((nb, D_VAL, D_HK), F32)],
        compiler_params=_cparams(("parallel", "arbitrary")),
        name="ssd_gla",
    )(z, q, k, v, r, xbc, xbc, side_dt, side_g,
      p["conv_w"].astype(F32), p["conv_b"].reshape(1, -1).astype(F32),
      pad_lanes(p["dt_bias"]), pad_lanes(p["a_log"]),
      jnp.repeat(p["d_skip"].astype(F32), C_HEAD_DIM).reshape(1, C_INNER), expand,
      p["ssd_norm"].reshape(1, -1).astype(F32), gkw, p["gk_b"].reshape(1, -1).astype(F32),
      p["gla_norm"].reshape(1, -1).astype(F32))


def _store_rows_tiled(ref, val):
    m, d = val.shape
    nc = d // LANES
    for c in range(nc):
        ref[pl.ds(c, m, stride=nc), :] = val[:, c * LANES:(c + 1) * LANES]


def _load_rows_tiled(ref, m, dtype=None):
    nc = ref.shape[0] // m
    parts = [ref[pl.ds(c, m, stride=nc), :] for c in range(nc)]
    if dtype is not None:
        parts = [p.astype(dtype) for p in parts]
    return jnp.concatenate(parts, axis=1)


def _pack_pairs(x):
    n = x.shape[1] // 2
    u = pltpu.bitcast(x.astype(BF16).astype(F32), jnp.uint32)
    return (u[:, :n] >> 16) | (u[:, n:] & jnp.uint32(0xFFFF0000))


def _unpack_pairs(u):
    lo = pltpu.bitcast(u << 16, F32).astype(BF16)
    hi = pltpu.bitcast(u & jnp.uint32(0xFFFF0000), F32).astype(BF16)
    return jnp.concatenate([lo, hi], axis=1)


RT_GATE, RT_EXPERT, RT_RANK = 0, 2, 4


def _route_block(lg, carry, earlier):
    m = lg.shape[0]
    lane = lax.broadcasted_iota(jnp.int32, (m, LANES), 1)
    lane_f = lane.astype(F32)
    none = float(LANES)
    neg = -jnp.inf
    first_max = lambda v, vmax: jnp.min(jnp.where(v == vmax, lane_f, none), axis=-1, keepdims=True)
    gl = jnp.where(lane < MOE_GROUPS, lg, neg)
    gmax = jnp.max(gl, axis=-1, keepdims=True)
    g_w = 1.0 / jnp.sum(jnp.exp(gl - gmax), axis=-1, keepdims=True)
    lo = MOE_GROUPS + first_max(gl, gmax) * MOE_EPG
    el = jnp.where((lane_f >= lo) & (lane_f < lo + MOE_EPG), lg, neg)
    emax = jnp.max(el, axis=-1, keepdims=True)
    esum = jnp.sum(jnp.exp(el - emax), axis=-1, keepdims=True)
    l0 = first_max(el, emax)
    el2 = jnp.where(lane_f == l0, neg, el)
    emax2 = jnp.max(el2, axis=-1, keepdims=True)
    l1 = first_max(el2, emax2)
    p0 = 1.0 / esum
    p1 = jnp.exp(emax2 - emax) / esum
    w0 = g_w * (p0 / (p0 + p1))
    w1 = g_w * (p1 / (p0 + p1))
    oh0 = lane_f == l0
    oh1 = lane_f == l1
    oh = (oh0 | oh1).astype(BF16)
    cum = jnp.dot(earlier, oh, preferred_element_type=F32) + carry
    rank0 = jnp.sum(jnp.where(oh0, cum, 0.0), axis=-1, keepdims=True)
    rank1 = jnp.sum(jnp.where(oh1, cum, 0.0), axis=-1, keepdims=True)
    carry = carry + jnp.sum(oh.astype(F32), axis=0, keepdims=True)
    rec = jnp.zeros((m, LANES), F32)
    for pos, val in ((RT_GATE, w0), (RT_GATE + 1, w1), (RT_EXPERT, l0 - MOE_GROUPS),
                     (RT_EXPERT + 1, l1 - MOE_GROUPS), (RT_RANK, rank0), (RT_RANK + 1, rank1)):
        rec = jnp.where(lane == pos, val, rec)
    return rec, carry


def _out_proj_kernel(n_parts, *refs):
    a_refs = refs[:n_parts]
    w_refs = refs[n_parts:2 * n_parts]
    h_ref, g_ref, wr_ref, br_ref, tri_ref, ho_ref, xt_ref, rt_ref, cnt_ref, carry_ref = refs[2 * n_parts:]

    @pl.when(pl.program_id(0) == 0)
    def _():
        carry_ref[...] = jnp.zeros_like(carry_ref)

    acc = h_ref[...]
    for a_ref, w_ref in zip(a_refs, w_refs):
        acc = acc + jnp.dot(a_ref[...], w_ref[...], preferred_element_type=F32)
    ho_ref[...] = acc
    xn = _rms(acc, g_ref[...])
    xt_ref[...] = xn.astype(xt_ref.dtype)
    x_hi, x_mid, _ = _split3(xn)
    wr = wr_ref[...]
    lg2 = jnp.dot(x_hi, wr, preferred_element_type=F32)
    lg = (lg2[:, :LANES] + lg2[:, LANES:] + jnp.dot(x_mid, wr[:, :LANES], preferred_element_type=F32)
          + br_ref[...])
    rec, carry = _route_block(lg, carry_ref[...], tri_ref[...])
    rt_ref[...] = rec
    carry_ref[...] = carry
    cnt_ref[0] = carry


def _out_proj(parts, w_parts, h, g, w_route, b_route, *, tm=512):
    t, d = h.shape
    tm = min(tm, t)
    nc = d // LANES
    row = lambda i: (i, 0)
    const = lambda i: (0, 0)
    in_specs = [pl.BlockSpec((tm, a.shape[1]), row) for a in parts]
    in_specs += [pl.BlockSpec(w.shape, const) for w in w_parts]
    in_specs += [pl.BlockSpec((tm, d), row), pl.BlockSpec((1, d), const),
                 pl.BlockSpec((d, 2 * LANES), const), pl.BlockSpec((1, LANES), const),
                 pl.BlockSpec((tm, tm), const)]
    earlier = jnp.asarray(np.tril(np.ones((tm, tm), np.float32), -1), BF16)
    return pl.pallas_call(
        functools.partial(_out_proj_kernel, len(parts)),
        grid=(t // tm,),
        in_specs=in_specs,
        out_specs=[pl.BlockSpec((tm, d), row), pl.BlockSpec((tm, d), row),
                   pl.BlockSpec((tm, LANES), row), pl.BlockSpec((1, 1, LANES), lambda i: (i, 0, 0))],
        out_shape=[jax.ShapeDtypeStruct((t, d), F32), jax.ShapeDtypeStruct((t, d), BF16),
                   jax.ShapeDtypeStruct((t, LANES), F32), jax.ShapeDtypeStruct((t // tm, 1, LANES), F32)],
        scratch_shapes=[pltpu.VMEM((1, LANES), F32)],
        compiler_params=_cparams(("arbitrary",)),
        name="out_proj",
    )(*parts, *w_parts, h, g.reshape(1, d), w_route, b_route, earlier)


LONG_RUN = 128


def _piece_sizes(max_rows, lo=1, hi=None):
    sizes = [1 << b for b in reversed(range(max_rows.bit_length()))]
    return [n for n in sizes if n >= lo and (hi is None or n < hi)]


def _for_each_piece(run, sizes, body):
    off = run // (2 * sizes[0]) * (2 * sizes[0])
    for n in sizes:
        hit = (run & n) != 0
        pl.when(hit)(functools.partial(body, off, n))
        off = off + jnp.where(hit, n, 0)


def _for_each_run(len_ref, tile, long_ref, max_rows, make_body):
    def sweep(sizes):
        def per_expert(e, first):
            run = len_ref[tile * MOE_EXPERTS + e]
            _for_each_piece(run, sizes, make_body(e, first))
            return first + run

        lax.fori_loop(0, MOE_EXPERTS, per_expert, jnp.int32(0))

    sweep(_piece_sizes(max_rows, hi=LONG_RUN))
    pl.when(long_ref[tile] != 0)(functools.partial(sweep, _piece_sizes(max_rows, lo=LONG_RUN)))


def _zero_fill_rows(rows_ref, z_ref, zsem, lo_ref, hi_ref, n_rows, nc):
    zb = z_ref.shape[0] // nc
    z_ref[...] = jnp.zeros_like(z_ref)
    assert MOE_TILE // 2 <= zb and MOE_TILE % zb == 0

    def piece(row0, n):
        return pltpu.make_async_copy(z_ref.at[pl.ds(0, n * nc)], rows_ref.at[pl.ds(row0 * nc, n * nc)], zsem)

    def sweep(issue):
        def per_expert(e, carry):
            lo = lo_ref[e]

            def one(off, n):
                piece(lo + off, n).start() if issue else piece(lo + off, n).wait()

            _for_each_piece(hi_ref[e] - lo, _piece_sizes(MOE_TILE // 2), one)
            return carry

        def per_block(i, carry):
            piece(i * zb, zb).start() if issue else piece(i * zb, zb).wait()
            return carry

        lax.fori_loop(0, MOE_EXPERTS, per_expert, 0)
        lax.fori_loop(hi_ref[MOE_EXPERTS - 1] // zb, n_rows // zb, per_block, 0)

    sweep(True)
    sweep(False)


def _tile_positions(rt, pos_base):
    lane_f = lax.broadcasted_iota(jnp.int32, rt.shape, 1).astype(F32)
    out = []
    for k in range(MOE_TOPK):
        e_lane = rt[:, RT_EXPERT + k:RT_EXPERT + k + 1] + MOE_GROUPS
        out.append(rt[:, RT_RANK + k:RT_RANK + k + 1]
                   + jnp.sum(jnp.where(lane_f == e_lane, pos_base, 0.0), axis=-1, keepdims=True))
    return out


def _moe_dispatch_kernel(dst_ref, len_ref, long_ref, lo_ref, hi_ref, x_ref, rt_ref, pb_ref, rows_ref,
                         sbuf, z_ref, sem, zsem, *, tm, nc, n_steps, n_rows):
    i = pl.program_id(0)
    rows = MOE_TOPK * tm
    slot = lax.rem(i, 2)

    def wait_slot(s):
        pltpu.make_async_copy(sbuf.at[s], rows_ref.at[pl.ds(0, rows * nc)], sem.at[s]).wait()

    @pl.when(i == 0)
    def _():
        _zero_fill_rows(rows_ref, z_ref, zsem, lo_ref, hi_ref, n_rows, nc)

    @pl.when(i >= 2)
    def _():
        wait_slot(slot)

    pos = _tile_positions(rt_ref[...], pb_ref[0])
    lane = lax.broadcasted_iota(jnp.int32, (tm, LANES), 1)
    pos_t = jnp.where(lane == 0, pos[0], jnp.where(lane == 1, pos[1], 0.0)).T
    p_iota = lax.broadcasted_iota(jnp.int32, (rows, tm), 0).astype(F32)
    place = ((p_iota == pos_t[0:1, :]) | (p_iota == pos_t[1:2, :])).astype(BF16)
    _store_rows_tiled(sbuf.at[slot], _pack_pairs(jnp.dot(place, x_ref[...], preferred_element_type=F32)))

    def sender(e, src):
        dst = dst_ref[i * MOE_EXPERTS + e]

        def send(off, n):
            pltpu.make_async_copy(sbuf.at[slot, pl.ds((src + off) * nc, n * nc)],
                                  rows_ref.at[pl.ds((dst + off) * nc, n * nc)], sem.at[slot]
                                  ).start(priority=n.bit_length() % 2)

        return send

    _for_each_run(len_ref, i, long_ref, rows, sender)

    @pl.when(i == n_steps - 1)
    def _():
        wait_slot(slot)
        if n_steps > 1:
            wait_slot(1 - slot)


def _moe_dispatch(x, route, pos_base, seg_dst, seg_len, seg_long, pad_lo, pad_hi, n_rows, *, tm):
    t, d = x.shape
    nc = d // (2 * LANES)
    n_steps = t // tm
    row = lambda i, *_: (i, 0)
    return pl.pallas_call(
        functools.partial(_moe_dispatch_kernel, tm=tm, nc=nc, n_steps=n_steps, n_rows=n_rows),
        grid_spec=pltpu.PrefetchScalarGridSpec(
            num_scalar_prefetch=5, grid=(n_steps,),
            in_specs=[pl.BlockSpec((tm, d), row), pl.BlockSpec((tm, LANES), row),
                      pl.BlockSpec((1, 1, LANES), lambda i, *_: (i, 0, 0))],
            out_specs=pl.BlockSpec(memory_space=pl.ANY),
            scratch_shapes=[pltpu.VMEM((2, MOE_TOPK * tm * nc, LANES), jnp.uint32),
                            pltpu.VMEM((MOE_TILE // 2 * nc, LANES), jnp.uint32),
                            pltpu.SemaphoreType.DMA((2,)), pltpu.SemaphoreType.DMA(())]),
        out_shape=jax.ShapeDtypeStruct((n_rows * nc, LANES), jnp.uint32),
        compiler_params=pltpu.CompilerParams(dimension_semantics=("arbitrary",), has_side_effects=True,
                                             vmem_limit_bytes=VMEM_LIMIT),
        name="moe_dispatch",
    )(seg_dst, seg_len, seg_long, pad_lo, pad_hi, x, route, pos_base)


def _moe_kernel(te_ref, nu_ref, first_ref, slot_ref, next_ref, tv_ref, x_ref, wg_hbm, wu_hbm, wd_hbm, y_ref,
                wg_buf, wu_buf, wd_buf, sem, *, layer):
    i = pl.program_id(0)
    live = i < nu_ref[0]
    pairs = ((wg_hbm, wg_buf), (wu_hbm, wu_buf), (wd_hbm, wd_buf))

    def weights(e, s):
        return [pltpu.make_async_copy(hbm.at[layer, e], buf.at[s], sem.at[s, k])
                for k, (hbm, buf) in enumerate(pairs)]

    @pl.when(i == 0)
    def _():
        for cp in weights(te_ref[0], 0):
            cp.start()

    @pl.when(live & (first_ref[i] != 0))
    def _():
        s = slot_ref[i]
        for cp in weights(te_ref[i], s):
            cp.wait()

        @pl.when(next_ref[i] >= 0)
        def _():
            for cp in weights(next_ref[i], 1 - s):
                cp.start()

    @pl.when(i >= nu_ref[0])
    def _():
        y_ref[...] = jnp.zeros_like(y_ref)

    half = MOE_TILE // 2
    nc = x_ref.shape[0] // MOE_TILE

    def mlp(rows):
        s = slot_ref[i]
        x = _unpack_pairs(_load_rows_tiled(x_ref.at[pl.ds(0, rows * nc)], rows))
        gate = jnp.dot(x, wg_buf[s].astype(BF16), preferred_element_type=F32)
        up = jnp.dot(x, wu_buf[s].astype(BF16), preferred_element_type=F32)
        act = (jax.nn.silu(gate) * up).astype(BF16)
        y = jnp.dot(act, wd_buf[s].astype(BF16), preferred_element_type=F32)
        _store_rows_tiled(y_ref.at[pl.ds(0, rows * nc)], _pack_pairs(y))
        if rows < MOE_TILE:
            y_ref[pl.ds(rows * nc, (MOE_TILE - rows) * nc), :] = jnp.zeros(((MOE_TILE - rows) * nc, LANES), y_ref.dtype)

    pl.when(live & (tv_ref[i] > half))(functools.partial(mlp, MOE_TILE))
    pl.when(live & (tv_ref[i] <= half))(functools.partial(mlp, half))


def _moe_experts(x_rows, tile_expert, tile_valid, n_used, w_gate, w_up, w_down, layer):
    d, ff = w_gate.shape[-2:]
    nc = d // (2 * LANES)
    n_tiles = x_rows.shape[0] // (MOE_TILE * nc)
    tiles = jnp.arange(n_tiles, dtype=jnp.int32)
    used = tiles < n_used[0]
    te = jnp.where(used, tile_expert, MOE_EXPERTS)
    first = (used & (te != jnp.concatenate([jnp.full((1,), -1, jnp.int32), te[:-1]]))).astype(jnp.int32)
    slot = ((jnp.cumsum(first) - 1) % 2).astype(jnp.int32)
    after = jnp.searchsorted(te, te, side="right")
    nxt = jnp.where(after < n_used[0], te[jnp.minimum(after, n_tiles - 1)], -1).astype(jnp.int32)
    live = lambda i, nu: jnp.minimum(i, nu[0] - 1)
    any_spec = pl.BlockSpec(memory_space=pl.ANY)
    return pl.pallas_call(
        functools.partial(_moe_kernel, layer=layer),
        grid_spec=pltpu.PrefetchScalarGridSpec(
            num_scalar_prefetch=6, grid=(n_tiles,),
            in_specs=[pl.BlockSpec((MOE_TILE * nc, LANES), lambda i, te, nu, *_: (live(i, nu), 0)),
                      any_spec, any_spec, any_spec],
            out_specs=pl.BlockSpec((MOE_TILE * nc, LANES), lambda i, *_: (i, 0)),
            scratch_shapes=[pltpu.VMEM((2, d, ff), F32), pltpu.VMEM((2, d, ff), F32), pltpu.VMEM((2, ff, d), F32),
                            pltpu.SemaphoreType.DMA((2, 3))]),
        out_shape=jax.ShapeDtypeStruct(x_rows.shape, x_rows.dtype),
        compiler_params=_cparams(("arbitrary",)),
        name="moe_experts",
    )(tile_expert, n_used, first, slot, nxt, tile_valid, x_rows, w_gate, w_up, w_down)


def _moe_combine_kernel(src_ref, len_ref, long_ref, h_ref, rt_ref, pb_ref, g_ref, y_hbm, o_ref, ybuf, sem,
                        *, tm, nc, final):
    i = pl.program_id(0)
    n_steps = pl.num_programs(0)
    rows = MOE_TOPK * tm

    def fetch(tile, slot):
        def receiver(e, dst):
            src = src_ref[tile * MOE_EXPERTS + e]

            def recv(off, n):
                pltpu.make_async_copy(y_hbm.at[pl.ds((src + off) * nc, n * nc)],
                                      ybuf.at[slot, pl.ds((dst + off) * nc, n * nc)], sem.at[slot]
                                      ).start(priority=n.bit_length() % 2)

            return recv

        _for_each_run(len_ref, tile, long_ref, rows, receiver)

    slot = lax.rem(i, 2)

    @pl.when(i == 0)
    def _():
        fetch(0, 0)

    @pl.when(i + 1 < n_steps)
    def _():
        fetch(i + 1, 1 - slot)

    pltpu.make_async_copy(y_hbm.at[pl.ds(0, rows * nc)], ybuf.at[slot], sem.at[slot]).wait()
    y = _unpack_pairs(_load_rows_tiled(ybuf.at[slot], rows))
    rt = rt_ref[...]
    pos_f = lax.broadcasted_iota(jnp.int32, (tm, rows), 1).astype(F32)
    pick = jnp.zeros((tm, rows), F32)
    for k, pos in enumerate(_tile_positions(rt, pb_ref[0])):
        pick = jnp.where(pos_f == pos, rt[:, RT_GATE + k:RT_GATE + k + 1], pick)
    out = h_ref[...] + jnp.dot(pick.astype(BF16), y, preferred_element_type=F32)
    o_ref[...] = _rms(out, g_ref[...]) if final else out


def _moe_combine_rows(h, y_rows, route, seg_src, seg_len, seg_long, pos_base, g, *, tm, final):
    t, d = h.shape
    nc = d // (2 * LANES)
    row = lambda i, *_: (i, 0)
    return pl.pallas_call(
        functools.partial(_moe_combine_kernel, tm=tm, nc=nc, final=final),
        grid_spec=pltpu.PrefetchScalarGridSpec(
            num_scalar_prefetch=3, grid=(t // tm,),
            in_specs=[pl.BlockSpec((tm, d), row), pl.BlockSpec((tm, LANES), row),
                      pl.BlockSpec((1, 1, LANES), lambda i, *_: (i, 0, 0)),
                      pl.BlockSpec((1, d), lambda i, *_: (0, 0)),
                      pl.BlockSpec(memory_space=pl.ANY)],
            out_specs=pl.BlockSpec((tm, d), row),
            scratch_shapes=[pltpu.VMEM((2, MOE_TOPK * tm * nc, LANES), y_rows.dtype),
                            pltpu.SemaphoreType.DMA((2,))]),
        out_shape=jax.ShapeDtypeStruct((t, d), F32),
        compiler_params=_cparams(("arbitrary",)),
        name="moe_combine",
    )(seg_src, seg_len, seg_long, h, route, pos_base, g.reshape(1, d), y_rows)


def _moe(h, x, route, tile_counts, w_gate, w_up, w_down, layer, g, *, final):
    t = route.shape[0]
    n_tt = tile_counts.shape[0]
    tm = t // n_tt
    after = tile_counts[:, 0, MOE_GROUPS:MOE_GROUPS + MOE_EXPERTS].astype(jnp.int32)
    before = jnp.concatenate([jnp.zeros((1, MOE_EXPERTS), jnp.int32), after[:-1]], axis=0)
    cnt = after[-1]
    padded = (cnt + MOE_TILE - 1) // MOE_TILE * MOE_TILE
    pad_ends = jnp.cumsum(padded)
    starts = (pad_ends - padded).astype(jnp.int32)
    n_tiles = (t * MOE_TOPK + MOE_EXPERTS * (MOE_TILE - 1)) // MOE_TILE
    tile_start = jnp.arange(n_tiles, dtype=jnp.int32) * MOE_TILE
    tile_expert = jnp.minimum(jnp.sum(tile_start[:, None] >= pad_ends[None, :], axis=1),
                              MOE_EXPERTS - 1).astype(jnp.int32)
    n_used = (pad_ends[-1] // MOE_TILE).astype(jnp.int32).reshape(1)
    tile_valid = jnp.clip(cnt[tile_expert] - (tile_start - starts[tile_expert]), 0, MOE_TILE).astype(jnp.int32)
    n_rows = n_tiles * MOE_TILE
    seg_len = (after - before).reshape(-1)
    seg_off = jnp.cumsum(after - before, axis=1) - (after - before)
    seg_row = (starts[None, :] + before).reshape(-1)
    pos_base = jnp.zeros((n_tt, 1, LANES), F32).at[:, 0, MOE_GROUPS:MOE_GROUPS + MOE_EXPERTS].set(
        (seg_off - before).astype(F32))
    seg_long = jnp.any(after - before >= LONG_RUN, axis=1).astype(jnp.int32)
    x_rows = _moe_dispatch(x, route, pos_base, seg_row, seg_len, seg_long, starts + cnt,
                           pad_ends.astype(jnp.int32), n_rows, tm=tm)
    y_rows = _moe_experts(x_rows, tile_expert, tile_valid, n_used, w_gate, w_up, w_down, layer)
    return _moe_combine_rows(h, y_rows, route, seg_row, seg_len, seg_long, pos_base, g, tm=tm, final=final)


def _router_weights(w_group, b_group, w_router, b_router):
    d = w_group.shape[0]
    w = jnp.zeros((d, LANES), F32)
    w = w.at[:, :MOE_GROUPS].set(w_group).at[:, MOE_GROUPS:MOE_GROUPS + MOE_EXPERTS].set(w_router)
    b = jnp.zeros((1, LANES), F32)
    b = b.at[0, :MOE_GROUPS].set(b_group).at[0, MOE_GROUPS:MOE_GROUPS + MOE_EXPERTS].set(b_router)
    w_hi = w.astype(BF16)
    w_mid = (w - w_hi.astype(F32)).astype(BF16)
    return jnp.concatenate([w_hi, w_mid], axis=1), b


def kernel(x, norm_mix, norm_moe, norm_final, even_w_in, even_sinks, even_forget_bias, even_w_out,
           odd_w_in, odd_conv_w, odd_conv_b, odd_dt_bias, odd_a_log, odd_d_skip, odd_ssd_norm,
           odd_gk_w, odd_gk_b, odd_gla_norm, odd_w_out, moe_w_group, moe_b_group, moe_w_router,
           moe_b_router, moe_w_gate, moe_w_up, moe_w_down):
    b, s, d = x.shape
    t = b * s
    depth = norm_mix.shape[0]
    h = x.reshape(t, d)
    for layer in range(depth):
        i = layer // 2
        if layer % 2 == 0:
            w = even_w_in.astype(BF16)[i]
            n_ab = (A_Q_HEADS + 2 * A_KV_HEADS + 3 * B_HEADS) * HEAD_DIM
            w_main = w[:, :n_ab]
            w_aux = jnp.pad(w[:, n_ab:], ((0, 0), (0, LANES - B_HEADS)))
            n_a, kv_a, n_b = (A_Q_HEADS + 2 * A_KV_HEADS) * HEAD_DIM, A_KV_HEADS * HEAD_DIM, B_HEADS * HEAD_DIM
            swa_rows, fox_rows = SWA_QBLOCKS * A_WINDOW, 256
            proj, f_aux, va_t, vb_t = _norm_proj(
                h, norm_mix[layer], [w_main, w_aux], ((0, 0, n_ab, BF16), (1, 0, LANES, F32)), tm=1024,
                t_plan=((0, n_a - kv_a, kv_a, BF16, swa_rows), (0, n_ab - n_b, n_b, BF16, fox_rows)))
            proj = proj.reshape(b, s, -1)
            out_a = _swa(proj, va_t.reshape(b, s // swa_rows, kv_a, swa_rows), even_sinks[i])
            c, ct = _fox_gate(f_aux.reshape(b, s, LANES), even_forget_bias[i])
            out_b = _fox(proj, vb_t.reshape(b, s // fox_rows, n_b, fox_rows), c, ct, tk=fox_rows)
            n_ha = A_Q_HEADS * HEAD_DIM
            w_out = even_w_out[i].astype(BF16)
            parts = [out_a.reshape(t, -1), out_b.reshape(t, -1)]
            w_parts = [w_out[:n_ha], w_out[n_ha:]]
        else:
            w = odd_w_in.astype(BF16)[i]
            o_z, o_xbc = 0, C_INNER
            o_dt = o_xbc + C_CONV_DIM
            o_q = o_dt + C_HEADS
            o_k = o_q + D_KEY
            o_v = o_k + D_KEY
            o_g = o_v + D_VAL
            o_r = o_g + D_GATE_RANK
            n_b = w.shape[1] - o_dt
            w_a = w[:, :o_dt]
            w_b = jnp.pad(w[:, o_dt:], ((0, 0), (0, -n_b % LANES)))
            g_win = (o_g - o_dt) // LANES * LANES
            plan = ((0, o_z, C_INNER, BF16), (1, o_q - o_dt, D_KEY, BF16), (1, o_k - o_dt, D_KEY, BF16),
                    (1, o_v - o_dt, D_VAL, BF16), (1, o_r - o_dt, D_VAL, BF16), (0, o_xbc, C_CONV_DIM, BF16),
                    (1, 0, LANES, F32), (1, g_win, LANES, F32))
            outs = _norm_proj(h, norm_mix[layer], [w_a, w_b], plan, tm=512)
            params = dict(conv_w=odd_conv_w[i], conv_b=odd_conv_b[i], dt_bias=odd_dt_bias[i], a_log=odd_a_log[i],
                          d_skip=odd_d_skip[i], ssd_norm=odd_ssd_norm[i], gk_w=odd_gk_w[i], gk_b=odd_gk_b[i],
                          gla_norm=odd_gla_norm[i])
            mixed = _ssd_gla(*[o.reshape(b, s, -1) for o in outs], o_g - o_dt - g_win, params)
            parts = [mixed.reshape(t, -1)]
            w_parts = [odd_w_out[i].astype(BF16)]
        w_route, b_route = _router_weights(moe_w_group[layer], moe_b_group[layer],
                                           moe_w_router[layer], moe_b_router[layer])
        h, x_tiled, route, tile_counts = _out_proj(parts, w_parts, h, norm_moe[layer], w_route, b_route)
        h = _moe(h, x_tiled, route, tile_counts, moe_w_gate, moe_w_up, moe_w_down, layer, norm_final,
                 final=layer == depth - 1)
    out = h
    return out.reshape(b, s, d)
```
